```python
import math
import jax, jax.numpy as jnp
from jax import lax
import numpy as np

D_MODEL = 1024
BATCH = 8
SEQ = 16384
DEPTH = 1

D_MIX = D_MODEL
ATTN_WIDTH = D_MIX // 2
HEAD_DIM = 64
N_HEADS = ATTN_WIDTH // HEAD_DIM
WINDOWS = (128, 512, 2048)
DILATIONS = (1, 4, 16)
BLOCK = 128
PAD_UNIT = max(DILATIONS) * BLOCK
SSM_WIDTH = D_MIX - ATTN_WIDTH
SSM_GROUP = 16
SSM_GROUPS = SSM_WIDTH // SSM_GROUP
SSM_STATE = 64
D_FF = 4 * D_MODEL
PROJ_WIDTH = 3 * ATTN_WIDTH + SSM_WIDTH
EPS = 1e-6
NEG_INF = -1e30
DT_MIN, DT_MAX = 1e-3, 1e-1

kernel_name = "hymba_longnet_s5_hybrid"


def _rmsnorm(x, g):
    xf = x.astype(jnp.float32)
    y = xf * lax.rsqrt(jnp.mean(xf * xf, axis=-1, keepdims=True) + EPS)
    return (y * g.astype(jnp.float32)).astype(x.dtype)


def _dilated_window(q, k, v, dilation, steps):
    b, sp, h, e = q.shape
    n = sp // dilation
    nb = n // BLOCK

    def to_blocks(t):
        t = t.reshape(b, n, dilation, h, e).transpose(0, 2, 3, 1, 4)
        return t.reshape(b, dilation, h, nb, BLOCK, e)

    qb, kb, vb = to_blocks(q), to_blocks(k), to_blocks(v)

    def with_prev(t):
        prev = jnp.pad(t[:, :, :, :-1], ((0, 0), (0, 0), (0, 0), (1, 0), (0, 0), (0, 0)))
        return jnp.concatenate([prev, t], axis=4)

    kw, vw = with_prev(kb), with_prev(vb)
    s = jnp.einsum('brhnqe,brhnke->brhnqk', qb, kw) * (HEAD_DIM ** -0.5)
    qi = jnp.arange(BLOCK)[:, None] + BLOCK
    ki = jnp.arange(2 * BLOCK)[None, :]
    dist = qi - ki
    band = (dist >= 0) & (dist <= steps)
    valid = (jnp.arange(nb)[:, None, None] * BLOCK - BLOCK + ki[None]) >= 0
    mask = band[None] & valid
    s = jnp.where(mask, s, NEG_INF)
    m = jnp.max(s, axis=-1)
    p = jnp.exp(s - m[..., None])
    l = jnp.sum(p, axis=-1)
    acc = jnp.einsum('brhnqk,brhnke->brhnqe', p, vw)

    def to_seq(t):
        t = t.reshape((b, dilation, h, n) + t.shape[5:])
        t = jnp.moveaxis(t, 3, 1)
        return t.reshape((b, sp, h) + t.shape[4:])

    return to_seq(acc), to_seq(m), to_seq(l)


def _dilated_attention(q, k, v):
    b, s, h, e = q.shape
    pad = (-s) % PAD_UNIT
    qf, kf, vf = [jnp.pad(t.astype(jnp.float32), ((0, 0), (0, pad), (0, 0), (0, 0))) for t in (q, k, v)]
    accs, ms, ls = [], [], []
    for w, d in zip(WINDOWS, DILATIONS):
        acc, m, l = _dilated_window(qf, kf, vf, d, w // d)
        accs.append(acc)
        ms.append(m)
        ls.append(l)
    m_all = jnp.stack(ms)
    wts = jnp.exp(m_all - jnp.max(m_all, axis=0, keepdims=True))
    num = jnp.sum(jnp.stack(accs) * wts[..., None], axis=0)
    den = jnp.sum(jnp.stack(ls) * wts, axis=0)
    out = num / den[..., None]
    return out[:, :s].astype(v.dtype)


def _complex_linear_combine(e1, e2):
    a1r, a1i, b1r, b1i = e1
    a2r, a2i, b2r, b2i = e2
    return (a2r * a1r - a2i * a1i,
            a2r * a1i + a2i * a1r,
            a2r * b1r - a2i * b1i + b2r,
            a2r * b1i + a2i * b1r + b2i)


def _s5_mixer(u, a_re, a_im, log_dt, b_re, b_im, c_re, c_im, d_skip, glu_w, glu_b):
    b, s, _ = u.shape
    f32 = jnp.float32
    uf = u.astype(f32).reshape(b, s, SSM_GROUPS, SSM_GROUP)
    lr, li = a_re.astype(f32), a_im.astype(f32)
    dt = jnp.exp(log_dt.astype(f32))[:, None]
    mag = jnp.exp(lr * dt)
    ab_r, ab_i = mag * jnp.cos(li * dt), mag * jnp.sin(li * dt)
    den = lr * lr + li * li
    nr, ni = ab_r - 1.0, ab_i
    cr = (nr * lr + ni * li) / den
    ci = (ni * lr - nr * li) / den
    br, bi = b_re.astype(f32), b_im.astype(f32)
    bb_r = cr[..., None] * br - ci[..., None] * bi
    bb_i = cr[..., None] * bi + ci[..., None] * br
    bu_r = jnp.einsum('bsgc,gpc->bsgp', uf, bb_r)
    bu_i = jnp.einsum('bsgc,gpc->bsgp', uf, bb_i)
    a_r = jnp.broadcast_to(ab_r, bu_r.shape)
    a_i = jnp.broadcast_to(ab_i, bu_i.shape)
    _, _, xr, xi = lax.associative_scan(_complex_linear_combine, (a_r, a_i, bu_r, bu_i), axis=1)
    y = (jnp.einsum('bsgp,gcp->bsgc', xr, c_re.astype(f32))
         - jnp.einsum('bsgp,gcp->bsgc', xi, c_im.astype(f32))
         + d_skip.astype(f32) * uf)
    y = y.reshape(b, s, SSM_WIDTH)
    z = jax.nn.gelu(y)
    out = z * jax.nn.sigmoid(z @ glu_w.astype(f32) + glu_b.astype(f32))
    return out.astype(u.dtype)


def _hybrid_layer(x, norm1_g, w_in, q_norm_g, k_norm_g, ssm_a_re, ssm_a_im, ssm_log_dt,
                  ssm_b_re, ssm_b_im, ssm_c_re, ssm_c_im, ssm_d, glu_w, glu_b,
                  attn_out_norm_g, ssm_out_norm_g, w_out, norm2_g, w_mlp_up, w_mlp_down):
    b, s, _ = x.shape
    xn = _rmsnorm(x, norm1_g)
    proj = xn @ w_in
    q = proj[..., :ATTN_WIDTH].reshape(b, s, N_HEADS, HEAD_DIM)
    k = proj[..., ATTN_WIDTH:2 * ATTN_WIDTH].reshape(b, s, N_HEADS, HEAD_DIM)
    v = proj[..., 2 * ATTN_WIDTH:3 * ATTN_WIDTH].reshape(b, s, N_HEADS, HEAD_DIM)
    u = proj[..., 3 * ATTN_WIDTH:]
    q = _rmsnorm(q, q_norm_g)
    k = _rmsnorm(k, k_norm_g)
    attn = _dilated_attention(q, k, v).reshape(b, s, ATTN_WIDTH)
    ssm = _s5_mixer(u, ssm_a_re, ssm_a_im, ssm_log_dt, ssm_b_re, ssm_b_im,
                    ssm_c_re, ssm_c_im, ssm_d, glu_w, glu_b)
    mix = jnp.concatenate([_rmsnorm(attn, attn_out_norm_g), _rmsnorm(ssm, ssm_out_norm_g)], axis=-1)
    x = x + mix @ w_out
    hdn = jnp.square(jax.nn.relu(_rmsnorm(x, norm2_g) @ w_mlp_up))
    return x + hdn @ w_mlp_down


def _fwd_setup_inputs(seed: int = 0) -> dict:
    key = jax.random.key(seed)
    ks = jax.random.split(key, 20)
    L, G, P, C = DEPTH, SSM_GROUPS, SSM_STATE, SSM_GROUP
    nrm = lambda k, shape, scale: jax.random.normal(k, shape, jnp.float32) * scale
    x = nrm(ks[0], (BATCH, SEQ, D_MODEL), 1.0)
    norm1_g = 1.0 + nrm(ks[1], (L, D_MODEL), 0.02)
    w_in = nrm(ks[2], (L, D_MODEL, PROJ_WIDTH), D_MODEL ** -0.5)
    q_norm_g = 1.0 + nrm(ks[3], (L, HEAD_DIM), 0.02)
    k_norm_g = 1.0 + nrm(ks[4], (L, HEAD_DIM), 0.02)
    ssm_a_re = -0.5 + nrm(ks[5], (L, G, P), 0.01)
    ssm_a_im = math.pi * jnp.arange(P, dtype=jnp.float32)[None, None, :] + nrm(ks[6], (L, G, P), 0.01)
    ssm_log_dt = jax.random.uniform(ks[7], (L, G), jnp.float32, math.log(DT_MIN), math.log(DT_MAX))
    ssm_b_re = nrm(ks[8], (L, G, P, C), (2 * C) ** -0.5)
    ssm_b_im = nrm(ks[9], (L, G, P, C), (2 * C) ** -0.5)
    ssm_c_re = nrm(ks[10], (L, G, C, P), (2 * P) ** -0.5)
    ssm_c_im = nrm(ks[11], (L, G, C, P), (2 * P) ** -0.5)
    ssm_d = nrm(ks[12], (L, G, C), 1.0)
    glu_w = nrm(ks[13], (L, SSM_WIDTH, SSM_WIDTH), SSM_WIDTH ** -0.5)
    glu_b = nrm(ks[14], (L, SSM_WIDTH), 0.01)
    attn_out_norm_g = 1.0 + nrm(ks[15], (L, ATTN_WIDTH), 0.02)
    ssm_out_norm_g = 1.0 + nrm(ks[16], (L, SSM_WIDTH), 0.02)
    w_out = nrm(ks[17], (L, D_MIX, D_MODEL), D_MIX ** -0.5)
    norm2_g = 1.0 + nrm(ks[18], (L, D_MODEL), 0.02)
    k_up, k_down = jax.random.split(ks[19])
    w_mlp_up = nrm(k_up, (L, D_MODEL, D_FF), D_MODEL ** -0.5)
    w_mlp_down = nrm(k_down, (L, D_FF, D_MODEL), D_FF ** -0.5)
    return {"x": x, "norm1_g": norm1_g, "w_in": w_in, "q_norm_g": q_norm_g, "k_norm_g": k_norm_g,
            "ssm_a_re": ssm_a_re, "ssm_a_im": ssm_a_im, "ssm_log_dt": ssm_log_dt,
            "ssm_b_re": ssm_b_re, "ssm_b_im": ssm_b_im, "ssm_c_re": ssm_c_re, "ssm_c_im": ssm_c_im,
            "ssm_d": ssm_d, "glu_w": glu_w, "glu_b": glu_b,
            "attn_out_norm_g": attn_out_norm_g, "ssm_out_norm_g": ssm_out_norm_g,
            "w_out": w_out, "norm2_g": norm2_g, "w_mlp_up": w_mlp_up, "w_mlp_down": w_mlp_down}


def _fwd_reference(x, norm1_g, w_in, q_norm_g, k_norm_g, ssm_a_re, ssm_a_im, ssm_log_dt,
              ssm_b_re, ssm_b_im, ssm_c_re, ssm_c_im, ssm_d, glu_w, glu_b,
              attn_out_norm_g, ssm_out_norm_g, w_out, norm2_g, w_mlp_up, w_mlp_down):
    h = x
    for l in range(DEPTH):
        h = _hybrid_layer(h, norm1_g[l], w_in[l], q_norm_g[l], k_norm_g[l], ssm_a_re[l], ssm_a_im[l],
                          ssm_log_dt[l], ssm_b_re[l], ssm_b_im[l], ssm_c_re[l], ssm_c_im[l], ssm_d[l],
                          glu_w[l], glu_b[l], attn_out_norm_g[l], ssm_out_norm_g[l], w_out[l],
                          norm2_g[l], w_mlp_up[l], w_mlp_down[l])
    return h


import jax as _jax
import jax.numpy as _jnp

TWIN_FORMAT = 'train_step'
FWD_PARAMS = ['x', 'norm1_g', 'w_in', 'q_norm_g', 'k_norm_g', 'ssm_a_re', 'ssm_a_im', 'ssm_log_dt', 'ssm_b_re', 'ssm_b_im', 'ssm_c_re', 'ssm_c_im', 'ssm_d', 'glu_w', 'glu_b', 'attn_out_norm_g', 'ssm_out_norm_g', 'w_out', 'norm2_g', 'w_mlp_up', 'w_mlp_down']
TWIN_WEIGHTS = ['norm1_g', 'w_in', 'q_norm_g', 'k_norm_g', 'ssm_a_re', 'ssm_a_im', 'ssm_log_dt', 'ssm_b_re', 'ssm_b_im', 'ssm_c_re', 'ssm_c_im', 'ssm_d', 'glu_w', 'glu_b', 'attn_out_norm_g', 'ssm_out_norm_g', 'w_out', 'norm2_g', 'w_mlp_up', 'w_mlp_down']
TWIN_DIFF_INPUT = 'x'
TWIN_INPUTS = ['x', 'norm1_g', 'w_in', 'q_norm_g', 'k_norm_g', 'ssm_a_re', 'ssm_a_im', 'ssm_log_dt', 'ssm_b_re', 'ssm_b_im', 'ssm_c_re', 'ssm_c_im', 'ssm_d', 'glu_w', 'glu_b', 'attn_out_norm_g', 'ssm_out_norm_g', 'w_out', 'norm2_g', 'w_mlp_up', 'w_mlp_down', 'loss_target', 'm_norm1_g', 'm_w_in', 'm_q_norm_g', 'm_k_norm_g', 'm_ssm_a_re', 'm_ssm_a_im', 'm_ssm_log_dt', 'm_ssm_b_re', 'm_ssm_b_im', 'm_ssm_c_re', 'm_ssm_c_im', 'm_ssm_d', 'm_glu_w', 'm_glu_b', 'm_attn_out_norm_g', 'm_ssm_out_norm_g', 'm_w_out', 'm_norm2_g', 'm_w_mlp_up', 'm_w_mlp_down', 'v_norm1_g', 'v_w_in', 'v_q_norm_g', 'v_k_norm_g', 'v_ssm_a_re', 'v_ssm_a_im', 'v_ssm_log_dt', 'v_ssm_b_re', 'v_ssm_b_im', 'v_ssm_c_re', 'v_ssm_c_im', 'v_ssm_d', 'v_glu_w', 'v_glu_b', 'v_attn_out_norm_g', 'v_ssm_out_norm_g', 'v_w_out', 'v_norm2_g', 'v_w_mlp_up', 'v_w_mlp_down']
TWIN_OUTPUTS = ['loss', 'grad_x', 'grad_norm1_g', 'grad_w_in', 'grad_q_norm_g', 'grad_k_norm_g', 'grad_ssm_a_re', 'grad_ssm_a_im', 'grad_ssm_log_dt', 'grad_ssm_b_re', 'grad_ssm_b_im', 'grad_ssm_c_re', 'grad_ssm_c_im', 'grad_ssm_d', 'grad_glu_w', 'grad_glu_b', 'grad_attn_out_norm_g', 'grad_ssm_out_norm_g', 'grad_w_out', 'grad_norm2_g', 'grad_w_mlp_up', 'grad_w_mlp_down', 'delta_norm1_g', 'delta_w_in', 'delta_q_norm_g', 'delta_k_norm_g', 'delta_ssm_a_re', 'delta_ssm_a_im', 'delta_ssm_log_dt', 'delta_ssm_b_re', 'delta_ssm_b_im', 'delta_ssm_c_re', 'delta_ssm_c_im', 'delta_ssm_d', 'delta_glu_w', 'delta_glu_b', 'delta_attn_out_norm_g', 'delta_ssm_out_norm_g', 'delta_w_out', 'delta_norm2_g', 'delta_w_mlp_up', 'delta_w_mlp_down', 'new_m_norm1_g', 'new_m_w_in', 'new_m_q_norm_g', 'new_m_k_norm_g', 'new_m_ssm_a_re', 'new_m_ssm_a_im', 'new_m_ssm_log_dt', 'new_m_ssm_b_re', 'new_m_ssm_b_im', 'new_m_ssm_c_re', 'new_m_ssm_c_im', 'new_m_ssm_d', 'new_m_glu_w', 'new_m_glu_b', 'new_m_attn_out_norm_g', 'new_m_ssm_out_norm_g', 'new_m_w_out', 'new_m_norm2_g', 'new_m_w_mlp_up', 'new_m_w_mlp_down', 'new_v_norm1_g', 'new_v_w_in', 'new_v_q_norm_g', 'new_v_k_norm_g', 'new_v_ssm_a_re', 'new_v_ssm_a_im', 'new_v_ssm_log_dt', 'new_v_ssm_b_re', 'new_v_ssm_b_im', 'new_v_ssm_c_re', 'new_v_ssm_c_im', 'new_v_ssm_d', 'new_v_glu_w', 'new_v_glu_b', 'new_v_attn_out_norm_g', 'new_v_ssm_out_norm_g', 'new_v_w_out', 'new_v_norm2_g', 'new_v_w_mlp_up', 'new_v_w_mlp_down']
TWIN_LEAF_KINDS = {'loss': 'loss', 'grad_x': 'grad_x', 'grad_norm1_g': 'grad_w', 'grad_w_in': 'grad_w', 'grad_q_norm_g': 'grad_w', 'grad_k_norm_g': 'grad_w', 'grad_ssm_a_re': 'grad_w', 'grad_ssm_a_im': 'grad_w', 'grad_ssm_log_dt': 'grad_w', 'grad_ssm_b_re': 'grad_w', 'grad_ssm_b_im': 'grad_w', 'grad_ssm_c_re': 'grad_w', 'grad_ssm_c_im': 'grad_w', 'grad_ssm_d': 'grad_w', 'grad_glu_w': 'grad_w', 'grad_glu_b': 'grad_w', 'grad_attn_out_norm_g': 'grad_w', 'grad_ssm_out_norm_g': 'grad_w', 'grad_w_out': 'grad_w', 'grad_norm2_g': 'grad_w', 'grad_w_mlp_up': 'grad_w', 'grad_w_mlp_down': 'grad_w', 'delta_norm1_g': 'delta_w', 'delta_w_in': 'delta_w', 'delta_q_norm_g': 'delta_w', 'delta_k_norm_g': 'delta_w', 'delta_ssm_a_re': 'delta_w', 'delta_ssm_a_im': 'delta_w', 'delta_ssm_log_dt': 'delta_w', 'delta_ssm_b_re': 'delta_w', 'delta_ssm_b_im': 'delta_w', 'delta_ssm_c_re': 'delta_w', 'delta_ssm_c_im': 'delta_w', 'delta_ssm_d': 'delta_w', 'delta_glu_w': 'delta_w', 'delta_glu_b': 'delta_w', 'delta_attn_out_norm_g': 'delta_w', 'delta_ssm_out_norm_g': 'delta_w', 'delta_w_out': 'delta_w', 'delta_norm2_g': 'delta_w', 'delta_w_mlp_up': 'delta_w', 'delta_w_mlp_down': 'delta_w', 'new_m_norm1_g': 'new_m', 'new_m_w_in': 'new_m', 'new_m_q_norm_g': 'new_m', 'new_m_k_norm_g': 'new_m', 'new_m_ssm_a_re': 'new_m', 'new_m_ssm_a_im': 'new_m', 'new_m_ssm_log_dt': 'new_m', 'new_m_ssm_b_re': 'new_m', 'new_m_ssm_b_im': 'new_m', 'new_m_ssm_c_re': 'new_m', 'new_m_ssm_c_im': 'new_m', 'new_m_ssm_d': 'new_m', 'new_m_glu_w': 'new_m', 'new_m_glu_b': 'new_m', 'new_m_attn_out_norm_g': 'new_m', 'new_m_ssm_out_norm_g': 'new_m', 'new_m_w_out': 'new_m', 'new_m_norm2_g': 'new_m', 'new_m_w_mlp_up': 'new_m', 'new_m_w_mlp_down': 'new_m', 'new_v_norm1_g': 'new_v', 'new_v_w_in': 'new_v', 'new_v_q_norm_g': 'new_v', 'new_v_k_norm_g': 'new_v', 'new_v_ssm_a_re': 'new_v', 'new_v_ssm_a_im': 'new_v', 'new_v_ssm_log_dt': 'new_v', 'new_v_ssm_b_re': 'new_v', 'new_v_ssm_b_im': 'new_v', 'new_v_ssm_c_re': 'new_v', 'new_v_ssm_c_im': 'new_v', 'new_v_ssm_d': 'new_v', 'new_v_glu_w': 'new_v', 'new_v_glu_b': 'new_v', 'new_v_attn_out_norm_g': 'new_v', 'new_v_ssm_out_norm_g': 'new_v', 'new_v_w_out': 'new_v', 'new_v_norm2_g': 'new_v', 'new_v_w_mlp_up': 'new_v', 'new_v_w_mlp_down': 'new_v'}


def _forward(args):
    return _fwd_reference(*[args[k] for k in FWD_PARAMS])


def _output_shape():
    def fwd():
        inp = _fwd_setup_inputs(0)
        return _fwd_reference(*[inp[k] for k in FWD_PARAMS])
    out = _jax.eval_shape(fwd)
    return out.shape, out.dtype

N_MICROBATCH = 1
ADAM_LR = 0.001
ADAM_B1 = 0.9
ADAM_B2 = 0.999
ADAM_EPS = 1e-08
ADAM_WD = 0.01
ADAM_STEP = 10
PER_EXAMPLE_BATCH_AXIS = {'x': 0, 'loss_target': 0}
SHARED_INPUTS = []
_WEIGHT_DTYPES = {'norm1_g': _jnp.float32, 'w_in': _jnp.float32, 'q_norm_g': _jnp.float32, 'k_norm_g': _jnp.float32, 'ssm_a_re': _jnp.float32, 'ssm_a_im': _jnp.float32, 'ssm_log_dt': _jnp.float32, 'ssm_b_re': _jnp.float32, 'ssm_b_im': _jnp.float32, 'ssm_c_re': _jnp.float32, 'ssm_c_im': _jnp.float32, 'ssm_d': _jnp.float32, 'glu_w': _jnp.float32, 'glu_b': _jnp.float32, 'attn_out_norm_g': _jnp.float32, 'ssm_out_norm_g': _jnp.float32, 'w_out': _jnp.float32, 'norm2_g': _jnp.float32, 'w_mlp_up': _jnp.float32, 'w_mlp_down': _jnp.float32}
MOMENT_SCALE = {'norm1_g': 4.873172e+00, 'w_in': 3.092047e+00, 'q_norm_g': 7.632303e+00, 'k_norm_g': 7.707596e+00, 'ssm_a_re': 8.418100e-02, 'ssm_a_im': 1.123326e-01, 'ssm_log_dt': 8.376044e+01, 'ssm_b_re': 7.765706e-02, 'ssm_b_im': 7.375249e-02, 'ssm_c_re': 1.423951e-01, 'ssm_c_im': 1.528595e-01, 'ssm_d': 6.489430e+01, 'glu_w': 8.383977e+00, 'glu_b': 2.643101e+01, 'attn_out_norm_g': 1.278973e+02, 'ssm_out_norm_g': 2.411436e+02, 'w_out': 3.843636e+01, 'norm2_g': 3.939635e+02, 'w_mlp_up': 1.459128e+01, 'w_mlp_down': 4.392844e+01}


def _to_microbatches(a, axis):
    t = _jnp.moveaxis(a, axis, 0)
    t = t.reshape((N_MICROBATCH, t.shape[0] // N_MICROBATCH) + t.shape[1:])
    return _jnp.moveaxis(t, 1, axis + 1)


def setup_inputs(seed: int = 0) -> dict:
    inp = _fwd_setup_inputs(seed)
    key = _jax.random.fold_in(_jax.random.key(seed), 7919)
    shape, _ = _output_shape()
    out = dict(inp)
    out["loss_target"] = _jax.random.normal(_jax.random.fold_in(key, 0), shape, _jnp.float32)
    for i, name in enumerate(TWIN_WEIGHTS):
        w = inp[name].astype(_jnp.float32)
        if MOMENT_SCALE is None:
            s = _jnp.sqrt(_jnp.mean(_jnp.square(w)) + 1e-30)
        else:
            s = MOMENT_SCALE[name]
        km, kv = _jax.random.split(_jax.random.fold_in(key, i + 1))
        out[name] = w
        out["m_" + name] = s * _jax.random.normal(km, w.shape, _jnp.float32)
        out["v_" + name] = (s * s) * _jax.random.uniform(kv, w.shape, _jnp.float32, 0.5, 1.5)
    if N_MICROBATCH > 1:
        for name, axis in PER_EXAMPLE_BATCH_AXIS.items():
            out[name] = _to_microbatches(out[name], axis)
    return {'x': out['x'], 'norm1_g': out['norm1_g'], 'w_in': out['w_in'], 'q_norm_g': out['q_norm_g'], 'k_norm_g': out['k_norm_g'], 'ssm_a_re': out['ssm_a_re'], 'ssm_a_im': out['ssm_a_im'], 'ssm_log_dt': out['ssm_log_dt'], 'ssm_b_re': out['ssm_b_re'], 'ssm_b_im': out['ssm_b_im'], 'ssm_c_re': out['ssm_c_re'], 'ssm_c_im': out['ssm_c_im'], 'ssm_d': out['ssm_d'], 'glu_w': out['glu_w'], 'glu_b': out['glu_b'], 'attn_out_norm_g': out['attn_out_norm_g'], 'ssm_out_norm_g': out['ssm_out_norm_g'], 'w_out': out['w_out'], 'norm2_g': out['norm2_g'], 'w_mlp_up': out['w_mlp_up'], 'w_mlp_down': out['w_mlp_down'], 'loss_target': out['loss_target'], 'm_norm1_g': out['m_norm1_g'], 'm_w_in': out['m_w_in'], 'm_q_norm_g': out['m_q_norm_g'], 'm_k_norm_g': out['m_k_norm_g'], 'm_ssm_a_re': out['m_ssm_a_re'], 'm_ssm_a_im': out['m_ssm_a_im'], 'm_ssm_log_dt': out['m_ssm_log_dt'], 'm_ssm_b_re': out['m_ssm_b_re'], 'm_ssm_b_im': out['m_ssm_b_im'], 'm_ssm_c_re': out['m_ssm_c_re'], 'm_ssm_c_im': out['m_ssm_c_im'], 'm_ssm_d': out['m_ssm_d'], 'm_glu_w': out['m_glu_w'], 'm_glu_b': out['m_glu_b'], 'm_attn_out_norm_g': out['m_attn_out_norm_g'], 'm_ssm_out_norm_g': out['m_ssm_out_norm_g'], 'm_w_out': out['m_w_out'], 'm_norm2_g': out['m_norm2_g'], 'm_w_mlp_up': out['m_w_mlp_up'], 'm_w_mlp_down': out['m_w_mlp_down'], 'v_norm1_g': out['v_norm1_g'], 'v_w_in': out['v_w_in'], 'v_q_norm_g': out['v_q_norm_g'], 'v_k_norm_g': out['v_k_norm_g'], 'v_ssm_a_re': out['v_ssm_a_re'], 'v_ssm_a_im': out['v_ssm_a_im'], 'v_ssm_log_dt': out['v_ssm_log_dt'], 'v_ssm_b_re': out['v_ssm_b_re'], 'v_ssm_b_im': out['v_ssm_b_im'], 'v_ssm_c_re': out['v_ssm_c_re'], 'v_ssm_c_im': out['v_ssm_c_im'], 'v_ssm_d': out['v_ssm_d'], 'v_glu_w': out['v_glu_w'], 'v_glu_b': out['v_glu_b'], 'v_attn_out_norm_g': out['v_attn_out_norm_g'], 'v_ssm_out_norm_g': out['v_ssm_out_norm_g'], 'v_w_out': out['v_w_out'], 'v_norm2_g': out['v_norm2_g'], 'v_w_mlp_up': out['v_w_mlp_up'], 'v_w_mlp_down': out['v_w_mlp_down']}


def _loss(weights, diff, rest, loss_target):
    with _jax.named_scope("forward"):
        args = {**rest, TWIN_DIFF_INPUT: diff, **{k: w.astype(_WEIGHT_DTYPES[k]) for k, w in weights.items()}}
        y = _forward(args)
    with _jax.named_scope("loss_head"):
        err = _jnp.square(y.astype(_jnp.float32) - loss_target)
        return 0.5 * _jnp.sum(_jnp.mean(err, axis=-1)) if err.ndim else 0.5 * err


def _adamw(w, g, m, v):
    m = ADAM_B1 * m + (1.0 - ADAM_B1) * g
    v = ADAM_B2 * v + (1.0 - ADAM_B2) * _jnp.square(g)
    m_hat = m / (1.0 - ADAM_B1 ** ADAM_STEP)
    v_hat = v / (1.0 - ADAM_B2 ** ADAM_STEP)
    delta = -ADAM_LR * (m_hat / (_jnp.sqrt(v_hat) + ADAM_EPS) + ADAM_WD * w)
    return delta, m, v


def reference(x, norm1_g, w_in, q_norm_g, k_norm_g, ssm_a_re, ssm_a_im, ssm_log_dt, ssm_b_re, ssm_b_im, ssm_c_re, ssm_c_im, ssm_d, glu_w, glu_b, attn_out_norm_g, ssm_out_norm_g, w_out, norm2_g, w_mlp_up, w_mlp_down, loss_target, m_norm1_g, m_w_in, m_q_norm_g, m_k_norm_g, m_ssm_a_re, m_ssm_a_im, m_ssm_log_dt, m_ssm_b_re, m_ssm_b_im, m_ssm_c_re, m_ssm_c_im, m_ssm_d, m_glu_w, m_glu_b, m_attn_out_norm_g, m_ssm_out_norm_g, m_w_out, m_norm2_g, m_w_mlp_up, m_w_mlp_down, v_norm1_g, v_w_in, v_q_norm_g, v_k_norm_g, v_ssm_a_re, v_ssm_a_im, v_ssm_log_dt, v_ssm_b_re, v_ssm_b_im, v_ssm_c_re, v_ssm_c_im, v_ssm_d, v_glu_w, v_glu_b, v_attn_out_norm_g, v_ssm_out_norm_g, v_w_out, v_norm2_g, v_w_mlp_up, v_w_mlp_down):
    given = dict(x=x, norm1_g=norm1_g, w_in=w_in, q_norm_g=q_norm_g, k_norm_g=k_norm_g, ssm_a_re=ssm_a_re, ssm_a_im=ssm_a_im, ssm_log_dt=ssm_log_dt, ssm_b_re=ssm_b_re, ssm_b_im=ssm_b_im, ssm_c_re=ssm_c_re, ssm_c_im=ssm_c_im, ssm_d=ssm_d, glu_w=glu_w, glu_b=glu_b, attn_out_norm_g=attn_out_norm_g, ssm_out_norm_g=ssm_out_norm_g, w_out=w_out, norm2_g=norm2_g, w_mlp_up=w_mlp_up, w_mlp_down=w_mlp_down, loss_target=loss_target, m_norm1_g=m_norm1_g, m_w_in=m_w_in, m_q_norm_g=m_q_norm_g, m_k_norm_g=m_k_norm_g, m_ssm_a_re=m_ssm_a_re, m_ssm_a_im=m_ssm_a_im, m_ssm_log_dt=m_ssm_log_dt, m_ssm_b_re=m_ssm_b_re, m_ssm_b_im=m_ssm_b_im, m_ssm_c_re=m_ssm_c_re, m_ssm_c_im=m_ssm_c_im, m_ssm_d=m_ssm_d, m_glu_w=m_glu_w, m_glu_b=m_glu_b, m_attn_out_norm_g=m_attn_out_norm_g, m_ssm_out_norm_g=m_ssm_out_norm_g, m_w_out=m_w_out, m_norm2_g=m_norm2_g, m_w_mlp_up=m_w_mlp_up, m_w_mlp_down=m_w_mlp_down, v_norm1_g=v_norm1_g, v_w_in=v_w_in, v_q_norm_g=v_q_norm_g, v_k_norm_g=v_k_norm_g, v_ssm_a_re=v_ssm_a_re, v_ssm_a_im=v_ssm_a_im, v_ssm_log_dt=v_ssm_log_dt, v_ssm_b_re=v_ssm_b_re, v_ssm_b_im=v_ssm_b_im, v_ssm_c_re=v_ssm_c_re, v_ssm_c_im=v_ssm_c_im, v_ssm_d=v_ssm_d, v_glu_w=v_glu_w, v_glu_b=v_glu_b, v_attn_out_norm_g=v_attn_out_norm_g, v_ssm_out_norm_g=v_ssm_out_norm_g, v_w_out=v_w_out, v_norm2_g=v_norm2_g, v_w_mlp_up=v_w_mlp_up, v_w_mlp_down=v_w_mlp_down)
    weights = {n: given[n] for n in TWIN_WEIGHTS}
    shared = {n: given[n] for n in SHARED_INPUTS}
    per_example = {n: given[n] for n in ['x']}
    grad_fn = _jax.value_and_grad(_loss, argnums=(0, 1))

    def one_microbatch(ex, loss_target):
        ex = dict(ex)
        diff = ex.pop(TWIN_DIFF_INPUT)
        return grad_fn(weights, diff, {**shared, **ex}, loss_target)

    if N_MICROBATCH == 1:
        loss, (grad_w, grad_x) = one_microbatch(per_example, given["loss_target"])
    else:
        def body(carry, xs):
            loss_sum, grad_sum = carry
            l_k, (gw_k, gx_k) = one_microbatch(xs[0], xs[1])
            with _jax.named_scope("update"):
                return (loss_sum + l_k, _jax.tree.map(_jnp.add, grad_sum, gw_k)), gx_k

        init = (_jnp.zeros((), _jnp.float32), _jax.tree.map(_jnp.zeros_like, weights))
        (loss, grad_w), grad_x = _jax.lax.scan(body, init, (per_example, given["loss_target"]))
    with _jax.named_scope("update"):
        delta_w, new_m, new_v = {}, {}, {}
        for n in TWIN_WEIGHTS:
            delta_w[n], new_m[n], new_v[n] = _adamw(weights[n], grad_w[n], given["m_" + n], given["v_" + n])
    return (loss, grad_x, *[grad_w[n] for n in TWIN_WEIGHTS], *[delta_w[n] for n in TWIN_WEIGHTS],
            *[new_m[n] for n in TWIN_WEIGHTS], *[new_v[n] for n in TWIN_WEIGHTS])
```

```python
import math

import jax
import jax.numpy as jnp
from jax import lax
from jax.experimental import pallas as pl
from jax.experimental.pallas import tpu as pltpu

F32 = jnp.float32
BF16 = jnp.bfloat16

D_MODEL = 1024
ATTN_W = 512
HEAD_DIM = 64
SSM_W = 512
SSM_GROUP = 16
SSM_GROUPS = 32
SSM_STATE = 64
N_STATE = SSM_GROUPS * SSM_STATE
D_FF = 4096
EPS = 1e-6
NEG_INF = -1e30
ATTN_CHUNK = 2048
ATTN_BLOCK = 128
DILATIONS = (1, 4, 16)
N_SEG = 8
SSM_LK = 64
N_DEV = 8
LANES = 128

ADAM_LR = 0.001
ADAM_B1 = 0.9
ADAM_B2 = 0.999
ADAM_EPS = 1e-08
ADAM_WD = 0.01
ADAM_STEP = 10

VMEM_LIMIT = 56 * 1024 * 1024
GELU_C = math.sqrt(2.0 / math.pi)
MESH = pl.DeviceIdType.MESH


def _cp(sem, vmem=VMEM_LIMIT):
    return pltpu.CompilerParams(dimension_semantics=sem, vmem_limit_bytes=vmem)


def _dot(a, b):
    return jnp.dot(a, b, preferred_element_type=F32)


def _dot_nt(a, b):
    return lax.dot_general(a, b, (((1,), (1,)), ((), ())), preferred_element_type=F32)


def _dot_tn(a, b):
    return lax.dot_general(a, b, (((0,), (0,)), ((), ())), preferred_element_type=F32)


def _group_mean(x2, gmat):
    hi = x2.astype(BF16)
    lo = (x2 - hi.astype(F32)).astype(BF16)
    return _dot(hi, gmat) + _dot(lo, gmat)


def _rms(x):
    return lax.rsqrt(jnp.mean(x * x, axis=-1, keepdims=True) + EPS)


def _rms_bwd(dy, x, g, n):
    r = _rms(x)
    gdy = dy * g
    dx = r * gdy - x * (r * r * r) * (jnp.sum(gdy * x, axis=-1, keepdims=True) / n)
    return dx, dy * (x * r)


def _gelu(y):
    t = jnp.tanh(GELU_C * (y + 0.044715 * (y * y * y)))
    return 0.5 * y * (1.0 + t), t


def _full(shape):
    nd = len(shape)
    return pl.BlockSpec(shape, lambda *_: (0,) * nd)


def _sds(shape, dtype=F32):
    return jax.ShapeDtypeStruct(shape, dtype)


def _fwd_proj(x, g1, w_in, gq, gk, gmat, tm=512):
    s = x.shape[0]

    def body(x_ref, g1_ref, w_ref, gq_ref, gk_ref, gm_ref, qk_ref, qn_ref, kn_ref, v_ref, u_ref, xn_ref):
        xv = x_ref[...]
        xnb = ((xv * _rms(xv)) * g1_ref[...]).astype(BF16)
        xn_ref[...] = xnb
        proj = _dot(xnb, w_ref[...])
        q = proj[:, :ATTN_W]
        k = proj[:, ATTN_W:2 * ATTN_W]
        qk_ref[...] = proj[:, :2 * ATTN_W]
        v_ref[...] = proj[:, 2 * ATTN_W:3 * ATTN_W]
        u_ref[...] = proj[:, 3 * ATTN_W:]
        gm = gm_ref[...]
        qn_ref[...] = (q * lax.rsqrt(_group_mean(q * q, gm) + EPS)) * gq_ref[...]
        kn_ref[...] = (k * lax.rsqrt(_group_mean(k * k, gm) + EPS)) * gk_ref[...]

    row = lambda w: pl.BlockSpec((tm, w), lambda i: (i, 0))
    return pl.pallas_call(
        body, name="fwd_proj", grid=(s // tm,),
        in_specs=[row(D_MODEL), _full((1, D_MODEL)), _full((D_MODEL, 4 * ATTN_W)), _full((1, ATTN_W)),
                  _full((1, ATTN_W)), _full((ATTN_W, ATTN_W))],
        out_specs=[row(2 * ATTN_W), row(ATTN_W), row(ATTN_W), row(ATTN_W), row(ATTN_W), row(D_MODEL)],
        out_shape=[_sds((s, 2 * ATTN_W)), _sds((s, ATTN_W)), _sds((s, ATTN_W)), _sds((s, ATTN_W)),
                   _sds((s, ATTN_W)), _sds((s, D_MODEL), BF16)],
        compiler_params=_cp(("parallel",)),
    )(x, g1, w_in, gq, gk, gmat)


def _attn_rows(t, d, nb):
    if d == 1:
        q0 = pl.multiple_of(t * ATTN_BLOCK, ATTN_BLOCK)
        return (t, pl.ds(q0, ATTN_BLOCK), pl.ds(pl.multiple_of(ATTN_CHUNK + q0, ATTN_BLOCK), ATTN_BLOCK),
                pl.ds(pl.multiple_of(ATTN_CHUNK - ATTN_BLOCK + q0, ATTN_BLOCK), ATTN_BLOCK))
    r = t // nb
    b = t % nb
    return (b, pl.ds(ATTN_BLOCK * b * d + r, ATTN_BLOCK, stride=d),
            pl.ds(ATTN_CHUNK + ATTN_BLOCK * b * d + r, ATTN_BLOCK, stride=d),
            pl.ds(ATTN_CHUNK + ATTN_BLOCK * (b - 1) * d + r, ATTN_BLOCK, stride=d))


def _attn_masks():
    row = lax.broadcasted_iota(jnp.int32, (ATTN_BLOCK, LANES), 0)
    col = lax.broadcasted_iota(jnp.int32, (ATTN_BLOCK, LANES), 1)
    return row, col


def _attn_fwd(qn, kn, v):
    s = qn.shape[0]
    nch = s // ATTN_CHUNK
    scale = HEAD_DIM ** -0.5
    npat = len(DILATIONS)

    def body(q_ref, kp_ref, kc_ref, vp_ref, vc_ref, o_ref, lse_ref, kk, vv, m_s, l_s, acc_s):
        i = pl.program_id(1)
        kk[pl.ds(0, ATTN_CHUNK), :] = kp_ref[...]
        kk[pl.ds(ATTN_CHUNK, ATTN_CHUNK), :] = kc_ref[...]
        vv[pl.ds(0, ATTN_CHUNK), :] = vp_ref[...]
        vv[pl.ds(ATTN_CHUNK, ATTN_CHUNK), :] = vc_ref[...]
        row, col = _attn_masks()
        head0 = col < HEAD_DIM
        mask_cur = col <= row
        diff = col - row

        for p, d in enumerate(DILATIONS):
            nb = ATTN_CHUNK // (ATTN_BLOCK * d)

            def blk(t, carry, p=p, d=d, nb=nb):
                b, qrows, crows, prows = _attn_rows(t, d, nb)
                q = q_ref[qrows, :]
                kc = kk[crows, :].astype(BF16)
                kp = kk[prows, :].astype(BF16)
                vc = vv[crows, :].astype(BF16)
                vp = vv[prows, :].astype(BF16)
                thr = jnp.where(jnp.logical_or(i > 0, b > 0), 0, 4 * ATTN_BLOCK)
                mask_prev = diff >= thr
                accs, ms, ls = [], [], []
                for h in range(2):
                    hm = head0 if h == 0 else jnp.logical_not(head0)
                    qh = jnp.where(hm, q, 0.0).astype(BF16)
                    s_p = jnp.where(mask_prev, _dot_nt(qh, kp) * scale, NEG_INF)
                    s_c = jnp.where(mask_cur, _dot_nt(qh, kc) * scale, NEG_INF)
                    m = jnp.maximum(jnp.max(s_p, axis=-1, keepdims=True), jnp.max(s_c, axis=-1, keepdims=True))
                    e_p = jnp.exp(s_p - m)
                    e_c = jnp.exp(s_c - m)
                    l = jnp.sum(e_p, axis=-1, keepdims=True) + jnp.sum(e_c, axis=-1, keepdims=True)
                    accs.append(_dot(e_p.astype(BF16), vp) + _dot(e_c.astype(BF16), vc))
                    ms.append(jnp.broadcast_to(m, (ATTN_BLOCK, LANES)))
                    ls.append(jnp.broadcast_to(l, (ATTN_BLOCK, LANES)))
                m_s[p, qrows, :] = jnp.where(head0, ms[0], ms[1])
                l_s[p, qrows, :] = jnp.where(head0, ls[0], ls[1])
                acc_s[p, qrows, :] = jnp.where(head0, accs[0], accs[1])
                return carry

            lax.fori_loop(0, ATTN_CHUNK // ATTN_BLOCK, blk, 0)

        def merge(t, carry):
            rows = pl.ds(pl.multiple_of(t * ATTN_BLOCK, ATTN_BLOCK), ATTN_BLOCK)
            m_all = m_s[0, rows, :]
            for p in range(1, npat):
                m_all = jnp.maximum(m_all, m_s[p, rows, :])
            num = jnp.zeros((ATTN_BLOCK, LANES), F32)
            den = jnp.zeros((ATTN_BLOCK, LANES), F32)
            for p in range(npat):
                w = jnp.exp(m_s[p, rows, :] - m_all)
                num = num + acc_s[p, rows, :] * w
                den = den + l_s[p, rows, :] * w
            o_ref[rows, :] = num / den
            lse_ref[rows, :] = m_all + jnp.log(den)
            return carry

        lax.fori_loop(0, ATTN_CHUNK // ATTN_BLOCK, merge, 0)

    cur = pl.BlockSpec((ATTN_CHUNK, LANES), lambda h, i: (i, h))
    prev = pl.BlockSpec((ATTN_CHUNK, LANES), lambda h, i: (jnp.maximum(i - 1, 0), h))
    return pl.pallas_call(
        body, name="attn_fwd", grid=(ATTN_W // LANES, nch),
        in_specs=[cur, prev, cur, prev, cur],
        out_specs=[cur, cur],
        out_shape=[_sds((s, ATTN_W)), _sds((s, ATTN_W))],
        scratch_shapes=[pltpu.VMEM((2 * ATTN_CHUNK, LANES), F32), pltpu.VMEM((2 * ATTN_CHUNK, LANES), F32),
                        pltpu.VMEM((npat, ATTN_CHUNK, LANES), F32), pltpu.VMEM((npat, ATTN_CHUNK, LANES), F32),
                        pltpu.VMEM((npat, ATTN_CHUNK, LANES), F32)],
        compiler_params=_cp(("parallel", "parallel")),
    )(qn, kn, kn, v, v)


def _attn_bwd(qn, kn, v, o, lse, do):
    s = qn.shape[0]
    nch = s // ATTN_CHUNK
    scale = HEAD_DIM ** -0.5
    npat = len(DILATIONS)

    def body(q_ref, kp_ref, kc_ref, vp_ref, vc_ref, o_ref, lse_ref, do_ref, dq_ref, dk_ref, dv_ref,
             kk, vv, dkk, dvv, dq_s, dl_s):
        step = pl.program_id(1)
        i = nch - 1 - step
        kk[pl.ds(0, ATTN_CHUNK), :] = kp_ref[...]
        kk[pl.ds(ATTN_CHUNK, ATTN_CHUNK), :] = kc_ref[...]
        vv[pl.ds(0, ATTN_CHUNK), :] = vp_ref[...]
        vv[pl.ds(ATTN_CHUNK, ATTN_CHUNK), :] = vc_ref[...]

        @pl.when(step == 0)
        def _():
            dkk[pl.ds(ATTN_CHUNK, ATTN_CHUNK), :] = jnp.zeros((ATTN_CHUNK, LANES), F32)
            dvv[pl.ds(ATTN_CHUNK, ATTN_CHUNK), :] = jnp.zeros((ATTN_CHUNK, LANES), F32)

        @pl.when(step > 0)
        def _():
            dkk[pl.ds(ATTN_CHUNK, ATTN_CHUNK), :] = dkk[pl.ds(0, ATTN_CHUNK), :]
            dvv[pl.ds(ATTN_CHUNK, ATTN_CHUNK), :] = dvv[pl.ds(0, ATTN_CHUNK), :]

        dkk[pl.ds(0, ATTN_CHUNK), :] = jnp.zeros((ATTN_CHUNK, LANES), F32)
        dvv[pl.ds(0, ATTN_CHUNK), :] = jnp.zeros((ATTN_CHUNK, LANES), F32)

        row, col = _attn_masks()
        head0 = col < HEAD_DIM
        mask_cur = col <= row
        diff = col - row

        def delta(t, carry):
            rows = pl.ds(pl.multiple_of(t * ATTN_BLOCK, ATTN_BLOCK), ATTN_BLOCK)
            prod = do_ref[rows, :] * o_ref[rows, :]
            d0 = jnp.sum(jnp.where(head0, prod, 0.0), axis=-1, keepdims=True)
            d1 = jnp.sum(jnp.where(head0, 0.0, prod), axis=-1, keepdims=True)
            dl_s[rows, :] = jnp.where(head0, d0, d1)
            return carry

        lax.fori_loop(0, ATTN_CHUNK // ATTN_BLOCK, delta, 0)

        for p, d in enumerate(DILATIONS):
            nb = ATTN_CHUNK // (ATTN_BLOCK * d)

            def blk(t, carry, p=p, d=d, nb=nb):
                b, qrows, crows, prows = _attn_rows(t, d, nb)
                q = q_ref[qrows, :]
                dout = do_ref[qrows, :]
                lse_b = lse_ref[qrows, :]
                dl_b = dl_s[qrows, :]
                kc = kk[crows, :].astype(BF16)
                kp = kk[prows, :].astype(BF16)
                vc = vv[crows, :].astype(BF16)
                vp = vv[prows, :].astype(BF16)
                thr = jnp.where(jnp.logical_or(i > 0, b > 0), 0, 4 * ATTN_BLOCK)
                mask_prev = diff >= thr
                dqs = []
                dk_p = jnp.zeros((ATTN_BLOCK, LANES), F32)
                dk_c = jnp.zeros((ATTN_BLOCK, LANES), F32)
                dv_p = jnp.zeros((ATTN_BLOCK, LANES), F32)
                dv_c = jnp.zeros((ATTN_BLOCK, LANES), F32)
                for h in range(2):
                    hm = head0 if h == 0 else jnp.logical_not(head0)
                    c0 = h * HEAD_DIM
                    qh = jnp.where(hm, q, 0.0).astype(BF16)
                    doh = jnp.where(hm, dout, 0.0).astype(BF16)
                    lse_h = lse_b[:, c0:c0 + 1]
                    dl_h = dl_b[:, c0:c0 + 1]
                    s_p = jnp.where(mask_prev, _dot_nt(qh, kp) * scale, NEG_INF)
                    s_c = jnp.where(mask_cur, _dot_nt(qh, kc) * scale, NEG_INF)
                    p_p = jnp.exp(s_p - lse_h)
                    p_c = jnp.exp(s_c - lse_h)
                    ds_p = (p_p * (_dot_nt(doh, vp) - dl_h) * scale).astype(BF16)
                    ds_c = (p_c * (_dot_nt(doh, vc) - dl_h) * scale).astype(BF16)
                    dqs.append(_dot(ds_p, kp) + _dot(ds_c, kc))
                    dk_p = dk_p + _dot_tn(ds_p, qh)
                    dk_c = dk_c + _dot_tn(ds_c, qh)
                    dv_p = dv_p + _dot_tn(p_p.astype(BF16), doh)
                    dv_c = dv_c + _dot_tn(p_c.astype(BF16), doh)
                dq_s[p, qrows, :] = jnp.where(head0, dqs[0], dqs[1])
                dkk[prows, :] = dkk[prows, :] + dk_p
                dkk[crows, :] = dkk[crows, :] + dk_c
                dvv[prows, :] = dvv[prows, :] + dv_p
                dvv[crows, :] = dvv[crows, :] + dv_c
                return carry

            lax.fori_loop(0, ATTN_CHUNK // ATTN_BLOCK, blk, 0)

        def finish(t, carry):
            rows = pl.ds(pl.multiple_of(t * ATTN_BLOCK, ATTN_BLOCK), ATTN_BLOCK)
            acc = dq_s[0, rows, :]
            for p in range(1, npat):
                acc = acc + dq_s[p, rows, :]
            dq_ref[rows, :] = acc
            return carry

        lax.fori_loop(0, ATTN_CHUNK // ATTN_BLOCK, finish, 0)
        dk_ref[...] = dkk[pl.ds(ATTN_CHUNK, ATTN_CHUNK), :]
        dv_ref[...] = dvv[pl.ds(ATTN_CHUNK, ATTN_CHUNK), :]

    cur = pl.BlockSpec((ATTN_CHUNK, LANES), lambda h, t: (nch - 1 - t, h))
    prev = pl.BlockSpec((ATTN_CHUNK, LANES), lambda h, t: (jnp.maximum(nch - 2 - t, 0), h))
    big = pltpu.VMEM((2 * ATTN_CHUNK, LANES), F32)
    return pl.pallas_call(
        body, name="attn_bwd", grid=(ATTN_W // LANES, nch),
        in_specs=[cur, prev, cur, prev, cur, cur, cur, cur],
        out_specs=[cur, cur, cur],
        out_shape=[_sds((s, ATTN_W))] * 3,
        scratch_shapes=[big, big, big, big, pltpu.VMEM((npat, ATTN_CHUNK, LANES), F32),
                        pltpu.VMEM((ATTN_CHUNK, LANES), F32)],
        compiler_params=_cp(("parallel", "arbitrary")),
    )(qn, kn, kn, v, v, o, lse, do)


def _discretize(lr, li, dt):
    mag = jnp.exp(lr * dt)
    abr = mag * jnp.cos(li * dt)
    abi = mag * jnp.sin(li * dt)
    den = lr * lr + li * li
    nr, ni = abr - 1.0, abi
    cr = (nr * lr + ni * li) / den
    ci = (ni * lr - nr * li) / den
    return abr, abi, den, nr, ni, cr, ci


def _ssm_discretize(a_re, a_im, log_dt, b_re_t, b_im_t):
    def body(ar_ref, ai_ref, ldt_ref, br_ref, bi_ref, abr_ref, abi_ref, bbr_ref, bbi_ref):
        abr, abi, _, _, _, cr, ci = _discretize(ar_ref[...], ai_ref[...], jnp.exp(ldt_ref[...]))
        br, bi = br_ref[...], bi_ref[...]
        abr_ref[...] = abr
        abi_ref[...] = abi
        bbr_ref[...] = cr * br - ci * bi
        bbi_ref[...] = cr * bi + ci * br

    return pl.pallas_call(
        body, name="ssm_discretize",
        out_shape=[_sds(a_re.shape)] * 2 + [_sds(b_re_t.shape)] * 2,
    )(a_re, a_im, log_dt, b_re_t, b_im_t)


def _ssm_discretize_bwd(a_re, a_im, log_dt, b_re_t, b_im_t, dabr, dabi, dbbr, dbbi):
    def body(ar_ref, ai_ref, ldt_ref, br_ref, bi_ref, dabr_ref, dabi_ref, dbbr_ref, dbbi_ref,
             dar_ref, dai_ref, dldt_ref, dbr_ref, dbi_ref):
        lr, li = ar_ref[...], ai_ref[...]
        dt = jnp.exp(ldt_ref[...])
        abr, abi, den, nr, ni, cr, ci = _discretize(lr, li, dt)
        br, bi = br_ref[...], bi_ref[...]
        gbr, gbi = dbbr_ref[...], dbbi_ref[...]
        dcr = jnp.sum(gbr * br + gbi * bi, axis=1, keepdims=True)
        dci = jnp.sum(gbi * br - gbr * bi, axis=1, keepdims=True)
        dbr_ref[...] = cr * gbr + ci * gbi
        dbi_ref[...] = cr * gbi - ci * gbr
        dnr = (dcr * lr - dci * li) / den
        dni = (dcr * li + dci * lr) / den
        dden = -(dcr * cr + dci * ci) / den
        dlr = (dcr * nr + dci * ni) / den + dden * 2.0 * lr
        dli = (dcr * ni - dci * nr) / den + dden * 2.0 * li
        gabr = dabr_ref[...] + dnr
        gabi = dabi_ref[...] + dni
        dphi = gabr * abr + gabi * abi
        dth = gabi * abr - gabr * abi
        dar_ref[...] = dlr + dphi * dt
        dai_ref[...] = dli + dth * dt
        dldt_ref[...] = jnp.sum(dphi * lr + dth * li, axis=2, keepdims=True) * dt

    return pl.pallas_call(
        body, name="ssm_discretize_bwd",
        out_shape=[_sds(a_re.shape)] * 2 + [_sds(log_dt.shape)] + [_sds(b_re_t.shape)] * 2,
    )(a_re, a_im, log_dt, b_re_t, b_im_t, dabr, dabi, dbbr, dbbi)


def _ssm_power(abr, abi, n_sq):
    def body(r_ref, i_ref, or_ref, oi_ref):
        r, i = r_ref[...], i_ref[...]
        for _ in range(n_sq):
            r, i = r * r - i * i, 2.0 * r * i
        or_ref[...] = r
        oi_ref[...] = i

    return pl.pallas_call(body, name="ssm_power", out_shape=[_sds(abr.shape)] * 2)(abr, abi)


N_CB = SSM_W // LANES
CB_STATES = N_STATE // N_CB
ROWS = N_SEG * SSM_LK


class _Neg:
    def __init__(self, ref):
        self.ref = ref

    def __getitem__(self, idx):
        return -self.ref[idx]


def _seg_init(fin_r, fin_i, pw_r, pw_i, x_r, x_i, reverse):
    zero = jnp.zeros((1, N_STATE), F32)
    cr, ci = zero, zero
    order = range(N_SEG - 1, -1, -1) if reverse else range(N_SEG)
    pr = pw_r[...]
    pi = -pw_i[...] if reverse else pw_i[...]
    for j in order:
        x_r[pl.ds(j, 1), :] = cr
        x_i[pl.ds(j, 1), :] = ci
        fr, fi = fin_r[pl.ds(j, 1), :], fin_i[pl.ds(j, 1), :]
        cr, ci = fr + pr * cr - pi * ci, fi + pr * ci + pi * cr


def _scan_rows(a_r, a_i, b_r, b_i, x_r, x_i, o_r, o_i, b_off, n_steps, reverse):
    w = 512
    for c in range(N_STATE // w):
        cols = pl.ds(c * w, w)
        ar = jnp.broadcast_to(a_r[:, cols], (N_SEG, w))
        ai = jnp.broadcast_to(a_i[:, cols], (N_SEG, w))

        def step(t, carry, cols=cols, ar=ar, ai=ai):
            xr, xi = carry
            k = (n_steps - 1 - t) if reverse else t
            rows = pl.ds(pl.multiple_of(k * N_SEG + b_off, N_SEG), N_SEG)
            nr = ar * xr - ai * xi + b_r[rows, cols]
            ni = ar * xi + ai * xr + b_i[rows, cols]
            o_r[rows, cols] = nr
            o_i[rows, cols] = ni
            return nr, ni

        xr, xi = lax.fori_loop(0, n_steps, step, (x_r[:, cols], x_i[:, cols]), unroll=4)
        x_r[:, cols] = xr
        x_i[:, cols] = xi


def _permute_in(src_ref, dst):
    for c in range(N_CB):
        for j in range(N_SEG):
            dst[c, pl.ds(j, SSM_LK, stride=N_SEG), :] = src_ref[j, :, pl.ds(c * LANES, LANES)]


def _permute_out(src, dst_ref):
    for c in range(N_CB):
        for j in range(N_SEG):
            dst_ref[j, :, pl.ds(c * LANES, LANES)] = src[c, pl.ds(j, SSM_LK, stride=N_SEG), :]


def _ssm_fwd(u3, abr, abi, pw_r, pw_i, fin_r, fin_i, bb_r, bb_i, cc_r, cc_i, dskip, finals_only):
    sl = u3.shape[1]
    nch = sl // SSM_LK

    def body(u_ref, abr_ref, abi_ref, pwr_ref, pwi_ref, finr_ref, fini_ref, bbr_ref, bbi_ref,
             ccr_ref, cci_ref, d_ref, *rest):
        if finals_only:
            xfr_ref, xfi_ref, up, xs_r, xs_i, x_r, x_i = rest
        else:
            y_ref, xsr_ref, xsi_ref, up, yp, xs_r, xs_i, x_r, x_i = rest
        k = pl.program_id(0)

        @pl.when(k == 0)
        def _():
            _seg_init(finr_ref, fini_ref, pwr_ref, pwi_ref, x_r, x_i, False)

        if not finals_only:
            xsr_ref[0] = x_r[...]
            xsi_ref[0] = x_i[...]
        _permute_in(u_ref, up)
        for c in range(N_CB):
            lhs = up[c].astype(BF16)
            xs_r[:, pl.ds(c * CB_STATES, CB_STATES)] = _dot(lhs, bbr_ref[c])
            xs_i[:, pl.ds(c * CB_STATES, CB_STATES)] = _dot(lhs, bbi_ref[c])
        _scan_rows(abr_ref, abi_ref, xs_r, xs_i, x_r, x_i, xs_r, xs_i, 0, SSM_LK, False)
        if finals_only:
            @pl.when(k == nch - 1)
            def _():
                xfr_ref[...] = x_r[...]
                xfi_ref[...] = x_i[...]
        else:
            for c in range(N_CB):
                cols = pl.ds(c * CB_STATES, CB_STATES)
                yp[c] = (_dot(xs_r[:, cols].astype(BF16), ccr_ref[c]) - _dot(xs_i[:, cols].astype(BF16), cci_ref[c])
                         + d_ref[:, pl.ds(c * LANES, LANES)] * up[c])
            _permute_out(yp, y_ref)

    ublk = pl.BlockSpec((N_SEG, SSM_LK, SSM_W), lambda k: (0, k, 0))
    st = pl.BlockSpec((1, N_SEG, N_STATE), lambda k: (k, 0, 0))
    vec = _full((1, N_STATE))
    mat = _full((N_SEG, N_STATE))
    chunk = pltpu.VMEM((N_CB, ROWS, LANES), F32)
    big = pltpu.VMEM((ROWS, N_STATE), F32)
    small = pltpu.VMEM((N_SEG, N_STATE), F32)
    if finals_only:
        out_specs, out_shape = [mat, mat], [_sds((N_SEG, N_STATE))] * 2
        scratch, name = [chunk, big, big, small, small], "ssm_fwd_finals"
    else:
        out_specs = [ublk, st, st]
        out_shape = [_sds(u3.shape)] + [_sds((nch, N_SEG, N_STATE))] * 2
        scratch, name = [chunk, chunk, big, big, small, small], "ssm_fwd"
    return pl.pallas_call(
        body, name=name, grid=(nch,),
        in_specs=[ublk, vec, vec, vec, vec, mat, mat,
                  _full((N_CB, LANES, CB_STATES)), _full((N_CB, LANES, CB_STATES)),
                  _full((N_CB, CB_STATES, LANES)), _full((N_CB, CB_STATES, LANES)), _full((1, SSM_W))],
        out_specs=out_specs, out_shape=out_shape, scratch_shapes=scratch,
        compiler_params=_cp(("arbitrary",)),
    )(u3, abr, abi, pw_r, pw_i, fin_r, fin_i, bb_r, bb_i, cc_r, cc_i, dskip)


def _ssm_bwd(u3, dy3, xst_r, xst_i, abr, abi, pw_r, pw_i, fin_r, fin_i, bb_r, bb_i, bbt_r, bbt_i,
             cct_r, cct_i, dskip, finals_only):
    sl = u3.shape[1]
    nch = sl // SSM_LK

    def body(u_ref, g_ref, xsr_ref, xsi_ref, abr_ref, abi_ref, pwr_ref, pwi_ref,
             finr_ref, fini_ref, bbr_ref, bbi_ref, btr_ref, bti_ref, ctr_ref, cti_ref, d_ref, *rest):
        if finals_only:
            lfr_ref, lfi_ref, gp, l_r, l_i, lam_r, lam_i = rest
        else:
            (du_ref, dar_ref, dai_ref, dbr_ref, dbi_ref, dcr_ref, dci_ref, dd_ref,
             gp, up, yp, l_r, l_i, lam_r, lam_i, x_r, x_i, xx_r, xx_i, sar, sai, sdd) = rest
        t = pl.program_id(0)

        @pl.when(t == 0)
        def _():
            _seg_init(finr_ref, fini_ref, pwr_ref, pwi_ref, lam_r, lam_i, True)
            if not finals_only:
                sar[...] = jnp.zeros_like(sar)
                sai[...] = jnp.zeros_like(sai)
                sdd[...] = jnp.zeros_like(sdd)
                dbr_ref[...] = jnp.zeros_like(dbr_ref)
                dbi_ref[...] = jnp.zeros_like(dbi_ref)
                dcr_ref[...] = jnp.zeros_like(dcr_ref)
                dci_ref[...] = jnp.zeros_like(dci_ref)

        _permute_in(g_ref, gp)
        for c in range(N_CB):
            cols = pl.ds(c * CB_STATES, CB_STATES)
            lhs = gp[c].astype(BF16)
            l_r[:, cols] = _dot(lhs, ctr_ref[c])
            l_i[:, cols] = -_dot(lhs, cti_ref[c])
        if not finals_only:
            _permute_in(u_ref, up)
            x_r[...] = xsr_ref[0]
            x_i[...] = xsi_ref[0]
            xx_r[pl.ds(0, N_SEG), :] = x_r[...]
            xx_i[pl.ds(0, N_SEG), :] = x_i[...]
            for c in range(N_CB):
                lhs = up[c].astype(BF16)
                xx_r[pl.ds(N_SEG, ROWS), pl.ds(c * CB_STATES, CB_STATES)] = _dot(lhs, bbr_ref[c])
                xx_i[pl.ds(N_SEG, ROWS), pl.ds(c * CB_STATES, CB_STATES)] = _dot(lhs, bbi_ref[c])
            _scan_rows(abr_ref, abi_ref, xx_r, xx_i, x_r, x_i, xx_r, xx_i, N_SEG, SSM_LK, False)
        _scan_rows(abr_ref, _Neg(abi_ref), l_r, l_i, lam_r, lam_i, l_r, l_i, 0, SSM_LK, True)
        if finals_only:
            @pl.when(t == nch - 1)
            def _():
                lfr_ref[...] = lam_r[...]
                lfi_ref[...] = lam_i[...]
        else:
            w = 512
            for cc in range(N_STATE // w):
                cols = pl.ds(cc * w, w)

                def acc_step(kk_, carry, cols=cols):
                    sr, si = carry
                    rows = pl.ds(pl.multiple_of(kk_ * N_SEG, N_SEG), N_SEG)
                    lr, li = l_r[rows, cols], l_i[rows, cols]
                    pr, pi = xx_r[rows, cols], xx_i[rows, cols]
                    return sr + lr * pr + li * pi, si + li * pr - lr * pi

                z = jnp.zeros((N_SEG, w), F32)
                sr, si = lax.fori_loop(0, SSM_LK, acc_step, (z, z), unroll=4)
                sar[:, cols] += sr
                sai[:, cols] += si
            for c in range(N_CB):
                cols = pl.ds(c * CB_STATES, CB_STATES)
                lrb = l_r[:, cols].astype(BF16)
                lib = l_i[:, cols].astype(BF16)
                ub = up[c].astype(BF16)
                gb = gp[c].astype(BF16)
                dbr_ref[c] += _dot_tn(lrb, ub)
                dbi_ref[c] += _dot_tn(lib, ub)
                dcr_ref[c] += _dot_tn(gb, xx_r[pl.ds(N_SEG, ROWS), cols].astype(BF16))
                dci_ref[c] += -_dot_tn(gb, xx_i[pl.ds(N_SEG, ROWS), cols].astype(BF16))
                yp[c] = _dot(lrb, btr_ref[c]) + _dot(lib, bti_ref[c]) + d_ref[:, pl.ds(c * LANES, LANES)] * gp[c]
                prod = gp[c] * up[c]
                sdd[:, pl.ds(c * LANES, LANES)] += jnp.sum(prod.reshape(SSM_LK, N_SEG, LANES), axis=0)
            _permute_out(yp, du_ref)

            @pl.when(t == nch - 1)
            def _():
                dar_ref[...] = jnp.sum(sar[...], axis=0, keepdims=True)
                dai_ref[...] = jnp.sum(sai[...], axis=0, keepdims=True)
                dd_ref[...] = jnp.sum(sdd[...], axis=0, keepdims=True)

    ublk = pl.BlockSpec((N_SEG, SSM_LK, SSM_W), lambda t: (0, nch - 1 - t, 0))
    st = pl.BlockSpec((1, N_SEG, N_STATE), lambda t: (nch - 1 - t, 0, 0))
    vec = _full((1, N_STATE))
    mat = _full((N_SEG, N_STATE))
    cs = _full((N_CB, LANES, CB_STATES))
    sc = _full((N_CB, CB_STATES, LANES))
    in_specs = [ublk, ublk, st, st, vec, vec, vec, vec, mat, mat, cs, cs, sc, sc, cs, cs, _full((1, SSM_W))]
    chunk = pltpu.VMEM((N_CB, ROWS, LANES), F32)
    big = pltpu.VMEM((ROWS, N_STATE), F32)
    small = pltpu.VMEM((N_SEG, N_STATE), F32)
    if finals_only:
        out_specs, out_shape = [mat, mat], [_sds((N_SEG, N_STATE))] * 2
        scratch, name = [chunk, big, big, small, small], "ssm_bwd_finals"
    else:
        out_specs = [ublk, vec, vec, sc, sc, cs, cs, _full((1, SSM_W))]
        out_shape = ([_sds(u3.shape), _sds((1, N_STATE)), _sds((1, N_STATE))]
                     + [_sds((N_CB, CB_STATES, LANES))] * 2 + [_sds((N_CB, LANES, CB_STATES))] * 2
                     + [_sds((1, SSM_W))])
        xx = pltpu.VMEM((ROWS + N_SEG, N_STATE), F32)
        scratch = [chunk, chunk, chunk, big, big, small, small, small, small, xx, xx, small, small,
                   pltpu.VMEM((N_SEG, SSM_W), F32)]
        name = "ssm_bwd"
    return pl.pallas_call(
        body, name=name, grid=(nch,), in_specs=in_specs, out_specs=out_specs, out_shape=out_shape,
        scratch_shapes=scratch, compiler_params=_cp(("arbitrary",)),
    )(u3, dy3, xst_r, xst_i, abr, abi, pw_r, pw_i, fin_r, fin_i, bb_r, bb_i, bbt_r, bbt_i, cct_r, cct_i, dskip)


def _row(tm, w):
    return pl.BlockSpec((tm, w), lambda i: (i, 0))


def _acc_rows(ref, rows, first):
    @pl.when(first)
    def _():
        ref[...] = jnp.zeros_like(ref)

    ref[...] += jnp.sum(rows, axis=0, keepdims=True)


def _fwd_mix(attn, y, x, glu_w, glu_b, ga, gs, w_out, g2, tm=512):
    s = x.shape[0]

    def body(a_ref, y_ref, x_ref, gw_ref, gb_ref, ga_ref, gs_ref, wo_ref, g2_ref, mix_ref, x2_ref, h_ref):
        a = a_ref[...]
        anb = ((a * _rms(a)) * ga_ref[...]).astype(BF16)
        z, _ = _gelu(y_ref[...])
        so = z * jax.nn.sigmoid(_dot(z.astype(BF16), gw_ref[...]) + gb_ref[...])
        snb = ((so * _rms(so)) * gs_ref[...]).astype(BF16)
        mix_ref[:, pl.ds(0, ATTN_W)] = anb
        mix_ref[:, pl.ds(ATTN_W, SSM_W)] = snb
        x2 = x_ref[...] + (_dot(anb, wo_ref[pl.ds(0, ATTN_W), :]) + _dot(snb, wo_ref[pl.ds(ATTN_W, SSM_W), :]))
        x2_ref[...] = x2
        h_ref[...] = ((x2 * _rms(x2)) * g2_ref[...]).astype(BF16)

    return pl.pallas_call(
        body, name="fwd_mix", grid=(s // tm,),
        in_specs=[_row(tm, ATTN_W), _row(tm, SSM_W), _row(tm, D_MODEL), _full((SSM_W, SSM_W)), _full((1, SSM_W)),
                  _full((1, ATTN_W)), _full((1, SSM_W)), _full((D_MODEL, D_MODEL)), _full((1, D_MODEL))],
        out_specs=[_row(tm, D_MODEL), _row(tm, D_MODEL), _row(tm, D_MODEL)],
        out_shape=[_sds((s, D_MODEL), BF16), _sds((s, D_MODEL)), _sds((s, D_MODEL), BF16)],
        compiler_params=_cp(("parallel",)),
    )(attn, y, x, glu_w, glu_b, ga, gs, w_out, g2)


def _mlp_up(h, w_up, tm=512, bn=1024):
    s = h.shape[0]

    def body(h_ref, w_ref, r_ref, hdn_ref):
        r = jnp.maximum(_dot(h_ref[...], w_ref[...]), 0.0)
        r_ref[...] = r.astype(BF16)
        hdn_ref[...] = (r * r).astype(BF16)

    blk = pl.BlockSpec((tm, bn), lambda i, j: (i, j))
    return pl.pallas_call(
        body, name="mlp_up", grid=(s // tm, D_FF // bn),
        in_specs=[pl.BlockSpec((tm, D_MODEL), lambda i, j: (i, 0)), pl.BlockSpec((D_MODEL, bn), lambda i, j: (0, j))],
        out_specs=[blk, blk], out_shape=[_sds((s, D_FF), BF16)] * 2,
        compiler_params=_cp(("parallel", "parallel")),
    )(h, w_up)


def _mlp_down_loss(hdn, w_down, x2, tgt, tm=512):
    s = x2.shape[0]

    def body(hdn_ref, w_ref, x2_ref, t_ref, dy_ref, dyb_ref, sse_ref):
        err = (x2_ref[...] + _dot(hdn_ref[...], w_ref[...])) - t_ref[...]
        dy = err * (1.0 / D_MODEL)
        dy_ref[...] = dy
        dyb_ref[...] = dy.astype(BF16)

        @pl.when(pl.program_id(0) == 0)
        def _():
            sse_ref[...] = jnp.zeros_like(sse_ref)

        sse_ref[...] += jnp.sum(jnp.sum(err * err, axis=0, keepdims=True), axis=1, keepdims=True)

    return pl.pallas_call(
        body, name="mlp_down_loss", grid=(s // tm,),
        in_specs=[_row(tm, D_FF), _full((D_FF, D_MODEL)), _row(tm, D_MODEL), _row(tm, D_MODEL)],
        out_specs=[_row(tm, D_MODEL), _row(tm, D_MODEL), _full((1, 1))],
        out_shape=[_sds((s, D_MODEL)), _sds((s, D_MODEL), BF16), _sds((1, 1))],
        compiler_params=_cp(("arbitrary",)),
    )(hdn, w_down, x2, tgt)


def _mlp_down_bwd(dyb, w_down_t, r, tm=512, bn=1024):
    s = dyb.shape[0]

    def body(dy_ref, w_ref, r_ref, dup_ref):
        dup_ref[...] = (_dot(dy_ref[...], w_ref[...]) * (2.0 * r_ref[...].astype(F32))).astype(BF16)

    blk = pl.BlockSpec((tm, bn), lambda i, j: (i, j))
    return pl.pallas_call(
        body, name="mlp_down_bwd", grid=(s // tm, D_FF // bn),
        in_specs=[pl.BlockSpec((tm, D_MODEL), lambda i, j: (i, 0)), pl.BlockSpec((D_MODEL, bn), lambda i, j: (0, j)), blk],
        out_specs=blk, out_shape=_sds((s, D_FF), BF16),
        compiler_params=_cp(("parallel", "parallel")),
    )(dyb, w_down_t, r)


def _mlp_up_bwd(dup, w_up_t, x2, g2, dy, tm=512):
    s = x2.shape[0]

    def body(dup_ref, w_ref, x2_ref, g2_ref, dy_ref, dx2_ref, dx2b_ref, dg_ref):
        dx, dg_rows = _rms_bwd(_dot(dup_ref[...], w_ref[...]), x2_ref[...], g2_ref[...], D_MODEL)
        dx2 = dy_ref[...] + dx
        dx2_ref[...] = dx2
        dx2b_ref[...] = dx2.astype(BF16)
        _acc_rows(dg_ref, dg_rows, pl.program_id(0) == 0)

    return pl.pallas_call(
        body, name="mlp_up_bwd", grid=(s // tm,),
        in_specs=[_row(tm, D_FF), _full((D_FF, D_MODEL)), _row(tm, D_MODEL), _full((1, D_MODEL)), _row(tm, D_MODEL)],
        out_specs=[_row(tm, D_MODEL), _row(tm, D_MODEL), _full((1, D_MODEL))],
        out_shape=[_sds((s, D_MODEL)), _sds((s, D_MODEL), BF16), _sds((1, D_MODEL))],
        compiler_params=_cp(("arbitrary",)),
    )(dup, w_up_t, x2, g2, dy)


def _mix_bwd(dx2b, w_out_t, attn, y, glu_w, glu_b, glu_w_t, ga, gs, tm=512):
    s = attn.shape[0]

    def body(dx2_ref, wot_ref, a_ref, y_ref, gw_ref, gb_ref, gwt_ref, ga_ref, gs_ref,
             da_ref, dys_ref, z_ref, dpre_ref, dga_ref, dgs_ref, dgb_ref):
        first = pl.program_id(0) == 0
        dmix = _dot(dx2_ref[...], wot_ref[...])
        da, dga_rows = _rms_bwd(dmix[:, :ATTN_W], a_ref[...], ga_ref[...], ATTN_W)
        da_ref[...] = da
        yv = y_ref[...]
        z, t = _gelu(yv)
        gate = jax.nn.sigmoid(_dot(z.astype(BF16), gw_ref[...]) + gb_ref[...])
        dso, dgs_rows = _rms_bwd(dmix[:, ATTN_W:], z * gate, gs_ref[...], SSM_W)
        dpre = dso * z * gate * (1.0 - gate)
        dpre_b = dpre.astype(BF16)
        dz = dso * gate + _dot(dpre_b, gwt_ref[...])
        dgelu = 0.5 * (1.0 + t) + 0.5 * yv * (1.0 - t * t) * (GELU_C * (1.0 + 3.0 * 0.044715 * (yv * yv)))
        dys_ref[...] = dz * dgelu
        z_ref[...] = z.astype(BF16)
        dpre_ref[...] = dpre_b
        _acc_rows(dga_ref, dga_rows, first)
        _acc_rows(dgs_ref, dgs_rows, first)
        _acc_rows(dgb_ref, dpre, first)

    vec = _full((1, SSM_W))
    return pl.pallas_call(
        body, name="mix_bwd", grid=(s // tm,),
        in_specs=[_row(tm, D_MODEL), _full((D_MODEL, D_MODEL)), _row(tm, ATTN_W), _row(tm, SSM_W),
                  _full((SSM_W, SSM_W)), vec, _full((SSM_W, SSM_W)), vec, vec],
        out_specs=[_row(tm, ATTN_W), _row(tm, SSM_W), _row(tm, SSM_W), _row(tm, SSM_W), vec, vec, vec],
        out_shape=[_sds((s, ATTN_W)), _sds((s, SSM_W)), _sds((s, SSM_W), BF16), _sds((s, SSM_W), BF16),
                   _sds((1, ATTN_W)), _sds((1, SSM_W)), _sds((1, SSM_W))],
        compiler_params=_cp(("arbitrary",)),
    )(dx2b, w_out_t, attn, y, glu_w, glu_b, glu_w_t, ga, gs)


def _qk_bwd(dqn, dkn, qk, dv, du, gq, gk, gmat, tm=512):
    s = qk.shape[0]

    def body(dq_ref, dk_ref, qk_ref, dv_ref, du_ref, gq_ref, gk_ref, gm_ref, dp_ref, dgq_ref, dgk_ref):
        first = pl.program_id(0) == 0
        gm = gm_ref[...]
        for idx, (d_ref, g_ref, dg_ref) in enumerate(((dq_ref, gq_ref, dgq_ref), (dk_ref, gk_ref, dgk_ref))):
            xv = qk_ref[:, pl.ds(idx * ATTN_W, ATTN_W)]
            dyv = d_ref[...]
            r = lax.rsqrt(_group_mean(xv * xv, gm) + EPS)
            gdy = dyv * g_ref[...]
            dx = r * gdy - xv * (r * r * r) * _group_mean(gdy * xv, gm)
            dp_ref[:, pl.ds(idx * ATTN_W, ATTN_W)] = dx.astype(BF16)
            _acc_rows(dg_ref, dyv * (xv * r), first)
        dp_ref[:, pl.ds(2 * ATTN_W, ATTN_W)] = dv_ref[...].astype(BF16)
        dp_ref[:, pl.ds(3 * ATTN_W, SSM_W)] = du_ref[...].astype(BF16)

    vec = _full((1, ATTN_W))
    return pl.pallas_call(
        body, name="qk_bwd", grid=(s // tm,),
        in_specs=[_row(tm, ATTN_W), _row(tm, ATTN_W), _row(tm, 2 * ATTN_W), _row(tm, ATTN_W), _row(tm, SSM_W),
                  vec, vec, _full((ATTN_W, ATTN_W))],
        out_specs=[_row(tm, 4 * ATTN_W), vec, vec],
        out_shape=[_sds((s, 4 * ATTN_W), BF16), _sds((1, ATTN_W)), _sds((1, ATTN_W))],
        compiler_params=_cp(("arbitrary",)),
    )(dqn, dkn, qk, dv, du, gq, gk, gmat)


def _in_bwd(dproj, w_in_t, x, g1, dx2, tm=512):
    s = x.shape[0]

    def body(dp_ref, w_ref, x_ref, g1_ref, dx2_ref, gx_ref, dg_ref):
        dx, dg_rows = _rms_bwd(_dot(dp_ref[...], w_ref[...]), x_ref[...], g1_ref[...], D_MODEL)
        gx_ref[...] = dx2_ref[...] + dx
        _acc_rows(dg_ref, dg_rows, pl.program_id(0) == 0)

    return pl.pallas_call(
        body, name="in_bwd", grid=(s // tm,),
        in_specs=[_row(tm, 4 * ATTN_W), _full((4 * ATTN_W, D_MODEL)), _row(tm, D_MODEL), _full((1, D_MODEL)),
                  _row(tm, D_MODEL)],
        out_specs=[_row(tm, D_MODEL), _full((1, D_MODEL))],
        out_shape=[_sds((s, D_MODEL)), _sds((1, D_MODEL))],
        compiler_params=_cp(("arbitrary",)),
    )(dproj, w_in_t, x, g1, dx2)


def _mm_tn(a, b, name, ts=1024):
    s, k = a.shape
    n = b.shape[1]
    bk, bn = min(k, 1024), min(n, 1024)

    def body(a_ref, b_ref, o_ref):
        @pl.when(pl.program_id(2) == 0)
        def _():
            o_ref[...] = jnp.zeros_like(o_ref)

        o_ref[...] += _dot_tn(a_ref[...], b_ref[...])

    return pl.pallas_call(
        body, name=name, grid=(k // bk, n // bn, s // ts),
        in_specs=[pl.BlockSpec((ts, bk), lambda i, j, t: (t, i)), pl.BlockSpec((ts, bn), lambda i, j, t: (t, j))],
        out_specs=pl.BlockSpec((bk, bn), lambda i, j, t: (i, j)), out_shape=_sds((k, n)),
        compiler_params=_cp(("parallel", "parallel", "arbitrary")),
    )(a, b)


def _peer(k):
    x, y, c = lax.axis_index("x"), lax.axis_index("y"), lax.axis_index("c")
    px = 1 - x if k & 4 else x
    py = 1 - y if k & 2 else y
    pc = 1 - c if k & 1 else c
    return (px, py, pc), 4 * px + 2 * py + pc


def _gather_rows(x_shard):
    m_per, n = x_shard.shape

    def body(x_ref, out_ref, send_sems, recv_sems, local_sem):
        x, y, c = lax.axis_index("x"), lax.axis_index("y"), lax.axis_index("c")
        me, sibling = (x, y, c), (x, y, 1 - c)
        chips = [(1 - x, y), (x, 1 - y), (1 - x, 1 - y)]

        def rows(px, py, pc):
            return out_ref.at[pl.ds((4 * px + 2 * py + pc) * m_per, m_per), :]

        def copy(k, block, to, src=None):
            return pltpu.make_async_remote_copy(
                src_ref=rows(*block) if src is None else src, dst_ref=rows(*block),
                send_sem=send_sems.at[k], recv_sem=recv_sems.at[k], device_id=to, device_id_type=MESH)

        mine = pltpu.make_async_copy(x_ref, rows(*me), local_sem)
        mine.start()
        first = [copy(0, me, sibling, src=x_ref)]
        first += [copy(1 + j, me, (*chip, c), src=x_ref) for j, chip in enumerate(chips)]
        for cp in first:
            cp.start()
        passed = [copy(4 + j, (*chip, c), sibling) for j, chip in enumerate(chips)]
        for j, chip in enumerate(chips):
            copy(1 + j, (*chip, c), me).wait_recv()
            passed[j].start()
        copy(0, sibling, me).wait_recv()
        for j, chip in enumerate(chips):
            copy(4 + j, (*chip, 1 - c), me).wait_recv()
        for cp in first + passed:
            cp.wait_send()
        mine.wait()

    return pl.pallas_call(
        body, name="gather_weights", out_shape=_sds((N_DEV * m_per, n), x_shard.dtype),
        in_specs=[pl.BlockSpec(memory_space=pltpu.VMEM)], out_specs=pl.BlockSpec(memory_space=pltpu.VMEM),
        scratch_shapes=[pltpu.SemaphoreType.DMA((7,)), pltpu.SemaphoreType.DMA((7,)), pltpu.SemaphoreType.DMA],
        compiler_params=pltpu.CompilerParams(vmem_limit_bytes=VMEM_LIMIT),
    )(x_shard)


def _grad_exchange(grads, small):
    srcs = list(grads) + [small]
    n = len(srcs)
    shard_rows = [g.shape[0] // N_DEV for g in grads] + [small.shape[0]]

    def body(*refs):
        ins, outs = refs[:n], refs[n:2 * n]
        send_sems, recv_sems, local_sems = refs[2 * n:]
        _, me = _peer(0)
        started = []
        for w in range(n):
            for k in range(N_DEV):
                peer, pidx = _peer(k)
                if w < n - 1:
                    src = ins[w].at[pl.ds(pidx * shard_rows[w], shard_rows[w]), :]
                else:
                    src = ins[w]
                if k == 0:
                    cp = pltpu.make_async_copy(src, outs[w].at[me], local_sems.at[w])
                else:
                    cp = pltpu.make_async_remote_copy(
                        src_ref=src, dst_ref=outs[w].at[me], send_sem=send_sems.at[w * 7 + k - 1],
                        recv_sem=recv_sems.at[w * 7 + k - 1], device_id=peer, device_id_type=MESH)
                cp.start()
                started.append((k, cp))
        for w in range(n):
            for k in range(1, N_DEV):
                peer, pidx = _peer(k)
                src = ins[w].at[pl.ds(0, shard_rows[w]), :] if w < n - 1 else ins[w]
                pltpu.make_async_remote_copy(
                    src_ref=src, dst_ref=outs[w].at[pidx], send_sem=send_sems.at[w * 7 + k - 1],
                    recv_sem=recv_sems.at[w * 7 + k - 1], device_id=peer, device_id_type=MESH).wait_recv()
        for k, cp in started:
            if k == 0:
                cp.wait()
            else:
                cp.wait_send()

    hbm = pl.BlockSpec(memory_space=pl.ANY)
    return pl.pallas_call(
        body, name="grad_exchange",
        out_shape=[_sds((N_DEV, r, a.shape[1])) for r, a in zip(shard_rows, srcs)],
        in_specs=[hbm] * n, out_specs=[hbm] * n,
        scratch_shapes=[pltpu.SemaphoreType.DMA((n * 7,)), pltpu.SemaphoreType.DMA((n * 7,)),
                        pltpu.SemaphoreType.DMA((n,))],
    )(*srcs)


def _adamw(w, m, v, gparts, name):
    r, c = w.shape
    tr = r if r * c <= 256 * 1024 else 128 * 1024 // c

    def body(w_ref, m_ref, v_ref, g_ref, go_ref, d_ref, mo_ref, vo_ref):
        g = g_ref[0]
        for i in range(1, N_DEV):
            g = g + g_ref[i]
        nm = ADAM_B1 * m_ref[...] + (1.0 - ADAM_B1) * g
        nv = ADAM_B2 * v_ref[...] + (1.0 - ADAM_B2) * (g * g)
        m_hat = nm / (1.0 - ADAM_B1 ** ADAM_STEP)
        v_hat = nv / (1.0 - ADAM_B2 ** ADAM_STEP)
        go_ref[...] = g
        d_ref[...] = -ADAM_LR * (m_hat / (jnp.sqrt(v_hat) + ADAM_EPS) + ADAM_WD * w_ref[...])
        mo_ref[...] = nm
        vo_ref[...] = nv

    blk = pl.BlockSpec((tr, c), lambda i: (i, 0))
    return pl.pallas_call(
        body, name=name, grid=(r // tr,),
        in_specs=[blk, blk, blk, pl.BlockSpec((N_DEV, tr, c), lambda i: (0, i, 0))],
        out_specs=[blk] * 4, out_shape=[_sds((r, c))] * 4,
        compiler_params=_cp(("parallel",)),
    )(w, m, v, gparts)


def _block_diag(a, states_first):
    a4 = a.reshape(N_CB, 8, SSM_GROUP, SSM_STATE)
    eye = jnp.eye(8, dtype=a.dtype)
    if states_first:
        return jnp.einsum("bgcp,gh->bgphc", a4, eye).reshape(N_CB, CB_STATES, LANES)
    return jnp.einsum("bgcp,gh->bgchp", a4, eye).reshape(N_CB, LANES, CB_STATES)


def _block_diag_of(full, states_first):
    if states_first:
        picked = jnp.einsum("bgphc,gh->bgcp", full.reshape(N_CB, 8, SSM_STATE, 8, SSM_GROUP), jnp.eye(8, dtype=full.dtype))
    else:
        picked = jnp.einsum("bgchp,gh->bgcp", full.reshape(N_CB, 8, SSM_GROUP, 8, SSM_STATE), jnp.eye(8, dtype=full.dtype))
    return picked.reshape(SSM_GROUPS, SSM_GROUP, SSM_STATE)


SMALL = ("norm1_g", "q_norm_g", "k_norm_g", "ssm_a_re", "ssm_a_im", "ssm_log_dt", "ssm_b_re", "ssm_b_im",
         "ssm_c_re", "ssm_c_im", "ssm_d", "glu_b", "attn_out_norm_g", "ssm_out_norm_g", "norm2_g")


def _pack_small(arrs):
    parts = []
    for a in arrs:
        flat = a.reshape(-1)
        rows = -(-flat.shape[0] // (8 * LANES)) * 8
        parts.append(jnp.pad(flat, (0, rows * LANES - flat.shape[0])).reshape(rows, LANES))
    return jnp.concatenate(parts, axis=0)


def _unpack_small(packed, shapes):
    out, r0 = [], 0
    for shp in shapes:
        size = math.prod(shp)
        rows = -(-size // (8 * LANES)) * 8
        out.append(packed[r0:r0 + rows].reshape(-1)[:size].reshape(shp))
        r0 += rows
    return out


def kernel(x, norm1_g, w_in, q_norm_g, k_norm_g, ssm_a_re, ssm_a_im, ssm_log_dt, ssm_b_re, ssm_b_im, ssm_c_re, ssm_c_im, ssm_d, glu_w, glu_b, attn_out_norm_g, ssm_out_norm_g, w_out, norm2_g, w_mlp_up, w_mlp_down, loss_target, m_norm1_g, m_w_in, m_q_norm_g, m_k_norm_g, m_ssm_a_re, m_ssm_a_im, m_ssm_log_dt, m_ssm_b_re, m_ssm_b_im, m_ssm_c_re, m_ssm_c_im, m_ssm_d, m_glu_w, m_glu_b, m_attn_out_norm_g, m_ssm_out_norm_g, m_w_out, m_norm2_g, m_w_mlp_up, m_w_mlp_down, v_norm1_g, v_w_in, v_q_norm_g, v_k_norm_g, v_ssm_a_re, v_ssm_a_im, v_ssm_log_dt, v_ssm_b_re, v_ssm_b_im, v_ssm_c_re, v_ssm_c_im, v_ssm_d, v_glu_w, v_glu_b, v_attn_out_norm_g, v_ssm_out_norm_g, v_w_out, v_norm2_g, v_w_mlp_up, v_w_mlp_down):
    weights = dict(norm1_g=norm1_g, w_in=w_in, q_norm_g=q_norm_g, k_norm_g=k_norm_g, ssm_a_re=ssm_a_re,
                   ssm_a_im=ssm_a_im, ssm_log_dt=ssm_log_dt, ssm_b_re=ssm_b_re, ssm_b_im=ssm_b_im,
                   ssm_c_re=ssm_c_re, ssm_c_im=ssm_c_im, ssm_d=ssm_d, glu_w=glu_w, glu_b=glu_b,
                   attn_out_norm_g=attn_out_norm_g, ssm_out_norm_g=ssm_out_norm_g, w_out=w_out, norm2_g=norm2_g,
                   w_mlp_up=w_mlp_up, w_mlp_down=w_mlp_down)
    mom_m = dict(norm1_g=m_norm1_g, w_in=m_w_in, q_norm_g=m_q_norm_g, k_norm_g=m_k_norm_g, ssm_a_re=m_ssm_a_re,
                 ssm_a_im=m_ssm_a_im, ssm_log_dt=m_ssm_log_dt, ssm_b_re=m_ssm_b_re, ssm_b_im=m_ssm_b_im,
                 ssm_c_re=m_ssm_c_re, ssm_c_im=m_ssm_c_im, ssm_d=m_ssm_d, glu_w=m_glu_w, glu_b=m_glu_b,
                 attn_out_norm_g=m_attn_out_norm_g, ssm_out_norm_g=m_ssm_out_norm_g, w_out=m_w_out,
                 norm2_g=m_norm2_g, w_mlp_up=m_w_mlp_up, w_mlp_down=m_w_mlp_down)
    mom_v = dict(norm1_g=v_norm1_g, w_in=v_w_in, q_norm_g=v_q_norm_g, k_norm_g=v_k_norm_g, ssm_a_re=v_ssm_a_re,
                 ssm_a_im=v_ssm_a_im, ssm_log_dt=v_ssm_log_dt, ssm_b_re=v_ssm_b_re, ssm_b_im=v_ssm_b_im,
                 ssm_c_re=v_ssm_c_re, ssm_c_im=v_ssm_c_im, ssm_d=v_ssm_d, glu_w=v_glu_w, glu_b=v_glu_b,
                 attn_out_norm_g=v_attn_out_norm_g, ssm_out_norm_g=v_ssm_out_norm_g, w_out=v_w_out,
                 norm2_g=v_norm2_g, w_mlp_up=v_w_mlp_up, w_mlp_down=v_w_mlp_down)
    order = list(weights)

    xs, tgt = x[0], loss_target[0]
    s = xs.shape[0]
    assert s % ATTN_CHUNK == 0 and (s // N_SEG) % SSM_LK == 0
    seg_len = s // N_SEG
    n_sq = seg_len.bit_length() - 1
    assert 1 << n_sq == seg_len

    win_rows, glu_rows, wout_rows, wup_rows, wdn_rows = 256, 32, 128, 512, 512
    pack = jnp.concatenate([w_in[0].T, glu_w[0].reshape(glu_rows, D_MODEL), w_out[0], w_mlp_up[0].T, w_mlp_down[0]],
                           axis=0).astype(BF16)
    per = pack.shape[0]
    gathered = _gather_rows(pack).reshape(N_DEV, per, D_MODEL)
    offs = [0, win_rows, win_rows + glu_rows, win_rows + glu_rows + wout_rows, win_rows + glu_rows + wout_rows + wup_rows, per]
    piece = lambda i: gathered[:, offs[i]:offs[i + 1], :]
    w_in_t = piece(0).reshape(4 * ATTN_W, D_MODEL)
    glu_full = piece(1).reshape(SSM_W, SSM_W)
    w_out_full = piece(2).reshape(D_MODEL, D_MODEL)
    w_up_t = piece(3).reshape(D_FF, D_MODEL)
    w_down_full = piece(4).reshape(D_FF, D_MODEL)
    w_in_full, w_up_full = w_in_t.T, w_up_t.T
    w_out_t, w_down_t, glu_t = w_out_full.T, w_down_full.T, glu_full.T

    gq = jnp.tile(q_norm_g[0], ATTN_W // HEAD_DIM)[None]
    gk = jnp.tile(k_norm_g[0], ATTN_W // HEAD_DIM)[None]
    lane = jnp.arange(ATTN_W) // HEAD_DIM
    gmat = jnp.where(lane[:, None] == lane[None, :], 1.0 / HEAD_DIM, 0.0).astype(BF16)
    a_re3 = ssm_a_re[0][:, None, :]
    a_im3 = ssm_a_im[0][:, None, :]
    ldt3 = ssm_log_dt[0][:, None, None]
    b_re_t = jnp.swapaxes(ssm_b_re[0], 1, 2)
    b_im_t = jnp.swapaxes(ssm_b_im[0], 1, 2)
    c_re, c_im = ssm_c_re[0], ssm_c_im[0]
    dskip = ssm_d[0].reshape(1, SSM_W)

    qk, qn, kn, v, u, xn = _fwd_proj(xs, norm1_g, w_in_full, gq, gk, gmat)
    attn, lse = _attn_fwd(qn, kn, v)

    abr3, abi3, bbr, bbi = _ssm_discretize(a_re3, a_im3, ldt3, b_re_t, b_im_t)
    abr, abi = abr3.reshape(1, N_STATE), abi3.reshape(1, N_STATE)
    pw_r, pw_i = _ssm_power(abr, abi, n_sq)
    bb_r, bb_i = _block_diag(bbr, False).astype(BF16), _block_diag(bbi, False).astype(BF16)
    bbt_r, bbt_i = _block_diag(bbr, True).astype(BF16), _block_diag(bbi, True).astype(BF16)
    cc_r, cc_i = _block_diag(c_re, True).astype(BF16), _block_diag(c_im, True).astype(BF16)
    cct_r, cct_i = _block_diag(c_re, False).astype(BF16), _block_diag(c_im, False).astype(BF16)
    u3 = u.reshape(N_SEG, seg_len, SSM_W)
    zero_fin = jnp.zeros((N_SEG, N_STATE), F32)
    ssm_args = (abr, abi, pw_r, pw_i)
    xf_r, xf_i = _ssm_fwd(u3, *ssm_args, zero_fin, zero_fin, bb_r, bb_i, cc_r, cc_i, dskip, True)
    y3, xst_r, xst_i = _ssm_fwd(u3, *ssm_args, xf_r, xf_i, bb_r, bb_i, cc_r, cc_i, dskip, False)
    y = y3.reshape(s, SSM_W)

    mix, x2, h = _fwd_mix(attn, y, xs, glu_full, glu_b, attn_out_norm_g, ssm_out_norm_g, w_out_full, norm2_g)
    r_act, hdn = _mlp_up(h, w_up_full)
    dy, dyb, sse = _mlp_down_loss(hdn, w_down_full, x2, tgt)
    loss = lax.psum(0.5 * sse[0, 0] / D_MODEL, ("x", "y", "c"))

    dup = _mlp_down_bwd(dyb, w_down_t, r_act)
    g_w_down = _mm_tn(hdn, dyb, "grad_w_down")
    dx2, dx2b, g_norm2 = _mlp_up_bwd(dup, w_up_t, x2, norm2_g, dy)
    g_w_up_t = _mm_tn(dup, h, "grad_w_up_t")
    dattn, dys, z_b, dpre_b, g_ga, g_gs, g_glu_b = _mix_bwd(dx2b, w_out_t, attn, y, glu_full, glu_b, glu_t,
                                                             attn_out_norm_g, ssm_out_norm_g)
    g_w_out = _mm_tn(mix, dx2b, "grad_w_out")
    g_glu_w = _mm_tn(z_b, dpre_b, "grad_glu_w")
    dqn, dkn, dv = _attn_bwd(qn, kn, v, attn, lse, dattn)

    dy3 = dys.reshape(N_SEG, seg_len, SSM_W)
    bwd_args = (u3, dy3, xst_r, xst_i, abr, abi, pw_r, pw_i)
    lf_r, lf_i = _ssm_bwd(*bwd_args, zero_fin, zero_fin, bb_r, bb_i, bbt_r, bbt_i, cct_r, cct_i, dskip, True)
    du3, dab_r, dab_i, dbb_r, dbb_i, dcc_r, dcc_i, g_d = _ssm_bwd(*bwd_args, lf_r, lf_i, bb_r, bb_i, bbt_r, bbt_i,
                                                                 cct_r, cct_i, dskip, False)
    g_a_re3, g_a_im3, g_ldt3, g_b_re_t, g_b_im_t = _ssm_discretize_bwd(
        a_re3, a_im3, ldt3, b_re_t, b_im_t, dab_r.reshape(a_re3.shape), dab_i.reshape(a_re3.shape),
        _block_diag_of(dbb_r, True), _block_diag_of(dbb_i, True))
    g_c_re, g_c_im = _block_diag_of(dcc_r, False), _block_diag_of(dcc_i, False)

    dproj, g_gq, g_gk = _qk_bwd(dqn, dkn, qk, dv, du3.reshape(s, SSM_W), gq, gk, gmat)
    grad_x, g_norm1 = _in_bwd(dproj, w_in_t, xs, norm1_g, dx2)
    g_w_in_t = _mm_tn(dproj, xn, "grad_w_in_t")

    small_grads = dict(
        norm1_g=g_norm1, q_norm_g=g_gq.reshape(ATTN_W // HEAD_DIM, HEAD_DIM).sum(0)[None],
        k_norm_g=g_gk.reshape(ATTN_W // HEAD_DIM, HEAD_DIM).sum(0)[None],
        ssm_a_re=g_a_re3.reshape(ssm_a_re.shape), ssm_a_im=g_a_im3.reshape(ssm_a_im.shape),
        ssm_log_dt=g_ldt3.reshape(ssm_log_dt.shape), ssm_b_re=jnp.swapaxes(g_b_re_t, 1, 2)[None],
        ssm_b_im=jnp.swapaxes(g_b_im_t, 1, 2)[None], ssm_c_re=g_c_re[None], ssm_c_im=g_c_im[None],
        ssm_d=g_d.reshape(ssm_d.shape), glu_b=g_glu_b, attn_out_norm_g=g_ga, ssm_out_norm_g=g_gs, norm2_g=g_norm2)

    big = ("w_in", "glu_w", "w_out", "w_mlp_up", "w_mlp_down")
    big_grads = [g_w_in_t, g_glu_w, g_w_out, g_w_up_t, g_w_down]
    parts = _grad_exchange(big_grads, _pack_small([small_grads[n] for n in SMALL]))
    transposed = {"w_in", "w_mlp_up"}
    res = {}
    for name, gp in zip(big, parts[:-1]):
        tr = (lambda a: a.T) if name in transposed else (lambda a: a)
        outs = _adamw(tr(weights[name][0]), tr(mom_m[name][0]), tr(mom_v[name][0]), gp, "adamw_" + name)
        res[name] = [tr(o)[None] for o in outs]
    outs = _adamw(_pack_small([weights[n] for n in SMALL]), _pack_small([mom_m[n] for n in SMALL]),
                  _pack_small([mom_v[n] for n in SMALL]), parts[-1], "adamw_small")
    shapes = [weights[n].shape for n in SMALL]
    unpacked = [_unpack_small(o, shapes) for o in outs]
    for i, n in enumerate(SMALL):
        res[n] = [unpacked[j][i] for j in range(4)]

    return (loss, grad_x[None], *[res[n][0] for n in order], *[res[n][1] for n in order],
            *[res[n][2] for n in order], *[res[n][3] for n in order])
```

```python
import math

import jax
import jax.numpy as jnp
from jax import lax
from jax.experimental import pallas as pl
from jax.experimental.pallas import tpu as pltpu

F32 = jnp.float32
BF16 = jnp.bfloat16

D_MODEL = 1024
ATTN_W = 512
HEAD_DIM = 64
SSM_W = 512
SSM_GROUP = 16
SSM_GROUPS = 32
SSM_STATE = 64
N_STATE = SSM_GROUPS * SSM_STATE
D_FF = 4096
EPS = 1e-6
NEG_INF = -1e30
ATTN_CHUNK = 2048
ATTN_BLOCK = 128
DILATIONS = (1, 4, 16)
N_SEG = 8
SSM_LK = 64
N_DEV = 8
LANES = 128

ADAM_LR = 0.001
ADAM_B1 = 0.9
ADAM_B2 = 0.999
ADAM_EPS = 1e-08
ADAM_WD = 0.01
ADAM_STEP = 10

VMEM_LIMIT = 56 * 1024 * 1024
GELU_C = math.sqrt(2.0 / math.pi)
MESH = pl.DeviceIdType.MESH


def _cp(sem, vmem=VMEM_LIMIT):
    return pltpu.CompilerParams(dimension_semantics=sem, vmem_limit_bytes=vmem)


def _dot(a, b):
    return jnp.dot(a, b, preferred_element_type=F32)


def _dot_nt(a, b):
    return lax.dot_general(a, b, (((1,), (1,)), ((), ())), preferred_element_type=F32)


def _dot_tn(a, b):
    return lax.dot_general(a, b, (((0,), (0,)), ((), ())), preferred_element_type=F32)


def _group_mean(x2, gmat):
    hi = x2.astype(BF16)
    lo = (x2 - hi.astype(F32)).astype(BF16)
    return _dot(hi, gmat) + _dot(lo, gmat)


def _rms(x):
    return lax.rsqrt(jnp.mean(x * x, axis=-1, keepdims=True) + EPS)


def _rms_bwd(dy, x, g, n):
    r = _rms(x)
    gdy = dy * g
    dx = r * gdy - x * (r * r * r) * (jnp.sum(gdy * x, axis=-1, keepdims=True) / n)
    return dx, dy * (x * r)


def _gelu(y):
    t = jnp.tanh(GELU_C * (y + 0.044715 * (y * y * y)))
    return 0.5 * y * (1.0 + t), t


def _full(shape):
    nd = len(shape)
    return pl.BlockSpec(shape, lambda *_: (0,) * nd)


def _sds(shape, dtype=F32):
    return jax.ShapeDtypeStruct(shape, dtype)


def _fwd_proj(x, g1, w_in, gq, gk, gmat, tm=512):
    s = x.shape[0]

    def body(x_ref, g1_ref, w_ref, gq_ref, gk_ref, gm_ref, qk_ref, qn_ref, kn_ref, v_ref, u_ref, xn_ref):
        xv = x_ref[...]
        xnb = ((xv * _rms(xv)) * g1_ref[...]).astype(BF16)
        xn_ref[...] = xnb
        proj = _dot(xnb, w_ref[...])
        q = proj[:, :ATTN_W]
        k = proj[:, ATTN_W:2 * ATTN_W]
        qk_ref[...] = proj[:, :2 * ATTN_W]
        v_ref[...] = proj[:, 2 * ATTN_W:3 * ATTN_W]
        u_ref[...] = proj[:, 3 * ATTN_W:]
        gm = gm_ref[...]
        qn_ref[...] = (q * lax.rsqrt(_group_mean(q * q, gm) + EPS)) * gq_ref[...]
        kn_ref[...] = (k * lax.rsqrt(_group_mean(k * k, gm) + EPS)) * gk_ref[...]

    row = lambda w: pl.BlockSpec((tm, w), lambda i: (i, 0))
    return pl.pallas_call(
        body, name="fwd_proj", grid=(s // tm,),
        in_specs=[row(D_MODEL), _full((1, D_MODEL)), _full((D_MODEL, 4 * ATTN_W)), _full((1, ATTN_W)),
                  _full((1, ATTN_W)), _full((ATTN_W, ATTN_W))],
        out_specs=[row(2 * ATTN_W), row(ATTN_W), row(ATTN_W), row(ATTN_W), row(ATTN_W), row(D_MODEL)],
        out_shape=[_sds((s, 2 * ATTN_W)), _sds((s, ATTN_W)), _sds((s, ATTN_W)), _sds((s, ATTN_W)),
                   _sds((s, ATTN_W)), _sds((s, D_MODEL), BF16)],
        compiler_params=_cp(("parallel",)),
    )(x, g1, w_in, gq, gk, gmat)


def _attn_rows(t, d, nb):
    if d == 1:
        q0 = pl.multiple_of(t * ATTN_BLOCK, ATTN_BLOCK)
        return (t, pl.ds(q0, ATTN_BLOCK), pl.ds(pl.multiple_of(ATTN_CHUNK + q0, ATTN_BLOCK), ATTN_BLOCK),
                pl.ds(pl.multiple_of(ATTN_CHUNK - ATTN_BLOCK + q0, ATTN_BLOCK), ATTN_BLOCK))
    r = t // nb
    b = t % nb
    return (b, pl.ds(ATTN_BLOCK * b * d + r, ATTN_BLOCK, stride=d),
            pl.ds(ATTN_CHUNK + ATTN_BLOCK * b * d + r, ATTN_BLOCK, stride=d),
            pl.ds(ATTN_CHUNK + ATTN_BLOCK * (b - 1) * d + r, ATTN_BLOCK, stride=d))


def _attn_masks():
    row = lax.broadcasted_iota(jnp.int32, (ATTN_BLOCK, LANES), 0)
    col = lax.broadcasted_iota(jnp.int32, (ATTN_BLOCK, LANES), 1)
    return row, col


NBLK = ATTN_CHUNK // ATTN_BLOCK


def _attn_bias(bias_s):
    row, col = _attn_masks()
    bias_s[:, pl.ds(0, LANES)] = jnp.where(col >= row, 0.0, NEG_INF)
    bias_s[:, pl.ds(LANES, LANES)] = jnp.where(col <= row, 0.0, NEG_INF)
    return col < HEAD_DIM


def _attn_fwd(qn, kn, v, unroll=8):
    s = qn.shape[0]
    nch = s // ATTN_CHUNK
    scale = HEAD_DIM ** -0.5
    npat = len(DILATIONS)

    def body(q_ref, kp_ref, kc_ref, vp_ref, vc_ref, o_ref, lse_ref, kk, vv, kt_s, vb_s, bias_s, m_s, a_s):
        i = pl.program_id(1)
        kk[pl.ds(0, ATTN_CHUNK), :] = kp_ref[...]
        kk[pl.ds(ATTN_CHUNK, ATTN_CHUNK), :] = kc_ref[...]
        vv[pl.ds(0, ATTN_CHUNK), :] = vp_ref[...]
        vv[pl.ds(ATTN_CHUNK, ATTN_CHUNK), :] = vc_ref[...]
        head0 = _attn_bias(bias_s)

        for p, d in enumerate(DILATIONS):
            nb = ATTN_CHUNK // (ATTN_BLOCK * d)

            def prep(t, carry, d=d, nb=nb):
                _, _, crows, prows = _attn_rows(t, d, nb)
                kt_s[t, :, pl.ds(0, LANES)] = kk[prows, :].T.astype(BF16)
                kt_s[t, :, pl.ds(LANES, LANES)] = kk[crows, :].T.astype(BF16)
                vp = vv[prows, :]
                vc = vv[crows, :]
                vb_s[2 * t, pl.ds(0, ATTN_BLOCK), :] = jnp.where(head0, vp, 1.0).astype(BF16)
                vb_s[2 * t, pl.ds(ATTN_BLOCK, ATTN_BLOCK), :] = jnp.where(head0, vc, 1.0).astype(BF16)
                vb_s[2 * t + 1, pl.ds(0, ATTN_BLOCK), :] = jnp.where(head0, 1.0, vp).astype(BF16)
                vb_s[2 * t + 1, pl.ds(ATTN_BLOCK, ATTN_BLOCK), :] = jnp.where(head0, 1.0, vc).astype(BF16)
                return carry

            lax.fori_loop(0, NBLK, prep, 0, unroll=4)

            def blk(t, carry, p=p, d=d, nb=nb):
                b, qrows, _, _ = _attn_rows(t, d, nb)
                q = q_ref[qrows, :]
                pen = jnp.where(jnp.logical_or(i > 0, b > 0), 0.0, NEG_INF)
                kt = kt_s[t]
                outs, ms = [], []
                for h in range(2):
                    hm = head0 if h == 0 else jnp.logical_not(head0)
                    qh = jnp.where(hm, q, 0.0).astype(BF16)
                    sc = _dot(qh, kt) * scale + bias_s[...]
                    s_p = sc[:, :LANES] + pen
                    s_c = sc[:, LANES:]
                    m = jnp.max(jnp.maximum(s_p, s_c), axis=-1, keepdims=True)
                    e_b = jnp.concatenate([jnp.exp(s_p - m), jnp.exp(s_c - m)], axis=1).astype(BF16)
                    outs.append(_dot(e_b, vb_s[2 * t + h]))
                    ms.append(jnp.broadcast_to(m, (ATTN_BLOCK, LANES)))
                m_s[p, qrows, :] = jnp.where(head0, ms[0], ms[1])
                a_s[2 * p, qrows, :] = outs[0]
                a_s[2 * p + 1, qrows, :] = outs[1]
                return carry

            lax.fori_loop(0, NBLK, blk, 0, unroll=unroll)

        def merge(t, carry):
            rows = pl.ds(pl.multiple_of(t * ATTN_BLOCK, ATTN_BLOCK), ATTN_BLOCK)
            m_all = m_s[0, rows, :]
            for p in range(1, npat):
                m_all = jnp.maximum(m_all, m_s[p, rows, :])
            num = jnp.zeros((ATTN_BLOCK, LANES), F32)
            den = jnp.zeros((ATTN_BLOCK, LANES), F32)
            for p in range(npat):
                w = jnp.exp(m_s[p, rows, :] - m_all)
                a0, a1 = a_s[2 * p, rows, :], a_s[2 * p + 1, rows, :]
                num = num + jnp.where(head0, a0, a1) * w
                den = den + pltpu.roll(jnp.where(head0, a1, a0), HEAD_DIM, 1) * w
            o_ref[rows, :] = num / den
            lse_ref[rows, :] = m_all + jnp.log(den)
            return carry

        lax.fori_loop(0, NBLK, merge, 0, unroll=2)

    cur = pl.BlockSpec((ATTN_CHUNK, LANES), lambda h, i: (i, h))
    prev = pl.BlockSpec((ATTN_CHUNK, LANES), lambda h, i: (jnp.maximum(i - 1, 0), h))
    return pl.pallas_call(
        body, name="attn_fwd", grid=(ATTN_W // LANES, nch),
        in_specs=[cur, prev, cur, prev, cur],
        out_specs=[cur, cur],
        out_shape=[_sds((s, ATTN_W)), _sds((s, ATTN_W))],
        scratch_shapes=[pltpu.VMEM((2 * ATTN_CHUNK, LANES), F32), pltpu.VMEM((2 * ATTN_CHUNK, LANES), F32),
                        pltpu.VMEM((NBLK, LANES, 2 * LANES), BF16), pltpu.VMEM((2 * NBLK, 2 * ATTN_BLOCK, LANES), BF16),
                        pltpu.VMEM((ATTN_BLOCK, 2 * LANES), F32),
                        pltpu.VMEM((npat, ATTN_CHUNK, LANES), F32), pltpu.VMEM((2 * npat, ATTN_CHUNK, LANES), F32)],
        compiler_params=_cp(("parallel", "parallel")),
    )(qn, kn, kn, v, v)


def _attn_bwd(qn, kn, v, o, lse, do, group=4):
    s = qn.shape[0]
    nch = s // ATTN_CHUNK
    scale = HEAD_DIM ** -0.5
    npat = len(DILATIONS)

    def body(q_ref, kp_ref, kc_ref, vp_ref, vc_ref, o_ref, lse_ref, do_ref, dq_ref, dk_ref, dv_ref,
             kk, vv, dkk, dvv, kt_s, vt_s, kn_s, bias_s, dq_s, dl_s, dkb, dvb):
        step = pl.program_id(1)
        i = nch - 1 - step
        kk[pl.ds(0, ATTN_CHUNK), :] = kp_ref[...]
        kk[pl.ds(ATTN_CHUNK, ATTN_CHUNK), :] = kc_ref[...]
        vv[pl.ds(0, ATTN_CHUNK), :] = vp_ref[...]
        vv[pl.ds(ATTN_CHUNK, ATTN_CHUNK), :] = vc_ref[...]

        @pl.when(step == 0)
        def _():
            dkk[pl.ds(ATTN_CHUNK, ATTN_CHUNK), :] = jnp.zeros((ATTN_CHUNK, LANES), F32)
            dvv[pl.ds(ATTN_CHUNK, ATTN_CHUNK), :] = jnp.zeros((ATTN_CHUNK, LANES), F32)

        @pl.when(step > 0)
        def _():
            dkk[pl.ds(ATTN_CHUNK, ATTN_CHUNK), :] = dkk[pl.ds(0, ATTN_CHUNK), :]
            dvv[pl.ds(ATTN_CHUNK, ATTN_CHUNK), :] = dvv[pl.ds(0, ATTN_CHUNK), :]

        dkk[pl.ds(0, ATTN_CHUNK), :] = jnp.zeros((ATTN_CHUNK, LANES), F32)
        dvv[pl.ds(0, ATTN_CHUNK), :] = jnp.zeros((ATTN_CHUNK, LANES), F32)
        head0 = _attn_bias(bias_s)

        def delta(t, carry):
            rows = pl.ds(pl.multiple_of(t * ATTN_BLOCK, ATTN_BLOCK), ATTN_BLOCK)
            prod = do_ref[rows, :] * o_ref[rows, :]
            d0 = jnp.sum(jnp.where(head0, prod, 0.0), axis=-1, keepdims=True)
            d1 = jnp.sum(jnp.where(head0, 0.0, prod), axis=-1, keepdims=True)
            dl_s[rows, :] = jnp.where(head0, d0, d1)
            return carry

        lax.fori_loop(0, NBLK, delta, 0, unroll=2)

        for p, d in enumerate(DILATIONS):
            nb = ATTN_CHUNK // (ATTN_BLOCK * d)

            def prep(t, carry, d=d, nb=nb):
                _, _, crows, prows = _attn_rows(t, d, nb)
                kp, kc = kk[prows, :], kk[crows, :]
                kt_s[t, :, pl.ds(0, LANES)] = kp.T.astype(BF16)
                kt_s[t, :, pl.ds(LANES, LANES)] = kc.T.astype(BF16)
                kn_s[t, pl.ds(0, ATTN_BLOCK), :] = kp.astype(BF16)
                kn_s[t, pl.ds(ATTN_BLOCK, ATTN_BLOCK), :] = kc.astype(BF16)
                vt_s[t, :, pl.ds(0, LANES)] = vv[prows, :].T.astype(BF16)
                vt_s[t, :, pl.ds(LANES, LANES)] = vv[crows, :].T.astype(BF16)
                return carry

            lax.fori_loop(0, NBLK, prep, 0, unroll=4)

            def blk(tg, carry, p=p, d=d, nb=nb):
                st = []
                for g in range(group):
                    t = tg * group + g
                    b, qrows, _, _ = _attn_rows(t, d, nb)
                    q = q_ref[qrows, :]
                    dout = do_ref[qrows, :]
                    lse_b = lse_ref[qrows, :]
                    dl_b = dl_s[qrows, :]
                    pen = jnp.where(jnp.logical_or(i > 0, b > 0), 0.0, NEG_INF)
                    for h in range(2):
                        hm = head0 if h == 0 else jnp.logical_not(head0)
                        c0 = h * HEAD_DIM
                        qh = jnp.where(hm, q, 0.0).astype(BF16)
                        doh = jnp.where(hm, dout, 0.0).astype(BF16)
                        st.append(dict(t=t, qrows=qrows, qh=qh, doh=doh, pen=pen, lse=lse_b[:, c0:c0 + 1],
                                       dl=dl_b[:, c0:c0 + 1], sc=_dot(qh, kt_s[t]), dp=_dot(doh, vt_s[t])))
                for e in st:
                    sc = e["sc"] * scale + bias_s[...]
                    sc = jnp.concatenate([sc[:, :LANES] + e["pen"], sc[:, LANES:]], axis=1)
                    pr = jnp.exp(sc - e["lse"])
                    e["ds"] = (pr * (e["dp"] - e["dl"]) * scale).astype(BF16)
                    e["pr"] = pr.astype(BF16)
                for g in range(group):
                    e0, e1 = st[2 * g], st[2 * g + 1]
                    t = e0["t"]
                    dq_s[p, e0["qrows"], :] = jnp.where(head0, _dot(e0["ds"], kn_s[t]), _dot(e1["ds"], kn_s[t]))
                    dkb[t] = _dot_tn(e0["ds"], e0["qh"]) + _dot_tn(e1["ds"], e1["qh"])
                    dvb[t] = _dot_tn(e0["pr"], e0["doh"]) + _dot_tn(e1["pr"], e1["doh"])
                return carry

            lax.fori_loop(0, NBLK // group, blk, 0)

            def scatter(t, carry, d=d, nb=nb):
                _, _, crows, prows = _attn_rows(t, d, nb)
                dkk[prows, :] = dkk[prows, :] + dkb[t, pl.ds(0, ATTN_BLOCK), :]
                dkk[crows, :] = dkk[crows, :] + dkb[t, pl.ds(ATTN_BLOCK, ATTN_BLOCK), :]
                dvv[prows, :] = dvv[prows, :] + dvb[t, pl.ds(0, ATTN_BLOCK), :]
                dvv[crows, :] = dvv[crows, :] + dvb[t, pl.ds(ATTN_BLOCK, ATTN_BLOCK), :]
                return carry

            lax.fori_loop(0, NBLK, scatter, 0)

        def finish(t, carry):
            rows = pl.ds(pl.multiple_of(t * ATTN_BLOCK, ATTN_BLOCK), ATTN_BLOCK)
            acc = dq_s[0, rows, :]
            for p in range(1, npat):
                acc = acc + dq_s[p, rows, :]
            dq_ref[rows, :] = acc
            return carry

        lax.fori_loop(0, NBLK, finish, 0, unroll=2)
        dk_ref[...] = dkk[pl.ds(ATTN_CHUNK, ATTN_CHUNK), :]
        dv_ref[...] = dvv[pl.ds(ATTN_CHUNK, ATTN_CHUNK), :]

    cur = pl.BlockSpec((ATTN_CHUNK, LANES), lambda h, t: (nch - 1 - t, h))
    prev = pl.BlockSpec((ATTN_CHUNK, LANES), lambda h, t: (jnp.maximum(nch - 2 - t, 0), h))
    big = pltpu.VMEM((2 * ATTN_CHUNK, LANES), F32)
    pair_t = pltpu.VMEM((NBLK, LANES, 2 * LANES), BF16)
    return pl.pallas_call(
        body, name="attn_bwd", grid=(ATTN_W // LANES, nch),
        in_specs=[cur, prev, cur, prev, cur, cur, cur, cur],
        out_specs=[cur, cur, cur],
        out_shape=[_sds((s, ATTN_W))] * 3,
        scratch_shapes=[big, big, big, big, pair_t, pair_t, pltpu.VMEM((NBLK, 2 * ATTN_BLOCK, LANES), BF16),
                        pltpu.VMEM((ATTN_BLOCK, 2 * LANES), F32),
                        pltpu.VMEM((npat, ATTN_CHUNK, LANES), F32), pltpu.VMEM((ATTN_CHUNK, LANES), F32),
                        pltpu.VMEM((NBLK, 2 * ATTN_BLOCK, LANES), F32), pltpu.VMEM((NBLK, 2 * ATTN_BLOCK, LANES), F32)],
        compiler_params=_cp(("parallel", "arbitrary")),
    )(qn, kn, kn, v, v, o, lse, do)


def _attn_fwd_v1(qn, kn, v):
    s = qn.shape[0]
    nch = s // ATTN_CHUNK
    scale = HEAD_DIM ** -0.5
    npat = len(DILATIONS)

    def body(q_ref, kp_ref, kc_ref, vp_ref, vc_ref, o_ref, lse_ref, kk, vv, m_s, l_s, acc_s):
        i = pl.program_id(1)
        kk[pl.ds(0, ATTN_CHUNK), :] = kp_ref[...]
        kk[pl.ds(ATTN_CHUNK, ATTN_CHUNK), :] = kc_ref[...]
        vv[pl.ds(0, ATTN_CHUNK), :] = vp_ref[...]
        vv[pl.ds(ATTN_CHUNK, ATTN_CHUNK), :] = vc_ref[...]
        row, col = _attn_masks()
        head0 = col < HEAD_DIM
        mask_cur = col <= row
        diff = col - row

        for p, d in enumerate(DILATIONS):
            nb = ATTN_CHUNK // (ATTN_BLOCK * d)

            def blk(t, carry, p=p, d=d, nb=nb):
                b, qrows, crows, prows = _attn_rows(t, d, nb)
                q = q_ref[qrows, :]
                kc = kk[crows, :].astype(BF16)
                kp = kk[prows, :].astype(BF16)
                vc = vv[crows, :].astype(BF16)
                vp = vv[prows, :].astype(BF16)
                thr = jnp.where(jnp.logical_or(i > 0, b > 0), 0, 4 * ATTN_BLOCK)
                mask_prev = diff >= thr
                accs, ms, ls = [], [], []
                for h in range(2):
                    hm = head0 if h == 0 else jnp.logical_not(head0)
                    qh = jnp.where(hm, q, 0.0).astype(BF16)
                    s_p = jnp.where(mask_prev, _dot_nt(qh, kp) * scale, NEG_INF)
                    s_c = jnp.where(mask_cur, _dot_nt(qh, kc) * scale, NEG_INF)
                    m = jnp.maximum(jnp.max(s_p, axis=-1, keepdims=True), jnp.max(s_c, axis=-1, keepdims=True))
                    e_p = jnp.exp(s_p - m)
                    e_c = jnp.exp(s_c - m)
                    l = jnp.sum(e_p, axis=-1, keepdims=True) + jnp.sum(e_c, axis=-1, keepdims=True)
                    accs.append(_dot(e_p.astype(BF16), vp) + _dot(e_c.astype(BF16), vc))
                    ms.append(jnp.broadcast_to(m, (ATTN_BLOCK, LANES)))
                    ls.append(jnp.broadcast_to(l, (ATTN_BLOCK, LANES)))
                m_s[p, qrows, :] = jnp.where(head0, ms[0], ms[1])
                l_s[p, qrows, :] = jnp.where(head0, ls[0], ls[1])
                acc_s[p, qrows, :] = jnp.where(head0, accs[0], accs[1])
                return carry

            lax.fori_loop(0, ATTN_CHUNK // ATTN_BLOCK, blk, 0)

        def merge(t, carry):
            rows = pl.ds(pl.multiple_of(t * ATTN_BLOCK, ATTN_BLOCK), ATTN_BLOCK)
            m_all = m_s[0, rows, :]
            for p in range(1, npat):
                m_all = jnp.maximum(m_all, m_s[p, rows, :])
            num = jnp.zeros((ATTN_BLOCK, LANES), F32)
            den = jnp.zeros((ATTN_BLOCK, LANES), F32)
            for p in range(npat):
                w = jnp.exp(m_s[p, rows, :] - m_all)
                num = num + acc_s[p, rows, :] * w
                den = den + l_s[p, rows, :] * w
            o_ref[rows, :] = num / den
            lse_ref[rows, :] = m_all + jnp.log(den)
            return carry

        lax.fori_loop(0, ATTN_CHUNK // ATTN_BLOCK, merge, 0)

    cur = pl.BlockSpec((ATTN_CHUNK, LANES), lambda h, i: (i, h))
    prev = pl.BlockSpec((ATTN_CHUNK, LANES), lambda h, i: (jnp.maximum(i - 1, 0), h))
    return pl.pallas_call(
        body, name="attn_fwd", grid=(ATTN_W // LANES, nch),
        in_specs=[cur, prev, cur, prev, cur],
        out_specs=[cur, cur],
        out_shape=[_sds((s, ATTN_W)), _sds((s, ATTN_W))],
        scratch_shapes=[pltpu.VMEM((2 * ATTN_CHUNK, LANES), F32), pltpu.VMEM((2 * ATTN_CHUNK, LANES), F32),
                        pltpu.VMEM((npat, ATTN_CHUNK, LANES), F32), pltpu.VMEM((npat, ATTN_CHUNK, LANES), F32),
                        pltpu.VMEM((npat, ATTN_CHUNK, LANES), F32)],
        compiler_params=_cp(("parallel", "parallel")),
    )(qn, kn, kn, v, v)


def _attn_bwd_v1(qn, kn, v, o, lse, do):
    s = qn.shape[0]
    nch = s // ATTN_CHUNK
    scale = HEAD_DIM ** -0.5
    npat = len(DILATIONS)

    def body(q_ref, kp_ref, kc_ref, vp_ref, vc_ref, o_ref, lse_ref, do_ref, dq_ref, dk_ref, dv_ref,
             kk, vv, dkk, dvv, dq_s, dl_s):
        step = pl.program_id(1)
        i = nch - 1 - step
        kk[pl.ds(0, ATTN_CHUNK), :] = kp_ref[...]
        kk[pl.ds(ATTN_CHUNK, ATTN_CHUNK), :] = kc_ref[...]
        vv[pl.ds(0, ATTN_CHUNK), :] = vp_ref[...]
        vv[pl.ds(ATTN_CHUNK, ATTN_CHUNK), :] = vc_ref[...]

        @pl.when(step == 0)
        def _():
            dkk[pl.ds(ATTN_CHUNK, ATTN_CHUNK), :] = jnp.zeros((ATTN_CHUNK, LANES), F32)
            dvv[pl.ds(ATTN_CHUNK, ATTN_CHUNK), :] = jnp.zeros((ATTN_CHUNK, LANES), F32)

        @pl.when(step > 0)
        def _():
            dkk[pl.ds(ATTN_CHUNK, ATTN_CHUNK), :] = dkk[pl.ds(0, ATTN_CHUNK), :]
            dvv[pl.ds(ATTN_CHUNK, ATTN_CHUNK), :] = dvv[pl.ds(0, ATTN_CHUNK), :]

        dkk[pl.ds(0, ATTN_CHUNK), :] = jnp.zeros((ATTN_CHUNK, LANES), F32)
        dvv[pl.ds(0, ATTN_CHUNK), :] = jnp.zeros((ATTN_CHUNK, LANES), F32)

        row, col = _attn_masks()
        head0 = col < HEAD_DIM
        mask_cur = col <= row
        diff = col - row

        def delta(t, carry):
            rows = pl.ds(pl.multiple_of(t * ATTN_BLOCK, ATTN_BLOCK), ATTN_BLOCK)
            prod = do_ref[rows, :] * o_ref[rows, :]
            d0 = jnp.sum(jnp.where(head0, prod, 0.0), axis=-1, keepdims=True)
            d1 = jnp.sum(jnp.where(head0, 0.0, prod), axis=-1, keepdims=True)
            dl_s[rows, :] = jnp.where(head0, d0, d1)
            return carry

        lax.fori_loop(0, ATTN_CHUNK // ATTN_BLOCK, delta, 0)

        for p, d in enumerate(DILATIONS):
            nb = ATTN_CHUNK // (ATTN_BLOCK * d)

            def blk(t, carry, p=p, d=d, nb=nb):
                b, qrows, crows, prows = _attn_rows(t, d, nb)
                q = q_ref[qrows, :]
                dout = do_ref[qrows, :]
                lse_b = lse_ref[qrows, :]
                dl_b = dl_s[qrows, :]
                kc = kk[crows, :].astype(BF16)
                kp = kk[prows, :].astype(BF16)
                vc = vv[crows, :].astype(BF16)
                vp = vv[prows, :].astype(BF16)
                thr = jnp.where(jnp.logical_or(i > 0, b > 0), 0, 4 * ATTN_BLOCK)
                mask_prev = diff >= thr
                dqs = []
                dk_p = jnp.zeros((ATTN_BLOCK, LANES), F32)
                dk_c = jnp.zeros((ATTN_BLOCK, LANES), F32)
                dv_p = jnp.zeros((ATTN_BLOCK, LANES), F32)
                dv_c = jnp.zeros((ATTN_BLOCK, LANES), F32)
                for h in range(2):
                    hm = head0 if h == 0 else jnp.logical_not(head0)
                    c0 = h * HEAD_DIM
                    qh = jnp.where(hm, q, 0.0).astype(BF16)
                    doh = jnp.where(hm, dout, 0.0).astype(BF16)
                    lse_h = lse_b[:, c0:c0 + 1]
                    dl_h = dl_b[:, c0:c0 + 1]
                    s_p = jnp.where(mask_prev, _dot_nt(qh, kp) * scale, NEG_INF)
                    s_c = jnp.where(mask_cur, _dot_nt(qh, kc) * scale, NEG_INF)
                    p_p = jnp.exp(s_p - lse_h)
                    p_c = jnp.exp(s_c - lse_h)
                    ds_p = (p_p * (_dot_nt(doh, vp) - dl_h) * scale).astype(BF16)
                    ds_c = (p_c * (_dot_nt(doh, vc) - dl_h) * scale).astype(BF16)
                    dqs.append(_dot(ds_p, kp) + _dot(ds_c, kc))
                    dk_p = dk_p + _dot_tn(ds_p, qh)
                    dk_c = dk_c + _dot_tn(ds_c, qh)
                    dv_p = dv_p + _dot_tn(p_p.astype(BF16), doh)
                    dv_c = dv_c + _dot_tn(p_c.astype(BF16), doh)
                dq_s[p, qrows, :] = jnp.where(head0, dqs[0], dqs[1])
                dkk[prows, :] = dkk[prows, :] + dk_p
                dkk[crows, :] = dkk[crows, :] + dk_c
                dvv[prows, :] = dvv[prows, :] + dv_p
                dvv[crows, :] = dvv[crows, :] + dv_c
                return carry

            lax.fori_loop(0, ATTN_CHUNK // ATTN_BLOCK, blk, 0)

        def finish(t, carry):
            rows = pl.ds(pl.multiple_of(t * ATTN_BLOCK, ATTN_BLOCK), ATTN_BLOCK)
            acc = dq_s[0, rows, :]
            for p in range(1, npat):
                acc = acc + dq_s[p, rows, :]
            dq_ref[rows, :] = acc
            return carry

        lax.fori_loop(0, ATTN_CHUNK // ATTN_BLOCK, finish, 0)
        dk_ref[...] = dkk[pl.ds(ATTN_CHUNK, ATTN_CHUNK), :]
        dv_ref[...] = dvv[pl.ds(ATTN_CHUNK, ATTN_CHUNK), :]

    cur = pl.BlockSpec((ATTN_CHUNK, LANES), lambda h, t: (nch - 1 - t, h))
    prev = pl.BlockSpec((ATTN_CHUNK, LANES), lambda h, t: (jnp.maximum(nch - 2 - t, 0), h))
    big = pltpu.VMEM((2 * ATTN_CHUNK, LANES), F32)
    return pl.pallas_call(
        body, name="attn_bwd", grid=(ATTN_W // LANES, nch),
        in_specs=[cur, prev, cur, prev, cur, cur, cur, cur],
        out_specs=[cur, cur, cur],
        out_shape=[_sds((s, ATTN_W))] * 3,
        scratch_shapes=[big, big, big, big, pltpu.VMEM((npat, ATTN_CHUNK, LANES), F32),
                        pltpu.VMEM((ATTN_CHUNK, LANES), F32)],
        compiler_params=_cp(("parallel", "arbitrary")),
    )(qn, kn, kn, v, v, o, lse, do)


def _discretize(lr, li, dt):
    mag = jnp.exp(lr * dt)
    abr = mag * jnp.cos(li * dt)
    abi = mag * jnp.sin(li * dt)
    den = lr * lr + li * li
    nr, ni = abr - 1.0, abi
    cr = (nr * lr + ni * li) / den
    ci = (ni * lr - nr * li) / den
    return abr, abi, den, nr, ni, cr, ci


def _ssm_discretize(a_re, a_im, log_dt, b_re_t, b_im_t):
    def body(ar_ref, ai_ref, ldt_ref, br_ref, bi_ref, abr_ref, abi_ref, bbr_ref, bbi_ref):
        abr, abi, _, _, _, cr, ci = _discretize(ar_ref[...], ai_ref[...], jnp.exp(ldt_ref[...]))
        br, bi = br_ref[...], bi_ref[...]
        abr_ref[...] = abr
        abi_ref[...] = abi
        bbr_ref[...] = cr * br - ci * bi
        bbi_ref[...] = cr * bi + ci * br

    return pl.pallas_call(
        body, name="ssm_discretize",
        out_shape=[_sds(a_re.shape)] * 2 + [_sds(b_re_t.shape)] * 2,
    )(a_re, a_im, log_dt, b_re_t, b_im_t)


def _ssm_discretize_bwd(a_re, a_im, log_dt, b_re_t, b_im_t, dabr, dabi, dbbr, dbbi):
    def body(ar_ref, ai_ref, ldt_ref, br_ref, bi_ref, dabr_ref, dabi_ref, dbbr_ref, dbbi_ref,
             dar_ref, dai_ref, dldt_ref, dbr_ref, dbi_ref):
        lr, li = ar_ref[...], ai_ref[...]
        dt = jnp.exp(ldt_ref[...])
        abr, abi, den, nr, ni, cr, ci = _discretize(lr, li, dt)
        br, bi = br_ref[...], bi_ref[...]
        gbr, gbi = dbbr_ref[...], dbbi_ref[...]
        dcr = jnp.sum(gbr * br + gbi * bi, axis=1, keepdims=True)
        dci = jnp.sum(gbi * br - gbr * bi, axis=1, keepdims=True)
        dbr_ref[...] = cr * gbr + ci * gbi
        dbi_ref[...] = cr * gbi - ci * gbr
        dnr = (dcr * lr - dci * li) / den
        dni = (dcr * li + dci * lr) / den
        dden = -(dcr * cr + dci * ci) / den
        dlr = (dcr * nr + dci * ni) / den + dden * 2.0 * lr
        dli = (dcr * ni - dci * nr) / den + dden * 2.0 * li
        gabr = dabr_ref[...] + dnr
        gabi = dabi_ref[...] + dni
        dphi = gabr * abr + gabi * abi
        dth = gabi * abr - gabr * abi
        dar_ref[...] = dlr + dphi * dt
        dai_ref[...] = dli + dth * dt
        dldt_ref[...] = jnp.sum(dphi * lr + dth * li, axis=2, keepdims=True) * dt

    return pl.pallas_call(
        body, name="ssm_discretize_bwd",
        out_shape=[_sds(a_re.shape)] * 2 + [_sds(log_dt.shape)] + [_sds(b_re_t.shape)] * 2,
    )(a_re, a_im, log_dt, b_re_t, b_im_t, dabr, dabi, dbbr, dbbi)


def _ssm_power(abr, abi, n_sq):
    def body(r_ref, i_ref, or_ref, oi_ref):
        r, i = r_ref[...], i_ref[...]
        for _ in range(n_sq):
            r, i = r * r - i * i, 2.0 * r * i
        or_ref[...] = r
        oi_ref[...] = i

    return pl.pallas_call(body, name="ssm_power", out_shape=[_sds(abr.shape)] * 2)(abr, abi)


N_CB = SSM_W // LANES
CB_STATES = N_STATE // N_CB
ROWS = N_SEG * SSM_LK


class _Neg:
    def __init__(self, ref):
        self.ref = ref

    def __getitem__(self, idx):
        return -self.ref[idx]


def _seg_init(fin_r, fin_i, pw_r, pw_i, x_r, x_i, reverse):
    zero = jnp.zeros((1, N_STATE), F32)
    cr, ci = zero, zero
    order = range(N_SEG - 1, -1, -1) if reverse else range(N_SEG)
    pr = pw_r[...]
    pi = -pw_i[...] if reverse else pw_i[...]
    for j in order:
        x_r[pl.ds(j, 1), :] = cr
        x_i[pl.ds(j, 1), :] = ci
        fr, fi = fin_r[pl.ds(j, 1), :], fin_i[pl.ds(j, 1), :]
        cr, ci = fr + pr * cr - pi * ci, fi + pr * ci + pi * cr


def _scan_rows(a_r, a_i, b_r, b_i, x_r, x_i, o_r, o_i, b_off, n_steps, reverse):
    w = 512
    for c in range(N_STATE // w):
        cols = pl.ds(c * w, w)
        ar = jnp.broadcast_to(a_r[:, cols], (N_SEG, w))
        ai = jnp.broadcast_to(a_i[:, cols], (N_SEG, w))

        def step(t, carry, cols=cols, ar=ar, ai=ai):
            xr, xi = carry
            k = (n_steps - 1 - t) if reverse else t
            rows = pl.ds(pl.multiple_of(k * N_SEG + b_off, N_SEG), N_SEG)
            nr = ar * xr - ai * xi + b_r[rows, cols]
            ni = ar * xi + ai * xr + b_i[rows, cols]
            o_r[rows, cols] = nr
            o_i[rows, cols] = ni
            return nr, ni

        xr, xi = lax.fori_loop(0, n_steps, step, (x_r[:, cols], x_i[:, cols]), unroll=4)
        x_r[:, cols] = xr
        x_i[:, cols] = xi


def _permute_in(src_ref, dst):
    for c in range(N_CB):
        for j in range(N_SEG):
            dst[c, pl.ds(j, SSM_LK, stride=N_SEG), :] = src_ref[j, :, pl.ds(c * LANES, LANES)]


def _permute_out(src, dst_ref):
    for c in range(N_CB):
        for j in range(N_SEG):
            dst_ref[j, :, pl.ds(c * LANES, LANES)] = src[c, pl.ds(j, SSM_LK, stride=N_SEG), :]


def _ssm_fwd(u3, abr, abi, pw_r, pw_i, fin_r, fin_i, bb_r, bb_i, cc_r, cc_i, dskip, finals_only):
    sl = u3.shape[1]
    nch = sl // SSM_LK

    def body(u_ref, abr_ref, abi_ref, pwr_ref, pwi_ref, finr_ref, fini_ref, bbr_ref, bbi_ref,
             ccr_ref, cci_ref, d_ref, *rest):
        if finals_only:
            xfr_ref, xfi_ref, up, xs_r, xs_i, x_r, x_i = rest
        else:
            y_ref, xsr_ref, xsi_ref, up, yp, xs_r, xs_i, x_r, x_i = rest
        k = pl.program_id(0)

        @pl.when(k == 0)
        def _():
            _seg_init(finr_ref, fini_ref, pwr_ref, pwi_ref, x_r, x_i, False)

        if not finals_only:
            xsr_ref[0] = x_r[...]
            xsi_ref[0] = x_i[...]
        _permute_in(u_ref, up)
        for c in range(N_CB):
            lhs = up[c].astype(BF16)
            xs_r[:, pl.ds(c * CB_STATES, CB_STATES)] = _dot(lhs, bbr_ref[c])
            xs_i[:, pl.ds(c * CB_STATES, CB_STATES)] = _dot(lhs, bbi_ref[c])
        _scan_rows(abr_ref, abi_ref, xs_r, xs_i, x_r, x_i, xs_r, xs_i, 0, SSM_LK, False)
        if finals_only:
            @pl.when(k == nch - 1)
            def _():
                xfr_ref[...] = x_r[...]
                xfi_ref[...] = x_i[...]
        else:
            for c in range(N_CB):
                cols = pl.ds(c * CB_STATES, CB_STATES)
                yp[c] = (_dot(xs_r[:, cols].astype(BF16), ccr_ref[c]) - _dot(xs_i[:, cols].astype(BF16), cci_ref[c])
                         + d_ref[:, pl.ds(c * LANES, LANES)] * up[c])
            _permute_out(yp, y_ref)

    ublk = pl.BlockSpec((N_SEG, SSM_LK, SSM_W), lambda k: (0, k, 0))
    st = pl.BlockSpec((1, N_SEG, N_STATE), lambda k: (k, 0, 0))
    vec = _full((1, N_STATE))
    mat = _full((N_SEG, N_STATE))
    chunk = pltpu.VMEM((N_CB, ROWS, LANES), F32)
    big = pltpu.VMEM((ROWS, N_STATE), F32)
    small = pltpu.VMEM((N_SEG, N_STATE), F32)
    if finals_only:
        out_specs, out_shape = [mat, mat], [_sds((N_SEG, N_STATE))] * 2
        scratch, name = [chunk, big, big, small, small], "ssm_fwd_finals"
    else:
        out_specs = [ublk, st, st]
        out_shape = [_sds(u3.shape)] + [_sds((nch, N_SEG, N_STATE))] * 2
        scratch, name = [chunk, chunk, big, big, small, small], "ssm_fwd"
    return pl.pallas_call(
        body, name=name, grid=(nch,),
        in_specs=[ublk, vec, vec, vec, vec, mat, mat,
                  _full((N_CB, LANES, CB_STATES)), _full((N_CB, LANES, CB_STATES)),
                  _full((N_CB, CB_STATES, LANES)), _full((N_CB, CB_STATES, LANES)), _full((1, SSM_W))],
        out_specs=out_specs, out_shape=out_shape, scratch_shapes=scratch,
        compiler_params=_cp(("arbitrary",)),
    )(u3, abr, abi, pw_r, pw_i, fin_r, fin_i, bb_r, bb_i, cc_r, cc_i, dskip)


def _ssm_bwd(u3, dy3, xst_r, xst_i, abr, abi, pw_r, pw_i, fin_r, fin_i, bb_r, bb_i, bbt_r, bbt_i,
             cct_r, cct_i, dskip, finals_only):
    sl = u3.shape[1]
    nch = sl // SSM_LK

    def body(u_ref, g_ref, xsr_ref, xsi_ref, abr_ref, abi_ref, pwr_ref, pwi_ref,
             finr_ref, fini_ref, bbr_ref, bbi_ref, btr_ref, bti_ref, ctr_ref, cti_ref, d_ref, *rest):
        if finals_only:
            lfr_ref, lfi_ref, gp, l_r, l_i, lam_r, lam_i = rest
        else:
            (du_ref, dar_ref, dai_ref, dbr_ref, dbi_ref, dcr_ref, dci_ref, dd_ref,
             gp, up, yp, l_r, l_i, lam_r, lam_i, x_r, x_i, xx_r, xx_i, sar, sai, sdd) = rest
        t = pl.program_id(0)

        @pl.when(t == 0)
        def _():
            _seg_init(finr_ref, fini_ref, pwr_ref, pwi_ref, lam_r, lam_i, True)
            if not finals_only:
                sar[...] = jnp.zeros_like(sar)
                sai[...] = jnp.zeros_like(sai)
                sdd[...] = jnp.zeros_like(sdd)
                dbr_ref[...] = jnp.zeros_like(dbr_ref)
                dbi_ref[...] = jnp.zeros_like(dbi_ref)
                dcr_ref[...] = jnp.zeros_like(dcr_ref)
                dci_ref[...] = jnp.zeros_like(dci_ref)

        _permute_in(g_ref, gp)
        for c in range(N_CB):
            cols = pl.ds(c * CB_STATES, CB_STATES)
            lhs = gp[c].astype(BF16)
            l_r[:, cols] = _dot(lhs, ctr_ref[c])
            l_i[:, cols] = -_dot(lhs, cti_ref[c])
        if not finals_only:
            _permute_in(u_ref, up)
            x_r[...] = xsr_ref[0]
            x_i[...] = xsi_ref[0]
            xx_r[pl.ds(0, N_SEG), :] = x_r[...]
            xx_i[pl.ds(0, N_SEG), :] = x_i[...]
            for c in range(N_CB):
                lhs = up[c].astype(BF16)
                xx_r[pl.ds(N_SEG, ROWS), pl.ds(c * CB_STATES, CB_STATES)] = _dot(lhs, bbr_ref[c])
                xx_i[pl.ds(N_SEG, ROWS), pl.ds(c * CB_STATES, CB_STATES)] = _dot(lhs, bbi_ref[c])
            _scan_rows(abr_ref, abi_ref, xx_r, xx_i, x_r, x_i, xx_r, xx_i, N_SEG, SSM_LK, False)
        _scan_rows(abr_ref, _Neg(abi_ref), l_r, l_i, lam_r, lam_i, l_r, l_i, 0, SSM_LK, True)
        if finals_only:
            @pl.when(t == nch - 1)
            def _():
                lfr_ref[...] = lam_r[...]
                lfi_ref[...] = lam_i[...]
        else:
            w = 512
            for cc in range(N_STATE // w):
                cols = pl.ds(cc * w, w)

                def acc_step(kk_, carry, cols=cols):
                    sr, si = carry
                    rows = pl.ds(pl.multiple_of(kk_ * N_SEG, N_SEG), N_SEG)
                    lr, li = l_r[rows, cols], l_i[rows, cols]
                    pr, pi = xx_r[rows, cols], xx_i[rows, cols]
                    return sr + lr * pr + li * pi, si + li * pr - lr * pi

                z = jnp.zeros((N_SEG, w), F32)
                sr, si = lax.fori_loop(0, SSM_LK, acc_step, (z, z), unroll=4)
                sar[:, cols] += sr
                sai[:, cols] += si
            for c in range(N_CB):
                cols = pl.ds(c * CB_STATES, CB_STATES)
                lrb = l_r[:, cols].astype(BF16)
                lib = l_i[:, cols].astype(BF16)
                ub = up[c].astype(BF16)
                gb = gp[c].astype(BF16)
                dbr_ref[c] += _dot_tn(lrb, ub)
                dbi_ref[c] += _dot_tn(lib, ub)
                dcr_ref[c] += _dot_tn(gb, xx_r[pl.ds(N_SEG, ROWS), cols].astype(BF16))
                dci_ref[c] += -_dot_tn(gb, xx_i[pl.ds(N_SEG, ROWS), cols].astype(BF16))
                yp[c] = _dot(lrb, btr_ref[c]) + _dot(lib, bti_ref[c]) + d_ref[:, pl.ds(c * LANES, LANES)] * gp[c]
                prod = gp[c] * up[c]
                sdd[:, pl.ds(c * LANES, LANES)] += jnp.sum(prod.reshape(SSM_LK, N_SEG, LANES), axis=0)
            _permute_out(yp, du_ref)

            @pl.when(t == nch - 1)
            def _():
                dar_ref[...] = jnp.sum(sar[...], axis=0, keepdims=True)
                dai_ref[...] = jnp.sum(sai[...], axis=0, keepdims=True)
                dd_ref[...] = jnp.sum(sdd[...], axis=0, keepdims=True)

    ublk = pl.BlockSpec((N_SEG, SSM_LK, SSM_W), lambda t: (0, nch - 1 - t, 0))
    st = pl.BlockSpec((1, N_SEG, N_STATE), lambda t: (nch - 1 - t, 0, 0))
    vec = _full((1, N_STATE))
    mat = _full((N_SEG, N_STATE))
    cs = _full((N_CB, LANES, CB_STATES))
    sc = _full((N_CB, CB_STATES, LANES))
    in_specs = [ublk, ublk, st, st, vec, vec, vec, vec, mat, mat, cs, cs, sc, sc, cs, cs, _full((1, SSM_W))]
    chunk = pltpu.VMEM((N_CB, ROWS, LANES), F32)
    big = pltpu.VMEM((ROWS, N_STATE), F32)
    small = pltpu.VMEM((N_SEG, N_STATE), F32)
    if finals_only:
        out_specs, out_shape = [mat, mat], [_sds((N_SEG, N_STATE))] * 2
        scratch, name = [chunk, big, big, small, small], "ssm_bwd_finals"
    else:
        out_specs = [ublk, vec, vec, sc, sc, cs, cs, _full((1, SSM_W))]
        out_shape = ([_sds(u3.shape), _sds((1, N_STATE)), _sds((1, N_STATE))]
                     + [_sds((N_CB, CB_STATES, LANES))] * 2 + [_sds((N_CB, LANES, CB_STATES))] * 2
                     + [_sds((1, SSM_W))])
        xx = pltpu.VMEM((ROWS + N_SEG, N_STATE), F32)
        scratch = [chunk, chunk, chunk, big, big, small, small, small, small, xx, xx, small, small,
                   pltpu.VMEM((N_SEG, SSM_W), F32)]
        name = "ssm_bwd"
    return pl.pallas_call(
        body, name=name, grid=(nch,), in_specs=in_specs, out_specs=out_specs, out_shape=out_shape,
        scratch_shapes=scratch, compiler_params=_cp(("arbitrary",)),
    )(u3, dy3, xst_r, xst_i, abr, abi, pw_r, pw_i, fin_r, fin_i, bb_r, bb_i, bbt_r, bbt_i, cct_r, cct_i, dskip)


def _row(tm, w):
    return pl.BlockSpec((tm, w), lambda i: (i, 0))


def _acc_rows(ref, rows, first):
    @pl.when(first)
    def _():
        ref[...] = jnp.zeros_like(ref)

    ref[...] += jnp.sum(rows, axis=0, keepdims=True)


def _fwd_mix(attn, y, x, glu_w, glu_b, ga, gs, w_out, g2, tm=512):
    s = x.shape[0]

    def body(a_ref, y_ref, x_ref, gw_ref, gb_ref, ga_ref, gs_ref, wo_ref, g2_ref, mix_ref, x2_ref, h_ref):
        a = a_ref[...]
        anb = ((a * _rms(a)) * ga_ref[...]).astype(BF16)
        z, _ = _gelu(y_ref[...])
        so = z * jax.nn.sigmoid(_dot(z.astype(BF16), gw_ref[...]) + gb_ref[...])
        snb = ((so * _rms(so)) * gs_ref[...]).astype(BF16)
        mix_ref[:, pl.ds(0, ATTN_W)] = anb
        mix_ref[:, pl.ds(ATTN_W, SSM_W)] = snb
        x2 = x_ref[...] + (_dot(anb, wo_ref[pl.ds(0, ATTN_W), :]) + _dot(snb, wo_ref[pl.ds(ATTN_W, SSM_W), :]))
        x2_ref[...] = x2
        h_ref[...] = ((x2 * _rms(x2)) * g2_ref[...]).astype(BF16)

    return pl.pallas_call(
        body, name="fwd_mix", grid=(s // tm,),
        in_specs=[_row(tm, ATTN_W), _row(tm, SSM_W), _row(tm, D_MODEL), _full((SSM_W, SSM_W)), _full((1, SSM_W)),
                  _full((1, ATTN_W)), _full((1, SSM_W)), _full((D_MODEL, D_MODEL)), _full((1, D_MODEL))],
        out_specs=[_row(tm, D_MODEL), _row(tm, D_MODEL), _row(tm, D_MODEL)],
        out_shape=[_sds((s, D_MODEL), BF16), _sds((s, D_MODEL)), _sds((s, D_MODEL), BF16)],
        compiler_params=_cp(("parallel",)),
    )(attn, y, x, glu_w, glu_b, ga, gs, w_out, g2)


def _mlp_up(h, w_up, tm=512, bn=1024):
    s = h.shape[0]

    def body(h_ref, w_ref, r_ref, hdn_ref):
        r = jnp.maximum(_dot(h_ref[...], w_ref[...]), 0.0)
        r_ref[...] = r.astype(BF16)
        hdn_ref[...] = (r * r).astype(BF16)

    blk = pl.BlockSpec((tm, bn), lambda i, j: (i, j))
    return pl.pallas_call(
        body, name="mlp_up", grid=(s // tm, D_FF // bn),
        in_specs=[pl.BlockSpec((tm, D_MODEL), lambda i, j: (i, 0)), pl.BlockSpec((D_MODEL, bn), lambda i, j: (0, j))],
        out_specs=[blk, blk], out_shape=[_sds((s, D_FF), BF16)] * 2,
        compiler_params=_cp(("parallel", "parallel")),
    )(h, w_up)


def _mlp_down_loss(hdn, w_down, x2, tgt, tm=512):
    s = x2.shape[0]

    def body(hdn_ref, w_ref, x2_ref, t_ref, dy_ref, dyb_ref, sse_ref):
        err = (x2_ref[...] + _dot(hdn_ref[...], w_ref[...])) - t_ref[...]
        dy = err * (1.0 / D_MODEL)
        dy_ref[...] = dy
        dyb_ref[...] = dy.astype(BF16)

        @pl.when(pl.program_id(0) == 0)
        def _():
            sse_ref[...] = jnp.zeros_like(sse_ref)

        sse_ref[...] += jnp.sum(jnp.sum(err * err, axis=0, keepdims=True), axis=1, keepdims=True)

    return pl.pallas_call(
        body, name="mlp_down_loss", grid=(s // tm,),
        in_specs=[_row(tm, D_FF), _full((D_FF, D_MODEL)), _row(tm, D_MODEL), _row(tm, D_MODEL)],
        out_specs=[_row(tm, D_MODEL), _row(tm, D_MODEL), _full((1, 1))],
        out_shape=[_sds((s, D_MODEL)), _sds((s, D_MODEL), BF16), _sds((1, 1))],
        compiler_params=_cp(("arbitrary",)),
    )(hdn, w_down, x2, tgt)


def _mlp_down_bwd(dyb, w_down_t, r, tm=512, bn=1024):
    s = dyb.shape[0]

    def body(dy_ref, w_ref, r_ref, dup_ref):
        dup_ref[...] = (_dot(dy_ref[...], w_ref[...]) * (2.0 * r_ref[...].astype(F32))).astype(BF16)

    blk = pl.BlockSpec((tm, bn), lambda i, j: (i, j))
    return pl.pallas_call(
        body, name="mlp_down_bwd", grid=(s // tm, D_FF // bn),
        in_specs=[pl.BlockSpec((tm, D_MODEL), lambda i, j: (i, 0)), pl.BlockSpec((D_MODEL, bn), lambda i, j: (0, j)), blk],
        out_specs=blk, out_shape=_sds((s, D_FF), BF16),
        compiler_params=_cp(("parallel", "parallel")),
    )(dyb, w_down_t, r)


def _mlp_up_bwd(dup, w_up_t, x2, g2, dy, tm=512):
    s = x2.shape[0]

    def body(dup_ref, w_ref, x2_ref, g2_ref, dy_ref, dx2_ref, dx2b_ref, dg_ref):
        dx, dg_rows = _rms_bwd(_dot(dup_ref[...], w_ref[...]), x2_ref[...], g2_ref[...], D_MODEL)
        dx2 = dy_ref[...] + dx
        dx2_ref[...] = dx2
        dx2b_ref[...] = dx2.astype(BF16)
        _acc_rows(dg_ref, dg_rows, pl.program_id(0) == 0)

    return pl.pallas_call(
        body, name="mlp_up_bwd", grid=(s // tm,),
        in_specs=[_row(tm, D_FF), _full((D_FF, D_MODEL)), _row(tm, D_MODEL), _full((1, D_MODEL)), _row(tm, D_MODEL)],
        out_specs=[_row(tm, D_MODEL), _row(tm, D_MODEL), _full((1, D_MODEL))],
        out_shape=[_sds((s, D_MODEL)), _sds((s, D_MODEL), BF16), _sds((1, D_MODEL))],
        compiler_params=_cp(("arbitrary",)),
    )(dup, w_up_t, x2, g2, dy)


def _mix_bwd(dx2b, w_out_t, attn, y, glu_w, glu_b, glu_w_t, ga, gs, tm=512):
    s = attn.shape[0]

    def body(dx2_ref, wot_ref, a_ref, y_ref, gw_ref, gb_ref, gwt_ref, ga_ref, gs_ref,
             da_ref, dys_ref, z_ref, dpre_ref, dga_ref, dgs_ref, dgb_ref):
        first = pl.program_id(0) == 0
        dmix = _dot(dx2_ref[...], wot_ref[...])
        da, dga_rows = _rms_bwd(dmix[:, :ATTN_W], a_ref[...], ga_ref[...], ATTN_W)
        da_ref[...] = da
        yv = y_ref[...]
        z, t = _gelu(yv)
        gate = jax.nn.sigmoid(_dot(z.astype(BF16), gw_ref[...]) + gb_ref[...])
        dso, dgs_rows = _rms_bwd(dmix[:, ATTN_W:], z * gate, gs_ref[...], SSM_W)
        dpre = dso * z * gate * (1.0 - gate)
        dpre_b = dpre.astype(BF16)
        dz = dso * gate + _dot(dpre_b, gwt_ref[...])
        dgelu = 0.5 * (1.0 + t) + 0.5 * yv * (1.0 - t * t) * (GELU_C * (1.0 + 3.0 * 0.044715 * (yv * yv)))
        dys_ref[...] = dz * dgelu
        z_ref[...] = z.astype(BF16)
        dpre_ref[...] = dpre_b
        _acc_rows(dga_ref, dga_rows, first)
        _acc_rows(dgs_ref, dgs_rows, first)
        _acc_rows(dgb_ref, dpre, first)

    vec = _full((1, SSM_W))
    return pl.pallas_call(
        body, name="mix_bwd", grid=(s // tm,),
        in_specs=[_row(tm, D_MODEL), _full((D_MODEL, D_MODEL)), _row(tm, ATTN_W), _row(tm, SSM_W),
                  _full((SSM_W, SSM_W)), vec, _full((SSM_W, SSM_W)), vec, vec],
        out_specs=[_row(tm, ATTN_W), _row(tm, SSM_W), _row(tm, SSM_W), _row(tm, SSM_W), vec, vec, vec],
        out_shape=[_sds((s, ATTN_W)), _sds((s, SSM_W)), _sds((s, SSM_W), BF16), _sds((s, SSM_W), BF16),
                   _sds((1, ATTN_W)), _sds((1, SSM_W)), _sds((1, SSM_W))],
        compiler_params=_cp(("arbitrary",)),
    )(dx2b, w_out_t, attn, y, glu_w, glu_b, glu_w_t, ga, gs)


def _qk_bwd(dqn, dkn, qk, dv, du, gq, gk, gmat, tm=512):
    s = qk.shape[0]

    def body(dq_ref, dk_ref, qk_ref, dv_ref, du_ref, gq_ref, gk_ref, gm_ref, dp_ref, dgq_ref, dgk_ref):
        first = pl.program_id(0) == 0
        gm = gm_ref[...]
        for idx, (d_ref, g_ref, dg_ref) in enumerate(((dq_ref, gq_ref, dgq_ref), (dk_ref, gk_ref, dgk_ref))):
            xv = qk_ref[:, pl.ds(idx * ATTN_W, ATTN_W)]
            dyv = d_ref[...]
            r = lax.rsqrt(_group_mean(xv * xv, gm) + EPS)
            gdy = dyv * g_ref[...]
            dx = r * gdy - xv * (r * r * r) * _group_mean(gdy * xv, gm)
            dp_ref[:, pl.ds(idx * ATTN_W, ATTN_W)] = dx.astype(BF16)
            _acc_rows(dg_ref, dyv * (xv * r), first)
        dp_ref[:, pl.ds(2 * ATTN_W, ATTN_W)] = dv_ref[...].astype(BF16)
        dp_ref[:, pl.ds(3 * ATTN_W, SSM_W)] = du_ref[...].astype(BF16)

    vec = _full((1, ATTN_W))
    return pl.pallas_call(
        body, name="qk_bwd", grid=(s // tm,),
        in_specs=[_row(tm, ATTN_W), _row(tm, ATTN_W), _row(tm, 2 * ATTN_W), _row(tm, ATTN_W), _row(tm, SSM_W),
                  vec, vec, _full((ATTN_W, ATTN_W))],
        out_specs=[_row(tm, 4 * ATTN_W), vec, vec],
        out_shape=[_sds((s, 4 * ATTN_W), BF16), _sds((1, ATTN_W)), _sds((1, ATTN_W))],
        compiler_params=_cp(("arbitrary",)),
    )(dqn, dkn, qk, dv, du, gq, gk, gmat)


def _in_bwd(dproj, w_in_t, x, g1, dx2, tm=512):
    s = x.shape[0]

    def body(dp_ref, w_ref, x_ref, g1_ref, dx2_ref, gx_ref, dg_ref):
        dx, dg_rows = _rms_bwd(_dot(dp_ref[...], w_ref[...]), x_ref[...], g1_ref[...], D_MODEL)
        gx_ref[...] = dx2_ref[...] + dx
        _acc_rows(dg_ref, dg_rows, pl.program_id(0) == 0)

    return pl.pallas_call(
        body, name="in_bwd", grid=(s // tm,),
        in_specs=[_row(tm, 4 * ATTN_W), _full((4 * ATTN_W, D_MODEL)), _row(tm, D_MODEL), _full((1, D_MODEL)),
                  _row(tm, D_MODEL)],
        out_specs=[_row(tm, D_MODEL), _full((1, D_MODEL))],
        out_shape=[_sds((s, D_MODEL)), _sds((1, D_MODEL))],
        compiler_params=_cp(("arbitrary",)),
    )(dproj, w_in_t, x, g1, dx2)


def _mm_tn(a, b, name, ts=1024):
    s, k = a.shape
    n = b.shape[1]
    bk, bn = min(k, 1024), min(n, 1024)

    def body(a_ref, b_ref, o_ref):
        @pl.when(pl.program_id(2) == 0)
        def _():
            o_ref[...] = jnp.zeros_like(o_ref)

        o_ref[...] += _dot_tn(a_ref[...], b_ref[...])

    return pl.pallas_call(
        body, name=name, grid=(k // bk, n // bn, s // ts),
        in_specs=[pl.BlockSpec((ts, bk), lambda i, j, t: (t, i)), pl.BlockSpec((ts, bn), lambda i, j, t: (t, j))],
        out_specs=pl.BlockSpec((bk, bn), lambda i, j, t: (i, j)), out_shape=_sds((k, n)),
        compiler_params=_cp(("parallel", "parallel", "arbitrary")),
    )(a, b)


def _peer(k):
    x, y, c = lax.axis_index("x"), lax.axis_index("y"), lax.axis_index("c")
    px = 1 - x if k & 4 else x
    py = 1 - y if k & 2 else y
    pc = 1 - c if k & 1 else c
    return (px, py, pc), 4 * px + 2 * py + pc


def _gather_rows(x_shard):
    m_per, n = x_shard.shape

    def body(x_ref, out_ref, send_sems, recv_sems, local_sem):
        x, y, c = lax.axis_index("x"), lax.axis_index("y"), lax.axis_index("c")
        me, sibling = (x, y, c), (x, y, 1 - c)
        chips = [(1 - x, y), (x, 1 - y), (1 - x, 1 - y)]

        def rows(px, py, pc):
            return out_ref.at[pl.ds((4 * px + 2 * py + pc) * m_per, m_per), :]

        def copy(k, block, to, src=None):
            return pltpu.make_async_remote_copy(
                src_ref=rows(*block) if src is None else src, dst_ref=rows(*block),
                send_sem=send_sems.at[k], recv_sem=recv_sems.at[k], device_id=to, device_id_type=MESH)

        mine = pltpu.make_async_copy(x_ref, rows(*me), local_sem)
        mine.start()
        first = [copy(0, me, sibling, src=x_ref)]
        first += [copy(1 + j, me, (*chip, c), src=x_ref) for j, chip in enumerate(chips)]
        for cp in first:
            cp.start()
        passed = [copy(4 + j, (*chip, c), sibling) for j, chip in enumerate(chips)]
        for j, chip in enumerate(chips):
            copy(1 + j, (*chip, c), me).wait_recv()
            passed[j].start()
        copy(0, sibling, me).wait_recv()
        for j, chip in enumerate(chips):
            copy(4 + j, (*chip, 1 - c), me).wait_recv()
        for cp in first + passed:
            cp.wait_send()
        mine.wait()

    return pl.pallas_call(
        body, name="gather_weights", out_shape=_sds((N_DEV * m_per, n), x_shard.dtype),
        in_specs=[pl.BlockSpec(memory_space=pltpu.VMEM)], out_specs=pl.BlockSpec(memory_space=pltpu.VMEM),
        scratch_shapes=[pltpu.SemaphoreType.DMA((7,)), pltpu.SemaphoreType.DMA((7,)), pltpu.SemaphoreType.DMA],
        compiler_params=pltpu.CompilerParams(vmem_limit_bytes=VMEM_LIMIT),
    )(x_shard)


def _grad_exchange(grads, small):
    srcs = list(grads) + [small]
    n = len(srcs)
    shard_rows = [g.shape[0] // N_DEV for g in grads] + [small.shape[0]]

    def body(*refs):
        ins, outs = refs[:n], refs[n:2 * n]
        send_sems, recv_sems, local_sems = refs[2 * n:]
        _, me = _peer(0)
        started = []
        for w in range(n):
            for k in range(N_DEV):
                peer, pidx = _peer(k)
                if w < n - 1:
                    src = ins[w].at[pl.ds(pidx * shard_rows[w], shard_rows[w]), :]
                else:
                    src = ins[w]
                if k == 0:
                    cp = pltpu.make_async_copy(src, outs[w].at[me], local_sems.at[w])
                else:
                    cp = pltpu.make_async_remote_copy(
                        src_ref=src, dst_ref=outs[w].at[me], send_sem=send_sems.at[w * 7 + k - 1],
                        recv_sem=recv_sems.at[w * 7 + k - 1], device_id=peer, device_id_type=MESH)
                cp.start()
                started.append((k, cp))
        for w in range(n):
            for k in range(1, N_DEV):
                peer, pidx = _peer(k)
                src = ins[w].at[pl.ds(0, shard_rows[w]), :] if w < n - 1 else ins[w]
                pltpu.make_async_remote_copy(
                    src_ref=src, dst_ref=outs[w].at[pidx], send_sem=send_sems.at[w * 7 + k - 1],
                    recv_sem=recv_sems.at[w * 7 + k - 1], device_id=peer, device_id_type=MESH).wait_recv()
        for k, cp in started:
            if k == 0:
                cp.wait()
            else:
                cp.wait_send()

    hbm = pl.BlockSpec(memory_space=pl.ANY)
    return pl.pallas_call(
        body, name="grad_exchange",
        out_shape=[_sds((N_DEV, r, a.shape[1]), a.dtype) for r, a in zip(shard_rows, srcs)],
        in_specs=[hbm] * n, out_specs=[hbm] * n,
        scratch_shapes=[pltpu.SemaphoreType.DMA((n * 7,)), pltpu.SemaphoreType.DMA((n * 7,)),
                        pltpu.SemaphoreType.DMA((n,))],
    )(*srcs)


def _adamw(w, m, v, gparts, name):
    r, c = w.shape
    tr = r if r * c <= 256 * 1024 else 128 * 1024 // c

    def body(w_ref, m_ref, v_ref, g_ref, go_ref, d_ref, mo_ref, vo_ref):
        g = g_ref[0].astype(F32)
        for i in range(1, N_DEV):
            g = g + g_ref[i].astype(F32)
        nm = ADAM_B1 * m_ref[...] + (1.0 - ADAM_B1) * g
        nv = ADAM_B2 * v_ref[...] + (1.0 - ADAM_B2) * (g * g)
        m_hat = nm / (1.0 - ADAM_B1 ** ADAM_STEP)
        v_hat = nv / (1.0 - ADAM_B2 ** ADAM_STEP)
        go_ref[...] = g
        d_ref[...] = -ADAM_LR * (m_hat / (jnp.sqrt(v_hat) + ADAM_EPS) + ADAM_WD * w_ref[...])
        mo_ref[...] = nm
        vo_ref[...] = nv

    blk = pl.BlockSpec((tr, c), lambda i: (i, 0))
    return pl.pallas_call(
        body, name=name, grid=(r // tr,),
        in_specs=[blk, blk, blk, pl.BlockSpec((N_DEV, tr, c), lambda i: (0, i, 0))],
        out_specs=[blk] * 4, out_shape=[_sds((r, c))] * 4,
        compiler_params=_cp(("parallel",)),
    )(w, m, v, gparts)


def _block_diag(a, states_first):
    a4 = a.reshape(N_CB, 8, SSM_GROUP, SSM_STATE)
    eye = jnp.eye(8, dtype=a.dtype)
    if states_first:
        return jnp.einsum("bgcp,gh->bgphc", a4, eye).reshape(N_CB, CB_STATES, LANES)
    return jnp.einsum("bgcp,gh->bgchp", a4, eye).reshape(N_CB, LANES, CB_STATES)


def _block_diag_of(full, states_first):
    if states_first:
        picked = jnp.einsum("bgphc,gh->bgcp", full.reshape(N_CB, 8, SSM_STATE, 8, SSM_GROUP), jnp.eye(8, dtype=full.dtype))
    else:
        picked = jnp.einsum("bgchp,gh->bgcp", full.reshape(N_CB, 8, SSM_GROUP, 8, SSM_STATE), jnp.eye(8, dtype=full.dtype))
    return picked.reshape(SSM_GROUPS, SSM_GROUP, SSM_STATE)


SMALL = ("norm1_g", "q_norm_g", "k_norm_g", "ssm_a_re", "ssm_a_im", "ssm_log_dt", "ssm_b_re", "ssm_b_im",
         "ssm_c_re", "ssm_c_im", "ssm_d", "glu_b", "attn_out_norm_g", "ssm_out_norm_g", "norm2_g")


def _pack_small(arrs):
    parts = []
    for a in arrs:
        flat = a.reshape(-1)
        rows = -(-flat.shape[0] // (8 * LANES)) * 8
        parts.append(jnp.pad(flat, (0, rows * LANES - flat.shape[0])).reshape(rows, LANES))
    return jnp.concatenate(parts, axis=0)


def _unpack_small(packed, shapes):
    out, r0 = [], 0
    for shp in shapes:
        size = math.prod(shp)
        rows = -(-size // (8 * LANES)) * 8
        out.append(packed[r0:r0 + rows].reshape(-1)[:size].reshape(shp))
        r0 += rows
    return out


def kernel(x, norm1_g, w_in, q_norm_g, k_norm_g, ssm_a_re, ssm_a_im, ssm_log_dt, ssm_b_re, ssm_b_im, ssm_c_re, ssm_c_im, ssm_d, glu_w, glu_b, attn_out_norm_g, ssm_out_norm_g, w_out, norm2_g, w_mlp_up, w_mlp_down, loss_target, m_norm1_g, m_w_in, m_q_norm_g, m_k_norm_g, m_ssm_a_re, m_ssm_a_im, m_ssm_log_dt, m_ssm_b_re, m_ssm_b_im, m_ssm_c_re, m_ssm_c_im, m_ssm_d, m_glu_w, m_glu_b, m_attn_out_norm_g, m_ssm_out_norm_g, m_w_out, m_norm2_g, m_w_mlp_up, m_w_mlp_down, v_norm1_g, v_w_in, v_q_norm_g, v_k_norm_g, v_ssm_a_re, v_ssm_a_im, v_ssm_log_dt, v_ssm_b_re, v_ssm_b_im, v_ssm_c_re, v_ssm_c_im, v_ssm_d, v_glu_w, v_glu_b, v_attn_out_norm_g, v_ssm_out_norm_g, v_w_out, v_norm2_g, v_w_mlp_up, v_w_mlp_down):
    weights = dict(norm1_g=norm1_g, w_in=w_in, q_norm_g=q_norm_g, k_norm_g=k_norm_g, ssm_a_re=ssm_a_re,
                   ssm_a_im=ssm_a_im, ssm_log_dt=ssm_log_dt, ssm_b_re=ssm_b_re, ssm_b_im=ssm_b_im,
                   ssm_c_re=ssm_c_re, ssm_c_im=ssm_c_im, ssm_d=ssm_d, glu_w=glu_w, glu_b=glu_b,
                   attn_out_norm_g=attn_out_norm_g, ssm_out_norm_g=ssm_out_norm_g, w_out=w_out, norm2_g=norm2_g,
                   w_mlp_up=w_mlp_up, w_mlp_down=w_mlp_down)
    mom_m = dict(norm1_g=m_norm1_g, w_in=m_w_in, q_norm_g=m_q_norm_g, k_norm_g=m_k_norm_g, ssm_a_re=m_ssm_a_re,
                 ssm_a_im=m_ssm_a_im, ssm_log_dt=m_ssm_log_dt, ssm_b_re=m_ssm_b_re, ssm_b_im=m_ssm_b_im,
                 ssm_c_re=m_ssm_c_re, ssm_c_im=m_ssm_c_im, ssm_d=m_ssm_d, glu_w=m_glu_w, glu_b=m_glu_b,
                 attn_out_norm_g=m_attn_out_norm_g, ssm_out_norm_g=m_ssm_out_norm_g, w_out=m_w_out,
                 norm2_g=m_norm2_g, w_mlp_up=m_w_mlp_up, w_mlp_down=m_w_mlp_down)
    mom_v = dict(norm1_g=v_norm1_g, w_in=v_w_in, q_norm_g=v_q_norm_g, k_norm_g=v_k_norm_g, ssm_a_re=v_ssm_a_re,
                 ssm_a_im=v_ssm_a_im, ssm_log_dt=v_ssm_log_dt, ssm_b_re=v_ssm_b_re, ssm_b_im=v_ssm_b_im,
                 ssm_c_re=v_ssm_c_re, ssm_c_im=v_ssm_c_im, ssm_d=v_ssm_d, glu_w=v_glu_w, glu_b=v_glu_b,
                 attn_out_norm_g=v_attn_out_norm_g, ssm_out_norm_g=v_ssm_out_norm_g, w_out=v_w_out,
                 norm2_g=v_norm2_g, w_mlp_up=v_w_mlp_up, w_mlp_down=v_w_mlp_down)
    order = list(weights)

    xs, tgt = x[0], loss_target[0]
    s = xs.shape[0]
    assert s % ATTN_CHUNK == 0 and (s // N_SEG) % SSM_LK == 0
    seg_len = s // N_SEG
    n_sq = seg_len.bit_length() - 1
    assert 1 << n_sq == seg_len

    win_rows, glu_rows, wout_rows, wup_rows, wdn_rows = 256, 32, 128, 512, 512
    pack = jnp.concatenate([w_in[0].T, glu_w[0].reshape(glu_rows, D_MODEL), w_out[0], w_mlp_up[0].T, w_mlp_down[0]],
                           axis=0).astype(BF16)
    per = pack.shape[0]
    gathered = _gather_rows(pack).reshape(N_DEV, per, D_MODEL)
    offs = [0, win_rows, win_rows + glu_rows, win_rows + glu_rows + wout_rows, win_rows + glu_rows + wout_rows + wup_rows, per]
    piece = lambda i: gathered[:, offs[i]:offs[i + 1], :]
    w_in_t = piece(0).reshape(4 * ATTN_W, D_MODEL)
    glu_full = piece(1).reshape(SSM_W, SSM_W)
    w_out_full = piece(2).reshape(D_MODEL, D_MODEL)
    w_up_t = piece(3).reshape(D_FF, D_MODEL)
    w_down_full = piece(4).reshape(D_FF, D_MODEL)
    w_in_full, w_up_full = w_in_t.T, w_up_t.T
    w_out_t, w_down_t, glu_t = w_out_full.T, w_down_full.T, glu_full.T

    gq = jnp.tile(q_norm_g[0], ATTN_W // HEAD_DIM)[None]
    gk = jnp.tile(k_norm_g[0], ATTN_W // HEAD_DIM)[None]
    lane = jnp.arange(ATTN_W) // HEAD_DIM
    gmat = jnp.where(lane[:, None] == lane[None, :], 1.0 / HEAD_DIM, 0.0).astype(BF16)
    a_re3 = ssm_a_re[0][:, None, :]
    a_im3 = ssm_a_im[0][:, None, :]
    ldt3 = ssm_log_dt[0][:, None, None]
    b_re_t = jnp.swapaxes(ssm_b_re[0], 1, 2)
    b_im_t = jnp.swapaxes(ssm_b_im[0], 1, 2)
    c_re, c_im = ssm_c_re[0], ssm_c_im[0]
    dskip = ssm_d[0].reshape(1, SSM_W)

    qk, qn, kn, v, u, xn = _fwd_proj(xs, norm1_g, w_in_full, gq, gk, gmat)
    attn, lse = _attn_fwd(qn, kn, v)

    abr3, abi3, bbr, bbi = _ssm_discretize(a_re3, a_im3, ldt3, b_re_t, b_im_t)
    abr, abi = abr3.reshape(1, N_STATE), abi3.reshape(1, N_STATE)
    pw_r, pw_i = _ssm_power(abr, abi, n_sq)
    bb_r, bb_i = _block_diag(bbr, False).astype(BF16), _block_diag(bbi, False).astype(BF16)
    bbt_r, bbt_i = _block_diag(bbr, True).astype(BF16), _block_diag(bbi, True).astype(BF16)
    cc_r, cc_i = _block_diag(c_re, True).astype(BF16), _block_diag(c_im, True).astype(BF16)
    cct_r, cct_i = _block_diag(c_re, False).astype(BF16), _block_diag(c_im, False).astype(BF16)
    u3 = u.reshape(N_SEG, seg_len, SSM_W)
    zero_fin = jnp.zeros((N_SEG, N_STATE), F32)
    ssm_args = (abr, abi, pw_r, pw_i)
    xf_r, xf_i = _ssm_fwd(u3, *ssm_args, zero_fin, zero_fin, bb_r, bb_i, cc_r, cc_i, dskip, True)
    y3, xst_r, xst_i = _ssm_fwd(u3, *ssm_args, xf_r, xf_i, bb_r, bb_i, cc_r, cc_i, dskip, False)
    y = y3.reshape(s, SSM_W)

    mix, x2, h = _fwd_mix(attn, y, xs, glu_full, glu_b, attn_out_norm_g, ssm_out_norm_g, w_out_full, norm2_g)
    r_act, hdn = _mlp_up(h, w_up_full)
    dy, dyb, sse = _mlp_down_loss(hdn, w_down_full, x2, tgt)
    loss = lax.psum(0.5 * sse[0, 0] / D_MODEL, ("x", "y", "c"))

    dup = _mlp_down_bwd(dyb, w_down_t, r_act)
    g_w_down = _mm_tn(hdn, dyb, "grad_w_down")
    dx2, dx2b, g_norm2 = _mlp_up_bwd(dup, w_up_t, x2, norm2_g, dy)
    g_w_up_t = _mm_tn(dup, h, "grad_w_up_t")
    dattn, dys, z_b, dpre_b, g_ga, g_gs, g_glu_b = _mix_bwd(dx2b, w_out_t, attn, y, glu_full, glu_b, glu_t,
                                                             attn_out_norm_g, ssm_out_norm_g)
    g_w_out = _mm_tn(mix, dx2b, "grad_w_out")
    g_glu_w = _mm_tn(z_b, dpre_b, "grad_glu_w")
    dqn, dkn, dv = _attn_bwd(qn, kn, v, attn, lse, dattn)

    dy3 = dys.reshape(N_SEG, seg_len, SSM_W)
    bwd_args = (u3, dy3, xst_r, xst_i, abr, abi, pw_r, pw_i)
    lf_r, lf_i = _ssm_bwd(*bwd_args, zero_fin, zero_fin, bb_r, bb_i, bbt_r, bbt_i, cct_r, cct_i, dskip, True)
    du3, dab_r, dab_i, dbb_r, dbb_i, dcc_r, dcc_i, g_d = _ssm_bwd(*bwd_args, lf_r, lf_i, bb_r, bb_i, bbt_r, bbt_i,
                                                                 cct_r, cct_i, dskip, False)
    g_a_re3, g_a_im3, g_ldt3, g_b_re_t, g_b_im_t = _ssm_discretize_bwd(
        a_re3, a_im3, ldt3, b_re_t, b_im_t, dab_r.reshape(a_re3.shape), dab_i.reshape(a_re3.shape),
        _block_diag_of(dbb_r, True), _block_diag_of(dbb_i, True))
    g_c_re, g_c_im = _block_diag_of(dcc_r, False), _block_diag_of(dcc_i, False)

    dproj, g_gq, g_gk = _qk_bwd(dqn, dkn, qk, dv, du3.reshape(s, SSM_W), gq, gk, gmat)
    grad_x, g_norm1 = _in_bwd(dproj, w_in_t, xs, norm1_g, dx2)
    g_w_in_t = _mm_tn(dproj, xn, "grad_w_in_t")

    small_grads = dict(
        norm1_g=g_norm1, q_norm_g=g_gq.reshape(ATTN_W // HEAD_DIM, HEAD_DIM).sum(0)[None],
        k_norm_g=g_gk.reshape(ATTN_W // HEAD_DIM, HEAD_DIM).sum(0)[None],
        ssm_a_re=g_a_re3.reshape(ssm_a_re.shape), ssm_a_im=g_a_im3.reshape(ssm_a_im.shape),
        ssm_log_dt=g_ldt3.reshape(ssm_log_dt.shape), ssm_b_re=jnp.swapaxes(g_b_re_t, 1, 2)[None],
        ssm_b_im=jnp.swapaxes(g_b_im_t, 1, 2)[None], ssm_c_re=g_c_re[None], ssm_c_im=g_c_im[None],
        ssm_d=g_d.reshape(ssm_d.shape), glu_b=g_glu_b, attn_out_norm_g=g_ga, ssm_out_norm_g=g_gs, norm2_g=g_norm2)

    big = ("w_in", "glu_w", "w_out", "w_mlp_up", "w_mlp_down")
    big_grads = [g.astype(BF16) for g in (g_w_in_t, g_glu_w, g_w_out, g_w_up_t, g_w_down)]
    parts = _grad_exchange(big_grads, _pack_small([small_grads[n] for n in SMALL]))
    transposed = {"w_in", "w_mlp_up"}
    res = {}
    for name, gp in zip(big, parts[:-1]):
        tr = (lambda a: a.T) if name in transposed else (lambda a: a)
        outs = _adamw(tr(weights[name][0]), tr(mom_m[name][0]), tr(mom_v[name][0]), gp, "adamw_" + name)
        res[name] = [tr(o)[None] for o in outs]
    outs = _adamw(_pack_small([weights[n] for n in SMALL]), _pack_small([mom_m[n] for n in SMALL]),
                  _pack_small([mom_v[n] for n in SMALL]), parts[-1], "adamw_small")
    shapes = [weights[n].shape for n in SMALL]
    unpacked = [_unpack_small(o, shapes) for o in outs]
    for i, n in enumerate(SMALL):
        res[n] = [unpacked[j][i] for j in range(4)]

    return (loss, grad_x[None], *[res[n][0] for n in order], *[res[n][1] for n in order],
            *[res[n][2] for n in order], *[res[n][3] for n in order])
```

```python
import math

import jax
import jax.numpy as jnp
from jax import lax
from jax.experimental import pallas as pl
from jax.experimental.pallas import tpu as pltpu

F32 = jnp.float32
BF16 = jnp.bfloat16

D_MODEL = 1024
ATTN_W = 512
HEAD_DIM = 64
SSM_W = 512
SSM_GROUP = 16
SSM_GROUPS = 32
SSM_STATE = 64
N_STATE = SSM_GROUPS * SSM_STATE
D_FF = 4096
EPS = 1e-6
NEG_INF = -1e30
ATTN_CHUNK = 2048
ATTN_BLOCK = 128
DILATIONS = (1, 4, 16)
N_SEG = 8
SSM_LK = 64
N_DEV = 8
LANES = 128

ADAM_LR = 0.001
ADAM_B1 = 0.9
ADAM_B2 = 0.999
ADAM_EPS = 1e-08
ADAM_WD = 0.01
ADAM_STEP = 10

VMEM_LIMIT = 56 * 1024 * 1024
GELU_C = math.sqrt(2.0 / math.pi)
MESH = pl.DeviceIdType.MESH


def _cp(sem, vmem=VMEM_LIMIT):
    return pltpu.CompilerParams(dimension_semantics=sem, vmem_limit_bytes=vmem)


def _dot(a, b):
    return jnp.dot(a, b, preferred_element_type=F32)


def _dot_nt(a, b):
    return lax.dot_general(a, b, (((1,), (1,)), ((), ())), preferred_element_type=F32)


def _dot_tn(a, b):
    return lax.dot_general(a, b, (((0,), (0,)), ((), ())), preferred_element_type=F32)


def _group_mean(x2, gmat):
    hi = x2.astype(BF16)
    lo = (x2 - hi.astype(F32)).astype(BF16)
    return _dot(hi, gmat) + _dot(lo, gmat)


def _rms(x):
    return lax.rsqrt(jnp.mean(x * x, axis=-1, keepdims=True) + EPS)


def _rms_bwd(dy, x, g, n):
    r = _rms(x)
    gdy = dy * g
    dx = r * gdy - x * (r * r * r) * (jnp.sum(gdy * x, axis=-1, keepdims=True) / n)
    return dx, dy * (x * r)


def _gelu(y):
    t = jnp.tanh(GELU_C * (y + 0.044715 * (y * y * y)))
    return 0.5 * y * (1.0 + t), t


def _full(shape):
    nd = len(shape)
    return pl.BlockSpec(shape, lambda *_: (0,) * nd)


def _sds(shape, dtype=F32):
    return jax.ShapeDtypeStruct(shape, dtype)


def _fwd_proj(x, g1, w_in, gq, gk, gmat, tm=512):
    s = x.shape[0]

    def body(x_ref, g1_ref, w_ref, gq_ref, gk_ref, gm_ref, qk_ref, qn_ref, kn_ref, v_ref, u_ref, xn_ref):
        xv = x_ref[...]
        xnb = ((xv * _rms(xv)) * g1_ref[...]).astype(BF16)
        xn_ref[...] = xnb
        proj = _dot(xnb, w_ref[...])
        q = proj[:, :ATTN_W]
        k = proj[:, ATTN_W:2 * ATTN_W]
        qk_ref[...] = proj[:, :2 * ATTN_W]
        v_ref[...] = proj[:, 2 * ATTN_W:3 * ATTN_W]
        u_ref[...] = proj[:, 3 * ATTN_W:]
        gm = gm_ref[...]
        qn_ref[...] = (q * lax.rsqrt(_group_mean(q * q, gm) + EPS)) * gq_ref[...]
        kn_ref[...] = (k * lax.rsqrt(_group_mean(k * k, gm) + EPS)) * gk_ref[...]

    row = lambda w: pl.BlockSpec((tm, w), lambda i: (i, 0))
    return pl.pallas_call(
        body, name="fwd_proj", grid=(s // tm,),
        in_specs=[row(D_MODEL), _full((1, D_MODEL)), _full((D_MODEL, 4 * ATTN_W)), _full((1, ATTN_W)),
                  _full((1, ATTN_W)), _full((ATTN_W, ATTN_W))],
        out_specs=[row(2 * ATTN_W), row(ATTN_W), row(ATTN_W), row(ATTN_W), row(ATTN_W), row(D_MODEL)],
        out_shape=[_sds((s, 2 * ATTN_W)), _sds((s, ATTN_W)), _sds((s, ATTN_W)), _sds((s, ATTN_W)),
                   _sds((s, ATTN_W)), _sds((s, D_MODEL), BF16)],
        compiler_params=_cp(("parallel",)),
    )(x, g1, w_in, gq, gk, gmat)


def _attn_rows(t, d, nb):
    if d == 1:
        q0 = pl.multiple_of(t * ATTN_BLOCK, ATTN_BLOCK)
        return (t, pl.ds(q0, ATTN_BLOCK), pl.ds(pl.multiple_of(ATTN_CHUNK + q0, ATTN_BLOCK), ATTN_BLOCK),
                pl.ds(pl.multiple_of(ATTN_CHUNK - ATTN_BLOCK + q0, ATTN_BLOCK), ATTN_BLOCK))
    r = t // nb
    b = t % nb
    return (b, pl.ds(ATTN_BLOCK * b * d + r, ATTN_BLOCK, stride=d),
            pl.ds(ATTN_CHUNK + ATTN_BLOCK * b * d + r, ATTN_BLOCK, stride=d),
            pl.ds(ATTN_CHUNK + ATTN_BLOCK * (b - 1) * d + r, ATTN_BLOCK, stride=d))


def _attn_masks():
    row = lax.broadcasted_iota(jnp.int32, (ATTN_BLOCK, LANES), 0)
    col = lax.broadcasted_iota(jnp.int32, (ATTN_BLOCK, LANES), 1)
    return row, col


NBLK = ATTN_CHUNK // ATTN_BLOCK


def _attn_bias(bias_s):
    row, col = _attn_masks()
    bias_s[:, pl.ds(0, LANES)] = jnp.where(col >= row, 0.0, NEG_INF)
    bias_s[:, pl.ds(LANES, LANES)] = jnp.where(col <= row, 0.0, NEG_INF)
    return col < HEAD_DIM


def _attn_fwd(qn, kn, v, ex=None, unroll=8):
    s = qn.shape[0]
    nch = s // ATTN_CHUNK
    scale = HEAD_DIM ** -0.5
    npat = len(DILATIONS)
    n_hp = ATTN_W // LANES

    def body(*refs):
        ((q_ref, kp_ref, kc_ref, vp_ref, vc_ref), (o_ref, lse_ref),
         (kk, vv, kt_s, vb_s, bias_s, m_s, a_s), hx) = _carry(ex, refs, 5, 2)
        i = pl.program_id(1)
        if ex is not None:
            @pl.when(jnp.logical_and(pl.program_id(0) == 0, i == 0))
            def _():
                ex.start(*hx)

        kk[pl.ds(0, ATTN_CHUNK), :] = kp_ref[...]
        kk[pl.ds(ATTN_CHUNK, ATTN_CHUNK), :] = kc_ref[...]
        vv[pl.ds(0, ATTN_CHUNK), :] = vp_ref[...]
        vv[pl.ds(ATTN_CHUNK, ATTN_CHUNK), :] = vc_ref[...]
        head0 = _attn_bias(bias_s)

        for p, d in enumerate(DILATIONS):
            nb = ATTN_CHUNK // (ATTN_BLOCK * d)

            def prep(t, carry, d=d, nb=nb):
                _, _, crows, prows = _attn_rows(t, d, nb)
                kt_s[t, :, pl.ds(0, LANES)] = kk[prows, :].T.astype(BF16)
                kt_s[t, :, pl.ds(LANES, LANES)] = kk[crows, :].T.astype(BF16)
                vp = vv[prows, :]
                vc = vv[crows, :]
                vb_s[2 * t, pl.ds(0, ATTN_BLOCK), :] = jnp.where(head0, vp, 1.0).astype(BF16)
                vb_s[2 * t, pl.ds(ATTN_BLOCK, ATTN_BLOCK), :] = jnp.where(head0, vc, 1.0).astype(BF16)
                vb_s[2 * t + 1, pl.ds(0, ATTN_BLOCK), :] = jnp.where(head0, 1.0, vp).astype(BF16)
                vb_s[2 * t + 1, pl.ds(ATTN_BLOCK, ATTN_BLOCK), :] = jnp.where(head0, 1.0, vc).astype(BF16)
                return carry

            lax.fori_loop(0, NBLK, prep, 0, unroll=4)

            def blk(t, carry, p=p, d=d, nb=nb):
                b, qrows, _, _ = _attn_rows(t, d, nb)
                q = q_ref[qrows, :]
                pen = jnp.where(jnp.logical_or(i > 0, b > 0), 0.0, NEG_INF)
                kt = kt_s[t]
                outs, ms = [], []
                for h in range(2):
                    hm = head0 if h == 0 else jnp.logical_not(head0)
                    qh = jnp.where(hm, q, 0.0).astype(BF16)
                    sc = _dot(qh, kt) * scale + bias_s[...]
                    s_p = sc[:, :LANES] + pen
                    s_c = sc[:, LANES:]
                    m = jnp.max(jnp.maximum(s_p, s_c), axis=-1, keepdims=True)
                    e_b = jnp.concatenate([jnp.exp(s_p - m), jnp.exp(s_c - m)], axis=1).astype(BF16)
                    outs.append(_dot(e_b, vb_s[2 * t + h]))
                    ms.append(jnp.broadcast_to(m, (ATTN_BLOCK, LANES)))
                m_s[p, qrows, :] = jnp.where(head0, ms[0], ms[1])
                a_s[2 * p, qrows, :] = outs[0]
                a_s[2 * p + 1, qrows, :] = outs[1]
                return carry

            lax.fori_loop(0, NBLK, blk, 0, unroll=unroll)

        def merge(t, carry):
            rows = pl.ds(pl.multiple_of(t * ATTN_BLOCK, ATTN_BLOCK), ATTN_BLOCK)
            m_all = m_s[0, rows, :]
            for p in range(1, npat):
                m_all = jnp.maximum(m_all, m_s[p, rows, :])
            num = jnp.zeros((ATTN_BLOCK, LANES), F32)
            den = jnp.zeros((ATTN_BLOCK, LANES), F32)
            for p in range(npat):
                w = jnp.exp(m_s[p, rows, :] - m_all)
                a0, a1 = a_s[2 * p, rows, :], a_s[2 * p + 1, rows, :]
                num = num + jnp.where(head0, a0, a1) * w
                den = den + pltpu.roll(jnp.where(head0, a1, a0), HEAD_DIM, 1) * w
            o_ref[rows, :] = num / den
            lse_ref[rows, :] = m_all + jnp.log(den)
            return carry

        lax.fori_loop(0, NBLK, merge, 0, unroll=2)
        if ex is not None:
            @pl.when(jnp.logical_and(pl.program_id(0) == n_hp - 1, i == nch - 1))
            def _():
                ex.wait(*hx)

    cur = pl.BlockSpec((ATTN_CHUNK, LANES), lambda h, i: (i, h))
    prev = pl.BlockSpec((ATTN_CHUNK, LANES), lambda h, i: (jnp.maximum(i - 1, 0), h))
    hosted = ex is not None
    return pl.pallas_call(
        body, name="attn_fwd", grid=(n_hp, nch),
        in_specs=[cur, prev, cur, prev, cur] + (ex.specs() if hosted else []),
        out_specs=[cur, cur] + (ex.specs() if hosted else []),
        out_shape=[_sds((s, ATTN_W)), _sds((s, ATTN_W))] + (ex.out_shape() if hosted else []),
        scratch_shapes=[pltpu.VMEM((2 * ATTN_CHUNK, LANES), F32), pltpu.VMEM((2 * ATTN_CHUNK, LANES), F32),
                        pltpu.VMEM((NBLK, LANES, 2 * LANES), BF16), pltpu.VMEM((2 * NBLK, 2 * ATTN_BLOCK, LANES), BF16),
                        pltpu.VMEM((ATTN_BLOCK, 2 * LANES), F32),
                        pltpu.VMEM((npat, ATTN_CHUNK, LANES), F32), pltpu.VMEM((2 * npat, ATTN_CHUNK, LANES), F32)]
        + (ex.scratch() if hosted else []),
        compiler_params=_cp(("arbitrary", "arbitrary")),
    )(qn, kn, kn, v, v, *(ex.srcs if hosted else []))


def _attn_bwd(qn, kn, v, o, lse, do, ex=None, group=4):
    s = qn.shape[0]
    nch = s // ATTN_CHUNK
    scale = HEAD_DIM ** -0.5
    npat = len(DILATIONS)
    n_hp = ATTN_W // LANES

    def body(*refs):
        ((q_ref, kp_ref, kc_ref, vp_ref, vc_ref, o_ref, lse_ref, do_ref), (dq_ref, dk_ref, dv_ref),
         (kk, vv, dkk, dvv, kt_s, vt_s, kn_s, bias_s, dq_s, dl_s, dkb, dvb), hx) = _carry(ex, refs, 8, 3)
        step = pl.program_id(1)
        i = nch - 1 - step
        if ex is not None:
            @pl.when(jnp.logical_and(pl.program_id(0) == 0, step == 0))
            def _():
                ex.start(*hx)

        kk[pl.ds(0, ATTN_CHUNK), :] = kp_ref[...]
        kk[pl.ds(ATTN_CHUNK, ATTN_CHUNK), :] = kc_ref[...]
        vv[pl.ds(0, ATTN_CHUNK), :] = vp_ref[...]
        vv[pl.ds(ATTN_CHUNK, ATTN_CHUNK), :] = vc_ref[...]

        @pl.when(step == 0)
        def _():
            dkk[pl.ds(ATTN_CHUNK, ATTN_CHUNK), :] = jnp.zeros((ATTN_CHUNK, LANES), F32)
            dvv[pl.ds(ATTN_CHUNK, ATTN_CHUNK), :] = jnp.zeros((ATTN_CHUNK, LANES), F32)

        @pl.when(step > 0)
        def _():
            dkk[pl.ds(ATTN_CHUNK, ATTN_CHUNK), :] = dkk[pl.ds(0, ATTN_CHUNK), :]
            dvv[pl.ds(ATTN_CHUNK, ATTN_CHUNK), :] = dvv[pl.ds(0, ATTN_CHUNK), :]

        dkk[pl.ds(0, ATTN_CHUNK), :] = jnp.zeros((ATTN_CHUNK, LANES), F32)
        dvv[pl.ds(0, ATTN_CHUNK), :] = jnp.zeros((ATTN_CHUNK, LANES), F32)
        head0 = _attn_bias(bias_s)

        def delta(t, carry):
            rows = pl.ds(pl.multiple_of(t * ATTN_BLOCK, ATTN_BLOCK), ATTN_BLOCK)
            prod = do_ref[rows, :] * o_ref[rows, :]
            d0 = jnp.sum(jnp.where(head0, prod, 0.0), axis=-1, keepdims=True)
            d1 = jnp.sum(jnp.where(head0, 0.0, prod), axis=-1, keepdims=True)
            dl_s[rows, :] = jnp.where(head0, d0, d1)
            return carry

        lax.fori_loop(0, NBLK, delta, 0, unroll=2)

        for p, d in enumerate(DILATIONS):
            nb = ATTN_CHUNK // (ATTN_BLOCK * d)

            def prep(t, carry, d=d, nb=nb):
                _, _, crows, prows = _attn_rows(t, d, nb)
                kp, kc = kk[prows, :], kk[crows, :]
                kt_s[t, :, pl.ds(0, LANES)] = kp.T.astype(BF16)
                kt_s[t, :, pl.ds(LANES, LANES)] = kc.T.astype(BF16)
                kn_s[t, pl.ds(0, ATTN_BLOCK), :] = kp.astype(BF16)
                kn_s[t, pl.ds(ATTN_BLOCK, ATTN_BLOCK), :] = kc.astype(BF16)
                vt_s[t, :, pl.ds(0, LANES)] = vv[prows, :].T.astype(BF16)
                vt_s[t, :, pl.ds(LANES, LANES)] = vv[crows, :].T.astype(BF16)
                return carry

            lax.fori_loop(0, NBLK, prep, 0, unroll=4)

            def blk(tg, carry, p=p, d=d, nb=nb):
                st = []
                for g in range(group):
                    t = tg * group + g
                    b, qrows, _, _ = _attn_rows(t, d, nb)
                    q = q_ref[qrows, :]
                    dout = do_ref[qrows, :]
                    lse_b = lse_ref[qrows, :]
                    dl_b = dl_s[qrows, :]
                    pen = jnp.where(jnp.logical_or(i > 0, b > 0), 0.0, NEG_INF)
                    for h in range(2):
                        hm = head0 if h == 0 else jnp.logical_not(head0)
                        c0 = h * HEAD_DIM
                        qh = jnp.where(hm, q, 0.0).astype(BF16)
                        doh = jnp.where(hm, dout, 0.0).astype(BF16)
                        st.append(dict(t=t, qrows=qrows, qh=qh, doh=doh, pen=pen, lse=lse_b[:, c0:c0 + 1],
                                       dl=dl_b[:, c0:c0 + 1], sc=_dot(qh, kt_s[t]), dp=_dot(doh, vt_s[t])))
                for e in st:
                    sc = e["sc"] * scale + bias_s[...]
                    sc = jnp.concatenate([sc[:, :LANES] + e["pen"], sc[:, LANES:]], axis=1)
                    pr = jnp.exp(sc - e["lse"])
                    e["ds"] = (pr * (e["dp"] - e["dl"]) * scale).astype(BF16)
                    e["pr"] = pr.astype(BF16)
                for g in range(group):
                    e0, e1 = st[2 * g], st[2 * g + 1]
                    t = e0["t"]
                    dq_s[p, e0["qrows"], :] = jnp.where(head0, _dot(e0["ds"], kn_s[t]), _dot(e1["ds"], kn_s[t]))
                    dkb[t] = _dot_tn(e0["ds"], e0["qh"]) + _dot_tn(e1["ds"], e1["qh"])
                    dvb[t] = _dot_tn(e0["pr"], e0["doh"]) + _dot_tn(e1["pr"], e1["doh"])
                return carry

            lax.fori_loop(0, NBLK // group, blk, 0)

            def scatter(t, carry, d=d, nb=nb):
                _, _, crows, prows = _attn_rows(t, d, nb)
                dkk[prows, :] = dkk[prows, :] + dkb[t, pl.ds(0, ATTN_BLOCK), :]
                dkk[crows, :] = dkk[crows, :] + dkb[t, pl.ds(ATTN_BLOCK, ATTN_BLOCK), :]
                dvv[prows, :] = dvv[prows, :] + dvb[t, pl.ds(0, ATTN_BLOCK), :]
                dvv[crows, :] = dvv[crows, :] + dvb[t, pl.ds(ATTN_BLOCK, ATTN_BLOCK), :]
                return carry

            lax.fori_loop(0, NBLK, scatter, 0)

        def finish(t, carry):
            rows = pl.ds(pl.multiple_of(t * ATTN_BLOCK, ATTN_BLOCK), ATTN_BLOCK)
            acc = dq_s[0, rows, :]
            for p in range(1, npat):
                acc = acc + dq_s[p, rows, :]
            dq_ref[rows, :] = acc
            return carry

        lax.fori_loop(0, NBLK, finish, 0, unroll=2)
        dk_ref[...] = dkk[pl.ds(ATTN_CHUNK, ATTN_CHUNK), :]
        dv_ref[...] = dvv[pl.ds(ATTN_CHUNK, ATTN_CHUNK), :]
        if ex is not None:
            @pl.when(jnp.logical_and(pl.program_id(0) == n_hp - 1, step == nch - 1))
            def _():
                ex.wait(*hx)

    cur = pl.BlockSpec((ATTN_CHUNK, LANES), lambda h, t: (nch - 1 - t, h))
    prev = pl.BlockSpec((ATTN_CHUNK, LANES), lambda h, t: (jnp.maximum(nch - 2 - t, 0), h))
    big = pltpu.VMEM((2 * ATTN_CHUNK, LANES), F32)
    pair_t = pltpu.VMEM((NBLK, LANES, 2 * LANES), BF16)
    hosted = ex is not None
    return pl.pallas_call(
        body, name="attn_bwd", grid=(n_hp, nch),
        in_specs=[cur, prev, cur, prev, cur, cur, cur, cur] + (ex.specs() if hosted else []),
        out_specs=[cur, cur, cur] + (ex.specs() if hosted else []),
        out_shape=[_sds((s, ATTN_W))] * 3 + (ex.out_shape() if hosted else []),
        scratch_shapes=[big, big, big, big, pair_t, pair_t, pltpu.VMEM((NBLK, 2 * ATTN_BLOCK, LANES), BF16),
                        pltpu.VMEM((ATTN_BLOCK, 2 * LANES), F32),
                        pltpu.VMEM((npat, ATTN_CHUNK, LANES), F32), pltpu.VMEM((ATTN_CHUNK, LANES), F32),
                        pltpu.VMEM((NBLK, 2 * ATTN_BLOCK, LANES), F32), pltpu.VMEM((NBLK, 2 * ATTN_BLOCK, LANES), F32)]
        + (ex.scratch() if hosted else []),
        compiler_params=_cp(("arbitrary", "arbitrary")),
    )(qn, kn, kn, v, v, o, lse, do, *(ex.srcs if hosted else []))


def _attn_fwd_v1(qn, kn, v):
    s = qn.shape[0]
    nch = s // ATTN_CHUNK
    scale = HEAD_DIM ** -0.5
    npat = len(DILATIONS)

    def body(q_ref, kp_ref, kc_ref, vp_ref, vc_ref, o_ref, lse_ref, kk, vv, m_s, l_s, acc_s):
        i = pl.program_id(1)
        kk[pl.ds(0, ATTN_CHUNK), :] = kp_ref[...]
        kk[pl.ds(ATTN_CHUNK, ATTN_CHUNK), :] = kc_ref[...]
        vv[pl.ds(0, ATTN_CHUNK), :] = vp_ref[...]
        vv[pl.ds(ATTN_CHUNK, ATTN_CHUNK), :] = vc_ref[...]
        row, col = _attn_masks()
        head0 = col < HEAD_DIM
        mask_cur = col <= row
        diff = col - row

        for p, d in enumerate(DILATIONS):
            nb = ATTN_CHUNK // (ATTN_BLOCK * d)

            def blk(t, carry, p=p, d=d, nb=nb):
                b, qrows, crows, prows = _attn_rows(t, d, nb)
                q = q_ref[qrows, :]
                kc = kk[crows, :].astype(BF16)
                kp = kk[prows, :].astype(BF16)
                vc = vv[crows, :].astype(BF16)
                vp = vv[prows, :].astype(BF16)
                thr = jnp.where(jnp.logical_or(i > 0, b > 0), 0, 4 * ATTN_BLOCK)
                mask_prev = diff >= thr
                accs, ms, ls = [], [], []
                for h in range(2):
                    hm = head0 if h == 0 else jnp.logical_not(head0)
                    qh = jnp.where(hm, q, 0.0).astype(BF16)
                    s_p = jnp.where(mask_prev, _dot_nt(qh, kp) * scale, NEG_INF)
                    s_c = jnp.where(mask_cur, _dot_nt(qh, kc) * scale, NEG_INF)
                    m = jnp.maximum(jnp.max(s_p, axis=-1, keepdims=True), jnp.max(s_c, axis=-1, keepdims=True))
                    e_p = jnp.exp(s_p - m)
                    e_c = jnp.exp(s_c - m)
                    l = jnp.sum(e_p, axis=-1, keepdims=True) + jnp.sum(e_c, axis=-1, keepdims=True)
                    accs.append(_dot(e_p.astype(BF16), vp) + _dot(e_c.astype(BF16), vc))
                    ms.append(jnp.broadcast_to(m, (ATTN_BLOCK, LANES)))
                    ls.append(jnp.broadcast_to(l, (ATTN_BLOCK, LANES)))
                m_s[p, qrows, :] = jnp.where(head0, ms[0], ms[1])
                l_s[p, qrows, :] = jnp.where(head0, ls[0], ls[1])
                acc_s[p, qrows, :] = jnp.where(head0, accs[0], accs[1])
                return carry

            lax.fori_loop(0, ATTN_CHUNK // ATTN_BLOCK, blk, 0)

        def merge(t, carry):
            rows = pl.ds(pl.multiple_of(t * ATTN_BLOCK, ATTN_BLOCK), ATTN_BLOCK)
            m_all = m_s[0, rows, :]
            for p in range(1, npat):
                m_all = jnp.maximum(m_all, m_s[p, rows, :])
            num = jnp.zeros((ATTN_BLOCK, LANES), F32)
            den = jnp.zeros((ATTN_BLOCK, LANES), F32)
            for p in range(npat):
                w = jnp.exp(m_s[p, rows, :] - m_all)
                num = num + acc_s[p, rows, :] * w
                den = den + l_s[p, rows, :] * w
            o_ref[rows, :] = num / den
            lse_ref[rows, :] = m_all + jnp.log(den)
            return carry

        lax.fori_loop(0, ATTN_CHUNK // ATTN_BLOCK, merge, 0)

    cur = pl.BlockSpec((ATTN_CHUNK, LANES), lambda h, i: (i, h))
    prev = pl.BlockSpec((ATTN_CHUNK, LANES), lambda h, i: (jnp.maximum(i - 1, 0), h))
    return pl.pallas_call(
        body, name="attn_fwd", grid=(ATTN_W // LANES, nch),
        in_specs=[cur, prev, cur, prev, cur],
        out_specs=[cur, cur],
        out_shape=[_sds((s, ATTN_W)), _sds((s, ATTN_W))],
        scratch_shapes=[pltpu.VMEM((2 * ATTN_CHUNK, LANES), F32), pltpu.VMEM((2 * ATTN_CHUNK, LANES), F32),
                        pltpu.VMEM((npat, ATTN_CHUNK, LANES), F32), pltpu.VMEM((npat, ATTN_CHUNK, LANES), F32),
                        pltpu.VMEM((npat, ATTN_CHUNK, LANES), F32)],
        compiler_params=_cp(("parallel", "parallel")),
    )(qn, kn, kn, v, v)


def _attn_bwd_v1(qn, kn, v, o, lse, do):
    s = qn.shape[0]
    nch = s // ATTN_CHUNK
    scale = HEAD_DIM ** -0.5
    npat = len(DILATIONS)

    def body(q_ref, kp_ref, kc_ref, vp_ref, vc_ref, o_ref, lse_ref, do_ref, dq_ref, dk_ref, dv_ref,
             kk, vv, dkk, dvv, dq_s, dl_s):
        step = pl.program_id(1)
        i = nch - 1 - step
        kk[pl.ds(0, ATTN_CHUNK), :] = kp_ref[...]
        kk[pl.ds(ATTN_CHUNK, ATTN_CHUNK), :] = kc_ref[...]
        vv[pl.ds(0, ATTN_CHUNK), :] = vp_ref[...]
        vv[pl.ds(ATTN_CHUNK, ATTN_CHUNK), :] = vc_ref[...]

        @pl.when(step == 0)
        def _():
            dkk[pl.ds(ATTN_CHUNK, ATTN_CHUNK), :] = jnp.zeros((ATTN_CHUNK, LANES), F32)
            dvv[pl.ds(ATTN_CHUNK, ATTN_CHUNK), :] = jnp.zeros((ATTN_CHUNK, LANES), F32)

        @pl.when(step > 0)
        def _():
            dkk[pl.ds(ATTN_CHUNK, ATTN_CHUNK), :] = dkk[pl.ds(0, ATTN_CHUNK), :]
            dvv[pl.ds(ATTN_CHUNK, ATTN_CHUNK), :] = dvv[pl.ds(0, ATTN_CHUNK), :]

        dkk[pl.ds(0, ATTN_CHUNK), :] = jnp.zeros((ATTN_CHUNK, LANES), F32)
        dvv[pl.ds(0, ATTN_CHUNK), :] = jnp.zeros((ATTN_CHUNK, LANES), F32)

        row, col = _attn_masks()
        head0 = col < HEAD_DIM
        mask_cur = col <= row
        diff = col - row

        def delta(t, carry):
            rows = pl.ds(pl.multiple_of(t * ATTN_BLOCK, ATTN_BLOCK), ATTN_BLOCK)
            prod = do_ref[rows, :] * o_ref[rows, :]
            d0 = jnp.sum(jnp.where(head0, prod, 0.0), axis=-1, keepdims=True)
            d1 = jnp.sum(jnp.where(head0, 0.0, prod), axis=-1, keepdims=True)
            dl_s[rows, :] = jnp.where(head0, d0, d1)
            return carry

        lax.fori_loop(0, ATTN_CHUNK // ATTN_BLOCK, delta, 0)

        for p, d in enumerate(DILATIONS):
            nb = ATTN_CHUNK // (ATTN_BLOCK * d)

            def blk(t, carry, p=p, d=d, nb=nb):
                b, qrows, crows, prows = _attn_rows(t, d, nb)
                q = q_ref[qrows, :]
                dout = do_ref[qrows, :]
                lse_b = lse_ref[qrows, :]
                dl_b = dl_s[qrows, :]
                kc = kk[crows, :].astype(BF16)
                kp = kk[prows, :].astype(BF16)
                vc = vv[crows, :].astype(BF16)
                vp = vv[prows, :].astype(BF16)
                thr = jnp.where(jnp.logical_or(i > 0, b > 0), 0, 4 * ATTN_BLOCK)
                mask_prev = diff >= thr
                dqs = []
                dk_p = jnp.zeros((ATTN_BLOCK, LANES), F32)
                dk_c = jnp.zeros((ATTN_BLOCK, LANES), F32)
                dv_p = jnp.zeros((ATTN_BLOCK, LANES), F32)
                dv_c = jnp.zeros((ATTN_BLOCK, LANES), F32)
                for h in range(2):
                    hm = head0 if h == 0 else jnp.logical_not(head0)
                    c0 = h * HEAD_DIM
                    qh = jnp.where(hm, q, 0.0).astype(BF16)
                    doh = jnp.where(hm, dout, 0.0).astype(BF16)
                    lse_h = lse_b[:, c0:c0 + 1]
                    dl_h = dl_b[:, c0:c0 + 1]
                    s_p = jnp.where(mask_prev, _dot_nt(qh, kp) * scale, NEG_INF)
                    s_c = jnp.where(mask_cur, _dot_nt(qh, kc) * scale, NEG_INF)
                    p_p = jnp.exp(s_p - lse_h)
                    p_c = jnp.exp(s_c - lse_h)
                    ds_p = (p_p * (_dot_nt(doh, vp) - dl_h) * scale).astype(BF16)
                    ds_c = (p_c * (_dot_nt(doh, vc) - dl_h) * scale).astype(BF16)
                    dqs.append(_dot(ds_p, kp) + _dot(ds_c, kc))
                    dk_p = dk_p + _dot_tn(ds_p, qh)
                    dk_c = dk_c + _dot_tn(ds_c, qh)
                    dv_p = dv_p + _dot_tn(p_p.astype(BF16), doh)
                    dv_c = dv_c + _dot_tn(p_c.astype(BF16), doh)
                dq_s[p, qrows, :] = jnp.where(head0, dqs[0], dqs[1])
                dkk[prows, :] = dkk[prows, :] + dk_p
                dkk[crows, :] = dkk[crows, :] + dk_c
                dvv[prows, :] = dvv[prows, :] + dv_p
                dvv[crows, :] = dvv[crows, :] + dv_c
                return carry

            lax.fori_loop(0, ATTN_CHUNK // ATTN_BLOCK, blk, 0)

        def finish(t, carry):
            rows = pl.ds(pl.multiple_of(t * ATTN_BLOCK, ATTN_BLOCK), ATTN_BLOCK)
            acc = dq_s[0, rows, :]
            for p in range(1, npat):
                acc = acc + dq_s[p, rows, :]
            dq_ref[rows, :] = acc
            return carry

        lax.fori_loop(0, ATTN_CHUNK // ATTN_BLOCK, finish, 0)
        dk_ref[...] = dkk[pl.ds(ATTN_CHUNK, ATTN_CHUNK), :]
        dv_ref[...] = dvv[pl.ds(ATTN_CHUNK, ATTN_CHUNK), :]

    cur = pl.BlockSpec((ATTN_CHUNK, LANES), lambda h, t: (nch - 1 - t, h))
    prev = pl.BlockSpec((ATTN_CHUNK, LANES), lambda h, t: (jnp.maximum(nch - 2 - t, 0), h))
    big = pltpu.VMEM((2 * ATTN_CHUNK, LANES), F32)
    return pl.pallas_call(
        body, name="attn_bwd", grid=(ATTN_W // LANES, nch),
        in_specs=[cur, prev, cur, prev, cur, cur, cur, cur],
        out_specs=[cur, cur, cur],
        out_shape=[_sds((s, ATTN_W))] * 3,
        scratch_shapes=[big, big, big, big, pltpu.VMEM((npat, ATTN_CHUNK, LANES), F32),
                        pltpu.VMEM((ATTN_CHUNK, LANES), F32)],
        compiler_params=_cp(("parallel", "arbitrary")),
    )(qn, kn, kn, v, v, o, lse, do)


def _discretize(lr, li, dt):
    mag = jnp.exp(lr * dt)
    abr = mag * jnp.cos(li * dt)
    abi = mag * jnp.sin(li * dt)
    den = lr * lr + li * li
    nr, ni = abr - 1.0, abi
    cr = (nr * lr + ni * li) / den
    ci = (ni * lr - nr * li) / den
    return abr, abi, den, nr, ni, cr, ci


def _ssm_discretize(a_re, a_im, log_dt, b_re_t, b_im_t):
    def body(ar_ref, ai_ref, ldt_ref, br_ref, bi_ref, abr_ref, abi_ref, bbr_ref, bbi_ref):
        abr, abi, _, _, _, cr, ci = _discretize(ar_ref[...], ai_ref[...], jnp.exp(ldt_ref[...]))
        br, bi = br_ref[...], bi_ref[...]
        abr_ref[...] = abr
        abi_ref[...] = abi
        bbr_ref[...] = cr * br - ci * bi
        bbi_ref[...] = cr * bi + ci * br

    return pl.pallas_call(
        body, name="ssm_discretize",
        out_shape=[_sds(a_re.shape)] * 2 + [_sds(b_re_t.shape)] * 2,
    )(a_re, a_im, log_dt, b_re_t, b_im_t)


def _ssm_discretize_bwd(a_re, a_im, log_dt, b_re_t, b_im_t, dabr, dabi, dbbr, dbbi):
    def body(ar_ref, ai_ref, ldt_ref, br_ref, bi_ref, dabr_ref, dabi_ref, dbbr_ref, dbbi_ref,
             dar_ref, dai_ref, dldt_ref, dbr_ref, dbi_ref):
        lr, li = ar_ref[...], ai_ref[...]
        dt = jnp.exp(ldt_ref[...])
        abr, abi, den, nr, ni, cr, ci = _discretize(lr, li, dt)
        br, bi = br_ref[...], bi_ref[...]
        gbr, gbi = dbbr_ref[...], dbbi_ref[...]
        dcr = jnp.sum(gbr * br + gbi * bi, axis=1, keepdims=True)
        dci = jnp.sum(gbi * br - gbr * bi, axis=1, keepdims=True)
        dbr_ref[...] = cr * gbr + ci * gbi
        dbi_ref[...] = cr * gbi - ci * gbr
        dnr = (dcr * lr - dci * li) / den
        dni = (dcr * li + dci * lr) / den
        dden = -(dcr * cr + dci * ci) / den
        dlr = (dcr * nr + dci * ni) / den + dden * 2.0 * lr
        dli = (dcr * ni - dci * nr) / den + dden * 2.0 * li
        gabr = dabr_ref[...] + dnr
        gabi = dabi_ref[...] + dni
        dphi = gabr * abr + gabi * abi
        dth = gabi * abr - gabr * abi
        dar_ref[...] = dlr + dphi * dt
        dai_ref[...] = dli + dth * dt
        dldt_ref[...] = jnp.sum(dphi * lr + dth * li, axis=2, keepdims=True) * dt

    return pl.pallas_call(
        body, name="ssm_discretize_bwd",
        out_shape=[_sds(a_re.shape)] * 2 + [_sds(log_dt.shape)] + [_sds(b_re_t.shape)] * 2,
    )(a_re, a_im, log_dt, b_re_t, b_im_t, dabr, dabi, dbbr, dbbi)


def _ssm_power(abr, abi, n_sq):
    def body(r_ref, i_ref, or_ref, oi_ref):
        r, i = r_ref[...], i_ref[...]
        for _ in range(n_sq):
            r, i = r * r - i * i, 2.0 * r * i
        or_ref[...] = r
        oi_ref[...] = i

    return pl.pallas_call(body, name="ssm_power", out_shape=[_sds(abr.shape)] * 2)(abr, abi)


N_CB = SSM_W // LANES
CB_STATES = N_STATE // N_CB
ROWS = N_SEG * SSM_LK


class _Neg:
    def __init__(self, ref):
        self.ref = ref

    def __getitem__(self, idx):
        return -self.ref[idx]


def _seg_init(fin_r, fin_i, pw_r, pw_i, x_r, x_i, reverse):
    zero = jnp.zeros((1, N_STATE), F32)
    cr, ci = zero, zero
    order = range(N_SEG - 1, -1, -1) if reverse else range(N_SEG)
    pr = pw_r[...]
    pi = -pw_i[...] if reverse else pw_i[...]
    for j in order:
        x_r[pl.ds(j, 1), :] = cr
        x_i[pl.ds(j, 1), :] = ci
        fr, fi = fin_r[pl.ds(j, 1), :], fin_i[pl.ds(j, 1), :]
        cr, ci = fr + pr * cr - pi * ci, fi + pr * ci + pi * cr


def _scan_rows(a_r, a_i, b_r, b_i, x_r, x_i, o_r, o_i, b_off, n_steps, reverse):
    w = 512
    for c in range(N_STATE // w):
        cols = pl.ds(c * w, w)
        ar = jnp.broadcast_to(a_r[:, cols], (N_SEG, w))
        ai = jnp.broadcast_to(a_i[:, cols], (N_SEG, w))

        def step(t, carry, cols=cols, ar=ar, ai=ai):
            xr, xi = carry
            k = (n_steps - 1 - t) if reverse else t
            rows = pl.ds(pl.multiple_of(k * N_SEG + b_off, N_SEG), N_SEG)
            nr = ar * xr - ai * xi + b_r[rows, cols]
            ni = ar * xi + ai * xr + b_i[rows, cols]
            o_r[rows, cols] = nr
            o_i[rows, cols] = ni
            return nr, ni

        xr, xi = lax.fori_loop(0, n_steps, step, (x_r[:, cols], x_i[:, cols]), unroll=4)
        x_r[:, cols] = xr
        x_i[:, cols] = xi


def _permute_in(src_ref, dst):
    for c in range(N_CB):
        for j in range(N_SEG):
            dst[c, pl.ds(j, SSM_LK, stride=N_SEG), :] = src_ref[j, :, pl.ds(c * LANES, LANES)]


def _permute_out(src, dst_ref):
    for c in range(N_CB):
        for j in range(N_SEG):
            dst_ref[j, :, pl.ds(c * LANES, LANES)] = src[c, pl.ds(j, SSM_LK, stride=N_SEG), :]


def _ssm_fwd(u3, abr, abi, pw_r, pw_i, fin_r, fin_i, bb_r, bb_i, cc_r, cc_i, dskip, finals_only):
    sl = u3.shape[1]
    nch = sl // SSM_LK

    def body(u_ref, abr_ref, abi_ref, pwr_ref, pwi_ref, finr_ref, fini_ref, bbr_ref, bbi_ref,
             ccr_ref, cci_ref, d_ref, *rest):
        if finals_only:
            xfr_ref, xfi_ref, up, xs_r, xs_i, x_r, x_i = rest
        else:
            y_ref, xsr_ref, xsi_ref, up, yp, xs_r, xs_i, x_r, x_i = rest
        k = pl.program_id(0)

        @pl.when(k == 0)
        def _():
            _seg_init(finr_ref, fini_ref, pwr_ref, pwi_ref, x_r, x_i, False)

        if not finals_only:
            xsr_ref[0] = x_r[...]
            xsi_ref[0] = x_i[...]
        _permute_in(u_ref, up)
        for c in range(N_CB):
            lhs = up[c].astype(BF16)
            xs_r[:, pl.ds(c * CB_STATES, CB_STATES)] = _dot(lhs, bbr_ref[c])
            xs_i[:, pl.ds(c * CB_STATES, CB_STATES)] = _dot(lhs, bbi_ref[c])
        _scan_rows(abr_ref, abi_ref, xs_r, xs_i, x_r, x_i, xs_r, xs_i, 0, SSM_LK, False)
        if finals_only:
            @pl.when(k == nch - 1)
            def _():
                xfr_ref[...] = x_r[...]
                xfi_ref[...] = x_i[...]
        else:
            for c in range(N_CB):
                cols = pl.ds(c * CB_STATES, CB_STATES)
                yp[c] = (_dot(xs_r[:, cols].astype(BF16), ccr_ref[c]) - _dot(xs_i[:, cols].astype(BF16), cci_ref[c])
                         + d_ref[:, pl.ds(c * LANES, LANES)] * up[c])
            _permute_out(yp, y_ref)

    ublk = pl.BlockSpec((N_SEG, SSM_LK, SSM_W), lambda k: (0, k, 0))
    st = pl.BlockSpec((1, N_SEG, N_STATE), lambda k: (k, 0, 0))
    vec = _full((1, N_STATE))
    mat = _full((N_SEG, N_STATE))
    chunk = pltpu.VMEM((N_CB, ROWS, LANES), F32)
    big = pltpu.VMEM((ROWS, N_STATE), F32)
    small = pltpu.VMEM((N_SEG, N_STATE), F32)
    if finals_only:
        out_specs, out_shape = [mat, mat], [_sds((N_SEG, N_STATE))] * 2
        scratch, name = [chunk, big, big, small, small], "ssm_fwd_finals"
    else:
        out_specs = [ublk, st, st]
        out_shape = [_sds(u3.shape)] + [_sds((nch, N_SEG, N_STATE))] * 2
        scratch, name = [chunk, chunk, big, big, small, small], "ssm_fwd"
    return pl.pallas_call(
        body, name=name, grid=(nch,),
        in_specs=[ublk, vec, vec, vec, vec, mat, mat,
                  _full((N_CB, LANES, CB_STATES)), _full((N_CB, LANES, CB_STATES)),
                  _full((N_CB, CB_STATES, LANES)), _full((N_CB, CB_STATES, LANES)), _full((1, SSM_W))],
        out_specs=out_specs, out_shape=out_shape, scratch_shapes=scratch,
        compiler_params=_cp(("arbitrary",)),
    )(u3, abr, abi, pw_r, pw_i, fin_r, fin_i, bb_r, bb_i, cc_r, cc_i, dskip)


def _ssm_bwd(u3, dy3, xst_r, xst_i, abr, abi, pw_r, pw_i, fin_r, fin_i, bb_r, bb_i, bbt_r, bbt_i,
             cct_r, cct_i, dskip, finals_only):
    sl = u3.shape[1]
    nch = sl // SSM_LK

    def body(u_ref, g_ref, xsr_ref, xsi_ref, abr_ref, abi_ref, pwr_ref, pwi_ref,
             finr_ref, fini_ref, bbr_ref, bbi_ref, btr_ref, bti_ref, ctr_ref, cti_ref, d_ref, *rest):
        if finals_only:
            lfr_ref, lfi_ref, gp, l_r, l_i, lam_r, lam_i = rest
        else:
            (du_ref, dar_ref, dai_ref, dbr_ref, dbi_ref, dcr_ref, dci_ref, dd_ref,
             gp, up, yp, l_r, l_i, lam_r, lam_i, x_r, x_i, xx_r, xx_i, sar, sai, sdd) = rest
        t = pl.program_id(0)

        @pl.when(t == 0)
        def _():
            _seg_init(finr_ref, fini_ref, pwr_ref, pwi_ref, lam_r, lam_i, True)
            if not finals_only:
                sar[...] = jnp.zeros_like(sar)
                sai[...] = jnp.zeros_like(sai)
                sdd[...] = jnp.zeros_like(sdd)
                dbr_ref[...] = jnp.zeros_like(dbr_ref)
                dbi_ref[...] = jnp.zeros_like(dbi_ref)
                dcr_ref[...] = jnp.zeros_like(dcr_ref)
                dci_ref[...] = jnp.zeros_like(dci_ref)

        _permute_in(g_ref, gp)
        for c in range(N_CB):
            cols = pl.ds(c * CB_STATES, CB_STATES)
            lhs = gp[c].astype(BF16)
            l_r[:, cols] = _dot(lhs, ctr_ref[c])
            l_i[:, cols] = -_dot(lhs, cti_ref[c])
        if not finals_only:
            _permute_in(u_ref, up)
            x_r[...] = xsr_ref[0]
            x_i[...] = xsi_ref[0]
            xx_r[pl.ds(0, N_SEG), :] = x_r[...]
            xx_i[pl.ds(0, N_SEG), :] = x_i[...]
            for c in range(N_CB):
                lhs = up[c].astype(BF16)
                xx_r[pl.ds(N_SEG, ROWS), pl.ds(c * CB_STATES, CB_STATES)] = _dot(lhs, bbr_ref[c])
                xx_i[pl.ds(N_SEG, ROWS), pl.ds(c * CB_STATES, CB_STATES)] = _dot(lhs, bbi_ref[c])
            _scan_rows(abr_ref, abi_ref, xx_r, xx_i, x_r, x_i, xx_r, xx_i, N_SEG, SSM_LK, False)
        _scan_rows(abr_ref, _Neg(abi_ref), l_r, l_i, lam_r, lam_i, l_r, l_i, 0, SSM_LK, True)
        if finals_only:
            @pl.when(t == nch - 1)
            def _():
                lfr_ref[...] = lam_r[...]
                lfi_ref[...] = lam_i[...]
        else:
            w = 512
            for cc in range(N_STATE // w):
                cols = pl.ds(cc * w, w)

                def acc_step(kk_, carry, cols=cols):
                    sr, si = carry
                    rows = pl.ds(pl.multiple_of(kk_ * N_SEG, N_SEG), N_SEG)
                    lr, li = l_r[rows, cols], l_i[rows, cols]
                    pr, pi = xx_r[rows, cols], xx_i[rows, cols]
                    return sr + lr * pr + li * pi, si + li * pr - lr * pi

                z = jnp.zeros((N_SEG, w), F32)
                sr, si = lax.fori_loop(0, SSM_LK, acc_step, (z, z), unroll=4)
                sar[:, cols] += sr
                sai[:, cols] += si
            for c in range(N_CB):
                cols = pl.ds(c * CB_STATES, CB_STATES)
                lrb = l_r[:, cols].astype(BF16)
                lib = l_i[:, cols].astype(BF16)
                ub = up[c].astype(BF16)
                gb = gp[c].astype(BF16)
                dbr_ref[c] += _dot_tn(lrb, ub)
                dbi_ref[c] += _dot_tn(lib, ub)
                dcr_ref[c] += _dot_tn(gb, xx_r[pl.ds(N_SEG, ROWS), cols].astype(BF16))
                dci_ref[c] += -_dot_tn(gb, xx_i[pl.ds(N_SEG, ROWS), cols].astype(BF16))
                yp[c] = _dot(lrb, btr_ref[c]) + _dot(lib, bti_ref[c]) + d_ref[:, pl.ds(c * LANES, LANES)] * gp[c]
                prod = gp[c] * up[c]
                sdd[:, pl.ds(c * LANES, LANES)] += jnp.sum(prod.reshape(SSM_LK, N_SEG, LANES), axis=0)
            _permute_out(yp, du_ref)

            @pl.when(t == nch - 1)
            def _():
                dar_ref[...] = jnp.sum(sar[...], axis=0, keepdims=True)
                dai_ref[...] = jnp.sum(sai[...], axis=0, keepdims=True)
                dd_ref[...] = jnp.sum(sdd[...], axis=0, keepdims=True)

    ublk = pl.BlockSpec((N_SEG, SSM_LK, SSM_W), lambda t: (0, nch - 1 - t, 0))
    st = pl.BlockSpec((1, N_SEG, N_STATE), lambda t: (nch - 1 - t, 0, 0))
    vec = _full((1, N_STATE))
    mat = _full((N_SEG, N_STATE))
    cs = _full((N_CB, LANES, CB_STATES))
    sc = _full((N_CB, CB_STATES, LANES))
    in_specs = [ublk, ublk, st, st, vec, vec, vec, vec, mat, mat, cs, cs, sc, sc, cs, cs, _full((1, SSM_W))]
    chunk = pltpu.VMEM((N_CB, ROWS, LANES), F32)
    big = pltpu.VMEM((ROWS, N_STATE), F32)
    small = pltpu.VMEM((N_SEG, N_STATE), F32)
    if finals_only:
        out_specs, out_shape = [mat, mat], [_sds((N_SEG, N_STATE))] * 2
        scratch, name = [chunk, big, big, small, small], "ssm_bwd_finals"
    else:
        out_specs = [ublk, vec, vec, sc, sc, cs, cs, _full((1, SSM_W))]
        out_shape = ([_sds(u3.shape), _sds((1, N_STATE)), _sds((1, N_STATE))]
                     + [_sds((N_CB, CB_STATES, LANES))] * 2 + [_sds((N_CB, LANES, CB_STATES))] * 2
                     + [_sds((1, SSM_W))])
        xx = pltpu.VMEM((ROWS + N_SEG, N_STATE), F32)
        scratch = [chunk, chunk, chunk, big, big, small, small, small, small, xx, xx, small, small,
                   pltpu.VMEM((N_SEG, SSM_W), F32)]
        name = "ssm_bwd"
    return pl.pallas_call(
        body, name=name, grid=(nch,), in_specs=in_specs, out_specs=out_specs, out_shape=out_shape,
        scratch_shapes=scratch, compiler_params=_cp(("arbitrary",)),
    )(u3, dy3, xst_r, xst_i, abr, abi, pw_r, pw_i, fin_r, fin_i, bb_r, bb_i, bbt_r, bbt_i, cct_r, cct_i, dskip)


def _row(tm, w):
    return pl.BlockSpec((tm, w), lambda i: (i, 0))


def _acc_rows(ref, rows, first):
    @pl.when(first)
    def _():
        ref[...] = jnp.zeros_like(ref)

    ref[...] += jnp.sum(rows, axis=0, keepdims=True)


def _fwd_mix(attn, y, x, glu_w, glu_b, ga, gs, w_out, g2, tm=512):
    s = x.shape[0]

    def body(a_ref, y_ref, x_ref, gw_ref, gb_ref, ga_ref, gs_ref, wo_ref, g2_ref, mix_ref, x2_ref, h_ref):
        a = a_ref[...]
        anb = ((a * _rms(a)) * ga_ref[...]).astype(BF16)
        z, _ = _gelu(y_ref[...])
        so = z * jax.nn.sigmoid(_dot(z.astype(BF16), gw_ref[...]) + gb_ref[...])
        snb = ((so * _rms(so)) * gs_ref[...]).astype(BF16)
        mix_ref[:, pl.ds(0, ATTN_W)] = anb
        mix_ref[:, pl.ds(ATTN_W, SSM_W)] = snb
        x2 = x_ref[...] + (_dot(anb, wo_ref[pl.ds(0, ATTN_W), :]) + _dot(snb, wo_ref[pl.ds(ATTN_W, SSM_W), :]))
        x2_ref[...] = x2
        h_ref[...] = ((x2 * _rms(x2)) * g2_ref[...]).astype(BF16)

    return pl.pallas_call(
        body, name="fwd_mix", grid=(s // tm,),
        in_specs=[_row(tm, ATTN_W), _row(tm, SSM_W), _row(tm, D_MODEL), _full((SSM_W, SSM_W)), _full((1, SSM_W)),
                  _full((1, ATTN_W)), _full((1, SSM_W)), _full((D_MODEL, D_MODEL)), _full((1, D_MODEL))],
        out_specs=[_row(tm, D_MODEL), _row(tm, D_MODEL), _row(tm, D_MODEL)],
        out_shape=[_sds((s, D_MODEL), BF16), _sds((s, D_MODEL)), _sds((s, D_MODEL), BF16)],
        compiler_params=_cp(("parallel",)),
    )(attn, y, x, glu_w, glu_b, ga, gs, w_out, g2)


def _mlp_up(h, w_up, tm=512, bn=1024):
    s = h.shape[0]

    def body(h_ref, w_ref, r_ref, hdn_ref):
        hv = h_ref[...]
        for j in range(D_FF // bn):
            cols = pl.ds(j * bn, bn)
            r = jnp.maximum(_dot(hv, w_ref[:, cols]), 0.0)
            r_ref[:, cols] = r.astype(BF16)
            hdn_ref[:, cols] = (r * r).astype(BF16)

    return pl.pallas_call(
        body, name="mlp_up", grid=(s // tm,),
        in_specs=[_row(tm, D_MODEL), _full((D_MODEL, D_FF))],
        out_specs=[_row(tm, D_FF), _row(tm, D_FF)], out_shape=[_sds((s, D_FF), BF16)] * 2,
        compiler_params=_cp(("parallel",)),
    )(h, w_up)


def _mlp_down_loss(hdn, w_down, x2, tgt, tm=512):
    s = x2.shape[0]

    def body(hdn_ref, w_ref, x2_ref, t_ref, dy_ref, dyb_ref, sse_ref):
        err = (x2_ref[...] + _dot(hdn_ref[...], w_ref[...])) - t_ref[...]
        dy = err * (1.0 / D_MODEL)
        dy_ref[...] = dy
        dyb_ref[...] = dy.astype(BF16)

        @pl.when(pl.program_id(0) == 0)
        def _():
            sse_ref[...] = jnp.zeros_like(sse_ref)

        sse_ref[...] += jnp.sum(jnp.sum(err * err, axis=0, keepdims=True), axis=1, keepdims=True)

    return pl.pallas_call(
        body, name="mlp_down_loss", grid=(s // tm,),
        in_specs=[_row(tm, D_FF), _full((D_FF, D_MODEL)), _row(tm, D_MODEL), _row(tm, D_MODEL)],
        out_specs=[_row(tm, D_MODEL), _row(tm, D_MODEL), _full((1, 1))],
        out_shape=[_sds((s, D_MODEL)), _sds((s, D_MODEL), BF16), _sds((1, 1))],
        compiler_params=_cp(("arbitrary",)),
    )(hdn, w_down, x2, tgt)


def _mlp_down_bwd(dyb, w_down_t, r, tm=512, bn=1024):
    s = dyb.shape[0]

    def body(dy_ref, w_ref, r_ref, dup_ref):
        dyv = dy_ref[...]
        for j in range(D_FF // bn):
            cols = pl.ds(j * bn, bn)
            dup_ref[:, cols] = (_dot(dyv, w_ref[:, cols]) * (2.0 * r_ref[:, cols].astype(F32))).astype(BF16)

    return pl.pallas_call(
        body, name="mlp_down_bwd", grid=(s // tm,),
        in_specs=[_row(tm, D_MODEL), _full((D_MODEL, D_FF)), _row(tm, D_FF)],
        out_specs=_row(tm, D_FF), out_shape=_sds((s, D_FF), BF16),
        compiler_params=_cp(("parallel",)),
    )(dyb, w_down_t, r)


def _mlp_up_bwd(dup, w_up_t, x2, g2, dy, tm=512):
    s = x2.shape[0]

    def body(dup_ref, w_ref, x2_ref, g2_ref, dy_ref, dx2_ref, dx2b_ref, dg_ref):
        dx, dg_rows = _rms_bwd(_dot(dup_ref[...], w_ref[...]), x2_ref[...], g2_ref[...], D_MODEL)
        dx2 = dy_ref[...] + dx
        dx2_ref[...] = dx2
        dx2b_ref[...] = dx2.astype(BF16)
        _acc_rows(dg_ref, dg_rows, pl.program_id(0) == 0)

    return pl.pallas_call(
        body, name="mlp_up_bwd", grid=(s // tm,),
        in_specs=[_row(tm, D_FF), _full((D_FF, D_MODEL)), _row(tm, D_MODEL), _full((1, D_MODEL)), _row(tm, D_MODEL)],
        out_specs=[_row(tm, D_MODEL), _row(tm, D_MODEL), _full((1, D_MODEL))],
        out_shape=[_sds((s, D_MODEL)), _sds((s, D_MODEL), BF16), _sds((1, D_MODEL))],
        compiler_params=_cp(("arbitrary",)),
    )(dup, w_up_t, x2, g2, dy)


def _mix_bwd(dx2b, w_out_t, attn, y, glu_w, glu_b, glu_w_t, ga, gs, tm=512):
    s = attn.shape[0]

    def body(dx2_ref, wot_ref, a_ref, y_ref, gw_ref, gb_ref, gwt_ref, ga_ref, gs_ref,
             da_ref, dys_ref, z_ref, dpre_ref, dga_ref, dgs_ref, dgb_ref):
        first = pl.program_id(0) == 0
        dmix = _dot(dx2_ref[...], wot_ref[...])
        da, dga_rows = _rms_bwd(dmix[:, :ATTN_W], a_ref[...], ga_ref[...], ATTN_W)
        da_ref[...] = da
        yv = y_ref[...]
        z, t = _gelu(yv)
        gate = jax.nn.sigmoid(_dot(z.astype(BF16), gw_ref[...]) + gb_ref[...])
        dso, dgs_rows = _rms_bwd(dmix[:, ATTN_W:], z * gate, gs_ref[...], SSM_W)
        dpre = dso * z * gate * (1.0 - gate)
        dpre_b = dpre.astype(BF16)
        dz = dso * gate + _dot(dpre_b, gwt_ref[...])
        dgelu = 0.5 * (1.0 + t) + 0.5 * yv * (1.0 - t * t) * (GELU_C * (1.0 + 3.0 * 0.044715 * (yv * yv)))
        dys_ref[...] = dz * dgelu
        z_ref[...] = z.astype(BF16)
        dpre_ref[...] = dpre_b
        _acc_rows(dga_ref, dga_rows, first)
        _acc_rows(dgs_ref, dgs_rows, first)
        _acc_rows(dgb_ref, dpre, first)

    vec = _full((1, SSM_W))
    return pl.pallas_call(
        body, name="mix_bwd", grid=(s // tm,),
        in_specs=[_row(tm, D_MODEL), _full((D_MODEL, D_MODEL)), _row(tm, ATTN_W), _row(tm, SSM_W),
                  _full((SSM_W, SSM_W)), vec, _full((SSM_W, SSM_W)), vec, vec],
        out_specs=[_row(tm, ATTN_W), _row(tm, SSM_W), _row(tm, SSM_W), _row(tm, SSM_W), vec, vec, vec],
        out_shape=[_sds((s, ATTN_W)), _sds((s, SSM_W)), _sds((s, SSM_W), BF16), _sds((s, SSM_W), BF16),
                   _sds((1, ATTN_W)), _sds((1, SSM_W)), _sds((1, SSM_W))],
        compiler_params=_cp(("arbitrary",)),
    )(dx2b, w_out_t, attn, y, glu_w, glu_b, glu_w_t, ga, gs)


def _qk_bwd(dqn, dkn, qk, dv, du, gq, gk, gmat, tm=512):
    s = qk.shape[0]

    def body(dq_ref, dk_ref, qk_ref, dv_ref, du_ref, gq_ref, gk_ref, gm_ref, dp_ref, dgq_ref, dgk_ref):
        first = pl.program_id(0) == 0
        gm = gm_ref[...]
        for idx, (d_ref, g_ref, dg_ref) in enumerate(((dq_ref, gq_ref, dgq_ref), (dk_ref, gk_ref, dgk_ref))):
            xv = qk_ref[:, pl.ds(idx * ATTN_W, ATTN_W)]
            dyv = d_ref[...]
            r = lax.rsqrt(_group_mean(xv * xv, gm) + EPS)
            gdy = dyv * g_ref[...]
            dx = r * gdy - xv * (r * r * r) * _group_mean(gdy * xv, gm)
            dp_ref[:, pl.ds(idx * ATTN_W, ATTN_W)] = dx.astype(BF16)
            _acc_rows(dg_ref, dyv * (xv * r), first)
        dp_ref[:, pl.ds(2 * ATTN_W, ATTN_W)] = dv_ref[...].astype(BF16)
        dp_ref[:, pl.ds(3 * ATTN_W, SSM_W)] = du_ref[...].astype(BF16)

    vec = _full((1, ATTN_W))
    return pl.pallas_call(
        body, name="qk_bwd", grid=(s // tm,),
        in_specs=[_row(tm, ATTN_W), _row(tm, ATTN_W), _row(tm, 2 * ATTN_W), _row(tm, ATTN_W), _row(tm, SSM_W),
                  vec, vec, _full((ATTN_W, ATTN_W))],
        out_specs=[_row(tm, 4 * ATTN_W), vec, vec],
        out_shape=[_sds((s, 4 * ATTN_W), BF16), _sds((1, ATTN_W)), _sds((1, ATTN_W))],
        compiler_params=_cp(("arbitrary",)),
    )(dqn, dkn, qk, dv, du, gq, gk, gmat)


def _in_bwd(dproj, w_in_t, x, g1, dx2, ex=None, tm=512):
    s = x.shape[0]
    steps = s // tm

    def body(*refs):
        (dp_ref, w_ref, x_ref, g1_ref, dx2_ref), (gx_ref, dg_ref), _, hx = _carry(ex, refs, 5, 2)
        if ex is not None:
            @pl.when(pl.program_id(0) == 0)
            def _():
                ex.start(*hx)

        dx, dg_rows = _rms_bwd(_dot(dp_ref[...], w_ref[...]), x_ref[...], g1_ref[...], D_MODEL)
        gx_ref[...] = dx2_ref[...] + dx
        _acc_rows(dg_ref, dg_rows, pl.program_id(0) == 0)
        if ex is not None:
            @pl.when(pl.program_id(0) == steps - 1)
            def _():
                ex.wait(*hx)

    hosted = ex is not None
    return pl.pallas_call(
        body, name="in_bwd", grid=(steps,),
        in_specs=[_row(tm, 4 * ATTN_W), _full((4 * ATTN_W, D_MODEL)), _row(tm, D_MODEL), _full((1, D_MODEL)),
                  _row(tm, D_MODEL)] + (ex.specs() if hosted else []),
        out_specs=[_row(tm, D_MODEL), _full((1, D_MODEL))] + (ex.specs() if hosted else []),
        out_shape=[_sds((s, D_MODEL)), _sds((1, D_MODEL))] + (ex.out_shape() if hosted else []),
        scratch_shapes=ex.scratch() if hosted else [],
        compiler_params=_cp(("arbitrary",)),
    )(dproj, w_in_t, x, g1, dx2, *(ex.srcs if hosted else []))


def _mm_tn(a, b, name, ts=1024):
    s, k = a.shape
    n = b.shape[1]
    bk, bn = min(k, 1024), min(n, 1024)

    def body(a_ref, b_ref, o_ref):
        @pl.when(pl.program_id(2) == 0)
        def _():
            o_ref[...] = jnp.zeros_like(o_ref)

        o_ref[...] += _dot_tn(a_ref[...], b_ref[...])

    return pl.pallas_call(
        body, name=name, grid=(k // bk, n // bn, s // ts),
        in_specs=[pl.BlockSpec((ts, bk), lambda i, j, t: (t, i)), pl.BlockSpec((ts, bn), lambda i, j, t: (t, j))],
        out_specs=pl.BlockSpec((bk, bn), lambda i, j, t: (i, j)), out_shape=_sds((k, n)),
        compiler_params=_cp(("parallel", "parallel", "arbitrary")),
    )(a, b)


def _peer(k):
    x, y, c = lax.axis_index("x"), lax.axis_index("y"), lax.axis_index("c")
    px = 1 - x if k & 4 else x
    py = 1 - y if k & 2 else y
    pc = 1 - c if k & 1 else c
    return (px, py, pc), 4 * px + 2 * py + pc


def _gather_rows(x_shard):
    m_per, n = x_shard.shape

    def body(x_ref, out_ref, send_sems, recv_sems, local_sem):
        x, y, c = lax.axis_index("x"), lax.axis_index("y"), lax.axis_index("c")
        me, sibling = (x, y, c), (x, y, 1 - c)
        chips = [(1 - x, y), (x, 1 - y), (1 - x, 1 - y)]

        def rows(px, py, pc):
            return out_ref.at[pl.ds((4 * px + 2 * py + pc) * m_per, m_per), :]

        def copy(k, block, to, src=None):
            return pltpu.make_async_remote_copy(
                src_ref=rows(*block) if src is None else src, dst_ref=rows(*block),
                send_sem=send_sems.at[k], recv_sem=recv_sems.at[k], device_id=to, device_id_type=MESH)

        mine = pltpu.make_async_copy(x_ref, rows(*me), local_sem)
        mine.start()
        first = [copy(0, me, sibling, src=x_ref)]
        first += [copy(1 + j, me, (*chip, c), src=x_ref) for j, chip in enumerate(chips)]
        for cp in first:
            cp.start()
        passed = [copy(4 + j, (*chip, c), sibling) for j, chip in enumerate(chips)]
        for j, chip in enumerate(chips):
            copy(1 + j, (*chip, c), me).wait_recv()
            passed[j].start()
        copy(0, sibling, me).wait_recv()
        for j, chip in enumerate(chips):
            copy(4 + j, (*chip, 1 - c), me).wait_recv()
        for cp in first + passed:
            cp.wait_send()
        mine.wait()

    return pl.pallas_call(
        body, name="gather_weights", out_shape=_sds((N_DEV * m_per, n), x_shard.dtype),
        in_specs=[pl.BlockSpec(memory_space=pltpu.VMEM)], out_specs=pl.BlockSpec(memory_space=pltpu.VMEM),
        scratch_shapes=[pltpu.SemaphoreType.DMA((7,)), pltpu.SemaphoreType.DMA((7,)), pltpu.SemaphoreType.DMA],
        compiler_params=pltpu.CompilerParams(vmem_limit_bytes=VMEM_LIMIT),
    )(x_shard)


class _Exchange:
    def __init__(self, srcs, whole):
        self.srcs, self.whole, self.n = list(srcs), list(whole), len(srcs)
        self.rows = [a.shape[0] if w else a.shape[0] // N_DEV for a, w in zip(self.srcs, self.whole)]

    def specs(self):
        return [pl.BlockSpec(memory_space=pl.ANY)] * self.n

    def out_shape(self):
        return [_sds((N_DEV, r, a.shape[1]), a.dtype) for r, a in zip(self.rows, self.srcs)]

    def scratch(self):
        return [pltpu.SemaphoreType.DMA((self.n * 7,)), pltpu.SemaphoreType.DMA((self.n * 7,)),
                pltpu.SemaphoreType.DMA((self.n,))]

    def _copies(self, ins, outs, sems):
        send_sems, recv_sems, local_sems = sems
        _, me = _peer(0)
        for w in range(self.n):
            for k in range(N_DEV):
                peer, pidx = _peer(k)
                src = ins[w] if self.whole[w] else ins[w].at[pl.ds(pidx * self.rows[w], self.rows[w]), :]
                if k == 0:
                    yield k, pltpu.make_async_copy(src, outs[w].at[me], local_sems.at[w]), None
                else:
                    sem = w * 7 + k - 1
                    out = pltpu.make_async_remote_copy(src_ref=src, dst_ref=outs[w].at[me], send_sem=send_sems.at[sem],
                                                       recv_sem=recv_sems.at[sem], device_id=peer, device_id_type=MESH)
                    back = pltpu.make_async_remote_copy(src_ref=src, dst_ref=outs[w].at[pidx], send_sem=send_sems.at[sem],
                                                        recv_sem=recv_sems.at[sem], device_id=peer, device_id_type=MESH)
                    yield k, out, back

    def start(self, ins, outs, sems):
        for _, out, _ in self._copies(ins, outs, sems):
            out.start()

    def wait(self, ins, outs, sems):
        for k, out, back in self._copies(ins, outs, sems):
            if k == 0:
                out.wait()
            else:
                back.wait_recv()
                out.wait_send()


def _carry(ex, refs, n_in, n_out):
    nh = ex.n if ex is not None else 0
    ins, hin = refs[:n_in], refs[n_in:n_in + nh]
    outs = refs[n_in + nh:n_in + nh + n_out]
    hout = refs[n_in + nh + n_out:n_in + 2 * nh + n_out]
    rest = refs[n_in + 2 * nh + n_out:]
    if ex is None:
        return ins, outs, rest, None
    return ins, outs, rest[:-3], (hin, hout, rest[-3:])


def _exchange_now(srcs, whole, name):
    ex = _Exchange(srcs, whole)

    def body(*refs):
        _, _, _, (hin, hout, sems) = _carry(ex, refs, 0, 0)
        ex.start(hin, hout, sems)
        ex.wait(hin, hout, sems)

    return pl.pallas_call(body, name=name, out_shape=ex.out_shape(), in_specs=ex.specs(), out_specs=ex.specs(),
                          scratch_shapes=ex.scratch())(*srcs)


def _adamw(w, m, v, gparts, name):
    r, c = w.shape
    tr = r if r * c <= 256 * 1024 else 128 * 1024 // c

    def body(w_ref, m_ref, v_ref, g_ref, go_ref, d_ref, mo_ref, vo_ref):
        g = g_ref[0].astype(F32)
        for i in range(1, N_DEV):
            g = g + g_ref[i].astype(F32)
        nm = ADAM_B1 * m_ref[...] + (1.0 - ADAM_B1) * g
        nv = ADAM_B2 * v_ref[...] + (1.0 - ADAM_B2) * (g * g)
        m_hat = nm / (1.0 - ADAM_B1 ** ADAM_STEP)
        v_hat = nv / (1.0 - ADAM_B2 ** ADAM_STEP)
        go_ref[...] = g
        d_ref[...] = -ADAM_LR * (m_hat / (jnp.sqrt(v_hat) + ADAM_EPS) + ADAM_WD * w_ref[...])
        mo_ref[...] = nm
        vo_ref[...] = nv

    blk = pl.BlockSpec((tr, c), lambda i: (i, 0))
    return pl.pallas_call(
        body, name=name, grid=(r // tr,),
        in_specs=[blk, blk, blk, pl.BlockSpec((N_DEV, tr, c), lambda i: (0, i, 0))],
        out_specs=[blk] * 4, out_shape=[_sds((r, c))] * 4,
        compiler_params=_cp(("parallel",)),
    )(w, m, v, gparts)


def _block_diag(a, states_first):
    a4 = a.reshape(N_CB, 8, SSM_GROUP, SSM_STATE)
    eye = jnp.eye(8, dtype=a.dtype)
    if states_first:
        return jnp.einsum("bgcp,gh->bgphc", a4, eye).reshape(N_CB, CB_STATES, LANES)
    return jnp.einsum("bgcp,gh->bgchp", a4, eye).reshape(N_CB, LANES, CB_STATES)


def _block_diag_of(full, states_first):
    if states_first:
        picked = jnp.einsum("bgphc,gh->bgcp", full.reshape(N_CB, 8, SSM_STATE, 8, SSM_GROUP), jnp.eye(8, dtype=full.dtype))
    else:
        picked = jnp.einsum("bgchp,gh->bgcp", full.reshape(N_CB, 8, SSM_GROUP, 8, SSM_STATE), jnp.eye(8, dtype=full.dtype))
    return picked.reshape(SSM_GROUPS, SSM_GROUP, SSM_STATE)


SMALL_EARLY = ("ssm_a_re", "ssm_a_im", "ssm_log_dt", "ssm_b_re", "ssm_b_im", "ssm_c_re", "ssm_c_im", "ssm_d", "glu_b",
               "attn_out_norm_g", "ssm_out_norm_g", "norm2_g")
SMALL_MID = ("q_norm_g", "k_norm_g")
SMALL_LATE = ("norm1_g",)
SMALL = SMALL_EARLY + SMALL_MID + SMALL_LATE


def _pack_small(arrs):
    parts = []
    for a in arrs:
        flat = a.reshape(-1)
        rows = -(-flat.shape[0] // (8 * LANES)) * 8
        parts.append(jnp.pad(flat, (0, rows * LANES - flat.shape[0])).reshape(rows, LANES))
    return jnp.concatenate(parts, axis=0)


def _unpack_small(packed, shapes):
    out, r0 = [], 0
    for shp in shapes:
        size = math.prod(shp)
        rows = -(-size // (8 * LANES)) * 8
        out.append(packed[r0:r0 + rows].reshape(-1)[:size].reshape(shp))
        r0 += rows
    return out


def kernel(x, norm1_g, w_in, q_norm_g, k_norm_g, ssm_a_re, ssm_a_im, ssm_log_dt, ssm_b_re, ssm_b_im, ssm_c_re, ssm_c_im, ssm_d, glu_w, glu_b, attn_out_norm_g, ssm_out_norm_g, w_out, norm2_g, w_mlp_up, w_mlp_down, loss_target, m_norm1_g, m_w_in, m_q_norm_g, m_k_norm_g, m_ssm_a_re, m_ssm_a_im, m_ssm_log_dt, m_ssm_b_re, m_ssm_b_im, m_ssm_c_re, m_ssm_c_im, m_ssm_d, m_glu_w, m_glu_b, m_attn_out_norm_g, m_ssm_out_norm_g, m_w_out, m_norm2_g, m_w_mlp_up, m_w_mlp_down, v_norm1_g, v_w_in, v_q_norm_g, v_k_norm_g, v_ssm_a_re, v_ssm_a_im, v_ssm_log_dt, v_ssm_b_re, v_ssm_b_im, v_ssm_c_re, v_ssm_c_im, v_ssm_d, v_glu_w, v_glu_b, v_attn_out_norm_g, v_ssm_out_norm_g, v_w_out, v_norm2_g, v_w_mlp_up, v_w_mlp_down):
    weights = dict(norm1_g=norm1_g, w_in=w_in, q_norm_g=q_norm_g, k_norm_g=k_norm_g, ssm_a_re=ssm_a_re,
                   ssm_a_im=ssm_a_im, ssm_log_dt=ssm_log_dt, ssm_b_re=ssm_b_re, ssm_b_im=ssm_b_im,
                   ssm_c_re=ssm_c_re, ssm_c_im=ssm_c_im, ssm_d=ssm_d, glu_w=glu_w, glu_b=glu_b,
                   attn_out_norm_g=attn_out_norm_g, ssm_out_norm_g=ssm_out_norm_g, w_out=w_out, norm2_g=norm2_g,
                   w_mlp_up=w_mlp_up, w_mlp_down=w_mlp_down)
    mom_m = dict(norm1_g=m_norm1_g, w_in=m_w_in, q_norm_g=m_q_norm_g, k_norm_g=m_k_norm_g, ssm_a_re=m_ssm_a_re,
                 ssm_a_im=m_ssm_a_im, ssm_log_dt=m_ssm_log_dt, ssm_b_re=m_ssm_b_re, ssm_b_im=m_ssm_b_im,
                 ssm_c_re=m_ssm_c_re, ssm_c_im=m_ssm_c_im, ssm_d=m_ssm_d, glu_w=m_glu_w, glu_b=m_glu_b,
                 attn_out_norm_g=m_attn_out_norm_g, ssm_out_norm_g=m_ssm_out_norm_g, w_out=m_w_out,
                 norm2_g=m_norm2_g, w_mlp_up=m_w_mlp_up, w_mlp_down=m_w_mlp_down)
    mom_v = dict(norm1_g=v_norm1_g, w_in=v_w_in, q_norm_g=v_q_norm_g, k_norm_g=v_k_norm_g, ssm_a_re=v_ssm_a_re,
                 ssm_a_im=v_ssm_a_im, ssm_log_dt=v_ssm_log_dt, ssm_b_re=v_ssm_b_re, ssm_b_im=v_ssm_b_im,
                 ssm_c_re=v_ssm_c_re, ssm_c_im=v_ssm_c_im, ssm_d=v_ssm_d, glu_w=v_glu_w, glu_b=v_glu_b,
                 attn_out_norm_g=v_attn_out_norm_g, ssm_out_norm_g=v_ssm_out_norm_g, w_out=v_w_out,
                 norm2_g=v_norm2_g, w_mlp_up=v_w_mlp_up, w_mlp_down=v_w_mlp_down)
    order = list(weights)

    xs, tgt = x[0], loss_target[0]
    s = xs.shape[0]
    assert s % ATTN_CHUNK == 0 and (s // N_SEG) % SSM_LK == 0
    seg_len = s // N_SEG
    n_sq = seg_len.bit_length() - 1
    assert 1 << n_sq == seg_len

    w_in_t = _gather_rows(w_in[0].T.astype(BF16))
    w_in_full = w_in_t.T
    later = _Exchange([glu_w[0].astype(BF16), w_out[0].astype(BF16), w_mlp_up[0].T.astype(BF16),
                       w_mlp_down[0].astype(BF16)], [True] * 4)

    gq = jnp.tile(q_norm_g[0], ATTN_W // HEAD_DIM)[None]
    gk = jnp.tile(k_norm_g[0], ATTN_W // HEAD_DIM)[None]
    lane = jnp.arange(ATTN_W) // HEAD_DIM
    gmat = jnp.where(lane[:, None] == lane[None, :], 1.0 / HEAD_DIM, 0.0).astype(BF16)
    a_re3 = ssm_a_re[0][:, None, :]
    a_im3 = ssm_a_im[0][:, None, :]
    ldt3 = ssm_log_dt[0][:, None, None]
    b_re_t = jnp.swapaxes(ssm_b_re[0], 1, 2)
    b_im_t = jnp.swapaxes(ssm_b_im[0], 1, 2)
    c_re, c_im = ssm_c_re[0], ssm_c_im[0]
    dskip = ssm_d[0].reshape(1, SSM_W)

    qk, qn, kn, v, u, xn = _fwd_proj(xs, norm1_g, w_in_full, gq, gk, gmat)
    attn, lse, glu_g, w_out_g, w_up_g, w_down_g = _attn_fwd(qn, kn, v, later)
    glu_full = glu_g.reshape(SSM_W, SSM_W)
    w_out_full = w_out_g.reshape(D_MODEL, D_MODEL)
    w_up_t = w_up_g.reshape(D_FF, D_MODEL)
    w_down_full = w_down_g.reshape(D_FF, D_MODEL)
    w_up_full, w_out_t, w_down_t, glu_t = w_up_t.T, w_out_full.T, w_down_full.T, glu_full.T

    abr3, abi3, bbr, bbi = _ssm_discretize(a_re3, a_im3, ldt3, b_re_t, b_im_t)
    abr, abi = abr3.reshape(1, N_STATE), abi3.reshape(1, N_STATE)
    pw_r, pw_i = _ssm_power(abr, abi, n_sq)
    bb_r, bb_i = _block_diag(bbr, False).astype(BF16), _block_diag(bbi, False).astype(BF16)
    bbt_r, bbt_i = _block_diag(bbr, True).astype(BF16), _block_diag(bbi, True).astype(BF16)
    cc_r, cc_i = _block_diag(c_re, True).astype(BF16), _block_diag(c_im, True).astype(BF16)
    cct_r, cct_i = _block_diag(c_re, False).astype(BF16), _block_diag(c_im, False).astype(BF16)
    u3 = u.reshape(N_SEG, seg_len, SSM_W)
    zero_fin = jnp.zeros((N_SEG, N_STATE), F32)
    ssm_args = (abr, abi, pw_r, pw_i)
    xf_r, xf_i = _ssm_fwd(u3, *ssm_args, zero_fin, zero_fin, bb_r, bb_i, cc_r, cc_i, dskip, True)
    y3, xst_r, xst_i = _ssm_fwd(u3, *ssm_args, xf_r, xf_i, bb_r, bb_i, cc_r, cc_i, dskip, False)
    y = y3.reshape(s, SSM_W)

    mix, x2, h = _fwd_mix(attn, y, xs, glu_full, glu_b, attn_out_norm_g, ssm_out_norm_g, w_out_full, norm2_g)
    r_act, hdn = _mlp_up(h, w_up_full)
    dy, dyb, sse = _mlp_down_loss(hdn, w_down_full, x2, tgt)
    loss = lax.psum(0.5 * sse[0, 0] / D_MODEL, ("x", "y", "c"))

    dup = _mlp_down_bwd(dyb, w_down_t, r_act)
    g_w_down = _mm_tn(hdn, dyb, "grad_w_down")
    dx2, dx2b, g_norm2 = _mlp_up_bwd(dup, w_up_t, x2, norm2_g, dy)
    g_w_up_t = _mm_tn(dup, h, "grad_w_up_t")
    dattn, dys, z_b, dpre_b, g_ga, g_gs, g_glu_b = _mix_bwd(dx2b, w_out_t, attn, y, glu_full, glu_b, glu_t,
                                                             attn_out_norm_g, ssm_out_norm_g)
    g_w_out = _mm_tn(mix, dx2b, "grad_w_out")
    g_glu_w = _mm_tn(z_b, dpre_b, "grad_glu_w")
    dy3 = dys.reshape(N_SEG, seg_len, SSM_W)
    bwd_args = (u3, dy3, xst_r, xst_i, abr, abi, pw_r, pw_i)
    lf_r, lf_i = _ssm_bwd(*bwd_args, zero_fin, zero_fin, bb_r, bb_i, bbt_r, bbt_i, cct_r, cct_i, dskip, True)
    du3, dab_r, dab_i, dbb_r, dbb_i, dcc_r, dcc_i, g_d = _ssm_bwd(*bwd_args, lf_r, lf_i, bb_r, bb_i, bbt_r, bbt_i,
                                                                 cct_r, cct_i, dskip, False)
    g_a_re3, g_a_im3, g_ldt3, g_b_re_t, g_b_im_t = _ssm_discretize_bwd(
        a_re3, a_im3, ldt3, b_re_t, b_im_t, dab_r.reshape(a_re3.shape), dab_i.reshape(a_re3.shape),
        _block_diag_of(dbb_r, True), _block_diag_of(dbb_i, True))
    g_c_re, g_c_im = _block_diag_of(dcc_r, False), _block_diag_of(dcc_i, False)
    small_grads = dict(
        ssm_a_re=g_a_re3.reshape(ssm_a_re.shape), ssm_a_im=g_a_im3.reshape(ssm_a_im.shape),
        ssm_log_dt=g_ldt3.reshape(ssm_log_dt.shape), ssm_b_re=jnp.swapaxes(g_b_re_t, 1, 2)[None],
        ssm_b_im=jnp.swapaxes(g_b_im_t, 1, 2)[None], ssm_c_re=g_c_re[None], ssm_c_im=g_c_im[None],
        ssm_d=g_d.reshape(ssm_d.shape), glu_b=g_glu_b, attn_out_norm_g=g_ga, ssm_out_norm_g=g_gs, norm2_g=g_norm2)

    early = _Exchange([g.astype(BF16) for g in (g_glu_w, g_w_out, g_w_up_t, g_w_down)]
                      + [_pack_small([small_grads[n] for n in SMALL_EARLY])], [False] * 4 + [True])
    dqn, dkn, dv, p_glu, p_w_out, p_w_up, p_w_down, p_early = _attn_bwd(qn, kn, v, attn, lse, dattn, early)

    dproj, g_gq, g_gk = _qk_bwd(dqn, dkn, qk, dv, du3.reshape(s, SSM_W), gq, gk, gmat)
    g_w_in_t = _mm_tn(dproj, xn, "grad_w_in_t")
    small_grads["q_norm_g"] = g_gq.reshape(ATTN_W // HEAD_DIM, HEAD_DIM).sum(0)[None]
    small_grads["k_norm_g"] = g_gk.reshape(ATTN_W // HEAD_DIM, HEAD_DIM).sum(0)[None]
    mid = _Exchange([g_w_in_t.astype(BF16), _pack_small([small_grads[n] for n in SMALL_MID])], [False, True])
    grad_x, g_norm1, p_w_in, p_mid = _in_bwd(dproj, w_in_t, xs, norm1_g, dx2, mid)
    (p_late,) = _exchange_now([_pack_small([g_norm1])], [True], "exchange_norm1")

    transposed = {"w_in", "w_mlp_up"}
    res = {}
    for name, gp in (("w_in", p_w_in), ("glu_w", p_glu), ("w_out", p_w_out), ("w_mlp_up", p_w_up), ("w_mlp_down", p_w_down)):
        tr = (lambda a: a.T) if name in transposed else (lambda a: a)
        outs = _adamw(tr(weights[name][0]), tr(mom_m[name][0]), tr(mom_v[name][0]), gp, "adamw_" + name)
        res[name] = [tr(o)[None] for o in outs]
    outs = _adamw(_pack_small([weights[n] for n in SMALL]), _pack_small([mom_m[n] for n in SMALL]),
                  _pack_small([mom_v[n] for n in SMALL]), jnp.concatenate([p_early, p_mid, p_late], axis=1), "adamw_small")
    shapes = [weights[n].shape for n in SMALL]
    unpacked = [_unpack_small(o, shapes) for o in outs]
    for i, n in enumerate(SMALL):
        res[n] = [unpacked[j][i] for j in range(4)]

    return (loss, grad_x[None], *[res[n][0] for n in order], *[res[n][1] for n in order],
            *[res[n][2] for n in order], *[res[n][3] for n in order])
```

```python
import math

import jax
import jax.numpy as jnp
from jax import lax
from jax.experimental import pallas as pl
from jax.experimental.pallas import tpu as pltpu

F32 = jnp.float32
BF16 = jnp.bfloat16

D_MODEL = 1024
ATTN_W = 512
HEAD_DIM = 64
SSM_W = 512
SSM_GROUP = 16
SSM_GROUPS = 32
SSM_STATE = 64
N_STATE = SSM_GROUPS * SSM_STATE
D_FF = 4096
EPS = 1e-6
NEG_INF = -1e30
ATTN_CHUNK = 2048
ATTN_BLOCK = 128
DILATIONS = (1, 4, 16)
N_SEG = 8
SSM_LK = 64
N_DEV = 8
LANES = 128

ADAM_LR = 0.001
ADAM_B1 = 0.9
ADAM_B2 = 0.999
ADAM_EPS = 1e-08
ADAM_WD = 0.01
ADAM_STEP = 10

VMEM_LIMIT = 56 * 1024 * 1024
GELU_C = math.sqrt(2.0 / math.pi)
MESH = pl.DeviceIdType.MESH


def _cp(sem, vmem=VMEM_LIMIT):
    return pltpu.CompilerParams(dimension_semantics=sem, vmem_limit_bytes=vmem)


def _dot(a, b):
    return jnp.dot(a, b, preferred_element_type=F32)


def _dot_nt(a, b):
    return lax.dot_general(a, b, (((1,), (1,)), ((), ())), preferred_element_type=F32)


def _dot_tn(a, b):
    return lax.dot_general(a, b, (((0,), (0,)), ((), ())), preferred_element_type=F32)


def _group_mean(x2, gmat):
    hi = x2.astype(BF16)
    lo = (x2 - hi.astype(F32)).astype(BF16)
    return _dot(hi, gmat) + _dot(lo, gmat)


def _rms(x):
    return lax.rsqrt(jnp.mean(x * x, axis=-1, keepdims=True) + EPS)


def _rms_bwd(dy, x, g, n):
    r = _rms(x)
    gdy = dy * g
    dx = r * gdy - x * (r * r * r) * (jnp.sum(gdy * x, axis=-1, keepdims=True) / n)
    return dx, dy * (x * r)


def _gelu(y):
    t = jnp.tanh(GELU_C * (y + 0.044715 * (y * y * y)))
    return 0.5 * y * (1.0 + t), t


def _full(shape):
    nd = len(shape)
    return pl.BlockSpec(shape, lambda *_: (0,) * nd)


def _sds(shape, dtype=F32):
    return jax.ShapeDtypeStruct(shape, dtype)


def _fwd_proj(x, g1, w_in, gq, gk, gmat, tm=512):
    s = x.shape[0]

    def body(x_ref, g1_ref, w_ref, gq_ref, gk_ref, gm_ref, qk_ref, qn_ref, kn_ref, v_ref, u_ref, xn_ref):
        xv = x_ref[...]
        xnb = ((xv * _rms(xv)) * g1_ref[...]).astype(BF16)
        xn_ref[...] = xnb
        proj = _dot(xnb, w_ref[...])
        q = proj[:, :ATTN_W]
        k = proj[:, ATTN_W:2 * ATTN_W]
        qk_ref[...] = proj[:, :2 * ATTN_W]
        v_ref[...] = proj[:, 2 * ATTN_W:3 * ATTN_W]
        u_ref[...] = proj[:, 3 * ATTN_W:]
        gm = gm_ref[...]
        qn_ref[...] = (q * lax.rsqrt(_group_mean(q * q, gm) + EPS)) * gq_ref[...]
        kn_ref[...] = (k * lax.rsqrt(_group_mean(k * k, gm) + EPS)) * gk_ref[...]

    row = lambda w: pl.BlockSpec((tm, w), lambda i: (i, 0))
    return pl.pallas_call(
        body, name="fwd_proj", grid=(s // tm,),
        in_specs=[row(D_MODEL), _full((1, D_MODEL)), _full((D_MODEL, 4 * ATTN_W)), _full((1, ATTN_W)),
                  _full((1, ATTN_W)), _full((ATTN_W, ATTN_W))],
        out_specs=[row(2 * ATTN_W), row(ATTN_W), row(ATTN_W), row(ATTN_W), row(ATTN_W), row(D_MODEL)],
        out_shape=[_sds((s, 2 * ATTN_W)), _sds((s, ATTN_W)), _sds((s, ATTN_W)), _sds((s, ATTN_W)),
                   _sds((s, ATTN_W)), _sds((s, D_MODEL), BF16)],
        compiler_params=_cp(("parallel",)),
    )(x, g1, w_in, gq, gk, gmat)


def _attn_rows(t, d, nb):
    if d == 1:
        q0 = pl.multiple_of(t * ATTN_BLOCK, ATTN_BLOCK)
        return (t, pl.ds(q0, ATTN_BLOCK), pl.ds(pl.multiple_of(ATTN_CHUNK + q0, ATTN_BLOCK), ATTN_BLOCK),
                pl.ds(pl.multiple_of(ATTN_CHUNK - ATTN_BLOCK + q0, ATTN_BLOCK), ATTN_BLOCK))
    r = t // nb
    b = t % nb
    return (b, pl.ds(ATTN_BLOCK * b * d + r, ATTN_BLOCK, stride=d),
            pl.ds(ATTN_CHUNK + ATTN_BLOCK * b * d + r, ATTN_BLOCK, stride=d),
            pl.ds(ATTN_CHUNK + ATTN_BLOCK * (b - 1) * d + r, ATTN_BLOCK, stride=d))


def _attn_masks():
    row = lax.broadcasted_iota(jnp.int32, (ATTN_BLOCK, LANES), 0)
    col = lax.broadcasted_iota(jnp.int32, (ATTN_BLOCK, LANES), 1)
    return row, col


NBLK = ATTN_CHUNK // ATTN_BLOCK


def _attn_bias(bias_s):
    row, col = _attn_masks()
    bias_s[:, pl.ds(0, LANES)] = jnp.where(col >= row, 0.0, NEG_INF)
    bias_s[:, pl.ds(LANES, LANES)] = jnp.where(col <= row, 0.0, NEG_INF)
    return col < HEAD_DIM


def _attn_fwd(qn, kn, v, ex=None, unroll=8):
    s = qn.shape[0]
    nch = s // ATTN_CHUNK
    scale = HEAD_DIM ** -0.5
    npat = len(DILATIONS)
    n_hp = ATTN_W // LANES

    def body(*refs):
        ((q_ref, kp_ref, kc_ref, vp_ref, vc_ref), (o_ref, lse_ref),
         (kk, vv, kt_s, vb_s, bias_s, m_s, a_s), hx) = _carry(ex, refs, 5, 2)
        i = pl.program_id(1)
        if ex is not None:
            @pl.when(jnp.logical_and(pl.program_id(0) == 0, i == 0))
            def _():
                ex.start(*hx)

        kk[pl.ds(0, ATTN_CHUNK), :] = kp_ref[...]
        kk[pl.ds(ATTN_CHUNK, ATTN_CHUNK), :] = kc_ref[...]
        vv[pl.ds(0, ATTN_CHUNK), :] = vp_ref[...]
        vv[pl.ds(ATTN_CHUNK, ATTN_CHUNK), :] = vc_ref[...]
        head0 = _attn_bias(bias_s)

        for p, d in enumerate(DILATIONS):
            nb = ATTN_CHUNK // (ATTN_BLOCK * d)

            def prep(t, carry, d=d, nb=nb):
                _, _, crows, prows = _attn_rows(t, d, nb)
                kt_s[t, :, pl.ds(0, LANES)] = kk[prows, :].T.astype(BF16)
                kt_s[t, :, pl.ds(LANES, LANES)] = kk[crows, :].T.astype(BF16)
                vp = vv[prows, :]
                vc = vv[crows, :]
                vb_s[2 * t, pl.ds(0, ATTN_BLOCK), :] = jnp.where(head0, vp, 1.0).astype(BF16)
                vb_s[2 * t, pl.ds(ATTN_BLOCK, ATTN_BLOCK), :] = jnp.where(head0, vc, 1.0).astype(BF16)
                vb_s[2 * t + 1, pl.ds(0, ATTN_BLOCK), :] = jnp.where(head0, 1.0, vp).astype(BF16)
                vb_s[2 * t + 1, pl.ds(ATTN_BLOCK, ATTN_BLOCK), :] = jnp.where(head0, 1.0, vc).astype(BF16)
                return carry

            lax.fori_loop(0, NBLK, prep, 0, unroll=4)

            def blk(t, carry, p=p, d=d, nb=nb):
                b, qrows, _, _ = _attn_rows(t, d, nb)
                q = q_ref[qrows, :]
                pen = jnp.where(jnp.logical_or(i > 0, b > 0), 0.0, NEG_INF)
                kt = kt_s[t]
                outs, ms = [], []
                for h in range(2):
                    hm = head0 if h == 0 else jnp.logical_not(head0)
                    qh = jnp.where(hm, q, 0.0).astype(BF16)
                    sc = _dot(qh, kt) * scale + bias_s[...]
                    s_p = sc[:, :LANES] + pen
                    s_c = sc[:, LANES:]
                    m = jnp.max(jnp.maximum(s_p, s_c), axis=-1, keepdims=True)
                    e_b = jnp.concatenate([jnp.exp(s_p - m), jnp.exp(s_c - m)], axis=1).astype(BF16)
                    outs.append(_dot(e_b, vb_s[2 * t + h]))
                    ms.append(jnp.broadcast_to(m, (ATTN_BLOCK, LANES)))
                m_s[p, qrows, :] = jnp.where(head0, ms[0], ms[1])
                a_s[2 * p, qrows, :] = outs[0]
                a_s[2 * p + 1, qrows, :] = outs[1]
                return carry

            lax.fori_loop(0, NBLK, blk, 0, unroll=unroll)

        def merge(t, carry):
            rows = pl.ds(pl.multiple_of(t * ATTN_BLOCK, ATTN_BLOCK), ATTN_BLOCK)
            m_all = m_s[0, rows, :]
            for p in range(1, npat):
                m_all = jnp.maximum(m_all, m_s[p, rows, :])
            num = jnp.zeros((ATTN_BLOCK, LANES), F32)
            den = jnp.zeros((ATTN_BLOCK, LANES), F32)
            for p in range(npat):
                w = jnp.exp(m_s[p, rows, :] - m_all)
                a0, a1 = a_s[2 * p, rows, :], a_s[2 * p + 1, rows, :]
                num = num + jnp.where(head0, a0, a1) * w
                den = den + pltpu.roll(jnp.where(head0, a1, a0), HEAD_DIM, 1) * w
            o_ref[rows, :] = num / den
            lse_ref[rows, :] = m_all + jnp.log(den)
            return carry

        lax.fori_loop(0, NBLK, merge, 0, unroll=2)
        if ex is not None:
            @pl.when(jnp.logical_and(pl.program_id(0) == n_hp - 1, i == nch - 1))
            def _():
                ex.wait(*hx)

    cur = pl.BlockSpec((ATTN_CHUNK, LANES), lambda h, i: (i, h))
    prev = pl.BlockSpec((ATTN_CHUNK, LANES), lambda h, i: (jnp.maximum(i - 1, 0), h))
    hosted = ex is not None
    return pl.pallas_call(
        body, name="attn_fwd", grid=(n_hp, nch),
        in_specs=[cur, prev, cur, prev, cur] + (ex.specs() if hosted else []),
        out_specs=[cur, cur] + (ex.specs() if hosted else []),
        out_shape=[_sds((s, ATTN_W)), _sds((s, ATTN_W))] + (ex.out_shape() if hosted else []),
        scratch_shapes=[pltpu.VMEM((2 * ATTN_CHUNK, LANES), F32), pltpu.VMEM((2 * ATTN_CHUNK, LANES), F32),
                        pltpu.VMEM((NBLK, LANES, 2 * LANES), BF16), pltpu.VMEM((2 * NBLK, 2 * ATTN_BLOCK, LANES), BF16),
                        pltpu.VMEM((ATTN_BLOCK, 2 * LANES), F32),
                        pltpu.VMEM((npat, ATTN_CHUNK, LANES), F32), pltpu.VMEM((2 * npat, ATTN_CHUNK, LANES), F32)]
        + (ex.scratch() if hosted else []),
        compiler_params=_cp(("arbitrary", "arbitrary")),
    )(qn, kn, kn, v, v, *(ex.srcs if hosted else []))


def _attn_bwd(qn, kn, v, o, lse, do, ex=None, group=4):
    s = qn.shape[0]
    nch = s // ATTN_CHUNK
    scale = HEAD_DIM ** -0.5
    npat = len(DILATIONS)
    n_hp = ATTN_W // LANES

    def body(*refs):
        ((q_ref, kp_ref, kc_ref, vp_ref, vc_ref, o_ref, lse_ref, do_ref), (dq_ref, dk_ref, dv_ref),
         (kk, vv, dkk, dvv, kt_s, vt_s, kn_s, bias_s, dq_s, dl_s, dkb, dvb), hx) = _carry(ex, refs, 8, 3)
        step = pl.program_id(1)
        i = nch - 1 - step
        if ex is not None:
            @pl.when(jnp.logical_and(pl.program_id(0) == 0, step == 0))
            def _():
                ex.start(*hx)

        kk[pl.ds(0, ATTN_CHUNK), :] = kp_ref[...]
        kk[pl.ds(ATTN_CHUNK, ATTN_CHUNK), :] = kc_ref[...]
        vv[pl.ds(0, ATTN_CHUNK), :] = vp_ref[...]
        vv[pl.ds(ATTN_CHUNK, ATTN_CHUNK), :] = vc_ref[...]

        @pl.when(step == 0)
        def _():
            dkk[pl.ds(ATTN_CHUNK, ATTN_CHUNK), :] = jnp.zeros((ATTN_CHUNK, LANES), F32)
            dvv[pl.ds(ATTN_CHUNK, ATTN_CHUNK), :] = jnp.zeros((ATTN_CHUNK, LANES), F32)

        @pl.when(step > 0)
        def _():
            dkk[pl.ds(ATTN_CHUNK, ATTN_CHUNK), :] = dkk[pl.ds(0, ATTN_CHUNK), :]
            dvv[pl.ds(ATTN_CHUNK, ATTN_CHUNK), :] = dvv[pl.ds(0, ATTN_CHUNK), :]

        dkk[pl.ds(0, ATTN_CHUNK), :] = jnp.zeros((ATTN_CHUNK, LANES), F32)
        dvv[pl.ds(0, ATTN_CHUNK), :] = jnp.zeros((ATTN_CHUNK, LANES), F32)
        head0 = _attn_bias(bias_s)

        def delta(t, carry):
            rows = pl.ds(pl.multiple_of(t * ATTN_BLOCK, ATTN_BLOCK), ATTN_BLOCK)
            prod = do_ref[rows, :] * o_ref[rows, :]
            d0 = jnp.sum(jnp.where(head0, prod, 0.0), axis=-1, keepdims=True)
            d1 = jnp.sum(jnp.where(head0, 0.0, prod), axis=-1, keepdims=True)
            dl_s[rows, :] = jnp.where(head0, d0, d1)
            return carry

        lax.fori_loop(0, NBLK, delta, 0, unroll=2)

        for p, d in enumerate(DILATIONS):
            nb = ATTN_CHUNK // (ATTN_BLOCK * d)

            def prep(t, carry, d=d, nb=nb):
                _, _, crows, prows = _attn_rows(t, d, nb)
                kp, kc = kk[prows, :], kk[crows, :]
                kt_s[t, :, pl.ds(0, LANES)] = kp.T.astype(BF16)
                kt_s[t, :, pl.ds(LANES, LANES)] = kc.T.astype(BF16)
                kn_s[t, pl.ds(0, ATTN_BLOCK), :] = kp.astype(BF16)
                kn_s[t, pl.ds(ATTN_BLOCK, ATTN_BLOCK), :] = kc.astype(BF16)
                vt_s[t, :, pl.ds(0, LANES)] = vv[prows, :].T.astype(BF16)
                vt_s[t, :, pl.ds(LANES, LANES)] = vv[crows, :].T.astype(BF16)
                return carry

            lax.fori_loop(0, NBLK, prep, 0, unroll=4)

            def blk(tg, carry, p=p, d=d, nb=nb):
                st = []
                for g in range(group):
                    t = tg * group + g
                    b, qrows, _, _ = _attn_rows(t, d, nb)
                    q = q_ref[qrows, :]
                    dout = do_ref[qrows, :]
                    lse_b = lse_ref[qrows, :]
                    dl_b = dl_s[qrows, :]
                    pen = jnp.where(jnp.logical_or(i > 0, b > 0), 0.0, NEG_INF)
                    for h in range(2):
                        hm = head0 if h == 0 else jnp.logical_not(head0)
                        c0 = h * HEAD_DIM
                        qh = jnp.where(hm, q, 0.0).astype(BF16)
                        doh = jnp.where(hm, dout, 0.0).astype(BF16)
                        st.append(dict(t=t, qrows=qrows, qh=qh, doh=doh, pen=pen, lse=lse_b[:, c0:c0 + 1],
                                       dl=dl_b[:, c0:c0 + 1], sc=_dot(qh, kt_s[t]), dp=_dot(doh, vt_s[t])))
                for e in st:
                    sc = e["sc"] * scale + bias_s[...]
                    sc = jnp.concatenate([sc[:, :LANES] + e["pen"], sc[:, LANES:]], axis=1)
                    pr = jnp.exp(sc - e["lse"])
                    e["ds"] = (pr * (e["dp"] - e["dl"]) * scale).astype(BF16)
                    e["pr"] = pr.astype(BF16)
                for g in range(group):
                    e0, e1 = st[2 * g], st[2 * g + 1]
                    t = e0["t"]
                    dq_s[p, e0["qrows"], :] = jnp.where(head0, _dot(e0["ds"], kn_s[t]), _dot(e1["ds"], kn_s[t]))
                    dkb[t] = _dot_tn(e0["ds"], e0["qh"]) + _dot_tn(e1["ds"], e1["qh"])
                    dvb[t] = _dot_tn(e0["pr"], e0["doh"]) + _dot_tn(e1["pr"], e1["doh"])
                return carry

            lax.fori_loop(0, NBLK // group, blk, 0)

            def scatter(t, carry, d=d, nb=nb):
                _, _, crows, prows = _attn_rows(t, d, nb)
                dkk[prows, :] = dkk[prows, :] + dkb[t, pl.ds(0, ATTN_BLOCK), :]
                dkk[crows, :] = dkk[crows, :] + dkb[t, pl.ds(ATTN_BLOCK, ATTN_BLOCK), :]
                dvv[prows, :] = dvv[prows, :] + dvb[t, pl.ds(0, ATTN_BLOCK), :]
                dvv[crows, :] = dvv[crows, :] + dvb[t, pl.ds(ATTN_BLOCK, ATTN_BLOCK), :]
                return carry

            lax.fori_loop(0, NBLK, scatter, 0)

        def finish(t, carry):
            rows = pl.ds(pl.multiple_of(t * ATTN_BLOCK, ATTN_BLOCK), ATTN_BLOCK)
            acc = dq_s[0, rows, :]
            for p in range(1, npat):
                acc = acc + dq_s[p, rows, :]
            dq_ref[rows, :] = acc
            return carry

        lax.fori_loop(0, NBLK, finish, 0, unroll=2)
        dk_ref[...] = dkk[pl.ds(ATTN_CHUNK, ATTN_CHUNK), :]
        dv_ref[...] = dvv[pl.ds(ATTN_CHUNK, ATTN_CHUNK), :]
        if ex is not None:
            @pl.when(jnp.logical_and(pl.program_id(0) == n_hp - 1, step == nch - 1))
            def _():
                ex.wait(*hx)

    cur = pl.BlockSpec((ATTN_CHUNK, LANES), lambda h, t: (nch - 1 - t, h))
    prev = pl.BlockSpec((ATTN_CHUNK, LANES), lambda h, t: (jnp.maximum(nch - 2 - t, 0), h))
    big = pltpu.VMEM((2 * ATTN_CHUNK, LANES), F32)
    pair_t = pltpu.VMEM((NBLK, LANES, 2 * LANES), BF16)
    hosted = ex is not None
    return pl.pallas_call(
        body, name="attn_bwd", grid=(n_hp, nch),
        in_specs=[cur, prev, cur, prev, cur, cur, cur, cur] + (ex.specs() if hosted else []),
        out_specs=[cur, cur, cur] + (ex.specs() if hosted else []),
        out_shape=[_sds((s, ATTN_W))] * 3 + (ex.out_shape() if hosted else []),
        scratch_shapes=[big, big, big, big, pair_t, pair_t, pltpu.VMEM((NBLK, 2 * ATTN_BLOCK, LANES), BF16),
                        pltpu.VMEM((ATTN_BLOCK, 2 * LANES), F32),
                        pltpu.VMEM((npat, ATTN_CHUNK, LANES), F32), pltpu.VMEM((ATTN_CHUNK, LANES), F32),
                        pltpu.VMEM((NBLK, 2 * ATTN_BLOCK, LANES), F32), pltpu.VMEM((NBLK, 2 * ATTN_BLOCK, LANES), F32)]
        + (ex.scratch() if hosted else []),
        compiler_params=_cp(("arbitrary", "arbitrary")),
    )(qn, kn, kn, v, v, o, lse, do, *(ex.srcs if hosted else []))


def _attn_fwd_v1(qn, kn, v):
    s = qn.shape[0]
    nch = s // ATTN_CHUNK
    scale = HEAD_DIM ** -0.5
    npat = len(DILATIONS)

    def body(q_ref, kp_ref, kc_ref, vp_ref, vc_ref, o_ref, lse_ref, kk, vv, m_s, l_s, acc_s):
        i = pl.program_id(1)
        kk[pl.ds(0, ATTN_CHUNK), :] = kp_ref[...]
        kk[pl.ds(ATTN_CHUNK, ATTN_CHUNK), :] = kc_ref[...]
        vv[pl.ds(0, ATTN_CHUNK), :] = vp_ref[...]
        vv[pl.ds(ATTN_CHUNK, ATTN_CHUNK), :] = vc_ref[...]
        row, col = _attn_masks()
        head0 = col < HEAD_DIM
        mask_cur = col <= row
        diff = col - row

        for p, d in enumerate(DILATIONS):
            nb = ATTN_CHUNK // (ATTN_BLOCK * d)

            def blk(t, carry, p=p, d=d, nb=nb):
                b, qrows, crows, prows = _attn_rows(t, d, nb)
                q = q_ref[qrows, :]
                kc = kk[crows, :].astype(BF16)
                kp = kk[prows, :].astype(BF16)
                vc = vv[crows, :].astype(BF16)
                vp = vv[prows, :].astype(BF16)
                thr = jnp.where(jnp.logical_or(i > 0, b > 0), 0, 4 * ATTN_BLOCK)
                mask_prev = diff >= thr
                accs, ms, ls = [], [], []
                for h in range(2):
                    hm = head0 if h == 0 else jnp.logical_not(head0)
                    qh = jnp.where(hm, q, 0.0).astype(BF16)
                    s_p = jnp.where(mask_prev, _dot_nt(qh, kp) * scale, NEG_INF)
                    s_c = jnp.where(mask_cur, _dot_nt(qh, kc) * scale, NEG_INF)
                    m = jnp.maximum(jnp.max(s_p, axis=-1, keepdims=True), jnp.max(s_c, axis=-1, keepdims=True))
                    e_p = jnp.exp(s_p - m)
                    e_c = jnp.exp(s_c - m)
                    l = jnp.sum(e_p, axis=-1, keepdims=True) + jnp.sum(e_c, axis=-1, keepdims=True)
                    accs.append(_dot(e_p.astype(BF16), vp) + _dot(e_c.astype(BF16), vc))
                    ms.append(jnp.broadcast_to(m, (ATTN_BLOCK, LANES)))
                    ls.append(jnp.broadcast_to(l, (ATTN_BLOCK, LANES)))
                m_s[p, qrows, :] = jnp.where(head0, ms[0], ms[1])
                l_s[p, qrows, :] = jnp.where(head0, ls[0], ls[1])
                acc_s[p, qrows, :] = jnp.where(head0, accs[0], accs[1])
                return carry

            lax.fori_loop(0, ATTN_CHUNK // ATTN_BLOCK, blk, 0)

        def merge(t, carry):
            rows = pl.ds(pl.multiple_of(t * ATTN_BLOCK, ATTN_BLOCK), ATTN_BLOCK)
            m_all = m_s[0, rows, :]
            for p in range(1, npat):
                m_all = jnp.maximum(m_all, m_s[p, rows, :])
            num = jnp.zeros((ATTN_BLOCK, LANES), F32)
            den = jnp.zeros((ATTN_BLOCK, LANES), F32)
            for p in range(npat):
                w = jnp.exp(m_s[p, rows, :] - m_all)
                num = num + acc_s[p, rows, :] * w
                den = den + l_s[p, rows, :] * w
            o_ref[rows, :] = num / den
            lse_ref[rows, :] = m_all + jnp.log(den)
            return carry

        lax.fori_loop(0, ATTN_CHUNK // ATTN_BLOCK, merge, 0)

    cur = pl.BlockSpec((ATTN_CHUNK, LANES), lambda h, i: (i, h))
    prev = pl.BlockSpec((ATTN_CHUNK, LANES), lambda h, i: (jnp.maximum(i - 1, 0), h))
    return pl.pallas_call(
        body, name="attn_fwd", grid=(ATTN_W // LANES, nch),
        in_specs=[cur, prev, cur, prev, cur],
        out_specs=[cur, cur],
        out_shape=[_sds((s, ATTN_W)), _sds((s, ATTN_W))],
        scratch_shapes=[pltpu.VMEM((2 * ATTN_CHUNK, LANES), F32), pltpu.VMEM((2 * ATTN_CHUNK, LANES), F32),
                        pltpu.VMEM((npat, ATTN_CHUNK, LANES), F32), pltpu.VMEM((npat, ATTN_CHUNK, LANES), F32),
                        pltpu.VMEM((npat, ATTN_CHUNK, LANES), F32)],
        compiler_params=_cp(("parallel", "parallel")),
    )(qn, kn, kn, v, v)


def _attn_bwd_v1(qn, kn, v, o, lse, do):
    s = qn.shape[0]
    nch = s // ATTN_CHUNK
    scale = HEAD_DIM ** -0.5
    npat = len(DILATIONS)

    def body(q_ref, kp_ref, kc_ref, vp_ref, vc_ref, o_ref, lse_ref, do_ref, dq_ref, dk_ref, dv_ref,
             kk, vv, dkk, dvv, dq_s, dl_s):
        step = pl.program_id(1)
        i = nch - 1 - step
        kk[pl.ds(0, ATTN_CHUNK), :] = kp_ref[...]
        kk[pl.ds(ATTN_CHUNK, ATTN_CHUNK), :] = kc_ref[...]
        vv[pl.ds(0, ATTN_CHUNK), :] = vp_ref[...]
        vv[pl.ds(ATTN_CHUNK, ATTN_CHUNK), :] = vc_ref[...]

        @pl.when(step == 0)
        def _():
            dkk[pl.ds(ATTN_CHUNK, ATTN_CHUNK), :] = jnp.zeros((ATTN_CHUNK, LANES), F32)
            dvv[pl.ds(ATTN_CHUNK, ATTN_CHUNK), :] = jnp.zeros((ATTN_CHUNK, LANES), F32)

        @pl.when(step > 0)
        def _():
            dkk[pl.ds(ATTN_CHUNK, ATTN_CHUNK), :] = dkk[pl.ds(0, ATTN_CHUNK), :]
            dvv[pl.ds(ATTN_CHUNK, ATTN_CHUNK), :] = dvv[pl.ds(0, ATTN_CHUNK), :]

        dkk[pl.ds(0, ATTN_CHUNK), :] = jnp.zeros((ATTN_CHUNK, LANES), F32)
        dvv[pl.ds(0, ATTN_CHUNK), :] = jnp.zeros((ATTN_CHUNK, LANES), F32)

        row, col = _attn_masks()
        head0 = col < HEAD_DIM
        mask_cur = col <= row
        diff = col - row

        def delta(t, carry):
            rows = pl.ds(pl.multiple_of(t * ATTN_BLOCK, ATTN_BLOCK), ATTN_BLOCK)
            prod = do_ref[rows, :] * o_ref[rows, :]
            d0 = jnp.sum(jnp.where(head0, prod, 0.0), axis=-1, keepdims=True)
            d1 = jnp.sum(jnp.where(head0, 0.0, prod), axis=-1, keepdims=True)
            dl_s[rows, :] = jnp.where(head0, d0, d1)
            return carry

        lax.fori_loop(0, ATTN_CHUNK // ATTN_BLOCK, delta, 0)

        for p, d in enumerate(DILATIONS):
            nb = ATTN_CHUNK // (ATTN_BLOCK * d)

            def blk(t, carry, p=p, d=d, nb=nb):
                b, qrows, crows, prows = _attn_rows(t, d, nb)
                q = q_ref[qrows, :]
                dout = do_ref[qrows, :]
                lse_b = lse_ref[qrows, :]
                dl_b = dl_s[qrows, :]
                kc = kk[crows, :].astype(BF16)
                kp = kk[prows, :].astype(BF16)
                vc = vv[crows, :].astype(BF16)
                vp = vv[prows, :].astype(BF16)
                thr = jnp.where(jnp.logical_or(i > 0, b > 0), 0, 4 * ATTN_BLOCK)
                mask_prev = diff >= thr
                dqs = []
                dk_p = jnp.zeros((ATTN_BLOCK, LANES), F32)
                dk_c = jnp.zeros((ATTN_BLOCK, LANES), F32)
                dv_p = jnp.zeros((ATTN_BLOCK, LANES), F32)
                dv_c = jnp.zeros((ATTN_BLOCK, LANES), F32)
                for h in range(2):
                    hm = head0 if h == 0 else jnp.logical_not(head0)
                    c0 = h * HEAD_DIM
                    qh = jnp.where(hm, q, 0.0).astype(BF16)
                    doh = jnp.where(hm, dout, 0.0).astype(BF16)
                    lse_h = lse_b[:, c0:c0 + 1]
                    dl_h = dl_b[:, c0:c0 + 1]
                    s_p = jnp.where(mask_prev, _dot_nt(qh, kp) * scale, NEG_INF)
                    s_c = jnp.where(mask_cur, _dot_nt(qh, kc) * scale, NEG_INF)
                    p_p = jnp.exp(s_p - lse_h)
                    p_c = jnp.exp(s_c - lse_h)
                    ds_p = (p_p * (_dot_nt(doh, vp) - dl_h) * scale).astype(BF16)
                    ds_c = (p_c * (_dot_nt(doh, vc) - dl_h) * scale).astype(BF16)
                    dqs.append(_dot(ds_p, kp) + _dot(ds_c, kc))
                    dk_p = dk_p + _dot_tn(ds_p, qh)
                    dk_c = dk_c + _dot_tn(ds_c, qh)
                    dv_p = dv_p + _dot_tn(p_p.astype(BF16), doh)
                    dv_c = dv_c + _dot_tn(p_c.astype(BF16), doh)
                dq_s[p, qrows, :] = jnp.where(head0, dqs[0], dqs[1])
                dkk[prows, :] = dkk[prows, :] + dk_p
                dkk[crows, :] = dkk[crows, :] + dk_c
                dvv[prows, :] = dvv[prows, :] + dv_p
                dvv[crows, :] = dvv[crows, :] + dv_c
                return carry

            lax.fori_loop(0, ATTN_CHUNK // ATTN_BLOCK, blk, 0)

        def finish(t, carry):
            rows = pl.ds(pl.multiple_of(t * ATTN_BLOCK, ATTN_BLOCK), ATTN_BLOCK)
            acc = dq_s[0, rows, :]
            for p in range(1, npat):
                acc = acc + dq_s[p, rows, :]
            dq_ref[rows, :] = acc
            return carry

        lax.fori_loop(0, ATTN_CHUNK // ATTN_BLOCK, finish, 0)
        dk_ref[...] = dkk[pl.ds(ATTN_CHUNK, ATTN_CHUNK), :]
        dv_ref[...] = dvv[pl.ds(ATTN_CHUNK, ATTN_CHUNK), :]

    cur = pl.BlockSpec((ATTN_CHUNK, LANES), lambda h, t: (nch - 1 - t, h))
    prev = pl.BlockSpec((ATTN_CHUNK, LANES), lambda h, t: (jnp.maximum(nch - 2 - t, 0), h))
    big = pltpu.VMEM((2 * ATTN_CHUNK, LANES), F32)
    return pl.pallas_call(
        body, name="attn_bwd", grid=(ATTN_W // LANES, nch),
        in_specs=[cur, prev, cur, prev, cur, cur, cur, cur],
        out_specs=[cur, cur, cur],
        out_shape=[_sds((s, ATTN_W))] * 3,
        scratch_shapes=[big, big, big, big, pltpu.VMEM((npat, ATTN_CHUNK, LANES), F32),
                        pltpu.VMEM((ATTN_CHUNK, LANES), F32)],
        compiler_params=_cp(("parallel", "arbitrary")),
    )(qn, kn, kn, v, v, o, lse, do)


def _discretize(lr, li, dt):
    mag = jnp.exp(lr * dt)
    abr = mag * jnp.cos(li * dt)
    abi = mag * jnp.sin(li * dt)
    den = lr * lr + li * li
    nr, ni = abr - 1.0, abi
    cr = (nr * lr + ni * li) / den
    ci = (ni * lr - nr * li) / den
    return abr, abi, den, nr, ni, cr, ci


def _ssm_discretize(a_re, a_im, log_dt, b_re_t, b_im_t):
    def body(ar_ref, ai_ref, ldt_ref, br_ref, bi_ref, abr_ref, abi_ref, bbr_ref, bbi_ref):
        abr, abi, _, _, _, cr, ci = _discretize(ar_ref[...], ai_ref[...], jnp.exp(ldt_ref[...]))
        br, bi = br_ref[...], bi_ref[...]
        abr_ref[...] = abr
        abi_ref[...] = abi
        bbr_ref[...] = cr * br - ci * bi
        bbi_ref[...] = cr * bi + ci * br

    return pl.pallas_call(
        body, name="ssm_discretize",
        out_shape=[_sds(a_re.shape)] * 2 + [_sds(b_re_t.shape)] * 2,
    )(a_re, a_im, log_dt, b_re_t, b_im_t)


def _ssm_discretize_bwd(a_re, a_im, log_dt, b_re_t, b_im_t, dabr, dabi, dbbr, dbbi):
    def body(ar_ref, ai_ref, ldt_ref, br_ref, bi_ref, dabr_ref, dabi_ref, dbbr_ref, dbbi_ref,
             dar_ref, dai_ref, dldt_ref, dbr_ref, dbi_ref):
        lr, li = ar_ref[...], ai_ref[...]
        dt = jnp.exp(ldt_ref[...])
        abr, abi, den, nr, ni, cr, ci = _discretize(lr, li, dt)
        br, bi = br_ref[...], bi_ref[...]
        gbr, gbi = dbbr_ref[...], dbbi_ref[...]
        dcr = jnp.sum(gbr * br + gbi * bi, axis=1, keepdims=True)
        dci = jnp.sum(gbi * br - gbr * bi, axis=1, keepdims=True)
        dbr_ref[...] = cr * gbr + ci * gbi
        dbi_ref[...] = cr * gbi - ci * gbr
        dnr = (dcr * lr - dci * li) / den
        dni = (dcr * li + dci * lr) / den
        dden = -(dcr * cr + dci * ci) / den
        dlr = (dcr * nr + dci * ni) / den + dden * 2.0 * lr
        dli = (dcr * ni - dci * nr) / den + dden * 2.0 * li
        gabr = dabr_ref[...] + dnr
        gabi = dabi_ref[...] + dni
        dphi = gabr * abr + gabi * abi
        dth = gabi * abr - gabr * abi
        dar_ref[...] = dlr + dphi * dt
        dai_ref[...] = dli + dth * dt
        dldt_ref[...] = jnp.sum(dphi * lr + dth * li, axis=2, keepdims=True) * dt

    return pl.pallas_call(
        body, name="ssm_discretize_bwd",
        out_shape=[_sds(a_re.shape)] * 2 + [_sds(log_dt.shape)] + [_sds(b_re_t.shape)] * 2,
    )(a_re, a_im, log_dt, b_re_t, b_im_t, dabr, dabi, dbbr, dbbi)


def _ssm_power(abr, abi, n_sq):
    def body(r_ref, i_ref, or_ref, oi_ref):
        r, i = r_ref[...], i_ref[...]
        for _ in range(n_sq):
            r, i = r * r - i * i, 2.0 * r * i
        or_ref[...] = r
        oi_ref[...] = i

    return pl.pallas_call(body, name="ssm_power", out_shape=[_sds(abr.shape)] * 2)(abr, abi)


N_CB = SSM_W // LANES
CB_STATES = N_STATE // N_CB
ROWS = N_SEG * SSM_LK


class _Neg:
    def __init__(self, ref):
        self.ref = ref

    def __getitem__(self, idx):
        return -self.ref[idx]


def _seg_init(fin_r, fin_i, pw_r, pw_i, x_r, x_i, reverse):
    zero = jnp.zeros((1, N_STATE), F32)
    cr, ci = zero, zero
    order = range(N_SEG - 1, -1, -1) if reverse else range(N_SEG)
    pr = pw_r[...]
    pi = -pw_i[...] if reverse else pw_i[...]
    for j in order:
        x_r[pl.ds(j, 1), :] = cr
        x_i[pl.ds(j, 1), :] = ci
        fr, fi = fin_r[pl.ds(j, 1), :], fin_i[pl.ds(j, 1), :]
        cr, ci = fr + pr * cr - pi * ci, fi + pr * ci + pi * cr


def _scan_rows(a_r, a_i, b_r, b_i, x_r, x_i, o_r, o_i, b_off, n_steps, reverse):
    w = 512
    for c in range(N_STATE // w):
        cols = pl.ds(c * w, w)
        ar = jnp.broadcast_to(a_r[:, cols], (N_SEG, w))
        ai = jnp.broadcast_to(a_i[:, cols], (N_SEG, w))

        def step(t, carry, cols=cols, ar=ar, ai=ai):
            xr, xi = carry
            k = (n_steps - 1 - t) if reverse else t
            rows = pl.ds(pl.multiple_of(k * N_SEG + b_off, N_SEG), N_SEG)
            nr = ar * xr - ai * xi + b_r[rows, cols]
            ni = ar * xi + ai * xr + b_i[rows, cols]
            o_r[rows, cols] = nr
            o_i[rows, cols] = ni
            return nr, ni

        xr, xi = lax.fori_loop(0, n_steps, step, (x_r[:, cols], x_i[:, cols]), unroll=4)
        x_r[:, cols] = xr
        x_i[:, cols] = xi


def _permute_in(src_ref, dst):
    for c in range(N_CB):
        dst[c] = src_ref[:, :, pl.ds(c * LANES, LANES)].reshape(ROWS, LANES)


def _permute_out(src, dst_ref):
    for c in range(N_CB):
        dst_ref[:, :, pl.ds(c * LANES, LANES)] = src[c].reshape(SSM_LK, N_SEG, LANES)


def _ssm_fwd(u3, abr, abi, pw_r, pw_i, fin_r, fin_i, bb_r, bb_i, cc_r, cc_i, dskip, finals_only):
    sl = u3.shape[0]
    nch = sl // SSM_LK

    def body(u_ref, abr_ref, abi_ref, pwr_ref, pwi_ref, finr_ref, fini_ref, bbr_ref, bbi_ref,
             ccr_ref, cci_ref, d_ref, *rest):
        if finals_only:
            xfr_ref, xfi_ref, up, xs_r, xs_i, x_r, x_i = rest
        else:
            y_ref, xsr_ref, xsi_ref, up, yp, xs_r, xs_i, x_r, x_i = rest
        k = pl.program_id(0)

        @pl.when(k == 0)
        def _():
            _seg_init(finr_ref, fini_ref, pwr_ref, pwi_ref, x_r, x_i, False)

        if not finals_only:
            xsr_ref[0] = x_r[...]
            xsi_ref[0] = x_i[...]
        _permute_in(u_ref, up)
        for c in range(N_CB):
            lhs = up[c].astype(BF16)
            xs_r[:, pl.ds(c * CB_STATES, CB_STATES)] = _dot(lhs, bbr_ref[c])
            xs_i[:, pl.ds(c * CB_STATES, CB_STATES)] = _dot(lhs, bbi_ref[c])
        _scan_rows(abr_ref, abi_ref, xs_r, xs_i, x_r, x_i, xs_r, xs_i, 0, SSM_LK, False)
        if finals_only:
            @pl.when(k == nch - 1)
            def _():
                xfr_ref[...] = x_r[...]
                xfi_ref[...] = x_i[...]
        else:
            for c in range(N_CB):
                cols = pl.ds(c * CB_STATES, CB_STATES)
                yp[c] = (_dot(xs_r[:, cols].astype(BF16), ccr_ref[c]) - _dot(xs_i[:, cols].astype(BF16), cci_ref[c])
                         + d_ref[:, pl.ds(c * LANES, LANES)] * up[c])
            _permute_out(yp, y_ref)

    ublk = pl.BlockSpec((SSM_LK, N_SEG, SSM_W), lambda k: (k, 0, 0))
    st = pl.BlockSpec((1, N_SEG, N_STATE), lambda k: (k, 0, 0))
    vec = _full((1, N_STATE))
    mat = _full((N_SEG, N_STATE))
    chunk = pltpu.VMEM((N_CB, ROWS, LANES), F32)
    big = pltpu.VMEM((ROWS, N_STATE), F32)
    small = pltpu.VMEM((N_SEG, N_STATE), F32)
    if finals_only:
        out_specs, out_shape = [mat, mat], [_sds((N_SEG, N_STATE))] * 2
        scratch, name = [chunk, big, big, small, small], "ssm_fwd_finals"
    else:
        out_specs = [ublk, st, st]
        out_shape = [_sds(u3.shape)] + [_sds((nch, N_SEG, N_STATE))] * 2
        scratch, name = [chunk, chunk, big, big, small, small], "ssm_fwd"
    return pl.pallas_call(
        body, name=name, grid=(nch,),
        in_specs=[ublk, vec, vec, vec, vec, mat, mat,
                  _full((N_CB, LANES, CB_STATES)), _full((N_CB, LANES, CB_STATES)),
                  _full((N_CB, CB_STATES, LANES)), _full((N_CB, CB_STATES, LANES)), _full((1, SSM_W))],
        out_specs=out_specs, out_shape=out_shape, scratch_shapes=scratch,
        compiler_params=_cp(("arbitrary",)),
    )(u3, abr, abi, pw_r, pw_i, fin_r, fin_i, bb_r, bb_i, cc_r, cc_i, dskip)


def _ssm_bwd(u3, dy3, xst_r, xst_i, abr, abi, pw_r, pw_i, fin_r, fin_i, bb_r, bb_i, bbt_r, bbt_i,
             cct_r, cct_i, dskip, finals_only):
    sl = u3.shape[0]
    nch = sl // SSM_LK

    def body(u_ref, g_ref, xsr_ref, xsi_ref, abr_ref, abi_ref, pwr_ref, pwi_ref,
             finr_ref, fini_ref, bbr_ref, bbi_ref, btr_ref, bti_ref, ctr_ref, cti_ref, d_ref, *rest):
        if finals_only:
            lfr_ref, lfi_ref, gp, l_r, l_i, lam_r, lam_i = rest
        else:
            (du_ref, dar_ref, dai_ref, dbr_ref, dbi_ref, dcr_ref, dci_ref, dd_ref,
             gp, up, yp, l_r, l_i, lam_r, lam_i, x_r, x_i, xx_r, xx_i, sar, sai, sdd) = rest
        t = pl.program_id(0)

        @pl.when(t == 0)
        def _():
            _seg_init(finr_ref, fini_ref, pwr_ref, pwi_ref, lam_r, lam_i, True)
            if not finals_only:
                sar[...] = jnp.zeros_like(sar)
                sai[...] = jnp.zeros_like(sai)
                sdd[...] = jnp.zeros_like(sdd)
                dbr_ref[...] = jnp.zeros_like(dbr_ref)
                dbi_ref[...] = jnp.zeros_like(dbi_ref)
                dcr_ref[...] = jnp.zeros_like(dcr_ref)
                dci_ref[...] = jnp.zeros_like(dci_ref)

        _permute_in(g_ref, gp)
        for c in range(N_CB):
            cols = pl.ds(c * CB_STATES, CB_STATES)
            lhs = gp[c].astype(BF16)
            l_r[:, cols] = _dot(lhs, ctr_ref[c])
            l_i[:, cols] = -_dot(lhs, cti_ref[c])
        if not finals_only:
            _permute_in(u_ref, up)
            x_r[...] = xsr_ref[0]
            x_i[...] = xsi_ref[0]
            xx_r[pl.ds(0, N_SEG), :] = x_r[...]
            xx_i[pl.ds(0, N_SEG), :] = x_i[...]
            for c in range(N_CB):
                lhs = up[c].astype(BF16)
                xx_r[pl.ds(N_SEG, ROWS), pl.ds(c * CB_STATES, CB_STATES)] = _dot(lhs, bbr_ref[c])
                xx_i[pl.ds(N_SEG, ROWS), pl.ds(c * CB_STATES, CB_STATES)] = _dot(lhs, bbi_ref[c])
            _scan_rows(abr_ref, abi_ref, xx_r, xx_i, x_r, x_i, xx_r, xx_i, N_SEG, SSM_LK, False)
        _scan_rows(abr_ref, _Neg(abi_ref), l_r, l_i, lam_r, lam_i, l_r, l_i, 0, SSM_LK, True)
        if finals_only:
            @pl.when(t == nch - 1)
            def _():
                lfr_ref[...] = lam_r[...]
                lfi_ref[...] = lam_i[...]
        else:
            w = 512
            for cc in range(N_STATE // w):
                cols = pl.ds(cc * w, w)

                def acc_step(kk_, carry, cols=cols):
                    sr, si = carry
                    rows = pl.ds(pl.multiple_of(kk_ * N_SEG, N_SEG), N_SEG)
                    lr, li = l_r[rows, cols], l_i[rows, cols]
                    pr, pi = xx_r[rows, cols], xx_i[rows, cols]
                    return sr + lr * pr + li * pi, si + li * pr - lr * pi

                z = jnp.zeros((N_SEG, w), F32)
                sr, si = lax.fori_loop(0, SSM_LK, acc_step, (z, z), unroll=4)
                sar[:, cols] += sr
                sai[:, cols] += si
            for c in range(N_CB):
                cols = pl.ds(c * CB_STATES, CB_STATES)
                lrb = l_r[:, cols].astype(BF16)
                lib = l_i[:, cols].astype(BF16)
                ub = up[c].astype(BF16)
                gb = gp[c].astype(BF16)
                dbr_ref[c] += _dot_tn(lrb, ub)
                dbi_ref[c] += _dot_tn(lib, ub)
                dcr_ref[c] += _dot_tn(gb, xx_r[pl.ds(N_SEG, ROWS), cols].astype(BF16))
                dci_ref[c] += -_dot_tn(gb, xx_i[pl.ds(N_SEG, ROWS), cols].astype(BF16))
                yp[c] = _dot(lrb, btr_ref[c]) + _dot(lib, bti_ref[c]) + d_ref[:, pl.ds(c * LANES, LANES)] * gp[c]
                prod = gp[c] * up[c]
                sdd[:, pl.ds(c * LANES, LANES)] += jnp.sum(prod.reshape(SSM_LK, N_SEG, LANES), axis=0)
            _permute_out(yp, du_ref)

            @pl.when(t == nch - 1)
            def _():
                dar_ref[...] = jnp.sum(sar[...], axis=0, keepdims=True)
                dai_ref[...] = jnp.sum(sai[...], axis=0, keepdims=True)
                dd_ref[...] = jnp.sum(sdd[...], axis=0, keepdims=True)

    ublk = pl.BlockSpec((SSM_LK, N_SEG, SSM_W), lambda t: (nch - 1 - t, 0, 0))
    st = pl.BlockSpec((1, N_SEG, N_STATE), lambda t: (nch - 1 - t, 0, 0))
    vec = _full((1, N_STATE))
    mat = _full((N_SEG, N_STATE))
    cs = _full((N_CB, LANES, CB_STATES))
    sc = _full((N_CB, CB_STATES, LANES))
    in_specs = [ublk, ublk, st, st, vec, vec, vec, vec, mat, mat, cs, cs, sc, sc, cs, cs, _full((1, SSM_W))]
    chunk = pltpu.VMEM((N_CB, ROWS, LANES), F32)
    big = pltpu.VMEM((ROWS, N_STATE), F32)
    small = pltpu.VMEM((N_SEG, N_STATE), F32)
    if finals_only:
        out_specs, out_shape = [mat, mat], [_sds((N_SEG, N_STATE))] * 2
        scratch, name = [chunk, big, big, small, small], "ssm_bwd_finals"
    else:
        out_specs = [ublk, vec, vec, sc, sc, cs, cs, _full((1, SSM_W))]
        out_shape = ([_sds(u3.shape), _sds((1, N_STATE)), _sds((1, N_STATE))]
                     + [_sds((N_CB, CB_STATES, LANES))] * 2 + [_sds((N_CB, LANES, CB_STATES))] * 2
                     + [_sds((1, SSM_W))])
        xx = pltpu.VMEM((ROWS + N_SEG, N_STATE), F32)
        scratch = [chunk, chunk, chunk, big, big, small, small, small, small, xx, xx, small, small,
                   pltpu.VMEM((N_SEG, SSM_W), F32)]
        name = "ssm_bwd"
    return pl.pallas_call(
        body, name=name, grid=(nch,), in_specs=in_specs, out_specs=out_specs, out_shape=out_shape,
        scratch_shapes=scratch, compiler_params=_cp(("arbitrary",)),
    )(u3, dy3, xst_r, xst_i, abr, abi, pw_r, pw_i, fin_r, fin_i, bb_r, bb_i, bbt_r, bbt_i, cct_r, cct_i, dskip)


def _row(tm, w):
    return pl.BlockSpec((tm, w), lambda i: (i, 0))


def _acc_rows(ref, rows, first):
    @pl.when(first)
    def _():
        ref[...] = jnp.zeros_like(ref)

    ref[...] += jnp.sum(rows, axis=0, keepdims=True)


def _fwd_mix(attn, y, x, glu_w, glu_b, ga, gs, w_out, g2, tm=512):
    s = x.shape[0]

    def body(a_ref, y_ref, x_ref, gw_ref, gb_ref, ga_ref, gs_ref, wo_ref, g2_ref, mix_ref, x2_ref, h_ref):
        a = a_ref[...]
        anb = ((a * _rms(a)) * ga_ref[...]).astype(BF16)
        z, _ = _gelu(y_ref[...])
        so = z * jax.nn.sigmoid(_dot(z.astype(BF16), gw_ref[...]) + gb_ref[...])
        snb = ((so * _rms(so)) * gs_ref[...]).astype(BF16)
        mix_ref[:, pl.ds(0, ATTN_W)] = anb
        mix_ref[:, pl.ds(ATTN_W, SSM_W)] = snb
        x2 = x_ref[...] + (_dot(anb, wo_ref[pl.ds(0, ATTN_W), :]) + _dot(snb, wo_ref[pl.ds(ATTN_W, SSM_W), :]))
        x2_ref[...] = x2
        h_ref[...] = ((x2 * _rms(x2)) * g2_ref[...]).astype(BF16)

    return pl.pallas_call(
        body, name="fwd_mix", grid=(s // tm,),
        in_specs=[_row(tm, ATTN_W), _row(tm, SSM_W), _row(tm, D_MODEL), _full((SSM_W, SSM_W)), _full((1, SSM_W)),
                  _full((1, ATTN_W)), _full((1, SSM_W)), _full((D_MODEL, D_MODEL)), _full((1, D_MODEL))],
        out_specs=[_row(tm, D_MODEL), _row(tm, D_MODEL), _row(tm, D_MODEL)],
        out_shape=[_sds((s, D_MODEL), BF16), _sds((s, D_MODEL)), _sds((s, D_MODEL), BF16)],
        compiler_params=_cp(("parallel",)),
    )(attn, y, x, glu_w, glu_b, ga, gs, w_out, g2)


def _mlp_up(h, w_up, tm=512, bn=1024):
    s = h.shape[0]

    def body(h_ref, w_ref, r_ref, hdn_ref):
        hv = h_ref[...]
        for j in range(D_FF // bn):
            cols = pl.ds(j * bn, bn)
            r = jnp.maximum(_dot(hv, w_ref[:, cols]), 0.0)
            r_ref[:, cols] = r.astype(BF16)
            hdn_ref[:, cols] = (r * r).astype(BF16)

    return pl.pallas_call(
        body, name="mlp_up", grid=(s // tm,),
        in_specs=[_row(tm, D_MODEL), _full((D_MODEL, D_FF))],
        out_specs=[_row(tm, D_FF), _row(tm, D_FF)], out_shape=[_sds((s, D_FF), BF16)] * 2,
        compiler_params=_cp(("parallel",)),
    )(h, w_up)


def _mlp_down_loss(hdn, w_down, x2, tgt, tm=512):
    s = x2.shape[0]

    def body(hdn_ref, w_ref, x2_ref, t_ref, dy_ref, dyb_ref, sse_ref):
        err = (x2_ref[...] + _dot(hdn_ref[...], w_ref[...])) - t_ref[...]
        dy = err * (1.0 / D_MODEL)
        dy_ref[...] = dy
        dyb_ref[...] = dy.astype(BF16)

        @pl.when(pl.program_id(0) == 0)
        def _():
            sse_ref[...] = jnp.zeros_like(sse_ref)

        sse_ref[...] += jnp.sum(jnp.sum(err * err, axis=0, keepdims=True), axis=1, keepdims=True)

    return pl.pallas_call(
        body, name="mlp_down_loss", grid=(s // tm,),
        in_specs=[_row(tm, D_FF), _full((D_FF, D_MODEL)), _row(tm, D_MODEL), _row(tm, D_MODEL)],
        out_specs=[_row(tm, D_MODEL), _row(tm, D_MODEL), _full((1, 1))],
        out_shape=[_sds((s, D_MODEL)), _sds((s, D_MODEL), BF16), _sds((1, 1))],
        compiler_params=_cp(("arbitrary",)),
    )(hdn, w_down, x2, tgt)


def _mlp_down_bwd(dyb, w_down_t, r, tm=512, bn=1024):
    s = dyb.shape[0]

    def body(dy_ref, w_ref, r_ref, dup_ref):
        dyv = dy_ref[...]
        for j in range(D_FF // bn):
            cols = pl.ds(j * bn, bn)
            dup_ref[:, cols] = (_dot(dyv, w_ref[:, cols]) * (2.0 * r_ref[:, cols].astype(F32))).astype(BF16)

    return pl.pallas_call(
        body, name="mlp_down_bwd", grid=(s // tm,),
        in_specs=[_row(tm, D_MODEL), _full((D_MODEL, D_FF)), _row(tm, D_FF)],
        out_specs=_row(tm, D_FF), out_shape=_sds((s, D_FF), BF16),
        compiler_params=_cp(("parallel",)),
    )(dyb, w_down_t, r)


def _mlp_up_bwd(dup, w_up_t, x2, g2, dy, tm=512):
    s = x2.shape[0]

    def body(dup_ref, w_ref, x2_ref, g2_ref, dy_ref, dx2_ref, dx2b_ref, dg_ref):
        dx, dg_rows = _rms_bwd(_dot(dup_ref[...], w_ref[...]), x2_ref[...], g2_ref[...], D_MODEL)
        dx2 = dy_ref[...] + dx
        dx2_ref[...] = dx2
        dx2b_ref[...] = dx2.astype(BF16)
        _acc_rows(dg_ref, dg_rows, pl.program_id(0) == 0)

    return pl.pallas_call(
        body, name="mlp_up_bwd", grid=(s // tm,),
        in_specs=[_row(tm, D_FF), _full((D_FF, D_MODEL)), _row(tm, D_MODEL), _full((1, D_MODEL)), _row(tm, D_MODEL)],
        out_specs=[_row(tm, D_MODEL), _row(tm, D_MODEL), _full((1, D_MODEL))],
        out_shape=[_sds((s, D_MODEL)), _sds((s, D_MODEL), BF16), _sds((1, D_MODEL))],
        compiler_params=_cp(("arbitrary",)),
    )(dup, w_up_t, x2, g2, dy)


def _mix_bwd(dx2b, w_out_t, attn, y, glu_w, glu_b, glu_w_t, ga, gs, tm=512):
    s = attn.shape[0]

    def body(dx2_ref, wot_ref, a_ref, y_ref, gw_ref, gb_ref, gwt_ref, ga_ref, gs_ref,
             da_ref, dys_ref, z_ref, dpre_ref, dga_ref, dgs_ref, dgb_ref):
        first = pl.program_id(0) == 0
        dmix = _dot(dx2_ref[...], wot_ref[...])
        da, dga_rows = _rms_bwd(dmix[:, :ATTN_W], a_ref[...], ga_ref[...], ATTN_W)
        da_ref[...] = da
        yv = y_ref[...]
        z, t = _gelu(yv)
        gate = jax.nn.sigmoid(_dot(z.astype(BF16), gw_ref[...]) + gb_ref[...])
        dso, dgs_rows = _rms_bwd(dmix[:, ATTN_W:], z * gate, gs_ref[...], SSM_W)
        dpre = dso * z * gate * (1.0 - gate)
        dpre_b = dpre.astype(BF16)
        dz = dso * gate + _dot(dpre_b, gwt_ref[...])
        dgelu = 0.5 * (1.0 + t) + 0.5 * yv * (1.0 - t * t) * (GELU_C * (1.0 + 3.0 * 0.044715 * (yv * yv)))
        dys_ref[...] = dz * dgelu
        z_ref[...] = z.astype(BF16)
        dpre_ref[...] = dpre_b
        _acc_rows(dga_ref, dga_rows, first)
        _acc_rows(dgs_ref, dgs_rows, first)
        _acc_rows(dgb_ref, dpre, first)

    vec = _full((1, SSM_W))
    return pl.pallas_call(
        body, name="mix_bwd", grid=(s // tm,),
        in_specs=[_row(tm, D_MODEL), _full((D_MODEL, D_MODEL)), _row(tm, ATTN_W), _row(tm, SSM_W),
                  _full((SSM_W, SSM_W)), vec, _full((SSM_W, SSM_W)), vec, vec],
        out_specs=[_row(tm, ATTN_W), _row(tm, SSM_W), _row(tm, SSM_W), _row(tm, SSM_W), vec, vec, vec],
        out_shape=[_sds((s, ATTN_W)), _sds((s, SSM_W)), _sds((s, SSM_W), BF16), _sds((s, SSM_W), BF16),
                   _sds((1, ATTN_W)), _sds((1, SSM_W)), _sds((1, SSM_W))],
        compiler_params=_cp(("arbitrary",)),
    )(dx2b, w_out_t, attn, y, glu_w, glu_b, glu_w_t, ga, gs)


def _qk_bwd(dqn, dkn, qk, dv, du, gq, gk, gmat, tm=512):
    s = qk.shape[0]

    def body(dq_ref, dk_ref, qk_ref, dv_ref, du_ref, gq_ref, gk_ref, gm_ref, dp_ref, dgq_ref, dgk_ref):
        first = pl.program_id(0) == 0
        gm = gm_ref[...]
        for idx, (d_ref, g_ref, dg_ref) in enumerate(((dq_ref, gq_ref, dgq_ref), (dk_ref, gk_ref, dgk_ref))):
            xv = qk_ref[:, pl.ds(idx * ATTN_W, ATTN_W)]
            dyv = d_ref[...]
            r = lax.rsqrt(_group_mean(xv * xv, gm) + EPS)
            gdy = dyv * g_ref[...]
            dx = r * gdy - xv * (r * r * r) * _group_mean(gdy * xv, gm)
            dp_ref[:, pl.ds(idx * ATTN_W, ATTN_W)] = dx.astype(BF16)
            _acc_rows(dg_ref, dyv * (xv * r), first)
        dp_ref[:, pl.ds(2 * ATTN_W, ATTN_W)] = dv_ref[...].astype(BF16)
        dp_ref[:, pl.ds(3 * ATTN_W, SSM_W)] = du_ref[...].astype(BF16)

    vec = _full((1, ATTN_W))
    return pl.pallas_call(
        body, name="qk_bwd", grid=(s // tm,),
        in_specs=[_row(tm, ATTN_W), _row(tm, ATTN_W), _row(tm, 2 * ATTN_W), _row(tm, ATTN_W), _row(tm, SSM_W),
                  vec, vec, _full((ATTN_W, ATTN_W))],
        out_specs=[_row(tm, 4 * ATTN_W), vec, vec],
        out_shape=[_sds((s, 4 * ATTN_W), BF16), _sds((1, ATTN_W)), _sds((1, ATTN_W))],
        compiler_params=_cp(("arbitrary",)),
    )(dqn, dkn, qk, dv, du, gq, gk, gmat)


def _in_bwd(dproj, w_in_t, x, g1, dx2, ex=None, tm=512):
    s = x.shape[0]
    steps = s // tm

    def body(*refs):
        (dp_ref, w_ref, x_ref, g1_ref, dx2_ref), (gx_ref, dg_ref), _, hx = _carry(ex, refs, 5, 2)
        if ex is not None:
            @pl.when(pl.program_id(0) == 0)
            def _():
                ex.start(*hx)

        dx, dg_rows = _rms_bwd(_dot(dp_ref[...], w_ref[...]), x_ref[...], g1_ref[...], D_MODEL)
        gx_ref[...] = dx2_ref[...] + dx
        _acc_rows(dg_ref, dg_rows, pl.program_id(0) == 0)
        if ex is not None:
            @pl.when(pl.program_id(0) == steps - 1)
            def _():
                ex.wait(*hx)

    hosted = ex is not None
    return pl.pallas_call(
        body, name="in_bwd", grid=(steps,),
        in_specs=[_row(tm, 4 * ATTN_W), _full((4 * ATTN_W, D_MODEL)), _row(tm, D_MODEL), _full((1, D_MODEL)),
                  _row(tm, D_MODEL)] + (ex.specs() if hosted else []),
        out_specs=[_row(tm, D_MODEL), _full((1, D_MODEL))] + (ex.specs() if hosted else []),
        out_shape=[_sds((s, D_MODEL)), _sds((1, D_MODEL))] + (ex.out_shape() if hosted else []),
        scratch_shapes=ex.scratch() if hosted else [],
        compiler_params=_cp(("arbitrary",)),
    )(dproj, w_in_t, x, g1, dx2, *(ex.srcs if hosted else []))


def _mm_tn(a, b, name, ts=1024):
    s, k = a.shape
    n = b.shape[1]
    bk, bn = min(k, 1024), min(n, 1024)

    def body(a_ref, b_ref, o_ref):
        @pl.when(pl.program_id(2) == 0)
        def _():
            o_ref[...] = jnp.zeros_like(o_ref)

        o_ref[...] += _dot_tn(a_ref[...], b_ref[...])

    return pl.pallas_call(
        body, name=name, grid=(k // bk, n // bn, s // ts),
        in_specs=[pl.BlockSpec((ts, bk), lambda i, j, t: (t, i)), pl.BlockSpec((ts, bn), lambda i, j, t: (t, j))],
        out_specs=pl.BlockSpec((bk, bn), lambda i, j, t: (i, j)), out_shape=_sds((k, n)),
        compiler_params=_cp(("parallel", "parallel", "arbitrary")),
    )(a, b)


def _peer(k):
    x, y, c = lax.axis_index("x"), lax.axis_index("y"), lax.axis_index("c")
    px = 1 - x if k & 4 else x
    py = 1 - y if k & 2 else y
    pc = 1 - c if k & 1 else c
    return (px, py, pc), 4 * px + 2 * py + pc


def _gather_rows(x_shard):
    m_per, n = x_shard.shape

    def body(x_ref, out_ref, send_sems, recv_sems, local_sem):
        x, y, c = lax.axis_index("x"), lax.axis_index("y"), lax.axis_index("c")
        me, sibling = (x, y, c), (x, y, 1 - c)
        chips = [(1 - x, y), (x, 1 - y), (1 - x, 1 - y)]

        def rows(px, py, pc):
            return out_ref.at[pl.ds((4 * px + 2 * py + pc) * m_per, m_per), :]

        def copy(k, block, to, src=None):
            return pltpu.make_async_remote_copy(
                src_ref=rows(*block) if src is None else src, dst_ref=rows(*block),
                send_sem=send_sems.at[k], recv_sem=recv_sems.at[k], device_id=to, device_id_type=MESH)

        mine = pltpu.make_async_copy(x_ref, rows(*me), local_sem)
        mine.start()
        first = [copy(0, me, sibling, src=x_ref)]
        first += [copy(1 + j, me, (*chip, c), src=x_ref) for j, chip in enumerate(chips)]
        for cp in first:
            cp.start()
        passed = [copy(4 + j, (*chip, c), sibling) for j, chip in enumerate(chips)]
        for j, chip in enumerate(chips):
            copy(1 + j, (*chip, c), me).wait_recv()
            passed[j].start()
        copy(0, sibling, me).wait_recv()
        for j, chip in enumerate(chips):
            copy(4 + j, (*chip, 1 - c), me).wait_recv()
        for cp in first + passed:
            cp.wait_send()
        mine.wait()

    return pl.pallas_call(
        body, name="gather_weights", out_shape=_sds((N_DEV * m_per, n), x_shard.dtype),
        in_specs=[pl.BlockSpec(memory_space=pltpu.VMEM)], out_specs=pl.BlockSpec(memory_space=pltpu.VMEM),
        scratch_shapes=[pltpu.SemaphoreType.DMA((7,)), pltpu.SemaphoreType.DMA((7,)), pltpu.SemaphoreType.DMA],
        compiler_params=pltpu.CompilerParams(vmem_limit_bytes=VMEM_LIMIT),
    )(x_shard)


class _Exchange:
    def __init__(self, srcs, whole):
        self.srcs, self.whole, self.n = list(srcs), list(whole), len(srcs)
        self.rows = [a.shape[0] if w else a.shape[0] // N_DEV for a, w in zip(self.srcs, self.whole)]

    def specs(self):
        return [pl.BlockSpec(memory_space=pl.ANY)] * self.n

    def out_shape(self):
        return [_sds((N_DEV, r, a.shape[1]), a.dtype) for r, a in zip(self.rows, self.srcs)]

    def scratch(self):
        return [pltpu.SemaphoreType.DMA((self.n * 7,)), pltpu.SemaphoreType.DMA((self.n * 7,)),
                pltpu.SemaphoreType.DMA((self.n,))]

    def _copies(self, ins, outs, sems):
        send_sems, recv_sems, local_sems = sems
        _, me = _peer(0)
        for w in range(self.n):
            for k in range(N_DEV):
                peer, pidx = _peer(k)
                src = ins[w] if self.whole[w] else ins[w].at[pl.ds(pidx * self.rows[w], self.rows[w]), :]
                if k == 0:
                    yield k, pltpu.make_async_copy(src, outs[w].at[me], local_sems.at[w]), None
                else:
                    sem = w * 7 + k - 1
                    out = pltpu.make_async_remote_copy(src_ref=src, dst_ref=outs[w].at[me], send_sem=send_sems.at[sem],
                                                       recv_sem=recv_sems.at[sem], device_id=peer, device_id_type=MESH)
                    back = pltpu.make_async_remote_copy(src_ref=src, dst_ref=outs[w].at[pidx], send_sem=send_sems.at[sem],
                                                        recv_sem=recv_sems.at[sem], device_id=peer, device_id_type=MESH)
                    yield k, out, back

    def start(self, ins, outs, sems):
        for _, out, _ in self._copies(ins, outs, sems):
            out.start()

    def wait(self, ins, outs, sems):
        for k, out, back in self._copies(ins, outs, sems):
            if k == 0:
                out.wait()
            else:
                back.wait_recv()
                out.wait_send()


def _carry(ex, refs, n_in, n_out):
    nh = ex.n if ex is not None else 0
    ins, hin = refs[:n_in], refs[n_in:n_in + nh]
    outs = refs[n_in + nh:n_in + nh + n_out]
    hout = refs[n_in + nh + n_out:n_in + 2 * nh + n_out]
    rest = refs[n_in + 2 * nh + n_out:]
    if ex is None:
        return ins, outs, rest, None
    return ins, outs, rest[:-3], (hin, hout, rest[-3:])


def _exchange_now(srcs, whole, name):
    ex = _Exchange(srcs, whole)

    def body(*refs):
        _, _, _, (hin, hout, sems) = _carry(ex, refs, 0, 0)
        ex.start(hin, hout, sems)
        ex.wait(hin, hout, sems)

    return pl.pallas_call(body, name=name, out_shape=ex.out_shape(), in_specs=ex.specs(), out_specs=ex.specs(),
                          scratch_shapes=ex.scratch())(*srcs)


def _adamw(w, m, v, gparts, name):
    r, c = w.shape
    tr = r if r * c <= 256 * 1024 else 128 * 1024 // c

    def body(w_ref, m_ref, v_ref, g_ref, go_ref, d_ref, mo_ref, vo_ref):
        g = g_ref[0].astype(F32)
        for i in range(1, N_DEV):
            g = g + g_ref[i].astype(F32)
        nm = ADAM_B1 * m_ref[...] + (1.0 - ADAM_B1) * g
        nv = ADAM_B2 * v_ref[...] + (1.0 - ADAM_B2) * (g * g)
        m_hat = nm / (1.0 - ADAM_B1 ** ADAM_STEP)
        v_hat = nv / (1.0 - ADAM_B2 ** ADAM_STEP)
        go_ref[...] = g
        d_ref[...] = -ADAM_LR * (m_hat / (jnp.sqrt(v_hat) + ADAM_EPS) + ADAM_WD * w_ref[...])
        mo_ref[...] = nm
        vo_ref[...] = nv

    blk = pl.BlockSpec((tr, c), lambda i: (i, 0))
    return pl.pallas_call(
        body, name=name, grid=(r // tr,),
        in_specs=[blk, blk, blk, pl.BlockSpec((N_DEV, tr, c), lambda i: (0, i, 0))],
        out_specs=[blk] * 4, out_shape=[_sds((r, c))] * 4,
        compiler_params=_cp(("parallel",)),
    )(w, m, v, gparts)


def _block_diag(a, states_first):
    a4 = a.reshape(N_CB, 8, SSM_GROUP, SSM_STATE)
    eye = jnp.eye(8, dtype=a.dtype)
    if states_first:
        return jnp.einsum("bgcp,gh->bgphc", a4, eye).reshape(N_CB, CB_STATES, LANES)
    return jnp.einsum("bgcp,gh->bgchp", a4, eye).reshape(N_CB, LANES, CB_STATES)


def _block_diag_of(full, states_first):
    if states_first:
        picked = jnp.einsum("bgphc,gh->bgcp", full.reshape(N_CB, 8, SSM_STATE, 8, SSM_GROUP), jnp.eye(8, dtype=full.dtype))
    else:
        picked = jnp.einsum("bgchp,gh->bgcp", full.reshape(N_CB, 8, SSM_GROUP, 8, SSM_STATE), jnp.eye(8, dtype=full.dtype))
    return picked.reshape(SSM_GROUPS, SSM_GROUP, SSM_STATE)


SMALL_EARLY = ("ssm_a_re", "ssm_a_im", "ssm_log_dt", "ssm_b_re", "ssm_b_im", "ssm_c_re", "ssm_c_im", "ssm_d", "glu_b",
               "attn_out_norm_g", "ssm_out_norm_g", "norm2_g")
SMALL_MID = ("q_norm_g", "k_norm_g")
SMALL_LATE = ("norm1_g",)
SMALL = SMALL_EARLY + SMALL_MID + SMALL_LATE


def _pack_small(arrs):
    parts = []
    for a in arrs:
        flat = a.reshape(-1)
        rows = -(-flat.shape[0] // (8 * LANES)) * 8
        parts.append(jnp.pad(flat, (0, rows * LANES - flat.shape[0])).reshape(rows, LANES))
    return jnp.concatenate(parts, axis=0)


def _unpack_small(packed, shapes):
    out, r0 = [], 0
    for shp in shapes:
        size = math.prod(shp)
        rows = -(-size // (8 * LANES)) * 8
        out.append(packed[r0:r0 + rows].reshape(-1)[:size].reshape(shp))
        r0 += rows
    return out


def kernel(x, norm1_g, w_in, q_norm_g, k_norm_g, ssm_a_re, ssm_a_im, ssm_log_dt, ssm_b_re, ssm_b_im, ssm_c_re, ssm_c_im, ssm_d, glu_w, glu_b, attn_out_norm_g, ssm_out_norm_g, w_out, norm2_g, w_mlp_up, w_mlp_down, loss_target, m_norm1_g, m_w_in, m_q_norm_g, m_k_norm_g, m_ssm_a_re, m_ssm_a_im, m_ssm_log_dt, m_ssm_b_re, m_ssm_b_im, m_ssm_c_re, m_ssm_c_im, m_ssm_d, m_glu_w, m_glu_b, m_attn_out_norm_g, m_ssm_out_norm_g, m_w_out, m_norm2_g, m_w_mlp_up, m_w_mlp_down, v_norm1_g, v_w_in, v_q_norm_g, v_k_norm_g, v_ssm_a_re, v_ssm_a_im, v_ssm_log_dt, v_ssm_b_re, v_ssm_b_im, v_ssm_c_re, v_ssm_c_im, v_ssm_d, v_glu_w, v_glu_b, v_attn_out_norm_g, v_ssm_out_norm_g, v_w_out, v_norm2_g, v_w_mlp_up, v_w_mlp_down):
    weights = dict(norm1_g=norm1_g, w_in=w_in, q_norm_g=q_norm_g, k_norm_g=k_norm_g, ssm_a_re=ssm_a_re,
                   ssm_a_im=ssm_a_im, ssm_log_dt=ssm_log_dt, ssm_b_re=ssm_b_re, ssm_b_im=ssm_b_im,
                   ssm_c_re=ssm_c_re, ssm_c_im=ssm_c_im, ssm_d=ssm_d, glu_w=glu_w, glu_b=glu_b,
                   attn_out_norm_g=attn_out_norm_g, ssm_out_norm_g=ssm_out_norm_g, w_out=w_out, norm2_g=norm2_g,
                   w_mlp_up=w_mlp_up, w_mlp_down=w_mlp_down)
    mom_m = dict(norm1_g=m_norm1_g, w_in=m_w_in, q_norm_g=m_q_norm_g, k_norm_g=m_k_norm_g, ssm_a_re=m_ssm_a_re,
                 ssm_a_im=m_ssm_a_im, ssm_log_dt=m_ssm_log_dt, ssm_b_re=m_ssm_b_re, ssm_b_im=m_ssm_b_im,
                 ssm_c_re=m_ssm_c_re, ssm_c_im=m_ssm_c_im, ssm_d=m_ssm_d, glu_w=m_glu_w, glu_b=m_glu_b,
                 attn_out_norm_g=m_attn_out_norm_g, ssm_out_norm_g=m_ssm_out_norm_g, w_out=m_w_out,
                 norm2_g=m_norm2_g, w_mlp_up=m_w_mlp_up, w_mlp_down=m_w_mlp_down)
    mom_v = dict(norm1_g=v_norm1_g, w_in=v_w_in, q_norm_g=v_q_norm_g, k_norm_g=v_k_norm_g, ssm_a_re=v_ssm_a_re,
                 ssm_a_im=v_ssm_a_im, ssm_log_dt=v_ssm_log_dt, ssm_b_re=v_ssm_b_re, ssm_b_im=v_ssm_b_im,
                 ssm_c_re=v_ssm_c_re, ssm_c_im=v_ssm_c_im, ssm_d=v_ssm_d, glu_w=v_glu_w, glu_b=v_glu_b,
                 attn_out_norm_g=v_attn_out_norm_g, ssm_out_norm_g=v_ssm_out_norm_g, w_out=v_w_out,
                 norm2_g=v_norm2_g, w_mlp_up=v_w_mlp_up, w_mlp_down=v_w_mlp_down)
    order = list(weights)

    xs, tgt = x[0], loss_target[0]
    s = xs.shape[0]
    assert s % ATTN_CHUNK == 0 and (s // N_SEG) % SSM_LK == 0
    seg_len = s // N_SEG
    n_sq = seg_len.bit_length() - 1
    assert 1 << n_sq == seg_len

    w_in_t = _gather_rows(w_in[0].T.astype(BF16))
    w_in_full = w_in_t.T
    later = _Exchange([glu_w[0].astype(BF16), w_out[0].astype(BF16), w_mlp_up[0].T.astype(BF16),
                       w_mlp_down[0].astype(BF16)], [True] * 4)

    gq = jnp.tile(q_norm_g[0], ATTN_W // HEAD_DIM)[None]
    gk = jnp.tile(k_norm_g[0], ATTN_W // HEAD_DIM)[None]
    lane = jnp.arange(ATTN_W) // HEAD_DIM
    gmat = jnp.where(lane[:, None] == lane[None, :], 1.0 / HEAD_DIM, 0.0).astype(BF16)
    a_re3 = ssm_a_re[0][:, None, :]
    a_im3 = ssm_a_im[0][:, None, :]
    ldt3 = ssm_log_dt[0][:, None, None]
    b_re_t = jnp.swapaxes(ssm_b_re[0], 1, 2)
    b_im_t = jnp.swapaxes(ssm_b_im[0], 1, 2)
    c_re, c_im = ssm_c_re[0], ssm_c_im[0]
    dskip = ssm_d[0].reshape(1, SSM_W)

    qk, qn, kn, v, u, xn = _fwd_proj(xs, norm1_g, w_in_full, gq, gk, gmat)
    attn, lse, glu_g, w_out_g, w_up_g, w_down_g = _attn_fwd(qn, kn, v, later)
    glu_full = glu_g.reshape(SSM_W, SSM_W)
    w_out_full = w_out_g.reshape(D_MODEL, D_MODEL)
    w_up_t = w_up_g.reshape(D_FF, D_MODEL)
    w_down_full = w_down_g.reshape(D_FF, D_MODEL)
    w_up_full, w_out_t, w_down_t, glu_t = w_up_t.T, w_out_full.T, w_down_full.T, glu_full.T

    abr3, abi3, bbr, bbi = _ssm_discretize(a_re3, a_im3, ldt3, b_re_t, b_im_t)
    abr, abi = abr3.reshape(1, N_STATE), abi3.reshape(1, N_STATE)
    pw_r, pw_i = _ssm_power(abr, abi, n_sq)
    bb_r, bb_i = _block_diag(bbr, False).astype(BF16), _block_diag(bbi, False).astype(BF16)
    bbt_r, bbt_i = _block_diag(bbr, True).astype(BF16), _block_diag(bbi, True).astype(BF16)
    cc_r, cc_i = _block_diag(c_re, True).astype(BF16), _block_diag(c_im, True).astype(BF16)
    cct_r, cct_i = _block_diag(c_re, False).astype(BF16), _block_diag(c_im, False).astype(BF16)
    seg_major = lambda a: jnp.swapaxes(a.reshape(N_SEG, seg_len, SSM_W), 0, 1)
    seg_minor = lambda a: jnp.swapaxes(a, 0, 1).reshape(s, SSM_W)
    u3 = seg_major(u)
    zero_fin = jnp.zeros((N_SEG, N_STATE), F32)
    ssm_args = (abr, abi, pw_r, pw_i)
    xf_r, xf_i = _ssm_fwd(u3, *ssm_args, zero_fin, zero_fin, bb_r, bb_i, cc_r, cc_i, dskip, True)
    y3, xst_r, xst_i = _ssm_fwd(u3, *ssm_args, xf_r, xf_i, bb_r, bb_i, cc_r, cc_i, dskip, False)
    y = seg_minor(y3)

    mix, x2, h = _fwd_mix(attn, y, xs, glu_full, glu_b, attn_out_norm_g, ssm_out_norm_g, w_out_full, norm2_g)
    r_act, hdn = _mlp_up(h, w_up_full)
    dy, dyb, sse = _mlp_down_loss(hdn, w_down_full, x2, tgt)
    loss = lax.psum(0.5 * sse[0, 0] / D_MODEL, ("x", "y", "c"))

    dup = _mlp_down_bwd(dyb, w_down_t, r_act)
    g_w_down = _mm_tn(hdn, dyb, "grad_w_down")
    dx2, dx2b, g_norm2 = _mlp_up_bwd(dup, w_up_t, x2, norm2_g, dy)
    g_w_up_t = _mm_tn(dup, h, "grad_w_up_t")
    dattn, dys, z_b, dpre_b, g_ga, g_gs, g_glu_b = _mix_bwd(dx2b, w_out_t, attn, y, glu_full, glu_b, glu_t,
                                                             attn_out_norm_g, ssm_out_norm_g)
    g_w_out = _mm_tn(mix, dx2b, "grad_w_out")
    g_glu_w = _mm_tn(z_b, dpre_b, "grad_glu_w")
    dy3 = seg_major(dys)
    bwd_args = (u3, dy3, xst_r, xst_i, abr, abi, pw_r, pw_i)
    lf_r, lf_i = _ssm_bwd(*bwd_args, zero_fin, zero_fin, bb_r, bb_i, bbt_r, bbt_i, cct_r, cct_i, dskip, True)
    du3, dab_r, dab_i, dbb_r, dbb_i, dcc_r, dcc_i, g_d = _ssm_bwd(*bwd_args, lf_r, lf_i, bb_r, bb_i, bbt_r, bbt_i,
                                                                 cct_r, cct_i, dskip, False)
    g_a_re3, g_a_im3, g_ldt3, g_b_re_t, g_b_im_t = _ssm_discretize_bwd(
        a_re3, a_im3, ldt3, b_re_t, b_im_t, dab_r.reshape(a_re3.shape), dab_i.reshape(a_re3.shape),
        _block_diag_of(dbb_r, True), _block_diag_of(dbb_i, True))
    g_c_re, g_c_im = _block_diag_of(dcc_r, False), _block_diag_of(dcc_i, False)
    small_grads = dict(
        ssm_a_re=g_a_re3.reshape(ssm_a_re.shape), ssm_a_im=g_a_im3.reshape(ssm_a_im.shape),
        ssm_log_dt=g_ldt3.reshape(ssm_log_dt.shape), ssm_b_re=jnp.swapaxes(g_b_re_t, 1, 2)[None],
        ssm_b_im=jnp.swapaxes(g_b_im_t, 1, 2)[None], ssm_c_re=g_c_re[None], ssm_c_im=g_c_im[None],
        ssm_d=g_d.reshape(ssm_d.shape), glu_b=g_glu_b, attn_out_norm_g=g_ga, ssm_out_norm_g=g_gs, norm2_g=g_norm2)

    early = _Exchange([g.astype(BF16) for g in (g_glu_w, g_w_out, g_w_up_t, g_w_down)]
                      + [_pack_small([small_grads[n] for n in SMALL_EARLY])], [False] * 4 + [True])
    dqn, dkn, dv, p_glu, p_w_out, p_w_up, p_w_down, p_early = _attn_bwd(qn, kn, v, attn, lse, dattn, early)

    dproj, g_gq, g_gk = _qk_bwd(dqn, dkn, qk, dv, seg_minor(du3), gq, gk, gmat)
    g_w_in_t = _mm_tn(dproj, xn, "grad_w_in_t")
    small_grads["q_norm_g"] = g_gq.reshape(ATTN_W // HEAD_DIM, HEAD_DIM).sum(0)[None]
    small_grads["k_norm_g"] = g_gk.reshape(ATTN_W // HEAD_DIM, HEAD_DIM).sum(0)[None]
    mid = _Exchange([g_w_in_t.astype(BF16), _pack_small([small_grads[n] for n in SMALL_MID])], [False, True])
    grad_x, g_norm1, p_w_in, p_mid = _in_bwd(dproj, w_in_t, xs, norm1_g, dx2, mid)
    (p_late,) = _exchange_now([_pack_small([g_norm1])], [True], "exchange_norm1")

    transposed = {"w_in", "w_mlp_up"}
    res = {}
    for name, gp in (("w_in", p_w_in), ("glu_w", p_glu), ("w_out", p_w_out), ("w_mlp_up", p_w_up), ("w_mlp_down", p_w_down)):
        tr = (lambda a: a.T) if name in transposed else (lambda a: a)
        outs = _adamw(tr(weights[name][0]), tr(mom_m[name][0]), tr(mom_v[name][0]), gp, "adamw_" + name)
        res[name] = [tr(o)[None] for o in outs]
    outs = _adamw(_pack_small([weights[n] for n in SMALL]), _pack_small([mom_m[n] for n in SMALL]),
                  _pack_small([mom_v[n] for n in SMALL]), jnp.concatenate([p_early, p_mid, p_late], axis=1), "adamw_small")
    shapes = [weights[n].shape for n in SMALL]
    unpacked = [_unpack_small(o, shapes) for o in outs]
    for i, n in enumerate(SMALL):
        res[n] = [unpacked[j][i] for j in range(4)]

    return (loss, grad_x[None], *[res[n][0] for n in order], *[res[n][1] for n in order],
            *[res[n][2] for n in order], *[res[n][3] for n in order])
```

```python
import math

import jax
import jax.numpy as jnp
from jax import lax
from jax.experimental import pallas as pl
from jax.experimental.pallas import tpu as pltpu

F32 = jnp.float32
BF16 = jnp.bfloat16

D_MODEL = 1024
ATTN_W = 512
HEAD_DIM = 64
SSM_W = 512
SSM_GROUP = 16
SSM_GROUPS = 32
SSM_STATE = 64
N_STATE = SSM_GROUPS * SSM_STATE
D_FF = 4096
EPS = 1e-6
NEG_INF = -1e30
ATTN_CHUNK = 2048
ATTN_BLOCK = 128
DILATIONS = (1, 4, 16)
N_SEG = 8
SSM_LK = 64
N_DEV = 8
LANES = 128

ADAM_LR = 0.001
ADAM_B1 = 0.9
ADAM_B2 = 0.999
ADAM_EPS = 1e-08
ADAM_WD = 0.01
ADAM_STEP = 10

VMEM_LIMIT = 56 * 1024 * 1024
GELU_C = math.sqrt(2.0 / math.pi)
MESH = pl.DeviceIdType.MESH


def _cp(sem, vmem=VMEM_LIMIT):
    return pltpu.CompilerParams(dimension_semantics=sem, vmem_limit_bytes=vmem)


def _dot(a, b):
    return jnp.dot(a, b, preferred_element_type=F32)


def _dot_nt(a, b):
    return lax.dot_general(a, b, (((1,), (1,)), ((), ())), preferred_element_type=F32)


def _dot_tn(a, b):
    return lax.dot_general(a, b, (((0,), (0,)), ((), ())), preferred_element_type=F32)


def _group_mean(x2, gmat):
    hi = x2.astype(BF16)
    lo = (x2 - hi.astype(F32)).astype(BF16)
    return _dot(hi, gmat) + _dot(lo, gmat)


def _rms(x):
    return lax.rsqrt(jnp.mean(x * x, axis=-1, keepdims=True) + EPS)


def _rms_bwd(dy, x, g, n):
    r = _rms(x)
    gdy = dy * g
    dx = r * gdy - x * (r * r * r) * (jnp.sum(gdy * x, axis=-1, keepdims=True) / n)
    return dx, dy * (x * r)


def _gelu(y):
    t = jnp.tanh(GELU_C * (y + 0.044715 * (y * y * y)))
    return 0.5 * y * (1.0 + t), t


def _full(shape):
    nd = len(shape)
    return pl.BlockSpec(shape, lambda *_: (0,) * nd)


def _sds(shape, dtype=F32):
    return jax.ShapeDtypeStruct(shape, dtype)


def _fwd_proj(x, g1, w_in, gq, gk, gmat, tm=512):
    s = x.shape[0]

    def body(x_ref, g1_ref, w_ref, gq_ref, gk_ref, gm_ref, qk_ref, qn_ref, kn_ref, v_ref, u_ref, xn_ref):
        xv = x_ref[...]
        xnb = ((xv * _rms(xv)) * g1_ref[...]).astype(BF16)
        xn_ref[...] = xnb
        proj = _dot(xnb, w_ref[...])
        q = proj[:, :ATTN_W]
        k = proj[:, ATTN_W:2 * ATTN_W]
        qk_ref[...] = proj[:, :2 * ATTN_W]
        v_ref[...] = proj[:, 2 * ATTN_W:3 * ATTN_W]
        u_ref[...] = proj[:, 3 * ATTN_W:]
        gm = gm_ref[...]
        qn_ref[...] = (q * lax.rsqrt(_group_mean(q * q, gm) + EPS)) * gq_ref[...]
        kn_ref[...] = (k * lax.rsqrt(_group_mean(k * k, gm) + EPS)) * gk_ref[...]

    row = lambda w: pl.BlockSpec((tm, w), lambda i: (i, 0))
    return pl.pallas_call(
        body, name="fwd_proj", grid=(s // tm,),
        in_specs=[row(D_MODEL), _full((1, D_MODEL)), _full((D_MODEL, 4 * ATTN_W)), _full((1, ATTN_W)),
                  _full((1, ATTN_W)), _full((ATTN_W, ATTN_W))],
        out_specs=[row(2 * ATTN_W), row(ATTN_W), row(ATTN_W), row(ATTN_W), row(ATTN_W), row(D_MODEL)],
        out_shape=[_sds((s, 2 * ATTN_W)), _sds((s, ATTN_W)), _sds((s, ATTN_W)), _sds((s, ATTN_W)),
                   _sds((s, ATTN_W)), _sds((s, D_MODEL), BF16)],
        compiler_params=_cp(("parallel",)),
    )(x, g1, w_in, gq, gk, gmat)


def _attn_rows(t, d, nb):
    if d == 1:
        q0 = pl.multiple_of(t * ATTN_BLOCK, ATTN_BLOCK)
        return (t, pl.ds(q0, ATTN_BLOCK), pl.ds(pl.multiple_of(ATTN_CHUNK + q0, ATTN_BLOCK), ATTN_BLOCK),
                pl.ds(pl.multiple_of(ATTN_CHUNK - ATTN_BLOCK + q0, ATTN_BLOCK), ATTN_BLOCK))
    r = t // nb
    b = t % nb
    return (b, pl.ds(ATTN_BLOCK * b * d + r, ATTN_BLOCK, stride=d),
            pl.ds(ATTN_CHUNK + ATTN_BLOCK * b * d + r, ATTN_BLOCK, stride=d),
            pl.ds(ATTN_CHUNK + ATTN_BLOCK * (b - 1) * d + r, ATTN_BLOCK, stride=d))


def _attn_masks():
    row = lax.broadcasted_iota(jnp.int32, (ATTN_BLOCK, LANES), 0)
    col = lax.broadcasted_iota(jnp.int32, (ATTN_BLOCK, LANES), 1)
    return row, col


NBLK = ATTN_CHUNK // ATTN_BLOCK


def _attn_bias(bias_s):
    row, col = _attn_masks()
    bias_s[:, pl.ds(0, LANES)] = jnp.where(col >= row, 0.0, NEG_INF)
    bias_s[:, pl.ds(LANES, LANES)] = jnp.where(col <= row, 0.0, NEG_INF)
    return col < HEAD_DIM


def _attn_fwd(qn, kn, v, ex=None, unroll=8):
    s = qn.shape[0]
    nch = s // ATTN_CHUNK
    scale = HEAD_DIM ** -0.5
    npat = len(DILATIONS)
    n_hp = ATTN_W // LANES

    def body(*refs):
        ((q_ref, kp_ref, kc_ref, vp_ref, vc_ref), (o_ref, lse_ref),
         (kk, vv, kt_s, vb_s, bias_s, m_s, a_s), hx) = _carry(ex, refs, 5, 2)
        i = pl.program_id(1)
        if ex is not None:
            @pl.when(jnp.logical_and(pl.program_id(0) == 0, i == 0))
            def _():
                ex.start(*hx)

        kk[pl.ds(0, ATTN_CHUNK), :] = kp_ref[...]
        kk[pl.ds(ATTN_CHUNK, ATTN_CHUNK), :] = kc_ref[...]
        vv[pl.ds(0, ATTN_CHUNK), :] = vp_ref[...]
        vv[pl.ds(ATTN_CHUNK, ATTN_CHUNK), :] = vc_ref[...]
        head0 = _attn_bias(bias_s)

        for p, d in enumerate(DILATIONS):
            nb = ATTN_CHUNK // (ATTN_BLOCK * d)

            def prep(t, carry, d=d, nb=nb):
                _, _, crows, prows = _attn_rows(t, d, nb)
                kt_s[t, :, pl.ds(0, LANES)] = kk[prows, :].T.astype(BF16)
                kt_s[t, :, pl.ds(LANES, LANES)] = kk[crows, :].T.astype(BF16)
                vp = vv[prows, :]
                vc = vv[crows, :]
                vb_s[2 * t, pl.ds(0, ATTN_BLOCK), :] = jnp.where(head0, vp, 1.0).astype(BF16)
                vb_s[2 * t, pl.ds(ATTN_BLOCK, ATTN_BLOCK), :] = jnp.where(head0, vc, 1.0).astype(BF16)
                vb_s[2 * t + 1, pl.ds(0, ATTN_BLOCK), :] = jnp.where(head0, 1.0, vp).astype(BF16)
                vb_s[2 * t + 1, pl.ds(ATTN_BLOCK, ATTN_BLOCK), :] = jnp.where(head0, 1.0, vc).astype(BF16)
                return carry

            lax.fori_loop(0, NBLK, prep, 0, unroll=4)

            def blk(t, carry, p=p, d=d, nb=nb):
                b, qrows, _, _ = _attn_rows(t, d, nb)
                q = q_ref[qrows, :]
                pen = jnp.where(jnp.logical_or(i > 0, b > 0), 0.0, NEG_INF)
                kt = kt_s[t]
                outs, ms = [], []
                for h in range(2):
                    hm = head0 if h == 0 else jnp.logical_not(head0)
                    qh = jnp.where(hm, q, 0.0).astype(BF16)
                    sc = _dot(qh, kt) * scale + bias_s[...]
                    s_p = sc[:, :LANES] + pen
                    s_c = sc[:, LANES:]
                    m = jnp.max(jnp.maximum(s_p, s_c), axis=-1, keepdims=True)
                    e_b = jnp.concatenate([jnp.exp(s_p - m), jnp.exp(s_c - m)], axis=1).astype(BF16)
                    outs.append(_dot(e_b, vb_s[2 * t + h]))
                    ms.append(jnp.broadcast_to(m, (ATTN_BLOCK, LANES)))
                m_s[p, qrows, :] = jnp.where(head0, ms[0], ms[1])
                a_s[2 * p, qrows, :] = outs[0]
                a_s[2 * p + 1, qrows, :] = outs[1]
                return carry

            lax.fori_loop(0, NBLK, blk, 0, unroll=unroll)

        def merge(t, carry):
            rows = pl.ds(pl.multiple_of(t * ATTN_BLOCK, ATTN_BLOCK), ATTN_BLOCK)
            m_all = m_s[0, rows, :]
            for p in range(1, npat):
                m_all = jnp.maximum(m_all, m_s[p, rows, :])
            num = jnp.zeros((ATTN_BLOCK, LANES), F32)
            den = jnp.zeros((ATTN_BLOCK, LANES), F32)
            for p in range(npat):
                w = jnp.exp(m_s[p, rows, :] - m_all)
                a0, a1 = a_s[2 * p, rows, :], a_s[2 * p + 1, rows, :]
                num = num + jnp.where(head0, a0, a1) * w
                den = den + pltpu.roll(jnp.where(head0, a1, a0), HEAD_DIM, 1) * w
            o_ref[rows, :] = num / den
            lse_ref[rows, :] = m_all + jnp.log(den)
            return carry

        lax.fori_loop(0, NBLK, merge, 0, unroll=2)
        if ex is not None:
            @pl.when(jnp.logical_and(pl.program_id(0) == n_hp - 1, i == nch - 1))
            def _():
                ex.wait(*hx)

    cur = pl.BlockSpec((ATTN_CHUNK, LANES), lambda h, i: (i, h))
    prev = pl.BlockSpec((ATTN_CHUNK, LANES), lambda h, i: (jnp.maximum(i - 1, 0), h))
    hosted = ex is not None
    return pl.pallas_call(
        body, name="attn_fwd", grid=(n_hp, nch),
        in_specs=[cur, prev, cur, prev, cur] + (ex.specs() if hosted else []),
        out_specs=[cur, cur] + (ex.specs() if hosted else []),
        out_shape=[_sds((s, ATTN_W)), _sds((s, ATTN_W))] + (ex.out_shape() if hosted else []),
        scratch_shapes=[pltpu.VMEM((2 * ATTN_CHUNK, LANES), F32), pltpu.VMEM((2 * ATTN_CHUNK, LANES), F32),
                        pltpu.VMEM((NBLK, LANES, 2 * LANES), BF16), pltpu.VMEM((2 * NBLK, 2 * ATTN_BLOCK, LANES), BF16),
                        pltpu.VMEM((ATTN_BLOCK, 2 * LANES), F32),
                        pltpu.VMEM((npat, ATTN_CHUNK, LANES), F32), pltpu.VMEM((2 * npat, ATTN_CHUNK, LANES), F32)]
        + (ex.scratch() if hosted else []),
        compiler_params=_cp(("arbitrary", "arbitrary")),
    )(qn, kn, kn, v, v, *(ex.srcs if hosted else []))


def _attn_bwd(qn, kn, v, o, lse, do, ex=None, group=4):
    s = qn.shape[0]
    nch = s // ATTN_CHUNK
    scale = HEAD_DIM ** -0.5
    npat = len(DILATIONS)
    n_hp = ATTN_W // LANES

    def body(*refs):
        ((q_ref, kp_ref, kc_ref, vp_ref, vc_ref, o_ref, lse_ref, do_ref), (dq_ref, dk_ref, dv_ref),
         (kk, vv, dkk, dvv, kt_s, vt_s, kn_s, bias_s, dq_s, dl_s, dkb, dvb), hx) = _carry(ex, refs, 8, 3)
        step = pl.program_id(1)
        i = nch - 1 - step
        if ex is not None:
            @pl.when(jnp.logical_and(pl.program_id(0) == 0, step == 0))
            def _():
                ex.start(*hx)

        kk[pl.ds(0, ATTN_CHUNK), :] = kp_ref[...]
        kk[pl.ds(ATTN_CHUNK, ATTN_CHUNK), :] = kc_ref[...]
        vv[pl.ds(0, ATTN_CHUNK), :] = vp_ref[...]
        vv[pl.ds(ATTN_CHUNK, ATTN_CHUNK), :] = vc_ref[...]

        @pl.when(step == 0)
        def _():
            dkk[pl.ds(ATTN_CHUNK, ATTN_CHUNK), :] = jnp.zeros((ATTN_CHUNK, LANES), F32)
            dvv[pl.ds(ATTN_CHUNK, ATTN_CHUNK), :] = jnp.zeros((ATTN_CHUNK, LANES), F32)

        @pl.when(step > 0)
        def _():
            dkk[pl.ds(ATTN_CHUNK, ATTN_CHUNK), :] = dkk[pl.ds(0, ATTN_CHUNK), :]
            dvv[pl.ds(ATTN_CHUNK, ATTN_CHUNK), :] = dvv[pl.ds(0, ATTN_CHUNK), :]

        dkk[pl.ds(0, ATTN_CHUNK), :] = jnp.zeros((ATTN_CHUNK, LANES), F32)
        dvv[pl.ds(0, ATTN_CHUNK), :] = jnp.zeros((ATTN_CHUNK, LANES), F32)
        head0 = _attn_bias(bias_s)

        def delta(t, carry):
            rows = pl.ds(pl.multiple_of(t * ATTN_BLOCK, ATTN_BLOCK), ATTN_BLOCK)
            prod = do_ref[rows, :] * o_ref[rows, :]
            d0 = jnp.sum(jnp.where(head0, prod, 0.0), axis=-1, keepdims=True)
            d1 = jnp.sum(jnp.where(head0, 0.0, prod), axis=-1, keepdims=True)
            dl_s[rows, :] = jnp.where(head0, d0, d1)
            return carry

        lax.fori_loop(0, NBLK, delta, 0, unroll=2)

        for p, d in enumerate(DILATIONS):
            nb = ATTN_CHUNK // (ATTN_BLOCK * d)

            def prep(t, carry, d=d, nb=nb):
                _, _, crows, prows = _attn_rows(t, d, nb)
                kp, kc = kk[prows, :], kk[crows, :]
                kt_s[t, :, pl.ds(0, LANES)] = kp.T.astype(BF16)
                kt_s[t, :, pl.ds(LANES, LANES)] = kc.T.astype(BF16)
                kn_s[t, pl.ds(0, ATTN_BLOCK), :] = kp.astype(BF16)
                kn_s[t, pl.ds(ATTN_BLOCK, ATTN_BLOCK), :] = kc.astype(BF16)
                vt_s[t, :, pl.ds(0, LANES)] = vv[prows, :].T.astype(BF16)
                vt_s[t, :, pl.ds(LANES, LANES)] = vv[crows, :].T.astype(BF16)
                return carry

            lax.fori_loop(0, NBLK, prep, 0, unroll=4)

            def blk(tg, carry, p=p, d=d, nb=nb):
                st = []
                for g in range(group):
                    t = tg * group + g
                    b, qrows, _, _ = _attn_rows(t, d, nb)
                    q = q_ref[qrows, :]
                    dout = do_ref[qrows, :]
                    lse_b = lse_ref[qrows, :]
                    dl_b = dl_s[qrows, :]
                    pen = jnp.where(jnp.logical_or(i > 0, b > 0), 0.0, NEG_INF)
                    for h in range(2):
                        hm = head0 if h == 0 else jnp.logical_not(head0)
                        c0 = h * HEAD_DIM
                        qh = jnp.where(hm, q, 0.0).astype(BF16)
                        doh = jnp.where(hm, dout, 0.0).astype(BF16)
                        st.append(dict(t=t, qrows=qrows, qh=qh, doh=doh, pen=pen, lse=lse_b[:, c0:c0 + 1],
                                       dl=dl_b[:, c0:c0 + 1], sc=_dot(qh, kt_s[t]), dp=_dot(doh, vt_s[t])))
                for e in st:
                    sc = e["sc"] * scale + bias_s[...]
                    sc = jnp.concatenate([sc[:, :LANES] + e["pen"], sc[:, LANES:]], axis=1)
                    pr = jnp.exp(sc - e["lse"])
                    e["ds"] = (pr * (e["dp"] - e["dl"]) * scale).astype(BF16)
                    e["pr"] = pr.astype(BF16)
                for g in range(group):
                    e0, e1 = st[2 * g], st[2 * g + 1]
                    t = e0["t"]
                    dq_s[p, e0["qrows"], :] = jnp.where(head0, _dot(e0["ds"], kn_s[t]), _dot(e1["ds"], kn_s[t]))
                    dkb[t] = _dot_tn(e0["ds"], e0["qh"]) + _dot_tn(e1["ds"], e1["qh"])
                    dvb[t] = _dot_tn(e0["pr"], e0["doh"]) + _dot_tn(e1["pr"], e1["doh"])
                return carry

            lax.fori_loop(0, NBLK // group, blk, 0)

            def scatter(t, carry, d=d, nb=nb):
                _, _, crows, prows = _attn_rows(t, d, nb)
                dkk[prows, :] = dkk[prows, :] + dkb[t, pl.ds(0, ATTN_BLOCK), :]
                dkk[crows, :] = dkk[crows, :] + dkb[t, pl.ds(ATTN_BLOCK, ATTN_BLOCK), :]
                dvv[prows, :] = dvv[prows, :] + dvb[t, pl.ds(0, ATTN_BLOCK), :]
                dvv[crows, :] = dvv[crows, :] + dvb[t, pl.ds(ATTN_BLOCK, ATTN_BLOCK), :]
                return carry

            lax.fori_loop(0, NBLK, scatter, 0)

        def finish(t, carry):
            rows = pl.ds(pl.multiple_of(t * ATTN_BLOCK, ATTN_BLOCK), ATTN_BLOCK)
            acc = dq_s[0, rows, :]
            for p in range(1, npat):
                acc = acc + dq_s[p, rows, :]
            dq_ref[rows, :] = acc
            return carry

        lax.fori_loop(0, NBLK, finish, 0, unroll=2)
        dk_ref[...] = dkk[pl.ds(ATTN_CHUNK, ATTN_CHUNK), :]
        dv_ref[...] = dvv[pl.ds(ATTN_CHUNK, ATTN_CHUNK), :]
        if ex is not None:
            @pl.when(jnp.logical_and(pl.program_id(0) == n_hp - 1, step == nch - 1))
            def _():
                ex.wait(*hx)

    cur = pl.BlockSpec((ATTN_CHUNK, LANES), lambda h, t: (nch - 1 - t, h))
    prev = pl.BlockSpec((ATTN_CHUNK, LANES), lambda h, t: (jnp.maximum(nch - 2 - t, 0), h))
    big = pltpu.VMEM((2 * ATTN_CHUNK, LANES), F32)
    pair_t = pltpu.VMEM((NBLK, LANES, 2 * LANES), BF16)
    hosted = ex is not None
    return pl.pallas_call(
        body, name="attn_bwd", grid=(n_hp, nch),
        in_specs=[cur, prev, cur, prev, cur, cur, cur, cur] + (ex.specs() if hosted else []),
        out_specs=[cur, cur, cur] + (ex.specs() if hosted else []),
        out_shape=[_sds((s, ATTN_W))] * 3 + (ex.out_shape() if hosted else []),
        scratch_shapes=[big, big, big, big, pair_t, pair_t, pltpu.VMEM((NBLK, 2 * ATTN_BLOCK, LANES), BF16),
                        pltpu.VMEM((ATTN_BLOCK, 2 * LANES), F32),
                        pltpu.VMEM((npat, ATTN_CHUNK, LANES), F32), pltpu.VMEM((ATTN_CHUNK, LANES), F32),
                        pltpu.VMEM((NBLK, 2 * ATTN_BLOCK, LANES), F32), pltpu.VMEM((NBLK, 2 * ATTN_BLOCK, LANES), F32)]
        + (ex.scratch() if hosted else []),
        compiler_params=_cp(("arbitrary", "arbitrary")),
    )(qn, kn, kn, v, v, o, lse, do, *(ex.srcs if hosted else []))


def _attn_fwd_v1(qn, kn, v):
    s = qn.shape[0]
    nch = s // ATTN_CHUNK
    scale = HEAD_DIM ** -0.5
    npat = len(DILATIONS)

    def body(q_ref, kp_ref, kc_ref, vp_ref, vc_ref, o_ref, lse_ref, kk, vv, m_s, l_s, acc_s):
        i = pl.program_id(1)
        kk[pl.ds(0, ATTN_CHUNK), :] = kp_ref[...]
        kk[pl.ds(ATTN_CHUNK, ATTN_CHUNK), :] = kc_ref[...]
        vv[pl.ds(0, ATTN_CHUNK), :] = vp_ref[...]
        vv[pl.ds(ATTN_CHUNK, ATTN_CHUNK), :] = vc_ref[...]
        row, col = _attn_masks()
        head0 = col < HEAD_DIM
        mask_cur = col <= row
        diff = col - row

        for p, d in enumerate(DILATIONS):
            nb = ATTN_CHUNK // (ATTN_BLOCK * d)

            def blk(t, carry, p=p, d=d, nb=nb):
                b, qrows, crows, prows = _attn_rows(t, d, nb)
                q = q_ref[qrows, :]
                kc = kk[crows, :].astype(BF16)
                kp = kk[prows, :].astype(BF16)
                vc = vv[crows, :].astype(BF16)
                vp = vv[prows, :].astype(BF16)
                thr = jnp.where(jnp.logical_or(i > 0, b > 0), 0, 4 * ATTN_BLOCK)
                mask_prev = diff >= thr
                accs, ms, ls = [], [], []
                for h in range(2):
                    hm = head0 if h == 0 else jnp.logical_not(head0)
                    qh = jnp.where(hm, q, 0.0).astype(BF16)
                    s_p = jnp.where(mask_prev, _dot_nt(qh, kp) * scale, NEG_INF)
                    s_c = jnp.where(mask_cur, _dot_nt(qh, kc) * scale, NEG_INF)
                    m = jnp.maximum(jnp.max(s_p, axis=-1, keepdims=True), jnp.max(s_c, axis=-1, keepdims=True))
                    e_p = jnp.exp(s_p - m)
                    e_c = jnp.exp(s_c - m)
                    l = jnp.sum(e_p, axis=-1, keepdims=True) + jnp.sum(e_c, axis=-1, keepdims=True)
                    accs.append(_dot(e_p.astype(BF16), vp) + _dot(e_c.astype(BF16), vc))
                    ms.append(jnp.broadcast_to(m, (ATTN_BLOCK, LANES)))
                    ls.append(jnp.broadcast_to(l, (ATTN_BLOCK, LANES)))
                m_s[p, qrows, :] = jnp.where(head0, ms[0], ms[1])
                l_s[p, qrows, :] = jnp.where(head0, ls[0], ls[1])
                acc_s[p, qrows, :] = jnp.where(head0, accs[0], accs[1])
                return carry

            lax.fori_loop(0, ATTN_CHUNK // ATTN_BLOCK, blk, 0)

        def merge(t, carry):
            rows = pl.ds(pl.multiple_of(t * ATTN_BLOCK, ATTN_BLOCK), ATTN_BLOCK)
            m_all = m_s[0, rows, :]
            for p in range(1, npat):
                m_all = jnp.maximum(m_all, m_s[p, rows, :])
            num = jnp.zeros((ATTN_BLOCK, LANES), F32)
            den = jnp.zeros((ATTN_BLOCK, LANES), F32)
            for p in range(npat):
                w = jnp.exp(m_s[p, rows, :] - m_all)
                num = num + acc_s[p, rows, :] * w
                den = den + l_s[p, rows, :] * w
            o_ref[rows, :] = num / den
            lse_ref[rows, :] = m_all + jnp.log(den)
            return carry

        lax.fori_loop(0, ATTN_CHUNK // ATTN_BLOCK, merge, 0)

    cur = pl.BlockSpec((ATTN_CHUNK, LANES), lambda h, i: (i, h))
    prev = pl.BlockSpec((ATTN_CHUNK, LANES), lambda h, i: (jnp.maximum(i - 1, 0), h))
    return pl.pallas_call(
        body, name="attn_fwd", grid=(ATTN_W // LANES, nch),
        in_specs=[cur, prev, cur, prev, cur],
        out_specs=[cur, cur],
        out_shape=[_sds((s, ATTN_W)), _sds((s, ATTN_W))],
        scratch_shapes=[pltpu.VMEM((2 * ATTN_CHUNK, LANES), F32), pltpu.VMEM((2 * ATTN_CHUNK, LANES), F32),
                        pltpu.VMEM((npat, ATTN_CHUNK, LANES), F32), pltpu.VMEM((npat, ATTN_CHUNK, LANES), F32),
                        pltpu.VMEM((npat, ATTN_CHUNK, LANES), F32)],
        compiler_params=_cp(("parallel", "parallel")),
    )(qn, kn, kn, v, v)


def _attn_bwd_v1(qn, kn, v, o, lse, do):
    s = qn.shape[0]
    nch = s // ATTN_CHUNK
    scale = HEAD_DIM ** -0.5
    npat = len(DILATIONS)

    def body(q_ref, kp_ref, kc_ref, vp_ref, vc_ref, o_ref, lse_ref, do_ref, dq_ref, dk_ref, dv_ref,
             kk, vv, dkk, dvv, dq_s, dl_s):
        step = pl.program_id(1)
        i = nch - 1 - step
        kk[pl.ds(0, ATTN_CHUNK), :] = kp_ref[...]
        kk[pl.ds(ATTN_CHUNK, ATTN_CHUNK), :] = kc_ref[...]
        vv[pl.ds(0, ATTN_CHUNK), :] = vp_ref[...]
        vv[pl.ds(ATTN_CHUNK, ATTN_CHUNK), :] = vc_ref[...]

        @pl.when(step == 0)
        def _():
            dkk[pl.ds(ATTN_CHUNK, ATTN_CHUNK), :] = jnp.zeros((ATTN_CHUNK, LANES), F32)
            dvv[pl.ds(ATTN_CHUNK, ATTN_CHUNK), :] = jnp.zeros((ATTN_CHUNK, LANES), F32)

        @pl.when(step > 0)
        def _():
            dkk[pl.ds(ATTN_CHUNK, ATTN_CHUNK), :] = dkk[pl.ds(0, ATTN_CHUNK), :]
            dvv[pl.ds(ATTN_CHUNK, ATTN_CHUNK), :] = dvv[pl.ds(0, ATTN_CHUNK), :]

        dkk[pl.ds(0, ATTN_CHUNK), :] = jnp.zeros((ATTN_CHUNK, LANES), F32)
        dvv[pl.ds(0, ATTN_CHUNK), :] = jnp.zeros((ATTN_CHUNK, LANES), F32)

        row, col = _attn_masks()
        head0 = col < HEAD_DIM
        mask_cur = col <= row
        diff = col - row

        def delta(t, carry):
            rows = pl.ds(pl.multiple_of(t * ATTN_BLOCK, ATTN_BLOCK), ATTN_BLOCK)
            prod = do_ref[rows, :] * o_ref[rows, :]
            d0 = jnp.sum(jnp.where(head0, prod, 0.0), axis=-1, keepdims=True)
            d1 = jnp.sum(jnp.where(head0, 0.0, prod), axis=-1, keepdims=True)
            dl_s[rows, :] = jnp.where(head0, d0, d1)
            return carry

        lax.fori_loop(0, ATTN_CHUNK // ATTN_BLOCK, delta, 0)

        for p, d in enumerate(DILATIONS):
            nb = ATTN_CHUNK // (ATTN_BLOCK * d)

            def blk(t, carry, p=p, d=d, nb=nb):
                b, qrows, crows, prows = _attn_rows(t, d, nb)
                q = q_ref[qrows, :]
                dout = do_ref[qrows, :]
                lse_b = lse_ref[qrows, :]
                dl_b = dl_s[qrows, :]
                kc = kk[crows, :].astype(BF16)
                kp = kk[prows, :].astype(BF16)
                vc = vv[crows, :].astype(BF16)
                vp = vv[prows, :].astype(BF16)
                thr = jnp.where(jnp.logical_or(i > 0, b > 0), 0, 4 * ATTN_BLOCK)
                mask_prev = diff >= thr
                dqs = []
                dk_p = jnp.zeros((ATTN_BLOCK, LANES), F32)
                dk_c = jnp.zeros((ATTN_BLOCK, LANES), F32)
                dv_p = jnp.zeros((ATTN_BLOCK, LANES), F32)
                dv_c = jnp.zeros((ATTN_BLOCK, LANES), F32)
                for h in range(2):
                    hm = head0 if h == 0 else jnp.logical_not(head0)
                    c0 = h * HEAD_DIM
                    qh = jnp.where(hm, q, 0.0).astype(BF16)
                    doh = jnp.where(hm, dout, 0.0).astype(BF16)
                    lse_h = lse_b[:, c0:c0 + 1]
                    dl_h = dl_b[:, c0:c0 + 1]
                    s_p = jnp.where(mask_prev, _dot_nt(qh, kp) * scale, NEG_INF)
                    s_c = jnp.where(mask_cur, _dot_nt(qh, kc) * scale, NEG_INF)
                    p_p = jnp.exp(s_p - lse_h)
                    p_c = jnp.exp(s_c - lse_h)
                    ds_p = (p_p * (_dot_nt(doh, vp) - dl_h) * scale).astype(BF16)
                    ds_c = (p_c * (_dot_nt(doh, vc) - dl_h) * scale).astype(BF16)
                    dqs.append(_dot(ds_p, kp) + _dot(ds_c, kc))
                    dk_p = dk_p + _dot_tn(ds_p, qh)
                    dk_c = dk_c + _dot_tn(ds_c, qh)
                    dv_p = dv_p + _dot_tn(p_p.astype(BF16), doh)
                    dv_c = dv_c + _dot_tn(p_c.astype(BF16), doh)
                dq_s[p, qrows, :] = jnp.where(head0, dqs[0], dqs[1])
                dkk[prows, :] = dkk[prows, :] + dk_p
                dkk[crows, :] = dkk[crows, :] + dk_c
                dvv[prows, :] = dvv[prows, :] + dv_p
                dvv[crows, :] = dvv[crows, :] + dv_c
                return carry

            lax.fori_loop(0, ATTN_CHUNK // ATTN_BLOCK, blk, 0)

        def finish(t, carry):
            rows = pl.ds(pl.multiple_of(t * ATTN_BLOCK, ATTN_BLOCK), ATTN_BLOCK)
            acc = dq_s[0, rows, :]
            for p in range(1, npat):
                acc = acc + dq_s[p, rows, :]
            dq_ref[rows, :] = acc
            return carry

        lax.fori_loop(0, ATTN_CHUNK // ATTN_BLOCK, finish, 0)
        dk_ref[...] = dkk[pl.ds(ATTN_CHUNK, ATTN_CHUNK), :]
        dv_ref[...] = dvv[pl.ds(ATTN_CHUNK, ATTN_CHUNK), :]

    cur = pl.BlockSpec((ATTN_CHUNK, LANES), lambda h, t: (nch - 1 - t, h))
    prev = pl.BlockSpec((ATTN_CHUNK, LANES), lambda h, t: (jnp.maximum(nch - 2 - t, 0), h))
    big = pltpu.VMEM((2 * ATTN_CHUNK, LANES), F32)
    return pl.pallas_call(
        body, name="attn_bwd", grid=(ATTN_W // LANES, nch),
        in_specs=[cur, prev, cur, prev, cur, cur, cur, cur],
        out_specs=[cur, cur, cur],
        out_shape=[_sds((s, ATTN_W))] * 3,
        scratch_shapes=[big, big, big, big, pltpu.VMEM((npat, ATTN_CHUNK, LANES), F32),
                        pltpu.VMEM((ATTN_CHUNK, LANES), F32)],
        compiler_params=_cp(("parallel", "arbitrary")),
    )(qn, kn, kn, v, v, o, lse, do)


def _discretize(lr, li, dt):
    mag = jnp.exp(lr * dt)
    abr = mag * jnp.cos(li * dt)
    abi = mag * jnp.sin(li * dt)
    den = lr * lr + li * li
    nr, ni = abr - 1.0, abi
    cr = (nr * lr + ni * li) / den
    ci = (ni * lr - nr * li) / den
    return abr, abi, den, nr, ni, cr, ci


def _ssm_discretize(a_re, a_im, log_dt, b_re_t, b_im_t):
    def body(ar_ref, ai_ref, ldt_ref, br_ref, bi_ref, abr_ref, abi_ref, bbr_ref, bbi_ref):
        abr, abi, _, _, _, cr, ci = _discretize(ar_ref[...], ai_ref[...], jnp.exp(ldt_ref[...]))
        br, bi = br_ref[...], bi_ref[...]
        abr_ref[...] = abr
        abi_ref[...] = abi
        bbr_ref[...] = cr * br - ci * bi
        bbi_ref[...] = cr * bi + ci * br

    return pl.pallas_call(
        body, name="ssm_discretize",
        out_shape=[_sds(a_re.shape)] * 2 + [_sds(b_re_t.shape)] * 2,
    )(a_re, a_im, log_dt, b_re_t, b_im_t)


def _ssm_discretize_bwd(a_re, a_im, log_dt, b_re_t, b_im_t, dabr, dabi, dbbr, dbbi):
    def body(ar_ref, ai_ref, ldt_ref, br_ref, bi_ref, dabr_ref, dabi_ref, dbbr_ref, dbbi_ref,
             dar_ref, dai_ref, dldt_ref, dbr_ref, dbi_ref):
        lr, li = ar_ref[...], ai_ref[...]
        dt = jnp.exp(ldt_ref[...])
        abr, abi, den, nr, ni, cr, ci = _discretize(lr, li, dt)
        br, bi = br_ref[...], bi_ref[...]
        gbr, gbi = dbbr_ref[...], dbbi_ref[...]
        dcr = jnp.sum(gbr * br + gbi * bi, axis=1, keepdims=True)
        dci = jnp.sum(gbi * br - gbr * bi, axis=1, keepdims=True)
        dbr_ref[...] = cr * gbr + ci * gbi
        dbi_ref[...] = cr * gbi - ci * gbr
        dnr = (dcr * lr - dci * li) / den
        dni = (dcr * li + dci * lr) / den
        dden = -(dcr * cr + dci * ci) / den
        dlr = (dcr * nr + dci * ni) / den + dden * 2.0 * lr
        dli = (dcr * ni - dci * nr) / den + dden * 2.0 * li
        gabr = dabr_ref[...] + dnr
        gabi = dabi_ref[...] + dni
        dphi = gabr * abr + gabi * abi
        dth = gabi * abr - gabr * abi
        dar_ref[...] = dlr + dphi * dt
        dai_ref[...] = dli + dth * dt
        dldt_ref[...] = jnp.sum(dphi * lr + dth * li, axis=2, keepdims=True) * dt

    return pl.pallas_call(
        body, name="ssm_discretize_bwd",
        out_shape=[_sds(a_re.shape)] * 2 + [_sds(log_dt.shape)] + [_sds(b_re_t.shape)] * 2,
    )(a_re, a_im, log_dt, b_re_t, b_im_t, dabr, dabi, dbbr, dbbi)


def _ssm_power(abr, abi, n_sq):
    def body(r_ref, i_ref, or_ref, oi_ref):
        r, i = r_ref[...], i_ref[...]
        for _ in range(n_sq):
            r, i = r * r - i * i, 2.0 * r * i
        or_ref[...] = r
        oi_ref[...] = i

    return pl.pallas_call(body, name="ssm_power", out_shape=[_sds(abr.shape)] * 2)(abr, abi)


N_CB = SSM_W // LANES
CB_STATES = N_STATE // N_CB
ROWS = N_SEG * SSM_LK


class _Neg:
    def __init__(self, ref):
        self.ref = ref

    def __getitem__(self, idx):
        return -self.ref[idx]


def _seg_init(fin_r, fin_i, pw_r, pw_i, x_r, x_i, reverse):
    zero = jnp.zeros((1, N_STATE), F32)
    cr, ci = zero, zero
    order = range(N_SEG - 1, -1, -1) if reverse else range(N_SEG)
    pr = pw_r[...]
    pi = -pw_i[...] if reverse else pw_i[...]
    for j in order:
        x_r[pl.ds(j, 1), :] = cr
        x_i[pl.ds(j, 1), :] = ci
        fr, fi = fin_r[pl.ds(j, 1), :], fin_i[pl.ds(j, 1), :]
        cr, ci = fr + pr * cr - pi * ci, fi + pr * ci + pi * cr


def _scan_rows(a_r, a_i, b_r, b_i, x_r, x_i, o_r, o_i, b_off, n_steps, reverse):
    w = 512
    for c in range(N_STATE // w):
        cols = pl.ds(c * w, w)
        ar = jnp.broadcast_to(a_r[:, cols], (N_SEG, w))
        ai = jnp.broadcast_to(a_i[:, cols], (N_SEG, w))

        def step(t, carry, cols=cols, ar=ar, ai=ai):
            xr, xi = carry
            k = (n_steps - 1 - t) if reverse else t
            rows = pl.ds(pl.multiple_of(k * N_SEG + b_off, N_SEG), N_SEG)
            nr = ar * xr - ai * xi + b_r[rows, cols]
            ni = ar * xi + ai * xr + b_i[rows, cols]
            o_r[rows, cols] = nr
            o_i[rows, cols] = ni
            return nr, ni

        xr, xi = lax.fori_loop(0, n_steps, step, (x_r[:, cols], x_i[:, cols]), unroll=4)
        x_r[:, cols] = xr
        x_i[:, cols] = xi


def _permute_in(src_ref, dst):
    for c in range(N_CB):
        dst[c] = src_ref[:, :, pl.ds(c * LANES, LANES)].reshape(ROWS, LANES)


def _permute_out(src, dst_ref):
    for c in range(N_CB):
        dst_ref[:, :, pl.ds(c * LANES, LANES)] = src[c].reshape(SSM_LK, N_SEG, LANES)


def _ssm_fwd_v1(u3, abr, abi, pw_r, pw_i, fin_r, fin_i, bb_r, bb_i, cc_r, cc_i, dskip, finals_only):
    sl = u3.shape[0]
    nch = sl // SSM_LK

    def body(u_ref, abr_ref, abi_ref, pwr_ref, pwi_ref, finr_ref, fini_ref, bbr_ref, bbi_ref,
             ccr_ref, cci_ref, d_ref, *rest):
        if finals_only:
            xfr_ref, xfi_ref, up, xs_r, xs_i, x_r, x_i = rest
        else:
            y_ref, xsr_ref, xsi_ref, up, yp, xs_r, xs_i, x_r, x_i = rest
        k = pl.program_id(0)

        @pl.when(k == 0)
        def _():
            _seg_init(finr_ref, fini_ref, pwr_ref, pwi_ref, x_r, x_i, False)

        if not finals_only:
            xsr_ref[0] = x_r[...]
            xsi_ref[0] = x_i[...]
        _permute_in(u_ref, up)
        for c in range(N_CB):
            lhs = up[c].astype(BF16)
            xs_r[:, pl.ds(c * CB_STATES, CB_STATES)] = _dot(lhs, bbr_ref[c])
            xs_i[:, pl.ds(c * CB_STATES, CB_STATES)] = _dot(lhs, bbi_ref[c])
        _scan_rows(abr_ref, abi_ref, xs_r, xs_i, x_r, x_i, xs_r, xs_i, 0, SSM_LK, False)
        if finals_only:
            @pl.when(k == nch - 1)
            def _():
                xfr_ref[...] = x_r[...]
                xfi_ref[...] = x_i[...]
        else:
            for c in range(N_CB):
                cols = pl.ds(c * CB_STATES, CB_STATES)
                yp[c] = (_dot(xs_r[:, cols].astype(BF16), ccr_ref[c]) - _dot(xs_i[:, cols].astype(BF16), cci_ref[c])
                         + d_ref[:, pl.ds(c * LANES, LANES)] * up[c])
            _permute_out(yp, y_ref)

    ublk = pl.BlockSpec((SSM_LK, N_SEG, SSM_W), lambda k: (k, 0, 0))
    st = pl.BlockSpec((1, N_SEG, N_STATE), lambda k: (k, 0, 0))
    vec = _full((1, N_STATE))
    mat = _full((N_SEG, N_STATE))
    chunk = pltpu.VMEM((N_CB, ROWS, LANES), F32)
    big = pltpu.VMEM((ROWS, N_STATE), F32)
    small = pltpu.VMEM((N_SEG, N_STATE), F32)
    if finals_only:
        out_specs, out_shape = [mat, mat], [_sds((N_SEG, N_STATE))] * 2
        scratch, name = [chunk, big, big, small, small], "ssm_fwd_finals"
    else:
        out_specs = [ublk, st, st]
        out_shape = [_sds(u3.shape)] + [_sds((nch, N_SEG, N_STATE))] * 2
        scratch, name = [chunk, chunk, big, big, small, small], "ssm_fwd"
    return pl.pallas_call(
        body, name=name, grid=(nch,),
        in_specs=[ublk, vec, vec, vec, vec, mat, mat,
                  _full((N_CB, LANES, CB_STATES)), _full((N_CB, LANES, CB_STATES)),
                  _full((N_CB, CB_STATES, LANES)), _full((N_CB, CB_STATES, LANES)), _full((1, SSM_W))],
        out_specs=out_specs, out_shape=out_shape, scratch_shapes=scratch,
        compiler_params=_cp(("arbitrary",)),
    )(u3, abr, abi, pw_r, pw_i, fin_r, fin_i, bb_r, bb_i, cc_r, cc_i, dskip)


def _ssm_bwd_v1(u3, dy3, xst_r, xst_i, abr, abi, pw_r, pw_i, fin_r, fin_i, bb_r, bb_i, bbt_r, bbt_i,
                cct_r, cct_i, dskip, finals_only):
    sl = u3.shape[0]
    nch = sl // SSM_LK

    def body(u_ref, g_ref, xsr_ref, xsi_ref, abr_ref, abi_ref, pwr_ref, pwi_ref,
             finr_ref, fini_ref, bbr_ref, bbi_ref, btr_ref, bti_ref, ctr_ref, cti_ref, d_ref, *rest):
        if finals_only:
            lfr_ref, lfi_ref, gp, l_r, l_i, lam_r, lam_i = rest
        else:
            (du_ref, dar_ref, dai_ref, dbr_ref, dbi_ref, dcr_ref, dci_ref, dd_ref,
             gp, up, yp, l_r, l_i, lam_r, lam_i, x_r, x_i, xx_r, xx_i, sar, sai, sdd) = rest
        t = pl.program_id(0)

        @pl.when(t == 0)
        def _():
            _seg_init(finr_ref, fini_ref, pwr_ref, pwi_ref, lam_r, lam_i, True)
            if not finals_only:
                sar[...] = jnp.zeros_like(sar)
                sai[...] = jnp.zeros_like(sai)
                sdd[...] = jnp.zeros_like(sdd)
                dbr_ref[...] = jnp.zeros_like(dbr_ref)
                dbi_ref[...] = jnp.zeros_like(dbi_ref)
                dcr_ref[...] = jnp.zeros_like(dcr_ref)
                dci_ref[...] = jnp.zeros_like(dci_ref)

        _permute_in(g_ref, gp)
        for c in range(N_CB):
            cols = pl.ds(c * CB_STATES, CB_STATES)
            lhs = gp[c].astype(BF16)
            l_r[:, cols] = _dot(lhs, ctr_ref[c])
            l_i[:, cols] = -_dot(lhs, cti_ref[c])
        if not finals_only:
            _permute_in(u_ref, up)
            x_r[...] = xsr_ref[0]
            x_i[...] = xsi_ref[0]
            xx_r[pl.ds(0, N_SEG), :] = x_r[...]
            xx_i[pl.ds(0, N_SEG), :] = x_i[...]
            for c in range(N_CB):
                lhs = up[c].astype(BF16)
                xx_r[pl.ds(N_SEG, ROWS), pl.ds(c * CB_STATES, CB_STATES)] = _dot(lhs, bbr_ref[c])
                xx_i[pl.ds(N_SEG, ROWS), pl.ds(c * CB_STATES, CB_STATES)] = _dot(lhs, bbi_ref[c])
            _scan_rows(abr_ref, abi_ref, xx_r, xx_i, x_r, x_i, xx_r, xx_i, N_SEG, SSM_LK, False)
        _scan_rows(abr_ref, _Neg(abi_ref), l_r, l_i, lam_r, lam_i, l_r, l_i, 0, SSM_LK, True)
        if finals_only:
            @pl.when(t == nch - 1)
            def _():
                lfr_ref[...] = lam_r[...]
                lfi_ref[...] = lam_i[...]
        else:
            w = 512
            for cc in range(N_STATE // w):
                cols = pl.ds(cc * w, w)

                def acc_step(kk_, carry, cols=cols):
                    sr, si = carry
                    rows = pl.ds(pl.multiple_of(kk_ * N_SEG, N_SEG), N_SEG)
                    lr, li = l_r[rows, cols], l_i[rows, cols]
                    pr, pi = xx_r[rows, cols], xx_i[rows, cols]
                    return sr + lr * pr + li * pi, si + li * pr - lr * pi

                z = jnp.zeros((N_SEG, w), F32)
                sr, si = lax.fori_loop(0, SSM_LK, acc_step, (z, z), unroll=4)
                sar[:, cols] += sr
                sai[:, cols] += si
            for c in range(N_CB):
                cols = pl.ds(c * CB_STATES, CB_STATES)
                lrb = l_r[:, cols].astype(BF16)
                lib = l_i[:, cols].astype(BF16)
                ub = up[c].astype(BF16)
                gb = gp[c].astype(BF16)
                dbr_ref[c] += _dot_tn(lrb, ub)
                dbi_ref[c] += _dot_tn(lib, ub)
                dcr_ref[c] += _dot_tn(gb, xx_r[pl.ds(N_SEG, ROWS), cols].astype(BF16))
                dci_ref[c] += -_dot_tn(gb, xx_i[pl.ds(N_SEG, ROWS), cols].astype(BF16))
                yp[c] = _dot(lrb, btr_ref[c]) + _dot(lib, bti_ref[c]) + d_ref[:, pl.ds(c * LANES, LANES)] * gp[c]
                prod = gp[c] * up[c]
                sdd[:, pl.ds(c * LANES, LANES)] += jnp.sum(prod.reshape(SSM_LK, N_SEG, LANES), axis=0)
            _permute_out(yp, du_ref)

            @pl.when(t == nch - 1)
            def _():
                dar_ref[...] = jnp.sum(sar[...], axis=0, keepdims=True)
                dai_ref[...] = jnp.sum(sai[...], axis=0, keepdims=True)
                dd_ref[...] = jnp.sum(sdd[...], axis=0, keepdims=True)

    ublk = pl.BlockSpec((SSM_LK, N_SEG, SSM_W), lambda t: (nch - 1 - t, 0, 0))
    st = pl.BlockSpec((1, N_SEG, N_STATE), lambda t: (nch - 1 - t, 0, 0))
    vec = _full((1, N_STATE))
    mat = _full((N_SEG, N_STATE))
    cs = _full((N_CB, LANES, CB_STATES))
    sc = _full((N_CB, CB_STATES, LANES))
    in_specs = [ublk, ublk, st, st, vec, vec, vec, vec, mat, mat, cs, cs, sc, sc, cs, cs, _full((1, SSM_W))]
    chunk = pltpu.VMEM((N_CB, ROWS, LANES), F32)
    big = pltpu.VMEM((ROWS, N_STATE), F32)
    small = pltpu.VMEM((N_SEG, N_STATE), F32)
    if finals_only:
        out_specs, out_shape = [mat, mat], [_sds((N_SEG, N_STATE))] * 2
        scratch, name = [chunk, big, big, small, small], "ssm_bwd_finals"
    else:
        out_specs = [ublk, vec, vec, sc, sc, cs, cs, _full((1, SSM_W))]
        out_shape = ([_sds(u3.shape), _sds((1, N_STATE)), _sds((1, N_STATE))]
                     + [_sds((N_CB, CB_STATES, LANES))] * 2 + [_sds((N_CB, LANES, CB_STATES))] * 2
                     + [_sds((1, SSM_W))])
        xx = pltpu.VMEM((ROWS + N_SEG, N_STATE), F32)
        scratch = [chunk, chunk, chunk, big, big, small, small, small, small, xx, xx, small, small,
                   pltpu.VMEM((N_SEG, SSM_W), F32)]
        name = "ssm_bwd"
    return pl.pallas_call(
        body, name=name, grid=(nch,), in_specs=in_specs, out_specs=out_specs, out_shape=out_shape,
        scratch_shapes=scratch, compiler_params=_cp(("arbitrary",)),
    )(u3, dy3, xst_r, xst_i, abr, abi, pw_r, pw_i, fin_r, fin_i, bb_r, bb_i, bbt_r, bbt_i, cct_r, cct_i, dskip)


def _scan_block(a_r, a_i, c, b_r, b_i, b_off, x_r, x_i, reverse, acc=None):
    cols = pl.ds(c * CB_STATES, CB_STATES)
    ar = jnp.broadcast_to(a_r[:, cols], (N_SEG, CB_STATES))
    ai = jnp.broadcast_to(a_i[:, cols], (N_SEG, CB_STATES))
    xr, xi = x_r[:, cols], x_i[:, cols]
    if acc is not None:
        sr = jnp.zeros((N_SEG, CB_STATES), F32)
        si = jnp.zeros((N_SEG, CB_STATES), F32)
    for t in range(SSM_LK):
        k = (SSM_LK - 1 - t) if reverse else t
        rows = pl.ds(k * N_SEG + b_off, N_SEG)
        xr, xi = ar * xr - ai * xi + b_r[rows, :], ar * xi + ai * xr + b_i[rows, :]
        b_r[rows, :] = xr
        b_i[rows, :] = xi
        if acc is not None:
            pr, pi = acc[0][pl.ds(k * N_SEG, N_SEG), :], acc[1][pl.ds(k * N_SEG, N_SEG), :]
            sr = sr + xr * pr + xi * pi
            si = si + xi * pr - xr * pi
    x_r[:, cols] = xr
    x_i[:, cols] = xi
    if acc is not None:
        acc[2][:, cols] += sr
        acc[3][:, cols] += si


def _ssm_fwd(u3, abr, abi, pw_r, pw_i, fin_r, fin_i, bb_r, bb_i, cc_r, cc_i, dskip, finals_only):
    sl = u3.shape[0]
    nch = sl // SSM_LK

    def body(u_ref, abr_ref, abi_ref, pwr_ref, pwi_ref, finr_ref, fini_ref, bbr_ref, bbi_ref,
             ccr_ref, cci_ref, d_ref, *rest):
        if finals_only:
            xfr_ref, xfi_ref, up, x_r, x_i = rest[:5]
        else:
            y_ref, xsr_ref, xsi_ref, up, yp, x_r, x_i = rest[:7]
        xs_r, xs_i = rest[-2 * N_CB:-N_CB], rest[-N_CB:]
        k = pl.program_id(0)

        @pl.when(k == 0)
        def _():
            _seg_init(finr_ref, fini_ref, pwr_ref, pwi_ref, x_r, x_i, False)

        if not finals_only:
            xsr_ref[0] = x_r[...]
            xsi_ref[0] = x_i[...]
        _permute_in(u_ref, up)

        def drive(c):
            lhs = up[c].astype(BF16)
            xs_r[c][...] = _dot(lhs, bbr_ref[c])
            xs_i[c][...] = _dot(lhs, bbi_ref[c])

        def readout(c):
            yp[c] = (_dot(xs_r[c][...].astype(BF16), ccr_ref[c]) - _dot(xs_i[c][...].astype(BF16), cci_ref[c])
                     + d_ref[:, pl.ds(c * LANES, LANES)] * up[c])

        drive(0)
        for c in range(N_CB):
            if c + 1 < N_CB:
                drive(c + 1)
            if c >= 1 and not finals_only:
                readout(c - 1)
            _scan_block(abr_ref, abi_ref, c, xs_r[c], xs_i[c], 0, x_r, x_i, False)
        if finals_only:
            @pl.when(k == nch - 1)
            def _():
                xfr_ref[...] = x_r[...]
                xfi_ref[...] = x_i[...]
        else:
            readout(N_CB - 1)
            _permute_out(yp, y_ref)

    ublk = pl.BlockSpec((SSM_LK, N_SEG, SSM_W), lambda k: (k, 0, 0))
    st = pl.BlockSpec((1, N_SEG, N_STATE), lambda k: (k, 0, 0))
    vec = _full((1, N_STATE))
    mat = _full((N_SEG, N_STATE))
    chunk = pltpu.VMEM((N_CB, ROWS, LANES), F32)
    blocks = [pltpu.VMEM((ROWS, CB_STATES), F32)] * (2 * N_CB)
    small = pltpu.VMEM((N_SEG, N_STATE), F32)
    if finals_only:
        out_specs, out_shape = [mat, mat], [_sds((N_SEG, N_STATE))] * 2
        scratch, name = [chunk, small, small] + blocks, "ssm_fwd_finals"
    else:
        out_specs = [ublk, st, st]
        out_shape = [_sds(u3.shape)] + [_sds((nch, N_SEG, N_STATE))] * 2
        scratch, name = [chunk, chunk, small, small] + blocks, "ssm_fwd"
    return pl.pallas_call(
        body, name=name, grid=(nch,),
        in_specs=[ublk, vec, vec, vec, vec, mat, mat,
                  _full((N_CB, LANES, CB_STATES)), _full((N_CB, LANES, CB_STATES)),
                  _full((N_CB, CB_STATES, LANES)), _full((N_CB, CB_STATES, LANES)), _full((1, SSM_W))],
        out_specs=out_specs, out_shape=out_shape, scratch_shapes=scratch,
        compiler_params=_cp(("arbitrary",)),
    )(u3, abr, abi, pw_r, pw_i, fin_r, fin_i, bb_r, bb_i, cc_r, cc_i, dskip)


def _ssm_bwd(u3, dy3, xst_r, xst_i, abr, abi, pw_r, pw_i, fin_r, fin_i, bb_r, bb_i, bbt_r, bbt_i,
             cct_r, cct_i, dskip, finals_only):
    sl = u3.shape[0]
    nch = sl // SSM_LK

    def body(u_ref, g_ref, xsr_ref, xsi_ref, abr_ref, abi_ref, pwr_ref, pwi_ref,
             finr_ref, fini_ref, bbr_ref, bbi_ref, btr_ref, bti_ref, ctr_ref, cti_ref, d_ref, *rest):
        if finals_only:
            lfr_ref, lfi_ref, gp, lam_r, lam_i = rest[:5]
            l_r, l_i = rest[-2 * N_CB:-N_CB], rest[-N_CB:]
        else:
            (du_ref, dar_ref, dai_ref, dbr_ref, dbi_ref, dcr_ref, dci_ref, dd_ref,
             gp, up, yp, lam_r, lam_i, x_r, x_i, sar, sai, sdd) = rest[:18]
            l_r, l_i = rest[18:18 + N_CB], rest[18 + N_CB:18 + 2 * N_CB]
            xx_r, xx_i = rest[18 + 2 * N_CB:18 + 3 * N_CB], rest[18 + 3 * N_CB:]
        t = pl.program_id(0)

        @pl.when(t == 0)
        def _():
            _seg_init(finr_ref, fini_ref, pwr_ref, pwi_ref, lam_r, lam_i, True)
            if not finals_only:
                sar[...] = jnp.zeros_like(sar)
                sai[...] = jnp.zeros_like(sai)
                sdd[...] = jnp.zeros_like(sdd)
                dbr_ref[...] = jnp.zeros_like(dbr_ref)
                dbi_ref[...] = jnp.zeros_like(dbi_ref)
                dcr_ref[...] = jnp.zeros_like(dcr_ref)
                dci_ref[...] = jnp.zeros_like(dci_ref)

        _permute_in(g_ref, gp)
        if not finals_only:
            _permute_in(u_ref, up)
            x_r[...] = xsr_ref[0]
            x_i[...] = xsi_ref[0]

        def drive(c):
            lhs = gp[c].astype(BF16)
            l_r[c][...] = _dot(lhs, ctr_ref[c])
            l_i[c][...] = -_dot(lhs, cti_ref[c])
            if not finals_only:
                cols = pl.ds(c * CB_STATES, CB_STATES)
                xx_r[c][pl.ds(0, N_SEG), :] = x_r[:, cols]
                xx_i[c][pl.ds(0, N_SEG), :] = x_i[:, cols]
                ub = up[c].astype(BF16)
                xx_r[c][pl.ds(N_SEG, ROWS), :] = _dot(ub, bbr_ref[c])
                xx_i[c][pl.ds(N_SEG, ROWS), :] = _dot(ub, bbi_ref[c])

        def collect(c):
            lrb = l_r[c][...].astype(BF16)
            lib = l_i[c][...].astype(BF16)
            ub = up[c].astype(BF16)
            gb = gp[c].astype(BF16)
            dbr_ref[c] += _dot_tn(lrb, ub)
            dbi_ref[c] += _dot_tn(lib, ub)
            dcr_ref[c] += _dot_tn(gb, xx_r[c][pl.ds(N_SEG, ROWS), :].astype(BF16))
            dci_ref[c] += -_dot_tn(gb, xx_i[c][pl.ds(N_SEG, ROWS), :].astype(BF16))
            yp[c] = _dot(lrb, btr_ref[c]) + _dot(lib, bti_ref[c]) + d_ref[:, pl.ds(c * LANES, LANES)] * gp[c]
            prod = gp[c] * up[c]
            sdd[:, pl.ds(c * LANES, LANES)] += jnp.sum(prod.reshape(SSM_LK, N_SEG, LANES), axis=0)

        drive(0)
        for c in range(N_CB):
            if c + 1 < N_CB:
                drive(c + 1)
            if finals_only:
                _scan_block(abr_ref, _Neg(abi_ref), c, l_r[c], l_i[c], 0, lam_r, lam_i, True)
            else:
                if c >= 1:
                    collect(c - 1)
                _scan_block(abr_ref, abi_ref, c, xx_r[c], xx_i[c], N_SEG, x_r, x_i, False)
                _scan_block(abr_ref, _Neg(abi_ref), c, l_r[c], l_i[c], 0, lam_r, lam_i, True,
                            acc=(xx_r[c], xx_i[c], sar, sai))
        if finals_only:
            @pl.when(t == nch - 1)
            def _():
                lfr_ref[...] = lam_r[...]
                lfi_ref[...] = lam_i[...]
        else:
            collect(N_CB - 1)
            _permute_out(yp, du_ref)

            @pl.when(t == nch - 1)
            def _():
                dar_ref[...] = jnp.sum(sar[...], axis=0, keepdims=True)
                dai_ref[...] = jnp.sum(sai[...], axis=0, keepdims=True)
                dd_ref[...] = jnp.sum(sdd[...], axis=0, keepdims=True)

    ublk = pl.BlockSpec((SSM_LK, N_SEG, SSM_W), lambda t: (nch - 1 - t, 0, 0))
    st = pl.BlockSpec((1, N_SEG, N_STATE), lambda t: (nch - 1 - t, 0, 0))
    vec = _full((1, N_STATE))
    mat = _full((N_SEG, N_STATE))
    cs = _full((N_CB, LANES, CB_STATES))
    sc = _full((N_CB, CB_STATES, LANES))
    in_specs = [ublk, ublk, st, st, vec, vec, vec, vec, mat, mat, cs, cs, sc, sc, cs, cs, _full((1, SSM_W))]
    chunk = pltpu.VMEM((N_CB, ROWS, LANES), F32)
    blocks = [pltpu.VMEM((ROWS, CB_STATES), F32)] * (2 * N_CB)
    small = pltpu.VMEM((N_SEG, N_STATE), F32)
    if finals_only:
        out_specs, out_shape = [mat, mat], [_sds((N_SEG, N_STATE))] * 2
        scratch, name = [chunk, small, small] + blocks, "ssm_bwd_finals"
    else:
        out_specs = [ublk, vec, vec, sc, sc, cs, cs, _full((1, SSM_W))]
        out_shape = ([_sds(u3.shape), _sds((1, N_STATE)), _sds((1, N_STATE))]
                     + [_sds((N_CB, CB_STATES, LANES))] * 2 + [_sds((N_CB, LANES, CB_STATES))] * 2
                     + [_sds((1, SSM_W))])
        scratch = ([chunk, chunk, chunk, small, small, small, small, small, small, pltpu.VMEM((N_SEG, SSM_W), F32)]
                   + blocks + [pltpu.VMEM((ROWS + N_SEG, CB_STATES), F32)] * (2 * N_CB))
        name = "ssm_bwd"
    return pl.pallas_call(
        body, name=name, grid=(nch,), in_specs=in_specs, out_specs=out_specs, out_shape=out_shape,
        scratch_shapes=scratch, compiler_params=_cp(("arbitrary",)),
    )(u3, dy3, xst_r, xst_i, abr, abi, pw_r, pw_i, fin_r, fin_i, bb_r, bb_i, bbt_r, bbt_i, cct_r, cct_i, dskip)


def _row(tm, w):
    return pl.BlockSpec((tm, w), lambda i: (i, 0))


def _acc_rows(ref, rows, first):
    @pl.when(first)
    def _():
        ref[...] = jnp.zeros_like(ref)

    ref[...] += jnp.sum(rows, axis=0, keepdims=True)


def _fwd_mix(attn, y, x, glu_w, glu_b, ga, gs, w_out, g2, tm=512):
    s = x.shape[0]

    def body(a_ref, y_ref, x_ref, gw_ref, gb_ref, ga_ref, gs_ref, wo_ref, g2_ref, mix_ref, x2_ref, h_ref):
        a = a_ref[...]
        anb = ((a * _rms(a)) * ga_ref[...]).astype(BF16)
        z, _ = _gelu(y_ref[...])
        so = z * jax.nn.sigmoid(_dot(z.astype(BF16), gw_ref[...]) + gb_ref[...])
        snb = ((so * _rms(so)) * gs_ref[...]).astype(BF16)
        mix_ref[:, pl.ds(0, ATTN_W)] = anb
        mix_ref[:, pl.ds(ATTN_W, SSM_W)] = snb
        x2 = x_ref[...] + (_dot(anb, wo_ref[pl.ds(0, ATTN_W), :]) + _dot(snb, wo_ref[pl.ds(ATTN_W, SSM_W), :]))
        x2_ref[...] = x2
        h_ref[...] = ((x2 * _rms(x2)) * g2_ref[...]).astype(BF16)

    return pl.pallas_call(
        body, name="fwd_mix", grid=(s // tm,),
        in_specs=[_row(tm, ATTN_W), _row(tm, SSM_W), _row(tm, D_MODEL), _full((SSM_W, SSM_W)), _full((1, SSM_W)),
                  _full((1, ATTN_W)), _full((1, SSM_W)), _full((D_MODEL, D_MODEL)), _full((1, D_MODEL))],
        out_specs=[_row(tm, D_MODEL), _row(tm, D_MODEL), _row(tm, D_MODEL)],
        out_shape=[_sds((s, D_MODEL), BF16), _sds((s, D_MODEL)), _sds((s, D_MODEL), BF16)],
        compiler_params=_cp(("parallel",)),
    )(attn, y, x, glu_w, glu_b, ga, gs, w_out, g2)


def _mlp_up(h, w_up, tm=512, bn=1024):
    s = h.shape[0]

    def body(h_ref, w_ref, r_ref, hdn_ref):
        hv = h_ref[...]
        for j in range(D_FF // bn):
            cols = pl.ds(j * bn, bn)
            r = jnp.maximum(_dot(hv, w_ref[:, cols]), 0.0)
            r_ref[:, cols] = r.astype(BF16)
            hdn_ref[:, cols] = (r * r).astype(BF16)

    return pl.pallas_call(
        body, name="mlp_up", grid=(s // tm,),
        in_specs=[_row(tm, D_MODEL), _full((D_MODEL, D_FF))],
        out_specs=[_row(tm, D_FF), _row(tm, D_FF)], out_shape=[_sds((s, D_FF), BF16)] * 2,
        compiler_params=_cp(("parallel",)),
    )(h, w_up)


def _mlp_down_loss(hdn, w_down, x2, tgt, tm=512):
    s = x2.shape[0]

    def body(hdn_ref, w_ref, x2_ref, t_ref, dy_ref, dyb_ref, sse_ref):
        err = (x2_ref[...] + _dot(hdn_ref[...], w_ref[...])) - t_ref[...]
        dy = err * (1.0 / D_MODEL)
        dy_ref[...] = dy
        dyb_ref[...] = dy.astype(BF16)

        @pl.when(pl.program_id(0) == 0)
        def _():
            sse_ref[...] = jnp.zeros_like(sse_ref)

        sse_ref[...] += jnp.sum(jnp.sum(err * err, axis=0, keepdims=True), axis=1, keepdims=True)

    return pl.pallas_call(
        body, name="mlp_down_loss", grid=(s // tm,),
        in_specs=[_row(tm, D_FF), _full((D_FF, D_MODEL)), _row(tm, D_MODEL), _row(tm, D_MODEL)],
        out_specs=[_row(tm, D_MODEL), _row(tm, D_MODEL), _full((1, 1))],
        out_shape=[_sds((s, D_MODEL)), _sds((s, D_MODEL), BF16), _sds((1, 1))],
        compiler_params=_cp(("arbitrary",)),
    )(hdn, w_down, x2, tgt)


def _mlp_down_bwd(dyb, w_down_t, r, tm=512, bn=1024):
    s = dyb.shape[0]

    def body(dy_ref, w_ref, r_ref, dup_ref):
        dyv = dy_ref[...]
        for j in range(D_FF // bn):
            cols = pl.ds(j * bn, bn)
            dup_ref[:, cols] = (_dot(dyv, w_ref[:, cols]) * (2.0 * r_ref[:, cols].astype(F32))).astype(BF16)

    return pl.pallas_call(
        body, name="mlp_down_bwd", grid=(s // tm,),
        in_specs=[_row(tm, D_MODEL), _full((D_MODEL, D_FF)), _row(tm, D_FF)],
        out_specs=_row(tm, D_FF), out_shape=_sds((s, D_FF), BF16),
        compiler_params=_cp(("parallel",)),
    )(dyb, w_down_t, r)


def _mlp_up_bwd(dup, w_up_t, x2, g2, dy, tm=512):
    s = x2.shape[0]

    def body(dup_ref, w_ref, x2_ref, g2_ref, dy_ref, dx2_ref, dx2b_ref, dg_ref):
        dx, dg_rows = _rms_bwd(_dot(dup_ref[...], w_ref[...]), x2_ref[...], g2_ref[...], D_MODEL)
        dx2 = dy_ref[...] + dx
        dx2_ref[...] = dx2
        dx2b_ref[...] = dx2.astype(BF16)
        _acc_rows(dg_ref, dg_rows, pl.program_id(0) == 0)

    return pl.pallas_call(
        body, name="mlp_up_bwd", grid=(s // tm,),
        in_specs=[_row(tm, D_FF), _full((D_FF, D_MODEL)), _row(tm, D_MODEL), _full((1, D_MODEL)), _row(tm, D_MODEL)],
        out_specs=[_row(tm, D_MODEL), _row(tm, D_MODEL), _full((1, D_MODEL))],
        out_shape=[_sds((s, D_MODEL)), _sds((s, D_MODEL), BF16), _sds((1, D_MODEL))],
        compiler_params=_cp(("arbitrary",)),
    )(dup, w_up_t, x2, g2, dy)


def _mix_bwd(dx2b, w_out_t, attn, y, glu_w, glu_b, glu_w_t, ga, gs, tm=512):
    s = attn.shape[0]

    def body(dx2_ref, wot_ref, a_ref, y_ref, gw_ref, gb_ref, gwt_ref, ga_ref, gs_ref,
             da_ref, dys_ref, z_ref, dpre_ref, dga_ref, dgs_ref, dgb_ref):
        first = pl.program_id(0) == 0
        dmix = _dot(dx2_ref[...], wot_ref[...])
        da, dga_rows = _rms_bwd(dmix[:, :ATTN_W], a_ref[...], ga_ref[...], ATTN_W)
        da_ref[...] = da
        yv = y_ref[...]
        z, t = _gelu(yv)
        gate = jax.nn.sigmoid(_dot(z.astype(BF16), gw_ref[...]) + gb_ref[...])
        dso, dgs_rows = _rms_bwd(dmix[:, ATTN_W:], z * gate, gs_ref[...], SSM_W)
        dpre = dso * z * gate * (1.0 - gate)
        dpre_b = dpre.astype(BF16)
        dz = dso * gate + _dot(dpre_b, gwt_ref[...])
        dgelu = 0.5 * (1.0 + t) + 0.5 * yv * (1.0 - t * t) * (GELU_C * (1.0 + 3.0 * 0.044715 * (yv * yv)))
        dys_ref[...] = dz * dgelu
        z_ref[...] = z.astype(BF16)
        dpre_ref[...] = dpre_b
        _acc_rows(dga_ref, dga_rows, first)
        _acc_rows(dgs_ref, dgs_rows, first)
        _acc_rows(dgb_ref, dpre, first)

    vec = _full((1, SSM_W))
    return pl.pallas_call(
        body, name="mix_bwd", grid=(s // tm,),
        in_specs=[_row(tm, D_MODEL), _full((D_MODEL, D_MODEL)), _row(tm, ATTN_W), _row(tm, SSM_W),
                  _full((SSM_W, SSM_W)), vec, _full((SSM_W, SSM_W)), vec, vec],
        out_specs=[_row(tm, ATTN_W), _row(tm, SSM_W), _row(tm, SSM_W), _row(tm, SSM_W), vec, vec, vec],
        out_shape=[_sds((s, ATTN_W)), _sds((s, SSM_W)), _sds((s, SSM_W), BF16), _sds((s, SSM_W), BF16),
                   _sds((1, ATTN_W)), _sds((1, SSM_W)), _sds((1, SSM_W))],
        compiler_params=_cp(("arbitrary",)),
    )(dx2b, w_out_t, attn, y, glu_w, glu_b, glu_w_t, ga, gs)


def _qk_bwd(dqn, dkn, qk, dv, du, gq, gk, gmat, tm=512):
    s = qk.shape[0]

    def body(dq_ref, dk_ref, qk_ref, dv_ref, du_ref, gq_ref, gk_ref, gm_ref, dp_ref, dgq_ref, dgk_ref):
        first = pl.program_id(0) == 0
        gm = gm_ref[...]
        for idx, (d_ref, g_ref, dg_ref) in enumerate(((dq_ref, gq_ref, dgq_ref), (dk_ref, gk_ref, dgk_ref))):
            xv = qk_ref[:, pl.ds(idx * ATTN_W, ATTN_W)]
            dyv = d_ref[...]
            r = lax.rsqrt(_group_mean(xv * xv, gm) + EPS)
            gdy = dyv * g_ref[...]
            dx = r * gdy - xv * (r * r * r) * _group_mean(gdy * xv, gm)
            dp_ref[:, pl.ds(idx * ATTN_W, ATTN_W)] = dx.astype(BF16)
            _acc_rows(dg_ref, dyv * (xv * r), first)
        dp_ref[:, pl.ds(2 * ATTN_W, ATTN_W)] = dv_ref[...].astype(BF16)
        dp_ref[:, pl.ds(3 * ATTN_W, SSM_W)] = du_ref[...].astype(BF16)

    vec = _full((1, ATTN_W))
    return pl.pallas_call(
        body, name="qk_bwd", grid=(s // tm,),
        in_specs=[_row(tm, ATTN_W), _row(tm, ATTN_W), _row(tm, 2 * ATTN_W), _row(tm, ATTN_W), _row(tm, SSM_W),
                  vec, vec, _full((ATTN_W, ATTN_W))],
        out_specs=[_row(tm, 4 * ATTN_W), vec, vec],
        out_shape=[_sds((s, 4 * ATTN_W), BF16), _sds((1, ATTN_W)), _sds((1, ATTN_W))],
        compiler_params=_cp(("arbitrary",)),
    )(dqn, dkn, qk, dv, du, gq, gk, gmat)


def _in_bwd(dproj, w_in_t, x, g1, dx2, ex=None, tm=512):
    s = x.shape[0]
    steps = s // tm

    def body(*refs):
        (dp_ref, w_ref, x_ref, g1_ref, dx2_ref), (gx_ref, dg_ref), _, hx = _carry(ex, refs, 5, 2)
        if ex is not None:
            @pl.when(pl.program_id(0) == 0)
            def _():
                ex.start(*hx)

        dx, dg_rows = _rms_bwd(_dot(dp_ref[...], w_ref[...]), x_ref[...], g1_ref[...], D_MODEL)
        gx_ref[...] = dx2_ref[...] + dx
        _acc_rows(dg_ref, dg_rows, pl.program_id(0) == 0)
        if ex is not None:
            @pl.when(pl.program_id(0) == steps - 1)
            def _():
                ex.wait(*hx)

    hosted = ex is not None
    return pl.pallas_call(
        body, name="in_bwd", grid=(steps,),
        in_specs=[_row(tm, 4 * ATTN_W), _full((4 * ATTN_W, D_MODEL)), _row(tm, D_MODEL), _full((1, D_MODEL)),
                  _row(tm, D_MODEL)] + (ex.specs() if hosted else []),
        out_specs=[_row(tm, D_MODEL), _full((1, D_MODEL))] + (ex.specs() if hosted else []),
        out_shape=[_sds((s, D_MODEL)), _sds((1, D_MODEL))] + (ex.out_shape() if hosted else []),
        scratch_shapes=ex.scratch() if hosted else [],
        compiler_params=_cp(("arbitrary",)),
    )(dproj, w_in_t, x, g1, dx2, *(ex.srcs if hosted else []))


def _mm_tn(a, b, name, ts=1024):
    s, k = a.shape
    n = b.shape[1]
    bk, bn = min(k, 1024), min(n, 1024)

    def body(a_ref, b_ref, o_ref):
        @pl.when(pl.program_id(2) == 0)
        def _():
            o_ref[...] = jnp.zeros_like(o_ref)

        o_ref[...] += _dot_tn(a_ref[...], b_ref[...])

    return pl.pallas_call(
        body, name=name, grid=(k // bk, n // bn, s // ts),
        in_specs=[pl.BlockSpec((ts, bk), lambda i, j, t: (t, i)), pl.BlockSpec((ts, bn), lambda i, j, t: (t, j))],
        out_specs=pl.BlockSpec((bk, bn), lambda i, j, t: (i, j)), out_shape=_sds((k, n)),
        compiler_params=_cp(("parallel", "parallel", "arbitrary")),
    )(a, b)


def _peer(k):
    x, y, c = lax.axis_index("x"), lax.axis_index("y"), lax.axis_index("c")
    px = 1 - x if k & 4 else x
    py = 1 - y if k & 2 else y
    pc = 1 - c if k & 1 else c
    return (px, py, pc), 4 * px + 2 * py + pc


def _gather_rows(x_shard):
    m_per, n = x_shard.shape

    def body(x_ref, out_ref, send_sems, recv_sems, local_sem):
        x, y, c = lax.axis_index("x"), lax.axis_index("y"), lax.axis_index("c")
        me, sibling = (x, y, c), (x, y, 1 - c)
        chips = [(1 - x, y), (x, 1 - y), (1 - x, 1 - y)]

        def rows(px, py, pc):
            return out_ref.at[pl.ds((4 * px + 2 * py + pc) * m_per, m_per), :]

        def copy(k, block, to, src=None):
            return pltpu.make_async_remote_copy(
                src_ref=rows(*block) if src is None else src, dst_ref=rows(*block),
                send_sem=send_sems.at[k], recv_sem=recv_sems.at[k], device_id=to, device_id_type=MESH)

        mine = pltpu.make_async_copy(x_ref, rows(*me), local_sem)
        mine.start()
        first = [copy(0, me, sibling, src=x_ref)]
        first += [copy(1 + j, me, (*chip, c), src=x_ref) for j, chip in enumerate(chips)]
        for cp in first:
            cp.start()
        passed = [copy(4 + j, (*chip, c), sibling) for j, chip in enumerate(chips)]
        for j, chip in enumerate(chips):
            copy(1 + j, (*chip, c), me).wait_recv()
            passed[j].start()
        copy(0, sibling, me).wait_recv()
        for j, chip in enumerate(chips):
            copy(4 + j, (*chip, 1 - c), me).wait_recv()
        for cp in first + passed:
            cp.wait_send()
        mine.wait()

    return pl.pallas_call(
        body, name="gather_weights", out_shape=_sds((N_DEV * m_per, n), x_shard.dtype),
        in_specs=[pl.BlockSpec(memory_space=pltpu.VMEM)], out_specs=pl.BlockSpec(memory_space=pltpu.VMEM),
        scratch_shapes=[pltpu.SemaphoreType.DMA((7,)), pltpu.SemaphoreType.DMA((7,)), pltpu.SemaphoreType.DMA],
        compiler_params=pltpu.CompilerParams(vmem_limit_bytes=VMEM_LIMIT),
    )(x_shard)


class _Exchange:
    def __init__(self, srcs, whole):
        self.srcs, self.whole, self.n = list(srcs), list(whole), len(srcs)
        self.rows = [a.shape[0] if w else a.shape[0] // N_DEV for a, w in zip(self.srcs, self.whole)]

    def specs(self):
        return [pl.BlockSpec(memory_space=pl.ANY)] * self.n

    def out_shape(self):
        return [_sds((N_DEV, r, a.shape[1]), a.dtype) for r, a in zip(self.rows, self.srcs)]

    def scratch(self):
        return [pltpu.SemaphoreType.DMA((self.n * 7,)), pltpu.SemaphoreType.DMA((self.n * 7,)),
                pltpu.SemaphoreType.DMA((self.n,))]

    def _copies(self, ins, outs, sems):
        send_sems, recv_sems, local_sems = sems
        _, me = _peer(0)
        for w in range(self.n):
            for k in range(N_DEV):
                peer, pidx = _peer(k)
                src = ins[w] if self.whole[w] else ins[w].at[pl.ds(pidx * self.rows[w], self.rows[w]), :]
                if k == 0:
                    yield k, pltpu.make_async_copy(src, outs[w].at[me], local_sems.at[w]), None
                else:
                    sem = w * 7 + k - 1
                    out = pltpu.make_async_remote_copy(src_ref=src, dst_ref=outs[w].at[me], send_sem=send_sems.at[sem],
                                                       recv_sem=recv_sems.at[sem], device_id=peer, device_id_type=MESH)
                    back = pltpu.make_async_remote_copy(src_ref=src, dst_ref=outs[w].at[pidx], send_sem=send_sems.at[sem],
                                                        recv_sem=recv_sems.at[sem], device_id=peer, device_id_type=MESH)
                    yield k, out, back

    def start(self, ins, outs, sems):
        for _, out, _ in self._copies(ins, outs, sems):
            out.start()

    def wait(self, ins, outs, sems):
        for k, out, back in self._copies(ins, outs, sems):
            if k == 0:
                out.wait()
            else:
                back.wait_recv()
                out.wait_send()


def _carry(ex, refs, n_in, n_out):
    nh = ex.n if ex is not None else 0
    ins, hin = refs[:n_in], refs[n_in:n_in + nh]
    outs = refs[n_in + nh:n_in + nh + n_out]
    hout = refs[n_in + nh + n_out:n_in + 2 * nh + n_out]
    rest = refs[n_in + 2 * nh + n_out:]
    if ex is None:
        return ins, outs, rest, None
    return ins, outs, rest[:-3], (hin, hout, rest[-3:])


def _exchange_now(srcs, whole, name):
    ex = _Exchange(srcs, whole)

    def body(*refs):
        _, _, _, (hin, hout, sems) = _carry(ex, refs, 0, 0)
        ex.start(hin, hout, sems)
        ex.wait(hin, hout, sems)

    return pl.pallas_call(body, name=name, out_shape=ex.out_shape(), in_specs=ex.specs(), out_specs=ex.specs(),
                          scratch_shapes=ex.scratch())(*srcs)


def _adamw(w, m, v, gparts, name):
    r, c = w.shape
    tr = r if r * c <= 256 * 1024 else 128 * 1024 // c

    def body(w_ref, m_ref, v_ref, g_ref, go_ref, d_ref, mo_ref, vo_ref):
        g = g_ref[0].astype(F32)
        for i in range(1, N_DEV):
            g = g + g_ref[i].astype(F32)
        nm = ADAM_B1 * m_ref[...] + (1.0 - ADAM_B1) * g
        nv = ADAM_B2 * v_ref[...] + (1.0 - ADAM_B2) * (g * g)
        m_hat = nm / (1.0 - ADAM_B1 ** ADAM_STEP)
        v_hat = nv / (1.0 - ADAM_B2 ** ADAM_STEP)
        go_ref[...] = g
        d_ref[...] = -ADAM_LR * (m_hat / (jnp.sqrt(v_hat) + ADAM_EPS) + ADAM_WD * w_ref[...])
        mo_ref[...] = nm
        vo_ref[...] = nv

    blk = pl.BlockSpec((tr, c), lambda i: (i, 0))
    return pl.pallas_call(
        body, name=name, grid=(r // tr,),
        in_specs=[blk, blk, blk, pl.BlockSpec((N_DEV, tr, c), lambda i: (0, i, 0))],
        out_specs=[blk] * 4, out_shape=[_sds((r, c))] * 4,
        compiler_params=_cp(("parallel",)),
    )(w, m, v, gparts)


def _block_diag(a, states_first):
    a4 = a.reshape(N_CB, 8, SSM_GROUP, SSM_STATE)
    eye = jnp.eye(8, dtype=a.dtype)
    if states_first:
        return jnp.einsum("bgcp,gh->bgphc", a4, eye).reshape(N_CB, CB_STATES, LANES)
    return jnp.einsum("bgcp,gh->bgchp", a4, eye).reshape(N_CB, LANES, CB_STATES)


def _block_diag_of(full, states_first):
    if states_first:
        picked = jnp.einsum("bgphc,gh->bgcp", full.reshape(N_CB, 8, SSM_STATE, 8, SSM_GROUP), jnp.eye(8, dtype=full.dtype))
    else:
        picked = jnp.einsum("bgchp,gh->bgcp", full.reshape(N_CB, 8, SSM_GROUP, 8, SSM_STATE), jnp.eye(8, dtype=full.dtype))
    return picked.reshape(SSM_GROUPS, SSM_GROUP, SSM_STATE)


SMALL_EARLY = ("ssm_a_re", "ssm_a_im", "ssm_log_dt", "ssm_b_re", "ssm_b_im", "ssm_c_re", "ssm_c_im", "ssm_d", "glu_b",
               "attn_out_norm_g", "ssm_out_norm_g", "norm2_g")
SMALL_MID = ("q_norm_g", "k_norm_g")
SMALL_LATE = ("norm1_g",)
SMALL = SMALL_EARLY + SMALL_MID + SMALL_LATE


def _pack_small(arrs):
    parts = []
    for a in arrs:
        flat = a.reshape(-1)
        rows = -(-flat.shape[0] // (8 * LANES)) * 8
        parts.append(jnp.pad(flat, (0, rows * LANES - flat.shape[0])).reshape(rows, LANES))
    return jnp.concatenate(parts, axis=0)


def _unpack_small(packed, shapes):
    out, r0 = [], 0
    for shp in shapes:
        size = math.prod(shp)
        rows = -(-size // (8 * LANES)) * 8
        out.append(packed[r0:r0 + rows].reshape(-1)[:size].reshape(shp))
        r0 += rows
    return out


def kernel(x, norm1_g, w_in, q_norm_g, k_norm_g, ssm_a_re, ssm_a_im, ssm_log_dt, ssm_b_re, ssm_b_im, ssm_c_re, ssm_c_im, ssm_d, glu_w, glu_b, attn_out_norm_g, ssm_out_norm_g, w_out, norm2_g, w_mlp_up, w_mlp_down, loss_target, m_norm1_g, m_w_in, m_q_norm_g, m_k_norm_g, m_ssm_a_re, m_ssm_a_im, m_ssm_log_dt, m_ssm_b_re, m_ssm_b_im, m_ssm_c_re, m_ssm_c_im, m_ssm_d, m_glu_w, m_glu_b, m_attn_out_norm_g, m_ssm_out_norm_g, m_w_out, m_norm2_g, m_w_mlp_up, m_w_mlp_down, v_norm1_g, v_w_in, v_q_norm_g, v_k_norm_g, v_ssm_a_re, v_ssm_a_im, v_ssm_log_dt, v_ssm_b_re, v_ssm_b_im, v_ssm_c_re, v_ssm_c_im, v_ssm_d, v_glu_w, v_glu_b, v_attn_out_norm_g, v_ssm_out_norm_g, v_w_out, v_norm2_g, v_w_mlp_up, v_w_mlp_down):
    weights = dict(norm1_g=norm1_g, w_in=w_in, q_norm_g=q_norm_g, k_norm_g=k_norm_g, ssm_a_re=ssm_a_re,
                   ssm_a_im=ssm_a_im, ssm_log_dt=ssm_log_dt, ssm_b_re=ssm_b_re, ssm_b_im=ssm_b_im,
                   ssm_c_re=ssm_c_re, ssm_c_im=ssm_c_im, ssm_d=ssm_d, glu_w=glu_w, glu_b=glu_b,
                   attn_out_norm_g=attn_out_norm_g, ssm_out_norm_g=ssm_out_norm_g, w_out=w_out, norm2_g=norm2_g,
                   w_mlp_up=w_mlp_up, w_mlp_down=w_mlp_down)
    mom_m = dict(norm1_g=m_norm1_g, w_in=m_w_in, q_norm_g=m_q_norm_g, k_norm_g=m_k_norm_g, ssm_a_re=m_ssm_a_re,
                 ssm_a_im=m_ssm_a_im, ssm_log_dt=m_ssm_log_dt, ssm_b_re=m_ssm_b_re, ssm_b_im=m_ssm_b_im,
                 ssm_c_re=m_ssm_c_re, ssm_c_im=m_ssm_c_im, ssm_d=m_ssm_d, glu_w=m_glu_w, glu_b=m_glu_b,
                 attn_out_norm_g=m_attn_out_norm_g, ssm_out_norm_g=m_ssm_out_norm_g, w_out=m_w_out,
                 norm2_g=m_norm2_g, w_mlp_up=m_w_mlp_up, w_mlp_down=m_w_mlp_down)
    mom_v = dict(norm1_g=v_norm1_g, w_in=v_w_in, q_norm_g=v_q_norm_g, k_norm_g=v_k_norm_g, ssm_a_re=v_ssm_a_re,
                 ssm_a_im=v_ssm_a_im, ssm_log_dt=v_ssm_log_dt, ssm_b_re=v_ssm_b_re, ssm_b_im=v_ssm_b_im,
                 ssm_c_re=v_ssm_c_re, ssm_c_im=v_ssm_c_im, ssm_d=v_ssm_d, glu_w=v_glu_w, glu_b=v_glu_b,
                 attn_out_norm_g=v_attn_out_norm_g, ssm_out_norm_g=v_ssm_out_norm_g, w_out=v_w_out,
                 norm2_g=v_norm2_g, w_mlp_up=v_w_mlp_up, w_mlp_down=v_w_mlp_down)
    order = list(weights)

    xs, tgt = x[0], loss_target[0]
    s = xs.shape[0]
    assert s % ATTN_CHUNK == 0 and (s // N_SEG) % SSM_LK == 0
    seg_len = s // N_SEG
    n_sq = seg_len.bit_length() - 1
    assert 1 << n_sq == seg_len

    w_in_t = _gather_rows(w_in[0].T.astype(BF16))
    w_in_full = w_in_t.T
    later = _Exchange([glu_w[0].astype(BF16), w_out[0].astype(BF16), w_mlp_up[0].T.astype(BF16),
                       w_mlp_down[0].astype(BF16)], [True] * 4)

    gq = jnp.tile(q_norm_g[0], ATTN_W // HEAD_DIM)[None]
    gk = jnp.tile(k_norm_g[0], ATTN_W // HEAD_DIM)[None]
    lane = jnp.arange(ATTN_W) // HEAD_DIM
    gmat = jnp.where(lane[:, None] == lane[None, :], 1.0 / HEAD_DIM, 0.0).astype(BF16)
    a_re3 = ssm_a_re[0][:, None, :]
    a_im3 = ssm_a_im[0][:, None, :]
    ldt3 = ssm_log_dt[0][:, None, None]
    b_re_t = jnp.swapaxes(ssm_b_re[0], 1, 2)
    b_im_t = jnp.swapaxes(ssm_b_im[0], 1, 2)
    c_re, c_im = ssm_c_re[0], ssm_c_im[0]
    dskip = ssm_d[0].reshape(1, SSM_W)

    qk, qn, kn, v, u, xn = _fwd_proj(xs, norm1_g, w_in_full, gq, gk, gmat)
    attn, lse, glu_g, w_out_g, w_up_g, w_down_g = _attn_fwd(qn, kn, v, later)
    glu_full = glu_g.reshape(SSM_W, SSM_W)
    w_out_full = w_out_g.reshape(D_MODEL, D_MODEL)
    w_up_t = w_up_g.reshape(D_FF, D_MODEL)
    w_down_full = w_down_g.reshape(D_FF, D_MODEL)
    w_up_full, w_out_t, w_down_t, glu_t = w_up_t.T, w_out_full.T, w_down_full.T, glu_full.T

    abr3, abi3, bbr, bbi = _ssm_discretize(a_re3, a_im3, ldt3, b_re_t, b_im_t)
    abr, abi = abr3.reshape(1, N_STATE), abi3.reshape(1, N_STATE)
    pw_r, pw_i = _ssm_power(abr, abi, n_sq)
    bb_r, bb_i = _block_diag(bbr, False).astype(BF16), _block_diag(bbi, False).astype(BF16)
    bbt_r, bbt_i = _block_diag(bbr, True).astype(BF16), _block_diag(bbi, True).astype(BF16)
    cc_r, cc_i = _block_diag(c_re, True).astype(BF16), _block_diag(c_im, True).astype(BF16)
    cct_r, cct_i = _block_diag(c_re, False).astype(BF16), _block_diag(c_im, False).astype(BF16)
    seg_major = lambda a: jnp.swapaxes(a.reshape(N_SEG, seg_len, SSM_W), 0, 1)
    seg_minor = lambda a: jnp.swapaxes(a, 0, 1).reshape(s, SSM_W)
    u3 = seg_major(u)
    zero_fin = jnp.zeros((N_SEG, N_STATE), F32)
    ssm_args = (abr, abi, pw_r, pw_i)
    xf_r, xf_i = _ssm_fwd(u3, *ssm_args, zero_fin, zero_fin, bb_r, bb_i, cc_r, cc_i, dskip, True)
    y3, xst_r, xst_i = _ssm_fwd(u3, *ssm_args, xf_r, xf_i, bb_r, bb_i, cc_r, cc_i, dskip, False)
    y = seg_minor(y3)

    mix, x2, h = _fwd_mix(attn, y, xs, glu_full, glu_b, attn_out_norm_g, ssm_out_norm_g, w_out_full, norm2_g)
    r_act, hdn = _mlp_up(h, w_up_full)
    dy, dyb, sse = _mlp_down_loss(hdn, w_down_full, x2, tgt)
    loss = lax.psum(0.5 * sse[0, 0] / D_MODEL, ("x", "y", "c"))

    dup = _mlp_down_bwd(dyb, w_down_t, r_act)
    g_w_down = _mm_tn(hdn, dyb, "grad_w_down")
    dx2, dx2b, g_norm2 = _mlp_up_bwd(dup, w_up_t, x2, norm2_g, dy)
    g_w_up_t = _mm_tn(dup, h, "grad_w_up_t")
    dattn, dys, z_b, dpre_b, g_ga, g_gs, g_glu_b = _mix_bwd(dx2b, w_out_t, attn, y, glu_full, glu_b, glu_t,
                                                             attn_out_norm_g, ssm_out_norm_g)
    g_w_out = _mm_tn(mix, dx2b, "grad_w_out")
    g_glu_w = _mm_tn(z_b, dpre_b, "grad_glu_w")
    dy3 = seg_major(dys)
    bwd_args = (u3, dy3, xst_r, xst_i, abr, abi, pw_r, pw_i)
    lf_r, lf_i = _ssm_bwd(*bwd_args, zero_fin, zero_fin, bb_r, bb_i, bbt_r, bbt_i, cct_r, cct_i, dskip, True)
    du3, dab_r, dab_i, dbb_r, dbb_i, dcc_r, dcc_i, g_d = _ssm_bwd(*bwd_args, lf_r, lf_i, bb_r, bb_i, bbt_r, bbt_i,
                                                                 cct_r, cct_i, dskip, False)
    g_a_re3, g_a_im3, g_ldt3, g_b_re_t, g_b_im_t = _ssm_discretize_bwd(
        a_re3, a_im3, ldt3, b_re_t, b_im_t, dab_r.reshape(a_re3.shape), dab_i.reshape(a_re3.shape),
        _block_diag_of(dbb_r, True), _block_diag_of(dbb_i, True))
    g_c_re, g_c_im = _block_diag_of(dcc_r, False), _block_diag_of(dcc_i, False)
    small_grads = dict(
        ssm_a_re=g_a_re3.reshape(ssm_a_re.shape), ssm_a_im=g_a_im3.reshape(ssm_a_im.shape),
        ssm_log_dt=g_ldt3.reshape(ssm_log_dt.shape), ssm_b_re=jnp.swapaxes(g_b_re_t, 1, 2)[None],
        ssm_b_im=jnp.swapaxes(g_b_im_t, 1, 2)[None], ssm_c_re=g_c_re[None], ssm_c_im=g_c_im[None],
        ssm_d=g_d.reshape(ssm_d.shape), glu_b=g_glu_b, attn_out_norm_g=g_ga, ssm_out_norm_g=g_gs, norm2_g=g_norm2)

    early = _Exchange([g.astype(BF16) for g in (g_glu_w, g_w_out, g_w_up_t, g_w_down)]
                      + [_pack_small([small_grads[n] for n in SMALL_EARLY])], [False] * 4 + [True])
    dqn, dkn, dv, p_glu, p_w_out, p_w_up, p_w_down, p_early = _attn_bwd(qn, kn, v, attn, lse, dattn, early)

    dproj, g_gq, g_gk = _qk_bwd(dqn, dkn, qk, dv, seg_minor(du3), gq, gk, gmat)
    g_w_in_t = _mm_tn(dproj, xn, "grad_w_in_t")
    small_grads["q_norm_g"] = g_gq.reshape(ATTN_W // HEAD_DIM, HEAD_DIM).sum(0)[None]
    small_grads["k_norm_g"] = g_gk.reshape(ATTN_W // HEAD_DIM, HEAD_DIM).sum(0)[None]
    mid = _Exchange([g_w_in_t.astype(BF16), _pack_small([small_grads[n] for n in SMALL_MID])], [False, True])
    grad_x, g_norm1, p_w_in, p_mid = _in_bwd(dproj, w_in_t, xs, norm1_g, dx2, mid)
    (p_late,) = _exchange_now([_pack_small([g_norm1])], [True], "exchange_norm1")

    transposed = {"w_in", "w_mlp_up"}
    res = {}
    for name, gp in (("w_in", p_w_in), ("glu_w", p_glu), ("w_out", p_w_out), ("w_mlp_up", p_w_up), ("w_mlp_down", p_w_down)):
        tr = (lambda a: a.T) if name in transposed else (lambda a: a)
        outs = _adamw(tr(weights[name][0]), tr(mom_m[name][0]), tr(mom_v[name][0]), gp, "adamw_" + name)
        res[name] = [tr(o)[None] for o in outs]
    outs = _adamw(_pack_small([weights[n] for n in SMALL]), _pack_small([mom_m[n] for n in SMALL]),
                  _pack_small([mom_v[n] for n in SMALL]), jnp.concatenate([p_early, p_mid, p_late], axis=1), "adamw_small")
    shapes = [weights[n].shape for n in SMALL]
    unpacked = [_unpack_small(o, shapes) for o in outs]
    for i, n in enumerate(SMALL):
        res[n] = [unpacked[j][i] for j in range(4)]

    return (loss, grad_x[None], *[res[n][0] for n in order], *[res[n][1] for n in order],
            *[res[n][2] for n in order], *[res[n][3] for n in order])
```

```python
import math

import jax
import jax.numpy as jnp
from jax import lax
from jax.experimental import pallas as pl
from jax.experimental.pallas import tpu as pltpu

F32 = jnp.float32
BF16 = jnp.bfloat16

D_MODEL = 1024
ATTN_W = 512
HEAD_DIM = 64
SSM_W = 512
SSM_GROUP = 16
SSM_GROUPS = 32
SSM_STATE = 64
N_STATE = SSM_GROUPS * SSM_STATE
D_FF = 4096
EPS = 1e-6
NEG_INF = -1e30
ATTN_CHUNK = 2048
ATTN_BLOCK = 128
DILATIONS = (1, 4, 16)
N_SEG = 8
SSM_LK = 64
N_DEV = 8
LANES = 128

ADAM_LR = 0.001
ADAM_B1 = 0.9
ADAM_B2 = 0.999
ADAM_EPS = 1e-08
ADAM_WD = 0.01
ADAM_STEP = 10

VMEM_LIMIT = 56 * 1024 * 1024
GELU_C = math.sqrt(2.0 / math.pi)
MESH = pl.DeviceIdType.MESH


def _cp(sem, vmem=VMEM_LIMIT):
    return pltpu.CompilerParams(dimension_semantics=sem, vmem_limit_bytes=vmem)


def _dot(a, b):
    return jnp.dot(a, b, preferred_element_type=F32)


def _dot_nt(a, b):
    return lax.dot_general(a, b, (((1,), (1,)), ((), ())), preferred_element_type=F32)


def _dot_tn(a, b):
    return lax.dot_general(a, b, (((0,), (0,)), ((), ())), preferred_element_type=F32)


def _group_mean(x2, gmat):
    hi = x2.astype(BF16)
    lo = (x2 - hi.astype(F32)).astype(BF16)
    return _dot(hi, gmat) + _dot(lo, gmat)


def _rms(x):
    return lax.rsqrt(jnp.mean(x * x, axis=-1, keepdims=True) + EPS)


def _rms_bwd(dy, x, g, n):
    r = _rms(x)
    gdy = dy * g
    dx = r * gdy - x * (r * r * r) * (jnp.sum(gdy * x, axis=-1, keepdims=True) / n)
    return dx, dy * (x * r)


def _gelu(y):
    t = jnp.tanh(GELU_C * (y + 0.044715 * (y * y * y)))
    return 0.5 * y * (1.0 + t), t


def _full(shape):
    nd = len(shape)
    return pl.BlockSpec(shape, lambda *_: (0,) * nd)


def _sds(shape, dtype=F32):
    return jax.ShapeDtypeStruct(shape, dtype)


def _fwd_proj(x, g1, w_in, gq, gk, gmat, tm=512):
    s = x.shape[0]

    def body(x_ref, g1_ref, w_ref, gq_ref, gk_ref, gm_ref, qk_ref, qn_ref, kn_ref, v_ref, u_ref, xn_ref):
        xv = x_ref[...]
        xnb = ((xv * _rms(xv)) * g1_ref[...]).astype(BF16)
        xn_ref[...] = xnb
        proj = _dot(xnb, w_ref[...])
        q = proj[:, :ATTN_W]
        k = proj[:, ATTN_W:2 * ATTN_W]
        qk_ref[...] = proj[:, :2 * ATTN_W]
        v_ref[...] = proj[:, 2 * ATTN_W:3 * ATTN_W]
        u_ref[...] = proj[:, 3 * ATTN_W:]
        gm = gm_ref[...]
        qn_ref[...] = (q * lax.rsqrt(_group_mean(q * q, gm) + EPS)) * gq_ref[...]
        kn_ref[...] = (k * lax.rsqrt(_group_mean(k * k, gm) + EPS)) * gk_ref[...]

    row = lambda w: pl.BlockSpec((tm, w), lambda i: (i, 0))
    return pl.pallas_call(
        body, name="fwd_proj", grid=(s // tm,),
        in_specs=[row(D_MODEL), _full((1, D_MODEL)), _full((D_MODEL, 4 * ATTN_W)), _full((1, ATTN_W)),
                  _full((1, ATTN_W)), _full((ATTN_W, ATTN_W))],
        out_specs=[row(2 * ATTN_W), row(ATTN_W), row(ATTN_W), row(ATTN_W), row(ATTN_W), row(D_MODEL)],
        out_shape=[_sds((s, 2 * ATTN_W)), _sds((s, ATTN_W)), _sds((s, ATTN_W)), _sds((s, ATTN_W)),
                   _sds((s, ATTN_W)), _sds((s, D_MODEL), BF16)],
        compiler_params=_cp(("parallel",)),
    )(x, g1, w_in, gq, gk, gmat)


def _attn_rows(t, d, nb):
    if d == 1:
        q0 = t * ATTN_BLOCK
        return (t, pl.ds(q0, ATTN_BLOCK), pl.ds(ATTN_CHUNK + q0, ATTN_BLOCK),
                pl.ds(ATTN_CHUNK - ATTN_BLOCK + q0, ATTN_BLOCK))
    r = t // nb
    b = t % nb
    return (b, pl.ds(ATTN_BLOCK * b * d + r, ATTN_BLOCK, stride=d),
            pl.ds(ATTN_CHUNK + ATTN_BLOCK * b * d + r, ATTN_BLOCK, stride=d),
            pl.ds(ATTN_CHUNK + ATTN_BLOCK * (b - 1) * d + r, ATTN_BLOCK, stride=d))


def _attn_masks():
    row = lax.broadcasted_iota(jnp.int32, (ATTN_BLOCK, LANES), 0)
    col = lax.broadcasted_iota(jnp.int32, (ATTN_BLOCK, LANES), 1)
    return row, col


NBLK = ATTN_CHUNK // ATTN_BLOCK


def _attn_bias(bias_s):
    row, col = _attn_masks()
    bias_s[:, pl.ds(0, LANES)] = jnp.where(col >= row, 0.0, NEG_INF)
    bias_s[:, pl.ds(LANES, LANES)] = jnp.where(col <= row, 0.0, NEG_INF)
    return col < HEAD_DIM


def _attn_fwd(qn, kn, v, ex=None, group=4):
    s = qn.shape[0]
    nch = s // ATTN_CHUNK
    scale = HEAD_DIM ** -0.5
    npat = len(DILATIONS)
    n_hp = ATTN_W // LANES

    def body(*refs):
        ((q_ref, kp_ref, kc_ref, vp_ref, vc_ref), (o_ref, lse_ref),
         (kk, vv, kt_s, vb_s, bias_s, m_s, a_s), hx) = _carry(ex, refs, 5, 2)
        i = pl.program_id(1)
        if ex is not None:
            @pl.when(jnp.logical_and(pl.program_id(0) == 0, i == 0))
            def _():
                ex.start(*hx)

        kk[pl.ds(0, ATTN_CHUNK), :] = kp_ref[...]
        kk[pl.ds(ATTN_CHUNK, ATTN_CHUNK), :] = kc_ref[...]
        vv[pl.ds(0, ATTN_CHUNK), :] = vp_ref[...]
        vv[pl.ds(ATTN_CHUNK, ATTN_CHUNK), :] = vc_ref[...]
        head0 = _attn_bias(bias_s)
        first_pen = jnp.where(i > 0, 0.0, NEG_INF)

        for p, d in enumerate(DILATIONS):
            nb = ATTN_CHUNK // (ATTN_BLOCK * d)

            def prep(t, d=d, nb=nb):
                _, _, crows, prows = _attn_rows(t, d, nb)
                kt_s[t, :, pl.ds(0, LANES)] = kk[prows, :].T.astype(BF16)
                kt_s[t, :, pl.ds(LANES, LANES)] = kk[crows, :].T.astype(BF16)
                vp = vv[prows, :]
                vc = vv[crows, :]
                vb_s[2 * t, pl.ds(0, ATTN_BLOCK), :] = jnp.where(head0, vp, 1.0).astype(BF16)
                vb_s[2 * t, pl.ds(ATTN_BLOCK, ATTN_BLOCK), :] = jnp.where(head0, vc, 1.0).astype(BF16)
                vb_s[2 * t + 1, pl.ds(0, ATTN_BLOCK), :] = jnp.where(head0, 1.0, vp).astype(BF16)
                vb_s[2 * t + 1, pl.ds(ATTN_BLOCK, ATTN_BLOCK), :] = jnp.where(head0, 1.0, vc).astype(BF16)

            def main(tg, p=p, d=d, nb=nb):
                st = []
                for g in range(group):
                    t = tg * group + g
                    b, qrows, _, _ = _attn_rows(t, d, nb)
                    q = q_ref[qrows, :]
                    for h in range(2):
                        hm = head0 if h == 0 else jnp.logical_not(head0)
                        st.append(dict(t=t, b=b, qrows=qrows, sc=_dot(jnp.where(hm, q, 0.0).astype(BF16), kt_s[t])))
                for e in st:
                    sc = e["sc"] * scale + bias_s[...]
                    s_p = sc[:, :LANES] + first_pen if e["b"] == 0 else sc[:, :LANES]
                    s_c = sc[:, LANES:]
                    m = jnp.max(jnp.maximum(s_p, s_c), axis=-1, keepdims=True)
                    e["eb"] = jnp.concatenate([jnp.exp(s_p - m), jnp.exp(s_c - m)], axis=1).astype(BF16)
                    e["m"] = jnp.broadcast_to(m, (ATTN_BLOCK, LANES))
                for g in range(group):
                    e0, e1 = st[2 * g], st[2 * g + 1]
                    t = e0["t"]
                    m_s[p, e0["qrows"], :] = jnp.where(head0, e0["m"], e1["m"])
                    a_s[2 * p, e0["qrows"], :] = _dot(e0["eb"], vb_s[2 * t])
                    a_s[2 * p + 1, e0["qrows"], :] = _dot(e1["eb"], vb_s[2 * t + 1])

            for g in range(group):
                prep(g)
            for tg in range(NBLK // group):
                if tg + 1 < NBLK // group:
                    for g in range(group):
                        prep((tg + 1) * group + g)
                main(tg)

        def merge(t, carry):
            rows = pl.ds(pl.multiple_of(t * ATTN_BLOCK, ATTN_BLOCK), ATTN_BLOCK)
            m_all = m_s[0, rows, :]
            for p in range(1, npat):
                m_all = jnp.maximum(m_all, m_s[p, rows, :])
            num = jnp.zeros((ATTN_BLOCK, LANES), F32)
            den = jnp.zeros((ATTN_BLOCK, LANES), F32)
            for p in range(npat):
                w = jnp.exp(m_s[p, rows, :] - m_all)
                a0, a1 = a_s[2 * p, rows, :], a_s[2 * p + 1, rows, :]
                num = num + jnp.where(head0, a0, a1) * w
                den = den + pltpu.roll(jnp.where(head0, a1, a0), HEAD_DIM, 1) * w
            o_ref[rows, :] = num / den
            lse_ref[rows, :] = m_all + jnp.log(den)
            return carry

        lax.fori_loop(0, NBLK, merge, 0, unroll=2)
        if ex is not None:
            @pl.when(jnp.logical_and(pl.program_id(0) == n_hp - 1, i == nch - 1))
            def _():
                ex.wait(*hx)

    cur = pl.BlockSpec((ATTN_CHUNK, LANES), lambda h, i: (i, h))
    prev = pl.BlockSpec((ATTN_CHUNK, LANES), lambda h, i: (jnp.maximum(i - 1, 0), h))
    hosted = ex is not None
    return pl.pallas_call(
        body, name="attn_fwd", grid=(n_hp, nch),
        in_specs=[cur, prev, cur, prev, cur] + (ex.specs() if hosted else []),
        out_specs=[cur, cur] + (ex.specs() if hosted else []),
        out_shape=[_sds((s, ATTN_W)), _sds((s, ATTN_W))] + (ex.out_shape() if hosted else []),
        scratch_shapes=[pltpu.VMEM((2 * ATTN_CHUNK, LANES), F32), pltpu.VMEM((2 * ATTN_CHUNK, LANES), F32),
                        pltpu.VMEM((NBLK, LANES, 2 * LANES), BF16), pltpu.VMEM((2 * NBLK, 2 * ATTN_BLOCK, LANES), BF16),
                        pltpu.VMEM((ATTN_BLOCK, 2 * LANES), F32),
                        pltpu.VMEM((npat, ATTN_CHUNK, LANES), F32), pltpu.VMEM((2 * npat, ATTN_CHUNK, LANES), F32)]
        + (ex.scratch() if hosted else []),
        compiler_params=_cp(("arbitrary", "arbitrary")),
    )(qn, kn, kn, v, v, *(ex.srcs if hosted else []))


def _attn_bwd(qn, kn, v, o, lse, do, ex=None, group=4):
    s = qn.shape[0]
    nch = s // ATTN_CHUNK
    scale = HEAD_DIM ** -0.5
    npat = len(DILATIONS)
    n_hp = ATTN_W // LANES

    def body(*refs):
        ((q_ref, kp_ref, kc_ref, vp_ref, vc_ref, o_ref, lse_ref, do_ref), (dq_ref, dk_ref, dv_ref),
         (kk, vv, dkk, dvv, kt_s, vt_s, kn_s, bias_s, dq_s, dl_s, dkb, dvb), hx) = _carry(ex, refs, 8, 3)
        step = pl.program_id(1)
        i = nch - 1 - step
        if ex is not None:
            @pl.when(jnp.logical_and(pl.program_id(0) == 0, step == 0))
            def _():
                ex.start(*hx)

        kk[pl.ds(0, ATTN_CHUNK), :] = kp_ref[...]
        kk[pl.ds(ATTN_CHUNK, ATTN_CHUNK), :] = kc_ref[...]
        vv[pl.ds(0, ATTN_CHUNK), :] = vp_ref[...]
        vv[pl.ds(ATTN_CHUNK, ATTN_CHUNK), :] = vc_ref[...]

        @pl.when(step == 0)
        def _():
            dkk[pl.ds(ATTN_CHUNK, ATTN_CHUNK), :] = jnp.zeros((ATTN_CHUNK, LANES), F32)
            dvv[pl.ds(ATTN_CHUNK, ATTN_CHUNK), :] = jnp.zeros((ATTN_CHUNK, LANES), F32)

        @pl.when(step > 0)
        def _():
            dkk[pl.ds(ATTN_CHUNK, ATTN_CHUNK), :] = dkk[pl.ds(0, ATTN_CHUNK), :]
            dvv[pl.ds(ATTN_CHUNK, ATTN_CHUNK), :] = dvv[pl.ds(0, ATTN_CHUNK), :]

        dkk[pl.ds(0, ATTN_CHUNK), :] = jnp.zeros((ATTN_CHUNK, LANES), F32)
        dvv[pl.ds(0, ATTN_CHUNK), :] = jnp.zeros((ATTN_CHUNK, LANES), F32)
        head0 = _attn_bias(bias_s)

        def delta(t, carry):
            rows = pl.ds(pl.multiple_of(t * ATTN_BLOCK, ATTN_BLOCK), ATTN_BLOCK)
            prod = do_ref[rows, :] * o_ref[rows, :]
            d0 = jnp.sum(jnp.where(head0, prod, 0.0), axis=-1, keepdims=True)
            d1 = jnp.sum(jnp.where(head0, 0.0, prod), axis=-1, keepdims=True)
            dl_s[rows, :] = jnp.where(head0, d0, d1)
            return carry

        lax.fori_loop(0, NBLK, delta, 0, unroll=2)

        first_pen = jnp.where(i > 0, 0.0, NEG_INF)

        for p, d in enumerate(DILATIONS):
            nb = ATTN_CHUNK // (ATTN_BLOCK * d)

            def prep(t, d=d, nb=nb):
                _, _, crows, prows = _attn_rows(t, d, nb)
                kp, kc = kk[prows, :], kk[crows, :]
                kt_s[t, :, pl.ds(0, LANES)] = kp.T.astype(BF16)
                kt_s[t, :, pl.ds(LANES, LANES)] = kc.T.astype(BF16)
                kn_s[t, pl.ds(0, ATTN_BLOCK), :] = kp.astype(BF16)
                kn_s[t, pl.ds(ATTN_BLOCK, ATTN_BLOCK), :] = kc.astype(BF16)
                vt_s[t, :, pl.ds(0, LANES)] = vv[prows, :].T.astype(BF16)
                vt_s[t, :, pl.ds(LANES, LANES)] = vv[crows, :].T.astype(BF16)

            def main(tg, p=p, d=d, nb=nb):
                st = []
                for g in range(group):
                    t = tg * group + g
                    b, qrows, _, _ = _attn_rows(t, d, nb)
                    q = q_ref[qrows, :]
                    dout = do_ref[qrows, :]
                    lse_b = lse_ref[qrows, :]
                    dl_b = dl_s[qrows, :]
                    for h in range(2):
                        hm = head0 if h == 0 else jnp.logical_not(head0)
                        c0 = h * HEAD_DIM
                        qh = jnp.where(hm, q, 0.0).astype(BF16)
                        doh = jnp.where(hm, dout, 0.0).astype(BF16)
                        st.append(dict(t=t, b=b, qrows=qrows, qh=qh, doh=doh, lse=lse_b[:, c0:c0 + 1],
                                       dl=dl_b[:, c0:c0 + 1], sc=_dot(qh, kt_s[t]), dp=_dot(doh, vt_s[t])))
                for e in st:
                    sc = e["sc"] * scale + bias_s[...]
                    if e["b"] == 0:
                        sc = jnp.concatenate([sc[:, :LANES] + first_pen, sc[:, LANES:]], axis=1)
                    pr = jnp.exp(sc - e["lse"])
                    e["ds"] = (pr * (e["dp"] - e["dl"]) * scale).astype(BF16)
                    e["pr"] = pr.astype(BF16)
                for g in range(group):
                    e0, e1 = st[2 * g], st[2 * g + 1]
                    t = e0["t"]
                    dq_s[p, e0["qrows"], :] = jnp.where(head0, _dot(e0["ds"], kn_s[t]), _dot(e1["ds"], kn_s[t]))
                    dkb[t] = _dot_tn(e0["ds"], e0["qh"]) + _dot_tn(e1["ds"], e1["qh"])
                    dvb[t] = _dot_tn(e0["pr"], e0["doh"]) + _dot_tn(e1["pr"], e1["doh"])

            def scatter(t, d=d, nb=nb):
                _, _, crows, prows = _attn_rows(t, d, nb)
                dkk[prows, :] = dkk[prows, :] + dkb[t, pl.ds(0, ATTN_BLOCK), :]
                dkk[crows, :] = dkk[crows, :] + dkb[t, pl.ds(ATTN_BLOCK, ATTN_BLOCK), :]
                dvv[prows, :] = dvv[prows, :] + dvb[t, pl.ds(0, ATTN_BLOCK), :]
                dvv[crows, :] = dvv[crows, :] + dvb[t, pl.ds(ATTN_BLOCK, ATTN_BLOCK), :]

            n_groups = NBLK // group
            for g in range(group):
                prep(g)
            for tg in range(n_groups):
                if tg + 1 < n_groups:
                    for g in range(group):
                        prep((tg + 1) * group + g)
                main(tg)
                if tg >= 1:
                    for g in range(group):
                        scatter((tg - 1) * group + g)
            for g in range(group):
                scatter((n_groups - 1) * group + g)

        def finish(t, carry):
            rows = pl.ds(pl.multiple_of(t * ATTN_BLOCK, ATTN_BLOCK), ATTN_BLOCK)
            acc = dq_s[0, rows, :]
            for p in range(1, npat):
                acc = acc + dq_s[p, rows, :]
            dq_ref[rows, :] = acc
            return carry

        lax.fori_loop(0, NBLK, finish, 0, unroll=2)
        dk_ref[...] = dkk[pl.ds(ATTN_CHUNK, ATTN_CHUNK), :]
        dv_ref[...] = dvv[pl.ds(ATTN_CHUNK, ATTN_CHUNK), :]
        if ex is not None:
            @pl.when(jnp.logical_and(pl.program_id(0) == n_hp - 1, step == nch - 1))
            def _():
                ex.wait(*hx)

    cur = pl.BlockSpec((ATTN_CHUNK, LANES), lambda h, t: (nch - 1 - t, h))
    prev = pl.BlockSpec((ATTN_CHUNK, LANES), lambda h, t: (jnp.maximum(nch - 2 - t, 0), h))
    big = pltpu.VMEM((2 * ATTN_CHUNK, LANES), F32)
    pair_t = pltpu.VMEM((NBLK, LANES, 2 * LANES), BF16)
    hosted = ex is not None
    return pl.pallas_call(
        body, name="attn_bwd", grid=(n_hp, nch),
        in_specs=[cur, prev, cur, prev, cur, cur, cur, cur] + (ex.specs() if hosted else []),
        out_specs=[cur, cur, cur] + (ex.specs() if hosted else []),
        out_shape=[_sds((s, ATTN_W))] * 3 + (ex.out_shape() if hosted else []),
        scratch_shapes=[big, big, big, big, pair_t, pair_t, pltpu.VMEM((NBLK, 2 * ATTN_BLOCK, LANES), BF16),
                        pltpu.VMEM((ATTN_BLOCK, 2 * LANES), F32),
                        pltpu.VMEM((npat, ATTN_CHUNK, LANES), F32), pltpu.VMEM((ATTN_CHUNK, LANES), F32),
                        pltpu.VMEM((NBLK, 2 * ATTN_BLOCK, LANES), F32), pltpu.VMEM((NBLK, 2 * ATTN_BLOCK, LANES), F32)]
        + (ex.scratch() if hosted else []),
        compiler_params=_cp(("arbitrary", "arbitrary")),
    )(qn, kn, kn, v, v, o, lse, do, *(ex.srcs if hosted else []))


def _attn_fwd_v1(qn, kn, v):
    s = qn.shape[0]
    nch = s // ATTN_CHUNK
    scale = HEAD_DIM ** -0.5
    npat = len(DILATIONS)

    def body(q_ref, kp_ref, kc_ref, vp_ref, vc_ref, o_ref, lse_ref, kk, vv, m_s, l_s, acc_s):
        i = pl.program_id(1)
        kk[pl.ds(0, ATTN_CHUNK), :] = kp_ref[...]
        kk[pl.ds(ATTN_CHUNK, ATTN_CHUNK), :] = kc_ref[...]
        vv[pl.ds(0, ATTN_CHUNK), :] = vp_ref[...]
        vv[pl.ds(ATTN_CHUNK, ATTN_CHUNK), :] = vc_ref[...]
        row, col = _attn_masks()
        head0 = col < HEAD_DIM
        mask_cur = col <= row
        diff = col - row

        for p, d in enumerate(DILATIONS):
            nb = ATTN_CHUNK // (ATTN_BLOCK * d)

            def blk(t, carry, p=p, d=d, nb=nb):
                b, qrows, crows, prows = _attn_rows(t, d, nb)
                q = q_ref[qrows, :]
                kc = kk[crows, :].astype(BF16)
                kp = kk[prows, :].astype(BF16)
                vc = vv[crows, :].astype(BF16)
                vp = vv[prows, :].astype(BF16)
                thr = jnp.where(jnp.logical_or(i > 0, b > 0), 0, 4 * ATTN_BLOCK)
                mask_prev = diff >= thr
                accs, ms, ls = [], [], []
                for h in range(2):
                    hm = head0 if h == 0 else jnp.logical_not(head0)
                    qh = jnp.where(hm, q, 0.0).astype(BF16)
                    s_p = jnp.where(mask_prev, _dot_nt(qh, kp) * scale, NEG_INF)
                    s_c = jnp.where(mask_cur, _dot_nt(qh, kc) * scale, NEG_INF)
                    m = jnp.maximum(jnp.max(s_p, axis=-1, keepdims=True), jnp.max(s_c, axis=-1, keepdims=True))
                    e_p = jnp.exp(s_p - m)
                    e_c = jnp.exp(s_c - m)
                    l = jnp.sum(e_p, axis=-1, keepdims=True) + jnp.sum(e_c, axis=-1, keepdims=True)
                    accs.append(_dot(e_p.astype(BF16), vp) + _dot(e_c.astype(BF16), vc))
                    ms.append(jnp.broadcast_to(m, (ATTN_BLOCK, LANES)))
                    ls.append(jnp.broadcast_to(l, (ATTN_BLOCK, LANES)))
                m_s[p, qrows, :] = jnp.where(head0, ms[0], ms[1])
                l_s[p, qrows, :] = jnp.where(head0, ls[0], ls[1])
                acc_s[p, qrows, :] = jnp.where(head0, accs[0], accs[1])
                return carry

            lax.fori_loop(0, ATTN_CHUNK // ATTN_BLOCK, blk, 0)

        def merge(t, carry):
            rows = pl.ds(pl.multiple_of(t * ATTN_BLOCK, ATTN_BLOCK), ATTN_BLOCK)
            m_all = m_s[0, rows, :]
            for p in range(1, npat):
                m_all = jnp.maximum(m_all, m_s[p, rows, :])
            num = jnp.zeros((ATTN_BLOCK, LANES), F32)
            den = jnp.zeros((ATTN_BLOCK, LANES), F32)
            for p in range(npat):
                w = jnp.exp(m_s[p, rows, :] - m_all)
                num = num + acc_s[p, rows, :] * w
                den = den + l_s[p, rows, :] * w
            o_ref[rows, :] = num / den
            lse_ref[rows, :] = m_all + jnp.log(den)
            return carry

        lax.fori_loop(0, ATTN_CHUNK // ATTN_BLOCK, merge, 0)

    cur = pl.BlockSpec((ATTN_CHUNK, LANES), lambda h, i: (i, h))
    prev = pl.BlockSpec((ATTN_CHUNK, LANES), lambda h, i: (jnp.maximum(i - 1, 0), h))
    return pl.pallas_call(
        body, name="attn_fwd", grid=(ATTN_W // LANES, nch),
        in_specs=[cur, prev, cur, prev, cur],
        out_specs=[cur, cur],
        out_shape=[_sds((s, ATTN_W)), _sds((s, ATTN_W))],
        scratch_shapes=[pltpu.VMEM((2 * ATTN_CHUNK, LANES), F32), pltpu.VMEM((2 * ATTN_CHUNK, LANES), F32),
                        pltpu.VMEM((npat, ATTN_CHUNK, LANES), F32), pltpu.VMEM((npat, ATTN_CHUNK, LANES), F32),
                        pltpu.VMEM((npat, ATTN_CHUNK, LANES), F32)],
        compiler_params=_cp(("parallel", "parallel")),
    )(qn, kn, kn, v, v)


def _attn_bwd_v1(qn, kn, v, o, lse, do):
    s = qn.shape[0]
    nch = s // ATTN_CHUNK
    scale = HEAD_DIM ** -0.5
    npat = len(DILATIONS)

    def body(q_ref, kp_ref, kc_ref, vp_ref, vc_ref, o_ref, lse_ref, do_ref, dq_ref, dk_ref, dv_ref,
             kk, vv, dkk, dvv, dq_s, dl_s):
        step = pl.program_id(1)
        i = nch - 1 - step
        kk[pl.ds(0, ATTN_CHUNK), :] = kp_ref[...]
        kk[pl.ds(ATTN_CHUNK, ATTN_CHUNK), :] = kc_ref[...]
        vv[pl.ds(0, ATTN_CHUNK), :] = vp_ref[...]
        vv[pl.ds(ATTN_CHUNK, ATTN_CHUNK), :] = vc_ref[...]

        @pl.when(step == 0)
        def _():
            dkk[pl.ds(ATTN_CHUNK, ATTN_CHUNK), :] = jnp.zeros((ATTN_CHUNK, LANES), F32)
            dvv[pl.ds(ATTN_CHUNK, ATTN_CHUNK), :] = jnp.zeros((ATTN_CHUNK, LANES), F32)

        @pl.when(step > 0)
        def _():
            dkk[pl.ds(ATTN_CHUNK, ATTN_CHUNK), :] = dkk[pl.ds(0, ATTN_CHUNK), :]
            dvv[pl.ds(ATTN_CHUNK, ATTN_CHUNK), :] = dvv[pl.ds(0, ATTN_CHUNK), :]

        dkk[pl.ds(0, ATTN_CHUNK), :] = jnp.zeros((ATTN_CHUNK, LANES), F32)
        dvv[pl.ds(0, ATTN_CHUNK), :] = jnp.zeros((ATTN_CHUNK, LANES), F32)

        row, col = _attn_masks()
        head0 = col < HEAD_DIM
        mask_cur = col <= row
        diff = col - row

        def delta(t, carry):
            rows = pl.ds(pl.multiple_of(t * ATTN_BLOCK, ATTN_BLOCK), ATTN_BLOCK)
            prod = do_ref[rows, :] * o_ref[rows, :]
            d0 = jnp.sum(jnp.where(head0, prod, 0.0), axis=-1, keepdims=True)
            d1 = jnp.sum(jnp.where(head0, 0.0, prod), axis=-1, keepdims=True)
            dl_s[rows, :] = jnp.where(head0, d0, d1)
            return carry

        lax.fori_loop(0, ATTN_CHUNK // ATTN_BLOCK, delta, 0)

        for p, d in enumerate(DILATIONS):
            nb = ATTN_CHUNK // (ATTN_BLOCK * d)

            def blk(t, carry, p=p, d=d, nb=nb):
                b, qrows, crows, prows = _attn_rows(t, d, nb)
                q = q_ref[qrows, :]
                dout = do_ref[qrows, :]
                lse_b = lse_ref[qrows, :]
                dl_b = dl_s[qrows, :]
                kc = kk[crows, :].astype(BF16)
                kp = kk[prows, :].astype(BF16)
                vc = vv[crows, :].astype(BF16)
                vp = vv[prows, :].astype(BF16)
                thr = jnp.where(jnp.logical_or(i > 0, b > 0), 0, 4 * ATTN_BLOCK)
                mask_prev = diff >= thr
                dqs = []
                dk_p = jnp.zeros((ATTN_BLOCK, LANES), F32)
                dk_c = jnp.zeros((ATTN_BLOCK, LANES), F32)
                dv_p = jnp.zeros((ATTN_BLOCK, LANES), F32)
                dv_c = jnp.zeros((ATTN_BLOCK, LANES), F32)
                for h in range(2):
                    hm = head0 if h == 0 else jnp.logical_not(head0)
                    c0 = h * HEAD_DIM
                    qh = jnp.where(hm, q, 0.0).astype(BF16)
                    doh = jnp.where(hm, dout, 0.0).astype(BF16)
                    lse_h = lse_b[:, c0:c0 + 1]
                    dl_h = dl_b[:, c0:c0 + 1]
                    s_p = jnp.where(mask_prev, _dot_nt(qh, kp) * scale, NEG_INF)
                    s_c = jnp.where(mask_cur, _dot_nt(qh, kc) * scale, NEG_INF)
                    p_p = jnp.exp(s_p - lse_h)
                    p_c = jnp.exp(s_c - lse_h)
                    ds_p = (p_p * (_dot_nt(doh, vp) - dl_h) * scale).astype(BF16)
                    ds_c = (p_c * (_dot_nt(doh, vc) - dl_h) * scale).astype(BF16)
                    dqs.append(_dot(ds_p, kp) + _dot(ds_c, kc))
                    dk_p = dk_p + _dot_tn(ds_p, qh)
                    dk_c = dk_c + _dot_tn(ds_c, qh)
                    dv_p = dv_p + _dot_tn(p_p.astype(BF16), doh)
                    dv_c = dv_c + _dot_tn(p_c.astype(BF16), doh)
                dq_s[p, qrows, :] = jnp.where(head0, dqs[0], dqs[1])
                dkk[prows, :] = dkk[prows, :] + dk_p
                dkk[crows, :] = dkk[crows, :] + dk_c
                dvv[prows, :] = dvv[prows, :] + dv_p
                dvv[crows, :] = dvv[crows, :] + dv_c
                return carry

            lax.fori_loop(0, ATTN_CHUNK // ATTN_BLOCK, blk, 0)

        def finish(t, carry):
            rows = pl.ds(pl.multiple_of(t * ATTN_BLOCK, ATTN_BLOCK), ATTN_BLOCK)
            acc = dq_s[0, rows, :]
            for p in range(1, npat):
                acc = acc + dq_s[p, rows, :]
            dq_ref[rows, :] = acc
            return carry

        lax.fori_loop(0, ATTN_CHUNK // ATTN_BLOCK, finish, 0)
        dk_ref[...] = dkk[pl.ds(ATTN_CHUNK, ATTN_CHUNK), :]
        dv_ref[...] = dvv[pl.ds(ATTN_CHUNK, ATTN_CHUNK), :]

    cur = pl.BlockSpec((ATTN_CHUNK, LANES), lambda h, t: (nch - 1 - t, h))
    prev = pl.BlockSpec((ATTN_CHUNK, LANES), lambda h, t: (jnp.maximum(nch - 2 - t, 0), h))
    big = pltpu.VMEM((2 * ATTN_CHUNK, LANES), F32)
    return pl.pallas_call(
        body, name="attn_bwd", grid=(ATTN_W // LANES, nch),
        in_specs=[cur, prev, cur, prev, cur, cur, cur, cur],
        out_specs=[cur, cur, cur],
        out_shape=[_sds((s, ATTN_W))] * 3,
        scratch_shapes=[big, big, big, big, pltpu.VMEM((npat, ATTN_CHUNK, LANES), F32),
                        pltpu.VMEM((ATTN_CHUNK, LANES), F32)],
        compiler_params=_cp(("parallel", "arbitrary")),
    )(qn, kn, kn, v, v, o, lse, do)


def _discretize(lr, li, dt):
    mag = jnp.exp(lr * dt)
    abr = mag * jnp.cos(li * dt)
    abi = mag * jnp.sin(li * dt)
    den = lr * lr + li * li
    nr, ni = abr - 1.0, abi
    cr = (nr * lr + ni * li) / den
    ci = (ni * lr - nr * li) / den
    return abr, abi, den, nr, ni, cr, ci


def _ssm_discretize(a_re, a_im, log_dt, b_re_t, b_im_t):
    def body(ar_ref, ai_ref, ldt_ref, br_ref, bi_ref, abr_ref, abi_ref, bbr_ref, bbi_ref):
        abr, abi, _, _, _, cr, ci = _discretize(ar_ref[...], ai_ref[...], jnp.exp(ldt_ref[...]))
        br, bi = br_ref[...], bi_ref[...]
        abr_ref[...] = abr
        abi_ref[...] = abi
        bbr_ref[...] = cr * br - ci * bi
        bbi_ref[...] = cr * bi + ci * br

    return pl.pallas_call(
        body, name="ssm_discretize",
        out_shape=[_sds(a_re.shape)] * 2 + [_sds(b_re_t.shape)] * 2,
    )(a_re, a_im, log_dt, b_re_t, b_im_t)


def _ssm_discretize_bwd(a_re, a_im, log_dt, b_re_t, b_im_t, dabr, dabi, dbbr, dbbi):
    def body(ar_ref, ai_ref, ldt_ref, br_ref, bi_ref, dabr_ref, dabi_ref, dbbr_ref, dbbi_ref,
             dar_ref, dai_ref, dldt_ref, dbr_ref, dbi_ref):
        lr, li = ar_ref[...], ai_ref[...]
        dt = jnp.exp(ldt_ref[...])
        abr, abi, den, nr, ni, cr, ci = _discretize(lr, li, dt)
        br, bi = br_ref[...], bi_ref[...]
        gbr, gbi = dbbr_ref[...], dbbi_ref[...]
        dcr = jnp.sum(gbr * br + gbi * bi, axis=1, keepdims=True)
        dci = jnp.sum(gbi * br - gbr * bi, axis=1, keepdims=True)
        dbr_ref[...] = cr * gbr + ci * gbi
        dbi_ref[...] = cr * gbi - ci * gbr
        dnr = (dcr * lr - dci * li) / den
        dni = (dcr * li + dci * lr) / den
        dden = -(dcr * cr + dci * ci) / den
        dlr = (dcr * nr + dci * ni) / den + dden * 2.0 * lr
        dli = (dcr * ni - dci * nr) / den + dden * 2.0 * li
        gabr = dabr_ref[...] + dnr
        gabi = dabi_ref[...] + dni
        dphi = gabr * abr + gabi * abi
        dth = gabi * abr - gabr * abi
        dar_ref[...] = dlr + dphi * dt
        dai_ref[...] = dli + dth * dt
        dldt_ref[...] = jnp.sum(dphi * lr + dth * li, axis=2, keepdims=True) * dt

    return pl.pallas_call(
        body, name="ssm_discretize_bwd",
        out_shape=[_sds(a_re.shape)] * 2 + [_sds(log_dt.shape)] + [_sds(b_re_t.shape)] * 2,
    )(a_re, a_im, log_dt, b_re_t, b_im_t, dabr, dabi, dbbr, dbbi)


def _ssm_power(abr, abi, n_sq):
    def body(r_ref, i_ref, or_ref, oi_ref):
        r, i = r_ref[...], i_ref[...]
        for _ in range(n_sq):
            r, i = r * r - i * i, 2.0 * r * i
        or_ref[...] = r
        oi_ref[...] = i

    return pl.pallas_call(body, name="ssm_power", out_shape=[_sds(abr.shape)] * 2)(abr, abi)


N_CB = SSM_W // LANES
CB_STATES = N_STATE // N_CB
ROWS = N_SEG * SSM_LK


class _Neg:
    def __init__(self, ref):
        self.ref = ref

    def __getitem__(self, idx):
        return -self.ref[idx]


def _seg_init(fin_r, fin_i, pw_r, pw_i, x_r, x_i, reverse):
    zero = jnp.zeros((1, N_STATE), F32)
    cr, ci = zero, zero
    order = range(N_SEG - 1, -1, -1) if reverse else range(N_SEG)
    pr = pw_r[...]
    pi = -pw_i[...] if reverse else pw_i[...]
    for j in order:
        x_r[pl.ds(j, 1), :] = cr
        x_i[pl.ds(j, 1), :] = ci
        fr, fi = fin_r[pl.ds(j, 1), :], fin_i[pl.ds(j, 1), :]
        cr, ci = fr + pr * cr - pi * ci, fi + pr * ci + pi * cr


def _scan_rows(a_r, a_i, b_r, b_i, x_r, x_i, o_r, o_i, b_off, n_steps, reverse):
    w = 512
    for c in range(N_STATE // w):
        cols = pl.ds(c * w, w)
        ar = jnp.broadcast_to(a_r[:, cols], (N_SEG, w))
        ai = jnp.broadcast_to(a_i[:, cols], (N_SEG, w))

        def step(t, carry, cols=cols, ar=ar, ai=ai):
            xr, xi = carry
            k = (n_steps - 1 - t) if reverse else t
            rows = pl.ds(pl.multiple_of(k * N_SEG + b_off, N_SEG), N_SEG)
            nr = ar * xr - ai * xi + b_r[rows, cols]
            ni = ar * xi + ai * xr + b_i[rows, cols]
            o_r[rows, cols] = nr
            o_i[rows, cols] = ni
            return nr, ni

        xr, xi = lax.fori_loop(0, n_steps, step, (x_r[:, cols], x_i[:, cols]), unroll=4)
        x_r[:, cols] = xr
        x_i[:, cols] = xi


def _permute_in(src_ref, dst):
    for c in range(N_CB):
        dst[c] = src_ref[:, :, pl.ds(c * LANES, LANES)].reshape(ROWS, LANES)


def _permute_out(src, dst_ref):
    for c in range(N_CB):
        dst_ref[:, :, pl.ds(c * LANES, LANES)] = src[c].reshape(SSM_LK, N_SEG, LANES)


def _ssm_fwd_v1(u3, abr, abi, pw_r, pw_i, fin_r, fin_i, bb_r, bb_i, cc_r, cc_i, dskip, finals_only):
    sl = u3.shape[0]
    nch = sl // SSM_LK

    def body(u_ref, abr_ref, abi_ref, pwr_ref, pwi_ref, finr_ref, fini_ref, bbr_ref, bbi_ref,
             ccr_ref, cci_ref, d_ref, *rest):
        if finals_only:
            xfr_ref, xfi_ref, up, xs_r, xs_i, x_r, x_i = rest
        else:
            y_ref, xsr_ref, xsi_ref, up, yp, xs_r, xs_i, x_r, x_i = rest
        k = pl.program_id(0)

        @pl.when(k == 0)
        def _():
            _seg_init(finr_ref, fini_ref, pwr_ref, pwi_ref, x_r, x_i, False)

        if not finals_only:
            xsr_ref[0] = x_r[...]
            xsi_ref[0] = x_i[...]
        _permute_in(u_ref, up)
        for c in range(N_CB):
            lhs = up[c].astype(BF16)
            xs_r[:, pl.ds(c * CB_STATES, CB_STATES)] = _dot(lhs, bbr_ref[c])
            xs_i[:, pl.ds(c * CB_STATES, CB_STATES)] = _dot(lhs, bbi_ref[c])
        _scan_rows(abr_ref, abi_ref, xs_r, xs_i, x_r, x_i, xs_r, xs_i, 0, SSM_LK, False)
        if finals_only:
            @pl.when(k == nch - 1)
            def _():
                xfr_ref[...] = x_r[...]
                xfi_ref[...] = x_i[...]
        else:
            for c in range(N_CB):
                cols = pl.ds(c * CB_STATES, CB_STATES)
                yp[c] = (_dot(xs_r[:, cols].astype(BF16), ccr_ref[c]) - _dot(xs_i[:, cols].astype(BF16), cci_ref[c])
                         + d_ref[:, pl.ds(c * LANES, LANES)] * up[c])
            _permute_out(yp, y_ref)

    ublk = pl.BlockSpec((SSM_LK, N_SEG, SSM_W), lambda k: (k, 0, 0))
    st = pl.BlockSpec((1, N_SEG, N_STATE), lambda k: (k, 0, 0))
    vec = _full((1, N_STATE))
    mat = _full((N_SEG, N_STATE))
    chunk = pltpu.VMEM((N_CB, ROWS, LANES), F32)
    big = pltpu.VMEM((ROWS, N_STATE), F32)
    small = pltpu.VMEM((N_SEG, N_STATE), F32)
    if finals_only:
        out_specs, out_shape = [mat, mat], [_sds((N_SEG, N_STATE))] * 2
        scratch, name = [chunk, big, big, small, small], "ssm_fwd_finals"
    else:
        out_specs = [ublk, st, st]
        out_shape = [_sds(u3.shape)] + [_sds((nch, N_SEG, N_STATE))] * 2
        scratch, name = [chunk, chunk, big, big, small, small], "ssm_fwd"
    return pl.pallas_call(
        body, name=name, grid=(nch,),
        in_specs=[ublk, vec, vec, vec, vec, mat, mat,
                  _full((N_CB, LANES, CB_STATES)), _full((N_CB, LANES, CB_STATES)),
                  _full((N_CB, CB_STATES, LANES)), _full((N_CB, CB_STATES, LANES)), _full((1, SSM_W))],
        out_specs=out_specs, out_shape=out_shape, scratch_shapes=scratch,
        compiler_params=_cp(("arbitrary",)),
    )(u3, abr, abi, pw_r, pw_i, fin_r, fin_i, bb_r, bb_i, cc_r, cc_i, dskip)


def _ssm_bwd_v1(u3, dy3, xst_r, xst_i, abr, abi, pw_r, pw_i, fin_r, fin_i, bb_r, bb_i, bbt_r, bbt_i,
                cct_r, cct_i, dskip, finals_only):
    sl = u3.shape[0]
    nch = sl // SSM_LK

    def body(u_ref, g_ref, xsr_ref, xsi_ref, abr_ref, abi_ref, pwr_ref, pwi_ref,
             finr_ref, fini_ref, bbr_ref, bbi_ref, btr_ref, bti_ref, ctr_ref, cti_ref, d_ref, *rest):
        if finals_only:
            lfr_ref, lfi_ref, gp, l_r, l_i, lam_r, lam_i = rest
        else:
            (du_ref, dar_ref, dai_ref, dbr_ref, dbi_ref, dcr_ref, dci_ref, dd_ref,
             gp, up, yp, l_r, l_i, lam_r, lam_i, x_r, x_i, xx_r, xx_i, sar, sai, sdd) = rest
        t = pl.program_id(0)

        @pl.when(t == 0)
        def _():
            _seg_init(finr_ref, fini_ref, pwr_ref, pwi_ref, lam_r, lam_i, True)
            if not finals_only:
                sar[...] = jnp.zeros_like(sar)
                sai[...] = jnp.zeros_like(sai)
                sdd[...] = jnp.zeros_like(sdd)
                dbr_ref[...] = jnp.zeros_like(dbr_ref)
                dbi_ref[...] = jnp.zeros_like(dbi_ref)
                dcr_ref[...] = jnp.zeros_like(dcr_ref)
                dci_ref[...] = jnp.zeros_like(dci_ref)

        _permute_in(g_ref, gp)
        for c in range(N_CB):
            cols = pl.ds(c * CB_STATES, CB_STATES)
            lhs = gp[c].astype(BF16)
            l_r[:, cols] = _dot(lhs, ctr_ref[c])
            l_i[:, cols] = -_dot(lhs, cti_ref[c])
        if not finals_only:
            _permute_in(u_ref, up)
            x_r[...] = xsr_ref[0]
            x_i[...] = xsi_ref[0]
            xx_r[pl.ds(0, N_SEG), :] = x_r[...]
            xx_i[pl.ds(0, N_SEG), :] = x_i[...]
            for c in range(N_CB):
                lhs = up[c].astype(BF16)
                xx_r[pl.ds(N_SEG, ROWS), pl.ds(c * CB_STATES, CB_STATES)] = _dot(lhs, bbr_ref[c])
                xx_i[pl.ds(N_SEG, ROWS), pl.ds(c * CB_STATES, CB_STATES)] = _dot(lhs, bbi_ref[c])
            _scan_rows(abr_ref, abi_ref, xx_r, xx_i, x_r, x_i, xx_r, xx_i, N_SEG, SSM_LK, False)
        _scan_rows(abr_ref, _Neg(abi_ref), l_r, l_i, lam_r, lam_i, l_r, l_i, 0, SSM_LK, True)
        if finals_only:
            @pl.when(t == nch - 1)
            def _():
                lfr_ref[...] = lam_r[...]
                lfi_ref[...] = lam_i[...]
        else:
            w = 512
            for cc in range(N_STATE // w):
                cols = pl.ds(cc * w, w)

                def acc_step(kk_, carry, cols=cols):
                    sr, si = carry
                    rows = pl.ds(pl.multiple_of(kk_ * N_SEG, N_SEG), N_SEG)
                    lr, li = l_r[rows, cols], l_i[rows, cols]
                    pr, pi = xx_r[rows, cols], xx_i[rows, cols]
                    return sr + lr * pr + li * pi, si + li * pr - lr * pi

                z = jnp.zeros((N_SEG, w), F32)
                sr, si = lax.fori_loop(0, SSM_LK, acc_step, (z, z), unroll=4)
                sar[:, cols] += sr
                sai[:, cols] += si
            for c in range(N_CB):
                cols = pl.ds(c * CB_STATES, CB_STATES)
                lrb = l_r[:, cols].astype(BF16)
                lib = l_i[:, cols].astype(BF16)
                ub = up[c].astype(BF16)
                gb = gp[c].astype(BF16)
                dbr_ref[c] += _dot_tn(lrb, ub)
                dbi_ref[c] += _dot_tn(lib, ub)
                dcr_ref[c] += _dot_tn(gb, xx_r[pl.ds(N_SEG, ROWS), cols].astype(BF16))
                dci_ref[c] += -_dot_tn(gb, xx_i[pl.ds(N_SEG, ROWS), cols].astype(BF16))
                yp[c] = _dot(lrb, btr_ref[c]) + _dot(lib, bti_ref[c]) + d_ref[:, pl.ds(c * LANES, LANES)] * gp[c]
                prod = gp[c] * up[c]
                sdd[:, pl.ds(c * LANES, LANES)] += jnp.sum(prod.reshape(SSM_LK, N_SEG, LANES), axis=0)
            _permute_out(yp, du_ref)

            @pl.when(t == nch - 1)
            def _():
                dar_ref[...] = jnp.sum(sar[...], axis=0, keepdims=True)
                dai_ref[...] = jnp.sum(sai[...], axis=0, keepdims=True)
                dd_ref[...] = jnp.sum(sdd[...], axis=0, keepdims=True)

    ublk = pl.BlockSpec((SSM_LK, N_SEG, SSM_W), lambda t: (nch - 1 - t, 0, 0))
    st = pl.BlockSpec((1, N_SEG, N_STATE), lambda t: (nch - 1 - t, 0, 0))
    vec = _full((1, N_STATE))
    mat = _full((N_SEG, N_STATE))
    cs = _full((N_CB, LANES, CB_STATES))
    sc = _full((N_CB, CB_STATES, LANES))
    in_specs = [ublk, ublk, st, st, vec, vec, vec, vec, mat, mat, cs, cs, sc, sc, cs, cs, _full((1, SSM_W))]
    chunk = pltpu.VMEM((N_CB, ROWS, LANES), F32)
    big = pltpu.VMEM((ROWS, N_STATE), F32)
    small = pltpu.VMEM((N_SEG, N_STATE), F32)
    if finals_only:
        out_specs, out_shape = [mat, mat], [_sds((N_SEG, N_STATE))] * 2
        scratch, name = [chunk, big, big, small, small], "ssm_bwd_finals"
    else:
        out_specs = [ublk, vec, vec, sc, sc, cs, cs, _full((1, SSM_W))]
        out_shape = ([_sds(u3.shape), _sds((1, N_STATE)), _sds((1, N_STATE))]
                     + [_sds((N_CB, CB_STATES, LANES))] * 2 + [_sds((N_CB, LANES, CB_STATES))] * 2
                     + [_sds((1, SSM_W))])
        xx = pltpu.VMEM((ROWS + N_SEG, N_STATE), F32)
        scratch = [chunk, chunk, chunk, big, big, small, small, small, small, xx, xx, small, small,
                   pltpu.VMEM((N_SEG, SSM_W), F32)]
        name = "ssm_bwd"
    return pl.pallas_call(
        body, name=name, grid=(nch,), in_specs=in_specs, out_specs=out_specs, out_shape=out_shape,
        scratch_shapes=scratch, compiler_params=_cp(("arbitrary",)),
    )(u3, dy3, xst_r, xst_i, abr, abi, pw_r, pw_i, fin_r, fin_i, bb_r, bb_i, bbt_r, bbt_i, cct_r, cct_i, dskip)


def _scan_block(a_r, a_i, c, b_r, b_i, b_off, x_r, x_i, reverse, acc=None):
    cols = pl.ds(c * CB_STATES, CB_STATES)
    ar = jnp.broadcast_to(a_r[:, cols], (N_SEG, CB_STATES))
    ai = jnp.broadcast_to(a_i[:, cols], (N_SEG, CB_STATES))
    xr, xi = x_r[:, cols], x_i[:, cols]
    if acc is not None:
        sr = jnp.zeros((N_SEG, CB_STATES), F32)
        si = jnp.zeros((N_SEG, CB_STATES), F32)
    for t in range(SSM_LK):
        k = (SSM_LK - 1 - t) if reverse else t
        rows = pl.ds(k * N_SEG + b_off, N_SEG)
        xr, xi = ar * xr - ai * xi + b_r[rows, :], ar * xi + ai * xr + b_i[rows, :]
        b_r[rows, :] = xr
        b_i[rows, :] = xi
        if acc is not None:
            pr, pi = acc[0][pl.ds(k * N_SEG, N_SEG), :], acc[1][pl.ds(k * N_SEG, N_SEG), :]
            sr = sr + xr * pr + xi * pi
            si = si + xi * pr - xr * pi
    x_r[:, cols] = xr
    x_i[:, cols] = xi
    if acc is not None:
        acc[2][:, cols] += sr
        acc[3][:, cols] += si


def _ssm_fwd(u3, abr, abi, pw_r, pw_i, fin_r, fin_i, bb_r, bb_i, cc_r, cc_i, dskip, finals_only):
    sl = u3.shape[0]
    nch = sl // SSM_LK

    def body(u_ref, abr_ref, abi_ref, pwr_ref, pwi_ref, finr_ref, fini_ref, bbr_ref, bbi_ref,
             ccr_ref, cci_ref, d_ref, *rest):
        if finals_only:
            xfr_ref, xfi_ref, up, x_r, x_i = rest[:5]
        else:
            y_ref, xsr_ref, xsi_ref, up, yp, x_r, x_i = rest[:7]
        xs_r, xs_i = rest[-2 * N_CB:-N_CB], rest[-N_CB:]
        k = pl.program_id(0)

        @pl.when(k == 0)
        def _():
            _seg_init(finr_ref, fini_ref, pwr_ref, pwi_ref, x_r, x_i, False)

        if not finals_only:
            xsr_ref[0] = x_r[...]
            xsi_ref[0] = x_i[...]
        _permute_in(u_ref, up)

        def drive(c):
            lhs = up[c].astype(BF16)
            xs_r[c][...] = _dot(lhs, bbr_ref[c])
            xs_i[c][...] = _dot(lhs, bbi_ref[c])

        def readout(c):
            yp[c] = (_dot(xs_r[c][...].astype(BF16), ccr_ref[c]) - _dot(xs_i[c][...].astype(BF16), cci_ref[c])
                     + d_ref[:, pl.ds(c * LANES, LANES)] * up[c])

        drive(0)
        for c in range(N_CB):
            if c + 1 < N_CB:
                drive(c + 1)
            if c >= 1 and not finals_only:
                readout(c - 1)
            _scan_block(abr_ref, abi_ref, c, xs_r[c], xs_i[c], 0, x_r, x_i, False)
        if finals_only:
            @pl.when(k == nch - 1)
            def _():
                xfr_ref[...] = x_r[...]
                xfi_ref[...] = x_i[...]
        else:
            readout(N_CB - 1)
            _permute_out(yp, y_ref)

    ublk = pl.BlockSpec((SSM_LK, N_SEG, SSM_W), lambda k: (k, 0, 0))
    st = pl.BlockSpec((1, N_SEG, N_STATE), lambda k: (k, 0, 0))
    vec = _full((1, N_STATE))
    mat = _full((N_SEG, N_STATE))
    chunk = pltpu.VMEM((N_CB, ROWS, LANES), F32)
    blocks = [pltpu.VMEM((ROWS, CB_STATES), F32)] * (2 * N_CB)
    small = pltpu.VMEM((N_SEG, N_STATE), F32)
    if finals_only:
        out_specs, out_shape = [mat, mat], [_sds((N_SEG, N_STATE))] * 2
        scratch, name = [chunk, small, small] + blocks, "ssm_fwd_finals"
    else:
        out_specs = [ublk, st, st]
        out_shape = [_sds(u3.shape)] + [_sds((nch, N_SEG, N_STATE))] * 2
        scratch, name = [chunk, chunk, small, small] + blocks, "ssm_fwd"
    return pl.pallas_call(
        body, name=name, grid=(nch,),
        in_specs=[ublk, vec, vec, vec, vec, mat, mat,
                  _full((N_CB, LANES, CB_STATES)), _full((N_CB, LANES, CB_STATES)),
                  _full((N_CB, CB_STATES, LANES)), _full((N_CB, CB_STATES, LANES)), _full((1, SSM_W))],
        out_specs=out_specs, out_shape=out_shape, scratch_shapes=scratch,
        compiler_params=_cp(("arbitrary",)),
    )(u3, abr, abi, pw_r, pw_i, fin_r, fin_i, bb_r, bb_i, cc_r, cc_i, dskip)


def _ssm_bwd(u3, dy3, xst_r, xst_i, abr, abi, pw_r, pw_i, fin_r, fin_i, bb_r, bb_i, bbt_r, bbt_i,
             cct_r, cct_i, dskip, finals_only):
    sl = u3.shape[0]
    nch = sl // SSM_LK

    def body(u_ref, g_ref, xsr_ref, xsi_ref, abr_ref, abi_ref, pwr_ref, pwi_ref,
             finr_ref, fini_ref, bbr_ref, bbi_ref, btr_ref, bti_ref, ctr_ref, cti_ref, d_ref, *rest):
        if finals_only:
            lfr_ref, lfi_ref, gp, lam_r, lam_i = rest[:5]
            l_r, l_i = rest[-2 * N_CB:-N_CB], rest[-N_CB:]
        else:
            (du_ref, dar_ref, dai_ref, dbr_ref, dbi_ref, dcr_ref, dci_ref, dd_ref,
             gp, up, yp, lam_r, lam_i, x_r, x_i, sar, sai, sdd) = rest[:18]
            l_r, l_i = rest[18:18 + N_CB], rest[18 + N_CB:18 + 2 * N_CB]
            xx_r, xx_i = rest[18 + 2 * N_CB:18 + 3 * N_CB], rest[18 + 3 * N_CB:]
        t = pl.program_id(0)

        @pl.when(t == 0)
        def _():
            _seg_init(finr_ref, fini_ref, pwr_ref, pwi_ref, lam_r, lam_i, True)
            if not finals_only:
                sar[...] = jnp.zeros_like(sar)
                sai[...] = jnp.zeros_like(sai)
                sdd[...] = jnp.zeros_like(sdd)
                dbr_ref[...] = jnp.zeros_like(dbr_ref)
                dbi_ref[...] = jnp.zeros_like(dbi_ref)
                dcr_ref[...] = jnp.zeros_like(dcr_ref)
                dci_ref[...] = jnp.zeros_like(dci_ref)

        _permute_in(g_ref, gp)
        if not finals_only:
            _permute_in(u_ref, up)
            x_r[...] = xsr_ref[0]
            x_i[...] = xsi_ref[0]

        def drive(c):
            lhs = gp[c].astype(BF16)
            l_r[c][...] = _dot(lhs, ctr_ref[c])
            l_i[c][...] = -_dot(lhs, cti_ref[c])
            if not finals_only:
                cols = pl.ds(c * CB_STATES, CB_STATES)
                xx_r[c][pl.ds(0, N_SEG), :] = x_r[:, cols]
                xx_i[c][pl.ds(0, N_SEG), :] = x_i[:, cols]
                ub = up[c].astype(BF16)
                xx_r[c][pl.ds(N_SEG, ROWS), :] = _dot(ub, bbr_ref[c])
                xx_i[c][pl.ds(N_SEG, ROWS), :] = _dot(ub, bbi_ref[c])

        def collect(c):
            lrb = l_r[c][...].astype(BF16)
            lib = l_i[c][...].astype(BF16)
            ub = up[c].astype(BF16)
            gb = gp[c].astype(BF16)
            dbr_ref[c] += _dot_tn(lrb, ub)
            dbi_ref[c] += _dot_tn(lib, ub)
            dcr_ref[c] += _dot_tn(gb, xx_r[c][pl.ds(N_SEG, ROWS), :].astype(BF16))
            dci_ref[c] += -_dot_tn(gb, xx_i[c][pl.ds(N_SEG, ROWS), :].astype(BF16))
            yp[c] = _dot(lrb, btr_ref[c]) + _dot(lib, bti_ref[c]) + d_ref[:, pl.ds(c * LANES, LANES)] * gp[c]
            prod = gp[c] * up[c]
            sdd[:, pl.ds(c * LANES, LANES)] += jnp.sum(prod.reshape(SSM_LK, N_SEG, LANES), axis=0)

        drive(0)
        for c in range(N_CB):
            if c + 1 < N_CB:
                drive(c + 1)
            if finals_only:
                _scan_block(abr_ref, _Neg(abi_ref), c, l_r[c], l_i[c], 0, lam_r, lam_i, True)
            else:
                if c >= 1:
                    collect(c - 1)
                _scan_block(abr_ref, abi_ref, c, xx_r[c], xx_i[c], N_SEG, x_r, x_i, False)
                _scan_block(abr_ref, _Neg(abi_ref), c, l_r[c], l_i[c], 0, lam_r, lam_i, True,
                            acc=(xx_r[c], xx_i[c], sar, sai))
        if finals_only:
            @pl.when(t == nch - 1)
            def _():
                lfr_ref[...] = lam_r[...]
                lfi_ref[...] = lam_i[...]
        else:
            collect(N_CB - 1)
            _permute_out(yp, du_ref)

            @pl.when(t == nch - 1)
            def _():
                dar_ref[...] = jnp.sum(sar[...], axis=0, keepdims=True)
                dai_ref[...] = jnp.sum(sai[...], axis=0, keepdims=True)
                dd_ref[...] = jnp.sum(sdd[...], axis=0, keepdims=True)

    ublk = pl.BlockSpec((SSM_LK, N_SEG, SSM_W), lambda t: (nch - 1 - t, 0, 0))
    st = pl.BlockSpec((1, N_SEG, N_STATE), lambda t: (nch - 1 - t, 0, 0))
    vec = _full((1, N_STATE))
    mat = _full((N_SEG, N_STATE))
    cs = _full((N_CB, LANES, CB_STATES))
    sc = _full((N_CB, CB_STATES, LANES))
    in_specs = [ublk, ublk, st, st, vec, vec, vec, vec, mat, mat, cs, cs, sc, sc, cs, cs, _full((1, SSM_W))]
    chunk = pltpu.VMEM((N_CB, ROWS, LANES), F32)
    blocks = [pltpu.VMEM((ROWS, CB_STATES), F32)] * (2 * N_CB)
    small = pltpu.VMEM((N_SEG, N_STATE), F32)
    if finals_only:
        out_specs, out_shape = [mat, mat], [_sds((N_SEG, N_STATE))] * 2
        scratch, name = [chunk, small, small] + blocks, "ssm_bwd_finals"
    else:
        out_specs = [ublk, vec, vec, sc, sc, cs, cs, _full((1, SSM_W))]
        out_shape = ([_sds(u3.shape), _sds((1, N_STATE)), _sds((1, N_STATE))]
                     + [_sds((N_CB, CB_STATES, LANES))] * 2 + [_sds((N_CB, LANES, CB_STATES))] * 2
                     + [_sds((1, SSM_W))])
        scratch = ([chunk, chunk, chunk, small, small, small, small, small, small, pltpu.VMEM((N_SEG, SSM_W), F32)]
                   + blocks + [pltpu.VMEM((ROWS + N_SEG, CB_STATES), F32)] * (2 * N_CB))
        name = "ssm_bwd"
    return pl.pallas_call(
        body, name=name, grid=(nch,), in_specs=in_specs, out_specs=out_specs, out_shape=out_shape,
        scratch_shapes=scratch, compiler_params=_cp(("arbitrary",)),
    )(u3, dy3, xst_r, xst_i, abr, abi, pw_r, pw_i, fin_r, fin_i, bb_r, bb_i, bbt_r, bbt_i, cct_r, cct_i, dskip)


def _row(tm, w):
    return pl.BlockSpec((tm, w), lambda i: (i, 0))


def _acc_rows(ref, rows, first):
    @pl.when(first)
    def _():
        ref[...] = jnp.zeros_like(ref)

    ref[...] += jnp.sum(rows, axis=0, keepdims=True)


def _fwd_mix(attn, y, x, glu_w, glu_b, ga, gs, w_out, g2, tm=512):
    s = x.shape[0]

    def body(a_ref, y_ref, x_ref, gw_ref, gb_ref, ga_ref, gs_ref, wo_ref, g2_ref, mix_ref, x2_ref, h_ref):
        a = a_ref[...]
        anb = ((a * _rms(a)) * ga_ref[...]).astype(BF16)
        z, _ = _gelu(y_ref[...])
        so = z * jax.nn.sigmoid(_dot(z.astype(BF16), gw_ref[...]) + gb_ref[...])
        snb = ((so * _rms(so)) * gs_ref[...]).astype(BF16)
        mix_ref[:, pl.ds(0, ATTN_W)] = anb
        mix_ref[:, pl.ds(ATTN_W, SSM_W)] = snb
        x2 = x_ref[...] + (_dot(anb, wo_ref[pl.ds(0, ATTN_W), :]) + _dot(snb, wo_ref[pl.ds(ATTN_W, SSM_W), :]))
        x2_ref[...] = x2
        h_ref[...] = ((x2 * _rms(x2)) * g2_ref[...]).astype(BF16)

    return pl.pallas_call(
        body, name="fwd_mix", grid=(s // tm,),
        in_specs=[_row(tm, ATTN_W), _row(tm, SSM_W), _row(tm, D_MODEL), _full((SSM_W, SSM_W)), _full((1, SSM_W)),
                  _full((1, ATTN_W)), _full((1, SSM_W)), _full((D_MODEL, D_MODEL)), _full((1, D_MODEL))],
        out_specs=[_row(tm, D_MODEL), _row(tm, D_MODEL), _row(tm, D_MODEL)],
        out_shape=[_sds((s, D_MODEL), BF16), _sds((s, D_MODEL)), _sds((s, D_MODEL), BF16)],
        compiler_params=_cp(("parallel",)),
    )(attn, y, x, glu_w, glu_b, ga, gs, w_out, g2)


def _mlp_up(h, w_up, tm=512, bn=1024):
    s = h.shape[0]

    def body(h_ref, w_ref, r_ref, hdn_ref):
        hv = h_ref[...]
        for j in range(D_FF // bn):
            cols = pl.ds(j * bn, bn)
            r = jnp.maximum(_dot(hv, w_ref[:, cols]), 0.0)
            r_ref[:, cols] = r.astype(BF16)
            hdn_ref[:, cols] = (r * r).astype(BF16)

    return pl.pallas_call(
        body, name="mlp_up", grid=(s // tm,),
        in_specs=[_row(tm, D_MODEL), _full((D_MODEL, D_FF))],
        out_specs=[_row(tm, D_FF), _row(tm, D_FF)], out_shape=[_sds((s, D_FF), BF16)] * 2,
        compiler_params=_cp(("parallel",)),
    )(h, w_up)


def _mlp_down_loss(hdn, w_down, x2, tgt, tm=512):
    s = x2.shape[0]

    def body(hdn_ref, w_ref, x2_ref, t_ref, dy_ref, dyb_ref, sse_ref):
        err = (x2_ref[...] + _dot(hdn_ref[...], w_ref[...])) - t_ref[...]
        dy = err * (1.0 / D_MODEL)
        dy_ref[...] = dy
        dyb_ref[...] = dy.astype(BF16)

        @pl.when(pl.program_id(0) == 0)
        def _():
            sse_ref[...] = jnp.zeros_like(sse_ref)

        sse_ref[...] += jnp.sum(jnp.sum(err * err, axis=0, keepdims=True), axis=1, keepdims=True)

    return pl.pallas_call(
        body, name="mlp_down_loss", grid=(s // tm,),
        in_specs=[_row(tm, D_FF), _full((D_FF, D_MODEL)), _row(tm, D_MODEL), _row(tm, D_MODEL)],
        out_specs=[_row(tm, D_MODEL), _row(tm, D_MODEL), _full((1, 1))],
        out_shape=[_sds((s, D_MODEL)), _sds((s, D_MODEL), BF16), _sds((1, 1))],
        compiler_params=_cp(("arbitrary",)),
    )(hdn, w_down, x2, tgt)


def _mlp_down_bwd(dyb, w_down_t, r, tm=512, bn=1024):
    s = dyb.shape[0]

    def body(dy_ref, w_ref, r_ref, dup_ref):
        dyv = dy_ref[...]
        for j in range(D_FF // bn):
            cols = pl.ds(j * bn, bn)
            dup_ref[:, cols] = (_dot(dyv, w_ref[:, cols]) * (2.0 * r_ref[:, cols].astype(F32))).astype(BF16)

    return pl.pallas_call(
        body, name="mlp_down_bwd", grid=(s // tm,),
        in_specs=[_row(tm, D_MODEL), _full((D_MODEL, D_FF)), _row(tm, D_FF)],
        out_specs=_row(tm, D_FF), out_shape=_sds((s, D_FF), BF16),
        compiler_params=_cp(("parallel",)),
    )(dyb, w_down_t, r)


def _mlp_up_bwd(dup, w_up_t, x2, g2, dy, tm=512):
    s = x2.shape[0]

    def body(dup_ref, w_ref, x2_ref, g2_ref, dy_ref, dx2_ref, dx2b_ref, dg_ref):
        dx, dg_rows = _rms_bwd(_dot(dup_ref[...], w_ref[...]), x2_ref[...], g2_ref[...], D_MODEL)
        dx2 = dy_ref[...] + dx
        dx2_ref[...] = dx2
        dx2b_ref[...] = dx2.astype(BF16)
        _acc_rows(dg_ref, dg_rows, pl.program_id(0) == 0)

    return pl.pallas_call(
        body, name="mlp_up_bwd", grid=(s // tm,),
        in_specs=[_row(tm, D_FF), _full((D_FF, D_MODEL)), _row(tm, D_MODEL), _full((1, D_MODEL)), _row(tm, D_MODEL)],
        out_specs=[_row(tm, D_MODEL), _row(tm, D_MODEL), _full((1, D_MODEL))],
        out_shape=[_sds((s, D_MODEL)), _sds((s, D_MODEL), BF16), _sds((1, D_MODEL))],
        compiler_params=_cp(("arbitrary",)),
    )(dup, w_up_t, x2, g2, dy)


def _mix_bwd(dx2b, w_out_t, attn, y, glu_w, glu_b, glu_w_t, ga, gs, tm=512):
    s = attn.shape[0]

    def body(dx2_ref, wot_ref, a_ref, y_ref, gw_ref, gb_ref, gwt_ref, ga_ref, gs_ref,
             da_ref, dys_ref, z_ref, dpre_ref, dga_ref, dgs_ref, dgb_ref):
        first = pl.program_id(0) == 0
        dmix = _dot(dx2_ref[...], wot_ref[...])
        da, dga_rows = _rms_bwd(dmix[:, :ATTN_W], a_ref[...], ga_ref[...], ATTN_W)
        da_ref[...] = da
        yv = y_ref[...]
        z, t = _gelu(yv)
        gate = jax.nn.sigmoid(_dot(z.astype(BF16), gw_ref[...]) + gb_ref[...])
        dso, dgs_rows = _rms_bwd(dmix[:, ATTN_W:], z * gate, gs_ref[...], SSM_W)
        dpre = dso * z * gate * (1.0 - gate)
        dpre_b = dpre.astype(BF16)
        dz = dso * gate + _dot(dpre_b, gwt_ref[...])
        dgelu = 0.5 * (1.0 + t) + 0.5 * yv * (1.0 - t * t) * (GELU_C * (1.0 + 3.0 * 0.044715 * (yv * yv)))
        dys_ref[...] = dz * dgelu
        z_ref[...] = z.astype(BF16)
        dpre_ref[...] = dpre_b
        _acc_rows(dga_ref, dga_rows, first)
        _acc_rows(dgs_ref, dgs_rows, first)
        _acc_rows(dgb_ref, dpre, first)

    vec = _full((1, SSM_W))
    return pl.pallas_call(
        body, name="mix_bwd", grid=(s // tm,),
        in_specs=[_row(tm, D_MODEL), _full((D_MODEL, D_MODEL)), _row(tm, ATTN_W), _row(tm, SSM_W),
                  _full((SSM_W, SSM_W)), vec, _full((SSM_W, SSM_W)), vec, vec],
        out_specs=[_row(tm, ATTN_W), _row(tm, SSM_W), _row(tm, SSM_W), _row(tm, SSM_W), vec, vec, vec],
        out_shape=[_sds((s, ATTN_W)), _sds((s, SSM_W)), _sds((s, SSM_W), BF16), _sds((s, SSM_W), BF16),
                   _sds((1, ATTN_W)), _sds((1, SSM_W)), _sds((1, SSM_W))],
        compiler_params=_cp(("arbitrary",)),
    )(dx2b, w_out_t, attn, y, glu_w, glu_b, glu_w_t, ga, gs)


def _qk_bwd(dqn, dkn, qk, dv, du, gq, gk, gmat, tm=512):
    s = qk.shape[0]

    def body(dq_ref, dk_ref, qk_ref, dv_ref, du_ref, gq_ref, gk_ref, gm_ref, dp_ref, dgq_ref, dgk_ref):
        first = pl.program_id(0) == 0
        gm = gm_ref[...]
        for idx, (d_ref, g_ref, dg_ref) in enumerate(((dq_ref, gq_ref, dgq_ref), (dk_ref, gk_ref, dgk_ref))):
            xv = qk_ref[:, pl.ds(idx * ATTN_W, ATTN_W)]
            dyv = d_ref[...]
            r = lax.rsqrt(_group_mean(xv * xv, gm) + EPS)
            gdy = dyv * g_ref[...]
            dx = r * gdy - xv * (r * r * r) * _group_mean(gdy * xv, gm)
            dp_ref[:, pl.ds(idx * ATTN_W, ATTN_W)] = dx.astype(BF16)
            _acc_rows(dg_ref, dyv * (xv * r), first)
        dp_ref[:, pl.ds(2 * ATTN_W, ATTN_W)] = dv_ref[...].astype(BF16)
        dp_ref[:, pl.ds(3 * ATTN_W, SSM_W)] = du_ref[...].astype(BF16)

    vec = _full((1, ATTN_W))
    return pl.pallas_call(
        body, name="qk_bwd", grid=(s // tm,),
        in_specs=[_row(tm, ATTN_W), _row(tm, ATTN_W), _row(tm, 2 * ATTN_W), _row(tm, ATTN_W), _row(tm, SSM_W),
                  vec, vec, _full((ATTN_W, ATTN_W))],
        out_specs=[_row(tm, 4 * ATTN_W), vec, vec],
        out_shape=[_sds((s, 4 * ATTN_W), BF16), _sds((1, ATTN_W)), _sds((1, ATTN_W))],
        compiler_params=_cp(("arbitrary",)),
    )(dqn, dkn, qk, dv, du, gq, gk, gmat)


def _in_bwd(dproj, w_in_t, x, g1, dx2, ex=None, tm=512):
    s = x.shape[0]
    steps = s // tm

    def body(*refs):
        (dp_ref, w_ref, x_ref, g1_ref, dx2_ref), (gx_ref, dg_ref), _, hx = _carry(ex, refs, 5, 2)
        if ex is not None:
            @pl.when(pl.program_id(0) == 0)
            def _():
                ex.start(*hx)

        dx, dg_rows = _rms_bwd(_dot(dp_ref[...], w_ref[...]), x_ref[...], g1_ref[...], D_MODEL)
        gx_ref[...] = dx2_ref[...] + dx
        _acc_rows(dg_ref, dg_rows, pl.program_id(0) == 0)
        if ex is not None:
            @pl.when(pl.program_id(0) == steps - 1)
            def _():
                ex.wait(*hx)

    hosted = ex is not None
    return pl.pallas_call(
        body, name="in_bwd", grid=(steps,),
        in_specs=[_row(tm, 4 * ATTN_W), _full((4 * ATTN_W, D_MODEL)), _row(tm, D_MODEL), _full((1, D_MODEL)),
                  _row(tm, D_MODEL)] + (ex.specs() if hosted else []),
        out_specs=[_row(tm, D_MODEL), _full((1, D_MODEL))] + (ex.specs() if hosted else []),
        out_shape=[_sds((s, D_MODEL)), _sds((1, D_MODEL))] + (ex.out_shape() if hosted else []),
        scratch_shapes=ex.scratch() if hosted else [],
        compiler_params=_cp(("arbitrary",)),
    )(dproj, w_in_t, x, g1, dx2, *(ex.srcs if hosted else []))


def _mm_tn(a, b, name, ts=1024):
    s, k = a.shape
    n = b.shape[1]
    bk, bn = min(k, 1024), min(n, 1024)

    def body(a_ref, b_ref, o_ref):
        @pl.when(pl.program_id(2) == 0)
        def _():
            o_ref[...] = jnp.zeros_like(o_ref)

        o_ref[...] += _dot_tn(a_ref[...], b_ref[...])

    return pl.pallas_call(
        body, name=name, grid=(k // bk, n // bn, s // ts),
        in_specs=[pl.BlockSpec((ts, bk), lambda i, j, t: (t, i)), pl.BlockSpec((ts, bn), lambda i, j, t: (t, j))],
        out_specs=pl.BlockSpec((bk, bn), lambda i, j, t: (i, j)), out_shape=_sds((k, n)),
        compiler_params=_cp(("parallel", "parallel", "arbitrary")),
    )(a, b)


def _peer(k):
    x, y, c = lax.axis_index("x"), lax.axis_index("y"), lax.axis_index("c")
    px = 1 - x if k & 4 else x
    py = 1 - y if k & 2 else y
    pc = 1 - c if k & 1 else c
    return (px, py, pc), 4 * px + 2 * py + pc


def _gather_rows(x_shard):
    m_per, n = x_shard.shape

    def body(x_ref, out_ref, send_sems, recv_sems, local_sem):
        x, y, c = lax.axis_index("x"), lax.axis_index("y"), lax.axis_index("c")
        me, sibling = (x, y, c), (x, y, 1 - c)
        chips = [(1 - x, y), (x, 1 - y), (1 - x, 1 - y)]

        def rows(px, py, pc):
            return out_ref.at[pl.ds((4 * px + 2 * py + pc) * m_per, m_per), :]

        def copy(k, block, to, src=None):
            return pltpu.make_async_remote_copy(
                src_ref=rows(*block) if src is None else src, dst_ref=rows(*block),
                send_sem=send_sems.at[k], recv_sem=recv_sems.at[k], device_id=to, device_id_type=MESH)

        mine = pltpu.make_async_copy(x_ref, rows(*me), local_sem)
        mine.start()
        first = [copy(0, me, sibling, src=x_ref)]
        first += [copy(1 + j, me, (*chip, c), src=x_ref) for j, chip in enumerate(chips)]
        for cp in first:
            cp.start()
        passed = [copy(4 + j, (*chip, c), sibling) for j, chip in enumerate(chips)]
        for j, chip in enumerate(chips):
            copy(1 + j, (*chip, c), me).wait_recv()
            passed[j].start()
        copy(0, sibling, me).wait_recv()
        for j, chip in enumerate(chips):
            copy(4 + j, (*chip, 1 - c), me).wait_recv()
        for cp in first + passed:
            cp.wait_send()
        mine.wait()

    return pl.pallas_call(
        body, name="gather_weights", out_shape=_sds((N_DEV * m_per, n), x_shard.dtype),
        in_specs=[pl.BlockSpec(memory_space=pltpu.VMEM)], out_specs=pl.BlockSpec(memory_space=pltpu.VMEM),
        scratch_shapes=[pltpu.SemaphoreType.DMA((7,)), pltpu.SemaphoreType.DMA((7,)), pltpu.SemaphoreType.DMA],
        compiler_params=pltpu.CompilerParams(vmem_limit_bytes=VMEM_LIMIT),
    )(x_shard)


class _Exchange:
    def __init__(self, srcs, whole):
        self.srcs, self.whole, self.n = list(srcs), list(whole), len(srcs)
        self.rows = [a.shape[0] if w else a.shape[0] // N_DEV for a, w in zip(self.srcs, self.whole)]

    def specs(self):
        return [pl.BlockSpec(memory_space=pl.ANY)] * self.n

    def out_shape(self):
        return [_sds((N_DEV, r, a.shape[1]), a.dtype) for r, a in zip(self.rows, self.srcs)]

    def scratch(self):
        return [pltpu.SemaphoreType.DMA((self.n * 7,)), pltpu.SemaphoreType.DMA((self.n * 7,)),
                pltpu.SemaphoreType.DMA((self.n,))]

    def _copies(self, ins, outs, sems):
        send_sems, recv_sems, local_sems = sems
        _, me = _peer(0)
        for w in range(self.n):
            for k in range(N_DEV):
                peer, pidx = _peer(k)
                src = ins[w] if self.whole[w] else ins[w].at[pl.ds(pidx * self.rows[w], self.rows[w]), :]
                if k == 0:
                    yield k, pltpu.make_async_copy(src, outs[w].at[me], local_sems.at[w]), None
                else:
                    sem = w * 7 + k - 1
                    out = pltpu.make_async_remote_copy(src_ref=src, dst_ref=outs[w].at[me], send_sem=send_sems.at[sem],
                                                       recv_sem=recv_sems.at[sem], device_id=peer, device_id_type=MESH)
                    back = pltpu.make_async_remote_copy(src_ref=src, dst_ref=outs[w].at[pidx], send_sem=send_sems.at[sem],
                                                        recv_sem=recv_sems.at[sem], device_id=peer, device_id_type=MESH)
                    yield k, out, back

    def start(self, ins, outs, sems):
        for _, out, _ in self._copies(ins, outs, sems):
            out.start()

    def wait(self, ins, outs, sems):
        for k, out, back in self._copies(ins, outs, sems):
            if k == 0:
                out.wait()
            else:
                back.wait_recv()
                out.wait_send()


def _carry(ex, refs, n_in, n_out):
    nh = ex.n if ex is not None else 0
    ins, hin = refs[:n_in], refs[n_in:n_in + nh]
    outs = refs[n_in + nh:n_in + nh + n_out]
    hout = refs[n_in + nh + n_out:n_in + 2 * nh + n_out]
    rest = refs[n_in + 2 * nh + n_out:]
    if ex is None:
        return ins, outs, rest, None
    return ins, outs, rest[:-3], (hin, hout, rest[-3:])


def _exchange_now(srcs, whole, name):
    ex = _Exchange(srcs, whole)

    def body(*refs):
        _, _, _, (hin, hout, sems) = _carry(ex, refs, 0, 0)
        ex.start(hin, hout, sems)
        ex.wait(hin, hout, sems)

    return pl.pallas_call(body, name=name, out_shape=ex.out_shape(), in_specs=ex.specs(), out_specs=ex.specs(),
                          scratch_shapes=ex.scratch())(*srcs)


def _adamw(w, m, v, gparts, name):
    r, c = w.shape
    tr = r if r * c <= 256 * 1024 else 128 * 1024 // c

    def body(w_ref, m_ref, v_ref, g_ref, go_ref, d_ref, mo_ref, vo_ref):
        g = g_ref[0].astype(F32)
        for i in range(1, N_DEV):
            g = g + g_ref[i].astype(F32)
        nm = ADAM_B1 * m_ref[...] + (1.0 - ADAM_B1) * g
        nv = ADAM_B2 * v_ref[...] + (1.0 - ADAM_B2) * (g * g)
        m_hat = nm / (1.0 - ADAM_B1 ** ADAM_STEP)
        v_hat = nv / (1.0 - ADAM_B2 ** ADAM_STEP)
        go_ref[...] = g
        d_ref[...] = -ADAM_LR * (m_hat / (jnp.sqrt(v_hat) + ADAM_EPS) + ADAM_WD * w_ref[...])
        mo_ref[...] = nm
        vo_ref[...] = nv

    blk = pl.BlockSpec((tr, c), lambda i: (i, 0))
    return pl.pallas_call(
        body, name=name, grid=(r // tr,),
        in_specs=[blk, blk, blk, pl.BlockSpec((N_DEV, tr, c), lambda i: (0, i, 0))],
        out_specs=[blk] * 4, out_shape=[_sds((r, c))] * 4,
        compiler_params=_cp(("parallel",)),
    )(w, m, v, gparts)


def _block_diag(a, states_first):
    a4 = a.reshape(N_CB, 8, SSM_GROUP, SSM_STATE)
    eye = jnp.eye(8, dtype=a.dtype)
    if states_first:
        return jnp.einsum("bgcp,gh->bgphc", a4, eye).reshape(N_CB, CB_STATES, LANES)
    return jnp.einsum("bgcp,gh->bgchp", a4, eye).reshape(N_CB, LANES, CB_STATES)


def _block_diag_of(full, states_first):
    if states_first:
        picked = jnp.einsum("bgphc,gh->bgcp", full.reshape(N_CB, 8, SSM_STATE, 8, SSM_GROUP), jnp.eye(8, dtype=full.dtype))
    else:
        picked = jnp.einsum("bgchp,gh->bgcp", full.reshape(N_CB, 8, SSM_GROUP, 8, SSM_STATE), jnp.eye(8, dtype=full.dtype))
    return picked.reshape(SSM_GROUPS, SSM_GROUP, SSM_STATE)


SMALL_EARLY = ("ssm_a_re", "ssm_a_im", "ssm_log_dt", "ssm_b_re", "ssm_b_im", "ssm_c_re", "ssm_c_im", "ssm_d", "glu_b",
               "attn_out_norm_g", "ssm_out_norm_g", "norm2_g")
SMALL_MID = ("q_norm_g", "k_norm_g")
SMALL_LATE = ("norm1_g",)
SMALL = SMALL_EARLY + SMALL_MID + SMALL_LATE


def _pack_small(arrs):
    parts = []
    for a in arrs:
        flat = a.reshape(-1)
        rows = -(-flat.shape[0] // (8 * LANES)) * 8
        parts.append(jnp.pad(flat, (0, rows * LANES - flat.shape[0])).reshape(rows, LANES))
    return jnp.concatenate(parts, axis=0)


def _unpack_small(packed, shapes):
    out, r0 = [], 0
    for shp in shapes:
        size = math.prod(shp)
        rows = -(-size // (8 * LANES)) * 8
        out.append(packed[r0:r0 + rows].reshape(-1)[:size].reshape(shp))
        r0 += rows
    return out


def kernel(x, norm1_g, w_in, q_norm_g, k_norm_g, ssm_a_re, ssm_a_im, ssm_log_dt, ssm_b_re, ssm_b_im, ssm_c_re, ssm_c_im, ssm_d, glu_w, glu_b, attn_out_norm_g, ssm_out_norm_g, w_out, norm2_g, w_mlp_up, w_mlp_down, loss_target, m_norm1_g, m_w_in, m_q_norm_g, m_k_norm_g, m_ssm_a_re, m_ssm_a_im, m_ssm_log_dt, m_ssm_b_re, m_ssm_b_im, m_ssm_c_re, m_ssm_c_im, m_ssm_d, m_glu_w, m_glu_b, m_attn_out_norm_g, m_ssm_out_norm_g, m_w_out, m_norm2_g, m_w_mlp_up, m_w_mlp_down, v_norm1_g, v_w_in, v_q_norm_g, v_k_norm_g, v_ssm_a_re, v_ssm_a_im, v_ssm_log_dt, v_ssm_b_re, v_ssm_b_im, v_ssm_c_re, v_ssm_c_im, v_ssm_d, v_glu_w, v_glu_b, v_attn_out_norm_g, v_ssm_out_norm_g, v_w_out, v_norm2_g, v_w_mlp_up, v_w_mlp_down):
    weights = dict(norm1_g=norm1_g, w_in=w_in, q_norm_g=q_norm_g, k_norm_g=k_norm_g, ssm_a_re=ssm_a_re,
                   ssm_a_im=ssm_a_im, ssm_log_dt=ssm_log_dt, ssm_b_re=ssm_b_re, ssm_b_im=ssm_b_im,
                   ssm_c_re=ssm_c_re, ssm_c_im=ssm_c_im, ssm_d=ssm_d, glu_w=glu_w, glu_b=glu_b,
                   attn_out_norm_g=attn_out_norm_g, ssm_out_norm_g=ssm_out_norm_g, w_out=w_out, norm2_g=norm2_g,
                   w_mlp_up=w_mlp_up, w_mlp_down=w_mlp_down)
    mom_m = dict(norm1_g=m_norm1_g, w_in=m_w_in, q_norm_g=m_q_norm_g, k_norm_g=m_k_norm_g, ssm_a_re=m_ssm_a_re,
                 ssm_a_im=m_ssm_a_im, ssm_log_dt=m_ssm_log_dt, ssm_b_re=m_ssm_b_re, ssm_b_im=m_ssm_b_im,
                 ssm_c_re=m_ssm_c_re, ssm_c_im=m_ssm_c_im, ssm_d=m_ssm_d, glu_w=m_glu_w, glu_b=m_glu_b,
                 attn_out_norm_g=m_attn_out_norm_g, ssm_out_norm_g=m_ssm_out_norm_g, w_out=m_w_out,
                 norm2_g=m_norm2_g, w_mlp_up=m_w_mlp_up, w_mlp_down=m_w_mlp_down)
    mom_v = dict(norm1_g=v_norm1_g, w_in=v_w_in, q_norm_g=v_q_norm_g, k_norm_g=v_k_norm_g, ssm_a_re=v_ssm_a_re,
                 ssm_a_im=v_ssm_a_im, ssm_log_dt=v_ssm_log_dt, ssm_b_re=v_ssm_b_re, ssm_b_im=v_ssm_b_im,
                 ssm_c_re=v_ssm_c_re, ssm_c_im=v_ssm_c_im, ssm_d=v_ssm_d, glu_w=v_glu_w, glu_b=v_glu_b,
                 attn_out_norm_g=v_attn_out_norm_g, ssm_out_norm_g=v_ssm_out_norm_g, w_out=v_w_out,
                 norm2_g=v_norm2_g, w_mlp_up=v_w_mlp_up, w_mlp_down=v_w_mlp_down)
    order = list(weights)

    xs, tgt = x[0], loss_target[0]
    s = xs.shape[0]
    assert s % ATTN_CHUNK == 0 and (s // N_SEG) % SSM_LK == 0
    seg_len = s // N_SEG
    n_sq = seg_len.bit_length() - 1
    assert 1 << n_sq == seg_len

    w_in_t = _gather_rows(w_in[0].T.astype(BF16))
    w_in_full = w_in_t.T
    later = _Exchange([glu_w[0].astype(BF16), w_out[0].astype(BF16), w_mlp_up[0].T.astype(BF16),
                       w_mlp_down[0].astype(BF16)], [True] * 4)

    gq = jnp.tile(q_norm_g[0], ATTN_W // HEAD_DIM)[None]
    gk = jnp.tile(k_norm_g[0], ATTN_W // HEAD_DIM)[None]
    lane = jnp.arange(ATTN_W) // HEAD_DIM
    gmat = jnp.where(lane[:, None] == lane[None, :], 1.0 / HEAD_DIM, 0.0).astype(BF16)
    a_re3 = ssm_a_re[0][:, None, :]
    a_im3 = ssm_a_im[0][:, None, :]
    ldt3 = ssm_log_dt[0][:, None, None]
    b_re_t = jnp.swapaxes(ssm_b_re[0], 1, 2)
    b_im_t = jnp.swapaxes(ssm_b_im[0], 1, 2)
    c_re, c_im = ssm_c_re[0], ssm_c_im[0]
    dskip = ssm_d[0].reshape(1, SSM_W)

    qk, qn, kn, v, u, xn = _fwd_proj(xs, norm1_g, w_in_full, gq, gk, gmat)
    attn, lse, glu_g, w_out_g, w_up_g, w_down_g = _attn_fwd(qn, kn, v, later)
    glu_full = glu_g.reshape(SSM_W, SSM_W)
    w_out_full = w_out_g.reshape(D_MODEL, D_MODEL)
    w_up_t = w_up_g.reshape(D_FF, D_MODEL)
    w_down_full = w_down_g.reshape(D_FF, D_MODEL)
    w_up_full, w_out_t, w_down_t, glu_t = w_up_t.T, w_out_full.T, w_down_full.T, glu_full.T

    abr3, abi3, bbr, bbi = _ssm_discretize(a_re3, a_im3, ldt3, b_re_t, b_im_t)
    abr, abi = abr3.reshape(1, N_STATE), abi3.reshape(1, N_STATE)
    pw_r, pw_i = _ssm_power(abr, abi, n_sq)
    bb_r, bb_i = _block_diag(bbr, False).astype(BF16), _block_diag(bbi, False).astype(BF16)
    bbt_r, bbt_i = _block_diag(bbr, True).astype(BF16), _block_diag(bbi, True).astype(BF16)
    cc_r, cc_i = _block_diag(c_re, True).astype(BF16), _block_diag(c_im, True).astype(BF16)
    cct_r, cct_i = _block_diag(c_re, False).astype(BF16), _block_diag(c_im, False).astype(BF16)
    seg_major = lambda a: jnp.swapaxes(a.reshape(N_SEG, seg_len, SSM_W), 0, 1)
    seg_minor = lambda a: jnp.swapaxes(a, 0, 1).reshape(s, SSM_W)
    u3 = seg_major(u)
    zero_fin = jnp.zeros((N_SEG, N_STATE), F32)
    ssm_args = (abr, abi, pw_r, pw_i)
    xf_r, xf_i = _ssm_fwd(u3, *ssm_args, zero_fin, zero_fin, bb_r, bb_i, cc_r, cc_i, dskip, True)
    y3, xst_r, xst_i = _ssm_fwd(u3, *ssm_args, xf_r, xf_i, bb_r, bb_i, cc_r, cc_i, dskip, False)
    y = seg_minor(y3)

    mix, x2, h = _fwd_mix(attn, y, xs, glu_full, glu_b, attn_out_norm_g, ssm_out_norm_g, w_out_full, norm2_g)
    r_act, hdn = _mlp_up(h, w_up_full)
    dy, dyb, sse = _mlp_down_loss(hdn, w_down_full, x2, tgt)
    loss = lax.psum(0.5 * sse[0, 0] / D_MODEL, ("x", "y", "c"))

    dup = _mlp_down_bwd(dyb, w_down_t, r_act)
    g_w_down = _mm_tn(hdn, dyb, "grad_w_down")
    dx2, dx2b, g_norm2 = _mlp_up_bwd(dup, w_up_t, x2, norm2_g, dy)
    g_w_up_t = _mm_tn(dup, h, "grad_w_up_t")
    dattn, dys, z_b, dpre_b, g_ga, g_gs, g_glu_b = _mix_bwd(dx2b, w_out_t, attn, y, glu_full, glu_b, glu_t,
                                                             attn_out_norm_g, ssm_out_norm_g)
    g_w_out = _mm_tn(mix, dx2b, "grad_w_out")
    g_glu_w = _mm_tn(z_b, dpre_b, "grad_glu_w")
    dy3 = seg_major(dys)
    bwd_args = (u3, dy3, xst_r, xst_i, abr, abi, pw_r, pw_i)
    lf_r, lf_i = _ssm_bwd(*bwd_args, zero_fin, zero_fin, bb_r, bb_i, bbt_r, bbt_i, cct_r, cct_i, dskip, True)
    du3, dab_r, dab_i, dbb_r, dbb_i, dcc_r, dcc_i, g_d = _ssm_bwd(*bwd_args, lf_r, lf_i, bb_r, bb_i, bbt_r, bbt_i,
                                                                 cct_r, cct_i, dskip, False)
    g_a_re3, g_a_im3, g_ldt3, g_b_re_t, g_b_im_t = _ssm_discretize_bwd(
        a_re3, a_im3, ldt3, b_re_t, b_im_t, dab_r.reshape(a_re3.shape), dab_i.reshape(a_re3.shape),
        _block_diag_of(dbb_r, True), _block_diag_of(dbb_i, True))
    g_c_re, g_c_im = _block_diag_of(dcc_r, False), _block_diag_of(dcc_i, False)
    small_grads = dict(
        ssm_a_re=g_a_re3.reshape(ssm_a_re.shape), ssm_a_im=g_a_im3.reshape(ssm_a_im.shape),
        ssm_log_dt=g_ldt3.reshape(ssm_log_dt.shape), ssm_b_re=jnp.swapaxes(g_b_re_t, 1, 2)[None],
        ssm_b_im=jnp.swapaxes(g_b_im_t, 1, 2)[None], ssm_c_re=g_c_re[None], ssm_c_im=g_c_im[None],
        ssm_d=g_d.reshape(ssm_d.shape), glu_b=g_glu_b, attn_out_norm_g=g_ga, ssm_out_norm_g=g_gs, norm2_g=g_norm2)

    early = _Exchange([g.astype(BF16) for g in (g_glu_w, g_w_out, g_w_up_t, g_w_down)]
                      + [_pack_small([small_grads[n] for n in SMALL_EARLY])], [False] * 4 + [True])
    dqn, dkn, dv, p_glu, p_w_out, p_w_up, p_w_down, p_early = _attn_bwd(qn, kn, v, attn, lse, dattn, early)

    dproj, g_gq, g_gk = _qk_bwd(dqn, dkn, qk, dv, seg_minor(du3), gq, gk, gmat)
    g_w_in_t = _mm_tn(dproj, xn, "grad_w_in_t")
    small_grads["q_norm_g"] = g_gq.reshape(ATTN_W // HEAD_DIM, HEAD_DIM).sum(0)[None]
    small_grads["k_norm_g"] = g_gk.reshape(ATTN_W // HEAD_DIM, HEAD_DIM).sum(0)[None]
    mid = _Exchange([g_w_in_t.astype(BF16), _pack_small([small_grads[n] for n in SMALL_MID])], [False, True])
    grad_x, g_norm1, p_w_in, p_mid = _in_bwd(dproj, w_in_t, xs, norm1_g, dx2, mid)
    (p_late,) = _exchange_now([_pack_small([g_norm1])], [True], "exchange_norm1")

    transposed = {"w_in", "w_mlp_up"}
    res = {}
    for name, gp in (("w_in", p_w_in), ("glu_w", p_glu), ("w_out", p_w_out), ("w_mlp_up", p_w_up), ("w_mlp_down", p_w_down)):
        tr = (lambda a: a.T) if name in transposed else (lambda a: a)
        outs = _adamw(tr(weights[name][0]), tr(mom_m[name][0]), tr(mom_v[name][0]), gp, "adamw_" + name)
        res[name] = [tr(o)[None] for o in outs]
    outs = _adamw(_pack_small([weights[n] for n in SMALL]), _pack_small([mom_m[n] for n in SMALL]),
                  _pack_small([mom_v[n] for n in SMALL]), jnp.concatenate([p_early, p_mid, p_late], axis=1), "adamw_small")
    shapes = [weights[n].shape for n in SMALL]
    unpacked = [_unpack_small(o, shapes) for o in outs]
    for i, n in enumerate(SMALL):
        res[n] = [unpacked[j][i] for j in range(4)]

    return (loss, grad_x[None], *[res[n][0] for n in order], *[res[n][1] for n in order],
            *[res[n][2] for n in order], *[res[n][3] for n in order])
```

```python
import math

import jax
import jax.numpy as jnp
from jax import lax
from jax.experimental import pallas as pl
from jax.experimental.pallas import tpu as pltpu

F32 = jnp.float32
BF16 = jnp.bfloat16

D_MODEL = 1024
ATTN_W = 512
HEAD_DIM = 64
SSM_W = 512
SSM_GROUP = 16
SSM_GROUPS = 32
SSM_STATE = 64
N_STATE = SSM_GROUPS * SSM_STATE
D_FF = 4096
EPS = 1e-6
NEG_INF = -1e30
ATTN_CHUNK = 2048
ATTN_BLOCK = 128
DILATIONS = (1, 4, 16)
N_SEG = 8
SSM_LK = 64
N_DEV = 8
LANES = 128

ADAM_LR = 0.001
ADAM_B1 = 0.9
ADAM_B2 = 0.999
ADAM_EPS = 1e-08
ADAM_WD = 0.01
ADAM_STEP = 10

VMEM_LIMIT = 56 * 1024 * 1024
GELU_C = math.sqrt(2.0 / math.pi)
MESH = pl.DeviceIdType.MESH


def _cp(sem, vmem=VMEM_LIMIT):
    return pltpu.CompilerParams(dimension_semantics=sem, vmem_limit_bytes=vmem)


def _dot(a, b):
    return jnp.dot(a, b, preferred_element_type=F32)


def _dot_nt(a, b):
    return lax.dot_general(a, b, (((1,), (1,)), ((), ())), preferred_element_type=F32)


def _dot_tn(a, b):
    return lax.dot_general(a, b, (((0,), (0,)), ((), ())), preferred_element_type=F32)


def _group_mean(x2, gmat):
    hi = x2.astype(BF16)
    lo = (x2 - hi.astype(F32)).astype(BF16)
    return _dot(hi, gmat) + _dot(lo, gmat)


def _rms(x):
    return lax.rsqrt(jnp.mean(x * x, axis=-1, keepdims=True) + EPS)


def _rms_bwd(dy, x, g, n):
    r = _rms(x)
    gdy = dy * g
    dx = r * gdy - x * (r * r * r) * (jnp.sum(gdy * x, axis=-1, keepdims=True) / n)
    return dx, dy * (x * r)


def _gelu(y):
    t = jnp.tanh(GELU_C * (y + 0.044715 * (y * y * y)))
    return 0.5 * y * (1.0 + t), t


def _full(shape):
    nd = len(shape)
    return pl.BlockSpec(shape, lambda *_: (0,) * nd)


def _sds(shape, dtype=F32):
    return jax.ShapeDtypeStruct(shape, dtype)


def _fwd_proj(x, g1, w_in, gq, gk, gmat, tm=512):
    s = x.shape[0]

    def body(x_ref, g1_ref, w_ref, gq_ref, gk_ref, gm_ref, qk_ref, qn_ref, kn_ref, v_ref, u_ref, xn_ref):
        xv = x_ref[...]
        xnb = ((xv * _rms(xv)) * g1_ref[...]).astype(BF16)
        xn_ref[...] = xnb
        proj = _dot(xnb, w_ref[...])
        q = proj[:, :ATTN_W]
        k = proj[:, ATTN_W:2 * ATTN_W]
        qk_ref[...] = proj[:, :2 * ATTN_W]
        v_ref[...] = proj[:, 2 * ATTN_W:3 * ATTN_W]
        u_ref[...] = proj[:, 3 * ATTN_W:]
        gm = gm_ref[...]
        qn_ref[...] = (q * lax.rsqrt(_group_mean(q * q, gm) + EPS)) * gq_ref[...]
        kn_ref[...] = (k * lax.rsqrt(_group_mean(k * k, gm) + EPS)) * gk_ref[...]

    row = lambda w: pl.BlockSpec((tm, w), lambda i: (i, 0))
    return pl.pallas_call(
        body, name="fwd_proj", grid=(s // tm,),
        in_specs=[row(D_MODEL), _full((1, D_MODEL)), _full((D_MODEL, 4 * ATTN_W)), _full((1, ATTN_W)),
                  _full((1, ATTN_W)), _full((ATTN_W, ATTN_W))],
        out_specs=[row(2 * ATTN_W), row(ATTN_W), row(ATTN_W), row(ATTN_W), row(ATTN_W), row(D_MODEL)],
        out_shape=[_sds((s, 2 * ATTN_W)), _sds((s, ATTN_W)), _sds((s, ATTN_W)), _sds((s, ATTN_W)),
                   _sds((s, ATTN_W)), _sds((s, D_MODEL), BF16)],
        compiler_params=_cp(("parallel",)),
    )(x, g1, w_in, gq, gk, gmat)


def _attn_rows(t, d, nb):
    if d == 1:
        q0 = t * ATTN_BLOCK
        return (t, pl.ds(q0, ATTN_BLOCK), pl.ds(ATTN_CHUNK + q0, ATTN_BLOCK),
                pl.ds(ATTN_CHUNK - ATTN_BLOCK + q0, ATTN_BLOCK))
    r = t // nb
    b = t % nb
    return (b, pl.ds(ATTN_BLOCK * b * d + r, ATTN_BLOCK, stride=d),
            pl.ds(ATTN_CHUNK + ATTN_BLOCK * b * d + r, ATTN_BLOCK, stride=d),
            pl.ds(ATTN_CHUNK + ATTN_BLOCK * (b - 1) * d + r, ATTN_BLOCK, stride=d))


def _attn_masks():
    row = lax.broadcasted_iota(jnp.int32, (ATTN_BLOCK, LANES), 0)
    col = lax.broadcasted_iota(jnp.int32, (ATTN_BLOCK, LANES), 1)
    return row, col


NBLK = ATTN_CHUNK // ATTN_BLOCK


def _attn_bias(bias_s):
    row, col = _attn_masks()
    bias_s[:, pl.ds(0, LANES)] = jnp.where(col >= row, 0.0, NEG_INF)
    bias_s[:, pl.ds(LANES, LANES)] = jnp.where(col <= row, 0.0, NEG_INF)
    return col < HEAD_DIM


def _attn_fwd(qn, kn, v, ex=None, group=4):
    s = qn.shape[0]
    nch = s // ATTN_CHUNK
    scale = HEAD_DIM ** -0.5
    npat = len(DILATIONS)
    n_hp = ATTN_W // LANES

    def body(*refs):
        ((q_ref, kp_ref, kc_ref, vp_ref, vc_ref), (o_ref, lse_ref),
         (kk, vv, kt_s, vb_s, bias_s, m_s, a_s), hx) = _carry(ex, refs, 5, 2)
        i = pl.program_id(1)
        if ex is not None:
            @pl.when(jnp.logical_and(pl.program_id(0) == 0, i == 0))
            def _():
                ex.start(*hx)

        kk[pl.ds(0, ATTN_CHUNK), :] = kp_ref[...]
        kk[pl.ds(ATTN_CHUNK, ATTN_CHUNK), :] = kc_ref[...]
        vv[pl.ds(0, ATTN_CHUNK), :] = vp_ref[...]
        vv[pl.ds(ATTN_CHUNK, ATTN_CHUNK), :] = vc_ref[...]
        head0 = _attn_bias(bias_s)
        first_pen = jnp.where(i > 0, 0.0, NEG_INF)

        for p, d in enumerate(DILATIONS):
            nb = ATTN_CHUNK // (ATTN_BLOCK * d)

            def prep(t, d=d, nb=nb):
                _, _, crows, prows = _attn_rows(t, d, nb)
                kt_s[t, :, pl.ds(0, LANES)] = kk[prows, :].T.astype(BF16)
                kt_s[t, :, pl.ds(LANES, LANES)] = kk[crows, :].T.astype(BF16)
                vp = vv[prows, :]
                vc = vv[crows, :]
                vb_s[2 * t, pl.ds(0, ATTN_BLOCK), :] = jnp.where(head0, vp, 1.0).astype(BF16)
                vb_s[2 * t, pl.ds(ATTN_BLOCK, ATTN_BLOCK), :] = jnp.where(head0, vc, 1.0).astype(BF16)
                vb_s[2 * t + 1, pl.ds(0, ATTN_BLOCK), :] = jnp.where(head0, 1.0, vp).astype(BF16)
                vb_s[2 * t + 1, pl.ds(ATTN_BLOCK, ATTN_BLOCK), :] = jnp.where(head0, 1.0, vc).astype(BF16)

            def main(tg, p=p, d=d, nb=nb):
                st = []
                for g in range(group):
                    t = tg * group + g
                    b, qrows, _, _ = _attn_rows(t, d, nb)
                    q = q_ref[qrows, :] * scale
                    for h in range(2):
                        hm = head0 if h == 0 else jnp.logical_not(head0)
                        st.append(dict(t=t, b=b, qrows=qrows, sc=_dot(jnp.where(hm, q, 0.0).astype(BF16), kt_s[t])))
                for e in st:
                    sc = e["sc"] + bias_s[...]
                    s_p = sc[:, :LANES] + first_pen if e["b"] == 0 else sc[:, :LANES]
                    s_c = sc[:, LANES:]
                    m = jnp.max(jnp.maximum(s_p, s_c), axis=-1, keepdims=True)
                    e["eb"] = jnp.concatenate([jnp.exp(s_p - m), jnp.exp(s_c - m)], axis=1).astype(BF16)
                    e["m"] = jnp.broadcast_to(m, (ATTN_BLOCK, LANES))
                for g in range(group):
                    e0, e1 = st[2 * g], st[2 * g + 1]
                    t = e0["t"]
                    m_s[p, e0["qrows"], :] = jnp.where(head0, e0["m"], e1["m"])
                    a_s[2 * p, e0["qrows"], :] = _dot(e0["eb"], vb_s[2 * t])
                    a_s[2 * p + 1, e0["qrows"], :] = _dot(e1["eb"], vb_s[2 * t + 1])

            for g in range(group):
                prep(g)
            for tg in range(NBLK // group):
                if tg + 1 < NBLK // group:
                    for g in range(group):
                        prep((tg + 1) * group + g)
                main(tg)

        def merge(t, carry):
            rows = pl.ds(pl.multiple_of(t * ATTN_BLOCK, ATTN_BLOCK), ATTN_BLOCK)
            m_all = m_s[0, rows, :]
            for p in range(1, npat):
                m_all = jnp.maximum(m_all, m_s[p, rows, :])
            num = jnp.zeros((ATTN_BLOCK, LANES), F32)
            den = jnp.zeros((ATTN_BLOCK, LANES), F32)
            for p in range(npat):
                w = jnp.exp(m_s[p, rows, :] - m_all)
                a0, a1 = a_s[2 * p, rows, :], a_s[2 * p + 1, rows, :]
                num = num + jnp.where(head0, a0, a1) * w
                den = den + pltpu.roll(jnp.where(head0, a1, a0), HEAD_DIM, 1) * w
            o_ref[rows, :] = num / den
            lse_ref[rows, :] = m_all + jnp.log(den)
            return carry

        lax.fori_loop(0, NBLK, merge, 0, unroll=2)
        if ex is not None:
            @pl.when(jnp.logical_and(pl.program_id(0) == n_hp - 1, i == nch - 1))
            def _():
                ex.wait(*hx)

    cur = pl.BlockSpec((ATTN_CHUNK, LANES), lambda h, i: (i, h))
    prev = pl.BlockSpec((ATTN_CHUNK, LANES), lambda h, i: (jnp.maximum(i - 1, 0), h))
    hosted = ex is not None
    return pl.pallas_call(
        body, name="attn_fwd", grid=(n_hp, nch),
        in_specs=[cur, prev, cur, prev, cur] + (ex.specs() if hosted else []),
        out_specs=[cur, cur] + (ex.specs() if hosted else []),
        out_shape=[_sds((s, ATTN_W)), _sds((s, ATTN_W))] + (ex.out_shape() if hosted else []),
        scratch_shapes=[pltpu.VMEM((2 * ATTN_CHUNK, LANES), F32), pltpu.VMEM((2 * ATTN_CHUNK, LANES), F32),
                        pltpu.VMEM((NBLK, LANES, 2 * LANES), BF16), pltpu.VMEM((2 * NBLK, 2 * ATTN_BLOCK, LANES), BF16),
                        pltpu.VMEM((ATTN_BLOCK, 2 * LANES), F32),
                        pltpu.VMEM((npat, ATTN_CHUNK, LANES), F32), pltpu.VMEM((2 * npat, ATTN_CHUNK, LANES), F32)]
        + (ex.scratch() if hosted else []),
        compiler_params=_cp(("arbitrary", "arbitrary")),
    )(qn, kn, kn, v, v, *(ex.srcs if hosted else []))


def _attn_bwd(qn, kn, v, o, lse, do, ex=None, group=4):
    s = qn.shape[0]
    nch = s // ATTN_CHUNK
    scale = HEAD_DIM ** -0.5
    npat = len(DILATIONS)
    n_hp = ATTN_W // LANES

    def body(*refs):
        ((q_ref, kp_ref, kc_ref, vp_ref, vc_ref, o_ref, lse_ref, do_ref), (dq_ref, dk_ref, dv_ref),
         (kk, vv, dkk, dvv, kt_s, vt_s, kn_s, bias_s, dq_s, dl_s, dkb, dvb), hx) = _carry(ex, refs, 8, 3)
        step = pl.program_id(1)
        i = nch - 1 - step
        if ex is not None:
            @pl.when(jnp.logical_and(pl.program_id(0) == 0, step == 0))
            def _():
                ex.start(*hx)

        kk[pl.ds(0, ATTN_CHUNK), :] = kp_ref[...]
        kk[pl.ds(ATTN_CHUNK, ATTN_CHUNK), :] = kc_ref[...]
        vv[pl.ds(0, ATTN_CHUNK), :] = vp_ref[...]
        vv[pl.ds(ATTN_CHUNK, ATTN_CHUNK), :] = vc_ref[...]

        @pl.when(step == 0)
        def _():
            dkk[pl.ds(ATTN_CHUNK, ATTN_CHUNK), :] = jnp.zeros((ATTN_CHUNK, LANES), F32)
            dvv[pl.ds(ATTN_CHUNK, ATTN_CHUNK), :] = jnp.zeros((ATTN_CHUNK, LANES), F32)

        @pl.when(step > 0)
        def _():
            dkk[pl.ds(ATTN_CHUNK, ATTN_CHUNK), :] = dkk[pl.ds(0, ATTN_CHUNK), :]
            dvv[pl.ds(ATTN_CHUNK, ATTN_CHUNK), :] = dvv[pl.ds(0, ATTN_CHUNK), :]

        dkk[pl.ds(0, ATTN_CHUNK), :] = jnp.zeros((ATTN_CHUNK, LANES), F32)
        dvv[pl.ds(0, ATTN_CHUNK), :] = jnp.zeros((ATTN_CHUNK, LANES), F32)
        head0 = _attn_bias(bias_s)

        def delta(t, carry):
            rows = pl.ds(pl.multiple_of(t * ATTN_BLOCK, ATTN_BLOCK), ATTN_BLOCK)
            prod = do_ref[rows, :] * o_ref[rows, :]
            d0 = jnp.sum(jnp.where(head0, prod, 0.0), axis=-1, keepdims=True)
            d1 = jnp.sum(jnp.where(head0, 0.0, prod), axis=-1, keepdims=True)
            dl_s[rows, :] = jnp.where(head0, d0, d1)
            return carry

        lax.fori_loop(0, NBLK, delta, 0, unroll=2)

        first_pen = jnp.where(i > 0, 0.0, NEG_INF)

        for p, d in enumerate(DILATIONS):
            nb = ATTN_CHUNK // (ATTN_BLOCK * d)

            def prep(t, d=d, nb=nb):
                _, _, crows, prows = _attn_rows(t, d, nb)
                kp, kc = kk[prows, :], kk[crows, :]
                kt_s[t, :, pl.ds(0, LANES)] = kp.T.astype(BF16)
                kt_s[t, :, pl.ds(LANES, LANES)] = kc.T.astype(BF16)
                kn_s[t, pl.ds(0, ATTN_BLOCK), :] = (kp * scale).astype(BF16)
                kn_s[t, pl.ds(ATTN_BLOCK, ATTN_BLOCK), :] = (kc * scale).astype(BF16)
                vt_s[t, :, pl.ds(0, LANES)] = vv[prows, :].T.astype(BF16)
                vt_s[t, :, pl.ds(LANES, LANES)] = vv[crows, :].T.astype(BF16)

            def main(tg, p=p, d=d, nb=nb):
                st = []
                for g in range(group):
                    t = tg * group + g
                    b, qrows, _, _ = _attn_rows(t, d, nb)
                    q = q_ref[qrows, :] * scale
                    dout = do_ref[qrows, :]
                    lse_b = lse_ref[qrows, :]
                    dl_b = dl_s[qrows, :]
                    for h in range(2):
                        hm = head0 if h == 0 else jnp.logical_not(head0)
                        c0 = h * HEAD_DIM
                        qh = jnp.where(hm, q, 0.0).astype(BF16)
                        doh = jnp.where(hm, dout, 0.0).astype(BF16)
                        st.append(dict(t=t, b=b, qrows=qrows, qh=qh, doh=doh, lse=lse_b[:, c0:c0 + 1],
                                       dl=dl_b[:, c0:c0 + 1], sc=_dot(qh, kt_s[t]), dp=_dot(doh, vt_s[t])))
                for e in st:
                    sc = e["sc"] + bias_s[...]
                    if e["b"] == 0:
                        sc = jnp.concatenate([sc[:, :LANES] + first_pen, sc[:, LANES:]], axis=1)
                    pr = jnp.exp(sc - e["lse"])
                    e["ds"] = (pr * (e["dp"] - e["dl"])).astype(BF16)
                    e["pr"] = pr.astype(BF16)
                for g in range(group):
                    e0, e1 = st[2 * g], st[2 * g + 1]
                    t = e0["t"]
                    dq_s[p, e0["qrows"], :] = jnp.where(head0, _dot(e0["ds"], kn_s[t]), _dot(e1["ds"], kn_s[t]))
                    dkb[t] = _dot_tn(e0["ds"], e0["qh"]) + _dot_tn(e1["ds"], e1["qh"])
                    dvb[t] = _dot_tn(e0["pr"], e0["doh"]) + _dot_tn(e1["pr"], e1["doh"])

            def scatter(t, d=d, nb=nb):
                _, _, crows, prows = _attn_rows(t, d, nb)
                dkk[prows, :] = dkk[prows, :] + dkb[t, pl.ds(0, ATTN_BLOCK), :]
                dkk[crows, :] = dkk[crows, :] + dkb[t, pl.ds(ATTN_BLOCK, ATTN_BLOCK), :]
                dvv[prows, :] = dvv[prows, :] + dvb[t, pl.ds(0, ATTN_BLOCK), :]
                dvv[crows, :] = dvv[crows, :] + dvb[t, pl.ds(ATTN_BLOCK, ATTN_BLOCK), :]

            n_groups = NBLK // group
            for g in range(group):
                prep(g)
            for tg in range(n_groups):
                if tg + 1 < n_groups:
                    for g in range(group):
                        prep((tg + 1) * group + g)
                main(tg)
                if tg >= 1:
                    for g in range(group):
                        scatter((tg - 1) * group + g)
            for g in range(group):
                scatter((n_groups - 1) * group + g)

        def finish(t, carry):
            rows = pl.ds(pl.multiple_of(t * ATTN_BLOCK, ATTN_BLOCK), ATTN_BLOCK)
            acc = dq_s[0, rows, :]
            for p in range(1, npat):
                acc = acc + dq_s[p, rows, :]
            dq_ref[rows, :] = acc
            return carry

        lax.fori_loop(0, NBLK, finish, 0, unroll=2)
        dk_ref[...] = dkk[pl.ds(ATTN_CHUNK, ATTN_CHUNK), :]
        dv_ref[...] = dvv[pl.ds(ATTN_CHUNK, ATTN_CHUNK), :]
        if ex is not None:
            @pl.when(jnp.logical_and(pl.program_id(0) == n_hp - 1, step == nch - 1))
            def _():
                ex.wait(*hx)

    cur = pl.BlockSpec((ATTN_CHUNK, LANES), lambda h, t: (nch - 1 - t, h))
    prev = pl.BlockSpec((ATTN_CHUNK, LANES), lambda h, t: (jnp.maximum(nch - 2 - t, 0), h))
    big = pltpu.VMEM((2 * ATTN_CHUNK, LANES), F32)
    pair_t = pltpu.VMEM((NBLK, LANES, 2 * LANES), BF16)
    hosted = ex is not None
    return pl.pallas_call(
        body, name="attn_bwd", grid=(n_hp, nch),
        in_specs=[cur, prev, cur, prev, cur, cur, cur, cur] + (ex.specs() if hosted else []),
        out_specs=[cur, cur, cur] + (ex.specs() if hosted else []),
        out_shape=[_sds((s, ATTN_W))] * 3 + (ex.out_shape() if hosted else []),
        scratch_shapes=[big, big, big, big, pair_t, pair_t, pltpu.VMEM((NBLK, 2 * ATTN_BLOCK, LANES), BF16),
                        pltpu.VMEM((ATTN_BLOCK, 2 * LANES), F32),
                        pltpu.VMEM((npat, ATTN_CHUNK, LANES), F32), pltpu.VMEM((ATTN_CHUNK, LANES), F32),
                        pltpu.VMEM((NBLK, 2 * ATTN_BLOCK, LANES), F32), pltpu.VMEM((NBLK, 2 * ATTN_BLOCK, LANES), F32)]
        + (ex.scratch() if hosted else []),
        compiler_params=_cp(("arbitrary", "arbitrary")),
    )(qn, kn, kn, v, v, o, lse, do, *(ex.srcs if hosted else []))


def _attn_fwd_v1(qn, kn, v):
    s = qn.shape[0]
    nch = s // ATTN_CHUNK
    scale = HEAD_DIM ** -0.5
    npat = len(DILATIONS)

    def body(q_ref, kp_ref, kc_ref, vp_ref, vc_ref, o_ref, lse_ref, kk, vv, m_s, l_s, acc_s):
        i = pl.program_id(1)
        kk[pl.ds(0, ATTN_CHUNK), :] = kp_ref[...]
        kk[pl.ds(ATTN_CHUNK, ATTN_CHUNK), :] = kc_ref[...]
        vv[pl.ds(0, ATTN_CHUNK), :] = vp_ref[...]
        vv[pl.ds(ATTN_CHUNK, ATTN_CHUNK), :] = vc_ref[...]
        row, col = _attn_masks()
        head0 = col < HEAD_DIM
        mask_cur = col <= row
        diff = col - row

        for p, d in enumerate(DILATIONS):
            nb = ATTN_CHUNK // (ATTN_BLOCK * d)

            def blk(t, carry, p=p, d=d, nb=nb):
                b, qrows, crows, prows = _attn_rows(t, d, nb)
                q = q_ref[qrows, :]
                kc = kk[crows, :].astype(BF16)
                kp = kk[prows, :].astype(BF16)
                vc = vv[crows, :].astype(BF16)
                vp = vv[prows, :].astype(BF16)
                thr = jnp.where(jnp.logical_or(i > 0, b > 0), 0, 4 * ATTN_BLOCK)
                mask_prev = diff >= thr
                accs, ms, ls = [], [], []
                for h in range(2):
                    hm = head0 if h == 0 else jnp.logical_not(head0)
                    qh = jnp.where(hm, q, 0.0).astype(BF16)
                    s_p = jnp.where(mask_prev, _dot_nt(qh, kp) * scale, NEG_INF)
                    s_c = jnp.where(mask_cur, _dot_nt(qh, kc) * scale, NEG_INF)
                    m = jnp.maximum(jnp.max(s_p, axis=-1, keepdims=True), jnp.max(s_c, axis=-1, keepdims=True))
                    e_p = jnp.exp(s_p - m)
                    e_c = jnp.exp(s_c - m)
                    l = jnp.sum(e_p, axis=-1, keepdims=True) + jnp.sum(e_c, axis=-1, keepdims=True)
                    accs.append(_dot(e_p.astype(BF16), vp) + _dot(e_c.astype(BF16), vc))
                    ms.append(jnp.broadcast_to(m, (ATTN_BLOCK, LANES)))
                    ls.append(jnp.broadcast_to(l, (ATTN_BLOCK, LANES)))
                m_s[p, qrows, :] = jnp.where(head0, ms[0], ms[1])
                l_s[p, qrows, :] = jnp.where(head0, ls[0], ls[1])
                acc_s[p, qrows, :] = jnp.where(head0, accs[0], accs[1])
                return carry

            lax.fori_loop(0, ATTN_CHUNK // ATTN_BLOCK, blk, 0)

        def merge(t, carry):
            rows = pl.ds(pl.multiple_of(t * ATTN_BLOCK, ATTN_BLOCK), ATTN_BLOCK)
            m_all = m_s[0, rows, :]
            for p in range(1, npat):
                m_all = jnp.maximum(m_all, m_s[p, rows, :])
            num = jnp.zeros((ATTN_BLOCK, LANES), F32)
            den = jnp.zeros((ATTN_BLOCK, LANES), F32)
            for p in range(npat):
                w = jnp.exp(m_s[p, rows, :] - m_all)
                num = num + acc_s[p, rows, :] * w
                den = den + l_s[p, rows, :] * w
            o_ref[rows, :] = num / den
            lse_ref[rows, :] = m_all + jnp.log(den)
            return carry

        lax.fori_loop(0, ATTN_CHUNK // ATTN_BLOCK, merge, 0)

    cur = pl.BlockSpec((ATTN_CHUNK, LANES), lambda h, i: (i, h))
    prev = pl.BlockSpec((ATTN_CHUNK, LANES), lambda h, i: (jnp.maximum(i - 1, 0), h))
    return pl.pallas_call(
        body, name="attn_fwd", grid=(ATTN_W // LANES, nch),
        in_specs=[cur, prev, cur, prev, cur],
        out_specs=[cur, cur],
        out_shape=[_sds((s, ATTN_W)), _sds((s, ATTN_W))],
        scratch_shapes=[pltpu.VMEM((2 * ATTN_CHUNK, LANES), F32), pltpu.VMEM((2 * ATTN_CHUNK, LANES), F32),
                        pltpu.VMEM((npat, ATTN_CHUNK, LANES), F32), pltpu.VMEM((npat, ATTN_CHUNK, LANES), F32),
                        pltpu.VMEM((npat, ATTN_CHUNK, LANES), F32)],
        compiler_params=_cp(("parallel", "parallel")),
    )(qn, kn, kn, v, v)


def _attn_bwd_v1(qn, kn, v, o, lse, do):
    s = qn.shape[0]
    nch = s // ATTN_CHUNK
    scale = HEAD_DIM ** -0.5
    npat = len(DILATIONS)

    def body(q_ref, kp_ref, kc_ref, vp_ref, vc_ref, o_ref, lse_ref, do_ref, dq_ref, dk_ref, dv_ref,
             kk, vv, dkk, dvv, dq_s, dl_s):
        step = pl.program_id(1)
        i = nch - 1 - step
        kk[pl.ds(0, ATTN_CHUNK), :] = kp_ref[...]
        kk[pl.ds(ATTN_CHUNK, ATTN_CHUNK), :] = kc_ref[...]
        vv[pl.ds(0, ATTN_CHUNK), :] = vp_ref[...]
        vv[pl.ds(ATTN_CHUNK, ATTN_CHUNK), :] = vc_ref[...]

        @pl.when(step == 0)
        def _():
            dkk[pl.ds(ATTN_CHUNK, ATTN_CHUNK), :] = jnp.zeros((ATTN_CHUNK, LANES), F32)
            dvv[pl.ds(ATTN_CHUNK, ATTN_CHUNK), :] = jnp.zeros((ATTN_CHUNK, LANES), F32)

        @pl.when(step > 0)
        def _():
            dkk[pl.ds(ATTN_CHUNK, ATTN_CHUNK), :] = dkk[pl.ds(0, ATTN_CHUNK), :]
            dvv[pl.ds(ATTN_CHUNK, ATTN_CHUNK), :] = dvv[pl.ds(0, ATTN_CHUNK), :]

        dkk[pl.ds(0, ATTN_CHUNK), :] = jnp.zeros((ATTN_CHUNK, LANES), F32)
        dvv[pl.ds(0, ATTN_CHUNK), :] = jnp.zeros((ATTN_CHUNK, LANES), F32)

        row, col = _attn_masks()
        head0 = col < HEAD_DIM
        mask_cur = col <= row
        diff = col - row

        def delta(t, carry):
            rows = pl.ds(pl.multiple_of(t * ATTN_BLOCK, ATTN_BLOCK), ATTN_BLOCK)
            prod = do_ref[rows, :] * o_ref[rows, :]
            d0 = jnp.sum(jnp.where(head0, prod, 0.0), axis=-1, keepdims=True)
            d1 = jnp.sum(jnp.where(head0, 0.0, prod), axis=-1, keepdims=True)
            dl_s[rows, :] = jnp.where(head0, d0, d1)
            return carry

        lax.fori_loop(0, ATTN_CHUNK // ATTN_BLOCK, delta, 0)

        for p, d in enumerate(DILATIONS):
            nb = ATTN_CHUNK // (ATTN_BLOCK * d)

            def blk(t, carry, p=p, d=d, nb=nb):
                b, qrows, crows, prows = _attn_rows(t, d, nb)
                q = q_ref[qrows, :]
                dout = do_ref[qrows, :]
                lse_b = lse_ref[qrows, :]
                dl_b = dl_s[qrows, :]
                kc = kk[crows, :].astype(BF16)
                kp = kk[prows, :].astype(BF16)
                vc = vv[crows, :].astype(BF16)
                vp = vv[prows, :].astype(BF16)
                thr = jnp.where(jnp.logical_or(i > 0, b > 0), 0, 4 * ATTN_BLOCK)
                mask_prev = diff >= thr
                dqs = []
                dk_p = jnp.zeros((ATTN_BLOCK, LANES), F32)
                dk_c = jnp.zeros((ATTN_BLOCK, LANES), F32)
                dv_p = jnp.zeros((ATTN_BLOCK, LANES), F32)
                dv_c = jnp.zeros((ATTN_BLOCK, LANES), F32)
                for h in range(2):
                    hm = head0 if h == 0 else jnp.logical_not(head0)
                    c0 = h * HEAD_DIM
                    qh = jnp.where(hm, q, 0.0).astype(BF16)
                    doh = jnp.where(hm, dout, 0.0).astype(BF16)
                    lse_h = lse_b[:, c0:c0 + 1]
                    dl_h = dl_b[:, c0:c0 + 1]
                    s_p = jnp.where(mask_prev, _dot_nt(qh, kp) * scale, NEG_INF)
                    s_c = jnp.where(mask_cur, _dot_nt(qh, kc) * scale, NEG_INF)
                    p_p = jnp.exp(s_p - lse_h)
                    p_c = jnp.exp(s_c - lse_h)
                    ds_p = (p_p * (_dot_nt(doh, vp) - dl_h) * scale).astype(BF16)
                    ds_c = (p_c * (_dot_nt(doh, vc) - dl_h) * scale).astype(BF16)
                    dqs.append(_dot(ds_p, kp) + _dot(ds_c, kc))
                    dk_p = dk_p + _dot_tn(ds_p, qh)
                    dk_c = dk_c + _dot_tn(ds_c, qh)
                    dv_p = dv_p + _dot_tn(p_p.astype(BF16), doh)
                    dv_c = dv_c + _dot_tn(p_c.astype(BF16), doh)
                dq_s[p, qrows, :] = jnp.where(head0, dqs[0], dqs[1])
                dkk[prows, :] = dkk[prows, :] + dk_p
                dkk[crows, :] = dkk[crows, :] + dk_c
                dvv[prows, :] = dvv[prows, :] + dv_p
                dvv[crows, :] = dvv[crows, :] + dv_c
                return carry

            lax.fori_loop(0, ATTN_CHUNK // ATTN_BLOCK, blk, 0)

        def finish(t, carry):
            rows = pl.ds(pl.multiple_of(t * ATTN_BLOCK, ATTN_BLOCK), ATTN_BLOCK)
            acc = dq_s[0, rows, :]
            for p in range(1, npat):
                acc = acc + dq_s[p, rows, :]
            dq_ref[rows, :] = acc
            return carry

        lax.fori_loop(0, ATTN_CHUNK // ATTN_BLOCK, finish, 0)
        dk_ref[...] = dkk[pl.ds(ATTN_CHUNK, ATTN_CHUNK), :]
        dv_ref[...] = dvv[pl.ds(ATTN_CHUNK, ATTN_CHUNK), :]

    cur = pl.BlockSpec((ATTN_CHUNK, LANES), lambda h, t: (nch - 1 - t, h))
    prev = pl.BlockSpec((ATTN_CHUNK, LANES), lambda h, t: (jnp.maximum(nch - 2 - t, 0), h))
    big = pltpu.VMEM((2 * ATTN_CHUNK, LANES), F32)
    return pl.pallas_call(
        body, name="attn_bwd", grid=(ATTN_W // LANES, nch),
        in_specs=[cur, prev, cur, prev, cur, cur, cur, cur],
        out_specs=[cur, cur, cur],
        out_shape=[_sds((s, ATTN_W))] * 3,
        scratch_shapes=[big, big, big, big, pltpu.VMEM((npat, ATTN_CHUNK, LANES), F32),
                        pltpu.VMEM((ATTN_CHUNK, LANES), F32)],
        compiler_params=_cp(("parallel", "arbitrary")),
    )(qn, kn, kn, v, v, o, lse, do)


def _discretize(lr, li, dt):
    mag = jnp.exp(lr * dt)
    abr = mag * jnp.cos(li * dt)
    abi = mag * jnp.sin(li * dt)
    den = lr * lr + li * li
    nr, ni = abr - 1.0, abi
    cr = (nr * lr + ni * li) / den
    ci = (ni * lr - nr * li) / den
    return abr, abi, den, nr, ni, cr, ci


def _ssm_discretize(a_re, a_im, log_dt, b_re_t, b_im_t):
    def body(ar_ref, ai_ref, ldt_ref, br_ref, bi_ref, abr_ref, abi_ref, bbr_ref, bbi_ref):
        abr, abi, _, _, _, cr, ci = _discretize(ar_ref[...], ai_ref[...], jnp.exp(ldt_ref[...]))
        br, bi = br_ref[...], bi_ref[...]
        abr_ref[...] = abr
        abi_ref[...] = abi
        bbr_ref[...] = cr * br - ci * bi
        bbi_ref[...] = cr * bi + ci * br

    return pl.pallas_call(
        body, name="ssm_discretize",
        out_shape=[_sds(a_re.shape)] * 2 + [_sds(b_re_t.shape)] * 2,
    )(a_re, a_im, log_dt, b_re_t, b_im_t)


def _ssm_discretize_bwd(a_re, a_im, log_dt, b_re_t, b_im_t, dabr, dabi, dbbr, dbbi):
    def body(ar_ref, ai_ref, ldt_ref, br_ref, bi_ref, dabr_ref, dabi_ref, dbbr_ref, dbbi_ref,
             dar_ref, dai_ref, dldt_ref, dbr_ref, dbi_ref):
        lr, li = ar_ref[...], ai_ref[...]
        dt = jnp.exp(ldt_ref[...])
        abr, abi, den, nr, ni, cr, ci = _discretize(lr, li, dt)
        br, bi = br_ref[...], bi_ref[...]
        gbr, gbi = dbbr_ref[...], dbbi_ref[...]
        dcr = jnp.sum(gbr * br + gbi * bi, axis=1, keepdims=True)
        dci = jnp.sum(gbi * br - gbr * bi, axis=1, keepdims=True)
        dbr_ref[...] = cr * gbr + ci * gbi
        dbi_ref[...] = cr * gbi - ci * gbr
        dnr = (dcr * lr - dci * li) / den
        dni = (dcr * li + dci * lr) / den
        dden = -(dcr * cr + dci * ci) / den
        dlr = (dcr * nr + dci * ni) / den + dden * 2.0 * lr
        dli = (dcr * ni - dci * nr) / den + dden * 2.0 * li
        gabr = dabr_ref[...] + dnr
        gabi = dabi_ref[...] + dni
        dphi = gabr * abr + gabi * abi
        dth = gabi * abr - gabr * abi
        dar_ref[...] = dlr + dphi * dt
        dai_ref[...] = dli + dth * dt
        dldt_ref[...] = jnp.sum(dphi * lr + dth * li, axis=2, keepdims=True) * dt

    return pl.pallas_call(
        body, name="ssm_discretize_bwd",
        out_shape=[_sds(a_re.shape)] * 2 + [_sds(log_dt.shape)] + [_sds(b_re_t.shape)] * 2,
    )(a_re, a_im, log_dt, b_re_t, b_im_t, dabr, dabi, dbbr, dbbi)


def _ssm_power(abr, abi, n_sq):
    def body(r_ref, i_ref, or_ref, oi_ref):
        r, i = r_ref[...], i_ref[...]
        for _ in range(n_sq):
            r, i = r * r - i * i, 2.0 * r * i
        or_ref[...] = r
        oi_ref[...] = i

    return pl.pallas_call(body, name="ssm_power", out_shape=[_sds(abr.shape)] * 2)(abr, abi)


N_CB = SSM_W // LANES
CB_STATES = N_STATE // N_CB
ROWS = N_SEG * SSM_LK


class _Neg:
    def __init__(self, ref):
        self.ref = ref

    def __getitem__(self, idx):
        return -self.ref[idx]


def _seg_init(fin_r, fin_i, pw_r, pw_i, x_r, x_i, reverse):
    zero = jnp.zeros((1, N_STATE), F32)
    cr, ci = zero, zero
    order = range(N_SEG - 1, -1, -1) if reverse else range(N_SEG)
    pr = pw_r[...]
    pi = -pw_i[...] if reverse else pw_i[...]
    for j in order:
        x_r[pl.ds(j, 1), :] = cr
        x_i[pl.ds(j, 1), :] = ci
        fr, fi = fin_r[pl.ds(j, 1), :], fin_i[pl.ds(j, 1), :]
        cr, ci = fr + pr * cr - pi * ci, fi + pr * ci + pi * cr


def _scan_rows(a_r, a_i, b_r, b_i, x_r, x_i, o_r, o_i, b_off, n_steps, reverse):
    w = 512
    for c in range(N_STATE // w):
        cols = pl.ds(c * w, w)
        ar = jnp.broadcast_to(a_r[:, cols], (N_SEG, w))
        ai = jnp.broadcast_to(a_i[:, cols], (N_SEG, w))

        def step(t, carry, cols=cols, ar=ar, ai=ai):
            xr, xi = carry
            k = (n_steps - 1 - t) if reverse else t
            rows = pl.ds(pl.multiple_of(k * N_SEG + b_off, N_SEG), N_SEG)
            nr = ar * xr - ai * xi + b_r[rows, cols]
            ni = ar * xi + ai * xr + b_i[rows, cols]
            o_r[rows, cols] = nr
            o_i[rows, cols] = ni
            return nr, ni

        xr, xi = lax.fori_loop(0, n_steps, step, (x_r[:, cols], x_i[:, cols]), unroll=4)
        x_r[:, cols] = xr
        x_i[:, cols] = xi


def _permute_in(src_ref, dst):
    for c in range(N_CB):
        dst[c] = src_ref[:, :, pl.ds(c * LANES, LANES)].reshape(ROWS, LANES)


def _permute_out(src, dst_ref):
    for c in range(N_CB):
        dst_ref[:, :, pl.ds(c * LANES, LANES)] = src[c].reshape(SSM_LK, N_SEG, LANES)


def _ssm_fwd_v1(u3, abr, abi, pw_r, pw_i, fin_r, fin_i, bb_r, bb_i, cc_r, cc_i, dskip, finals_only):
    sl = u3.shape[0]
    nch = sl // SSM_LK

    def body(u_ref, abr_ref, abi_ref, pwr_ref, pwi_ref, finr_ref, fini_ref, bbr_ref, bbi_ref,
             ccr_ref, cci_ref, d_ref, *rest):
        if finals_only:
            xfr_ref, xfi_ref, up, xs_r, xs_i, x_r, x_i = rest
        else:
            y_ref, xsr_ref, xsi_ref, up, yp, xs_r, xs_i, x_r, x_i = rest
        k = pl.program_id(0)

        @pl.when(k == 0)
        def _():
            _seg_init(finr_ref, fini_ref, pwr_ref, pwi_ref, x_r, x_i, False)

        if not finals_only:
            xsr_ref[0] = x_r[...]
            xsi_ref[0] = x_i[...]
        _permute_in(u_ref, up)
        for c in range(N_CB):
            lhs = up[c].astype(BF16)
            xs_r[:, pl.ds(c * CB_STATES, CB_STATES)] = _dot(lhs, bbr_ref[c])
            xs_i[:, pl.ds(c * CB_STATES, CB_STATES)] = _dot(lhs, bbi_ref[c])
        _scan_rows(abr_ref, abi_ref, xs_r, xs_i, x_r, x_i, xs_r, xs_i, 0, SSM_LK, False)
        if finals_only:
            @pl.when(k == nch - 1)
            def _():
                xfr_ref[...] = x_r[...]
                xfi_ref[...] = x_i[...]
        else:
            for c in range(N_CB):
                cols = pl.ds(c * CB_STATES, CB_STATES)
                yp[c] = (_dot(xs_r[:, cols].astype(BF16), ccr_ref[c]) - _dot(xs_i[:, cols].astype(BF16), cci_ref[c])
                         + d_ref[:, pl.ds(c * LANES, LANES)] * up[c])
            _permute_out(yp, y_ref)

    ublk = pl.BlockSpec((SSM_LK, N_SEG, SSM_W), lambda k: (k, 0, 0))
    st = pl.BlockSpec((1, N_SEG, N_STATE), lambda k: (k, 0, 0))
    vec = _full((1, N_STATE))
    mat = _full((N_SEG, N_STATE))
    chunk = pltpu.VMEM((N_CB, ROWS, LANES), F32)
    big = pltpu.VMEM((ROWS, N_STATE), F32)
    small = pltpu.VMEM((N_SEG, N_STATE), F32)
    if finals_only:
        out_specs, out_shape = [mat, mat], [_sds((N_SEG, N_STATE))] * 2
        scratch, name = [chunk, big, big, small, small], "ssm_fwd_finals"
    else:
        out_specs = [ublk, st, st]
        out_shape = [_sds(u3.shape)] + [_sds((nch, N_SEG, N_STATE))] * 2
        scratch, name = [chunk, chunk, big, big, small, small], "ssm_fwd"
    return pl.pallas_call(
        body, name=name, grid=(nch,),
        in_specs=[ublk, vec, vec, vec, vec, mat, mat,
                  _full((N_CB, LANES, CB_STATES)), _full((N_CB, LANES, CB_STATES)),
                  _full((N_CB, CB_STATES, LANES)), _full((N_CB, CB_STATES, LANES)), _full((1, SSM_W))],
        out_specs=out_specs, out_shape=out_shape, scratch_shapes=scratch,
        compiler_params=_cp(("arbitrary",)),
    )(u3, abr, abi, pw_r, pw_i, fin_r, fin_i, bb_r, bb_i, cc_r, cc_i, dskip)


def _ssm_bwd_v1(u3, dy3, xst_r, xst_i, abr, abi, pw_r, pw_i, fin_r, fin_i, bb_r, bb_i, bbt_r, bbt_i,
                cct_r, cct_i, dskip, finals_only):
    sl = u3.shape[0]
    nch = sl // SSM_LK

    def body(u_ref, g_ref, xsr_ref, xsi_ref, abr_ref, abi_ref, pwr_ref, pwi_ref,
             finr_ref, fini_ref, bbr_ref, bbi_ref, btr_ref, bti_ref, ctr_ref, cti_ref, d_ref, *rest):
        if finals_only:
            lfr_ref, lfi_ref, gp, l_r, l_i, lam_r, lam_i = rest
        else:
            (du_ref, dar_ref, dai_ref, dbr_ref, dbi_ref, dcr_ref, dci_ref, dd_ref,
             gp, up, yp, l_r, l_i, lam_r, lam_i, x_r, x_i, xx_r, xx_i, sar, sai, sdd) = rest
        t = pl.program_id(0)

        @pl.when(t == 0)
        def _():
            _seg_init(finr_ref, fini_ref, pwr_ref, pwi_ref, lam_r, lam_i, True)
            if not finals_only:
                sar[...] = jnp.zeros_like(sar)
                sai[...] = jnp.zeros_like(sai)
                sdd[...] = jnp.zeros_like(sdd)
                dbr_ref[...] = jnp.zeros_like(dbr_ref)
                dbi_ref[...] = jnp.zeros_like(dbi_ref)
                dcr_ref[...] = jnp.zeros_like(dcr_ref)
                dci_ref[...] = jnp.zeros_like(dci_ref)

        _permute_in(g_ref, gp)
        for c in range(N_CB):
            cols = pl.ds(c * CB_STATES, CB_STATES)
            lhs = gp[c].astype(BF16)
            l_r[:, cols] = _dot(lhs, ctr_ref[c])
            l_i[:, cols] = -_dot(lhs, cti_ref[c])
        if not finals_only:
            _permute_in(u_ref, up)
            x_r[...] = xsr_ref[0]
            x_i[...] = xsi_ref[0]
            xx_r[pl.ds(0, N_SEG), :] = x_r[...]
            xx_i[pl.ds(0, N_SEG), :] = x_i[...]
            for c in range(N_CB):
                lhs = up[c].astype(BF16)
                xx_r[pl.ds(N_SEG, ROWS), pl.ds(c * CB_STATES, CB_STATES)] = _dot(lhs, bbr_ref[c])
                xx_i[pl.ds(N_SEG, ROWS), pl.ds(c * CB_STATES, CB_STATES)] = _dot(lhs, bbi_ref[c])
            _scan_rows(abr_ref, abi_ref, xx_r, xx_i, x_r, x_i, xx_r, xx_i, N_SEG, SSM_LK, False)
        _scan_rows(abr_ref, _Neg(abi_ref), l_r, l_i, lam_r, lam_i, l_r, l_i, 0, SSM_LK, True)
        if finals_only:
            @pl.when(t == nch - 1)
            def _():
                lfr_ref[...] = lam_r[...]
                lfi_ref[...] = lam_i[...]
        else:
            w = 512
            for cc in range(N_STATE // w):
                cols = pl.ds(cc * w, w)

                def acc_step(kk_, carry, cols=cols):
                    sr, si = carry
                    rows = pl.ds(pl.multiple_of(kk_ * N_SEG, N_SEG), N_SEG)
                    lr, li = l_r[rows, cols], l_i[rows, cols]
                    pr, pi = xx_r[rows, cols], xx_i[rows, cols]
                    return sr + lr * pr + li * pi, si + li * pr - lr * pi

                z = jnp.zeros((N_SEG, w), F32)
                sr, si = lax.fori_loop(0, SSM_LK, acc_step, (z, z), unroll=4)
                sar[:, cols] += sr
                sai[:, cols] += si
            for c in range(N_CB):
                cols = pl.ds(c * CB_STATES, CB_STATES)
                lrb = l_r[:, cols].astype(BF16)
                lib = l_i[:, cols].astype(BF16)
                ub = up[c].astype(BF16)
                gb = gp[c].astype(BF16)
                dbr_ref[c] += _dot_tn(lrb, ub)
                dbi_ref[c] += _dot_tn(lib, ub)
                dcr_ref[c] += _dot_tn(gb, xx_r[pl.ds(N_SEG, ROWS), cols].astype(BF16))
                dci_ref[c] += -_dot_tn(gb, xx_i[pl.ds(N_SEG, ROWS), cols].astype(BF16))
                yp[c] = _dot(lrb, btr_ref[c]) + _dot(lib, bti_ref[c]) + d_ref[:, pl.ds(c * LANES, LANES)] * gp[c]
                prod = gp[c] * up[c]
                sdd[:, pl.ds(c * LANES, LANES)] += jnp.sum(prod.reshape(SSM_LK, N_SEG, LANES), axis=0)
            _permute_out(yp, du_ref)

            @pl.when(t == nch - 1)
            def _():
                dar_ref[...] = jnp.sum(sar[...], axis=0, keepdims=True)
                dai_ref[...] = jnp.sum(sai[...], axis=0, keepdims=True)
                dd_ref[...] = jnp.sum(sdd[...], axis=0, keepdims=True)

    ublk = pl.BlockSpec((SSM_LK, N_SEG, SSM_W), lambda t: (nch - 1 - t, 0, 0))
    st = pl.BlockSpec((1, N_SEG, N_STATE), lambda t: (nch - 1 - t, 0, 0))
    vec = _full((1, N_STATE))
    mat = _full((N_SEG, N_STATE))
    cs = _full((N_CB, LANES, CB_STATES))
    sc = _full((N_CB, CB_STATES, LANES))
    in_specs = [ublk, ublk, st, st, vec, vec, vec, vec, mat, mat, cs, cs, sc, sc, cs, cs, _full((1, SSM_W))]
    chunk = pltpu.VMEM((N_CB, ROWS, LANES), F32)
    big = pltpu.VMEM((ROWS, N_STATE), F32)
    small = pltpu.VMEM((N_SEG, N_STATE), F32)
    if finals_only:
        out_specs, out_shape = [mat, mat], [_sds((N_SEG, N_STATE))] * 2
        scratch, name = [chunk, big, big, small, small], "ssm_bwd_finals"
    else:
        out_specs = [ublk, vec, vec, sc, sc, cs, cs, _full((1, SSM_W))]
        out_shape = ([_sds(u3.shape), _sds((1, N_STATE)), _sds((1, N_STATE))]
                     + [_sds((N_CB, CB_STATES, LANES))] * 2 + [_sds((N_CB, LANES, CB_STATES))] * 2
                     + [_sds((1, SSM_W))])
        xx = pltpu.VMEM((ROWS + N_SEG, N_STATE), F32)
        scratch = [chunk, chunk, chunk, big, big, small, small, small, small, xx, xx, small, small,
                   pltpu.VMEM((N_SEG, SSM_W), F32)]
        name = "ssm_bwd"
    return pl.pallas_call(
        body, name=name, grid=(nch,), in_specs=in_specs, out_specs=out_specs, out_shape=out_shape,
        scratch_shapes=scratch, compiler_params=_cp(("arbitrary",)),
    )(u3, dy3, xst_r, xst_i, abr, abi, pw_r, pw_i, fin_r, fin_i, bb_r, bb_i, bbt_r, bbt_i, cct_r, cct_i, dskip)


def _scan_block(a_r, a_i, c, b_r, b_i, b_off, x_r, x_i, reverse, acc=None):
    cols = pl.ds(c * CB_STATES, CB_STATES)
    ar = jnp.broadcast_to(a_r[:, cols], (N_SEG, CB_STATES))
    ai = jnp.broadcast_to(a_i[:, cols], (N_SEG, CB_STATES))
    xr, xi = x_r[:, cols], x_i[:, cols]
    if acc is not None:
        sr = jnp.zeros((N_SEG, CB_STATES), F32)
        si = jnp.zeros((N_SEG, CB_STATES), F32)
    for t in range(SSM_LK):
        k = (SSM_LK - 1 - t) if reverse else t
        rows = pl.ds(k * N_SEG + b_off, N_SEG)
        xr, xi = ar * xr - ai * xi + b_r[rows, :], ar * xi + ai * xr + b_i[rows, :]
        b_r[rows, :] = xr
        b_i[rows, :] = xi
        if acc is not None:
            pr, pi = acc[0][pl.ds(k * N_SEG, N_SEG), :], acc[1][pl.ds(k * N_SEG, N_SEG), :]
            sr = sr + xr * pr + xi * pi
            si = si + xi * pr - xr * pi
    x_r[:, cols] = xr
    x_i[:, cols] = xi
    if acc is not None:
        acc[2][:, cols] += sr
        acc[3][:, cols] += si


def _ssm_fwd(u3, abr, abi, pw_r, pw_i, fin_r, fin_i, bb_r, bb_i, cc_r, cc_i, dskip, finals_only):
    sl = u3.shape[0]
    nch = sl // SSM_LK

    def body(u_ref, abr_ref, abi_ref, pwr_ref, pwi_ref, finr_ref, fini_ref, bbr_ref, bbi_ref,
             ccr_ref, cci_ref, d_ref, *rest):
        if finals_only:
            xfr_ref, xfi_ref, up, x_r, x_i = rest[:5]
        else:
            y_ref, xsr_ref, xsi_ref, up, yp, x_r, x_i = rest[:7]
        xs_r, xs_i = rest[-2 * N_CB:-N_CB], rest[-N_CB:]
        k = pl.program_id(0)

        @pl.when(k == 0)
        def _():
            _seg_init(finr_ref, fini_ref, pwr_ref, pwi_ref, x_r, x_i, False)

        if not finals_only:
            xsr_ref[0] = x_r[...]
            xsi_ref[0] = x_i[...]
        _permute_in(u_ref, up)

        def drive(c):
            lhs = up[c].astype(BF16)
            xs_r[c][...] = _dot(lhs, bbr_ref[c])
            xs_i[c][...] = _dot(lhs, bbi_ref[c])

        def readout(c):
            yp[c] = (_dot(xs_r[c][...].astype(BF16), ccr_ref[c]) - _dot(xs_i[c][...].astype(BF16), cci_ref[c])
                     + d_ref[:, pl.ds(c * LANES, LANES)] * up[c])

        drive(0)
        for c in range(N_CB):
            if c + 1 < N_CB:
                drive(c + 1)
            if c >= 1 and not finals_only:
                readout(c - 1)
            _scan_block(abr_ref, abi_ref, c, xs_r[c], xs_i[c], 0, x_r, x_i, False)
        if finals_only:
            @pl.when(k == nch - 1)
            def _():
                xfr_ref[...] = x_r[...]
                xfi_ref[...] = x_i[...]
        else:
            readout(N_CB - 1)
            _permute_out(yp, y_ref)

    ublk = pl.BlockSpec((SSM_LK, N_SEG, SSM_W), lambda k: (k, 0, 0))
    st = pl.BlockSpec((1, N_SEG, N_STATE), lambda k: (k, 0, 0))
    vec = _full((1, N_STATE))
    mat = _full((N_SEG, N_STATE))
    chunk = pltpu.VMEM((N_CB, ROWS, LANES), F32)
    blocks = [pltpu.VMEM((ROWS, CB_STATES), F32)] * (2 * N_CB)
    small = pltpu.VMEM((N_SEG, N_STATE), F32)
    if finals_only:
        out_specs, out_shape = [mat, mat], [_sds((N_SEG, N_STATE))] * 2
        scratch, name = [chunk, small, small] + blocks, "ssm_fwd_finals"
    else:
        out_specs = [ublk, st, st]
        out_shape = [_sds(u3.shape)] + [_sds((nch, N_SEG, N_STATE))] * 2
        scratch, name = [chunk, chunk, small, small] + blocks, "ssm_fwd"
    return pl.pallas_call(
        body, name=name, grid=(nch,),
        in_specs=[ublk, vec, vec, vec, vec, mat, mat,
                  _full((N_CB, LANES, CB_STATES)), _full((N_CB, LANES, CB_STATES)),
                  _full((N_CB, CB_STATES, LANES)), _full((N_CB, CB_STATES, LANES)), _full((1, SSM_W))],
        out_specs=out_specs, out_shape=out_shape, scratch_shapes=scratch,
        compiler_params=_cp(("arbitrary",)),
    )(u3, abr, abi, pw_r, pw_i, fin_r, fin_i, bb_r, bb_i, cc_r, cc_i, dskip)


def _ssm_bwd(u3, dy3, xst_r, xst_i, abr, abi, pw_r, pw_i, fin_r, fin_i, bb_r, bb_i, bbt_r, bbt_i,
             cct_r, cct_i, dskip, finals_only):
    sl = u3.shape[0]
    nch = sl // SSM_LK

    def body(u_ref, g_ref, xsr_ref, xsi_ref, abr_ref, abi_ref, pwr_ref, pwi_ref,
             finr_ref, fini_ref, bbr_ref, bbi_ref, btr_ref, bti_ref, ctr_ref, cti_ref, d_ref, *rest):
        if finals_only:
            lfr_ref, lfi_ref, gp, lam_r, lam_i = rest[:5]
            l_r, l_i = rest[-2 * N_CB:-N_CB], rest[-N_CB:]
        else:
            (du_ref, dar_ref, dai_ref, dbr_ref, dbi_ref, dcr_ref, dci_ref, dd_ref,
             gp, up, yp, lam_r, lam_i, x_r, x_i, sar, sai, sdd) = rest[:18]
            l_r, l_i = rest[18:18 + N_CB], rest[18 + N_CB:18 + 2 * N_CB]
            xx_r, xx_i = rest[18 + 2 * N_CB:18 + 3 * N_CB], rest[18 + 3 * N_CB:]
        t = pl.program_id(0)

        @pl.when(t == 0)
        def _():
            _seg_init(finr_ref, fini_ref, pwr_ref, pwi_ref, lam_r, lam_i, True)
            if not finals_only:
                sar[...] = jnp.zeros_like(sar)
                sai[...] = jnp.zeros_like(sai)
                sdd[...] = jnp.zeros_like(sdd)
                dbr_ref[...] = jnp.zeros_like(dbr_ref)
                dbi_ref[...] = jnp.zeros_like(dbi_ref)
                dcr_ref[...] = jnp.zeros_like(dcr_ref)
                dci_ref[...] = jnp.zeros_like(dci_ref)

        _permute_in(g_ref, gp)
        if not finals_only:
            _permute_in(u_ref, up)
            x_r[...] = xsr_ref[0]
            x_i[...] = xsi_ref[0]

        def drive(c):
            lhs = gp[c].astype(BF16)
            l_r[c][...] = _dot(lhs, ctr_ref[c])
            l_i[c][...] = -_dot(lhs, cti_ref[c])
            if not finals_only:
                cols = pl.ds(c * CB_STATES, CB_STATES)
                xx_r[c][pl.ds(0, N_SEG), :] = x_r[:, cols]
                xx_i[c][pl.ds(0, N_SEG), :] = x_i[:, cols]
                ub = up[c].astype(BF16)
                xx_r[c][pl.ds(N_SEG, ROWS), :] = _dot(ub, bbr_ref[c])
                xx_i[c][pl.ds(N_SEG, ROWS), :] = _dot(ub, bbi_ref[c])

        def collect(c):
            lrb = l_r[c][...].astype(BF16)
            lib = l_i[c][...].astype(BF16)
            ub = up[c].astype(BF16)
            gb = gp[c].astype(BF16)
            dbr_ref[c] += _dot_tn(lrb, ub)
            dbi_ref[c] += _dot_tn(lib, ub)
            dcr_ref[c] += _dot_tn(gb, xx_r[c][pl.ds(N_SEG, ROWS), :].astype(BF16))
            dci_ref[c] += -_dot_tn(gb, xx_i[c][pl.ds(N_SEG, ROWS), :].astype(BF16))
            yp[c] = _dot(lrb, btr_ref[c]) + _dot(lib, bti_ref[c]) + d_ref[:, pl.ds(c * LANES, LANES)] * gp[c]
            prod = gp[c] * up[c]
            sdd[:, pl.ds(c * LANES, LANES)] += jnp.sum(prod.reshape(SSM_LK, N_SEG, LANES), axis=0)

        drive(0)
        for c in range(N_CB):
            if c + 1 < N_CB:
                drive(c + 1)
            if finals_only:
                _scan_block(abr_ref, _Neg(abi_ref), c, l_r[c], l_i[c], 0, lam_r, lam_i, True)
            else:
                if c >= 1:
                    collect(c - 1)
                _scan_block(abr_ref, abi_ref, c, xx_r[c], xx_i[c], N_SEG, x_r, x_i, False)
                _scan_block(abr_ref, _Neg(abi_ref), c, l_r[c], l_i[c], 0, lam_r, lam_i, True,
                            acc=(xx_r[c], xx_i[c], sar, sai))
        if finals_only:
            @pl.when(t == nch - 1)
            def _():
                lfr_ref[...] = lam_r[...]
                lfi_ref[...] = lam_i[...]
        else:
            collect(N_CB - 1)
            _permute_out(yp, du_ref)

            @pl.when(t == nch - 1)
            def _():
                dar_ref[...] = jnp.sum(sar[...], axis=0, keepdims=True)
                dai_ref[...] = jnp.sum(sai[...], axis=0, keepdims=True)
                dd_ref[...] = jnp.sum(sdd[...], axis=0, keepdims=True)

    ublk = pl.BlockSpec((SSM_LK, N_SEG, SSM_W), lambda t: (nch - 1 - t, 0, 0))
    st = pl.BlockSpec((1, N_SEG, N_STATE), lambda t: (nch - 1 - t, 0, 0))
    vec = _full((1, N_STATE))
    mat = _full((N_SEG, N_STATE))
    cs = _full((N_CB, LANES, CB_STATES))
    sc = _full((N_CB, CB_STATES, LANES))
    in_specs = [ublk, ublk, st, st, vec, vec, vec, vec, mat, mat, cs, cs, sc, sc, cs, cs, _full((1, SSM_W))]
    chunk = pltpu.VMEM((N_CB, ROWS, LANES), F32)
    blocks = [pltpu.VMEM((ROWS, CB_STATES), F32)] * (2 * N_CB)
    small = pltpu.VMEM((N_SEG, N_STATE), F32)
    if finals_only:
        out_specs, out_shape = [mat, mat], [_sds((N_SEG, N_STATE))] * 2
        scratch, name = [chunk, small, small] + blocks, "ssm_bwd_finals"
    else:
        out_specs = [ublk, vec, vec, sc, sc, cs, cs, _full((1, SSM_W))]
        out_shape = ([_sds(u3.shape), _sds((1, N_STATE)), _sds((1, N_STATE))]
                     + [_sds((N_CB, CB_STATES, LANES))] * 2 + [_sds((N_CB, LANES, CB_STATES))] * 2
                     + [_sds((1, SSM_W))])
        scratch = ([chunk, chunk, chunk, small, small, small, small, small, small, pltpu.VMEM((N_SEG, SSM_W), F32)]
                   + blocks + [pltpu.VMEM((ROWS + N_SEG, CB_STATES), F32)] * (2 * N_CB))
        name = "ssm_bwd"
    return pl.pallas_call(
        body, name=name, grid=(nch,), in_specs=in_specs, out_specs=out_specs, out_shape=out_shape,
        scratch_shapes=scratch, compiler_params=_cp(("arbitrary",)),
    )(u3, dy3, xst_r, xst_i, abr, abi, pw_r, pw_i, fin_r, fin_i, bb_r, bb_i, bbt_r, bbt_i, cct_r, cct_i, dskip)


def _row(tm, w):
    return pl.BlockSpec((tm, w), lambda i: (i, 0))


def _acc_rows(ref, rows, first):
    @pl.when(first)
    def _():
        ref[...] = jnp.zeros_like(ref)

    ref[...] += jnp.sum(rows, axis=0, keepdims=True)


def _fwd_mix(attn, y, x, glu_w, glu_b, ga, gs, w_out, g2, tm=1024):
    s = x.shape[0]

    def body(a_ref, y_ref, x_ref, gw_ref, gb_ref, ga_ref, gs_ref, wo_ref, g2_ref, mix_ref, x2_ref, h_ref):
        a = a_ref[...]
        anb = ((a * _rms(a)) * ga_ref[...]).astype(BF16)
        z, _ = _gelu(y_ref[...])
        so = z * jax.nn.sigmoid(_dot(z.astype(BF16), gw_ref[...]) + gb_ref[...])
        snb = ((so * _rms(so)) * gs_ref[...]).astype(BF16)
        mix_ref[:, pl.ds(0, ATTN_W)] = anb
        mix_ref[:, pl.ds(ATTN_W, SSM_W)] = snb
        x2 = x_ref[...] + (_dot(anb, wo_ref[pl.ds(0, ATTN_W), :]) + _dot(snb, wo_ref[pl.ds(ATTN_W, SSM_W), :]))
        x2_ref[...] = x2
        h_ref[...] = ((x2 * _rms(x2)) * g2_ref[...]).astype(BF16)

    return pl.pallas_call(
        body, name="fwd_mix", grid=(s // tm,),
        in_specs=[_row(tm, ATTN_W), _row(tm, SSM_W), _row(tm, D_MODEL), _full((SSM_W, SSM_W)), _full((1, SSM_W)),
                  _full((1, ATTN_W)), _full((1, SSM_W)), _full((D_MODEL, D_MODEL)), _full((1, D_MODEL))],
        out_specs=[_row(tm, D_MODEL), _row(tm, D_MODEL), _row(tm, D_MODEL)],
        out_shape=[_sds((s, D_MODEL), BF16), _sds((s, D_MODEL)), _sds((s, D_MODEL), BF16)],
        compiler_params=_cp(("parallel",)),
    )(attn, y, x, glu_w, glu_b, ga, gs, w_out, g2)


def _mlp_up(h, w_up, tm=512, bn=1024):
    s = h.shape[0]

    def body(h_ref, w_ref, r_ref, hdn_ref):
        hv = h_ref[...]
        for j in range(D_FF // bn):
            cols = pl.ds(j * bn, bn)
            r = jnp.maximum(_dot(hv, w_ref[:, cols]), 0.0)
            r_ref[:, cols] = r.astype(BF16)
            hdn_ref[:, cols] = (r * r).astype(BF16)

    return pl.pallas_call(
        body, name="mlp_up", grid=(s // tm,),
        in_specs=[_row(tm, D_MODEL), _full((D_MODEL, D_FF))],
        out_specs=[_row(tm, D_FF), _row(tm, D_FF)], out_shape=[_sds((s, D_FF), BF16)] * 2,
        compiler_params=_cp(("parallel",)),
    )(h, w_up)


def _mlp_down_loss(hdn, w_down, x2, tgt, tm=512):
    s = x2.shape[0]

    def body(hdn_ref, w_ref, x2_ref, t_ref, dy_ref, dyb_ref, sse_ref):
        err = (x2_ref[...] + _dot(hdn_ref[...], w_ref[...])) - t_ref[...]
        dy = err * (1.0 / D_MODEL)
        dy_ref[...] = dy
        dyb_ref[...] = dy.astype(BF16)

        @pl.when(pl.program_id(0) == 0)
        def _():
            sse_ref[...] = jnp.zeros_like(sse_ref)

        sse_ref[...] += jnp.sum(jnp.sum(err * err, axis=0, keepdims=True), axis=1, keepdims=True)

    return pl.pallas_call(
        body, name="mlp_down_loss", grid=(s // tm,),
        in_specs=[_row(tm, D_FF), _full((D_FF, D_MODEL)), _row(tm, D_MODEL), _row(tm, D_MODEL)],
        out_specs=[_row(tm, D_MODEL), _row(tm, D_MODEL), _full((1, 1))],
        out_shape=[_sds((s, D_MODEL)), _sds((s, D_MODEL), BF16), _sds((1, 1))],
        compiler_params=_cp(("arbitrary",)),
    )(hdn, w_down, x2, tgt)


def _mlp_down_bwd(dyb, w_down_t, r, tm=512, bn=1024):
    s = dyb.shape[0]

    def body(dy_ref, w_ref, r_ref, dup_ref):
        dyv = dy_ref[...]
        for j in range(D_FF // bn):
            cols = pl.ds(j * bn, bn)
            dup_ref[:, cols] = (_dot(dyv, w_ref[:, cols]) * (2.0 * r_ref[:, cols].astype(F32))).astype(BF16)

    return pl.pallas_call(
        body, name="mlp_down_bwd", grid=(s // tm,),
        in_specs=[_row(tm, D_MODEL), _full((D_MODEL, D_FF)), _row(tm, D_FF)],
        out_specs=_row(tm, D_FF), out_shape=_sds((s, D_FF), BF16),
        compiler_params=_cp(("parallel",)),
    )(dyb, w_down_t, r)


def _mlp_up_bwd(dup, w_up_t, x2, g2, dy, tm=512):
    s = x2.shape[0]

    def body(dup_ref, w_ref, x2_ref, g2_ref, dy_ref, dx2_ref, dx2b_ref, dg_ref):
        dx, dg_rows = _rms_bwd(_dot(dup_ref[...], w_ref[...]), x2_ref[...], g2_ref[...], D_MODEL)
        dx2 = dy_ref[...] + dx
        dx2_ref[...] = dx2
        dx2b_ref[...] = dx2.astype(BF16)
        _acc_rows(dg_ref, dg_rows, pl.program_id(0) == 0)

    return pl.pallas_call(
        body, name="mlp_up_bwd", grid=(s // tm,),
        in_specs=[_row(tm, D_FF), _full((D_FF, D_MODEL)), _row(tm, D_MODEL), _full((1, D_MODEL)), _row(tm, D_MODEL)],
        out_specs=[_row(tm, D_MODEL), _row(tm, D_MODEL), _full((1, D_MODEL))],
        out_shape=[_sds((s, D_MODEL)), _sds((s, D_MODEL), BF16), _sds((1, D_MODEL))],
        compiler_params=_cp(("arbitrary",)),
    )(dup, w_up_t, x2, g2, dy)


def _mix_bwd(dx2b, w_out_t, attn, y, glu_w, glu_b, glu_w_t, ga, gs, tm=1024):
    s = attn.shape[0]

    def body(dx2_ref, wot_ref, a_ref, y_ref, gw_ref, gb_ref, gwt_ref, ga_ref, gs_ref,
             da_ref, dys_ref, z_ref, dpre_ref, dga_ref, dgs_ref, dgb_ref):
        first = pl.program_id(0) == 0
        dmix = _dot(dx2_ref[...], wot_ref[...])
        da, dga_rows = _rms_bwd(dmix[:, :ATTN_W], a_ref[...], ga_ref[...], ATTN_W)
        da_ref[...] = da
        yv = y_ref[...]
        z, t = _gelu(yv)
        gate = jax.nn.sigmoid(_dot(z.astype(BF16), gw_ref[...]) + gb_ref[...])
        dso, dgs_rows = _rms_bwd(dmix[:, ATTN_W:], z * gate, gs_ref[...], SSM_W)
        dpre = dso * z * gate * (1.0 - gate)
        dpre_b = dpre.astype(BF16)
        dz = dso * gate + _dot(dpre_b, gwt_ref[...])
        dgelu = 0.5 * (1.0 + t) + 0.5 * yv * (1.0 - t * t) * (GELU_C * (1.0 + 3.0 * 0.044715 * (yv * yv)))
        dys_ref[...] = dz * dgelu
        z_ref[...] = z.astype(BF16)
        dpre_ref[...] = dpre_b
        _acc_rows(dga_ref, dga_rows, first)
        _acc_rows(dgs_ref, dgs_rows, first)
        _acc_rows(dgb_ref, dpre, first)

    vec = _full((1, SSM_W))
    return pl.pallas_call(
        body, name="mix_bwd", grid=(s // tm,),
        in_specs=[_row(tm, D_MODEL), _full((D_MODEL, D_MODEL)), _row(tm, ATTN_W), _row(tm, SSM_W),
                  _full((SSM_W, SSM_W)), vec, _full((SSM_W, SSM_W)), vec, vec],
        out_specs=[_row(tm, ATTN_W), _row(tm, SSM_W), _row(tm, SSM_W), _row(tm, SSM_W), vec, vec, vec],
        out_shape=[_sds((s, ATTN_W)), _sds((s, SSM_W)), _sds((s, SSM_W), BF16), _sds((s, SSM_W), BF16),
                   _sds((1, ATTN_W)), _sds((1, SSM_W)), _sds((1, SSM_W))],
        compiler_params=_cp(("arbitrary",)),
    )(dx2b, w_out_t, attn, y, glu_w, glu_b, glu_w_t, ga, gs)


def _qk_bwd(dqn, dkn, qk, dv, du, gq, gk, gmat, tm=1024):
    s = qk.shape[0]

    def body(dq_ref, dk_ref, qk_ref, dv_ref, du_ref, gq_ref, gk_ref, gm_ref, dp_ref, dgq_ref, dgk_ref):
        first = pl.program_id(0) == 0
        gm = gm_ref[...]
        for idx, (d_ref, g_ref, dg_ref) in enumerate(((dq_ref, gq_ref, dgq_ref), (dk_ref, gk_ref, dgk_ref))):
            xv = qk_ref[:, pl.ds(idx * ATTN_W, ATTN_W)]
            dyv = d_ref[...]
            r = lax.rsqrt(_group_mean(xv * xv, gm) + EPS)
            gdy = dyv * g_ref[...]
            dx = r * gdy - xv * (r * r * r) * _group_mean(gdy * xv, gm)
            dp_ref[:, pl.ds(idx * ATTN_W, ATTN_W)] = dx.astype(BF16)
            _acc_rows(dg_ref, dyv * (xv * r), first)
        dp_ref[:, pl.ds(2 * ATTN_W, ATTN_W)] = dv_ref[...].astype(BF16)
        dp_ref[:, pl.ds(3 * ATTN_W, SSM_W)] = du_ref[...].astype(BF16)

    vec = _full((1, ATTN_W))
    return pl.pallas_call(
        body, name="qk_bwd", grid=(s // tm,),
        in_specs=[_row(tm, ATTN_W), _row(tm, ATTN_W), _row(tm, 2 * ATTN_W), _row(tm, ATTN_W), _row(tm, SSM_W),
                  vec, vec, _full((ATTN_W, ATTN_W))],
        out_specs=[_row(tm, 4 * ATTN_W), vec, vec],
        out_shape=[_sds((s, 4 * ATTN_W), BF16), _sds((1, ATTN_W)), _sds((1, ATTN_W))],
        compiler_params=_cp(("arbitrary",)),
    )(dqn, dkn, qk, dv, du, gq, gk, gmat)


def _in_bwd(dproj, w_in_t, x, g1, dx2, ex=None, tm=1024):
    s = x.shape[0]
    steps = s // tm

    def body(*refs):
        (dp_ref, w_ref, x_ref, g1_ref, dx2_ref), (gx_ref, dg_ref), _, hx = _carry(ex, refs, 5, 2)
        if ex is not None:
            @pl.when(pl.program_id(0) == 0)
            def _():
                ex.start(*hx)

        dx, dg_rows = _rms_bwd(_dot(dp_ref[...], w_ref[...]), x_ref[...], g1_ref[...], D_MODEL)
        gx_ref[...] = dx2_ref[...] + dx
        _acc_rows(dg_ref, dg_rows, pl.program_id(0) == 0)
        if ex is not None:
            @pl.when(pl.program_id(0) == steps - 1)
            def _():
                ex.wait(*hx)

    hosted = ex is not None
    return pl.pallas_call(
        body, name="in_bwd", grid=(steps,),
        in_specs=[_row(tm, 4 * ATTN_W), _full((4 * ATTN_W, D_MODEL)), _row(tm, D_MODEL), _full((1, D_MODEL)),
                  _row(tm, D_MODEL)] + (ex.specs() if hosted else []),
        out_specs=[_row(tm, D_MODEL), _full((1, D_MODEL))] + (ex.specs() if hosted else []),
        out_shape=[_sds((s, D_MODEL)), _sds((1, D_MODEL))] + (ex.out_shape() if hosted else []),
        scratch_shapes=ex.scratch() if hosted else [],
        compiler_params=_cp(("arbitrary",)),
    )(dproj, w_in_t, x, g1, dx2, *(ex.srcs if hosted else []))


def _mm_tn(a, b, name, ts=2048):
    s, k = a.shape
    n = b.shape[1]
    bk, bn = min(k, 1024), min(n, 1024)
    steps = s // ts

    def body(a_ref, b_ref, o_ref, acc):
        t = pl.program_id(2)

        @pl.when(t == 0)
        def _():
            acc[...] = jnp.zeros_like(acc)

        acc[...] += _dot_tn(a_ref[...], b_ref[...])

        @pl.when(t == steps - 1)
        def _():
            o_ref[...] = acc[...].astype(BF16)

    return pl.pallas_call(
        body, name=name, grid=(k // bk, n // bn, steps),
        in_specs=[pl.BlockSpec((ts, bk), lambda i, j, t: (t, i)), pl.BlockSpec((ts, bn), lambda i, j, t: (t, j))],
        out_specs=pl.BlockSpec((bk, bn), lambda i, j, t: (i, j)), out_shape=_sds((k, n), BF16),
        scratch_shapes=[pltpu.VMEM((bk, bn), F32)],
        compiler_params=_cp(("parallel", "parallel", "arbitrary")),
    )(a, b)


def _peer(k):
    x, y, c = lax.axis_index("x"), lax.axis_index("y"), lax.axis_index("c")
    px = 1 - x if k & 4 else x
    py = 1 - y if k & 2 else y
    pc = 1 - c if k & 1 else c
    return (px, py, pc), 4 * px + 2 * py + pc


def _gather_rows(x_shard):
    m_per, n = x_shard.shape

    def body(x_ref, out_ref, send_sems, recv_sems, local_sem):
        x, y, c = lax.axis_index("x"), lax.axis_index("y"), lax.axis_index("c")
        me, sibling = (x, y, c), (x, y, 1 - c)
        chips = [(1 - x, y), (x, 1 - y), (1 - x, 1 - y)]

        def rows(px, py, pc):
            return out_ref.at[pl.ds((4 * px + 2 * py + pc) * m_per, m_per), :]

        def copy(k, block, to, src=None):
            return pltpu.make_async_remote_copy(
                src_ref=rows(*block) if src is None else src, dst_ref=rows(*block),
                send_sem=send_sems.at[k], recv_sem=recv_sems.at[k], device_id=to, device_id_type=MESH)

        mine = pltpu.make_async_copy(x_ref, rows(*me), local_sem)
        mine.start()
        first = [copy(0, me, sibling, src=x_ref)]
        first += [copy(1 + j, me, (*chip, c), src=x_ref) for j, chip in enumerate(chips)]
        for cp in first:
            cp.start()
        passed = [copy(4 + j, (*chip, c), sibling) for j, chip in enumerate(chips)]
        for j, chip in enumerate(chips):
            copy(1 + j, (*chip, c), me).wait_recv()
            passed[j].start()
        copy(0, sibling, me).wait_recv()
        for j, chip in enumerate(chips):
            copy(4 + j, (*chip, 1 - c), me).wait_recv()
        for cp in first + passed:
            cp.wait_send()
        mine.wait()

    return pl.pallas_call(
        body, name="gather_weights", out_shape=_sds((N_DEV * m_per, n), x_shard.dtype),
        in_specs=[pl.BlockSpec(memory_space=pltpu.VMEM)], out_specs=pl.BlockSpec(memory_space=pltpu.VMEM),
        scratch_shapes=[pltpu.SemaphoreType.DMA((7,)), pltpu.SemaphoreType.DMA((7,)), pltpu.SemaphoreType.DMA],
        compiler_params=pltpu.CompilerParams(vmem_limit_bytes=VMEM_LIMIT),
    )(x_shard)


class _Exchange:
    def __init__(self, srcs, whole):
        self.srcs, self.whole, self.n = list(srcs), list(whole), len(srcs)
        self.rows = [a.shape[0] if w else a.shape[0] // N_DEV for a, w in zip(self.srcs, self.whole)]

    def specs(self):
        return [pl.BlockSpec(memory_space=pl.ANY)] * self.n

    def out_shape(self):
        return [_sds((N_DEV, r, a.shape[1]), a.dtype) for r, a in zip(self.rows, self.srcs)]

    def scratch(self):
        return [pltpu.SemaphoreType.DMA((self.n * 7,)), pltpu.SemaphoreType.DMA((self.n * 7,)),
                pltpu.SemaphoreType.DMA((self.n,))]

    def _copies(self, ins, outs, sems):
        send_sems, recv_sems, local_sems = sems
        _, me = _peer(0)
        for w in range(self.n):
            for k in range(N_DEV):
                peer, pidx = _peer(k)
                src = ins[w] if self.whole[w] else ins[w].at[pl.ds(pidx * self.rows[w], self.rows[w]), :]
                if k == 0:
                    yield k, pltpu.make_async_copy(src, outs[w].at[me], local_sems.at[w]), None
                else:
                    sem = w * 7 + k - 1
                    out = pltpu.make_async_remote_copy(src_ref=src, dst_ref=outs[w].at[me], send_sem=send_sems.at[sem],
                                                       recv_sem=recv_sems.at[sem], device_id=peer, device_id_type=MESH)
                    back = pltpu.make_async_remote_copy(src_ref=src, dst_ref=outs[w].at[pidx], send_sem=send_sems.at[sem],
                                                        recv_sem=recv_sems.at[sem], device_id=peer, device_id_type=MESH)
                    yield k, out, back

    def start(self, ins, outs, sems):
        for _, out, _ in self._copies(ins, outs, sems):
            out.start()

    def wait(self, ins, outs, sems):
        for k, out, back in self._copies(ins, outs, sems):
            if k == 0:
                out.wait()
            else:
                back.wait_recv()
                out.wait_send()


def _carry(ex, refs, n_in, n_out):
    nh = ex.n if ex is not None else 0
    ins, hin = refs[:n_in], refs[n_in:n_in + nh]
    outs = refs[n_in + nh:n_in + nh + n_out]
    hout = refs[n_in + nh + n_out:n_in + 2 * nh + n_out]
    rest = refs[n_in + 2 * nh + n_out:]
    if ex is None:
        return ins, outs, rest, None
    return ins, outs, rest[:-3], (hin, hout, rest[-3:])


def _exchange_now(srcs, whole, name):
    ex = _Exchange(srcs, whole)

    def body(*refs):
        _, _, _, (hin, hout, sems) = _carry(ex, refs, 0, 0)
        ex.start(hin, hout, sems)
        ex.wait(hin, hout, sems)

    return pl.pallas_call(body, name=name, out_shape=ex.out_shape(), in_specs=ex.specs(), out_specs=ex.specs(),
                          scratch_shapes=ex.scratch())(*srcs)


def _adamw(w, m, v, gparts, name):
    r, c = w.shape
    tr = r if r * c <= 256 * 1024 else 128 * 1024 // c

    def body(w_ref, m_ref, v_ref, g_ref, go_ref, d_ref, mo_ref, vo_ref):
        g = g_ref[0].astype(F32)
        for i in range(1, N_DEV):
            g = g + g_ref[i].astype(F32)
        nm = ADAM_B1 * m_ref[...] + (1.0 - ADAM_B1) * g
        nv = ADAM_B2 * v_ref[...] + (1.0 - ADAM_B2) * (g * g)
        m_hat = nm / (1.0 - ADAM_B1 ** ADAM_STEP)
        v_hat = nv / (1.0 - ADAM_B2 ** ADAM_STEP)
        go_ref[...] = g
        d_ref[...] = -ADAM_LR * (m_hat / (jnp.sqrt(v_hat) + ADAM_EPS) + ADAM_WD * w_ref[...])
        mo_ref[...] = nm
        vo_ref[...] = nv

    blk = pl.BlockSpec((tr, c), lambda i: (i, 0))
    return pl.pallas_call(
        body, name=name, grid=(r // tr,),
        in_specs=[blk, blk, blk, pl.BlockSpec((N_DEV, tr, c), lambda i: (0, i, 0))],
        out_specs=[blk] * 4, out_shape=[_sds((r, c))] * 4,
        compiler_params=_cp(("parallel",)),
    )(w, m, v, gparts)


def _block_diag(a, states_first):
    a4 = a.reshape(N_CB, 8, SSM_GROUP, SSM_STATE)
    eye = jnp.eye(8, dtype=a.dtype)
    if states_first:
        return jnp.einsum("bgcp,gh->bgphc", a4, eye).reshape(N_CB, CB_STATES, LANES)
    return jnp.einsum("bgcp,gh->bgchp", a4, eye).reshape(N_CB, LANES, CB_STATES)


def _block_diag_of(full, states_first):
    if states_first:
        picked = jnp.einsum("bgphc,gh->bgcp", full.reshape(N_CB, 8, SSM_STATE, 8, SSM_GROUP), jnp.eye(8, dtype=full.dtype))
    else:
        picked = jnp.einsum("bgchp,gh->bgcp", full.reshape(N_CB, 8, SSM_GROUP, 8, SSM_STATE), jnp.eye(8, dtype=full.dtype))
    return picked.reshape(SSM_GROUPS, SSM_GROUP, SSM_STATE)


SMALL_EARLY = ("ssm_a_re", "ssm_a_im", "ssm_log_dt", "ssm_b_re", "ssm_b_im", "ssm_c_re", "ssm_c_im", "ssm_d", "glu_b",
               "attn_out_norm_g", "ssm_out_norm_g", "norm2_g")
SMALL_MID = ("q_norm_g", "k_norm_g")
SMALL_LATE = ("norm1_g",)
SMALL = SMALL_EARLY + SMALL_MID + SMALL_LATE


def _pack_small(arrs):
    parts = []
    for a in arrs:
        flat = a.reshape(-1)
        rows = -(-flat.shape[0] // (8 * LANES)) * 8
        parts.append(jnp.pad(flat, (0, rows * LANES - flat.shape[0])).reshape(rows, LANES))
    return jnp.concatenate(parts, axis=0)


def _unpack_small(packed, shapes):
    out, r0 = [], 0
    for shp in shapes:
        size = math.prod(shp)
        rows = -(-size // (8 * LANES)) * 8
        out.append(packed[r0:r0 + rows].reshape(-1)[:size].reshape(shp))
        r0 += rows
    return out


def kernel(x, norm1_g, w_in, q_norm_g, k_norm_g, ssm_a_re, ssm_a_im, ssm_log_dt, ssm_b_re, ssm_b_im, ssm_c_re, ssm_c_im, ssm_d, glu_w, glu_b, attn_out_norm_g, ssm_out_norm_g, w_out, norm2_g, w_mlp_up, w_mlp_down, loss_target, m_norm1_g, m_w_in, m_q_norm_g, m_k_norm_g, m_ssm_a_re, m_ssm_a_im, m_ssm_log_dt, m_ssm_b_re, m_ssm_b_im, m_ssm_c_re, m_ssm_c_im, m_ssm_d, m_glu_w, m_glu_b, m_attn_out_norm_g, m_ssm_out_norm_g, m_w_out, m_norm2_g, m_w_mlp_up, m_w_mlp_down, v_norm1_g, v_w_in, v_q_norm_g, v_k_norm_g, v_ssm_a_re, v_ssm_a_im, v_ssm_log_dt, v_ssm_b_re, v_ssm_b_im, v_ssm_c_re, v_ssm_c_im, v_ssm_d, v_glu_w, v_glu_b, v_attn_out_norm_g, v_ssm_out_norm_g, v_w_out, v_norm2_g, v_w_mlp_up, v_w_mlp_down):
    weights = dict(norm1_g=norm1_g, w_in=w_in, q_norm_g=q_norm_g, k_norm_g=k_norm_g, ssm_a_re=ssm_a_re,
                   ssm_a_im=ssm_a_im, ssm_log_dt=ssm_log_dt, ssm_b_re=ssm_b_re, ssm_b_im=ssm_b_im,
                   ssm_c_re=ssm_c_re, ssm_c_im=ssm_c_im, ssm_d=ssm_d, glu_w=glu_w, glu_b=glu_b,
                   attn_out_norm_g=attn_out_norm_g, ssm_out_norm_g=ssm_out_norm_g, w_out=w_out, norm2_g=norm2_g,
                   w_mlp_up=w_mlp_up, w_mlp_down=w_mlp_down)
    mom_m = dict(norm1_g=m_norm1_g, w_in=m_w_in, q_norm_g=m_q_norm_g, k_norm_g=m_k_norm_g, ssm_a_re=m_ssm_a_re,
                 ssm_a_im=m_ssm_a_im, ssm_log_dt=m_ssm_log_dt, ssm_b_re=m_ssm_b_re, ssm_b_im=m_ssm_b_im,
                 ssm_c_re=m_ssm_c_re, ssm_c_im=m_ssm_c_im, ssm_d=m_ssm_d, glu_w=m_glu_w, glu_b=m_glu_b,
                 attn_out_norm_g=m_attn_out_norm_g, ssm_out_norm_g=m_ssm_out_norm_g, w_out=m_w_out,
                 norm2_g=m_norm2_g, w_mlp_up=m_w_mlp_up, w_mlp_down=m_w_mlp_down)
    mom_v = dict(norm1_g=v_norm1_g, w_in=v_w_in, q_norm_g=v_q_norm_g, k_norm_g=v_k_norm_g, ssm_a_re=v_ssm_a_re,
                 ssm_a_im=v_ssm_a_im, ssm_log_dt=v_ssm_log_dt, ssm_b_re=v_ssm_b_re, ssm_b_im=v_ssm_b_im,
                 ssm_c_re=v_ssm_c_re, ssm_c_im=v_ssm_c_im, ssm_d=v_ssm_d, glu_w=v_glu_w, glu_b=v_glu_b,
                 attn_out_norm_g=v_attn_out_norm_g, ssm_out_norm_g=v_ssm_out_norm_g, w_out=v_w_out,
                 norm2_g=v_norm2_g, w_mlp_up=v_w_mlp_up, w_mlp_down=v_w_mlp_down)
    order = list(weights)

    xs, tgt = x[0], loss_target[0]
    s = xs.shape[0]
    assert s % ATTN_CHUNK == 0 and (s // N_SEG) % SSM_LK == 0
    seg_len = s // N_SEG
    n_sq = seg_len.bit_length() - 1
    assert 1 << n_sq == seg_len

    w_in_t = _gather_rows(w_in[0].T.astype(BF16))
    w_in_full = w_in_t.T
    later = _Exchange([glu_w[0].astype(BF16), w_out[0].astype(BF16), w_mlp_up[0].T.astype(BF16),
                       w_mlp_down[0].astype(BF16)], [True] * 4)

    gq = jnp.tile(q_norm_g[0], ATTN_W // HEAD_DIM)[None]
    gk = jnp.tile(k_norm_g[0], ATTN_W // HEAD_DIM)[None]
    lane = jnp.arange(ATTN_W) // HEAD_DIM
    gmat = jnp.where(lane[:, None] == lane[None, :], 1.0 / HEAD_DIM, 0.0).astype(BF16)
    a_re3 = ssm_a_re[0][:, None, :]
    a_im3 = ssm_a_im[0][:, None, :]
    ldt3 = ssm_log_dt[0][:, None, None]
    b_re_t = jnp.swapaxes(ssm_b_re[0], 1, 2)
    b_im_t = jnp.swapaxes(ssm_b_im[0], 1, 2)
    c_re, c_im = ssm_c_re[0], ssm_c_im[0]
    dskip = ssm_d[0].reshape(1, SSM_W)

    qk, qn, kn, v, u, xn = _fwd_proj(xs, norm1_g, w_in_full, gq, gk, gmat)
    attn, lse, glu_g, w_out_g, w_up_g, w_down_g = _attn_fwd(qn, kn, v, later)
    glu_full = glu_g.reshape(SSM_W, SSM_W)
    w_out_full = w_out_g.reshape(D_MODEL, D_MODEL)
    w_up_t = w_up_g.reshape(D_FF, D_MODEL)
    w_down_full = w_down_g.reshape(D_FF, D_MODEL)
    w_up_full, w_out_t, w_down_t, glu_t = w_up_t.T, w_out_full.T, w_down_full.T, glu_full.T

    abr3, abi3, bbr, bbi = _ssm_discretize(a_re3, a_im3, ldt3, b_re_t, b_im_t)
    abr, abi = abr3.reshape(1, N_STATE), abi3.reshape(1, N_STATE)
    pw_r, pw_i = _ssm_power(abr, abi, n_sq)
    bb_r, bb_i = _block_diag(bbr, False).astype(BF16), _block_diag(bbi, False).astype(BF16)
    bbt_r, bbt_i = _block_diag(bbr, True).astype(BF16), _block_diag(bbi, True).astype(BF16)
    cc_r, cc_i = _block_diag(c_re, True).astype(BF16), _block_diag(c_im, True).astype(BF16)
    cct_r, cct_i = _block_diag(c_re, False).astype(BF16), _block_diag(c_im, False).astype(BF16)
    seg_major = lambda a: jnp.swapaxes(a.reshape(N_SEG, seg_len, SSM_W), 0, 1)
    seg_minor = lambda a: jnp.swapaxes(a, 0, 1).reshape(s, SSM_W)
    u3 = seg_major(u)
    zero_fin = jnp.zeros((N_SEG, N_STATE), F32)
    ssm_args = (abr, abi, pw_r, pw_i)
    xf_r, xf_i = _ssm_fwd(u3, *ssm_args, zero_fin, zero_fin, bb_r, bb_i, cc_r, cc_i, dskip, True)
    y3, xst_r, xst_i = _ssm_fwd(u3, *ssm_args, xf_r, xf_i, bb_r, bb_i, cc_r, cc_i, dskip, False)
    y = seg_minor(y3)

    mix, x2, h = _fwd_mix(attn, y, xs, glu_full, glu_b, attn_out_norm_g, ssm_out_norm_g, w_out_full, norm2_g)
    r_act, hdn = _mlp_up(h, w_up_full)
    dy, dyb, sse = _mlp_down_loss(hdn, w_down_full, x2, tgt)
    loss = lax.psum(0.5 * sse[0, 0] / D_MODEL, ("x", "y", "c"))

    dup = _mlp_down_bwd(dyb, w_down_t, r_act)
    g_w_down = _mm_tn(hdn, dyb, "grad_w_down")
    dx2, dx2b, g_norm2 = _mlp_up_bwd(dup, w_up_t, x2, norm2_g, dy)
    g_w_up_t = _mm_tn(dup, h, "grad_w_up_t")
    dattn, dys, z_b, dpre_b, g_ga, g_gs, g_glu_b = _mix_bwd(dx2b, w_out_t, attn, y, glu_full, glu_b, glu_t,
                                                             attn_out_norm_g, ssm_out_norm_g)
    g_w_out = _mm_tn(mix, dx2b, "grad_w_out")
    g_glu_w = _mm_tn(z_b, dpre_b, "grad_glu_w")
    dy3 = seg_major(dys)
    bwd_args = (u3, dy3, xst_r, xst_i, abr, abi, pw_r, pw_i)
    lf_r, lf_i = _ssm_bwd(*bwd_args, zero_fin, zero_fin, bb_r, bb_i, bbt_r, bbt_i, cct_r, cct_i, dskip, True)
    du3, dab_r, dab_i, dbb_r, dbb_i, dcc_r, dcc_i, g_d = _ssm_bwd(*bwd_args, lf_r, lf_i, bb_r, bb_i, bbt_r, bbt_i,
                                                                 cct_r, cct_i, dskip, False)
    g_a_re3, g_a_im3, g_ldt3, g_b_re_t, g_b_im_t = _ssm_discretize_bwd(
        a_re3, a_im3, ldt3, b_re_t, b_im_t, dab_r.reshape(a_re3.shape), dab_i.reshape(a_re3.shape),
        _block_diag_of(dbb_r, True), _block_diag_of(dbb_i, True))
    g_c_re, g_c_im = _block_diag_of(dcc_r, False), _block_diag_of(dcc_i, False)
    small_grads = dict(
        ssm_a_re=g_a_re3.reshape(ssm_a_re.shape), ssm_a_im=g_a_im3.reshape(ssm_a_im.shape),
        ssm_log_dt=g_ldt3.reshape(ssm_log_dt.shape), ssm_b_re=jnp.swapaxes(g_b_re_t, 1, 2)[None],
        ssm_b_im=jnp.swapaxes(g_b_im_t, 1, 2)[None], ssm_c_re=g_c_re[None], ssm_c_im=g_c_im[None],
        ssm_d=g_d.reshape(ssm_d.shape), glu_b=g_glu_b, attn_out_norm_g=g_ga, ssm_out_norm_g=g_gs, norm2_g=g_norm2)

    early = _Exchange([g_glu_w, g_w_out, g_w_up_t, g_w_down]
                      + [_pack_small([small_grads[n] for n in SMALL_EARLY])], [False] * 4 + [True])
    dqn, dkn, dv, p_glu, p_w_out, p_w_up, p_w_down, p_early = _attn_bwd(qn, kn, v, attn, lse, dattn, early)

    dproj, g_gq, g_gk = _qk_bwd(dqn, dkn, qk, dv, seg_minor(du3), gq, gk, gmat)
    g_w_in_t = _mm_tn(dproj, xn, "grad_w_in_t")
    small_grads["q_norm_g"] = g_gq.reshape(ATTN_W // HEAD_DIM, HEAD_DIM).sum(0)[None]
    small_grads["k_norm_g"] = g_gk.reshape(ATTN_W // HEAD_DIM, HEAD_DIM).sum(0)[None]
    mid = _Exchange([g_w_in_t, _pack_small([small_grads[n] for n in SMALL_MID])], [False, True])
    grad_x, g_norm1, p_w_in, p_mid = _in_bwd(dproj, w_in_t, xs, norm1_g, dx2, mid)
    (p_late,) = _exchange_now([_pack_small([g_norm1])], [True], "exchange_norm1")

    transposed = {"w_in", "w_mlp_up"}
    res = {}
    for name, gp in (("w_in", p_w_in), ("glu_w", p_glu), ("w_out", p_w_out), ("w_mlp_up", p_w_up), ("w_mlp_down", p_w_down)):
        tr = (lambda a: a.T) if name in transposed else (lambda a: a)
        outs = _adamw(tr(weights[name][0]), tr(mom_m[name][0]), tr(mom_v[name][0]), gp, "adamw_" + name)
        res[name] = [tr(o)[None] for o in outs]
    outs = _adamw(_pack_small([weights[n] for n in SMALL]), _pack_small([mom_m[n] for n in SMALL]),
                  _pack_small([mom_v[n] for n in SMALL]), jnp.concatenate([p_early, p_mid, p_late], axis=1), "adamw_small")
    shapes = [weights[n].shape for n in SMALL]
    unpacked = [_unpack_small(o, shapes) for o in outs]
    for i, n in enumerate(SMALL):
        res[n] = [unpacked[j][i] for j in range(4)]

    return (loss, grad_x[None], *[res[n][0] for n in order], *[res[n][1] for n in order],
            *[res[n][2] for n in order], *[res[n][3] for n in order])
```

```python
import math

import jax
import jax.numpy as jnp
from jax import lax
from jax.experimental import pallas as pl
from jax.experimental.pallas import tpu as pltpu

F32 = jnp.float32
BF16 = jnp.bfloat16

D_MODEL = 1024
ATTN_W = 512
HEAD_DIM = 64
SSM_W = 512
SSM_GROUP = 16
SSM_GROUPS = 32
SSM_STATE = 64
N_STATE = SSM_GROUPS * SSM_STATE
D_FF = 4096
EPS = 1e-6
NEG_INF = -1e30
ATTN_CHUNK = 2048
ATTN_BLOCK = 128
DILATIONS = (1, 4, 16)
N_SEG = 8
SSM_LK = 64
N_DEV = 8
LANES = 128

ADAM_LR = 0.001
ADAM_B1 = 0.9
ADAM_B2 = 0.999
ADAM_EPS = 1e-08
ADAM_WD = 0.01
ADAM_STEP = 10

VMEM_LIMIT = 56 * 1024 * 1024
GELU_C = math.sqrt(2.0 / math.pi)
MESH = pl.DeviceIdType.MESH


def _cp(sem, vmem=VMEM_LIMIT):
    return pltpu.CompilerParams(dimension_semantics=sem, vmem_limit_bytes=vmem)


def _dot(a, b):
    return jnp.dot(a, b, preferred_element_type=F32)


def _dot_nt(a, b):
    return lax.dot_general(a, b, (((1,), (1,)), ((), ())), preferred_element_type=F32)


def _dot_tn(a, b):
    return lax.dot_general(a, b, (((0,), (0,)), ((), ())), preferred_element_type=F32)


def _group_mean(x2, gmat):
    hi = x2.astype(BF16)
    lo = (x2 - hi.astype(F32)).astype(BF16)
    return _dot(hi, gmat) + _dot(lo, gmat)


def _rms(x):
    return lax.rsqrt(jnp.mean(x * x, axis=-1, keepdims=True) + EPS)


def _rms_bwd(dy, x, g, n):
    r = _rms(x)
    gdy = dy * g
    dx = r * gdy - x * (r * r * r) * (jnp.sum(gdy * x, axis=-1, keepdims=True) / n)
    return dx, dy * (x * r)


def _gelu(y):
    t = jnp.tanh(GELU_C * (y + 0.044715 * (y * y * y)))
    return 0.5 * y * (1.0 + t), t


def _full(shape):
    nd = len(shape)
    return pl.BlockSpec(shape, lambda *_: (0,) * nd)


def _sds(shape, dtype=F32):
    return jax.ShapeDtypeStruct(shape, dtype)


def _fwd_proj(x, g1, w_in, gq, gk, gmat, tm=512):
    s = x.shape[0]

    def body(x_ref, g1_ref, w_ref, gq_ref, gk_ref, gm_ref, qk_ref, qn_ref, kn_ref, v_ref, u_ref, xn_ref):
        xv = x_ref[...]
        xnb = ((xv * _rms(xv)) * g1_ref[...]).astype(BF16)
        xn_ref[...] = xnb
        proj = _dot(xnb, w_ref[...])
        q = proj[:, :ATTN_W]
        k = proj[:, ATTN_W:2 * ATTN_W]
        qk_ref[...] = proj[:, :2 * ATTN_W]
        v_ref[...] = proj[:, 2 * ATTN_W:3 * ATTN_W]
        u_ref[...] = proj[:, 3 * ATTN_W:]
        gm = gm_ref[...]
        qn_ref[...] = (q * lax.rsqrt(_group_mean(q * q, gm) + EPS)) * gq_ref[...]
        kn_ref[...] = (k * lax.rsqrt(_group_mean(k * k, gm) + EPS)) * gk_ref[...]

    row = lambda w: pl.BlockSpec((tm, w), lambda i: (i, 0))
    return pl.pallas_call(
        body, name="fwd_proj", grid=(s // tm,),
        in_specs=[row(D_MODEL), _full((1, D_MODEL)), _full((D_MODEL, 4 * ATTN_W)), _full((1, ATTN_W)),
                  _full((1, ATTN_W)), _full((ATTN_W, ATTN_W))],
        out_specs=[row(2 * ATTN_W), row(ATTN_W), row(ATTN_W), row(ATTN_W), row(ATTN_W), row(D_MODEL)],
        out_shape=[_sds((s, 2 * ATTN_W)), _sds((s, ATTN_W)), _sds((s, ATTN_W)), _sds((s, ATTN_W)),
                   _sds((s, ATTN_W)), _sds((s, D_MODEL), BF16)],
        compiler_params=_cp(("parallel",)),
    )(x, g1, w_in, gq, gk, gmat)


def _attn_rows(t, d, nb):
    if d == 1:
        q0 = t * ATTN_BLOCK
        return (t, pl.ds(q0, ATTN_BLOCK), pl.ds(ATTN_CHUNK + q0, ATTN_BLOCK),
                pl.ds(ATTN_CHUNK - ATTN_BLOCK + q0, ATTN_BLOCK))
    r = t // nb
    b = t % nb
    return (b, pl.ds(ATTN_BLOCK * b * d + r, ATTN_BLOCK, stride=d),
            pl.ds(ATTN_CHUNK + ATTN_BLOCK * b * d + r, ATTN_BLOCK, stride=d),
            pl.ds(ATTN_CHUNK + ATTN_BLOCK * (b - 1) * d + r, ATTN_BLOCK, stride=d))


def _attn_masks():
    row = lax.broadcasted_iota(jnp.int32, (ATTN_BLOCK, LANES), 0)
    col = lax.broadcasted_iota(jnp.int32, (ATTN_BLOCK, LANES), 1)
    return row, col


NBLK = ATTN_CHUNK // ATTN_BLOCK


def _attn_bias(bias_s):
    row, col = _attn_masks()
    bias_s[:, pl.ds(0, LANES)] = jnp.where(col >= row, 0.0, NEG_INF)
    bias_s[:, pl.ds(LANES, LANES)] = jnp.where(col <= row, 0.0, NEG_INF)
    return col < HEAD_DIM


def _attn_fwd(qn, kn, v, ex=None, group=4):
    s = qn.shape[0]
    nch = s // ATTN_CHUNK
    scale = HEAD_DIM ** -0.5
    npat = len(DILATIONS)
    n_hp = ATTN_W // LANES

    def body(*refs):
        ((q_ref, kp_ref, kc_ref, vp_ref, vc_ref), (o_ref, lse_ref),
         (kk, vv, kt_s, vb_s, bias_s, m_s, a_s), hx) = _carry(ex, refs, 5, 2)
        i = pl.program_id(1)
        if ex is not None:
            @pl.when(jnp.logical_and(pl.program_id(0) == 0, i == 0))
            def _():
                ex.start(*hx)

        kk[pl.ds(0, ATTN_CHUNK), :] = kp_ref[...]
        kk[pl.ds(ATTN_CHUNK, ATTN_CHUNK), :] = kc_ref[...]
        vv[pl.ds(0, ATTN_CHUNK), :] = vp_ref[...]
        vv[pl.ds(ATTN_CHUNK, ATTN_CHUNK), :] = vc_ref[...]
        head0 = _attn_bias(bias_s)
        first_pen = jnp.where(i > 0, 0.0, NEG_INF)

        for p, d in enumerate(DILATIONS):
            nb = ATTN_CHUNK // (ATTN_BLOCK * d)

            def prep(t, d=d, nb=nb):
                _, _, crows, prows = _attn_rows(t, d, nb)
                kt_s[t, :, pl.ds(0, LANES)] = kk[prows, :].T.astype(BF16)
                kt_s[t, :, pl.ds(LANES, LANES)] = kk[crows, :].T.astype(BF16)
                vp = vv[prows, :]
                vc = vv[crows, :]
                vb_s[2 * t, pl.ds(0, ATTN_BLOCK), :] = jnp.where(head0, vp, 1.0).astype(BF16)
                vb_s[2 * t, pl.ds(ATTN_BLOCK, ATTN_BLOCK), :] = jnp.where(head0, vc, 1.0).astype(BF16)
                vb_s[2 * t + 1, pl.ds(0, ATTN_BLOCK), :] = jnp.where(head0, 1.0, vp).astype(BF16)
                vb_s[2 * t + 1, pl.ds(ATTN_BLOCK, ATTN_BLOCK), :] = jnp.where(head0, 1.0, vc).astype(BF16)

            def main(tg, p=p, d=d, nb=nb):
                st = []
                for g in range(group):
                    t = tg * group + g
                    b, qrows, _, _ = _attn_rows(t, d, nb)
                    q = q_ref[qrows, :] * scale
                    for h in range(2):
                        hm = head0 if h == 0 else jnp.logical_not(head0)
                        st.append(dict(t=t, b=b, qrows=qrows, sc=_dot(jnp.where(hm, q, 0.0).astype(BF16), kt_s[t])))
                for e in st:
                    sc = e["sc"] + bias_s[...]
                    s_p = sc[:, :LANES] + first_pen if e["b"] == 0 else sc[:, :LANES]
                    s_c = sc[:, LANES:]
                    m = jnp.max(jnp.maximum(s_p, s_c), axis=-1, keepdims=True)
                    e["eb"] = jnp.concatenate([jnp.exp(s_p - m), jnp.exp(s_c - m)], axis=1).astype(BF16)
                    e["m"] = jnp.broadcast_to(m, (ATTN_BLOCK, LANES))
                for g in range(group):
                    e0, e1 = st[2 * g], st[2 * g + 1]
                    t = e0["t"]
                    m_s[p, e0["qrows"], :] = jnp.where(head0, e0["m"], e1["m"])
                    a_s[2 * p, e0["qrows"], :] = _dot(e0["eb"], vb_s[2 * t])
                    a_s[2 * p + 1, e0["qrows"], :] = _dot(e1["eb"], vb_s[2 * t + 1])

            for g in range(group):
                prep(g)
            for tg in range(NBLK // group):
                if tg + 1 < NBLK // group:
                    for g in range(group):
                        prep((tg + 1) * group + g)
                main(tg)

        def merge(t, carry):
            rows = pl.ds(pl.multiple_of(t * ATTN_BLOCK, ATTN_BLOCK), ATTN_BLOCK)
            m_all = m_s[0, rows, :]
            for p in range(1, npat):
                m_all = jnp.maximum(m_all, m_s[p, rows, :])
            num = jnp.zeros((ATTN_BLOCK, LANES), F32)
            den = jnp.zeros((ATTN_BLOCK, LANES), F32)
            for p in range(npat):
                w = jnp.exp(m_s[p, rows, :] - m_all)
                a0, a1 = a_s[2 * p, rows, :], a_s[2 * p + 1, rows, :]
                num = num + jnp.where(head0, a0, a1) * w
                den = den + pltpu.roll(jnp.where(head0, a1, a0), HEAD_DIM, 1) * w
            o_ref[rows, :] = num / den
            lse_ref[rows, :] = m_all + jnp.log(den)
            return carry

        lax.fori_loop(0, NBLK, merge, 0, unroll=2)
        if ex is not None:
            @pl.when(jnp.logical_and(pl.program_id(0) == n_hp - 1, i == nch - 1))
            def _():
                ex.wait(*hx)

    cur = pl.BlockSpec((ATTN_CHUNK, LANES), lambda h, i: (i, h))
    prev = pl.BlockSpec((ATTN_CHUNK, LANES), lambda h, i: (jnp.maximum(i - 1, 0), h))
    hosted = ex is not None
    return pl.pallas_call(
        body, name="attn_fwd", grid=(n_hp, nch),
        in_specs=[cur, prev, cur, prev, cur] + (ex.specs() if hosted else []),
        out_specs=[cur, cur] + (ex.specs() if hosted else []),
        out_shape=[_sds((s, ATTN_W)), _sds((s, ATTN_W))] + (ex.out_shape() if hosted else []),
        scratch_shapes=[pltpu.VMEM((2 * ATTN_CHUNK, LANES), F32), pltpu.VMEM((2 * ATTN_CHUNK, LANES), F32),
                        pltpu.VMEM((NBLK, LANES, 2 * LANES), BF16), pltpu.VMEM((2 * NBLK, 2 * ATTN_BLOCK, LANES), BF16),
                        pltpu.VMEM((ATTN_BLOCK, 2 * LANES), F32),
                        pltpu.VMEM((npat, ATTN_CHUNK, LANES), F32), pltpu.VMEM((2 * npat, ATTN_CHUNK, LANES), F32)]
        + (ex.scratch() if hosted else []),
        compiler_params=_cp(("arbitrary", "arbitrary")),
    )(qn, kn, kn, v, v, *(ex.srcs if hosted else []))


def _attn_bwd(qn, kn, v, o, lse, do, ex=None, group=4):
    s = qn.shape[0]
    nch = s // ATTN_CHUNK
    scale = HEAD_DIM ** -0.5
    npat = len(DILATIONS)
    n_hp = ATTN_W // LANES

    def body(*refs):
        ((q_ref, kp_ref, kc_ref, vp_ref, vc_ref, o_ref, lse_ref, do_ref), (dq_ref, dk_ref, dv_ref),
         (kk, vv, dkk, dvv, kt_s, vt_s, kn_s, bias_s, dq_s, dl_s, dkb, dvb), hx) = _carry(ex, refs, 8, 3)
        step = pl.program_id(1)
        i = nch - 1 - step
        if ex is not None:
            @pl.when(jnp.logical_and(pl.program_id(0) == 0, step == 0))
            def _():
                ex.start(*hx)

        kk[pl.ds(0, ATTN_CHUNK), :] = kp_ref[...]
        kk[pl.ds(ATTN_CHUNK, ATTN_CHUNK), :] = kc_ref[...]
        vv[pl.ds(0, ATTN_CHUNK), :] = vp_ref[...]
        vv[pl.ds(ATTN_CHUNK, ATTN_CHUNK), :] = vc_ref[...]

        @pl.when(step == 0)
        def _():
            dkk[pl.ds(ATTN_CHUNK, ATTN_CHUNK), :] = jnp.zeros((ATTN_CHUNK, LANES), F32)
            dvv[pl.ds(ATTN_CHUNK, ATTN_CHUNK), :] = jnp.zeros((ATTN_CHUNK, LANES), F32)

        @pl.when(step > 0)
        def _():
            dkk[pl.ds(ATTN_CHUNK, ATTN_CHUNK), :] = dkk[pl.ds(0, ATTN_CHUNK), :]
            dvv[pl.ds(ATTN_CHUNK, ATTN_CHUNK), :] = dvv[pl.ds(0, ATTN_CHUNK), :]

        dkk[pl.ds(0, ATTN_CHUNK), :] = jnp.zeros((ATTN_CHUNK, LANES), F32)
        dvv[pl.ds(0, ATTN_CHUNK), :] = jnp.zeros((ATTN_CHUNK, LANES), F32)
        head0 = _attn_bias(bias_s)

        def delta(t, carry):
            rows = pl.ds(pl.multiple_of(t * ATTN_BLOCK, ATTN_BLOCK), ATTN_BLOCK)
            prod = do_ref[rows, :] * o_ref[rows, :]
            d0 = jnp.sum(jnp.where(head0, prod, 0.0), axis=-1, keepdims=True)
            d1 = jnp.sum(jnp.where(head0, 0.0, prod), axis=-1, keepdims=True)
            dl_s[rows, :] = jnp.where(head0, d0, d1)
            return carry

        lax.fori_loop(0, NBLK, delta, 0, unroll=2)

        first_pen = jnp.where(i > 0, 0.0, NEG_INF)

        for p, d in enumerate(DILATIONS):
            nb = ATTN_CHUNK // (ATTN_BLOCK * d)

            def prep(t, d=d, nb=nb):
                _, _, crows, prows = _attn_rows(t, d, nb)
                kp, kc = kk[prows, :], kk[crows, :]
                kt_s[t, :, pl.ds(0, LANES)] = kp.T.astype(BF16)
                kt_s[t, :, pl.ds(LANES, LANES)] = kc.T.astype(BF16)
                kn_s[t, pl.ds(0, ATTN_BLOCK), :] = (kp * scale).astype(BF16)
                kn_s[t, pl.ds(ATTN_BLOCK, ATTN_BLOCK), :] = (kc * scale).astype(BF16)
                vt_s[t, :, pl.ds(0, LANES)] = vv[prows, :].T.astype(BF16)
                vt_s[t, :, pl.ds(LANES, LANES)] = vv[crows, :].T.astype(BF16)

            def main(tg, p=p, d=d, nb=nb):
                st = []
                for g in range(group):
                    t = tg * group + g
                    b, qrows, _, _ = _attn_rows(t, d, nb)
                    q = q_ref[qrows, :] * scale
                    dout = do_ref[qrows, :]
                    lse_b = lse_ref[qrows, :]
                    dl_b = dl_s[qrows, :]
                    for h in range(2):
                        hm = head0 if h == 0 else jnp.logical_not(head0)
                        c0 = h * HEAD_DIM
                        qh = jnp.where(hm, q, 0.0).astype(BF16)
                        doh = jnp.where(hm, dout, 0.0).astype(BF16)
                        st.append(dict(t=t, b=b, qrows=qrows, qh=qh, doh=doh, lse=lse_b[:, c0:c0 + 1],
                                       dl=dl_b[:, c0:c0 + 1], sc=_dot(qh, kt_s[t]), dp=_dot(doh, vt_s[t])))
                for e in st:
                    sc = e["sc"] + bias_s[...]
                    if e["b"] == 0:
                        sc = jnp.concatenate([sc[:, :LANES] + first_pen, sc[:, LANES:]], axis=1)
                    pr = jnp.exp(sc - e["lse"])
                    e["ds"] = (pr * (e["dp"] - e["dl"])).astype(BF16)
                    e["pr"] = pr.astype(BF16)
                for g in range(group):
                    e0, e1 = st[2 * g], st[2 * g + 1]
                    t = e0["t"]
                    dq_s[p, e0["qrows"], :] = jnp.where(head0, _dot(e0["ds"], kn_s[t]), _dot(e1["ds"], kn_s[t]))
                    dkb[t] = _dot_tn(e0["ds"], e0["qh"]) + _dot_tn(e1["ds"], e1["qh"])
                    dvb[t] = _dot_tn(e0["pr"], e0["doh"]) + _dot_tn(e1["pr"], e1["doh"])

            def scatter(t, d=d, nb=nb):
                _, _, crows, prows = _attn_rows(t, d, nb)
                dkk[prows, :] = dkk[prows, :] + dkb[t, pl.ds(0, ATTN_BLOCK), :]
                dkk[crows, :] = dkk[crows, :] + dkb[t, pl.ds(ATTN_BLOCK, ATTN_BLOCK), :]
                dvv[prows, :] = dvv[prows, :] + dvb[t, pl.ds(0, ATTN_BLOCK), :]
                dvv[crows, :] = dvv[crows, :] + dvb[t, pl.ds(ATTN_BLOCK, ATTN_BLOCK), :]

            n_groups = NBLK // group
            for g in range(group):
                prep(g)
            for tg in range(n_groups):
                if tg + 1 < n_groups:
                    for g in range(group):
                        prep((tg + 1) * group + g)
                main(tg)
                if tg >= 1:
                    for g in range(group):
                        scatter((tg - 1) * group + g)
            for g in range(group):
                scatter((n_groups - 1) * group + g)

        def finish(t, carry):
            rows = pl.ds(pl.multiple_of(t * ATTN_BLOCK, ATTN_BLOCK), ATTN_BLOCK)
            acc = dq_s[0, rows, :]
            for p in range(1, npat):
                acc = acc + dq_s[p, rows, :]
            dq_ref[rows, :] = acc
            return carry

        lax.fori_loop(0, NBLK, finish, 0, unroll=2)
        dk_ref[...] = dkk[pl.ds(ATTN_CHUNK, ATTN_CHUNK), :]
        dv_ref[...] = dvv[pl.ds(ATTN_CHUNK, ATTN_CHUNK), :]
        if ex is not None:
            @pl.when(jnp.logical_and(pl.program_id(0) == n_hp - 1, step == nch - 1))
            def _():
                ex.wait(*hx)

    cur = pl.BlockSpec((ATTN_CHUNK, LANES), lambda h, t: (nch - 1 - t, h))
    prev = pl.BlockSpec((ATTN_CHUNK, LANES), lambda h, t: (jnp.maximum(nch - 2 - t, 0), h))
    big = pltpu.VMEM((2 * ATTN_CHUNK, LANES), F32)
    pair_t = pltpu.VMEM((NBLK, LANES, 2 * LANES), BF16)
    hosted = ex is not None
    return pl.pallas_call(
        body, name="attn_bwd", grid=(n_hp, nch),
        in_specs=[cur, prev, cur, prev, cur, cur, cur, cur] + (ex.specs() if hosted else []),
        out_specs=[cur, cur, cur] + (ex.specs() if hosted else []),
        out_shape=[_sds((s, ATTN_W))] * 3 + (ex.out_shape() if hosted else []),
        scratch_shapes=[big, big, big, big, pair_t, pair_t, pltpu.VMEM((NBLK, 2 * ATTN_BLOCK, LANES), BF16),
                        pltpu.VMEM((ATTN_BLOCK, 2 * LANES), F32),
                        pltpu.VMEM((npat, ATTN_CHUNK, LANES), F32), pltpu.VMEM((ATTN_CHUNK, LANES), F32),
                        pltpu.VMEM((NBLK, 2 * ATTN_BLOCK, LANES), F32), pltpu.VMEM((NBLK, 2 * ATTN_BLOCK, LANES), F32)]
        + (ex.scratch() if hosted else []),
        compiler_params=_cp(("arbitrary", "arbitrary")),
    )(qn, kn, kn, v, v, o, lse, do, *(ex.srcs if hosted else []))


def _attn_fwd_v1(qn, kn, v):
    s = qn.shape[0]
    nch = s // ATTN_CHUNK
    scale = HEAD_DIM ** -0.5
    npat = len(DILATIONS)

    def body(q_ref, kp_ref, kc_ref, vp_ref, vc_ref, o_ref, lse_ref, kk, vv, m_s, l_s, acc_s):
        i = pl.program_id(1)
        kk[pl.ds(0, ATTN_CHUNK), :] = kp_ref[...]
        kk[pl.ds(ATTN_CHUNK, ATTN_CHUNK), :] = kc_ref[...]
        vv[pl.ds(0, ATTN_CHUNK), :] = vp_ref[...]
        vv[pl.ds(ATTN_CHUNK, ATTN_CHUNK), :] = vc_ref[...]
        row, col = _attn_masks()
        head0 = col < HEAD_DIM
        mask_cur = col <= row
        diff = col - row

        for p, d in enumerate(DILATIONS):
            nb = ATTN_CHUNK // (ATTN_BLOCK * d)

            def blk(t, carry, p=p, d=d, nb=nb):
                b, qrows, crows, prows = _attn_rows(t, d, nb)
                q = q_ref[qrows, :]
                kc = kk[crows, :].astype(BF16)
                kp = kk[prows, :].astype(BF16)
                vc = vv[crows, :].astype(BF16)
                vp = vv[prows, :].astype(BF16)
                thr = jnp.where(jnp.logical_or(i > 0, b > 0), 0, 4 * ATTN_BLOCK)
                mask_prev = diff >= thr
                accs, ms, ls = [], [], []
                for h in range(2):
                    hm = head0 if h == 0 else jnp.logical_not(head0)
                    qh = jnp.where(hm, q, 0.0).astype(BF16)
                    s_p = jnp.where(mask_prev, _dot_nt(qh, kp) * scale, NEG_INF)
                    s_c = jnp.where(mask_cur, _dot_nt(qh, kc) * scale, NEG_INF)
                    m = jnp.maximum(jnp.max(s_p, axis=-1, keepdims=True), jnp.max(s_c, axis=-1, keepdims=True))
                    e_p = jnp.exp(s_p - m)
                    e_c = jnp.exp(s_c - m)
                    l = jnp.sum(e_p, axis=-1, keepdims=True) + jnp.sum(e_c, axis=-1, keepdims=True)
                    accs.append(_dot(e_p.astype(BF16), vp) + _dot(e_c.astype(BF16), vc))
                    ms.append(jnp.broadcast_to(m, (ATTN_BLOCK, LANES)))
                    ls.append(jnp.broadcast_to(l, (ATTN_BLOCK, LANES)))
                m_s[p, qrows, :] = jnp.where(head0, ms[0], ms[1])
                l_s[p, qrows, :] = jnp.where(head0, ls[0], ls[1])
                acc_s[p, qrows, :] = jnp.where(head0, accs[0], accs[1])
                return carry

            lax.fori_loop(0, ATTN_CHUNK // ATTN_BLOCK, blk, 0)

        def merge(t, carry):
            rows = pl.ds(pl.multiple_of(t * ATTN_BLOCK, ATTN_BLOCK), ATTN_BLOCK)
            m_all = m_s[0, rows, :]
            for p in range(1, npat):
                m_all = jnp.maximum(m_all, m_s[p, rows, :])
            num = jnp.zeros((ATTN_BLOCK, LANES), F32)
            den = jnp.zeros((ATTN_BLOCK, LANES), F32)
            for p in range(npat):
                w = jnp.exp(m_s[p, rows, :] - m_all)
                num = num + acc_s[p, rows, :] * w
                den = den + l_s[p, rows, :] * w
            o_ref[rows, :] = num / den
            lse_ref[rows, :] = m_all + jnp.log(den)
            return carry

        lax.fori_loop(0, ATTN_CHUNK // ATTN_BLOCK, merge, 0)

    cur = pl.BlockSpec((ATTN_CHUNK, LANES), lambda h, i: (i, h))
    prev = pl.BlockSpec((ATTN_CHUNK, LANES), lambda h, i: (jnp.maximum(i - 1, 0), h))
    return pl.pallas_call(
        body, name="attn_fwd", grid=(ATTN_W // LANES, nch),
        in_specs=[cur, prev, cur, prev, cur],
        out_specs=[cur, cur],
        out_shape=[_sds((s, ATTN_W)), _sds((s, ATTN_W))],
        scratch_shapes=[pltpu.VMEM((2 * ATTN_CHUNK, LANES), F32), pltpu.VMEM((2 * ATTN_CHUNK, LANES), F32),
                        pltpu.VMEM((npat, ATTN_CHUNK, LANES), F32), pltpu.VMEM((npat, ATTN_CHUNK, LANES), F32),
                        pltpu.VMEM((npat, ATTN_CHUNK, LANES), F32)],
        compiler_params=_cp(("parallel", "parallel")),
    )(qn, kn, kn, v, v)


def _attn_bwd_v1(qn, kn, v, o, lse, do):
    s = qn.shape[0]
    nch = s // ATTN_CHUNK
    scale = HEAD_DIM ** -0.5
    npat = len(DILATIONS)

    def body(q_ref, kp_ref, kc_ref, vp_ref, vc_ref, o_ref, lse_ref, do_ref, dq_ref, dk_ref, dv_ref,
             kk, vv, dkk, dvv, dq_s, dl_s):
        step = pl.program_id(1)
        i = nch - 1 - step
        kk[pl.ds(0, ATTN_CHUNK), :] = kp_ref[...]
        kk[pl.ds(ATTN_CHUNK, ATTN_CHUNK), :] = kc_ref[...]
        vv[pl.ds(0, ATTN_CHUNK), :] = vp_ref[...]
        vv[pl.ds(ATTN_CHUNK, ATTN_CHUNK), :] = vc_ref[...]

        @pl.when(step == 0)
        def _():
            dkk[pl.ds(ATTN_CHUNK, ATTN_CHUNK), :] = jnp.zeros((ATTN_CHUNK, LANES), F32)
            dvv[pl.ds(ATTN_CHUNK, ATTN_CHUNK), :] = jnp.zeros((ATTN_CHUNK, LANES), F32)

        @pl.when(step > 0)
        def _():
            dkk[pl.ds(ATTN_CHUNK, ATTN_CHUNK), :] = dkk[pl.ds(0, ATTN_CHUNK), :]
            dvv[pl.ds(ATTN_CHUNK, ATTN_CHUNK), :] = dvv[pl.ds(0, ATTN_CHUNK), :]

        dkk[pl.ds(0, ATTN_CHUNK), :] = jnp.zeros((ATTN_CHUNK, LANES), F32)
        dvv[pl.ds(0, ATTN_CHUNK), :] = jnp.zeros((ATTN_CHUNK, LANES), F32)

        row, col = _attn_masks()
        head0 = col < HEAD_DIM
        mask_cur = col <= row
        diff = col - row

        def delta(t, carry):
            rows = pl.ds(pl.multiple_of(t * ATTN_BLOCK, ATTN_BLOCK), ATTN_BLOCK)
            prod = do_ref[rows, :] * o_ref[rows, :]
            d0 = jnp.sum(jnp.where(head0, prod, 0.0), axis=-1, keepdims=True)
            d1 = jnp.sum(jnp.where(head0, 0.0, prod), axis=-1, keepdims=True)
            dl_s[rows, :] = jnp.where(head0, d0, d1)
            return carry

        lax.fori_loop(0, ATTN_CHUNK // ATTN_BLOCK, delta, 0)

        for p, d in enumerate(DILATIONS):
            nb = ATTN_CHUNK // (ATTN_BLOCK * d)

            def blk(t, carry, p=p, d=d, nb=nb):
                b, qrows, crows, prows = _attn_rows(t, d, nb)
                q = q_ref[qrows, :]
                dout = do_ref[qrows, :]
                lse_b = lse_ref[qrows, :]
                dl_b = dl_s[qrows, :]
                kc = kk[crows, :].astype(BF16)
                kp = kk[prows, :].astype(BF16)
                vc = vv[crows, :].astype(BF16)
                vp = vv[prows, :].astype(BF16)
                thr = jnp.where(jnp.logical_or(i > 0, b > 0), 0, 4 * ATTN_BLOCK)
                mask_prev = diff >= thr
                dqs = []
                dk_p = jnp.zeros((ATTN_BLOCK, LANES), F32)
                dk_c = jnp.zeros((ATTN_BLOCK, LANES), F32)
                dv_p = jnp.zeros((ATTN_BLOCK, LANES), F32)
                dv_c = jnp.zeros((ATTN_BLOCK, LANES), F32)
                for h in range(2):
                    hm = head0 if h == 0 else jnp.logical_not(head0)
                    c0 = h * HEAD_DIM
                    qh = jnp.where(hm, q, 0.0).astype(BF16)
                    doh = jnp.where(hm, dout, 0.0).astype(BF16)
                    lse_h = lse_b[:, c0:c0 + 1]
                    dl_h = dl_b[:, c0:c0 + 1]
                    s_p = jnp.where(mask_prev, _dot_nt(qh, kp) * scale, NEG_INF)
                    s_c = jnp.where(mask_cur, _dot_nt(qh, kc) * scale, NEG_INF)
                    p_p = jnp.exp(s_p - lse_h)
                    p_c = jnp.exp(s_c - lse_h)
                    ds_p = (p_p * (_dot_nt(doh, vp) - dl_h) * scale).astype(BF16)
                    ds_c = (p_c * (_dot_nt(doh, vc) - dl_h) * scale).astype(BF16)
                    dqs.append(_dot(ds_p, kp) + _dot(ds_c, kc))
                    dk_p = dk_p + _dot_tn(ds_p, qh)
                    dk_c = dk_c + _dot_tn(ds_c, qh)
                    dv_p = dv_p + _dot_tn(p_p.astype(BF16), doh)
                    dv_c = dv_c + _dot_tn(p_c.astype(BF16), doh)
                dq_s[p, qrows, :] = jnp.where(head0, dqs[0], dqs[1])
                dkk[prows, :] = dkk[prows, :] + dk_p
                dkk[crows, :] = dkk[crows, :] + dk_c
                dvv[prows, :] = dvv[prows, :] + dv_p
                dvv[crows, :] = dvv[crows, :] + dv_c
                return carry

            lax.fori_loop(0, ATTN_CHUNK // ATTN_BLOCK, blk, 0)

        def finish(t, carry):
            rows = pl.ds(pl.multiple_of(t * ATTN_BLOCK, ATTN_BLOCK), ATTN_BLOCK)
            acc = dq_s[0, rows, :]
            for p in range(1, npat):
                acc = acc + dq_s[p, rows, :]
            dq_ref[rows, :] = acc
            return carry

        lax.fori_loop(0, ATTN_CHUNK // ATTN_BLOCK, finish, 0)
        dk_ref[...] = dkk[pl.ds(ATTN_CHUNK, ATTN_CHUNK), :]
        dv_ref[...] = dvv[pl.ds(ATTN_CHUNK, ATTN_CHUNK), :]

    cur = pl.BlockSpec((ATTN_CHUNK, LANES), lambda h, t: (nch - 1 - t, h))
    prev = pl.BlockSpec((ATTN_CHUNK, LANES), lambda h, t: (jnp.maximum(nch - 2 - t, 0), h))
    big = pltpu.VMEM((2 * ATTN_CHUNK, LANES), F32)
    return pl.pallas_call(
        body, name="attn_bwd", grid=(ATTN_W // LANES, nch),
        in_specs=[cur, prev, cur, prev, cur, cur, cur, cur],
        out_specs=[cur, cur, cur],
        out_shape=[_sds((s, ATTN_W))] * 3,
        scratch_shapes=[big, big, big, big, pltpu.VMEM((npat, ATTN_CHUNK, LANES), F32),
                        pltpu.VMEM((ATTN_CHUNK, LANES), F32)],
        compiler_params=_cp(("parallel", "arbitrary")),
    )(qn, kn, kn, v, v, o, lse, do)


def _discretize(lr, li, dt):
    mag = jnp.exp(lr * dt)
    abr = mag * jnp.cos(li * dt)
    abi = mag * jnp.sin(li * dt)
    den = lr * lr + li * li
    nr, ni = abr - 1.0, abi
    cr = (nr * lr + ni * li) / den
    ci = (ni * lr - nr * li) / den
    return abr, abi, den, nr, ni, cr, ci


def _ssm_discretize(a_re, a_im, log_dt, b_re_t, b_im_t):
    def body(ar_ref, ai_ref, ldt_ref, br_ref, bi_ref, abr_ref, abi_ref, bbr_ref, bbi_ref):
        abr, abi, _, _, _, cr, ci = _discretize(ar_ref[...], ai_ref[...], jnp.exp(ldt_ref[...]))
        br, bi = br_ref[...], bi_ref[...]
        abr_ref[...] = abr
        abi_ref[...] = abi
        bbr_ref[...] = cr * br - ci * bi
        bbi_ref[...] = cr * bi + ci * br

    return pl.pallas_call(
        body, name="ssm_discretize",
        out_shape=[_sds(a_re.shape)] * 2 + [_sds(b_re_t.shape)] * 2,
    )(a_re, a_im, log_dt, b_re_t, b_im_t)


def _ssm_discretize_bwd(a_re, a_im, log_dt, b_re_t, b_im_t, dabr, dabi, dbbr, dbbi):
    def body(ar_ref, ai_ref, ldt_ref, br_ref, bi_ref, dabr_ref, dabi_ref, dbbr_ref, dbbi_ref,
             dar_ref, dai_ref, dldt_ref, dbr_ref, dbi_ref):
        lr, li = ar_ref[...], ai_ref[...]
        dt = jnp.exp(ldt_ref[...])
        abr, abi, den, nr, ni, cr, ci = _discretize(lr, li, dt)
        br, bi = br_ref[...], bi_ref[...]
        gbr, gbi = dbbr_ref[...], dbbi_ref[...]
        dcr = jnp.sum(gbr * br + gbi * bi, axis=1, keepdims=True)
        dci = jnp.sum(gbi * br - gbr * bi, axis=1, keepdims=True)
        dbr_ref[...] = cr * gbr + ci * gbi
        dbi_ref[...] = cr * gbi - ci * gbr
        dnr = (dcr * lr - dci * li) / den
        dni = (dcr * li + dci * lr) / den
        dden = -(dcr * cr + dci * ci) / den
        dlr = (dcr * nr + dci * ni) / den + dden * 2.0 * lr
        dli = (dcr * ni - dci * nr) / den + dden * 2.0 * li
        gabr = dabr_ref[...] + dnr
        gabi = dabi_ref[...] + dni
        dphi = gabr * abr + gabi * abi
        dth = gabi * abr - gabr * abi
        dar_ref[...] = dlr + dphi * dt
        dai_ref[...] = dli + dth * dt
        dldt_ref[...] = jnp.sum(dphi * lr + dth * li, axis=2, keepdims=True) * dt

    return pl.pallas_call(
        body, name="ssm_discretize_bwd",
        out_shape=[_sds(a_re.shape)] * 2 + [_sds(log_dt.shape)] + [_sds(b_re_t.shape)] * 2,
    )(a_re, a_im, log_dt, b_re_t, b_im_t, dabr, dabi, dbbr, dbbi)


def _ssm_power(abr, abi, n_sq):
    def body(r_ref, i_ref, or_ref, oi_ref):
        r, i = r_ref[...], i_ref[...]
        for _ in range(n_sq):
            r, i = r * r - i * i, 2.0 * r * i
        or_ref[...] = r
        oi_ref[...] = i

    return pl.pallas_call(body, name="ssm_power", out_shape=[_sds(abr.shape)] * 2)(abr, abi)


N_CB = SSM_W // LANES
CB_STATES = N_STATE // N_CB
ROWS = N_SEG * SSM_LK
SCAN_W = CB_STATES


class _Neg:
    def __init__(self, ref):
        self.ref = ref

    def __getitem__(self, idx):
        return -self.ref[idx]


def _seg_init(fin_r, fin_i, pw_r, pw_i, x_r, x_i, reverse):
    zero = jnp.zeros((1, N_STATE), F32)
    cr, ci = zero, zero
    order = range(N_SEG - 1, -1, -1) if reverse else range(N_SEG)
    pr = pw_r[...]
    pi = -pw_i[...] if reverse else pw_i[...]
    for j in order:
        x_r[pl.ds(j, 1), :] = cr
        x_i[pl.ds(j, 1), :] = ci
        fr, fi = fin_r[pl.ds(j, 1), :], fin_i[pl.ds(j, 1), :]
        cr, ci = fr + pr * cr - pi * ci, fi + pr * ci + pi * cr


def _scan_rows(a_r, a_i, b_r, b_i, x_r, x_i, o_r, o_i, b_off, n_steps, reverse):
    w = 512
    for c in range(N_STATE // w):
        cols = pl.ds(c * w, w)
        ar = jnp.broadcast_to(a_r[:, cols], (N_SEG, w))
        ai = jnp.broadcast_to(a_i[:, cols], (N_SEG, w))

        def step(t, carry, cols=cols, ar=ar, ai=ai):
            xr, xi = carry
            k = (n_steps - 1 - t) if reverse else t
            rows = pl.ds(pl.multiple_of(k * N_SEG + b_off, N_SEG), N_SEG)
            nr = ar * xr - ai * xi + b_r[rows, cols]
            ni = ar * xi + ai * xr + b_i[rows, cols]
            o_r[rows, cols] = nr
            o_i[rows, cols] = ni
            return nr, ni

        xr, xi = lax.fori_loop(0, n_steps, step, (x_r[:, cols], x_i[:, cols]), unroll=4)
        x_r[:, cols] = xr
        x_i[:, cols] = xi


def _permute_in(src_ref, dst):
    for c in range(N_CB):
        dst[c] = src_ref[:, :, pl.ds(c * LANES, LANES)].reshape(ROWS, LANES)


def _permute_out(src, dst_ref):
    for c in range(N_CB):
        dst_ref[:, :, pl.ds(c * LANES, LANES)] = src[c].reshape(SSM_LK, N_SEG, LANES)


def _ssm_fwd_v1(u3, abr, abi, pw_r, pw_i, fin_r, fin_i, bb_r, bb_i, cc_r, cc_i, dskip, finals_only):
    sl = u3.shape[0]
    nch = sl // SSM_LK

    def body(u_ref, abr_ref, abi_ref, pwr_ref, pwi_ref, finr_ref, fini_ref, bbr_ref, bbi_ref,
             ccr_ref, cci_ref, d_ref, *rest):
        if finals_only:
            xfr_ref, xfi_ref, up, xs_r, xs_i, x_r, x_i = rest
        else:
            y_ref, xsr_ref, xsi_ref, up, yp, xs_r, xs_i, x_r, x_i = rest
        k = pl.program_id(0)

        @pl.when(k == 0)
        def _():
            _seg_init(finr_ref, fini_ref, pwr_ref, pwi_ref, x_r, x_i, False)

        if not finals_only:
            xsr_ref[0] = x_r[...]
            xsi_ref[0] = x_i[...]
        _permute_in(u_ref, up)
        for c in range(N_CB):
            lhs = up[c].astype(BF16)
            xs_r[:, pl.ds(c * CB_STATES, CB_STATES)] = _dot(lhs, bbr_ref[c])
            xs_i[:, pl.ds(c * CB_STATES, CB_STATES)] = _dot(lhs, bbi_ref[c])
        _scan_rows(abr_ref, abi_ref, xs_r, xs_i, x_r, x_i, xs_r, xs_i, 0, SSM_LK, False)
        if finals_only:
            @pl.when(k == nch - 1)
            def _():
                xfr_ref[...] = x_r[...]
                xfi_ref[...] = x_i[...]
        else:
            for c in range(N_CB):
                cols = pl.ds(c * CB_STATES, CB_STATES)
                yp[c] = (_dot(xs_r[:, cols].astype(BF16), ccr_ref[c]) - _dot(xs_i[:, cols].astype(BF16), cci_ref[c])
                         + d_ref[:, pl.ds(c * LANES, LANES)] * up[c])
            _permute_out(yp, y_ref)

    ublk = pl.BlockSpec((SSM_LK, N_SEG, SSM_W), lambda k: (k, 0, 0))
    st = pl.BlockSpec((1, N_SEG, N_STATE), lambda k: (k, 0, 0))
    vec = _full((1, N_STATE))
    mat = _full((N_SEG, N_STATE))
    chunk = pltpu.VMEM((N_CB, ROWS, LANES), F32)
    big = pltpu.VMEM((ROWS, N_STATE), F32)
    small = pltpu.VMEM((N_SEG, N_STATE), F32)
    if finals_only:
        out_specs, out_shape = [mat, mat], [_sds((N_SEG, N_STATE))] * 2
        scratch, name = [chunk, big, big, small, small], "ssm_fwd_finals"
    else:
        out_specs = [ublk, st, st]
        out_shape = [_sds(u3.shape)] + [_sds((nch, N_SEG, N_STATE))] * 2
        scratch, name = [chunk, chunk, big, big, small, small], "ssm_fwd"
    return pl.pallas_call(
        body, name=name, grid=(nch,),
        in_specs=[ublk, vec, vec, vec, vec, mat, mat,
                  _full((N_CB, LANES, CB_STATES)), _full((N_CB, LANES, CB_STATES)),
                  _full((N_CB, CB_STATES, LANES)), _full((N_CB, CB_STATES, LANES)), _full((1, SSM_W))],
        out_specs=out_specs, out_shape=out_shape, scratch_shapes=scratch,
        compiler_params=_cp(("arbitrary",)),
    )(u3, abr, abi, pw_r, pw_i, fin_r, fin_i, bb_r, bb_i, cc_r, cc_i, dskip)


def _ssm_bwd_v1(u3, dy3, xst_r, xst_i, abr, abi, pw_r, pw_i, fin_r, fin_i, bb_r, bb_i, bbt_r, bbt_i,
                cct_r, cct_i, dskip, finals_only):
    sl = u3.shape[0]
    nch = sl // SSM_LK

    def body(u_ref, g_ref, xsr_ref, xsi_ref, abr_ref, abi_ref, pwr_ref, pwi_ref,
             finr_ref, fini_ref, bbr_ref, bbi_ref, btr_ref, bti_ref, ctr_ref, cti_ref, d_ref, *rest):
        if finals_only:
            lfr_ref, lfi_ref, gp, l_r, l_i, lam_r, lam_i = rest
        else:
            (du_ref, dar_ref, dai_ref, dbr_ref, dbi_ref, dcr_ref, dci_ref, dd_ref,
             gp, up, yp, l_r, l_i, lam_r, lam_i, x_r, x_i, xx_r, xx_i, sar, sai, sdd) = rest
        t = pl.program_id(0)

        @pl.when(t == 0)
        def _():
            _seg_init(finr_ref, fini_ref, pwr_ref, pwi_ref, lam_r, lam_i, True)
            if not finals_only:
                sar[...] = jnp.zeros_like(sar)
                sai[...] = jnp.zeros_like(sai)
                sdd[...] = jnp.zeros_like(sdd)
                dbr_ref[...] = jnp.zeros_like(dbr_ref)
                dbi_ref[...] = jnp.zeros_like(dbi_ref)
                dcr_ref[...] = jnp.zeros_like(dcr_ref)
                dci_ref[...] = jnp.zeros_like(dci_ref)

        _permute_in(g_ref, gp)
        for c in range(N_CB):
            cols = pl.ds(c * CB_STATES, CB_STATES)
            lhs = gp[c].astype(BF16)
            l_r[:, cols] = _dot(lhs, ctr_ref[c])
            l_i[:, cols] = -_dot(lhs, cti_ref[c])
        if not finals_only:
            _permute_in(u_ref, up)
            x_r[...] = xsr_ref[0]
            x_i[...] = xsi_ref[0]
            xx_r[pl.ds(0, N_SEG), :] = x_r[...]
            xx_i[pl.ds(0, N_SEG), :] = x_i[...]
            for c in range(N_CB):
                lhs = up[c].astype(BF16)
                xx_r[pl.ds(N_SEG, ROWS), pl.ds(c * CB_STATES, CB_STATES)] = _dot(lhs, bbr_ref[c])
                xx_i[pl.ds(N_SEG, ROWS), pl.ds(c * CB_STATES, CB_STATES)] = _dot(lhs, bbi_ref[c])
            _scan_rows(abr_ref, abi_ref, xx_r, xx_i, x_r, x_i, xx_r, xx_i, N_SEG, SSM_LK, False)
        _scan_rows(abr_ref, _Neg(abi_ref), l_r, l_i, lam_r, lam_i, l_r, l_i, 0, SSM_LK, True)
        if finals_only:
            @pl.when(t == nch - 1)
            def _():
                lfr_ref[...] = lam_r[...]
                lfi_ref[...] = lam_i[...]
        else:
            w = 512
            for cc in range(N_STATE // w):
                cols = pl.ds(cc * w, w)

                def acc_step(kk_, carry, cols=cols):
                    sr, si = carry
                    rows = pl.ds(pl.multiple_of(kk_ * N_SEG, N_SEG), N_SEG)
                    lr, li = l_r[rows, cols], l_i[rows, cols]
                    pr, pi = xx_r[rows, cols], xx_i[rows, cols]
                    return sr + lr * pr + li * pi, si + li * pr - lr * pi

                z = jnp.zeros((N_SEG, w), F32)
                sr, si = lax.fori_loop(0, SSM_LK, acc_step, (z, z), unroll=4)
                sar[:, cols] += sr
                sai[:, cols] += si
            for c in range(N_CB):
                cols = pl.ds(c * CB_STATES, CB_STATES)
                lrb = l_r[:, cols].astype(BF16)
                lib = l_i[:, cols].astype(BF16)
                ub = up[c].astype(BF16)
                gb = gp[c].astype(BF16)
                dbr_ref[c] += _dot_tn(lrb, ub)
                dbi_ref[c] += _dot_tn(lib, ub)
                dcr_ref[c] += _dot_tn(gb, xx_r[pl.ds(N_SEG, ROWS), cols].astype(BF16))
                dci_ref[c] += -_dot_tn(gb, xx_i[pl.ds(N_SEG, ROWS), cols].astype(BF16))
                yp[c] = _dot(lrb, btr_ref[c]) + _dot(lib, bti_ref[c]) + d_ref[:, pl.ds(c * LANES, LANES)] * gp[c]
                prod = gp[c] * up[c]
                sdd[:, pl.ds(c * LANES, LANES)] += jnp.sum(prod.reshape(SSM_LK, N_SEG, LANES), axis=0)
            _permute_out(yp, du_ref)

            @pl.when(t == nch - 1)
            def _():
                dar_ref[...] = jnp.sum(sar[...], axis=0, keepdims=True)
                dai_ref[...] = jnp.sum(sai[...], axis=0, keepdims=True)
                dd_ref[...] = jnp.sum(sdd[...], axis=0, keepdims=True)

    ublk = pl.BlockSpec((SSM_LK, N_SEG, SSM_W), lambda t: (nch - 1 - t, 0, 0))
    st = pl.BlockSpec((1, N_SEG, N_STATE), lambda t: (nch - 1 - t, 0, 0))
    vec = _full((1, N_STATE))
    mat = _full((N_SEG, N_STATE))
    cs = _full((N_CB, LANES, CB_STATES))
    sc = _full((N_CB, CB_STATES, LANES))
    in_specs = [ublk, ublk, st, st, vec, vec, vec, vec, mat, mat, cs, cs, sc, sc, cs, cs, _full((1, SSM_W))]
    chunk = pltpu.VMEM((N_CB, ROWS, LANES), F32)
    big = pltpu.VMEM((ROWS, N_STATE), F32)
    small = pltpu.VMEM((N_SEG, N_STATE), F32)
    if finals_only:
        out_specs, out_shape = [mat, mat], [_sds((N_SEG, N_STATE))] * 2
        scratch, name = [chunk, big, big, small, small], "ssm_bwd_finals"
    else:
        out_specs = [ublk, vec, vec, sc, sc, cs, cs, _full((1, SSM_W))]
        out_shape = ([_sds(u3.shape), _sds((1, N_STATE)), _sds((1, N_STATE))]
                     + [_sds((N_CB, CB_STATES, LANES))] * 2 + [_sds((N_CB, LANES, CB_STATES))] * 2
                     + [_sds((1, SSM_W))])
        xx = pltpu.VMEM((ROWS + N_SEG, N_STATE), F32)
        scratch = [chunk, chunk, chunk, big, big, small, small, small, small, xx, xx, small, small,
                   pltpu.VMEM((N_SEG, SSM_W), F32)]
        name = "ssm_bwd"
    return pl.pallas_call(
        body, name=name, grid=(nch,), in_specs=in_specs, out_specs=out_specs, out_shape=out_shape,
        scratch_shapes=scratch, compiler_params=_cp(("arbitrary",)),
    )(u3, dy3, xst_r, xst_i, abr, abi, pw_r, pw_i, fin_r, fin_i, bb_r, bb_i, bbt_r, bbt_i, cct_r, cct_i, dskip)


def _scan_block(a_r, a_i, c, b_r, b_i, b_off, x_r, x_i, reverse, acc=None):
    for part in range(CB_STATES // SCAN_W):
        here = pl.ds(part * SCAN_W, SCAN_W)
        cols = pl.ds(c * CB_STATES + part * SCAN_W, SCAN_W)
        ar = jnp.broadcast_to(a_r[:, cols], (N_SEG, SCAN_W))
        ai = jnp.broadcast_to(a_i[:, cols], (N_SEG, SCAN_W))
        xr, xi = x_r[:, cols], x_i[:, cols]
        if acc is not None:
            sr = jnp.zeros((N_SEG, SCAN_W), F32)
            si = jnp.zeros((N_SEG, SCAN_W), F32)
        for t in range(SSM_LK):
            k = (SSM_LK - 1 - t) if reverse else t
            rows = pl.ds(k * N_SEG + b_off, N_SEG)
            xr, xi = ar * xr - ai * xi + b_r[rows, here], ar * xi + ai * xr + b_i[rows, here]
            b_r[rows, here] = xr
            b_i[rows, here] = xi
            if acc is not None:
                pr, pi = acc[0][pl.ds(k * N_SEG, N_SEG), here], acc[1][pl.ds(k * N_SEG, N_SEG), here]
                sr = sr + xr * pr + xi * pi
                si = si + xi * pr - xr * pi
        x_r[:, cols] = xr
        x_i[:, cols] = xi
        if acc is not None:
            acc[2][:, cols] += sr
            acc[3][:, cols] += si


def _ssm_fwd(u3, abr, abi, pw_r, pw_i, fin_r, fin_i, bb_r, bb_i, cc_r, cc_i, dskip, finals_only):
    sl = u3.shape[0]
    nch = sl // SSM_LK

    def body(u_ref, abr_ref, abi_ref, pwr_ref, pwi_ref, finr_ref, fini_ref, bbr_ref, bbi_ref,
             ccr_ref, cci_ref, d_ref, *rest):
        if finals_only:
            xfr_ref, xfi_ref, up, x_r, x_i = rest[:5]
        else:
            y_ref, xsr_ref, xsi_ref, up, yp, x_r, x_i = rest[:7]
        xs_r, xs_i = rest[-2 * N_CB:-N_CB], rest[-N_CB:]
        k = pl.program_id(0)

        @pl.when(k == 0)
        def _():
            _seg_init(finr_ref, fini_ref, pwr_ref, pwi_ref, x_r, x_i, False)

        if not finals_only:
            xsr_ref[0] = x_r[...]
            xsi_ref[0] = x_i[...]
        _permute_in(u_ref, up)

        def drive(c):
            lhs = up[c].astype(BF16)
            xs_r[c][...] = _dot(lhs, bbr_ref[c])
            xs_i[c][...] = _dot(lhs, bbi_ref[c])

        def readout(c):
            yp[c] = (_dot(xs_r[c][...].astype(BF16), ccr_ref[c]) - _dot(xs_i[c][...].astype(BF16), cci_ref[c])
                     + d_ref[:, pl.ds(c * LANES, LANES)] * up[c])

        drive(0)
        for c in range(N_CB):
            if c + 1 < N_CB:
                drive(c + 1)
            if c >= 1 and not finals_only:
                readout(c - 1)
            _scan_block(abr_ref, abi_ref, c, xs_r[c], xs_i[c], 0, x_r, x_i, False)
        if finals_only:
            @pl.when(k == nch - 1)
            def _():
                xfr_ref[...] = x_r[...]
                xfi_ref[...] = x_i[...]
        else:
            readout(N_CB - 1)
            _permute_out(yp, y_ref)

    ublk = pl.BlockSpec((SSM_LK, N_SEG, SSM_W), lambda k: (k, 0, 0))
    st = pl.BlockSpec((1, N_SEG, N_STATE), lambda k: (k, 0, 0))
    vec = _full((1, N_STATE))
    mat = _full((N_SEG, N_STATE))
    chunk = pltpu.VMEM((N_CB, ROWS, LANES), F32)
    blocks = [pltpu.VMEM((ROWS, CB_STATES), F32)] * (2 * N_CB)
    small = pltpu.VMEM((N_SEG, N_STATE), F32)
    if finals_only:
        out_specs, out_shape = [mat, mat], [_sds((N_SEG, N_STATE))] * 2
        scratch, name = [chunk, small, small] + blocks, "ssm_fwd_finals"
    else:
        out_specs = [ublk, st, st]
        out_shape = [_sds(u3.shape)] + [_sds((nch, N_SEG, N_STATE))] * 2
        scratch, name = [chunk, chunk, small, small] + blocks, "ssm_fwd"
    return pl.pallas_call(
        body, name=name, grid=(nch,),
        in_specs=[ublk, vec, vec, vec, vec, mat, mat,
                  _full((N_CB, LANES, CB_STATES)), _full((N_CB, LANES, CB_STATES)),
                  _full((N_CB, CB_STATES, LANES)), _full((N_CB, CB_STATES, LANES)), _full((1, SSM_W))],
        out_specs=out_specs, out_shape=out_shape, scratch_shapes=scratch,
        compiler_params=_cp(("arbitrary",)),
    )(u3, abr, abi, pw_r, pw_i, fin_r, fin_i, bb_r, bb_i, cc_r, cc_i, dskip)


def _ssm_bwd(u3, dy3, xst_r, xst_i, abr, abi, pw_r, pw_i, fin_r, fin_i, bb_r, bb_i, bbt_r, bbt_i,
             cct_r, cct_i, dskip, finals_only):
    sl = u3.shape[0]
    nch = sl // SSM_LK

    def body(u_ref, g_ref, xsr_ref, xsi_ref, abr_ref, abi_ref, pwr_ref, pwi_ref,
             finr_ref, fini_ref, bbr_ref, bbi_ref, btr_ref, bti_ref, ctr_ref, cti_ref, d_ref, *rest):
        if finals_only:
            lfr_ref, lfi_ref, gp, lam_r, lam_i = rest[:5]
            l_r, l_i = rest[-2 * N_CB:-N_CB], rest[-N_CB:]
        else:
            (du_ref, dar_ref, dai_ref, dbr_ref, dbi_ref, dcr_ref, dci_ref, dd_ref,
             gp, up, yp, lam_r, lam_i, x_r, x_i, sar, sai, sdd) = rest[:18]
            l_r, l_i = rest[18:18 + N_CB], rest[18 + N_CB:18 + 2 * N_CB]
            xx_r, xx_i = rest[18 + 2 * N_CB:18 + 3 * N_CB], rest[18 + 3 * N_CB:]
        t = pl.program_id(0)

        @pl.when(t == 0)
        def _():
            _seg_init(finr_ref, fini_ref, pwr_ref, pwi_ref, lam_r, lam_i, True)
            if not finals_only:
                sar[...] = jnp.zeros_like(sar)
                sai[...] = jnp.zeros_like(sai)
                sdd[...] = jnp.zeros_like(sdd)
                dbr_ref[...] = jnp.zeros_like(dbr_ref)
                dbi_ref[...] = jnp.zeros_like(dbi_ref)
                dcr_ref[...] = jnp.zeros_like(dcr_ref)
                dci_ref[...] = jnp.zeros_like(dci_ref)

        _permute_in(g_ref, gp)
        if not finals_only:
            _permute_in(u_ref, up)
            x_r[...] = xsr_ref[0]
            x_i[...] = xsi_ref[0]

        def drive(c):
            lhs = gp[c].astype(BF16)
            l_r[c][...] = _dot(lhs, ctr_ref[c])
            l_i[c][...] = -_dot(lhs, cti_ref[c])
            if not finals_only:
                cols = pl.ds(c * CB_STATES, CB_STATES)
                xx_r[c][pl.ds(0, N_SEG), :] = x_r[:, cols]
                xx_i[c][pl.ds(0, N_SEG), :] = x_i[:, cols]
                ub = up[c].astype(BF16)
                xx_r[c][pl.ds(N_SEG, ROWS), :] = _dot(ub, bbr_ref[c])
                xx_i[c][pl.ds(N_SEG, ROWS), :] = _dot(ub, bbi_ref[c])

        def collect(c):
            lrb = l_r[c][...].astype(BF16)
            lib = l_i[c][...].astype(BF16)
            ub = up[c].astype(BF16)
            gb = gp[c].astype(BF16)
            dbr_ref[c] += _dot_tn(lrb, ub)
            dbi_ref[c] += _dot_tn(lib, ub)
            dcr_ref[c] += _dot_tn(gb, xx_r[c][pl.ds(N_SEG, ROWS), :].astype(BF16))
            dci_ref[c] += -_dot_tn(gb, xx_i[c][pl.ds(N_SEG, ROWS), :].astype(BF16))
            yp[c] = _dot(lrb, btr_ref[c]) + _dot(lib, bti_ref[c]) + d_ref[:, pl.ds(c * LANES, LANES)] * gp[c]
            prod = gp[c] * up[c]
            sdd[:, pl.ds(c * LANES, LANES)] += jnp.sum(prod.reshape(SSM_LK, N_SEG, LANES), axis=0)

        drive(0)
        for c in range(N_CB):
            if c + 1 < N_CB:
                drive(c + 1)
            if finals_only:
                _scan_block(abr_ref, _Neg(abi_ref), c, l_r[c], l_i[c], 0, lam_r, lam_i, True)
            else:
                if c >= 1:
                    collect(c - 1)
                _scan_block(abr_ref, abi_ref, c, xx_r[c], xx_i[c], N_SEG, x_r, x_i, False)
                _scan_block(abr_ref, _Neg(abi_ref), c, l_r[c], l_i[c], 0, lam_r, lam_i, True,
                            acc=(xx_r[c], xx_i[c], sar, sai))
        if finals_only:
            @pl.when(t == nch - 1)
            def _():
                lfr_ref[...] = lam_r[...]
                lfi_ref[...] = lam_i[...]
        else:
            collect(N_CB - 1)
            _permute_out(yp, du_ref)

            @pl.when(t == nch - 1)
            def _():
                dar_ref[...] = jnp.sum(sar[...], axis=0, keepdims=True)
                dai_ref[...] = jnp.sum(sai[...], axis=0, keepdims=True)
                dd_ref[...] = jnp.sum(sdd[...], axis=0, keepdims=True)

    ublk = pl.BlockSpec((SSM_LK, N_SEG, SSM_W), lambda t: (nch - 1 - t, 0, 0))
    st = pl.BlockSpec((1, N_SEG, N_STATE), lambda t: (nch - 1 - t, 0, 0))
    vec = _full((1, N_STATE))
    mat = _full((N_SEG, N_STATE))
    cs = _full((N_CB, LANES, CB_STATES))
    sc = _full((N_CB, CB_STATES, LANES))
    in_specs = [ublk, ublk, st, st, vec, vec, vec, vec, mat, mat, cs, cs, sc, sc, cs, cs, _full((1, SSM_W))]
    chunk = pltpu.VMEM((N_CB, ROWS, LANES), F32)
    blocks = [pltpu.VMEM((ROWS, CB_STATES), F32)] * (2 * N_CB)
    small = pltpu.VMEM((N_SEG, N_STATE), F32)
    if finals_only:
        out_specs, out_shape = [mat, mat], [_sds((N_SEG, N_STATE))] * 2
        scratch, name = [chunk, small, small] + blocks, "ssm_bwd_finals"
    else:
        out_specs = [ublk, vec, vec, sc, sc, cs, cs, _full((1, SSM_W))]
        out_shape = ([_sds(u3.shape), _sds((1, N_STATE)), _sds((1, N_STATE))]
                     + [_sds((N_CB, CB_STATES, LANES))] * 2 + [_sds((N_CB, LANES, CB_STATES))] * 2
                     + [_sds((1, SSM_W))])
        scratch = ([chunk, chunk, chunk, small, small, small, small, small, small, pltpu.VMEM((N_SEG, SSM_W), F32)]
                   + blocks + [pltpu.VMEM((ROWS + N_SEG, CB_STATES), F32)] * (2 * N_CB))
        name = "ssm_bwd"
    return pl.pallas_call(
        body, name=name, grid=(nch,), in_specs=in_specs, out_specs=out_specs, out_shape=out_shape,
        scratch_shapes=scratch, compiler_params=_cp(("arbitrary",)),
    )(u3, dy3, xst_r, xst_i, abr, abi, pw_r, pw_i, fin_r, fin_i, bb_r, bb_i, bbt_r, bbt_i, cct_r, cct_i, dskip)


def _row(tm, w):
    return pl.BlockSpec((tm, w), lambda i: (i, 0))


def _acc_rows(ref, rows, first):
    @pl.when(first)
    def _():
        ref[...] = jnp.zeros_like(ref)

    ref[...] += jnp.sum(rows, axis=0, keepdims=True)


def _fwd_mix(attn, y, x, glu_w, glu_b, ga, gs, w_out, g2, tm=1024):
    s = x.shape[0]

    def body(a_ref, y_ref, x_ref, gw_ref, gb_ref, ga_ref, gs_ref, wo_ref, g2_ref, mix_ref, x2_ref, h_ref):
        a = a_ref[...]
        anb = ((a * _rms(a)) * ga_ref[...]).astype(BF16)
        z, _ = _gelu(y_ref[...])
        so = z * jax.nn.sigmoid(_dot(z.astype(BF16), gw_ref[...]) + gb_ref[...])
        snb = ((so * _rms(so)) * gs_ref[...]).astype(BF16)
        mix_ref[:, pl.ds(0, ATTN_W)] = anb
        mix_ref[:, pl.ds(ATTN_W, SSM_W)] = snb
        x2 = x_ref[...] + (_dot(anb, wo_ref[pl.ds(0, ATTN_W), :]) + _dot(snb, wo_ref[pl.ds(ATTN_W, SSM_W), :]))
        x2_ref[...] = x2
        h_ref[...] = ((x2 * _rms(x2)) * g2_ref[...]).astype(BF16)

    return pl.pallas_call(
        body, name="fwd_mix", grid=(s // tm,),
        in_specs=[_row(tm, ATTN_W), _row(tm, SSM_W), _row(tm, D_MODEL), _full((SSM_W, SSM_W)), _full((1, SSM_W)),
                  _full((1, ATTN_W)), _full((1, SSM_W)), _full((D_MODEL, D_MODEL)), _full((1, D_MODEL))],
        out_specs=[_row(tm, D_MODEL), _row(tm, D_MODEL), _row(tm, D_MODEL)],
        out_shape=[_sds((s, D_MODEL), BF16), _sds((s, D_MODEL)), _sds((s, D_MODEL), BF16)],
        compiler_params=_cp(("parallel",)),
    )(attn, y, x, glu_w, glu_b, ga, gs, w_out, g2)


def _mlp_up(h, w_up, tm=512, bn=1024):
    s = h.shape[0]

    def body(h_ref, w_ref, r_ref, hdn_ref):
        hv = h_ref[...]
        for j in range(D_FF // bn):
            cols = pl.ds(j * bn, bn)
            r = jnp.maximum(_dot(hv, w_ref[:, cols]), 0.0)
            r_ref[:, cols] = r.astype(BF16)
            hdn_ref[:, cols] = (r * r).astype(BF16)

    return pl.pallas_call(
        body, name="mlp_up", grid=(s // tm,),
        in_specs=[_row(tm, D_MODEL), _full((D_MODEL, D_FF))],
        out_specs=[_row(tm, D_FF), _row(tm, D_FF)], out_shape=[_sds((s, D_FF), BF16)] * 2,
        compiler_params=_cp(("parallel",)),
    )(h, w_up)


def _mlp_down_loss(hdn, w_down, x2, tgt, tm=512):
    s = x2.shape[0]

    def body(hdn_ref, w_ref, x2_ref, t_ref, dy_ref, dyb_ref, sse_ref):
        err = (x2_ref[...] + _dot(hdn_ref[...], w_ref[...])) - t_ref[...]
        dy = err * (1.0 / D_MODEL)
        dy_ref[...] = dy
        dyb_ref[...] = dy.astype(BF16)

        @pl.when(pl.program_id(0) == 0)
        def _():
            sse_ref[...] = jnp.zeros_like(sse_ref)

        sse_ref[...] += jnp.sum(jnp.sum(err * err, axis=0, keepdims=True), axis=1, keepdims=True)

    return pl.pallas_call(
        body, name="mlp_down_loss", grid=(s // tm,),
        in_specs=[_row(tm, D_FF), _full((D_FF, D_MODEL)), _row(tm, D_MODEL), _row(tm, D_MODEL)],
        out_specs=[_row(tm, D_MODEL), _row(tm, D_MODEL), _full((1, 1))],
        out_shape=[_sds((s, D_MODEL)), _sds((s, D_MODEL), BF16), _sds((1, 1))],
        compiler_params=_cp(("arbitrary",)),
    )(hdn, w_down, x2, tgt)


def _mlp_down_bwd(dyb, w_down_t, r, tm=512, bn=1024):
    s = dyb.shape[0]

    def body(dy_ref, w_ref, r_ref, dup_ref):
        dyv = dy_ref[...]
        for j in range(D_FF // bn):
            cols = pl.ds(j * bn, bn)
            dup_ref[:, cols] = (_dot(dyv, w_ref[:, cols]) * (2.0 * r_ref[:, cols].astype(F32))).astype(BF16)

    return pl.pallas_call(
        body, name="mlp_down_bwd", grid=(s // tm,),
        in_specs=[_row(tm, D_MODEL), _full((D_MODEL, D_FF)), _row(tm, D_FF)],
        out_specs=_row(tm, D_FF), out_shape=_sds((s, D_FF), BF16),
        compiler_params=_cp(("parallel",)),
    )(dyb, w_down_t, r)


def _mlp_up_bwd(dup, w_up_t, x2, g2, dy, tm=512):
    s = x2.shape[0]

    def body(dup_ref, w_ref, x2_ref, g2_ref, dy_ref, dx2_ref, dx2b_ref, dg_ref):
        dx, dg_rows = _rms_bwd(_dot(dup_ref[...], w_ref[...]), x2_ref[...], g2_ref[...], D_MODEL)
        dx2 = dy_ref[...] + dx
        dx2_ref[...] = dx2
        dx2b_ref[...] = dx2.astype(BF16)
        _acc_rows(dg_ref, dg_rows, pl.program_id(0) == 0)

    return pl.pallas_call(
        body, name="mlp_up_bwd", grid=(s // tm,),
        in_specs=[_row(tm, D_FF), _full((D_FF, D_MODEL)), _row(tm, D_MODEL), _full((1, D_MODEL)), _row(tm, D_MODEL)],
        out_specs=[_row(tm, D_MODEL), _row(tm, D_MODEL), _full((1, D_MODEL))],
        out_shape=[_sds((s, D_MODEL)), _sds((s, D_MODEL), BF16), _sds((1, D_MODEL))],
        compiler_params=_cp(("arbitrary",)),
    )(dup, w_up_t, x2, g2, dy)


def _mix_bwd(dx2b, w_out_t, attn, y, glu_w, glu_b, glu_w_t, ga, gs, tm=1024):
    s = attn.shape[0]

    def body(dx2_ref, wot_ref, a_ref, y_ref, gw_ref, gb_ref, gwt_ref, ga_ref, gs_ref,
             da_ref, dys_ref, z_ref, dpre_ref, dga_ref, dgs_ref, dgb_ref):
        first = pl.program_id(0) == 0
        dmix = _dot(dx2_ref[...], wot_ref[...])
        da, dga_rows = _rms_bwd(dmix[:, :ATTN_W], a_ref[...], ga_ref[...], ATTN_W)
        da_ref[...] = da
        yv = y_ref[...]
        z, t = _gelu(yv)
        gate = jax.nn.sigmoid(_dot(z.astype(BF16), gw_ref[...]) + gb_ref[...])
        dso, dgs_rows = _rms_bwd(dmix[:, ATTN_W:], z * gate, gs_ref[...], SSM_W)
        dpre = dso * z * gate * (1.0 - gate)
        dpre_b = dpre.astype(BF16)
        dz = dso * gate + _dot(dpre_b, gwt_ref[...])
        dgelu = 0.5 * (1.0 + t) + 0.5 * yv * (1.0 - t * t) * (GELU_C * (1.0 + 3.0 * 0.044715 * (yv * yv)))
        dys_ref[...] = dz * dgelu
        z_ref[...] = z.astype(BF16)
        dpre_ref[...] = dpre_b
        _acc_rows(dga_ref, dga_rows, first)
        _acc_rows(dgs_ref, dgs_rows, first)
        _acc_rows(dgb_ref, dpre, first)

    vec = _full((1, SSM_W))
    return pl.pallas_call(
        body, name="mix_bwd", grid=(s // tm,),
        in_specs=[_row(tm, D_MODEL), _full((D_MODEL, D_MODEL)), _row(tm, ATTN_W), _row(tm, SSM_W),
                  _full((SSM_W, SSM_W)), vec, _full((SSM_W, SSM_W)), vec, vec],
        out_specs=[_row(tm, ATTN_W), _row(tm, SSM_W), _row(tm, SSM_W), _row(tm, SSM_W), vec, vec, vec],
        out_shape=[_sds((s, ATTN_W)), _sds((s, SSM_W)), _sds((s, SSM_W), BF16), _sds((s, SSM_W), BF16),
                   _sds((1, ATTN_W)), _sds((1, SSM_W)), _sds((1, SSM_W))],
        compiler_params=_cp(("arbitrary",)),
    )(dx2b, w_out_t, attn, y, glu_w, glu_b, glu_w_t, ga, gs)


def _qk_bwd(dqn, dkn, qk, dv, du, gq, gk, gmat, tm=1024):
    s = qk.shape[0]

    def body(dq_ref, dk_ref, qk_ref, dv_ref, du_ref, gq_ref, gk_ref, gm_ref, dp_ref, dgq_ref, dgk_ref):
        first = pl.program_id(0) == 0
        gm = gm_ref[...]
        for idx, (d_ref, g_ref, dg_ref) in enumerate(((dq_ref, gq_ref, dgq_ref), (dk_ref, gk_ref, dgk_ref))):
            xv = qk_ref[:, pl.ds(idx * ATTN_W, ATTN_W)]
            dyv = d_ref[...]
            r = lax.rsqrt(_group_mean(xv * xv, gm) + EPS)
            gdy = dyv * g_ref[...]
            dx = r * gdy - xv * (r * r * r) * _group_mean(gdy * xv, gm)
            dp_ref[:, pl.ds(idx * ATTN_W, ATTN_W)] = dx.astype(BF16)
            _acc_rows(dg_ref, dyv * (xv * r), first)
        dp_ref[:, pl.ds(2 * ATTN_W, ATTN_W)] = dv_ref[...].astype(BF16)
        dp_ref[:, pl.ds(3 * ATTN_W, SSM_W)] = du_ref[...].astype(BF16)

    vec = _full((1, ATTN_W))
    return pl.pallas_call(
        body, name="qk_bwd", grid=(s // tm,),
        in_specs=[_row(tm, ATTN_W), _row(tm, ATTN_W), _row(tm, 2 * ATTN_W), _row(tm, ATTN_W), _row(tm, SSM_W),
                  vec, vec, _full((ATTN_W, ATTN_W))],
        out_specs=[_row(tm, 4 * ATTN_W), vec, vec],
        out_shape=[_sds((s, 4 * ATTN_W), BF16), _sds((1, ATTN_W)), _sds((1, ATTN_W))],
        compiler_params=_cp(("arbitrary",)),
    )(dqn, dkn, qk, dv, du, gq, gk, gmat)


def _in_bwd(dproj, w_in_t, x, g1, dx2, ex=None, tm=1024):
    s = x.shape[0]
    steps = s // tm

    def body(*refs):
        (dp_ref, w_ref, x_ref, g1_ref, dx2_ref), (gx_ref, dg_ref), _, hx = _carry(ex, refs, 5, 2)
        if ex is not None:
            @pl.when(pl.program_id(0) == 0)
            def _():
                ex.start(*hx)

        dx, dg_rows = _rms_bwd(_dot(dp_ref[...], w_ref[...]), x_ref[...], g1_ref[...], D_MODEL)
        gx_ref[...] = dx2_ref[...] + dx
        _acc_rows(dg_ref, dg_rows, pl.program_id(0) == 0)
        if ex is not None:
            @pl.when(pl.program_id(0) == steps - 1)
            def _():
                ex.wait(*hx)

    hosted = ex is not None
    return pl.pallas_call(
        body, name="in_bwd", grid=(steps,),
        in_specs=[_row(tm, 4 * ATTN_W), _full((4 * ATTN_W, D_MODEL)), _row(tm, D_MODEL), _full((1, D_MODEL)),
                  _row(tm, D_MODEL)] + (ex.specs() if hosted else []),
        out_specs=[_row(tm, D_MODEL), _full((1, D_MODEL))] + (ex.specs() if hosted else []),
        out_shape=[_sds((s, D_MODEL)), _sds((1, D_MODEL))] + (ex.out_shape() if hosted else []),
        scratch_shapes=ex.scratch() if hosted else [],
        compiler_params=_cp(("arbitrary",)),
    )(dproj, w_in_t, x, g1, dx2, *(ex.srcs if hosted else []))


def _mm_tn(a, b, name, ts=2048):
    s, k = a.shape
    n = b.shape[1]
    bk, bn = min(k, 1024), min(n, 1024)
    steps = s // ts

    def body(a_ref, b_ref, o_ref, acc):
        t = pl.program_id(2)

        @pl.when(t == 0)
        def _():
            acc[...] = jnp.zeros_like(acc)

        acc[...] += _dot_tn(a_ref[...], b_ref[...])

        @pl.when(t == steps - 1)
        def _():
            o_ref[...] = acc[...].astype(BF16)

    return pl.pallas_call(
        body, name=name, grid=(k // bk, n // bn, steps),
        in_specs=[pl.BlockSpec((ts, bk), lambda i, j, t: (t, i)), pl.BlockSpec((ts, bn), lambda i, j, t: (t, j))],
        out_specs=pl.BlockSpec((bk, bn), lambda i, j, t: (i, j)), out_shape=_sds((k, n), BF16),
        scratch_shapes=[pltpu.VMEM((bk, bn), F32)],
        compiler_params=_cp(("parallel", "parallel", "arbitrary")),
    )(a, b)


def _peer(k):
    x, y, c = lax.axis_index("x"), lax.axis_index("y"), lax.axis_index("c")
    px = 1 - x if k & 4 else x
    py = 1 - y if k & 2 else y
    pc = 1 - c if k & 1 else c
    return (px, py, pc), 4 * px + 2 * py + pc


def _gather_rows(x_shard):
    m_per, n = x_shard.shape

    def body(x_ref, out_ref, send_sems, recv_sems, local_sem):
        x, y, c = lax.axis_index("x"), lax.axis_index("y"), lax.axis_index("c")
        me, sibling = (x, y, c), (x, y, 1 - c)
        chips = [(1 - x, y), (x, 1 - y), (1 - x, 1 - y)]

        def rows(px, py, pc):
            return out_ref.at[pl.ds((4 * px + 2 * py + pc) * m_per, m_per), :]

        def copy(k, block, to, src=None):
            return pltpu.make_async_remote_copy(
                src_ref=rows(*block) if src is None else src, dst_ref=rows(*block),
                send_sem=send_sems.at[k], recv_sem=recv_sems.at[k], device_id=to, device_id_type=MESH)

        mine = pltpu.make_async_copy(x_ref, rows(*me), local_sem)
        mine.start()
        first = [copy(0, me, sibling, src=x_ref)]
        first += [copy(1 + j, me, (*chip, c), src=x_ref) for j, chip in enumerate(chips)]
        for cp in first:
            cp.start()
        passed = [copy(4 + j, (*chip, c), sibling) for j, chip in enumerate(chips)]
        for j, chip in enumerate(chips):
            copy(1 + j, (*chip, c), me).wait_recv()
            passed[j].start()
        copy(0, sibling, me).wait_recv()
        for j, chip in enumerate(chips):
            copy(4 + j, (*chip, 1 - c), me).wait_recv()
        for cp in first + passed:
            cp.wait_send()
        mine.wait()

    return pl.pallas_call(
        body, name="gather_weights", out_shape=_sds((N_DEV * m_per, n), x_shard.dtype),
        in_specs=[pl.BlockSpec(memory_space=pltpu.VMEM)], out_specs=pl.BlockSpec(memory_space=pltpu.VMEM),
        scratch_shapes=[pltpu.SemaphoreType.DMA((7,)), pltpu.SemaphoreType.DMA((7,)), pltpu.SemaphoreType.DMA],
        compiler_params=pltpu.CompilerParams(vmem_limit_bytes=VMEM_LIMIT),
    )(x_shard)


class _Exchange:
    def __init__(self, srcs, whole):
        self.srcs, self.whole, self.n = list(srcs), list(whole), len(srcs)
        self.rows = [a.shape[0] // N_DEV if w is False else a.shape[0] for a, w in zip(self.srcs, self.whole)]
        self.cols = [a.shape[1] // N_DEV if w == "col" else a.shape[1] for a, w in zip(self.srcs, self.whole)]

    def specs(self):
        return [pl.BlockSpec(memory_space=pl.ANY)] * self.n

    def out_shape(self):
        return [_sds((N_DEV, r, c), a.dtype) for r, c, a in zip(self.rows, self.cols, self.srcs)]

    def scratch(self):
        return [pltpu.SemaphoreType.DMA((self.n * 7,)), pltpu.SemaphoreType.DMA((self.n * 7,)),
                pltpu.SemaphoreType.DMA((self.n,))]

    def _copies(self, ins, outs, sems):
        send_sems, recv_sems, local_sems = sems
        _, me = _peer(0)
        for w in range(self.n):
            for k in range(N_DEV):
                peer, pidx = _peer(k)
                if self.whole[w] is True:
                    src = ins[w]
                elif self.whole[w] == "col":
                    src = ins[w].at[:, pl.ds(pl.multiple_of(pidx * self.cols[w], self.cols[w]), self.cols[w])]
                else:
                    src = ins[w].at[pl.ds(pidx * self.rows[w], self.rows[w]), :]
                if k == 0:
                    yield k, pltpu.make_async_copy(src, outs[w].at[me], local_sems.at[w]), None
                else:
                    sem = w * 7 + k - 1
                    out = pltpu.make_async_remote_copy(src_ref=src, dst_ref=outs[w].at[me], send_sem=send_sems.at[sem],
                                                       recv_sem=recv_sems.at[sem], device_id=peer, device_id_type=MESH)
                    back = pltpu.make_async_remote_copy(src_ref=src, dst_ref=outs[w].at[pidx], send_sem=send_sems.at[sem],
                                                        recv_sem=recv_sems.at[sem], device_id=peer, device_id_type=MESH)
                    yield k, out, back

    def start(self, ins, outs, sems):
        for _, out, _ in self._copies(ins, outs, sems):
            out.start()

    def wait(self, ins, outs, sems):
        for k, out, back in self._copies(ins, outs, sems):
            if k == 0:
                out.wait()
            else:
                back.wait_recv()
                out.wait_send()


def _carry(ex, refs, n_in, n_out):
    nh = ex.n if ex is not None else 0
    ins, hin = refs[:n_in], refs[n_in:n_in + nh]
    outs = refs[n_in + nh:n_in + nh + n_out]
    hout = refs[n_in + nh + n_out:n_in + 2 * nh + n_out]
    rest = refs[n_in + 2 * nh + n_out:]
    if ex is None:
        return ins, outs, rest, None
    return ins, outs, rest[:-3], (hin, hout, rest[-3:])


def _exchange_now(srcs, whole, name):
    ex = _Exchange(srcs, whole)

    def body(*refs):
        _, _, _, (hin, hout, sems) = _carry(ex, refs, 0, 0)
        ex.start(hin, hout, sems)
        ex.wait(hin, hout, sems)

    return pl.pallas_call(body, name=name, out_shape=ex.out_shape(), in_specs=ex.specs(), out_specs=ex.specs(),
                          scratch_shapes=ex.scratch())(*srcs)


def _adamw(w, m, v, gparts, name):
    r, c = w.shape
    tr = r if r * c <= 256 * 1024 else 128 * 1024 // c

    def body(w_ref, m_ref, v_ref, g_ref, go_ref, d_ref, mo_ref, vo_ref):
        g = g_ref[0].astype(F32)
        for i in range(1, N_DEV):
            g = g + g_ref[i].astype(F32)
        go_ref[...] = g
        d_ref[...], mo_ref[...], vo_ref[...] = _adamw_step(w_ref[...], m_ref[...], v_ref[...], g)

    blk = pl.BlockSpec((tr, c), lambda i: (i, 0))
    return pl.pallas_call(
        body, name=name, grid=(r // tr,),
        in_specs=[blk, blk, blk, pl.BlockSpec((N_DEV, tr, c), lambda i: (0, i, 0))],
        out_specs=[blk] * 4, out_shape=[_sds((r, c))] * 4,
        compiler_params=_cp(("parallel",)),
    )(w, m, v, gparts)


def _adamw_step(w, m, v, g):
    nm = ADAM_B1 * m + (1.0 - ADAM_B1) * g
    nv = ADAM_B2 * v + (1.0 - ADAM_B2) * (g * g)
    m_hat = nm / (1.0 - ADAM_B1 ** ADAM_STEP)
    v_hat = nv / (1.0 - ADAM_B2 ** ADAM_STEP)
    return -ADAM_LR * (m_hat / (jnp.sqrt(v_hat) + ADAM_EPS) + ADAM_WD * w), nm, nv


def _sum_slots(parts):
    def body(p_ref, o_ref):
        g = p_ref[0]
        for i in range(1, N_DEV):
            g = g + p_ref[i]
        o_ref[...] = g

    return pl.pallas_call(body, name="sum_small_grads", out_shape=_sds(parts.shape[1:]))(parts)


def _adamw_native(ws, ms, vs, gs):
    n = len(ws)

    def body(*refs):
        for i in range(n):
            w_ref, m_ref, v_ref, g_ref = refs[i], refs[n + i], refs[2 * n + i], refs[3 * n + i]
            d, nm, nv = _adamw_step(w_ref[...], m_ref[...], v_ref[...], g_ref[...])
            refs[4 * n + i][...] = d
            refs[5 * n + i][...] = nm
            refs[6 * n + i][...] = nv

    outs = pl.pallas_call(body, name="adamw_small", out_shape=[_sds(w.shape) for w in ws] * 3,
                          compiler_params=pltpu.CompilerParams(vmem_limit_bytes=VMEM_LIMIT))(*ws, *ms, *vs, *gs)
    return outs[:n], outs[n:2 * n], outs[2 * n:]


def _block_diag(a, states_first):
    a4 = a.reshape(N_CB, 8, SSM_GROUP, SSM_STATE)
    eye = jnp.eye(8, dtype=a.dtype)
    if states_first:
        return jnp.einsum("bgcp,gh->bgphc", a4, eye).reshape(N_CB, CB_STATES, LANES)
    return jnp.einsum("bgcp,gh->bgchp", a4, eye).reshape(N_CB, LANES, CB_STATES)


def _block_diag_of(full, states_first):
    if states_first:
        picked = jnp.einsum("bgphc,gh->bgcp", full.reshape(N_CB, 8, SSM_STATE, 8, SSM_GROUP), jnp.eye(8, dtype=full.dtype))
    else:
        picked = jnp.einsum("bgchp,gh->bgcp", full.reshape(N_CB, 8, SSM_GROUP, 8, SSM_STATE), jnp.eye(8, dtype=full.dtype))
    return picked.reshape(SSM_GROUPS, SSM_GROUP, SSM_STATE)


SMALL_EARLY = ("ssm_a_re", "ssm_a_im", "ssm_log_dt", "ssm_b_re", "ssm_b_im", "ssm_c_re", "ssm_c_im", "ssm_d", "glu_b",
               "attn_out_norm_g", "ssm_out_norm_g", "norm2_g")
SMALL_MID = ("q_norm_g", "k_norm_g")
SMALL_LATE = ("norm1_g",)
SMALL = SMALL_EARLY + SMALL_MID + SMALL_LATE


def _pack_small(arrs):
    parts = []
    for a in arrs:
        flat = a.reshape(-1)
        rows = -(-flat.shape[0] // (8 * LANES)) * 8
        parts.append(jnp.pad(flat, (0, rows * LANES - flat.shape[0])).reshape(rows, LANES))
    return jnp.concatenate(parts, axis=0)


def _unpack_small(packed, shapes):
    out, r0 = [], 0
    for shp in shapes:
        size = math.prod(shp)
        rows = -(-size // (8 * LANES)) * 8
        out.append(packed[r0:r0 + rows].reshape(-1)[:size].reshape(shp))
        r0 += rows
    return out


def kernel(x, norm1_g, w_in, q_norm_g, k_norm_g, ssm_a_re, ssm_a_im, ssm_log_dt, ssm_b_re, ssm_b_im, ssm_c_re, ssm_c_im, ssm_d, glu_w, glu_b, attn_out_norm_g, ssm_out_norm_g, w_out, norm2_g, w_mlp_up, w_mlp_down, loss_target, m_norm1_g, m_w_in, m_q_norm_g, m_k_norm_g, m_ssm_a_re, m_ssm_a_im, m_ssm_log_dt, m_ssm_b_re, m_ssm_b_im, m_ssm_c_re, m_ssm_c_im, m_ssm_d, m_glu_w, m_glu_b, m_attn_out_norm_g, m_ssm_out_norm_g, m_w_out, m_norm2_g, m_w_mlp_up, m_w_mlp_down, v_norm1_g, v_w_in, v_q_norm_g, v_k_norm_g, v_ssm_a_re, v_ssm_a_im, v_ssm_log_dt, v_ssm_b_re, v_ssm_b_im, v_ssm_c_re, v_ssm_c_im, v_ssm_d, v_glu_w, v_glu_b, v_attn_out_norm_g, v_ssm_out_norm_g, v_w_out, v_norm2_g, v_w_mlp_up, v_w_mlp_down):
    weights = dict(norm1_g=norm1_g, w_in=w_in, q_norm_g=q_norm_g, k_norm_g=k_norm_g, ssm_a_re=ssm_a_re,
                   ssm_a_im=ssm_a_im, ssm_log_dt=ssm_log_dt, ssm_b_re=ssm_b_re, ssm_b_im=ssm_b_im,
                   ssm_c_re=ssm_c_re, ssm_c_im=ssm_c_im, ssm_d=ssm_d, glu_w=glu_w, glu_b=glu_b,
                   attn_out_norm_g=attn_out_norm_g, ssm_out_norm_g=ssm_out_norm_g, w_out=w_out, norm2_g=norm2_g,
                   w_mlp_up=w_mlp_up, w_mlp_down=w_mlp_down)
    mom_m = dict(norm1_g=m_norm1_g, w_in=m_w_in, q_norm_g=m_q_norm_g, k_norm_g=m_k_norm_g, ssm_a_re=m_ssm_a_re,
                 ssm_a_im=m_ssm_a_im, ssm_log_dt=m_ssm_log_dt, ssm_b_re=m_ssm_b_re, ssm_b_im=m_ssm_b_im,
                 ssm_c_re=m_ssm_c_re, ssm_c_im=m_ssm_c_im, ssm_d=m_ssm_d, glu_w=m_glu_w, glu_b=m_glu_b,
                 attn_out_norm_g=m_attn_out_norm_g, ssm_out_norm_g=m_ssm_out_norm_g, w_out=m_w_out,
                 norm2_g=m_norm2_g, w_mlp_up=m_w_mlp_up, w_mlp_down=m_w_mlp_down)
    mom_v = dict(norm1_g=v_norm1_g, w_in=v_w_in, q_norm_g=v_q_norm_g, k_norm_g=v_k_norm_g, ssm_a_re=v_ssm_a_re,
                 ssm_a_im=v_ssm_a_im, ssm_log_dt=v_ssm_log_dt, ssm_b_re=v_ssm_b_re, ssm_b_im=v_ssm_b_im,
                 ssm_c_re=v_ssm_c_re, ssm_c_im=v_ssm_c_im, ssm_d=v_ssm_d, glu_w=v_glu_w, glu_b=v_glu_b,
                 attn_out_norm_g=v_attn_out_norm_g, ssm_out_norm_g=v_ssm_out_norm_g, w_out=v_w_out,
                 norm2_g=v_norm2_g, w_mlp_up=v_w_mlp_up, w_mlp_down=v_w_mlp_down)
    order = list(weights)

    xs, tgt = x[0], loss_target[0]
    s = xs.shape[0]
    assert s % ATTN_CHUNK == 0 and (s // N_SEG) % SSM_LK == 0
    seg_len = s // N_SEG
    n_sq = seg_len.bit_length() - 1
    assert 1 << n_sq == seg_len

    w_in_t = _gather_rows(w_in[0].T.astype(BF16))
    w_in_full = w_in_t.T
    later = _Exchange([glu_w[0].astype(BF16), w_out[0].astype(BF16), w_mlp_up[0].T.astype(BF16),
                       w_mlp_down[0].astype(BF16)], [True] * 4)

    gq = jnp.tile(q_norm_g[0], ATTN_W // HEAD_DIM)[None]
    gk = jnp.tile(k_norm_g[0], ATTN_W // HEAD_DIM)[None]
    lane = jnp.arange(ATTN_W) // HEAD_DIM
    gmat = jnp.where(lane[:, None] == lane[None, :], 1.0 / HEAD_DIM, 0.0).astype(BF16)
    a_re3 = ssm_a_re[0][:, None, :]
    a_im3 = ssm_a_im[0][:, None, :]
    ldt3 = ssm_log_dt[0][:, None, None]
    b_re_t = jnp.swapaxes(ssm_b_re[0], 1, 2)
    b_im_t = jnp.swapaxes(ssm_b_im[0], 1, 2)
    c_re, c_im = ssm_c_re[0], ssm_c_im[0]
    dskip = ssm_d[0].reshape(1, SSM_W)

    qk, qn, kn, v, u, xn = _fwd_proj(xs, norm1_g, w_in_full, gq, gk, gmat)
    attn, lse, glu_g, w_out_g, w_up_g, w_down_g = _attn_fwd(qn, kn, v, later)
    glu_full = glu_g.reshape(SSM_W, SSM_W)
    w_out_full = w_out_g.reshape(D_MODEL, D_MODEL)
    w_up_t = w_up_g.reshape(D_FF, D_MODEL)
    w_down_full = w_down_g.reshape(D_FF, D_MODEL)
    w_up_full, w_out_t, w_down_t, glu_t = w_up_t.T, w_out_full.T, w_down_full.T, glu_full.T

    abr3, abi3, bbr, bbi = _ssm_discretize(a_re3, a_im3, ldt3, b_re_t, b_im_t)
    abr, abi = abr3.reshape(1, N_STATE), abi3.reshape(1, N_STATE)
    pw_r, pw_i = _ssm_power(abr, abi, n_sq)
    bb_r, bb_i = _block_diag(bbr, False).astype(BF16), _block_diag(bbi, False).astype(BF16)
    bbt_r, bbt_i = _block_diag(bbr, True).astype(BF16), _block_diag(bbi, True).astype(BF16)
    cc_r, cc_i = _block_diag(c_re, True).astype(BF16), _block_diag(c_im, True).astype(BF16)
    cct_r, cct_i = _block_diag(c_re, False).astype(BF16), _block_diag(c_im, False).astype(BF16)
    seg_major = lambda a: jnp.swapaxes(a.reshape(N_SEG, seg_len, SSM_W), 0, 1)
    seg_minor = lambda a: jnp.swapaxes(a, 0, 1).reshape(s, SSM_W)
    u3 = seg_major(u)
    zero_fin = jnp.zeros((N_SEG, N_STATE), F32)
    ssm_args = (abr, abi, pw_r, pw_i)
    xf_r, xf_i = _ssm_fwd(u3, *ssm_args, zero_fin, zero_fin, bb_r, bb_i, cc_r, cc_i, dskip, True)
    y3, xst_r, xst_i = _ssm_fwd(u3, *ssm_args, xf_r, xf_i, bb_r, bb_i, cc_r, cc_i, dskip, False)
    y = seg_minor(y3)

    mix, x2, h = _fwd_mix(attn, y, xs, glu_full, glu_b, attn_out_norm_g, ssm_out_norm_g, w_out_full, norm2_g)
    r_act, hdn = _mlp_up(h, w_up_full)
    dy, dyb, sse = _mlp_down_loss(hdn, w_down_full, x2, tgt)
    loss = lax.psum(0.5 * sse[0, 0] / D_MODEL, ("x", "y", "c"))

    dup = _mlp_down_bwd(dyb, w_down_t, r_act)
    g_w_down = _mm_tn(hdn, dyb, "grad_w_down")
    dx2, dx2b, g_norm2 = _mlp_up_bwd(dup, w_up_t, x2, norm2_g, dy)
    g_w_up = _mm_tn(h, dup, "grad_w_up")
    dattn, dys, z_b, dpre_b, g_ga, g_gs, g_glu_b = _mix_bwd(dx2b, w_out_t, attn, y, glu_full, glu_b, glu_t,
                                                             attn_out_norm_g, ssm_out_norm_g)
    g_w_out = _mm_tn(mix, dx2b, "grad_w_out")
    g_glu_w = _mm_tn(z_b, dpre_b, "grad_glu_w")
    dy3 = seg_major(dys)
    bwd_args = (u3, dy3, xst_r, xst_i, abr, abi, pw_r, pw_i)
    lf_r, lf_i = _ssm_bwd(*bwd_args, zero_fin, zero_fin, bb_r, bb_i, bbt_r, bbt_i, cct_r, cct_i, dskip, True)
    du3, dab_r, dab_i, dbb_r, dbb_i, dcc_r, dcc_i, g_d = _ssm_bwd(*bwd_args, lf_r, lf_i, bb_r, bb_i, bbt_r, bbt_i,
                                                                 cct_r, cct_i, dskip, False)
    g_a_re3, g_a_im3, g_ldt3, g_b_re_t, g_b_im_t = _ssm_discretize_bwd(
        a_re3, a_im3, ldt3, b_re_t, b_im_t, dab_r.reshape(a_re3.shape), dab_i.reshape(a_re3.shape),
        _block_diag_of(dbb_r, True), _block_diag_of(dbb_i, True))
    g_c_re, g_c_im = _block_diag_of(dcc_r, False), _block_diag_of(dcc_i, False)
    small_grads = dict(
        ssm_a_re=g_a_re3.reshape(ssm_a_re.shape), ssm_a_im=g_a_im3.reshape(ssm_a_im.shape),
        ssm_log_dt=g_ldt3.reshape(ssm_log_dt.shape), ssm_b_re=jnp.swapaxes(g_b_re_t, 1, 2)[None],
        ssm_b_im=jnp.swapaxes(g_b_im_t, 1, 2)[None], ssm_c_re=g_c_re[None], ssm_c_im=g_c_im[None],
        ssm_d=g_d.reshape(ssm_d.shape), glu_b=g_glu_b, attn_out_norm_g=g_ga, ssm_out_norm_g=g_gs, norm2_g=g_norm2)

    early = _Exchange([g_glu_w, g_w_out, g_w_up, g_w_down, _pack_small([small_grads[n] for n in SMALL_EARLY])],
                      [False, False, "col", False, True])
    dqn, dkn, dv, p_glu, p_w_out, p_w_up, p_w_down, p_early = _attn_bwd(qn, kn, v, attn, lse, dattn, early)

    dproj, g_gq, g_gk = _qk_bwd(dqn, dkn, qk, dv, seg_minor(du3), gq, gk, gmat)
    g_w_in = _mm_tn(xn, dproj, "grad_w_in")
    small_grads["q_norm_g"] = g_gq.reshape(ATTN_W // HEAD_DIM, HEAD_DIM).sum(0)[None]
    small_grads["k_norm_g"] = g_gk.reshape(ATTN_W // HEAD_DIM, HEAD_DIM).sum(0)[None]
    mid = _Exchange([g_w_in, _pack_small([small_grads[n] for n in SMALL_MID])], ["col", True])
    grad_x, g_norm1, p_w_in, p_mid = _in_bwd(dproj, w_in_t, xs, norm1_g, dx2, mid)
    (p_late,) = _exchange_now([_pack_small([g_norm1])], [True], "exchange_norm1")

    res = {}
    for name, gp in (("w_in", p_w_in), ("glu_w", p_glu), ("w_out", p_w_out), ("w_mlp_up", p_w_up), ("w_mlp_down", p_w_down)):
        outs = _adamw(weights[name][0], mom_m[name][0], mom_v[name][0], gp, "adamw_" + name)
        res[name] = [o[None] for o in outs]
    g_small = _unpack_small(_sum_slots(jnp.concatenate([p_early, p_mid, p_late], axis=1)), [weights[n].shape for n in SMALL])
    d_small, m_small, v_small = _adamw_native([weights[n] for n in SMALL], [mom_m[n] for n in SMALL],
                                              [mom_v[n] for n in SMALL], g_small)
    for i, n in enumerate(SMALL):
        res[n] = [g_small[i], d_small[i], m_small[i], v_small[i]]

    return (loss, grad_x[None], *[res[n][0] for n in order], *[res[n][1] for n in order],
            *[res[n][2] for n in order], *[res[n][3] for n in order])
```

```python
import math

import jax
import jax.numpy as jnp
from jax import lax
from jax.experimental import pallas as pl
from jax.experimental.pallas import tpu as pltpu

F32 = jnp.float32
BF16 = jnp.bfloat16

D_MODEL = 1024
ATTN_W = 512
HEAD_DIM = 64
SSM_W = 512
SSM_GROUP = 16
SSM_GROUPS = 32
SSM_STATE = 64
N_STATE = SSM_GROUPS * SSM_STATE
D_FF = 4096
EPS = 1e-6
NEG_INF = -1e30
ATTN_CHUNK = 2048
ATTN_BLOCK = 128
DILATIONS = (1, 4, 16)
N_SEG = 8
SSM_LK = 64
N_DEV = 8
LANES = 128

ADAM_LR = 0.001
ADAM_B1 = 0.9
ADAM_B2 = 0.999
ADAM_EPS = 1e-08
ADAM_WD = 0.01
ADAM_STEP = 10

VMEM_LIMIT = 56 * 1024 * 1024
GELU_C = math.sqrt(2.0 / math.pi)
MESH = pl.DeviceIdType.MESH


def _cp(sem, vmem=VMEM_LIMIT):
    return pltpu.CompilerParams(dimension_semantics=sem, vmem_limit_bytes=vmem)


def _dot(a, b):
    return jnp.dot(a, b, preferred_element_type=F32)


def _dot_tn(a, b):
    return lax.dot_general(a, b, (((0,), (0,)), ((), ())), preferred_element_type=F32)


def _group_mean(x2, gmat):
    hi = x2.astype(BF16)
    lo = (x2 - hi.astype(F32)).astype(BF16)
    return _dot(hi, gmat) + _dot(lo, gmat)


def _rms(x):
    return lax.rsqrt(jnp.mean(x * x, axis=-1, keepdims=True) + EPS)


def _rms_bwd(dy, x, g, n):
    r = _rms(x)
    gdy = dy * g
    dx = r * gdy - x * (r * r * r) * (jnp.sum(gdy * x, axis=-1, keepdims=True) / n)
    return dx, dy * (x * r)


def _gelu(y):
    t = jnp.tanh(GELU_C * (y + 0.044715 * (y * y * y)))
    return 0.5 * y * (1.0 + t), t


def _full(shape):
    nd = len(shape)
    return pl.BlockSpec(shape, lambda *_: (0,) * nd)


def _sds(shape, dtype=F32):
    return jax.ShapeDtypeStruct(shape, dtype)


def _fwd_proj(x, g1, w_in, gq, gk, gmat, tm=512):
    s = x.shape[0]

    def body(x_ref, g1_ref, w_ref, gq_ref, gk_ref, gm_ref, qk_ref, qn_ref, kn_ref, v_ref, u_ref, xn_ref):
        xv = x_ref[...]
        xnb = ((xv * _rms(xv)) * g1_ref[...]).astype(BF16)
        xn_ref[...] = xnb
        proj = _dot(xnb, w_ref[...])
        q = proj[:, :ATTN_W]
        k = proj[:, ATTN_W:2 * ATTN_W]
        qk_ref[...] = proj[:, :2 * ATTN_W]
        v_ref[...] = proj[:, 2 * ATTN_W:3 * ATTN_W]
        u_ref[...] = proj[:, 3 * ATTN_W:]
        gm = gm_ref[...]
        qn_ref[...] = (q * lax.rsqrt(_group_mean(q * q, gm) + EPS)) * gq_ref[...]
        kn_ref[...] = (k * lax.rsqrt(_group_mean(k * k, gm) + EPS)) * gk_ref[...]

    row = lambda w: pl.BlockSpec((tm, w), lambda i: (i, 0))
    return pl.pallas_call(
        body, name="fwd_proj", grid=(s // tm,),
        in_specs=[row(D_MODEL), _full((1, D_MODEL)), _full((D_MODEL, 4 * ATTN_W)), _full((1, ATTN_W)),
                  _full((1, ATTN_W)), _full((ATTN_W, ATTN_W))],
        out_specs=[row(2 * ATTN_W), row(ATTN_W), row(ATTN_W), row(ATTN_W), row(ATTN_W), row(D_MODEL)],
        out_shape=[_sds((s, 2 * ATTN_W)), _sds((s, ATTN_W)), _sds((s, ATTN_W)), _sds((s, ATTN_W)),
                   _sds((s, ATTN_W)), _sds((s, D_MODEL), BF16)],
        compiler_params=_cp(("parallel",)),
    )(x, g1, w_in, gq, gk, gmat)


def _attn_rows(t, d, nb):
    if d == 1:
        q0 = t * ATTN_BLOCK
        return (t, pl.ds(q0, ATTN_BLOCK), pl.ds(ATTN_CHUNK + q0, ATTN_BLOCK),
                pl.ds(ATTN_CHUNK - ATTN_BLOCK + q0, ATTN_BLOCK))
    r = t // nb
    b = t % nb
    return (b, pl.ds(ATTN_BLOCK * b * d + r, ATTN_BLOCK, stride=d),
            pl.ds(ATTN_CHUNK + ATTN_BLOCK * b * d + r, ATTN_BLOCK, stride=d),
            pl.ds(ATTN_CHUNK + ATTN_BLOCK * (b - 1) * d + r, ATTN_BLOCK, stride=d))


def _attn_masks():
    row = lax.broadcasted_iota(jnp.int32, (ATTN_BLOCK, LANES), 0)
    col = lax.broadcasted_iota(jnp.int32, (ATTN_BLOCK, LANES), 1)
    return row, col


NBLK = ATTN_CHUNK // ATTN_BLOCK


def _attn_bias(bias_s):
    row, col = _attn_masks()
    bias_s[:, pl.ds(0, LANES)] = jnp.where(col >= row, 0.0, NEG_INF)
    bias_s[:, pl.ds(LANES, LANES)] = jnp.where(col <= row, 0.0, NEG_INF)
    return col < HEAD_DIM


def _attn_fwd(qn, kn, v, ex=None, group=4):
    s = qn.shape[0]
    nch = s // ATTN_CHUNK
    scale = HEAD_DIM ** -0.5
    npat = len(DILATIONS)
    n_hp = ATTN_W // LANES

    def body(*refs):
        ((q_ref, kp_ref, kc_ref, vp_ref, vc_ref), (o_ref, lse_ref),
         (kk, vv, kt_s, vb_s, bias_s, m_s, a_s), hx) = _carry(ex, refs, 5, 2)
        i = pl.program_id(1)
        if ex is not None:
            @pl.when(jnp.logical_and(pl.program_id(0) == 0, i == 0))
            def _():
                ex.start(*hx)

        kk[pl.ds(0, ATTN_CHUNK), :] = kp_ref[...]
        kk[pl.ds(ATTN_CHUNK, ATTN_CHUNK), :] = kc_ref[...]
        vv[pl.ds(0, ATTN_CHUNK), :] = vp_ref[...]
        vv[pl.ds(ATTN_CHUNK, ATTN_CHUNK), :] = vc_ref[...]
        head0 = _attn_bias(bias_s)
        first_pen = jnp.where(i > 0, 0.0, NEG_INF)

        for p, d in enumerate(DILATIONS):
            nb = ATTN_CHUNK // (ATTN_BLOCK * d)

            def prep(t, d=d, nb=nb):
                _, _, crows, prows = _attn_rows(t, d, nb)
                kt_s[t, :, pl.ds(0, LANES)] = kk[prows, :].T.astype(BF16)
                kt_s[t, :, pl.ds(LANES, LANES)] = kk[crows, :].T.astype(BF16)
                vp = vv[prows, :]
                vc = vv[crows, :]
                vb_s[2 * t, pl.ds(0, ATTN_BLOCK), :] = jnp.where(head0, vp, 1.0).astype(BF16)
                vb_s[2 * t, pl.ds(ATTN_BLOCK, ATTN_BLOCK), :] = jnp.where(head0, vc, 1.0).astype(BF16)
                vb_s[2 * t + 1, pl.ds(0, ATTN_BLOCK), :] = jnp.where(head0, 1.0, vp).astype(BF16)
                vb_s[2 * t + 1, pl.ds(ATTN_BLOCK, ATTN_BLOCK), :] = jnp.where(head0, 1.0, vc).astype(BF16)

            def main(tg, p=p, d=d, nb=nb):
                st = []
                for g in range(group):
                    t = tg * group + g
                    b, qrows, _, _ = _attn_rows(t, d, nb)
                    q = q_ref[qrows, :] * scale
                    for h in range(2):
                        hm = head0 if h == 0 else jnp.logical_not(head0)
                        st.append(dict(t=t, b=b, qrows=qrows, sc=_dot(jnp.where(hm, q, 0.0).astype(BF16), kt_s[t])))
                for e in st:
                    sc = e["sc"] + bias_s[...]
                    s_p = sc[:, :LANES] + first_pen if e["b"] == 0 else sc[:, :LANES]
                    s_c = sc[:, LANES:]
                    m = jnp.max(jnp.maximum(s_p, s_c), axis=-1, keepdims=True)
                    e["eb"] = jnp.concatenate([jnp.exp(s_p - m), jnp.exp(s_c - m)], axis=1).astype(BF16)
                    e["m"] = jnp.broadcast_to(m, (ATTN_BLOCK, LANES))
                for g in range(group):
                    e0, e1 = st[2 * g], st[2 * g + 1]
                    t = e0["t"]
                    m_s[p, e0["qrows"], :] = jnp.where(head0, e0["m"], e1["m"])
                    a_s[2 * p, e0["qrows"], :] = _dot(e0["eb"], vb_s[2 * t])
                    a_s[2 * p + 1, e0["qrows"], :] = _dot(e1["eb"], vb_s[2 * t + 1])

            for g in range(group):
                prep(g)
            for tg in range(NBLK // group):
                if tg + 1 < NBLK // group:
                    for g in range(group):
                        prep((tg + 1) * group + g)
                main(tg)

        def merge(t, carry):
            rows = pl.ds(pl.multiple_of(t * ATTN_BLOCK, ATTN_BLOCK), ATTN_BLOCK)
            m_all = m_s[0, rows, :]
            for p in range(1, npat):
                m_all = jnp.maximum(m_all, m_s[p, rows, :])
            num = jnp.zeros((ATTN_BLOCK, LANES), F32)
            den = jnp.zeros((ATTN_BLOCK, LANES), F32)
            for p in range(npat):
                w = jnp.exp(m_s[p, rows, :] - m_all)
                a0, a1 = a_s[2 * p, rows, :], a_s[2 * p + 1, rows, :]
                num = num + jnp.where(head0, a0, a1) * w
                den = den + pltpu.roll(jnp.where(head0, a1, a0), HEAD_DIM, 1) * w
            o_ref[rows, :] = num / den
            lse_ref[rows, :] = m_all + jnp.log(den)
            return carry

        lax.fori_loop(0, NBLK, merge, 0, unroll=2)
        if ex is not None:
            @pl.when(jnp.logical_and(pl.program_id(0) == n_hp - 1, i == nch - 1))
            def _():
                ex.wait(*hx)

    cur = pl.BlockSpec((ATTN_CHUNK, LANES), lambda h, i: (i, h))
    prev = pl.BlockSpec((ATTN_CHUNK, LANES), lambda h, i: (jnp.maximum(i - 1, 0), h))
    hosted = ex is not None
    return pl.pallas_call(
        body, name="attn_fwd", grid=(n_hp, nch),
        in_specs=[cur, prev, cur, prev, cur] + (ex.specs() if hosted else []),
        out_specs=[cur, cur] + (ex.specs() if hosted else []),
        out_shape=[_sds((s, ATTN_W)), _sds((s, ATTN_W))] + (ex.out_shape() if hosted else []),
        scratch_shapes=[pltpu.VMEM((2 * ATTN_CHUNK, LANES), F32), pltpu.VMEM((2 * ATTN_CHUNK, LANES), F32),
                        pltpu.VMEM((NBLK, LANES, 2 * LANES), BF16), pltpu.VMEM((2 * NBLK, 2 * ATTN_BLOCK, LANES), BF16),
                        pltpu.VMEM((ATTN_BLOCK, 2 * LANES), F32),
                        pltpu.VMEM((npat, ATTN_CHUNK, LANES), F32), pltpu.VMEM((2 * npat, ATTN_CHUNK, LANES), F32)]
        + (ex.scratch() if hosted else []),
        compiler_params=_cp(("arbitrary", "arbitrary")),
    )(qn, kn, kn, v, v, *(ex.srcs if hosted else []))


def _attn_bwd(qn, kn, v, o, lse, do, ex=None, group=4):
    s = qn.shape[0]
    nch = s // ATTN_CHUNK
    scale = HEAD_DIM ** -0.5
    npat = len(DILATIONS)
    n_hp = ATTN_W // LANES

    def body(*refs):
        ((q_ref, kp_ref, kc_ref, vp_ref, vc_ref, o_ref, lse_ref, do_ref), (dq_ref, dk_ref, dv_ref),
         (kk, vv, dkk, dvv, kt_s, vt_s, kn_s, bias_s, dq_s, dl_s, dkb, dvb), hx) = _carry(ex, refs, 8, 3)
        step = pl.program_id(1)
        i = nch - 1 - step
        if ex is not None:
            @pl.when(jnp.logical_and(pl.program_id(0) == 0, step == 0))
            def _():
                ex.start(*hx)

        kk[pl.ds(0, ATTN_CHUNK), :] = kp_ref[...]
        kk[pl.ds(ATTN_CHUNK, ATTN_CHUNK), :] = kc_ref[...]
        vv[pl.ds(0, ATTN_CHUNK), :] = vp_ref[...]
        vv[pl.ds(ATTN_CHUNK, ATTN_CHUNK), :] = vc_ref[...]

        @pl.when(step == 0)
        def _():
            dkk[pl.ds(ATTN_CHUNK, ATTN_CHUNK), :] = jnp.zeros((ATTN_CHUNK, LANES), F32)
            dvv[pl.ds(ATTN_CHUNK, ATTN_CHUNK), :] = jnp.zeros((ATTN_CHUNK, LANES), F32)

        @pl.when(step > 0)
        def _():
            dkk[pl.ds(ATTN_CHUNK, ATTN_CHUNK), :] = dkk[pl.ds(0, ATTN_CHUNK), :]
            dvv[pl.ds(ATTN_CHUNK, ATTN_CHUNK), :] = dvv[pl.ds(0, ATTN_CHUNK), :]

        dkk[pl.ds(0, ATTN_CHUNK), :] = jnp.zeros((ATTN_CHUNK, LANES), F32)
        dvv[pl.ds(0, ATTN_CHUNK), :] = jnp.zeros((ATTN_CHUNK, LANES), F32)
        head0 = _attn_bias(bias_s)

        def delta(t, carry):
            rows = pl.ds(pl.multiple_of(t * ATTN_BLOCK, ATTN_BLOCK), ATTN_BLOCK)
            prod = do_ref[rows, :] * o_ref[rows, :]
            d0 = jnp.sum(jnp.where(head0, prod, 0.0), axis=-1, keepdims=True)
            d1 = jnp.sum(jnp.where(head0, 0.0, prod), axis=-1, keepdims=True)
            dl_s[rows, :] = jnp.where(head0, d0, d1)
            return carry

        lax.fori_loop(0, NBLK, delta, 0, unroll=2)

        first_pen = jnp.where(i > 0, 0.0, NEG_INF)

        for p, d in enumerate(DILATIONS):
            nb = ATTN_CHUNK // (ATTN_BLOCK * d)

            def prep(t, d=d, nb=nb):
                _, _, crows, prows = _attn_rows(t, d, nb)
                kp, kc = kk[prows, :], kk[crows, :]
                kt_s[t, :, pl.ds(0, LANES)] = kp.T.astype(BF16)
                kt_s[t, :, pl.ds(LANES, LANES)] = kc.T.astype(BF16)
                kn_s[t, pl.ds(0, ATTN_BLOCK), :] = (kp * scale).astype(BF16)
                kn_s[t, pl.ds(ATTN_BLOCK, ATTN_BLOCK), :] = (kc * scale).astype(BF16)
                vt_s[t, :, pl.ds(0, LANES)] = vv[prows, :].T.astype(BF16)
                vt_s[t, :, pl.ds(LANES, LANES)] = vv[crows, :].T.astype(BF16)

            def main(tg, p=p, d=d, nb=nb):
                st = []
                for g in range(group):
                    t = tg * group + g
                    b, qrows, _, _ = _attn_rows(t, d, nb)
                    q = q_ref[qrows, :] * scale
                    dout = do_ref[qrows, :]
                    lse_b = lse_ref[qrows, :]
                    dl_b = dl_s[qrows, :]
                    for h in range(2):
                        hm = head0 if h == 0 else jnp.logical_not(head0)
                        c0 = h * HEAD_DIM
                        qh = jnp.where(hm, q, 0.0).astype(BF16)
                        doh = jnp.where(hm, dout, 0.0).astype(BF16)
                        st.append(dict(t=t, b=b, qrows=qrows, qh=qh, doh=doh, lse=lse_b[:, c0:c0 + 1],
                                       dl=dl_b[:, c0:c0 + 1], sc=_dot(qh, kt_s[t]), dp=_dot(doh, vt_s[t])))
                for e in st:
                    sc = e["sc"] + bias_s[...]
                    if e["b"] == 0:
                        sc = jnp.concatenate([sc[:, :LANES] + first_pen, sc[:, LANES:]], axis=1)
                    pr = jnp.exp(sc - e["lse"])
                    e["ds"] = (pr * (e["dp"] - e["dl"])).astype(BF16)
                    e["pr"] = pr.astype(BF16)
                for g in range(group):
                    e0, e1 = st[2 * g], st[2 * g + 1]
                    t = e0["t"]
                    dq_s[p, e0["qrows"], :] = jnp.where(head0, _dot(e0["ds"], kn_s[t]), _dot(e1["ds"], kn_s[t]))
                    dkb[t] = _dot_tn(e0["ds"], e0["qh"]) + _dot_tn(e1["ds"], e1["qh"])
                    dvb[t] = _dot_tn(e0["pr"], e0["doh"]) + _dot_tn(e1["pr"], e1["doh"])

            def scatter(t, d=d, nb=nb):
                _, _, crows, prows = _attn_rows(t, d, nb)
                dkk[prows, :] = dkk[prows, :] + dkb[t, pl.ds(0, ATTN_BLOCK), :]
                dkk[crows, :] = dkk[crows, :] + dkb[t, pl.ds(ATTN_BLOCK, ATTN_BLOCK), :]
                dvv[prows, :] = dvv[prows, :] + dvb[t, pl.ds(0, ATTN_BLOCK), :]
                dvv[crows, :] = dvv[crows, :] + dvb[t, pl.ds(ATTN_BLOCK, ATTN_BLOCK), :]

            n_groups = NBLK // group
            for g in range(group):
                prep(g)
            for tg in range(n_groups):
                if tg + 1 < n_groups:
                    for g in range(group):
                        prep((tg + 1) * group + g)
                main(tg)
                if tg >= 1:
                    for g in range(group):
                        scatter((tg - 1) * group + g)
            for g in range(group):
                scatter((n_groups - 1) * group + g)

        def finish(t, carry):
            rows = pl.ds(pl.multiple_of(t * ATTN_BLOCK, ATTN_BLOCK), ATTN_BLOCK)
            acc = dq_s[0, rows, :]
            for p in range(1, npat):
                acc = acc + dq_s[p, rows, :]
            dq_ref[rows, :] = acc
            return carry

        lax.fori_loop(0, NBLK, finish, 0, unroll=2)
        dk_ref[...] = dkk[pl.ds(ATTN_CHUNK, ATTN_CHUNK), :]
        dv_ref[...] = dvv[pl.ds(ATTN_CHUNK, ATTN_CHUNK), :]
        if ex is not None:
            @pl.when(jnp.logical_and(pl.program_id(0) == n_hp - 1, step == nch - 1))
            def _():
                ex.wait(*hx)

    cur = pl.BlockSpec((ATTN_CHUNK, LANES), lambda h, t: (nch - 1 - t, h))
    prev = pl.BlockSpec((ATTN_CHUNK, LANES), lambda h, t: (jnp.maximum(nch - 2 - t, 0), h))
    big = pltpu.VMEM((2 * ATTN_CHUNK, LANES), F32)
    pair_t = pltpu.VMEM((NBLK, LANES, 2 * LANES), BF16)
    hosted = ex is not None
    return pl.pallas_call(
        body, name="attn_bwd", grid=(n_hp, nch),
        in_specs=[cur, prev, cur, prev, cur, cur, cur, cur] + (ex.specs() if hosted else []),
        out_specs=[cur, cur, cur] + (ex.specs() if hosted else []),
        out_shape=[_sds((s, ATTN_W))] * 3 + (ex.out_shape() if hosted else []),
        scratch_shapes=[big, big, big, big, pair_t, pair_t, pltpu.VMEM((NBLK, 2 * ATTN_BLOCK, LANES), BF16),
                        pltpu.VMEM((ATTN_BLOCK, 2 * LANES), F32),
                        pltpu.VMEM((npat, ATTN_CHUNK, LANES), F32), pltpu.VMEM((ATTN_CHUNK, LANES), F32),
                        pltpu.VMEM((NBLK, 2 * ATTN_BLOCK, LANES), F32), pltpu.VMEM((NBLK, 2 * ATTN_BLOCK, LANES), F32)]
        + (ex.scratch() if hosted else []),
        compiler_params=_cp(("arbitrary", "arbitrary")),
    )(qn, kn, kn, v, v, o, lse, do, *(ex.srcs if hosted else []))


def _discretize(lr, li, dt):
    mag = jnp.exp(lr * dt)
    abr = mag * jnp.cos(li * dt)
    abi = mag * jnp.sin(li * dt)
    den = lr * lr + li * li
    nr, ni = abr - 1.0, abi
    cr = (nr * lr + ni * li) / den
    ci = (ni * lr - nr * li) / den
    return abr, abi, den, nr, ni, cr, ci


def _ssm_discretize(a_re, a_im, log_dt, b_re_t, b_im_t):
    def body(ar_ref, ai_ref, ldt_ref, br_ref, bi_ref, abr_ref, abi_ref, bbr_ref, bbi_ref):
        abr, abi, _, _, _, cr, ci = _discretize(ar_ref[...], ai_ref[...], jnp.exp(ldt_ref[...]))
        br, bi = br_ref[...], bi_ref[...]
        abr_ref[...] = abr
        abi_ref[...] = abi
        bbr_ref[...] = cr * br - ci * bi
        bbi_ref[...] = cr * bi + ci * br

    return pl.pallas_call(
        body, name="ssm_discretize",
        out_shape=[_sds(a_re.shape)] * 2 + [_sds(b_re_t.shape)] * 2,
    )(a_re, a_im, log_dt, b_re_t, b_im_t)


def _ssm_discretize_bwd(a_re, a_im, log_dt, b_re_t, b_im_t, dabr, dabi, dbbr, dbbi):
    def body(ar_ref, ai_ref, ldt_ref, br_ref, bi_ref, dabr_ref, dabi_ref, dbbr_ref, dbbi_ref,
             dar_ref, dai_ref, dldt_ref, dbr_ref, dbi_ref):
        lr, li = ar_ref[...], ai_ref[...]
        dt = jnp.exp(ldt_ref[...])
        abr, abi, den, nr, ni, cr, ci = _discretize(lr, li, dt)
        br, bi = br_ref[...], bi_ref[...]
        gbr, gbi = dbbr_ref[...], dbbi_ref[...]
        dcr = jnp.sum(gbr * br + gbi * bi, axis=1, keepdims=True)
        dci = jnp.sum(gbi * br - gbr * bi, axis=1, keepdims=True)
        dbr_ref[...] = cr * gbr + ci * gbi
        dbi_ref[...] = cr * gbi - ci * gbr
        dnr = (dcr * lr - dci * li) / den
        dni = (dcr * li + dci * lr) / den
        dden = -(dcr * cr + dci * ci) / den
        dlr = (dcr * nr + dci * ni) / den + dden * 2.0 * lr
        dli = (dcr * ni - dci * nr) / den + dden * 2.0 * li
        gabr = dabr_ref[...] + dnr
        gabi = dabi_ref[...] + dni
        dphi = gabr * abr + gabi * abi
        dth = gabi * abr - gabr * abi
        dar_ref[...] = dlr + dphi * dt
        dai_ref[...] = dli + dth * dt
        dldt_ref[...] = jnp.sum(dphi * lr + dth * li, axis=2, keepdims=True) * dt

    return pl.pallas_call(
        body, name="ssm_discretize_bwd",
        out_shape=[_sds(a_re.shape)] * 2 + [_sds(log_dt.shape)] + [_sds(b_re_t.shape)] * 2,
    )(a_re, a_im, log_dt, b_re_t, b_im_t, dabr, dabi, dbbr, dbbi)


def _ssm_power(abr, abi, n_sq):
    def body(r_ref, i_ref, or_ref, oi_ref):
        r, i = r_ref[...], i_ref[...]
        for _ in range(n_sq):
            r, i = r * r - i * i, 2.0 * r * i
        or_ref[...] = r
        oi_ref[...] = i

    return pl.pallas_call(body, name="ssm_power", out_shape=[_sds(abr.shape)] * 2)(abr, abi)


N_CB = SSM_W // LANES
CB_STATES = N_STATE // N_CB
ROWS = N_SEG * SSM_LK


class _Neg:
    def __init__(self, ref):
        self.ref = ref

    def __getitem__(self, idx):
        return -self.ref[idx]


def _seg_init(fin_r, fin_i, pw_r, pw_i, x_r, x_i, reverse):
    zero = jnp.zeros((1, N_STATE), F32)
    cr, ci = zero, zero
    order = range(N_SEG - 1, -1, -1) if reverse else range(N_SEG)
    pr = pw_r[...]
    pi = -pw_i[...] if reverse else pw_i[...]
    for j in order:
        x_r[pl.ds(j, 1), :] = cr
        x_i[pl.ds(j, 1), :] = ci
        fr, fi = fin_r[pl.ds(j, 1), :], fin_i[pl.ds(j, 1), :]
        cr, ci = fr + pr * cr - pi * ci, fi + pr * ci + pi * cr


def _permute_in(src_ref, dst):
    for c in range(N_CB):
        dst[c] = src_ref[:, :, pl.ds(c * LANES, LANES)].reshape(ROWS, LANES)


def _permute_out(src, dst_ref):
    for c in range(N_CB):
        dst_ref[:, :, pl.ds(c * LANES, LANES)] = src[c].reshape(SSM_LK, N_SEG, LANES)


def _scan_block(a_r, a_i, c, b_r, b_i, b_off, x_r, x_i, reverse, acc=None):
    cols = pl.ds(c * CB_STATES, CB_STATES)
    ar = jnp.broadcast_to(a_r[:, cols], (N_SEG, CB_STATES))
    ai = jnp.broadcast_to(a_i[:, cols], (N_SEG, CB_STATES))
    xr, xi = x_r[:, cols], x_i[:, cols]
    if acc is not None:
        sr = jnp.zeros((N_SEG, CB_STATES), F32)
        si = jnp.zeros((N_SEG, CB_STATES), F32)
    for t in range(SSM_LK):
        k = (SSM_LK - 1 - t) if reverse else t
        rows = pl.ds(k * N_SEG + b_off, N_SEG)
        xr, xi = ar * xr - ai * xi + b_r[rows, :], ar * xi + ai * xr + b_i[rows, :]
        b_r[rows, :] = xr
        b_i[rows, :] = xi
        if acc is not None:
            pr, pi = acc[0][pl.ds(k * N_SEG, N_SEG), :], acc[1][pl.ds(k * N_SEG, N_SEG), :]
            sr = sr + xr * pr + xi * pi
            si = si + xi * pr - xr * pi
    x_r[:, cols] = xr
    x_i[:, cols] = xi
    if acc is not None:
        acc[2][:, cols] += sr
        acc[3][:, cols] += si


def _ssm_fwd(u3, abr, abi, pw_r, pw_i, fin_r, fin_i, bb_r, bb_i, cc_r, cc_i, dskip, finals_only):
    sl = u3.shape[0]
    nch = sl // SSM_LK

    def body(u_ref, abr_ref, abi_ref, pwr_ref, pwi_ref, finr_ref, fini_ref, bbr_ref, bbi_ref,
             ccr_ref, cci_ref, d_ref, *rest):
        if finals_only:
            xfr_ref, xfi_ref, up, x_r, x_i = rest[:5]
        else:
            y_ref, xsr_ref, xsi_ref, up, yp, x_r, x_i = rest[:7]
        xs_r, xs_i = rest[-2 * N_CB:-N_CB], rest[-N_CB:]
        k = pl.program_id(0)

        @pl.when(k == 0)
        def _():
            _seg_init(finr_ref, fini_ref, pwr_ref, pwi_ref, x_r, x_i, False)

        if not finals_only:
            xsr_ref[0] = x_r[...]
            xsi_ref[0] = x_i[...]
        _permute_in(u_ref, up)

        def drive(c):
            lhs = up[c].astype(BF16)
            xs_r[c][...] = _dot(lhs, bbr_ref[c])
            xs_i[c][...] = _dot(lhs, bbi_ref[c])

        def readout(c):
            yp[c] = (_dot(xs_r[c][...].astype(BF16), ccr_ref[c]) - _dot(xs_i[c][...].astype(BF16), cci_ref[c])
                     + d_ref[:, pl.ds(c * LANES, LANES)] * up[c])

        drive(0)
        for c in range(N_CB):
            if c + 1 < N_CB:
                drive(c + 1)
            if c >= 1 and not finals_only:
                readout(c - 1)
            _scan_block(abr_ref, abi_ref, c, xs_r[c], xs_i[c], 0, x_r, x_i, False)
        if finals_only:
            @pl.when(k == nch - 1)
            def _():
                xfr_ref[...] = x_r[...]
                xfi_ref[...] = x_i[...]
        else:
            readout(N_CB - 1)
            _permute_out(yp, y_ref)

    ublk = pl.BlockSpec((SSM_LK, N_SEG, SSM_W), lambda k: (k, 0, 0))
    st = pl.BlockSpec((1, N_SEG, N_STATE), lambda k: (k, 0, 0))
    vec = _full((1, N_STATE))
    mat = _full((N_SEG, N_STATE))
    chunk = pltpu.VMEM((N_CB, ROWS, LANES), F32)
    blocks = [pltpu.VMEM((ROWS, CB_STATES), F32)] * (2 * N_CB)
    small = pltpu.VMEM((N_SEG, N_STATE), F32)
    if finals_only:
        out_specs, out_shape = [mat, mat], [_sds((N_SEG, N_STATE))] * 2
        scratch, name = [chunk, small, small] + blocks, "ssm_fwd_finals"
    else:
        out_specs = [ublk, st, st]
        out_shape = [_sds(u3.shape)] + [_sds((nch, N_SEG, N_STATE))] * 2
        scratch, name = [chunk, chunk, small, small] + blocks, "ssm_fwd"
    return pl.pallas_call(
        body, name=name, grid=(nch,),
        in_specs=[ublk, vec, vec, vec, vec, mat, mat,
                  _full((N_CB, LANES, CB_STATES)), _full((N_CB, LANES, CB_STATES)),
                  _full((N_CB, CB_STATES, LANES)), _full((N_CB, CB_STATES, LANES)), _full((1, SSM_W))],
        out_specs=out_specs, out_shape=out_shape, scratch_shapes=scratch,
        compiler_params=_cp(("arbitrary",)),
    )(u3, abr, abi, pw_r, pw_i, fin_r, fin_i, bb_r, bb_i, cc_r, cc_i, dskip)


def _ssm_bwd(u3, dy3, xst_r, xst_i, abr, abi, pw_r, pw_i, fin_r, fin_i, bb_r, bb_i, bbt_r, bbt_i,
             cct_r, cct_i, dskip, finals_only):
    sl = u3.shape[0]
    nch = sl // SSM_LK

    def body(u_ref, g_ref, xsr_ref, xsi_ref, abr_ref, abi_ref, pwr_ref, pwi_ref,
             finr_ref, fini_ref, bbr_ref, bbi_ref, btr_ref, bti_ref, ctr_ref, cti_ref, d_ref, *rest):
        if finals_only:
            lfr_ref, lfi_ref, gp, lam_r, lam_i = rest[:5]
            l_r, l_i = rest[-2 * N_CB:-N_CB], rest[-N_CB:]
        else:
            (du_ref, dar_ref, dai_ref, dbr_ref, dbi_ref, dcr_ref, dci_ref, dd_ref,
             gp, up, yp, lam_r, lam_i, x_r, x_i, sar, sai, sdd) = rest[:18]
            l_r, l_i = rest[18:18 + N_CB], rest[18 + N_CB:18 + 2 * N_CB]
            xx_r, xx_i = rest[18 + 2 * N_CB:18 + 3 * N_CB], rest[18 + 3 * N_CB:]
        t = pl.program_id(0)

        @pl.when(t == 0)
        def _():
            _seg_init(finr_ref, fini_ref, pwr_ref, pwi_ref, lam_r, lam_i, True)
            if not finals_only:
                sar[...] = jnp.zeros_like(sar)
                sai[...] = jnp.zeros_like(sai)
                sdd[...] = jnp.zeros_like(sdd)
                dbr_ref[...] = jnp.zeros_like(dbr_ref)
                dbi_ref[...] = jnp.zeros_like(dbi_ref)
                dcr_ref[...] = jnp.zeros_like(dcr_ref)
                dci_ref[...] = jnp.zeros_like(dci_ref)

        _permute_in(g_ref, gp)
        if not finals_only:
            _permute_in(u_ref, up)
            x_r[...] = xsr_ref[0]
            x_i[...] = xsi_ref[0]

        def drive(c):
            lhs = gp[c].astype(BF16)
            l_r[c][...] = _dot(lhs, ctr_ref[c])
            l_i[c][...] = -_dot(lhs, cti_ref[c])
            if not finals_only:
                cols = pl.ds(c * CB_STATES, CB_STATES)
                xx_r[c][pl.ds(0, N_SEG), :] = x_r[:, cols]
                xx_i[c][pl.ds(0, N_SEG), :] = x_i[:, cols]
                ub = up[c].astype(BF16)
                xx_r[c][pl.ds(N_SEG, ROWS), :] = _dot(ub, bbr_ref[c])
                xx_i[c][pl.ds(N_SEG, ROWS), :] = _dot(ub, bbi_ref[c])

        def collect(c):
            lrb = l_r[c][...].astype(BF16)
            lib = l_i[c][...].astype(BF16)
            ub = up[c].astype(BF16)
            gb = gp[c].astype(BF16)
            dbr_ref[c] += _dot_tn(lrb, ub)
            dbi_ref[c] += _dot_tn(lib, ub)
            dcr_ref[c] += _dot_tn(gb, xx_r[c][pl.ds(N_SEG, ROWS), :].astype(BF16))
            dci_ref[c] += -_dot_tn(gb, xx_i[c][pl.ds(N_SEG, ROWS), :].astype(BF16))
            yp[c] = _dot(lrb, btr_ref[c]) + _dot(lib, bti_ref[c]) + d_ref[:, pl.ds(c * LANES, LANES)] * gp[c]
            prod = gp[c] * up[c]
            sdd[:, pl.ds(c * LANES, LANES)] += jnp.sum(prod.reshape(SSM_LK, N_SEG, LANES), axis=0)

        drive(0)
        for c in range(N_CB):
            if c + 1 < N_CB:
                drive(c + 1)
            if finals_only:
                _scan_block(abr_ref, _Neg(abi_ref), c, l_r[c], l_i[c], 0, lam_r, lam_i, True)
            else:
                if c >= 1:
                    collect(c - 1)
                _scan_block(abr_ref, abi_ref, c, xx_r[c], xx_i[c], N_SEG, x_r, x_i, False)
                _scan_block(abr_ref, _Neg(abi_ref), c, l_r[c], l_i[c], 0, lam_r, lam_i, True,
                            acc=(xx_r[c], xx_i[c], sar, sai))
        if finals_only:
            @pl.when(t == nch - 1)
            def _():
                lfr_ref[...] = lam_r[...]
                lfi_ref[...] = lam_i[...]
        else:
            collect(N_CB - 1)
            _permute_out(yp, du_ref)

            @pl.when(t == nch - 1)
            def _():
                dar_ref[...] = jnp.sum(sar[...], axis=0, keepdims=True)
                dai_ref[...] = jnp.sum(sai[...], axis=0, keepdims=True)
                dd_ref[...] = jnp.sum(sdd[...], axis=0, keepdims=True)

    ublk = pl.BlockSpec((SSM_LK, N_SEG, SSM_W), lambda t: (nch - 1 - t, 0, 0))
    st = pl.BlockSpec((1, N_SEG, N_STATE), lambda t: (nch - 1 - t, 0, 0))
    vec = _full((1, N_STATE))
    mat = _full((N_SEG, N_STATE))
    cs = _full((N_CB, LANES, CB_STATES))
    sc = _full((N_CB, CB_STATES, LANES))
    in_specs = [ublk, ublk, st, st, vec, vec, vec, vec, mat, mat, cs, cs, sc, sc, cs, cs, _full((1, SSM_W))]
    chunk = pltpu.VMEM((N_CB, ROWS, LANES), F32)
    blocks = [pltpu.VMEM((ROWS, CB_STATES), F32)] * (2 * N_CB)
    small = pltpu.VMEM((N_SEG, N_STATE), F32)
    if finals_only:
        out_specs, out_shape = [mat, mat], [_sds((N_SEG, N_STATE))] * 2
        scratch, name = [chunk, small, small] + blocks, "ssm_bwd_finals"
    else:
        out_specs = [ublk, vec, vec, sc, sc, cs, cs, _full((1, SSM_W))]
        out_shape = ([_sds(u3.shape), _sds((1, N_STATE)), _sds((1, N_STATE))]
                     + [_sds((N_CB, CB_STATES, LANES))] * 2 + [_sds((N_CB, LANES, CB_STATES))] * 2
                     + [_sds((1, SSM_W))])
        scratch = ([chunk, chunk, chunk, small, small, small, small, small, small, pltpu.VMEM((N_SEG, SSM_W), F32)]
                   + blocks + [pltpu.VMEM((ROWS + N_SEG, CB_STATES), F32)] * (2 * N_CB))
        name = "ssm_bwd"
    return pl.pallas_call(
        body, name=name, grid=(nch,), in_specs=in_specs, out_specs=out_specs, out_shape=out_shape,
        scratch_shapes=scratch, compiler_params=_cp(("arbitrary",)),
    )(u3, dy3, xst_r, xst_i, abr, abi, pw_r, pw_i, fin_r, fin_i, bb_r, bb_i, bbt_r, bbt_i, cct_r, cct_i, dskip)


def _row(tm, w):
    return pl.BlockSpec((tm, w), lambda i: (i, 0))


def _acc_rows(ref, rows, first):
    @pl.when(first)
    def _():
        ref[...] = jnp.zeros_like(ref)

    ref[...] += jnp.sum(rows, axis=0, keepdims=True)


def _fwd_mix(attn, y, x, glu_w, glu_b, ga, gs, w_out, g2, tm=1024):
    s = x.shape[0]

    def body(a_ref, y_ref, x_ref, gw_ref, gb_ref, ga_ref, gs_ref, wo_ref, g2_ref, mix_ref, x2_ref, h_ref):
        a = a_ref[...]
        anb = ((a * _rms(a)) * ga_ref[...]).astype(BF16)
        z, _ = _gelu(y_ref[...])
        so = z * jax.nn.sigmoid(_dot(z.astype(BF16), gw_ref[...]) + gb_ref[...])
        snb = ((so * _rms(so)) * gs_ref[...]).astype(BF16)
        mix_ref[:, pl.ds(0, ATTN_W)] = anb
        mix_ref[:, pl.ds(ATTN_W, SSM_W)] = snb
        x2 = x_ref[...] + (_dot(anb, wo_ref[pl.ds(0, ATTN_W), :]) + _dot(snb, wo_ref[pl.ds(ATTN_W, SSM_W), :]))
        x2_ref[...] = x2
        h_ref[...] = ((x2 * _rms(x2)) * g2_ref[...]).astype(BF16)

    return pl.pallas_call(
        body, name="fwd_mix", grid=(s // tm,),
        in_specs=[_row(tm, ATTN_W), _row(tm, SSM_W), _row(tm, D_MODEL), _full((SSM_W, SSM_W)), _full((1, SSM_W)),
                  _full((1, ATTN_W)), _full((1, SSM_W)), _full((D_MODEL, D_MODEL)), _full((1, D_MODEL))],
        out_specs=[_row(tm, D_MODEL), _row(tm, D_MODEL), _row(tm, D_MODEL)],
        out_shape=[_sds((s, D_MODEL), BF16), _sds((s, D_MODEL)), _sds((s, D_MODEL), BF16)],
        compiler_params=_cp(("parallel",)),
    )(attn, y, x, glu_w, glu_b, ga, gs, w_out, g2)


def _mlp_up(h, w_up, tm=512, bn=1024):
    s = h.shape[0]

    def body(h_ref, w_ref, r_ref, hdn_ref):
        hv = h_ref[...]
        for j in range(D_FF // bn):
            cols = pl.ds(j * bn, bn)
            r = jnp.maximum(_dot(hv, w_ref[:, cols]), 0.0)
            r_ref[:, cols] = r.astype(BF16)
            hdn_ref[:, cols] = (r * r).astype(BF16)

    return pl.pallas_call(
        body, name="mlp_up", grid=(s // tm,),
        in_specs=[_row(tm, D_MODEL), _full((D_MODEL, D_FF))],
        out_specs=[_row(tm, D_FF), _row(tm, D_FF)], out_shape=[_sds((s, D_FF), BF16)] * 2,
        compiler_params=_cp(("parallel",)),
    )(h, w_up)


def _mlp_down_loss(hdn, w_down, x2, tgt, tm=512):
    s = x2.shape[0]

    def body(hdn_ref, w_ref, x2_ref, t_ref, dy_ref, dyb_ref, sse_ref):
        err = (x2_ref[...] + _dot(hdn_ref[...], w_ref[...])) - t_ref[...]
        dy = err * (1.0 / D_MODEL)
        dy_ref[...] = dy
        dyb_ref[...] = dy.astype(BF16)

        @pl.when(pl.program_id(0) == 0)
        def _():
            sse_ref[...] = jnp.zeros_like(sse_ref)

        sse_ref[...] += jnp.sum(jnp.sum(err * err, axis=0, keepdims=True), axis=1, keepdims=True)

    return pl.pallas_call(
        body, name="mlp_down_loss", grid=(s // tm,),
        in_specs=[_row(tm, D_FF), _full((D_FF, D_MODEL)), _row(tm, D_MODEL), _row(tm, D_MODEL)],
        out_specs=[_row(tm, D_MODEL), _row(tm, D_MODEL), _full((1, 1))],
        out_shape=[_sds((s, D_MODEL)), _sds((s, D_MODEL), BF16), _sds((1, 1))],
        compiler_params=_cp(("arbitrary",)),
    )(hdn, w_down, x2, tgt)


def _mlp_down_bwd(dyb, w_down_t, r, tm=512, bn=1024):
    s = dyb.shape[0]

    def body(dy_ref, w_ref, r_ref, dup_ref):
        dyv = dy_ref[...]
        for j in range(D_FF // bn):
            cols = pl.ds(j * bn, bn)
            dup_ref[:, cols] = (_dot(dyv, w_ref[:, cols]) * (2.0 * r_ref[:, cols].astype(F32))).astype(BF16)

    return pl.pallas_call(
        body, name="mlp_down_bwd", grid=(s // tm,),
        in_specs=[_row(tm, D_MODEL), _full((D_MODEL, D_FF)), _row(tm, D_FF)],
        out_specs=_row(tm, D_FF), out_shape=_sds((s, D_FF), BF16),
        compiler_params=_cp(("parallel",)),
    )(dyb, w_down_t, r)


def _mlp_up_bwd(dup, w_up_t, x2, g2, dy, tm=512):
    s = x2.shape[0]

    def body(dup_ref, w_ref, x2_ref, g2_ref, dy_ref, dx2_ref, dx2b_ref, dg_ref):
        dx, dg_rows = _rms_bwd(_dot(dup_ref[...], w_ref[...]), x2_ref[...], g2_ref[...], D_MODEL)
        dx2 = dy_ref[...] + dx
        dx2_ref[...] = dx2
        dx2b_ref[...] = dx2.astype(BF16)
        _acc_rows(dg_ref, dg_rows, pl.program_id(0) == 0)

    return pl.pallas_call(
        body, name="mlp_up_bwd", grid=(s // tm,),
        in_specs=[_row(tm, D_FF), _full((D_FF, D_MODEL)), _row(tm, D_MODEL), _full((1, D_MODEL)), _row(tm, D_MODEL)],
        out_specs=[_row(tm, D_MODEL), _row(tm, D_MODEL), _full((1, D_MODEL))],
        out_shape=[_sds((s, D_MODEL)), _sds((s, D_MODEL), BF16), _sds((1, D_MODEL))],
        compiler_params=_cp(("arbitrary",)),
    )(dup, w_up_t, x2, g2, dy)


def _mix_bwd(dx2b, w_out_t, attn, y, glu_w, glu_b, glu_w_t, ga, gs, tm=1024):
    s = attn.shape[0]

    def body(dx2_ref, wot_ref, a_ref, y_ref, gw_ref, gb_ref, gwt_ref, ga_ref, gs_ref,
             da_ref, dys_ref, z_ref, dpre_ref, dga_ref, dgs_ref, dgb_ref):
        first = pl.program_id(0) == 0
        dmix = _dot(dx2_ref[...], wot_ref[...])
        da, dga_rows = _rms_bwd(dmix[:, :ATTN_W], a_ref[...], ga_ref[...], ATTN_W)
        da_ref[...] = da
        yv = y_ref[...]
        z, t = _gelu(yv)
        gate = jax.nn.sigmoid(_dot(z.astype(BF16), gw_ref[...]) + gb_ref[...])
        dso, dgs_rows = _rms_bwd(dmix[:, ATTN_W:], z * gate, gs_ref[...], SSM_W)
        dpre = dso * z * gate * (1.0 - gate)
        dpre_b = dpre.astype(BF16)
        dz = dso * gate + _dot(dpre_b, gwt_ref[...])
        dgelu = 0.5 * (1.0 + t) + 0.5 * yv * (1.0 - t * t) * (GELU_C * (1.0 + 3.0 * 0.044715 * (yv * yv)))
        dys_ref[...] = dz * dgelu
        z_ref[...] = z.astype(BF16)
        dpre_ref[...] = dpre_b
        _acc_rows(dga_ref, dga_rows, first)
        _acc_rows(dgs_ref, dgs_rows, first)
        _acc_rows(dgb_ref, dpre, first)

    vec = _full((1, SSM_W))
    return pl.pallas_call(
        body, name="mix_bwd", grid=(s // tm,),
        in_specs=[_row(tm, D_MODEL), _full((D_MODEL, D_MODEL)), _row(tm, ATTN_W), _row(tm, SSM_W),
                  _full((SSM_W, SSM_W)), vec, _full((SSM_W, SSM_W)), vec, vec],
        out_specs=[_row(tm, ATTN_W), _row(tm, SSM_W), _row(tm, SSM_W), _row(tm, SSM_W), vec, vec, vec],
        out_shape=[_sds((s, ATTN_W)), _sds((s, SSM_W)), _sds((s, SSM_W), BF16), _sds((s, SSM_W), BF16),
                   _sds((1, ATTN_W)), _sds((1, SSM_W)), _sds((1, SSM_W))],
        compiler_params=_cp(("arbitrary",)),
    )(dx2b, w_out_t, attn, y, glu_w, glu_b, glu_w_t, ga, gs)


def _qk_bwd(dqn, dkn, qk, dv, du, gq, gk, gmat, tm=1024):
    s = qk.shape[0]

    def body(dq_ref, dk_ref, qk_ref, dv_ref, du_ref, gq_ref, gk_ref, gm_ref, dp_ref, dgq_ref, dgk_ref):
        first = pl.program_id(0) == 0
        gm = gm_ref[...]
        for idx, (d_ref, g_ref, dg_ref) in enumerate(((dq_ref, gq_ref, dgq_ref), (dk_ref, gk_ref, dgk_ref))):
            xv = qk_ref[:, pl.ds(idx * ATTN_W, ATTN_W)]
            dyv = d_ref[...]
            r = lax.rsqrt(_group_mean(xv * xv, gm) + EPS)
            gdy = dyv * g_ref[...]
            dx = r * gdy - xv * (r * r * r) * _group_mean(gdy * xv, gm)
            dp_ref[:, pl.ds(idx * ATTN_W, ATTN_W)] = dx.astype(BF16)
            _acc_rows(dg_ref, dyv * (xv * r), first)
        dp_ref[:, pl.ds(2 * ATTN_W, ATTN_W)] = dv_ref[...].astype(BF16)
        dp_ref[:, pl.ds(3 * ATTN_W, SSM_W)] = du_ref[...].astype(BF16)

    vec = _full((1, ATTN_W))
    return pl.pallas_call(
        body, name="qk_bwd", grid=(s // tm,),
        in_specs=[_row(tm, ATTN_W), _row(tm, ATTN_W), _row(tm, 2 * ATTN_W), _row(tm, ATTN_W), _row(tm, SSM_W),
                  vec, vec, _full((ATTN_W, ATTN_W))],
        out_specs=[_row(tm, 4 * ATTN_W), vec, vec],
        out_shape=[_sds((s, 4 * ATTN_W), BF16), _sds((1, ATTN_W)), _sds((1, ATTN_W))],
        compiler_params=_cp(("arbitrary",)),
    )(dqn, dkn, qk, dv, du, gq, gk, gmat)


def _in_bwd(dproj, w_in_t, x, g1, dx2, ex=None, tm=1024):
    s = x.shape[0]
    steps = s // tm

    def body(*refs):
        (dp_ref, w_ref, x_ref, g1_ref, dx2_ref), (gx_ref, dg_ref), _, hx = _carry(ex, refs, 5, 2)
        if ex is not None:
            @pl.when(pl.program_id(0) == 0)
            def _():
                ex.start(*hx)

        dx, dg_rows = _rms_bwd(_dot(dp_ref[...], w_ref[...]), x_ref[...], g1_ref[...], D_MODEL)
        gx_ref[...] = dx2_ref[...] + dx
        _acc_rows(dg_ref, dg_rows, pl.program_id(0) == 0)
        if ex is not None:
            @pl.when(pl.program_id(0) == steps - 1)
            def _():
                ex.wait(*hx)

    hosted = ex is not None
    return pl.pallas_call(
        body, name="in_bwd", grid=(steps,),
        in_specs=[_row(tm, 4 * ATTN_W), _full((4 * ATTN_W, D_MODEL)), _row(tm, D_MODEL), _full((1, D_MODEL)),
                  _row(tm, D_MODEL)] + (ex.specs() if hosted else []),
        out_specs=[_row(tm, D_MODEL), _full((1, D_MODEL))] + (ex.specs() if hosted else []),
        out_shape=[_sds((s, D_MODEL)), _sds((1, D_MODEL))] + (ex.out_shape() if hosted else []),
        scratch_shapes=ex.scratch() if hosted else [],
        compiler_params=_cp(("arbitrary",)),
    )(dproj, w_in_t, x, g1, dx2, *(ex.srcs if hosted else []))


def _mm_tn(a, b, name, ts=2048):
    s, k = a.shape
    n = b.shape[1]
    bk, bn = min(k, 1024), min(n, 1024)
    steps = s // ts

    def body(a_ref, b_ref, o_ref, acc):
        t = pl.program_id(2)

        @pl.when(t == 0)
        def _():
            acc[...] = jnp.zeros_like(acc)

        acc[...] += _dot_tn(a_ref[...], b_ref[...])

        @pl.when(t == steps - 1)
        def _():
            o_ref[...] = acc[...].astype(BF16)

    return pl.pallas_call(
        body, name=name, grid=(k // bk, n // bn, steps),
        in_specs=[pl.BlockSpec((ts, bk), lambda i, j, t: (t, i)), pl.BlockSpec((ts, bn), lambda i, j, t: (t, j))],
        out_specs=pl.BlockSpec((bk, bn), lambda i, j, t: (i, j)), out_shape=_sds((k, n), BF16),
        scratch_shapes=[pltpu.VMEM((bk, bn), F32)],
        compiler_params=_cp(("parallel", "parallel", "arbitrary")),
    )(a, b)


def _peer(k):
    x, y, c = lax.axis_index("x"), lax.axis_index("y"), lax.axis_index("c")
    px = 1 - x if k & 4 else x
    py = 1 - y if k & 2 else y
    pc = 1 - c if k & 1 else c
    return (px, py, pc), 4 * px + 2 * py + pc


def _gather_rows(x_shard):
    m_per, n = x_shard.shape

    def body(x_ref, out_ref, send_sems, recv_sems, local_sem):
        x, y, c = lax.axis_index("x"), lax.axis_index("y"), lax.axis_index("c")
        me, sibling = (x, y, c), (x, y, 1 - c)
        chips = [(1 - x, y), (x, 1 - y), (1 - x, 1 - y)]

        def rows(px, py, pc):
            return out_ref.at[pl.ds((4 * px + 2 * py + pc) * m_per, m_per), :]

        def copy(k, block, to, src=None):
            return pltpu.make_async_remote_copy(
                src_ref=rows(*block) if src is None else src, dst_ref=rows(*block),
                send_sem=send_sems.at[k], recv_sem=recv_sems.at[k], device_id=to, device_id_type=MESH)

        mine = pltpu.make_async_copy(x_ref, rows(*me), local_sem)
        mine.start()
        first = [copy(0, me, sibling, src=x_ref)]
        first += [copy(1 + j, me, (*chip, c), src=x_ref) for j, chip in enumerate(chips)]
        for cp in first:
            cp.start()
        passed = [copy(4 + j, (*chip, c), sibling) for j, chip in enumerate(chips)]
        for j, chip in enumerate(chips):
            copy(1 + j, (*chip, c), me).wait_recv()
            passed[j].start()
        copy(0, sibling, me).wait_recv()
        for j, chip in enumerate(chips):
            copy(4 + j, (*chip, 1 - c), me).wait_recv()
        for cp in first + passed:
            cp.wait_send()
        mine.wait()

    return pl.pallas_call(
        body, name="gather_weights", out_shape=_sds((N_DEV * m_per, n), x_shard.dtype),
        in_specs=[pl.BlockSpec(memory_space=pltpu.VMEM)], out_specs=pl.BlockSpec(memory_space=pltpu.VMEM),
        scratch_shapes=[pltpu.SemaphoreType.DMA((7,)), pltpu.SemaphoreType.DMA((7,)), pltpu.SemaphoreType.DMA],
        compiler_params=pltpu.CompilerParams(vmem_limit_bytes=VMEM_LIMIT),
    )(x_shard)


class _Exchange:
    def __init__(self, srcs, whole):
        self.srcs, self.whole, self.n = list(srcs), list(whole), len(srcs)
        self.rows = [a.shape[0] // N_DEV if w is False else a.shape[0] for a, w in zip(self.srcs, self.whole)]
        self.cols = [a.shape[1] // N_DEV if w == "col" else a.shape[1] for a, w in zip(self.srcs, self.whole)]

    def specs(self):
        return [pl.BlockSpec(memory_space=pl.ANY)] * self.n

    def out_shape(self):
        return [_sds((N_DEV, r, c), a.dtype) for r, c, a in zip(self.rows, self.cols, self.srcs)]

    def scratch(self):
        return [pltpu.SemaphoreType.DMA((self.n * 7,)), pltpu.SemaphoreType.DMA((self.n * 7,)),
                pltpu.SemaphoreType.DMA((self.n,))]

    def _copies(self, ins, outs, sems):
        send_sems, recv_sems, local_sems = sems
        _, me = _peer(0)
        for w in range(self.n):
            for k in range(N_DEV):
                peer, pidx = _peer(k)
                if self.whole[w] is True:
                    src = ins[w]
                elif self.whole[w] == "col":
                    src = ins[w].at[:, pl.ds(pl.multiple_of(pidx * self.cols[w], self.cols[w]), self.cols[w])]
                else:
                    src = ins[w].at[pl.ds(pidx * self.rows[w], self.rows[w]), :]
                if k == 0:
                    yield k, pltpu.make_async_copy(src, outs[w].at[me], local_sems.at[w]), None
                else:
                    sem = w * 7 + k - 1
                    out = pltpu.make_async_remote_copy(src_ref=src, dst_ref=outs[w].at[me], send_sem=send_sems.at[sem],
                                                       recv_sem=recv_sems.at[sem], device_id=peer, device_id_type=MESH)
                    back = pltpu.make_async_remote_copy(src_ref=src, dst_ref=outs[w].at[pidx], send_sem=send_sems.at[sem],
                                                        recv_sem=recv_sems.at[sem], device_id=peer, device_id_type=MESH)
                    yield k, out, back

    def start(self, ins, outs, sems):
        for _, out, _ in self._copies(ins, outs, sems):
            out.start()

    def wait(self, ins, outs, sems):
        for k, out, back in self._copies(ins, outs, sems):
            if k == 0:
                out.wait()
            else:
                back.wait_recv()
                out.wait_send()


def _carry(ex, refs, n_in, n_out):
    nh = ex.n if ex is not None else 0
    ins, hin = refs[:n_in], refs[n_in:n_in + nh]
    outs = refs[n_in + nh:n_in + nh + n_out]
    hout = refs[n_in + nh + n_out:n_in + 2 * nh + n_out]
    rest = refs[n_in + 2 * nh + n_out:]
    if ex is None:
        return ins, outs, rest, None
    return ins, outs, rest[:-3], (hin, hout, rest[-3:])


def _exchange_now(srcs, whole, name):
    ex = _Exchange(srcs, whole)

    def body(*refs):
        _, _, _, (hin, hout, sems) = _carry(ex, refs, 0, 0)
        ex.start(hin, hout, sems)
        ex.wait(hin, hout, sems)

    return pl.pallas_call(body, name=name, out_shape=ex.out_shape(), in_specs=ex.specs(), out_specs=ex.specs(),
                          scratch_shapes=ex.scratch())(*srcs)


def _adamw(w, m, v, gparts, name):
    r, c = w.shape
    tr = r if r * c <= 256 * 1024 else 128 * 1024 // c

    def body(w_ref, m_ref, v_ref, g_ref, go_ref, d_ref, mo_ref, vo_ref):
        g = g_ref[0].astype(F32)
        for i in range(1, N_DEV):
            g = g + g_ref[i].astype(F32)
        go_ref[...] = g
        d_ref[...], mo_ref[...], vo_ref[...] = _adamw_step(w_ref[...], m_ref[...], v_ref[...], g)

    blk = pl.BlockSpec((tr, c), lambda i: (i, 0))
    return pl.pallas_call(
        body, name=name, grid=(r // tr,),
        in_specs=[blk, blk, blk, pl.BlockSpec((N_DEV, tr, c), lambda i: (0, i, 0))],
        out_specs=[blk] * 4, out_shape=[_sds((r, c))] * 4,
        compiler_params=_cp(("parallel",)),
    )(w, m, v, gparts)


def _adamw_step(w, m, v, g):
    nm = ADAM_B1 * m + (1.0 - ADAM_B1) * g
    nv = ADAM_B2 * v + (1.0 - ADAM_B2) * (g * g)
    m_hat = nm / (1.0 - ADAM_B1 ** ADAM_STEP)
    v_hat = nv / (1.0 - ADAM_B2 ** ADAM_STEP)
    return -ADAM_LR * (m_hat / (jnp.sqrt(v_hat) + ADAM_EPS) + ADAM_WD * w), nm, nv


def _sum_slots(parts):
    def body(p_ref, o_ref):
        g = p_ref[0]
        for i in range(1, N_DEV):
            g = g + p_ref[i]
        o_ref[...] = g

    return pl.pallas_call(body, name="sum_small_grads", out_shape=_sds(parts.shape[1:]))(parts)


def _adamw_native(ws, ms, vs, gs):
    n = len(ws)

    def body(*refs):
        for i in range(n):
            w_ref, m_ref, v_ref, g_ref = refs[i], refs[n + i], refs[2 * n + i], refs[3 * n + i]
            d, nm, nv = _adamw_step(w_ref[...], m_ref[...], v_ref[...], g_ref[...])
            refs[4 * n + i][...] = d
            refs[5 * n + i][...] = nm
            refs[6 * n + i][...] = nv

    outs = pl.pallas_call(body, name="adamw_small", out_shape=[_sds(w.shape) for w in ws] * 3,
                          compiler_params=pltpu.CompilerParams(vmem_limit_bytes=VMEM_LIMIT))(*ws, *ms, *vs, *gs)
    return outs[:n], outs[n:2 * n], outs[2 * n:]


def _block_diag(a, states_first):
    a4 = a.reshape(N_CB, 8, SSM_GROUP, SSM_STATE)
    eye = jnp.eye(8, dtype=a.dtype)
    if states_first:
        return jnp.einsum("bgcp,gh->bgphc", a4, eye).reshape(N_CB, CB_STATES, LANES)
    return jnp.einsum("bgcp,gh->bgchp", a4, eye).reshape(N_CB, LANES, CB_STATES)


def _block_diag_of(full, states_first):
    if states_first:
        picked = jnp.einsum("bgphc,gh->bgcp", full.reshape(N_CB, 8, SSM_STATE, 8, SSM_GROUP), jnp.eye(8, dtype=full.dtype))
    else:
        picked = jnp.einsum("bgchp,gh->bgcp", full.reshape(N_CB, 8, SSM_GROUP, 8, SSM_STATE), jnp.eye(8, dtype=full.dtype))
    return picked.reshape(SSM_GROUPS, SSM_GROUP, SSM_STATE)


SMALL_EARLY = ("ssm_a_re", "ssm_a_im", "ssm_log_dt", "ssm_b_re", "ssm_b_im", "ssm_c_re", "ssm_c_im", "ssm_d", "glu_b",
               "attn_out_norm_g", "ssm_out_norm_g", "norm2_g")
SMALL_MID = ("q_norm_g", "k_norm_g")
SMALL_LATE = ("norm1_g",)
SMALL = SMALL_EARLY + SMALL_MID + SMALL_LATE


def _pack_small(arrs):
    parts = []
    for a in arrs:
        flat = a.reshape(-1)
        rows = -(-flat.shape[0] // (8 * LANES)) * 8
        parts.append(jnp.pad(flat, (0, rows * LANES - flat.shape[0])).reshape(rows, LANES))
    return jnp.concatenate(parts, axis=0)


def _unpack_small(packed, shapes):
    out, r0 = [], 0
    for shp in shapes:
        size = math.prod(shp)
        rows = -(-size // (8 * LANES)) * 8
        out.append(packed[r0:r0 + rows].reshape(-1)[:size].reshape(shp))
        r0 += rows
    return out


def kernel(x, norm1_g, w_in, q_norm_g, k_norm_g, ssm_a_re, ssm_a_im, ssm_log_dt, ssm_b_re, ssm_b_im, ssm_c_re, ssm_c_im, ssm_d, glu_w, glu_b, attn_out_norm_g, ssm_out_norm_g, w_out, norm2_g, w_mlp_up, w_mlp_down, loss_target, m_norm1_g, m_w_in, m_q_norm_g, m_k_norm_g, m_ssm_a_re, m_ssm_a_im, m_ssm_log_dt, m_ssm_b_re, m_ssm_b_im, m_ssm_c_re, m_ssm_c_im, m_ssm_d, m_glu_w, m_glu_b, m_attn_out_norm_g, m_ssm_out_norm_g, m_w_out, m_norm2_g, m_w_mlp_up, m_w_mlp_down, v_norm1_g, v_w_in, v_q_norm_g, v_k_norm_g, v_ssm_a_re, v_ssm_a_im, v_ssm_log_dt, v_ssm_b_re, v_ssm_b_im, v_ssm_c_re, v_ssm_c_im, v_ssm_d, v_glu_w, v_glu_b, v_attn_out_norm_g, v_ssm_out_norm_g, v_w_out, v_norm2_g, v_w_mlp_up, v_w_mlp_down):
    weights = dict(norm1_g=norm1_g, w_in=w_in, q_norm_g=q_norm_g, k_norm_g=k_norm_g, ssm_a_re=ssm_a_re,
                   ssm_a_im=ssm_a_im, ssm_log_dt=ssm_log_dt, ssm_b_re=ssm_b_re, ssm_b_im=ssm_b_im,
                   ssm_c_re=ssm_c_re, ssm_c_im=ssm_c_im, ssm_d=ssm_d, glu_w=glu_w, glu_b=glu_b,
                   attn_out_norm_g=attn_out_norm_g, ssm_out_norm_g=ssm_out_norm_g, w_out=w_out, norm2_g=norm2_g,
                   w_mlp_up=w_mlp_up, w_mlp_down=w_mlp_down)
    mom_m = dict(norm1_g=m_norm1_g, w_in=m_w_in, q_norm_g=m_q_norm_g, k_norm_g=m_k_norm_g, ssm_a_re=m_ssm_a_re,
                 ssm_a_im=m_ssm_a_im, ssm_log_dt=m_ssm_log_dt, ssm_b_re=m_ssm_b_re, ssm_b_im=m_ssm_b_im,
                 ssm_c_re=m_ssm_c_re, ssm_c_im=m_ssm_c_im, ssm_d=m_ssm_d, glu_w=m_glu_w, glu_b=m_glu_b,
                 attn_out_norm_g=m_attn_out_norm_g, ssm_out_norm_g=m_ssm_out_norm_g, w_out=m_w_out,
                 norm2_g=m_norm2_g, w_mlp_up=m_w_mlp_up, w_mlp_down=m_w_mlp_down)
    mom_v = dict(norm1_g=v_norm1_g, w_in=v_w_in, q_norm_g=v_q_norm_g, k_norm_g=v_k_norm_g, ssm_a_re=v_ssm_a_re,
                 ssm_a_im=v_ssm_a_im, ssm_log_dt=v_ssm_log_dt, ssm_b_re=v_ssm_b_re, ssm_b_im=v_ssm_b_im,
                 ssm_c_re=v_ssm_c_re, ssm_c_im=v_ssm_c_im, ssm_d=v_ssm_d, glu_w=v_glu_w, glu_b=v_glu_b,
                 attn_out_norm_g=v_attn_out_norm_g, ssm_out_norm_g=v_ssm_out_norm_g, w_out=v_w_out,
                 norm2_g=v_norm2_g, w_mlp_up=v_w_mlp_up, w_mlp_down=v_w_mlp_down)
    order = list(weights)

    xs, tgt = x[0], loss_target[0]
    s = xs.shape[0]
    assert s % ATTN_CHUNK == 0 and (s // N_SEG) % SSM_LK == 0
    seg_len = s // N_SEG
    n_sq = seg_len.bit_length() - 1
    assert 1 << n_sq == seg_len

    w_in_t = _gather_rows(w_in[0].T.astype(BF16))
    w_in_full = w_in_t.T
    later = _Exchange([glu_w[0].astype(BF16), w_out[0].astype(BF16), w_mlp_up[0].T.astype(BF16),
                       w_mlp_down[0].astype(BF16)], [True] * 4)

    gq = jnp.tile(q_norm_g[0], ATTN_W // HEAD_DIM)[None]
    gk = jnp.tile(k_norm_g[0], ATTN_W // HEAD_DIM)[None]
    lane = jnp.arange(ATTN_W) // HEAD_DIM
    gmat = jnp.where(lane[:, None] == lane[None, :], 1.0 / HEAD_DIM, 0.0).astype(BF16)
    a_re3 = ssm_a_re[0][:, None, :]
    a_im3 = ssm_a_im[0][:, None, :]
    ldt3 = ssm_log_dt[0][:, None, None]
    b_re_t = jnp.swapaxes(ssm_b_re[0], 1, 2)
    b_im_t = jnp.swapaxes(ssm_b_im[0], 1, 2)
    c_re, c_im = ssm_c_re[0], ssm_c_im[0]
    dskip = ssm_d[0].reshape(1, SSM_W)

    qk, qn, kn, v, u, xn = _fwd_proj(xs, norm1_g, w_in_full, gq, gk, gmat)
    attn, lse, glu_g, w_out_g, w_up_g, w_down_g = _attn_fwd(qn, kn, v, later)
    glu_full = glu_g.reshape(SSM_W, SSM_W)
    w_out_full = w_out_g.reshape(D_MODEL, D_MODEL)
    w_up_t = w_up_g.reshape(D_FF, D_MODEL)
    w_down_full = w_down_g.reshape(D_FF, D_MODEL)
    w_up_full, w_out_t, w_down_t, glu_t = w_up_t.T, w_out_full.T, w_down_full.T, glu_full.T

    abr3, abi3, bbr, bbi = _ssm_discretize(a_re3, a_im3, ldt3, b_re_t, b_im_t)
    abr, abi = abr3.reshape(1, N_STATE), abi3.reshape(1, N_STATE)
    pw_r, pw_i = _ssm_power(abr, abi, n_sq)
    bb_r, bb_i = _block_diag(bbr, False).astype(BF16), _block_diag(bbi, False).astype(BF16)
    bbt_r, bbt_i = _block_diag(bbr, True).astype(BF16), _block_diag(bbi, True).astype(BF16)
    cc_r, cc_i = _block_diag(c_re, True).astype(BF16), _block_diag(c_im, True).astype(BF16)
    cct_r, cct_i = _block_diag(c_re, False).astype(BF16), _block_diag(c_im, False).astype(BF16)
    seg_major = lambda a: jnp.swapaxes(a.reshape(N_SEG, seg_len, SSM_W), 0, 1)
    seg_minor = lambda a: jnp.swapaxes(a, 0, 1).reshape(s, SSM_W)
    u3 = seg_major(u)
    zero_fin = jnp.zeros((N_SEG, N_STATE), F32)
    ssm_args = (abr, abi, pw_r, pw_i)
    xf_r, xf_i = _ssm_fwd(u3, *ssm_args, zero_fin, zero_fin, bb_r, bb_i, cc_r, cc_i, dskip, True)
    y3, xst_r, xst_i = _ssm_fwd(u3, *ssm_args, xf_r, xf_i, bb_r, bb_i, cc_r, cc_i, dskip, False)
    y = seg_minor(y3)

    mix, x2, h = _fwd_mix(attn, y, xs, glu_full, glu_b, attn_out_norm_g, ssm_out_norm_g, w_out_full, norm2_g)
    r_act, hdn = _mlp_up(h, w_up_full)
    dy, dyb, sse = _mlp_down_loss(hdn, w_down_full, x2, tgt)

    dup = _mlp_down_bwd(dyb, w_down_t, r_act)
    g_w_down = _mm_tn(hdn, dyb, "grad_w_down")
    dx2, dx2b, g_norm2 = _mlp_up_bwd(dup, w_up_t, x2, norm2_g, dy)
    g_w_up = _mm_tn(h, dup, "grad_w_up")
    dattn, dys, z_b, dpre_b, g_ga, g_gs, g_glu_b = _mix_bwd(dx2b, w_out_t, attn, y, glu_full, glu_b, glu_t,
                                                             attn_out_norm_g, ssm_out_norm_g)
    g_w_out = _mm_tn(mix, dx2b, "grad_w_out")
    g_glu_w = _mm_tn(z_b, dpre_b, "grad_glu_w")
    dy3 = seg_major(dys)
    bwd_args = (u3, dy3, xst_r, xst_i, abr, abi, pw_r, pw_i)
    lf_r, lf_i = _ssm_bwd(*bwd_args, zero_fin, zero_fin, bb_r, bb_i, bbt_r, bbt_i, cct_r, cct_i, dskip, True)
    du3, dab_r, dab_i, dbb_r, dbb_i, dcc_r, dcc_i, g_d = _ssm_bwd(*bwd_args, lf_r, lf_i, bb_r, bb_i, bbt_r, bbt_i,
                                                                 cct_r, cct_i, dskip, False)
    g_a_re3, g_a_im3, g_ldt3, g_b_re_t, g_b_im_t = _ssm_discretize_bwd(
        a_re3, a_im3, ldt3, b_re_t, b_im_t, dab_r.reshape(a_re3.shape), dab_i.reshape(a_re3.shape),
        _block_diag_of(dbb_r, True), _block_diag_of(dbb_i, True))
    g_c_re, g_c_im = _block_diag_of(dcc_r, False), _block_diag_of(dcc_i, False)
    small_grads = dict(
        ssm_a_re=g_a_re3.reshape(ssm_a_re.shape), ssm_a_im=g_a_im3.reshape(ssm_a_im.shape),
        ssm_log_dt=g_ldt3.reshape(ssm_log_dt.shape), ssm_b_re=jnp.swapaxes(g_b_re_t, 1, 2)[None],
        ssm_b_im=jnp.swapaxes(g_b_im_t, 1, 2)[None], ssm_c_re=g_c_re[None], ssm_c_im=g_c_im[None],
        ssm_d=g_d.reshape(ssm_d.shape), glu_b=g_glu_b, attn_out_norm_g=g_ga, ssm_out_norm_g=g_gs, norm2_g=g_norm2)

    early = _Exchange([g_glu_w, g_w_out, g_w_up, g_w_down, _pack_small([small_grads[n] for n in SMALL_EARLY] + [sse])],
                      [False, False, "col", False, True])
    dqn, dkn, dv, p_glu, p_w_out, p_w_up, p_w_down, p_early = _attn_bwd(qn, kn, v, attn, lse, dattn, early)

    dproj, g_gq, g_gk = _qk_bwd(dqn, dkn, qk, dv, seg_minor(du3), gq, gk, gmat)
    g_w_in = _mm_tn(xn, dproj, "grad_w_in")
    small_grads["q_norm_g"] = g_gq.reshape(ATTN_W // HEAD_DIM, HEAD_DIM).sum(0)[None]
    small_grads["k_norm_g"] = g_gk.reshape(ATTN_W // HEAD_DIM, HEAD_DIM).sum(0)[None]
    mid = _Exchange([g_w_in, _pack_small([small_grads[n] for n in SMALL_MID])], ["col", True])
    grad_x, g_norm1, p_w_in, p_mid = _in_bwd(dproj, w_in_t, xs, norm1_g, dx2, mid)
    (p_late,) = _exchange_now([_pack_small([g_norm1])], [True], "exchange_norm1")

    res = {}
    for name, gp in (("w_in", p_w_in), ("glu_w", p_glu), ("w_out", p_w_out), ("w_mlp_up", p_w_up), ("w_mlp_down", p_w_down)):
        outs = _adamw(weights[name][0], mom_m[name][0], mom_v[name][0], gp, "adamw_" + name)
        res[name] = [o[None] for o in outs]
    shapes = [weights[n].shape for n in SMALL_EARLY] + [sse.shape] + [weights[n].shape for n in SMALL_MID + SMALL_LATE]
    g_small = _unpack_small(_sum_slots(jnp.concatenate([p_early, p_mid, p_late], axis=1)), shapes)
    loss = 0.5 * g_small.pop(len(SMALL_EARLY))[0, 0] / D_MODEL
    d_small, m_small, v_small = _adamw_native([weights[n] for n in SMALL], [mom_m[n] for n in SMALL],
                                              [mom_v[n] for n in SMALL], g_small)
    for i, n in enumerate(SMALL):
        res[n] = [g_small[i], d_small[i], m_small[i], v_small[i]]

    return (loss, grad_x[None], *[res[n][0] for n in order], *[res[n][1] for n in order],
            *[res[n][2] for n in order], *[res[n][3] for n in order])
```

```python
import math

import jax
import jax.numpy as jnp
from jax import lax
from jax.experimental import pallas as pl
from jax.experimental.pallas import tpu as pltpu

F32 = jnp.float32
BF16 = jnp.bfloat16

D_MODEL = 1024
ATTN_W = 512
HEAD_DIM = 64
SSM_W = 512
SSM_GROUP = 16
SSM_GROUPS = 32
SSM_STATE = 64
N_STATE = SSM_GROUPS * SSM_STATE
D_FF = 4096
EPS = 1e-6
NEG_INF = -1e30
ATTN_CHUNK = 2048
ATTN_BLOCK = 128
DILATIONS = (1, 4, 16)
N_SEG = 8
SSM_LK = 64
N_DEV = 8
LANES = 128

ADAM_LR = 0.001
ADAM_B1 = 0.9
ADAM_B2 = 0.999
ADAM_EPS = 1e-08
ADAM_WD = 0.01
ADAM_STEP = 10

VMEM_LIMIT = 56 * 1024 * 1024
GELU_C = math.sqrt(2.0 / math.pi)
MESH = pl.DeviceIdType.MESH


def _cp(sem, vmem=VMEM_LIMIT):
    return pltpu.CompilerParams(dimension_semantics=sem, vmem_limit_bytes=vmem)


def _dot(a, b):
    return jnp.dot(a, b, preferred_element_type=F32)


def _dot_tn(a, b):
    return lax.dot_general(a, b, (((0,), (0,)), ((), ())), preferred_element_type=F32)


def _group_mean(x2, gmat):
    hi = x2.astype(BF16)
    lo = (x2 - hi.astype(F32)).astype(BF16)
    return _dot(hi, gmat) + _dot(lo, gmat)


def _rms(x):
    return lax.rsqrt(jnp.mean(x * x, axis=-1, keepdims=True) + EPS)


def _rms_bwd(dy, x, g, n):
    r = _rms(x)
    gdy = dy * g
    dx = r * gdy - x * (r * r * r) * (jnp.sum(gdy * x, axis=-1, keepdims=True) / n)
    return dx, dy * (x * r)


def _gelu(y):
    t = jnp.tanh(GELU_C * (y + 0.044715 * (y * y * y)))
    return 0.5 * y * (1.0 + t), t


def _full(shape):
    nd = len(shape)
    return pl.BlockSpec(shape, lambda *_: (0,) * nd)


def _sds(shape, dtype=F32):
    return jax.ShapeDtypeStruct(shape, dtype)


def _fwd_proj(x, g1, w_in, gq, gk, gmat, tm=512):
    s = x.shape[0]

    def body(x_ref, g1_ref, w_ref, gq_ref, gk_ref, gm_ref, qk_ref, qn_ref, kn_ref, v_ref, u_ref, xn_ref):
        xv = x_ref[...]
        xnb = ((xv * _rms(xv)) * g1_ref[...]).astype(BF16)
        xn_ref[...] = xnb
        proj = _dot(xnb, w_ref[...])
        q = proj[:, :ATTN_W]
        k = proj[:, ATTN_W:2 * ATTN_W]
        qk_ref[...] = proj[:, :2 * ATTN_W]
        v_ref[...] = proj[:, 2 * ATTN_W:3 * ATTN_W]
        u_ref[...] = proj[:, 3 * ATTN_W:]
        gm = gm_ref[...]
        qn_ref[...] = (q * lax.rsqrt(_group_mean(q * q, gm) + EPS)) * gq_ref[...]
        kn_ref[...] = (k * lax.rsqrt(_group_mean(k * k, gm) + EPS)) * gk_ref[...]

    row = lambda w: pl.BlockSpec((tm, w), lambda i: (i, 0))
    return pl.pallas_call(
        body, name="fwd_proj", grid=(s // tm,),
        in_specs=[row(D_MODEL), _full((1, D_MODEL)), _full((D_MODEL, 4 * ATTN_W)), _full((1, ATTN_W)),
                  _full((1, ATTN_W)), _full((ATTN_W, ATTN_W))],
        out_specs=[row(2 * ATTN_W), row(ATTN_W), row(ATTN_W), row(ATTN_W), row(ATTN_W), row(D_MODEL)],
        out_shape=[_sds((s, 2 * ATTN_W)), _sds((s, ATTN_W)), _sds((s, ATTN_W)), _sds((s, ATTN_W)),
                   _sds((s, ATTN_W)), _sds((s, D_MODEL), BF16)],
        compiler_params=_cp(("parallel",)),
    )(x, g1, w_in, gq, gk, gmat)


def _attn_rows(t, d, nb):
    if d == 1:
        q0 = t * ATTN_BLOCK
        return (t, pl.ds(q0, ATTN_BLOCK), pl.ds(ATTN_CHUNK + q0, ATTN_BLOCK),
                pl.ds(ATTN_CHUNK - ATTN_BLOCK + q0, ATTN_BLOCK))
    r = t // nb
    b = t % nb
    return (b, pl.ds(ATTN_BLOCK * b * d + r, ATTN_BLOCK, stride=d),
            pl.ds(ATTN_CHUNK + ATTN_BLOCK * b * d + r, ATTN_BLOCK, stride=d),
            pl.ds(ATTN_CHUNK + ATTN_BLOCK * (b - 1) * d + r, ATTN_BLOCK, stride=d))


def _attn_masks():
    row = lax.broadcasted_iota(jnp.int32, (ATTN_BLOCK, LANES), 0)
    col = lax.broadcasted_iota(jnp.int32, (ATTN_BLOCK, LANES), 1)
    return row, col


NBLK = ATTN_CHUNK // ATTN_BLOCK


def _attn_bias(bias_s):
    row, col = _attn_masks()
    bias_s[:, pl.ds(0, LANES)] = jnp.where(col >= row, 0.0, NEG_INF)
    bias_s[:, pl.ds(LANES, LANES)] = jnp.where(col <= row, 0.0, NEG_INF)
    return col < HEAD_DIM


def _attn_fwd(qn, kn, v, ex=None, group=4):
    s = qn.shape[0]
    nch = s // ATTN_CHUNK
    scale = HEAD_DIM ** -0.5
    npat = len(DILATIONS)
    n_hp = ATTN_W // LANES

    def body(*refs):
        ((q_ref, kp_ref, kc_ref, vp_ref, vc_ref), (o_ref, lse_ref),
         (kk, vv, kt_s, vb_s, bias_s, m_s, a_s), hx) = _carry(ex, refs, 5, 2)
        i = pl.program_id(1)
        if ex is not None:
            @pl.when(jnp.logical_and(pl.program_id(0) == 0, i == 0))
            def _():
                ex.start(*hx)

        kk[pl.ds(0, ATTN_CHUNK), :] = kp_ref[...]
        kk[pl.ds(ATTN_CHUNK, ATTN_CHUNK), :] = kc_ref[...]
        vv[pl.ds(0, ATTN_CHUNK), :] = vp_ref[...]
        vv[pl.ds(ATTN_CHUNK, ATTN_CHUNK), :] = vc_ref[...]
        head0 = _attn_bias(bias_s)
        first_pen = jnp.where(i > 0, 0.0, NEG_INF)

        for p, d in enumerate(DILATIONS):
            nb = ATTN_CHUNK // (ATTN_BLOCK * d)

            def prep(t, d=d, nb=nb):
                _, _, crows, prows = _attn_rows(t, d, nb)
                kt_s[t, :, pl.ds(0, LANES)] = kk[prows, :].T.astype(BF16)
                kt_s[t, :, pl.ds(LANES, LANES)] = kk[crows, :].T.astype(BF16)
                vp = vv[prows, :]
                vc = vv[crows, :]
                vb_s[2 * t, pl.ds(0, ATTN_BLOCK), :] = jnp.where(head0, vp, 1.0).astype(BF16)
                vb_s[2 * t, pl.ds(ATTN_BLOCK, ATTN_BLOCK), :] = jnp.where(head0, vc, 1.0).astype(BF16)
                vb_s[2 * t + 1, pl.ds(0, ATTN_BLOCK), :] = jnp.where(head0, 1.0, vp).astype(BF16)
                vb_s[2 * t + 1, pl.ds(ATTN_BLOCK, ATTN_BLOCK), :] = jnp.where(head0, 1.0, vc).astype(BF16)

            def main(tg, p=p, d=d, nb=nb):
                st = []
                for g in range(group):
                    t = tg * group + g
                    b, qrows, _, _ = _attn_rows(t, d, nb)
                    q = q_ref[qrows, :] * scale
                    for h in range(2):
                        hm = head0 if h == 0 else jnp.logical_not(head0)
                        st.append(dict(t=t, b=b, qrows=qrows, sc=_dot(jnp.where(hm, q, 0.0).astype(BF16), kt_s[t])))
                for e in st:
                    sc = e["sc"] + bias_s[...]
                    s_p = sc[:, :LANES] + first_pen if e["b"] == 0 else sc[:, :LANES]
                    s_c = sc[:, LANES:]
                    m = jnp.max(jnp.maximum(s_p, s_c), axis=-1, keepdims=True)
                    e["eb"] = jnp.concatenate([jnp.exp(s_p - m), jnp.exp(s_c - m)], axis=1).astype(BF16)
                    e["m"] = jnp.broadcast_to(m, (ATTN_BLOCK, LANES))
                for g in range(group):
                    e0, e1 = st[2 * g], st[2 * g + 1]
                    t = e0["t"]
                    m_s[p, e0["qrows"], :] = jnp.where(head0, e0["m"], e1["m"])
                    a_s[2 * p, e0["qrows"], :] = _dot(e0["eb"], vb_s[2 * t])
                    a_s[2 * p + 1, e0["qrows"], :] = _dot(e1["eb"], vb_s[2 * t + 1])

            for g in range(group):
                prep(g)
            for tg in range(NBLK // group):
                if tg + 1 < NBLK // group:
                    for g in range(group):
                        prep((tg + 1) * group + g)
                main(tg)

        def merge(t, carry):
            rows = pl.ds(pl.multiple_of(t * ATTN_BLOCK, ATTN_BLOCK), ATTN_BLOCK)
            m_all = m_s[0, rows, :]
            for p in range(1, npat):
                m_all = jnp.maximum(m_all, m_s[p, rows, :])
            num = jnp.zeros((ATTN_BLOCK, LANES), F32)
            den = jnp.zeros((ATTN_BLOCK, LANES), F32)
            for p in range(npat):
                w = jnp.exp(m_s[p, rows, :] - m_all)
                a0, a1 = a_s[2 * p, rows, :], a_s[2 * p + 1, rows, :]
                num = num + jnp.where(head0, a0, a1) * w
                den = den + pltpu.roll(jnp.where(head0, a1, a0), HEAD_DIM, 1) * w
            o_ref[rows, :] = num / den
            lse_ref[rows, :] = m_all + jnp.log(den)
            return carry

        lax.fori_loop(0, NBLK, merge, 0, unroll=2)
        if ex is not None:
            @pl.when(jnp.logical_and(pl.program_id(0) == n_hp - 1, i == nch - 1))
            def _():
                ex.wait(*hx)

    cur = pl.BlockSpec((ATTN_CHUNK, LANES), lambda h, i: (i, h))
    prev = pl.BlockSpec((ATTN_CHUNK, LANES), lambda h, i: (jnp.maximum(i - 1, 0), h))
    hosted = ex is not None
    return pl.pallas_call(
        body, name="attn_fwd", grid=(n_hp, nch),
        in_specs=[cur, prev, cur, prev, cur] + (ex.specs() if hosted else []),
        out_specs=[cur, cur] + (ex.specs() if hosted else []),
        out_shape=[_sds((s, ATTN_W)), _sds((s, ATTN_W))] + (ex.out_shape() if hosted else []),
        scratch_shapes=[pltpu.VMEM((2 * ATTN_CHUNK, LANES), F32), pltpu.VMEM((2 * ATTN_CHUNK, LANES), F32),
                        pltpu.VMEM((NBLK, LANES, 2 * LANES), BF16), pltpu.VMEM((2 * NBLK, 2 * ATTN_BLOCK, LANES), BF16),
                        pltpu.VMEM((ATTN_BLOCK, 2 * LANES), F32),
                        pltpu.VMEM((npat, ATTN_CHUNK, LANES), F32), pltpu.VMEM((2 * npat, ATTN_CHUNK, LANES), F32)]
        + (ex.scratch() if hosted else []),
        compiler_params=_cp(("arbitrary", "arbitrary")),
    )(qn, kn, kn, v, v, *(ex.srcs if hosted else []))


def _attn_bwd(qn, kn, v, o, lse, do, ex=None, group=4):
    s = qn.shape[0]
    nch = s // ATTN_CHUNK
    scale = HEAD_DIM ** -0.5
    npat = len(DILATIONS)
    n_hp = ATTN_W // LANES

    def body(*refs):
        ((q_ref, kp_ref, kc_ref, vp_ref, vc_ref, o_ref, lse_ref, do_ref), (dq_ref, dk_ref, dv_ref),
         (kk, vv, dkk, dvv, kt_s, vt_s, kn_s, bias_s, dq_s, dl_s, dkb, dvb), hx) = _carry(ex, refs, 8, 3)
        step = pl.program_id(1)
        i = nch - 1 - step
        if ex is not None:
            @pl.when(jnp.logical_and(pl.program_id(0) == 0, step == 0))
            def _():
                ex.start(*hx)

        kk[pl.ds(0, ATTN_CHUNK), :] = kp_ref[...]
        kk[pl.ds(ATTN_CHUNK, ATTN_CHUNK), :] = kc_ref[...]
        vv[pl.ds(0, ATTN_CHUNK), :] = vp_ref[...]
        vv[pl.ds(ATTN_CHUNK, ATTN_CHUNK), :] = vc_ref[...]

        @pl.when(step == 0)
        def _():
            dkk[pl.ds(ATTN_CHUNK, ATTN_CHUNK), :] = jnp.zeros((ATTN_CHUNK, LANES), F32)
            dvv[pl.ds(ATTN_CHUNK, ATTN_CHUNK), :] = jnp.zeros((ATTN_CHUNK, LANES), F32)

        @pl.when(step > 0)
        def _():
            dkk[pl.ds(ATTN_CHUNK, ATTN_CHUNK), :] = dkk[pl.ds(0, ATTN_CHUNK), :]
            dvv[pl.ds(ATTN_CHUNK, ATTN_CHUNK), :] = dvv[pl.ds(0, ATTN_CHUNK), :]

        dkk[pl.ds(0, ATTN_CHUNK), :] = jnp.zeros((ATTN_CHUNK, LANES), F32)
        dvv[pl.ds(0, ATTN_CHUNK), :] = jnp.zeros((ATTN_CHUNK, LANES), F32)
        head0 = _attn_bias(bias_s)

        def delta(t, carry):
            rows = pl.ds(pl.multiple_of(t * ATTN_BLOCK, ATTN_BLOCK), ATTN_BLOCK)
            prod = do_ref[rows, :] * o_ref[rows, :]
            d0 = jnp.sum(jnp.where(head0, prod, 0.0), axis=-1, keepdims=True)
            d1 = jnp.sum(jnp.where(head0, 0.0, prod), axis=-1, keepdims=True)
            dl_s[rows, :] = jnp.where(head0, d0, d1)
            return carry

        lax.fori_loop(0, NBLK, delta, 0, unroll=2)

        first_pen = jnp.where(i > 0, 0.0, NEG_INF)

        for p, d in enumerate(DILATIONS):
            nb = ATTN_CHUNK // (ATTN_BLOCK * d)

            def prep(t, d=d, nb=nb):
                _, _, crows, prows = _attn_rows(t, d, nb)
                kp, kc = kk[prows, :], kk[crows, :]
                kt_s[t, :, pl.ds(0, LANES)] = kp.T.astype(BF16)
                kt_s[t, :, pl.ds(LANES, LANES)] = kc.T.astype(BF16)
                kn_s[t, pl.ds(0, ATTN_BLOCK), :] = (kp * scale).astype(BF16)
                kn_s[t, pl.ds(ATTN_BLOCK, ATTN_BLOCK), :] = (kc * scale).astype(BF16)
                vt_s[t, :, pl.ds(0, LANES)] = vv[prows, :].T.astype(BF16)
                vt_s[t, :, pl.ds(LANES, LANES)] = vv[crows, :].T.astype(BF16)

            def main(tg, p=p, d=d, nb=nb):
                st = []
                for g in range(group):
                    t = tg * group + g
                    b, qrows, _, _ = _attn_rows(t, d, nb)
                    q = q_ref[qrows, :] * scale
                    dout = do_ref[qrows, :]
                    lse_b = lse_ref[qrows, :]
                    dl_b = dl_s[qrows, :]
                    for h in range(2):
                        hm = head0 if h == 0 else jnp.logical_not(head0)
                        c0 = h * HEAD_DIM
                        qh = jnp.where(hm, q, 0.0).astype(BF16)
                        doh = jnp.where(hm, dout, 0.0).astype(BF16)
                        st.append(dict(t=t, b=b, qrows=qrows, qh=qh, doh=doh, lse=lse_b[:, c0:c0 + 1],
                                       dl=dl_b[:, c0:c0 + 1], sc=_dot(qh, kt_s[t]), dp=_dot(doh, vt_s[t])))
                for e in st:
                    sc = e["sc"] + bias_s[...]
                    if e["b"] == 0:
                        sc = jnp.concatenate([sc[:, :LANES] + first_pen, sc[:, LANES:]], axis=1)
                    pr = jnp.exp(sc - e["lse"])
                    e["ds"] = (pr * (e["dp"] - e["dl"])).astype(BF16)
                    e["pr"] = pr.astype(BF16)
                for g in range(group):
                    e0, e1 = st[2 * g], st[2 * g + 1]
                    t = e0["t"]
                    dq_s[p, e0["qrows"], :] = jnp.where(head0, _dot(e0["ds"], kn_s[t]), _dot(e1["ds"], kn_s[t]))
                    dkb[t] = _dot_tn(e0["ds"], e0["qh"]) + _dot_tn(e1["ds"], e1["qh"])
                    dvb[t] = _dot_tn(e0["pr"], e0["doh"]) + _dot_tn(e1["pr"], e1["doh"])

            def scatter(t, d=d, nb=nb):
                _, _, crows, prows = _attn_rows(t, d, nb)
                dkk[prows, :] = dkk[prows, :] + dkb[t, pl.ds(0, ATTN_BLOCK), :]
                dkk[crows, :] = dkk[crows, :] + dkb[t, pl.ds(ATTN_BLOCK, ATTN_BLOCK), :]
                dvv[prows, :] = dvv[prows, :] + dvb[t, pl.ds(0, ATTN_BLOCK), :]
                dvv[crows, :] = dvv[crows, :] + dvb[t, pl.ds(ATTN_BLOCK, ATTN_BLOCK), :]

            n_groups = NBLK // group
            for g in range(group):
                prep(g)
            for tg in range(n_groups):
                if tg + 1 < n_groups:
                    for g in range(group):
                        prep((tg + 1) * group + g)
                main(tg)
                if tg >= 1:
                    for g in range(group):
                        scatter((tg - 1) * group + g)
            for g in range(group):
                scatter((n_groups - 1) * group + g)

        def finish(t, carry):
            rows = pl.ds(pl.multiple_of(t * ATTN_BLOCK, ATTN_BLOCK), ATTN_BLOCK)
            acc = dq_s[0, rows, :]
            for p in range(1, npat):
                acc = acc + dq_s[p, rows, :]
            dq_ref[rows, :] = acc
            return carry

        lax.fori_loop(0, NBLK, finish, 0, unroll=2)
        dk_ref[...] = dkk[pl.ds(ATTN_CHUNK, ATTN_CHUNK), :]
        dv_ref[...] = dvv[pl.ds(ATTN_CHUNK, ATTN_CHUNK), :]
        if ex is not None:
            @pl.when(jnp.logical_and(pl.program_id(0) == n_hp - 1, step == nch - 1))
            def _():
                ex.wait(*hx)

    cur = pl.BlockSpec((ATTN_CHUNK, LANES), lambda h, t: (nch - 1 - t, h))
    prev = pl.BlockSpec((ATTN_CHUNK, LANES), lambda h, t: (jnp.maximum(nch - 2 - t, 0), h))
    big = pltpu.VMEM((2 * ATTN_CHUNK, LANES), F32)
    pair_t = pltpu.VMEM((NBLK, LANES, 2 * LANES), BF16)
    hosted = ex is not None
    return pl.pallas_call(
        body, name="attn_bwd", grid=(n_hp, nch),
        in_specs=[cur, prev, cur, prev, cur, cur, cur, cur] + (ex.specs() if hosted else []),
        out_specs=[cur, cur, cur] + (ex.specs() if hosted else []),
        out_shape=[_sds((s, ATTN_W))] * 3 + (ex.out_shape() if hosted else []),
        scratch_shapes=[big, big, big, big, pair_t, pair_t, pltpu.VMEM((NBLK, 2 * ATTN_BLOCK, LANES), BF16),
                        pltpu.VMEM((ATTN_BLOCK, 2 * LANES), F32),
                        pltpu.VMEM((npat, ATTN_CHUNK, LANES), F32), pltpu.VMEM((ATTN_CHUNK, LANES), F32),
                        pltpu.VMEM((NBLK, 2 * ATTN_BLOCK, LANES), F32), pltpu.VMEM((NBLK, 2 * ATTN_BLOCK, LANES), F32)]
        + (ex.scratch() if hosted else []),
        compiler_params=_cp(("arbitrary", "arbitrary")),
    )(qn, kn, kn, v, v, o, lse, do, *(ex.srcs if hosted else []))


def _discretize(lr, li, dt):
    mag = jnp.exp(lr * dt)
    abr = mag * jnp.cos(li * dt)
    abi = mag * jnp.sin(li * dt)
    den = lr * lr + li * li
    nr, ni = abr - 1.0, abi
    cr = (nr * lr + ni * li) / den
    ci = (ni * lr - nr * li) / den
    return abr, abi, den, nr, ni, cr, ci


def _ssm_discretize(a_re, a_im, log_dt, b_re_t, b_im_t):
    def body(ar_ref, ai_ref, ldt_ref, br_ref, bi_ref, abr_ref, abi_ref, bbr_ref, bbi_ref):
        abr, abi, _, _, _, cr, ci = _discretize(ar_ref[...], ai_ref[...], jnp.exp(ldt_ref[...]))
        br, bi = br_ref[...], bi_ref[...]
        abr_ref[...] = abr
        abi_ref[...] = abi
        bbr_ref[...] = cr * br - ci * bi
        bbi_ref[...] = cr * bi + ci * br

    return pl.pallas_call(
        body, name="ssm_discretize",
        out_shape=[_sds(a_re.shape)] * 2 + [_sds(b_re_t.shape)] * 2,
    )(a_re, a_im, log_dt, b_re_t, b_im_t)


def _ssm_discretize_bwd(a_re, a_im, log_dt, b_re_t, b_im_t, dabr, dabi, dbbr, dbbi):
    def body(ar_ref, ai_ref, ldt_ref, br_ref, bi_ref, dabr_ref, dabi_ref, dbbr_ref, dbbi_ref,
             dar_ref, dai_ref, dldt_ref, dbr_ref, dbi_ref):
        lr, li = ar_ref[...], ai_ref[...]
        dt = jnp.exp(ldt_ref[...])
        abr, abi, den, nr, ni, cr, ci = _discretize(lr, li, dt)
        br, bi = br_ref[...], bi_ref[...]
        gbr, gbi = dbbr_ref[...], dbbi_ref[...]
        dcr = jnp.sum(gbr * br + gbi * bi, axis=1, keepdims=True)
        dci = jnp.sum(gbi * br - gbr * bi, axis=1, keepdims=True)
        dbr_ref[...] = cr * gbr + ci * gbi
        dbi_ref[...] = cr * gbi - ci * gbr
        dnr = (dcr * lr - dci * li) / den
        dni = (dcr * li + dci * lr) / den
        dden = -(dcr * cr + dci * ci) / den
        dlr = (dcr * nr + dci * ni) / den + dden * 2.0 * lr
        dli = (dcr * ni - dci * nr) / den + dden * 2.0 * li
        gabr = dabr_ref[...] + dnr
        gabi = dabi_ref[...] + dni
        dphi = gabr * abr + gabi * abi
        dth = gabi * abr - gabr * abi
        dar_ref[...] = dlr + dphi * dt
        dai_ref[...] = dli + dth * dt
        dldt_ref[...] = jnp.sum(dphi * lr + dth * li, axis=2, keepdims=True) * dt

    return pl.pallas_call(
        body, name="ssm_discretize_bwd",
        out_shape=[_sds(a_re.shape)] * 2 + [_sds(log_dt.shape)] + [_sds(b_re_t.shape)] * 2,
    )(a_re, a_im, log_dt, b_re_t, b_im_t, dabr, dabi, dbbr, dbbi)


def _ssm_power(abr, abi, n_sq):
    def body(r_ref, i_ref, or_ref, oi_ref):
        r, i = r_ref[...], i_ref[...]
        for _ in range(n_sq):
            r, i = r * r - i * i, 2.0 * r * i
        or_ref[...] = r
        oi_ref[...] = i

    return pl.pallas_call(body, name="ssm_power", out_shape=[_sds(abr.shape)] * 2)(abr, abi)


N_CB = SSM_W // LANES
CB_STATES = N_STATE // N_CB
ROWS = N_SEG * SSM_LK


class _Neg:
    def __init__(self, ref):
        self.ref = ref

    def __getitem__(self, idx):
        return -self.ref[idx]


def _seg_init(fin_r, fin_i, pw_r, pw_i, x_r, x_i, reverse):
    zero = jnp.zeros((1, N_STATE), F32)
    cr, ci = zero, zero
    order = range(N_SEG - 1, -1, -1) if reverse else range(N_SEG)
    pr = pw_r[...]
    pi = -pw_i[...] if reverse else pw_i[...]
    for j in order:
        x_r[pl.ds(j, 1), :] = cr
        x_i[pl.ds(j, 1), :] = ci
        fr, fi = fin_r[pl.ds(j, 1), :], fin_i[pl.ds(j, 1), :]
        cr, ci = fr + pr * cr - pi * ci, fi + pr * ci + pi * cr


def _permute_in(src_ref, dst):
    for c in range(N_CB):
        dst[c] = src_ref[:, :, pl.ds(c * LANES, LANES)].reshape(ROWS, LANES)


def _permute_out(src, dst_ref):
    for c in range(N_CB):
        dst_ref[:, :, pl.ds(c * LANES, LANES)] = src[c].reshape(SSM_LK, N_SEG, LANES)


def _scan_block(a_r, a_i, c, b_r, b_i, b_off, x_r, x_i, reverse, acc=None, keep=True):
    cols = pl.ds(c * CB_STATES, CB_STATES)
    ar = jnp.broadcast_to(a_r[:, cols], (N_SEG, CB_STATES))
    ai = jnp.broadcast_to(a_i[:, cols], (N_SEG, CB_STATES))
    xr, xi = x_r[:, cols], x_i[:, cols]
    if acc is not None:
        sr = jnp.zeros((N_SEG, CB_STATES), F32)
        si = jnp.zeros((N_SEG, CB_STATES), F32)
    for t in range(SSM_LK):
        k = (SSM_LK - 1 - t) if reverse else t
        rows = pl.ds(k * N_SEG + b_off, N_SEG)
        xr, xi = ar * xr - ai * xi + b_r[rows, :], ar * xi + ai * xr + b_i[rows, :]
        if keep:
            b_r[rows, :] = xr
            b_i[rows, :] = xi
        if acc is not None:
            pr, pi = acc[0][pl.ds(k * N_SEG, N_SEG), :], acc[1][pl.ds(k * N_SEG, N_SEG), :]
            sr = sr + xr * pr + xi * pi
            si = si + xi * pr - xr * pi
    x_r[:, cols] = xr
    x_i[:, cols] = xi
    if acc is not None:
        acc[2][:, cols] += sr
        acc[3][:, cols] += si


def _ssm_fwd(u3, abr, abi, pw_r, pw_i, fin_r, fin_i, bb_r, bb_i, cc_r, cc_i, dskip, finals_only):
    sl = u3.shape[0]
    nch = sl // SSM_LK

    def body(u_ref, abr_ref, abi_ref, pwr_ref, pwi_ref, finr_ref, fini_ref, bbr_ref, bbi_ref,
             ccr_ref, cci_ref, d_ref, *rest):
        if finals_only:
            xfr_ref, xfi_ref, up, x_r, x_i = rest[:5]
        else:
            y_ref, xsr_ref, xsi_ref, up, yp, x_r, x_i = rest[:7]
        xs_r, xs_i = rest[-2 * N_CB:-N_CB], rest[-N_CB:]
        k = pl.program_id(0)

        @pl.when(k == 0)
        def _():
            _seg_init(finr_ref, fini_ref, pwr_ref, pwi_ref, x_r, x_i, False)

        if not finals_only:
            xsr_ref[0] = x_r[...]
            xsi_ref[0] = x_i[...]
        _permute_in(u_ref, up)

        def drive(c):
            lhs = up[c].astype(BF16)
            xs_r[c][...] = _dot(lhs, bbr_ref[c])
            xs_i[c][...] = _dot(lhs, bbi_ref[c])

        def readout(c):
            yp[c] = (_dot(xs_r[c][...].astype(BF16), ccr_ref[c]) - _dot(xs_i[c][...].astype(BF16), cci_ref[c])
                     + d_ref[:, pl.ds(c * LANES, LANES)] * up[c])

        drive(0)
        for c in range(N_CB):
            if c + 1 < N_CB:
                drive(c + 1)
            if c >= 1 and not finals_only:
                readout(c - 1)
            _scan_block(abr_ref, abi_ref, c, xs_r[c], xs_i[c], 0, x_r, x_i, False, keep=not finals_only)
        if finals_only:
            @pl.when(k == nch - 1)
            def _():
                xfr_ref[...] = x_r[...]
                xfi_ref[...] = x_i[...]
        else:
            readout(N_CB - 1)
            _permute_out(yp, y_ref)

    ublk = pl.BlockSpec((SSM_LK, N_SEG, SSM_W), lambda k: (k, 0, 0))
    st = pl.BlockSpec((1, N_SEG, N_STATE), lambda k: (k, 0, 0))
    vec = _full((1, N_STATE))
    mat = _full((N_SEG, N_STATE))
    chunk = pltpu.VMEM((N_CB, ROWS, LANES), F32)
    blocks = [pltpu.VMEM((ROWS, CB_STATES), F32)] * (2 * N_CB)
    small = pltpu.VMEM((N_SEG, N_STATE), F32)
    if finals_only:
        out_specs, out_shape = [mat, mat], [_sds((N_SEG, N_STATE))] * 2
        scratch, name = [chunk, small, small] + blocks, "ssm_fwd_finals"
    else:
        out_specs = [ublk, st, st]
        out_shape = [_sds(u3.shape)] + [_sds((nch, N_SEG, N_STATE))] * 2
        scratch, name = [chunk, chunk, small, small] + blocks, "ssm_fwd"
    return pl.pallas_call(
        body, name=name, grid=(nch,),
        in_specs=[ublk, vec, vec, vec, vec, mat, mat,
                  _full((N_CB, LANES, CB_STATES)), _full((N_CB, LANES, CB_STATES)),
                  _full((N_CB, CB_STATES, LANES)), _full((N_CB, CB_STATES, LANES)), _full((1, SSM_W))],
        out_specs=out_specs, out_shape=out_shape, scratch_shapes=scratch,
        compiler_params=_cp(("arbitrary",)),
    )(u3, abr, abi, pw_r, pw_i, fin_r, fin_i, bb_r, bb_i, cc_r, cc_i, dskip)


def _ssm_bwd(u3, dy3, xst_r, xst_i, abr, abi, pw_r, pw_i, fin_r, fin_i, bb_r, bb_i, bbt_r, bbt_i,
             cct_r, cct_i, dskip, finals_only):
    sl = u3.shape[0]
    nch = sl // SSM_LK

    def body(u_ref, g_ref, xsr_ref, xsi_ref, abr_ref, abi_ref, pwr_ref, pwi_ref,
             finr_ref, fini_ref, bbr_ref, bbi_ref, btr_ref, bti_ref, ctr_ref, cti_ref, d_ref, *rest):
        if finals_only:
            lfr_ref, lfi_ref, gp, lam_r, lam_i = rest[:5]
            l_r, l_i = rest[-2 * N_CB:-N_CB], rest[-N_CB:]
        else:
            (du_ref, dar_ref, dai_ref, dbr_ref, dbi_ref, dcr_ref, dci_ref, dd_ref,
             gp, up, yp, lam_r, lam_i, x_r, x_i, sar, sai, sdd) = rest[:18]
            l_r, l_i = rest[18:18 + N_CB], rest[18 + N_CB:18 + 2 * N_CB]
            xx_r, xx_i = rest[18 + 2 * N_CB:18 + 3 * N_CB], rest[18 + 3 * N_CB:]
        t = pl.program_id(0)

        @pl.when(t == 0)
        def _():
            _seg_init(finr_ref, fini_ref, pwr_ref, pwi_ref, lam_r, lam_i, True)
            if not finals_only:
                sar[...] = jnp.zeros_like(sar)
                sai[...] = jnp.zeros_like(sai)
                sdd[...] = jnp.zeros_like(sdd)
                dbr_ref[...] = jnp.zeros_like(dbr_ref)
                dbi_ref[...] = jnp.zeros_like(dbi_ref)
                dcr_ref[...] = jnp.zeros_like(dcr_ref)
                dci_ref[...] = jnp.zeros_like(dci_ref)

        _permute_in(g_ref, gp)
        if not finals_only:
            _permute_in(u_ref, up)
            x_r[...] = xsr_ref[0]
            x_i[...] = xsi_ref[0]

        def drive(c):
            lhs = gp[c].astype(BF16)
            l_r[c][...] = _dot(lhs, ctr_ref[c])
            l_i[c][...] = -_dot(lhs, cti_ref[c])
            if not finals_only:
                cols = pl.ds(c * CB_STATES, CB_STATES)
                xx_r[c][pl.ds(0, N_SEG), :] = x_r[:, cols]
                xx_i[c][pl.ds(0, N_SEG), :] = x_i[:, cols]
                ub = up[c].astype(BF16)
                xx_r[c][pl.ds(N_SEG, ROWS), :] = _dot(ub, bbr_ref[c])
                xx_i[c][pl.ds(N_SEG, ROWS), :] = _dot(ub, bbi_ref[c])

        def collect(c):
            lrb = l_r[c][...].astype(BF16)
            lib = l_i[c][...].astype(BF16)
            ub = up[c].astype(BF16)
            gb = gp[c].astype(BF16)
            dbr_ref[c] += _dot_tn(lrb, ub)
            dbi_ref[c] += _dot_tn(lib, ub)
            dcr_ref[c] += _dot_tn(gb, xx_r[c][pl.ds(N_SEG, ROWS), :].astype(BF16))
            dci_ref[c] += -_dot_tn(gb, xx_i[c][pl.ds(N_SEG, ROWS), :].astype(BF16))
            yp[c] = _dot(lrb, btr_ref[c]) + _dot(lib, bti_ref[c]) + d_ref[:, pl.ds(c * LANES, LANES)] * gp[c]
            prod = gp[c] * up[c]
            sdd[:, pl.ds(c * LANES, LANES)] += jnp.sum(prod.reshape(SSM_LK, N_SEG, LANES), axis=0)

        drive(0)
        for c in range(N_CB):
            if c + 1 < N_CB:
                drive(c + 1)
            if finals_only:
                _scan_block(abr_ref, _Neg(abi_ref), c, l_r[c], l_i[c], 0, lam_r, lam_i, True, keep=False)
            else:
                if c >= 1:
                    collect(c - 1)
                _scan_block(abr_ref, abi_ref, c, xx_r[c], xx_i[c], N_SEG, x_r, x_i, False)
                _scan_block(abr_ref, _Neg(abi_ref), c, l_r[c], l_i[c], 0, lam_r, lam_i, True,
                            acc=(xx_r[c], xx_i[c], sar, sai))
        if finals_only:
            @pl.when(t == nch - 1)
            def _():
                lfr_ref[...] = lam_r[...]
                lfi_ref[...] = lam_i[...]
        else:
            collect(N_CB - 1)
            _permute_out(yp, du_ref)

            @pl.when(t == nch - 1)
            def _():
                dar_ref[...] = jnp.sum(sar[...], axis=0, keepdims=True)
                dai_ref[...] = jnp.sum(sai[...], axis=0, keepdims=True)
                dd_ref[...] = jnp.sum(sdd[...], axis=0, keepdims=True)

    ublk = pl.BlockSpec((SSM_LK, N_SEG, SSM_W), lambda t: (nch - 1 - t, 0, 0))
    st = pl.BlockSpec((1, N_SEG, N_STATE), lambda t: (nch - 1 - t, 0, 0))
    vec = _full((1, N_STATE))
    mat = _full((N_SEG, N_STATE))
    cs = _full((N_CB, LANES, CB_STATES))
    sc = _full((N_CB, CB_STATES, LANES))
    in_specs = [ublk, ublk, st, st, vec, vec, vec, vec, mat, mat, cs, cs, sc, sc, cs, cs, _full((1, SSM_W))]
    chunk = pltpu.VMEM((N_CB, ROWS, LANES), F32)
    blocks = [pltpu.VMEM((ROWS, CB_STATES), F32)] * (2 * N_CB)
    small = pltpu.VMEM((N_SEG, N_STATE), F32)
    if finals_only:
        out_specs, out_shape = [mat, mat], [_sds((N_SEG, N_STATE))] * 2
        scratch, name = [chunk, small, small] + blocks, "ssm_bwd_finals"
    else:
        out_specs = [ublk, vec, vec, sc, sc, cs, cs, _full((1, SSM_W))]
        out_shape = ([_sds(u3.shape), _sds((1, N_STATE)), _sds((1, N_STATE))]
                     + [_sds((N_CB, CB_STATES, LANES))] * 2 + [_sds((N_CB, LANES, CB_STATES))] * 2
                     + [_sds((1, SSM_W))])
        scratch = ([chunk, chunk, chunk, small, small, small, small, small, small, pltpu.VMEM((N_SEG, SSM_W), F32)]
                   + blocks + [pltpu.VMEM((ROWS + N_SEG, CB_STATES), F32)] * (2 * N_CB))
        name = "ssm_bwd"
    return pl.pallas_call(
        body, name=name, grid=(nch,), in_specs=in_specs, out_specs=out_specs, out_shape=out_shape,
        scratch_shapes=scratch, compiler_params=_cp(("arbitrary",)),
    )(u3, dy3, xst_r, xst_i, abr, abi, pw_r, pw_i, fin_r, fin_i, bb_r, bb_i, bbt_r, bbt_i, cct_r, cct_i, dskip)


def _row(tm, w):
    return pl.BlockSpec((tm, w), lambda i: (i, 0))


def _acc_rows(ref, rows, first):
    @pl.when(first)
    def _():
        ref[...] = jnp.zeros_like(ref)

    ref[...] += jnp.sum(rows, axis=0, keepdims=True)


def _fwd_mix(attn, y, x, glu_w, glu_b, ga, gs, w_out, g2, tm=1024):
    s = x.shape[0]

    def body(a_ref, y_ref, x_ref, gw_ref, gb_ref, ga_ref, gs_ref, wo_ref, g2_ref, mix_ref, x2_ref, h_ref):
        a = a_ref[...]
        anb = ((a * _rms(a)) * ga_ref[...]).astype(BF16)
        z, _ = _gelu(y_ref[...])
        so = z * jax.nn.sigmoid(_dot(z.astype(BF16), gw_ref[...]) + gb_ref[...])
        snb = ((so * _rms(so)) * gs_ref[...]).astype(BF16)
        mix_ref[:, pl.ds(0, ATTN_W)] = anb
        mix_ref[:, pl.ds(ATTN_W, SSM_W)] = snb
        x2 = x_ref[...] + (_dot(anb, wo_ref[pl.ds(0, ATTN_W), :]) + _dot(snb, wo_ref[pl.ds(ATTN_W, SSM_W), :]))
        x2_ref[...] = x2
        h_ref[...] = ((x2 * _rms(x2)) * g2_ref[...]).astype(BF16)

    return pl.pallas_call(
        body, name="fwd_mix", grid=(s // tm,),
        in_specs=[_row(tm, ATTN_W), _row(tm, SSM_W), _row(tm, D_MODEL), _full((SSM_W, SSM_W)), _full((1, SSM_W)),
                  _full((1, ATTN_W)), _full((1, SSM_W)), _full((D_MODEL, D_MODEL)), _full((1, D_MODEL))],
        out_specs=[_row(tm, D_MODEL), _row(tm, D_MODEL), _row(tm, D_MODEL)],
        out_shape=[_sds((s, D_MODEL), BF16), _sds((s, D_MODEL)), _sds((s, D_MODEL), BF16)],
        compiler_params=_cp(("parallel",)),
    )(attn, y, x, glu_w, glu_b, ga, gs, w_out, g2)


def _mlp_up(h, w_up, tm=512):
    s = h.shape[0]
    bn = D_FF // N_DEV

    def body(h_ref, w_ref, r_ref, hdn_ref):
        hv = h_ref[...]
        for j in range(N_DEV):
            cols = pl.ds(j * bn, bn)
            r = jnp.maximum(_dot(hv, w_ref[j]), 0.0)
            r_ref[:, cols] = r.astype(BF16)
            hdn_ref[:, cols] = (r * r).astype(BF16)

    return pl.pallas_call(
        body, name="mlp_up", grid=(s // tm,),
        in_specs=[_row(tm, D_MODEL), _full((N_DEV, D_MODEL, bn))],
        out_specs=[_row(tm, D_FF), _row(tm, D_FF)], out_shape=[_sds((s, D_FF), BF16)] * 2,
        compiler_params=_cp(("parallel",)),
    )(h, w_up)


def _mlp_down_loss(hdn, w_down, x2, tgt, tm=512):
    s = x2.shape[0]

    def body(hdn_ref, w_ref, x2_ref, t_ref, dy_ref, dyb_ref, sse_ref):
        err = (x2_ref[...] + _dot(hdn_ref[...], w_ref[...])) - t_ref[...]
        dy = err * (1.0 / D_MODEL)
        dy_ref[...] = dy
        dyb_ref[...] = dy.astype(BF16)

        @pl.when(pl.program_id(0) == 0)
        def _():
            sse_ref[...] = jnp.zeros_like(sse_ref)

        sse_ref[...] += jnp.sum(jnp.sum(err * err, axis=0, keepdims=True), axis=1, keepdims=True)

    return pl.pallas_call(
        body, name="mlp_down_loss", grid=(s // tm,),
        in_specs=[_row(tm, D_FF), _full((D_FF, D_MODEL)), _row(tm, D_MODEL), _row(tm, D_MODEL)],
        out_specs=[_row(tm, D_MODEL), _row(tm, D_MODEL), _full((1, 1))],
        out_shape=[_sds((s, D_MODEL)), _sds((s, D_MODEL), BF16), _sds((1, 1))],
        compiler_params=_cp(("arbitrary",)),
    )(hdn, w_down, x2, tgt)


def _mlp_down_bwd(dyb, w_down_t, r, tm=512):
    s = dyb.shape[0]
    bn = D_FF // N_DEV

    def body(dy_ref, w_ref, r_ref, dup_ref):
        dyv = dy_ref[...]
        for j in range(N_DEV):
            cols = pl.ds(j * bn, bn)
            dup_ref[:, cols] = (_dot(dyv, w_ref[j]) * (2.0 * r_ref[:, cols].astype(F32))).astype(BF16)

    return pl.pallas_call(
        body, name="mlp_down_bwd", grid=(s // tm,),
        in_specs=[_row(tm, D_MODEL), _full((N_DEV, D_MODEL, bn)), _row(tm, D_FF)],
        out_specs=_row(tm, D_FF), out_shape=_sds((s, D_FF), BF16),
        compiler_params=_cp(("parallel",)),
    )(dyb, w_down_t, r)


def _mlp_up_bwd(dup, w_up_t, x2, g2, dy, tm=512):
    s = x2.shape[0]

    def body(dup_ref, w_ref, x2_ref, g2_ref, dy_ref, dx2_ref, dx2b_ref, dg_ref):
        dx, dg_rows = _rms_bwd(_dot(dup_ref[...], w_ref[...]), x2_ref[...], g2_ref[...], D_MODEL)
        dx2 = dy_ref[...] + dx
        dx2_ref[...] = dx2
        dx2b_ref[...] = dx2.astype(BF16)
        _acc_rows(dg_ref, dg_rows, pl.program_id(0) == 0)

    return pl.pallas_call(
        body, name="mlp_up_bwd", grid=(s // tm,),
        in_specs=[_row(tm, D_FF), _full((D_FF, D_MODEL)), _row(tm, D_MODEL), _full((1, D_MODEL)), _row(tm, D_MODEL)],
        out_specs=[_row(tm, D_MODEL), _row(tm, D_MODEL), _full((1, D_MODEL))],
        out_shape=[_sds((s, D_MODEL)), _sds((s, D_MODEL), BF16), _sds((1, D_MODEL))],
        compiler_params=_cp(("arbitrary",)),
    )(dup, w_up_t, x2, g2, dy)


def _mix_bwd(dx2b, w_out_t, attn, y, glu_w, glu_b, glu_w_t, ga, gs, tm=1024):
    s = attn.shape[0]

    def body(dx2_ref, wot_ref, a_ref, y_ref, gw_ref, gb_ref, gwt_ref, ga_ref, gs_ref,
             da_ref, dys_ref, z_ref, dpre_ref, dga_ref, dgs_ref, dgb_ref):
        first = pl.program_id(0) == 0
        dmix = _dot(dx2_ref[...], wot_ref[...])
        da, dga_rows = _rms_bwd(dmix[:, :ATTN_W], a_ref[...], ga_ref[...], ATTN_W)
        da_ref[...] = da
        yv = y_ref[...]
        z, t = _gelu(yv)
        gate = jax.nn.sigmoid(_dot(z.astype(BF16), gw_ref[...]) + gb_ref[...])
        dso, dgs_rows = _rms_bwd(dmix[:, ATTN_W:], z * gate, gs_ref[...], SSM_W)
        dpre = dso * z * gate * (1.0 - gate)
        dpre_b = dpre.astype(BF16)
        dz = dso * gate + _dot(dpre_b, gwt_ref[...])
        dgelu = 0.5 * (1.0 + t) + 0.5 * yv * (1.0 - t * t) * (GELU_C * (1.0 + 3.0 * 0.044715 * (yv * yv)))
        dys_ref[...] = dz * dgelu
        z_ref[...] = z.astype(BF16)
        dpre_ref[...] = dpre_b
        _acc_rows(dga_ref, dga_rows, first)
        _acc_rows(dgs_ref, dgs_rows, first)
        _acc_rows(dgb_ref, dpre, first)

    vec = _full((1, SSM_W))
    return pl.pallas_call(
        body, name="mix_bwd", grid=(s // tm,),
        in_specs=[_row(tm, D_MODEL), _full((D_MODEL, D_MODEL)), _row(tm, ATTN_W), _row(tm, SSM_W),
                  _full((SSM_W, SSM_W)), vec, _full((SSM_W, SSM_W)), vec, vec],
        out_specs=[_row(tm, ATTN_W), _row(tm, SSM_W), _row(tm, SSM_W), _row(tm, SSM_W), vec, vec, vec],
        out_shape=[_sds((s, ATTN_W)), _sds((s, SSM_W)), _sds((s, SSM_W), BF16), _sds((s, SSM_W), BF16),
                   _sds((1, ATTN_W)), _sds((1, SSM_W)), _sds((1, SSM_W))],
        compiler_params=_cp(("arbitrary",)),
    )(dx2b, w_out_t, attn, y, glu_w, glu_b, glu_w_t, ga, gs)


def _qk_bwd(dqn, dkn, qk, dv, du, gq, gk, gmat, tm=1024):
    s = qk.shape[0]

    def body(dq_ref, dk_ref, qk_ref, dv_ref, du_ref, gq_ref, gk_ref, gm_ref, dp_ref, dgq_ref, dgk_ref):
        first = pl.program_id(0) == 0
        gm = gm_ref[...]
        for idx, (d_ref, g_ref, dg_ref) in enumerate(((dq_ref, gq_ref, dgq_ref), (dk_ref, gk_ref, dgk_ref))):
            xv = qk_ref[:, pl.ds(idx * ATTN_W, ATTN_W)]
            dyv = d_ref[...]
            r = lax.rsqrt(_group_mean(xv * xv, gm) + EPS)
            gdy = dyv * g_ref[...]
            dx = r * gdy - xv * (r * r * r) * _group_mean(gdy * xv, gm)
            dp_ref[:, pl.ds(idx * ATTN_W, ATTN_W)] = dx.astype(BF16)
            _acc_rows(dg_ref, dyv * (xv * r), first)
        dp_ref[:, pl.ds(2 * ATTN_W, ATTN_W)] = dv_ref[...].astype(BF16)
        dp_ref[:, pl.ds(3 * ATTN_W, SSM_W)] = du_ref[...].astype(BF16)

    vec = _full((1, ATTN_W))
    return pl.pallas_call(
        body, name="qk_bwd", grid=(s // tm,),
        in_specs=[_row(tm, ATTN_W), _row(tm, ATTN_W), _row(tm, 2 * ATTN_W), _row(tm, ATTN_W), _row(tm, SSM_W),
                  vec, vec, _full((ATTN_W, ATTN_W))],
        out_specs=[_row(tm, 4 * ATTN_W), vec, vec],
        out_shape=[_sds((s, 4 * ATTN_W), BF16), _sds((1, ATTN_W)), _sds((1, ATTN_W))],
        compiler_params=_cp(("arbitrary",)),
    )(dqn, dkn, qk, dv, du, gq, gk, gmat)


def _in_bwd(dproj, w_in_t, x, g1, dx2, ex=None, tm=1024):
    s = x.shape[0]
    steps = s // tm

    def body(*refs):
        (dp_ref, w_ref, x_ref, g1_ref, dx2_ref), (gx_ref, dg_ref), _, hx = _carry(ex, refs, 5, 2)
        if ex is not None:
            @pl.when(pl.program_id(0) == 0)
            def _():
                ex.start(*hx)

        dx, dg_rows = _rms_bwd(_dot(dp_ref[...], w_ref[...]), x_ref[...], g1_ref[...], D_MODEL)
        gx_ref[...] = dx2_ref[...] + dx
        _acc_rows(dg_ref, dg_rows, pl.program_id(0) == 0)
        if ex is not None:
            @pl.when(pl.program_id(0) == steps - 1)
            def _():
                ex.wait(*hx)

    hosted = ex is not None
    return pl.pallas_call(
        body, name="in_bwd", grid=(steps,),
        in_specs=[_row(tm, 4 * ATTN_W), _full((4 * ATTN_W, D_MODEL)), _row(tm, D_MODEL), _full((1, D_MODEL)),
                  _row(tm, D_MODEL)] + (ex.specs() if hosted else []),
        out_specs=[_row(tm, D_MODEL), _full((1, D_MODEL))] + (ex.specs() if hosted else []),
        out_shape=[_sds((s, D_MODEL)), _sds((1, D_MODEL))] + (ex.out_shape() if hosted else []),
        scratch_shapes=ex.scratch() if hosted else [],
        compiler_params=_cp(("arbitrary",)),
    )(dproj, w_in_t, x, g1, dx2, *(ex.srcs if hosted else []))


def _mm_tn(a, b, name, ts=2048):
    s, k = a.shape
    n = b.shape[1]
    bk, bn = min(k, 1024), min(n, 1024)
    steps = s // ts

    def body(a_ref, b_ref, o_ref, acc):
        t = pl.program_id(2)

        @pl.when(t == 0)
        def _():
            acc[...] = jnp.zeros_like(acc)

        acc[...] += _dot_tn(a_ref[...], b_ref[...])

        @pl.when(t == steps - 1)
        def _():
            o_ref[...] = acc[...].astype(BF16)

    return pl.pallas_call(
        body, name=name, grid=(k // bk, n // bn, steps),
        in_specs=[pl.BlockSpec((ts, bk), lambda i, j, t: (t, i)), pl.BlockSpec((ts, bn), lambda i, j, t: (t, j))],
        out_specs=pl.BlockSpec((bk, bn), lambda i, j, t: (i, j)), out_shape=_sds((k, n), BF16),
        scratch_shapes=[pltpu.VMEM((bk, bn), F32)],
        compiler_params=_cp(("parallel", "parallel", "arbitrary")),
    )(a, b)


def _peer(k):
    x, y, c = lax.axis_index("x"), lax.axis_index("y"), lax.axis_index("c")
    px = 1 - x if k & 4 else x
    py = 1 - y if k & 2 else y
    pc = 1 - c if k & 1 else c
    return (px, py, pc), 4 * px + 2 * py + pc


def _gather_rows(x_shard):
    m_per, n = x_shard.shape

    def body(x_ref, out_ref, send_sems, recv_sems, local_sem):
        x, y, c = lax.axis_index("x"), lax.axis_index("y"), lax.axis_index("c")
        me, sibling = (x, y, c), (x, y, 1 - c)
        chips = [(1 - x, y), (x, 1 - y), (1 - x, 1 - y)]

        def rows(px, py, pc):
            return out_ref.at[pl.ds((4 * px + 2 * py + pc) * m_per, m_per), :]

        def copy(k, block, to, src=None):
            return pltpu.make_async_remote_copy(
                src_ref=rows(*block) if src is None else src, dst_ref=rows(*block),
                send_sem=send_sems.at[k], recv_sem=recv_sems.at[k], device_id=to, device_id_type=MESH)

        mine = pltpu.make_async_copy(x_ref, rows(*me), local_sem)
        mine.start()
        first = [copy(0, me, sibling, src=x_ref)]
        first += [copy(1 + j, me, (*chip, c), src=x_ref) for j, chip in enumerate(chips)]
        for cp in first:
            cp.start()
        passed = [copy(4 + j, (*chip, c), sibling) for j, chip in enumerate(chips)]
        for j, chip in enumerate(chips):
            copy(1 + j, (*chip, c), me).wait_recv()
            passed[j].start()
        copy(0, sibling, me).wait_recv()
        for j, chip in enumerate(chips):
            copy(4 + j, (*chip, 1 - c), me).wait_recv()
        for cp in first + passed:
            cp.wait_send()
        mine.wait()

    return pl.pallas_call(
        body, name="gather_weights", out_shape=_sds((N_DEV * m_per, n), x_shard.dtype),
        in_specs=[pl.BlockSpec(memory_space=pltpu.VMEM)], out_specs=pl.BlockSpec(memory_space=pltpu.VMEM),
        scratch_shapes=[pltpu.SemaphoreType.DMA((7,)), pltpu.SemaphoreType.DMA((7,)), pltpu.SemaphoreType.DMA],
        compiler_params=pltpu.CompilerParams(vmem_limit_bytes=VMEM_LIMIT),
    )(x_shard)


class _Exchange:
    def __init__(self, srcs, whole):
        self.srcs, self.whole, self.n = list(srcs), list(whole), len(srcs)
        self.rows = [a.shape[0] // N_DEV if w is False else a.shape[0] for a, w in zip(self.srcs, self.whole)]
        self.cols = [a.shape[1] // N_DEV if w == "col" else a.shape[1] for a, w in zip(self.srcs, self.whole)]

    def specs(self):
        return [pl.BlockSpec(memory_space=pl.ANY)] * self.n

    def out_shape(self):
        return [_sds((N_DEV, r, c), a.dtype) for r, c, a in zip(self.rows, self.cols, self.srcs)]

    def scratch(self):
        return [pltpu.SemaphoreType.DMA((self.n * 7,)), pltpu.SemaphoreType.DMA((self.n * 7,)),
                pltpu.SemaphoreType.DMA((self.n,))]

    def _copies(self, ins, outs, sems):
        send_sems, recv_sems, local_sems = sems
        _, me = _peer(0)
        for w in range(self.n):
            for k in range(N_DEV):
                peer, pidx = _peer(k)
                if self.whole[w] is True:
                    src = ins[w]
                elif self.whole[w] == "col":
                    src = ins[w].at[:, pl.ds(pl.multiple_of(pidx * self.cols[w], self.cols[w]), self.cols[w])]
                else:
                    src = ins[w].at[pl.ds(pidx * self.rows[w], self.rows[w]), :]
                if k == 0:
                    yield k, pltpu.make_async_copy(src, outs[w].at[me], local_sems.at[w]), None
                else:
                    sem = w * 7 + k - 1
                    out = pltpu.make_async_remote_copy(src_ref=src, dst_ref=outs[w].at[me], send_sem=send_sems.at[sem],
                                                       recv_sem=recv_sems.at[sem], device_id=peer, device_id_type=MESH)
                    back = pltpu.make_async_remote_copy(src_ref=src, dst_ref=outs[w].at[pidx], send_sem=send_sems.at[sem],
                                                        recv_sem=recv_sems.at[sem], device_id=peer, device_id_type=MESH)
                    yield k, out, back

    def start(self, ins, outs, sems):
        for _, out, _ in self._copies(ins, outs, sems):
            out.start()

    def wait(self, ins, outs, sems):
        for k, out, back in self._copies(ins, outs, sems):
            if k == 0:
                out.wait()
            else:
                back.wait_recv()
                out.wait_send()


def _carry(ex, refs, n_in, n_out):
    nh = ex.n if ex is not None else 0
    ins, hin = refs[:n_in], refs[n_in:n_in + nh]
    outs = refs[n_in + nh:n_in + nh + n_out]
    hout = refs[n_in + nh + n_out:n_in + 2 * nh + n_out]
    rest = refs[n_in + 2 * nh + n_out:]
    if ex is None:
        return ins, outs, rest, None
    return ins, outs, rest[:-3], (hin, hout, rest[-3:])


def _exchange_now(srcs, whole, name):
    ex = _Exchange(srcs, whole)

    def body(*refs):
        _, _, _, (hin, hout, sems) = _carry(ex, refs, 0, 0)
        ex.start(hin, hout, sems)
        ex.wait(hin, hout, sems)

    return pl.pallas_call(body, name=name, out_shape=ex.out_shape(), in_specs=ex.specs(), out_specs=ex.specs(),
                          scratch_shapes=ex.scratch())(*srcs)


def _adamw(w, m, v, gparts, name):
    r, c = w.shape
    tr = r if r * c <= 256 * 1024 else 128 * 1024 // c

    def body(w_ref, m_ref, v_ref, g_ref, go_ref, d_ref, mo_ref, vo_ref):
        g = g_ref[0].astype(F32)
        for i in range(1, N_DEV):
            g = g + g_ref[i].astype(F32)
        go_ref[...] = g
        d_ref[...], mo_ref[...], vo_ref[...] = _adamw_step(w_ref[...], m_ref[...], v_ref[...], g)

    blk = pl.BlockSpec((tr, c), lambda i: (i, 0))
    return pl.pallas_call(
        body, name=name, grid=(r // tr,),
        in_specs=[blk, blk, blk, pl.BlockSpec((N_DEV, tr, c), lambda i: (0, i, 0))],
        out_specs=[blk] * 4, out_shape=[_sds((r, c))] * 4,
        compiler_params=_cp(("parallel",)),
    )(w, m, v, gparts)


def _adamw_step(w, m, v, g):
    nm = ADAM_B1 * m + (1.0 - ADAM_B1) * g
    nv = ADAM_B2 * v + (1.0 - ADAM_B2) * (g * g)
    m_hat = nm / (1.0 - ADAM_B1 ** ADAM_STEP)
    v_hat = nv / (1.0 - ADAM_B2 ** ADAM_STEP)
    return -ADAM_LR * (m_hat / (jnp.sqrt(v_hat) + ADAM_EPS) + ADAM_WD * w), nm, nv


def _sum_slots(parts):
    def body(p_ref, o_ref):
        g = p_ref[0]
        for i in range(1, N_DEV):
            g = g + p_ref[i]
        o_ref[...] = g

    return pl.pallas_call(body, name="sum_small_grads", out_shape=_sds(parts.shape[1:]))(parts)


def _adamw_native(ws, ms, vs, gs):
    n = len(ws)

    def body(*refs):
        for i in range(n):
            w_ref, m_ref, v_ref, g_ref = refs[i], refs[n + i], refs[2 * n + i], refs[3 * n + i]
            d, nm, nv = _adamw_step(w_ref[...], m_ref[...], v_ref[...], g_ref[...])
            refs[4 * n + i][...] = d
            refs[5 * n + i][...] = nm
            refs[6 * n + i][...] = nv

    outs = pl.pallas_call(body, name="adamw_small", out_shape=[_sds(w.shape) for w in ws] * 3,
                          compiler_params=pltpu.CompilerParams(vmem_limit_bytes=VMEM_LIMIT))(*ws, *ms, *vs, *gs)
    return outs[:n], outs[n:2 * n], outs[2 * n:]


def _block_diag(a, states_first):
    a4 = a.reshape(N_CB, 8, SSM_GROUP, SSM_STATE)
    eye = jnp.eye(8, dtype=a.dtype)
    if states_first:
        return jnp.einsum("bgcp,gh->bgphc", a4, eye).reshape(N_CB, CB_STATES, LANES)
    return jnp.einsum("bgcp,gh->bgchp", a4, eye).reshape(N_CB, LANES, CB_STATES)


def _block_diag_of(full, states_first):
    if states_first:
        picked = jnp.einsum("bgphc,gh->bgcp", full.reshape(N_CB, 8, SSM_STATE, 8, SSM_GROUP), jnp.eye(8, dtype=full.dtype))
    else:
        picked = jnp.einsum("bgchp,gh->bgcp", full.reshape(N_CB, 8, SSM_GROUP, 8, SSM_STATE), jnp.eye(8, dtype=full.dtype))
    return picked.reshape(SSM_GROUPS, SSM_GROUP, SSM_STATE)


SMALL_EARLY = ("ssm_a_re", "ssm_a_im", "ssm_log_dt", "ssm_b_re", "ssm_b_im", "ssm_c_re", "ssm_c_im", "ssm_d", "glu_b",
               "attn_out_norm_g", "ssm_out_norm_g", "norm2_g")
SMALL_MID = ("q_norm_g", "k_norm_g")
SMALL_LATE = ("norm1_g",)
SMALL = SMALL_EARLY + SMALL_MID + SMALL_LATE


def _pack_small(arrs):
    parts = []
    for a in arrs:
        flat = a.reshape(-1)
        rows = -(-flat.shape[0] // (8 * LANES)) * 8
        parts.append(jnp.pad(flat, (0, rows * LANES - flat.shape[0])).reshape(rows, LANES))
    return jnp.concatenate(parts, axis=0)


def _unpack_small(packed, shapes):
    out, r0 = [], 0
    for shp in shapes:
        size = math.prod(shp)
        rows = -(-size // (8 * LANES)) * 8
        out.append(packed[r0:r0 + rows].reshape(-1)[:size].reshape(shp))
        r0 += rows
    return out


def kernel(x, norm1_g, w_in, q_norm_g, k_norm_g, ssm_a_re, ssm_a_im, ssm_log_dt, ssm_b_re, ssm_b_im, ssm_c_re, ssm_c_im, ssm_d, glu_w, glu_b, attn_out_norm_g, ssm_out_norm_g, w_out, norm2_g, w_mlp_up, w_mlp_down, loss_target, m_norm1_g, m_w_in, m_q_norm_g, m_k_norm_g, m_ssm_a_re, m_ssm_a_im, m_ssm_log_dt, m_ssm_b_re, m_ssm_b_im, m_ssm_c_re, m_ssm_c_im, m_ssm_d, m_glu_w, m_glu_b, m_attn_out_norm_g, m_ssm_out_norm_g, m_w_out, m_norm2_g, m_w_mlp_up, m_w_mlp_down, v_norm1_g, v_w_in, v_q_norm_g, v_k_norm_g, v_ssm_a_re, v_ssm_a_im, v_ssm_log_dt, v_ssm_b_re, v_ssm_b_im, v_ssm_c_re, v_ssm_c_im, v_ssm_d, v_glu_w, v_glu_b, v_attn_out_norm_g, v_ssm_out_norm_g, v_w_out, v_norm2_g, v_w_mlp_up, v_w_mlp_down):
    weights = dict(norm1_g=norm1_g, w_in=w_in, q_norm_g=q_norm_g, k_norm_g=k_norm_g, ssm_a_re=ssm_a_re,
                   ssm_a_im=ssm_a_im, ssm_log_dt=ssm_log_dt, ssm_b_re=ssm_b_re, ssm_b_im=ssm_b_im,
                   ssm_c_re=ssm_c_re, ssm_c_im=ssm_c_im, ssm_d=ssm_d, glu_w=glu_w, glu_b=glu_b,
                   attn_out_norm_g=attn_out_norm_g, ssm_out_norm_g=ssm_out_norm_g, w_out=w_out, norm2_g=norm2_g,
                   w_mlp_up=w_mlp_up, w_mlp_down=w_mlp_down)
    mom_m = dict(norm1_g=m_norm1_g, w_in=m_w_in, q_norm_g=m_q_norm_g, k_norm_g=m_k_norm_g, ssm_a_re=m_ssm_a_re,
                 ssm_a_im=m_ssm_a_im, ssm_log_dt=m_ssm_log_dt, ssm_b_re=m_ssm_b_re, ssm_b_im=m_ssm_b_im,
                 ssm_c_re=m_ssm_c_re, ssm_c_im=m_ssm_c_im, ssm_d=m_ssm_d, glu_w=m_glu_w, glu_b=m_glu_b,
                 attn_out_norm_g=m_attn_out_norm_g, ssm_out_norm_g=m_ssm_out_norm_g, w_out=m_w_out,
                 norm2_g=m_norm2_g, w_mlp_up=m_w_mlp_up, w_mlp_down=m_w_mlp_down)
    mom_v = dict(norm1_g=v_norm1_g, w_in=v_w_in, q_norm_g=v_q_norm_g, k_norm_g=v_k_norm_g, ssm_a_re=v_ssm_a_re,
                 ssm_a_im=v_ssm_a_im, ssm_log_dt=v_ssm_log_dt, ssm_b_re=v_ssm_b_re, ssm_b_im=v_ssm_b_im,
                 ssm_c_re=v_ssm_c_re, ssm_c_im=v_ssm_c_im, ssm_d=v_ssm_d, glu_w=v_glu_w, glu_b=v_glu_b,
                 attn_out_norm_g=v_attn_out_norm_g, ssm_out_norm_g=v_ssm_out_norm_g, w_out=v_w_out,
                 norm2_g=v_norm2_g, w_mlp_up=v_w_mlp_up, w_mlp_down=v_w_mlp_down)
    order = list(weights)

    xs, tgt = x[0], loss_target[0]
    s = xs.shape[0]
    assert s % ATTN_CHUNK == 0 and (s // N_SEG) % SSM_LK == 0
    seg_len = s // N_SEG
    n_sq = seg_len.bit_length() - 1
    assert 1 << n_sq == seg_len

    w_in_t = _gather_rows(w_in[0].T.astype(BF16))
    w_in_full = w_in_t.T
    later = _Exchange([glu_w[0].astype(BF16), w_out[0].astype(BF16), w_mlp_up[0].T.astype(BF16),
                       w_mlp_down[0].astype(BF16), w_mlp_up[0].astype(BF16), w_mlp_down[0].T.astype(BF16)], [True] * 6)

    gq = jnp.tile(q_norm_g[0], ATTN_W // HEAD_DIM)[None]
    gk = jnp.tile(k_norm_g[0], ATTN_W // HEAD_DIM)[None]
    lane = jnp.arange(ATTN_W) // HEAD_DIM
    gmat = jnp.where(lane[:, None] == lane[None, :], 1.0 / HEAD_DIM, 0.0).astype(BF16)
    a_re3 = ssm_a_re[0][:, None, :]
    a_im3 = ssm_a_im[0][:, None, :]
    ldt3 = ssm_log_dt[0][:, None, None]
    b_re_t = jnp.swapaxes(ssm_b_re[0], 1, 2)
    b_im_t = jnp.swapaxes(ssm_b_im[0], 1, 2)
    c_re, c_im = ssm_c_re[0], ssm_c_im[0]
    dskip = ssm_d[0].reshape(1, SSM_W)

    qk, qn, kn, v, u, xn = _fwd_proj(xs, norm1_g, w_in_full, gq, gk, gmat)
    attn, lse, glu_g, w_out_g, w_up_g, w_down_g, w_up_blocks, w_down_t_blocks = _attn_fwd(qn, kn, v, later)
    glu_full = glu_g.reshape(SSM_W, SSM_W)
    w_out_full = w_out_g.reshape(D_MODEL, D_MODEL)
    w_up_t = w_up_g.reshape(D_FF, D_MODEL)
    w_down_full = w_down_g.reshape(D_FF, D_MODEL)
    w_out_t, glu_t = w_out_full.T, glu_full.T

    abr3, abi3, bbr, bbi = _ssm_discretize(a_re3, a_im3, ldt3, b_re_t, b_im_t)
    abr, abi = abr3.reshape(1, N_STATE), abi3.reshape(1, N_STATE)
    pw_r, pw_i = _ssm_power(abr, abi, n_sq)
    bb_r, bb_i = _block_diag(bbr, False).astype(BF16), _block_diag(bbi, False).astype(BF16)
    bbt_r, bbt_i = _block_diag(bbr, True).astype(BF16), _block_diag(bbi, True).astype(BF16)
    cc_r, cc_i = _block_diag(c_re, True).astype(BF16), _block_diag(c_im, True).astype(BF16)
    cct_r, cct_i = _block_diag(c_re, False).astype(BF16), _block_diag(c_im, False).astype(BF16)
    seg_major = lambda a: jnp.swapaxes(a.reshape(N_SEG, seg_len, SSM_W), 0, 1)
    seg_minor = lambda a: jnp.swapaxes(a, 0, 1).reshape(s, SSM_W)
    u3 = seg_major(u)
    zero_fin = jnp.zeros((N_SEG, N_STATE), F32)
    ssm_args = (abr, abi, pw_r, pw_i)
    xf_r, xf_i = _ssm_fwd(u3, *ssm_args, zero_fin, zero_fin, bb_r, bb_i, cc_r, cc_i, dskip, True)
    y3, xst_r, xst_i = _ssm_fwd(u3, *ssm_args, xf_r, xf_i, bb_r, bb_i, cc_r, cc_i, dskip, False)
    y = seg_minor(y3)

    mix, x2, h = _fwd_mix(attn, y, xs, glu_full, glu_b, attn_out_norm_g, ssm_out_norm_g, w_out_full, norm2_g)
    r_act, hdn = _mlp_up(h, w_up_blocks)
    dy, dyb, sse = _mlp_down_loss(hdn, w_down_full, x2, tgt)

    dup = _mlp_down_bwd(dyb, w_down_t_blocks, r_act)
    g_w_down = _mm_tn(hdn, dyb, "grad_w_down")
    dx2, dx2b, g_norm2 = _mlp_up_bwd(dup, w_up_t, x2, norm2_g, dy)
    g_w_up = _mm_tn(h, dup, "grad_w_up")
    dattn, dys, z_b, dpre_b, g_ga, g_gs, g_glu_b = _mix_bwd(dx2b, w_out_t, attn, y, glu_full, glu_b, glu_t,
                                                             attn_out_norm_g, ssm_out_norm_g)
    g_w_out = _mm_tn(mix, dx2b, "grad_w_out")
    g_glu_w = _mm_tn(z_b, dpre_b, "grad_glu_w")
    dy3 = seg_major(dys)
    bwd_args = (u3, dy3, xst_r, xst_i, abr, abi, pw_r, pw_i)
    lf_r, lf_i = _ssm_bwd(*bwd_args, zero_fin, zero_fin, bb_r, bb_i, bbt_r, bbt_i, cct_r, cct_i, dskip, True)
    du3, dab_r, dab_i, dbb_r, dbb_i, dcc_r, dcc_i, g_d = _ssm_bwd(*bwd_args, lf_r, lf_i, bb_r, bb_i, bbt_r, bbt_i,
                                                                 cct_r, cct_i, dskip, False)
    g_a_re3, g_a_im3, g_ldt3, g_b_re_t, g_b_im_t = _ssm_discretize_bwd(
        a_re3, a_im3, ldt3, b_re_t, b_im_t, dab_r.reshape(a_re3.shape), dab_i.reshape(a_re3.shape),
        _block_diag_of(dbb_r, True), _block_diag_of(dbb_i, True))
    g_c_re, g_c_im = _block_diag_of(dcc_r, False), _block_diag_of(dcc_i, False)
    small_grads = dict(
        ssm_a_re=g_a_re3.reshape(ssm_a_re.shape), ssm_a_im=g_a_im3.reshape(ssm_a_im.shape),
        ssm_log_dt=g_ldt3.reshape(ssm_log_dt.shape), ssm_b_re=jnp.swapaxes(g_b_re_t, 1, 2)[None],
        ssm_b_im=jnp.swapaxes(g_b_im_t, 1, 2)[None], ssm_c_re=g_c_re[None], ssm_c_im=g_c_im[None],
        ssm_d=g_d.reshape(ssm_d.shape), glu_b=g_glu_b, attn_out_norm_g=g_ga, ssm_out_norm_g=g_gs, norm2_g=g_norm2)

    early = _Exchange([g_glu_w, g_w_out, g_w_up, g_w_down, _pack_small([small_grads[n] for n in SMALL_EARLY] + [sse])],
                      [False, False, "col", False, True])
    dqn, dkn, dv, p_glu, p_w_out, p_w_up, p_w_down, p_early = _attn_bwd(qn, kn, v, attn, lse, dattn, early)

    dproj, g_gq, g_gk = _qk_bwd(dqn, dkn, qk, dv, seg_minor(du3), gq, gk, gmat)
    g_w_in = _mm_tn(xn, dproj, "grad_w_in")
    small_grads["q_norm_g"] = g_gq.reshape(ATTN_W // HEAD_DIM, HEAD_DIM).sum(0)[None]
    small_grads["k_norm_g"] = g_gk.reshape(ATTN_W // HEAD_DIM, HEAD_DIM).sum(0)[None]
    mid = _Exchange([g_w_in, _pack_small([small_grads[n] for n in SMALL_MID])], ["col", True])
    grad_x, g_norm1, p_w_in, p_mid = _in_bwd(dproj, w_in_t, xs, norm1_g, dx2, mid)
    (p_late,) = _exchange_now([_pack_small([g_norm1])], [True], "exchange_norm1")

    res = {}
    for name, gp in (("w_in", p_w_in), ("glu_w", p_glu), ("w_out", p_w_out), ("w_mlp_up", p_w_up), ("w_mlp_down", p_w_down)):
        outs = _adamw(weights[name][0], mom_m[name][0], mom_v[name][0], gp, "adamw_" + name)
        res[name] = [o[None] for o in outs]
    shapes = [weights[n].shape for n in SMALL_EARLY] + [sse.shape] + [weights[n].shape for n in SMALL_MID + SMALL_LATE]
    g_small = _unpack_small(_sum_slots(jnp.concatenate([p_early, p_mid, p_late], axis=1)), shapes)
    loss = 0.5 * g_small.pop(len(SMALL_EARLY))[0, 0] / D_MODEL
    d_small, m_small, v_small = _adamw_native([weights[n] for n in SMALL], [mom_m[n] for n in SMALL],
                                              [mom_v[n] for n in SMALL], g_small)
    for i, n in enumerate(SMALL):
        res[n] = [g_small[i], d_small[i], m_small[i], v_small[i]]

    return (loss, grad_x[None], *[res[n][0] for n in order], *[res[n][1] for n in order],
            *[res[n][2] for n in order], *[res[n][3] for n in order])
```

```python
import math

import jax
import jax.numpy as jnp
from jax import lax
from jax.experimental import pallas as pl
from jax.experimental.pallas import tpu as pltpu

F32 = jnp.float32
BF16 = jnp.bfloat16

D_MODEL = 1024
ATTN_W = 512
HEAD_DIM = 64
SSM_W = 512
SSM_GROUP = 16
SSM_GROUPS = 32
SSM_STATE = 64
N_STATE = SSM_GROUPS * SSM_STATE
D_FF = 4096
EPS = 1e-6
NEG_INF = -1e30
ATTN_CHUNK = 2048
ATTN_BLOCK = 128
DILATIONS = (1, 4, 16)
N_SEG = 8
SSM_LK = 64
N_DEV = 8
LANES = 128

ADAM_LR = 0.001
ADAM_B1 = 0.9
ADAM_B2 = 0.999
ADAM_EPS = 1e-08
ADAM_WD = 0.01
ADAM_STEP = 10

VMEM_LIMIT = 56 * 1024 * 1024
GELU_C = math.sqrt(2.0 / math.pi)
MESH = pl.DeviceIdType.MESH


def _cp(sem, vmem=VMEM_LIMIT):
    return pltpu.CompilerParams(dimension_semantics=sem, vmem_limit_bytes=vmem)


def _dot(a, b):
    return jnp.dot(a, b, preferred_element_type=F32)


def _dot_tn(a, b):
    return lax.dot_general(a, b, (((0,), (0,)), ((), ())), preferred_element_type=F32)


def _group_mean(x2, gmat):
    hi = x2.astype(BF16)
    lo = (x2 - hi.astype(F32)).astype(BF16)
    return _dot(hi, gmat) + _dot(lo, gmat)


def _rms(x):
    return lax.rsqrt(jnp.mean(x * x, axis=-1, keepdims=True) + EPS)


def _rms_bwd(dy, x, g, n):
    r = _rms(x)
    gdy = dy * g
    dx = r * gdy - x * (r * r * r) * (jnp.sum(gdy * x, axis=-1, keepdims=True) / n)
    return dx, dy * (x * r)


def _gelu(y):
    t = jnp.tanh(GELU_C * (y + 0.044715 * (y * y * y)))
    return 0.5 * y * (1.0 + t), t


def _full(shape):
    nd = len(shape)
    return pl.BlockSpec(shape, lambda *_: (0,) * nd)


def _sds(shape, dtype=F32):
    return jax.ShapeDtypeStruct(shape, dtype)


def _fwd_proj(x, g1, w_in, gq, gk, gmat, tm=512):
    s = x.shape[0]

    def body(x_ref, g1_ref, w_ref, gq_ref, gk_ref, gm_ref, qk_ref, qn_ref, kn_ref, v_ref, u_ref, xn_ref):
        xv = x_ref[...]
        xnb = ((xv * _rms(xv)) * g1_ref[...]).astype(BF16)
        xn_ref[...] = xnb
        proj = _dot(xnb, w_ref[...])
        q = proj[:, :ATTN_W]
        k = proj[:, ATTN_W:2 * ATTN_W]
        qk_ref[...] = proj[:, :2 * ATTN_W]
        v_ref[...] = proj[:, 2 * ATTN_W:3 * ATTN_W]
        u_ref[...] = proj[:, 3 * ATTN_W:]
        gm = gm_ref[...]
        qn_ref[...] = (q * lax.rsqrt(_group_mean(q * q, gm) + EPS)) * gq_ref[...]
        kn_ref[...] = (k * lax.rsqrt(_group_mean(k * k, gm) + EPS)) * gk_ref[...]

    row = lambda w: pl.BlockSpec((tm, w), lambda i: (i, 0))
    return pl.pallas_call(
        body, name="fwd_proj", grid=(s // tm,),
        in_specs=[row(D_MODEL), _full((1, D_MODEL)), _full((D_MODEL, 4 * ATTN_W)), _full((1, ATTN_W)),
                  _full((1, ATTN_W)), _full((ATTN_W, ATTN_W))],
        out_specs=[row(2 * ATTN_W), row(ATTN_W), row(ATTN_W), row(ATTN_W), row(ATTN_W), row(D_MODEL)],
        out_shape=[_sds((s, 2 * ATTN_W)), _sds((s, ATTN_W)), _sds((s, ATTN_W)), _sds((s, ATTN_W)),
                   _sds((s, ATTN_W)), _sds((s, D_MODEL), BF16)],
        compiler_params=_cp(("parallel",)),
    )(x, g1, w_in, gq, gk, gmat)


def _attn_rows(t, d, nb):
    if d == 1:
        q0 = t * ATTN_BLOCK
        return (t, pl.ds(q0, ATTN_BLOCK), pl.ds(ATTN_CHUNK + q0, ATTN_BLOCK),
                pl.ds(ATTN_CHUNK - ATTN_BLOCK + q0, ATTN_BLOCK))
    r = t // nb
    b = t % nb
    return (b, pl.ds(ATTN_BLOCK * b * d + r, ATTN_BLOCK, stride=d),
            pl.ds(ATTN_CHUNK + ATTN_BLOCK * b * d + r, ATTN_BLOCK, stride=d),
            pl.ds(ATTN_CHUNK + ATTN_BLOCK * (b - 1) * d + r, ATTN_BLOCK, stride=d))


def _attn_masks():
    row = lax.broadcasted_iota(jnp.int32, (ATTN_BLOCK, LANES), 0)
    col = lax.broadcasted_iota(jnp.int32, (ATTN_BLOCK, LANES), 1)
    return row, col


NBLK = ATTN_CHUNK // ATTN_BLOCK


def _attn_bias(bias_s):
    row, col = _attn_masks()
    bias_s[:, pl.ds(0, LANES)] = jnp.where(col >= row, 0.0, NEG_INF)
    bias_s[:, pl.ds(LANES, LANES)] = jnp.where(col <= row, 0.0, NEG_INF)
    return col < HEAD_DIM


def _attn_fwd(qn, kn, v, ex=None, group=4):
    s = qn.shape[0]
    nch = s // ATTN_CHUNK
    scale = HEAD_DIM ** -0.5
    npat = len(DILATIONS)
    n_hp = ATTN_W // LANES

    def body(*refs):
        ((q_ref, kp_ref, kc_ref, vp_ref, vc_ref), (o_ref, lse_ref),
         (kk, vv, kt_s, vb_s, bias_s, m_s, a_s), hx) = _carry(ex, refs, 5, 2)
        i = pl.program_id(1)
        if ex is not None:
            @pl.when(jnp.logical_and(pl.program_id(0) == 0, i == 0))
            def _():
                ex.start(*hx)

        kk[pl.ds(0, ATTN_CHUNK), :] = kp_ref[...]
        kk[pl.ds(ATTN_CHUNK, ATTN_CHUNK), :] = kc_ref[...]
        vv[pl.ds(0, ATTN_CHUNK), :] = vp_ref[...]
        vv[pl.ds(ATTN_CHUNK, ATTN_CHUNK), :] = vc_ref[...]
        head0 = _attn_bias(bias_s)
        first_pen = jnp.where(i > 0, 0.0, NEG_INF)

        for p, d in enumerate(DILATIONS):
            nb = ATTN_CHUNK // (ATTN_BLOCK * d)

            def prep(t, d=d, nb=nb):
                _, _, crows, prows = _attn_rows(t, d, nb)
                kt_s[t, :, pl.ds(0, LANES)] = kk[prows, :].T.astype(BF16)
                kt_s[t, :, pl.ds(LANES, LANES)] = kk[crows, :].T.astype(BF16)
                vp = vv[prows, :]
                vc = vv[crows, :]
                vb_s[2 * t, pl.ds(0, ATTN_BLOCK), :] = jnp.where(head0, vp, 1.0).astype(BF16)
                vb_s[2 * t, pl.ds(ATTN_BLOCK, ATTN_BLOCK), :] = jnp.where(head0, vc, 1.0).astype(BF16)
                vb_s[2 * t + 1, pl.ds(0, ATTN_BLOCK), :] = jnp.where(head0, 1.0, vp).astype(BF16)
                vb_s[2 * t + 1, pl.ds(ATTN_BLOCK, ATTN_BLOCK), :] = jnp.where(head0, 1.0, vc).astype(BF16)

            def main(tg, p=p, d=d, nb=nb):
                st = []
                for g in range(group):
                    t = tg * group + g
                    b, qrows, _, _ = _attn_rows(t, d, nb)
                    q = q_ref[qrows, :] * scale
                    for h in range(2):
                        hm = head0 if h == 0 else jnp.logical_not(head0)
                        st.append(dict(t=t, b=b, qrows=qrows, sc=_dot(jnp.where(hm, q, 0.0).astype(BF16), kt_s[t])))
                for e in st:
                    sc = e["sc"] + bias_s[...]
                    s_p = sc[:, :LANES] + first_pen if e["b"] == 0 else sc[:, :LANES]
                    s_c = sc[:, LANES:]
                    m = jnp.max(jnp.maximum(s_p, s_c), axis=-1, keepdims=True)
                    e["eb"] = jnp.concatenate([jnp.exp(s_p - m), jnp.exp(s_c - m)], axis=1).astype(BF16)
                    e["m"] = jnp.broadcast_to(m, (ATTN_BLOCK, LANES))
                for g in range(group):
                    e0, e1 = st[2 * g], st[2 * g + 1]
                    t = e0["t"]
                    m_s[p, e0["qrows"], :] = jnp.where(head0, e0["m"], e1["m"])
                    a_s[2 * p, e0["qrows"], :] = _dot(e0["eb"], vb_s[2 * t])
                    a_s[2 * p + 1, e0["qrows"], :] = _dot(e1["eb"], vb_s[2 * t + 1])

            for g in range(group):
                prep(g)
            for tg in range(NBLK // group):
                if tg + 1 < NBLK // group:
                    for g in range(group):
                        prep((tg + 1) * group + g)
                main(tg)

        def merge(t, carry):
            rows = pl.ds(pl.multiple_of(t * ATTN_BLOCK, ATTN_BLOCK), ATTN_BLOCK)
            m_all = m_s[0, rows, :]
            for p in range(1, npat):
                m_all = jnp.maximum(m_all, m_s[p, rows, :])
            num = jnp.zeros((ATTN_BLOCK, LANES), F32)
            den = jnp.zeros((ATTN_BLOCK, LANES), F32)
            for p in range(npat):
                w = jnp.exp(m_s[p, rows, :] - m_all)
                a0, a1 = a_s[2 * p, rows, :], a_s[2 * p + 1, rows, :]
                num = num + jnp.where(head0, a0, a1) * w
                den = den + pltpu.roll(jnp.where(head0, a1, a0), HEAD_DIM, 1) * w
            o_ref[rows, :] = num / den
            lse_ref[rows, :] = m_all + jnp.log(den)
            return carry

        lax.fori_loop(0, NBLK, merge, 0, unroll=2)
        if ex is not None:
            @pl.when(jnp.logical_and(pl.program_id(0) == n_hp - 1, i == nch - 1))
            def _():
                ex.wait(*hx)

    cur = pl.BlockSpec((ATTN_CHUNK, LANES), lambda h, i: (i, h))
    prev = pl.BlockSpec((ATTN_CHUNK, LANES), lambda h, i: (jnp.maximum(i - 1, 0), h))
    hosted = ex is not None
    return pl.pallas_call(
        body, name="attn_fwd", grid=(n_hp, nch),
        in_specs=[cur, prev, cur, prev, cur] + (ex.specs() if hosted else []),
        out_specs=[cur, cur] + (ex.specs() if hosted else []),
        out_shape=[_sds((s, ATTN_W)), _sds((s, ATTN_W))] + (ex.out_shape() if hosted else []),
        scratch_shapes=[pltpu.VMEM((2 * ATTN_CHUNK, LANES), F32), pltpu.VMEM((2 * ATTN_CHUNK, LANES), F32),
                        pltpu.VMEM((NBLK, LANES, 2 * LANES), BF16), pltpu.VMEM((2 * NBLK, 2 * ATTN_BLOCK, LANES), BF16),
                        pltpu.VMEM((ATTN_BLOCK, 2 * LANES), F32),
                        pltpu.VMEM((npat, ATTN_CHUNK, LANES), F32), pltpu.VMEM((2 * npat, ATTN_CHUNK, LANES), F32)]
        + (ex.scratch() if hosted else []),
        compiler_params=_cp(("arbitrary", "arbitrary")),
    )(qn, kn, kn, v, v, *(ex.srcs if hosted else []))


def _attn_bwd(qn, kn, v, o, lse, do, ex=None, group=4):
    s = qn.shape[0]
    nch = s // ATTN_CHUNK
    scale = HEAD_DIM ** -0.5
    npat = len(DILATIONS)
    n_hp = ATTN_W // LANES

    def body(*refs):
        ((q_ref, kp_ref, kc_ref, vp_ref, vc_ref, o_ref, lse_ref, do_ref), (dq_ref, dk_ref, dv_ref),
         (kk, vv, dkk, dvv, kt_s, vt_s, kn_s, bias_s, dq_s, dl_s, dkb, dvb), hx) = _carry(ex, refs, 8, 3)
        step = pl.program_id(1)
        i = nch - 1 - step
        if ex is not None:
            @pl.when(jnp.logical_and(pl.program_id(0) == 0, step == 0))
            def _():
                ex.start(*hx)

        kk[pl.ds(0, ATTN_CHUNK), :] = kp_ref[...]
        kk[pl.ds(ATTN_CHUNK, ATTN_CHUNK), :] = kc_ref[...]
        vv[pl.ds(0, ATTN_CHUNK), :] = vp_ref[...]
        vv[pl.ds(ATTN_CHUNK, ATTN_CHUNK), :] = vc_ref[...]

        @pl.when(step == 0)
        def _():
            dkk[pl.ds(ATTN_CHUNK, ATTN_CHUNK), :] = jnp.zeros((ATTN_CHUNK, LANES), F32)
            dvv[pl.ds(ATTN_CHUNK, ATTN_CHUNK), :] = jnp.zeros((ATTN_CHUNK, LANES), F32)

        @pl.when(step > 0)
        def _():
            dkk[pl.ds(ATTN_CHUNK, ATTN_CHUNK), :] = dkk[pl.ds(0, ATTN_CHUNK), :]
            dvv[pl.ds(ATTN_CHUNK, ATTN_CHUNK), :] = dvv[pl.ds(0, ATTN_CHUNK), :]

        dkk[pl.ds(0, ATTN_CHUNK), :] = jnp.zeros((ATTN_CHUNK, LANES), F32)
        dvv[pl.ds(0, ATTN_CHUNK), :] = jnp.zeros((ATTN_CHUNK, LANES), F32)
        head0 = _attn_bias(bias_s)

        def delta(t, carry):
            rows = pl.ds(pl.multiple_of(t * ATTN_BLOCK, ATTN_BLOCK), ATTN_BLOCK)
            prod = do_ref[rows, :] * o_ref[rows, :]
            d0 = jnp.sum(jnp.where(head0, prod, 0.0), axis=-1, keepdims=True)
            d1 = jnp.sum(jnp.where(head0, 0.0, prod), axis=-1, keepdims=True)
            dl_s[rows, :] = jnp.where(head0, d0, d1)
            return carry

        lax.fori_loop(0, NBLK, delta, 0, unroll=2)

        first_pen = jnp.where(i > 0, 0.0, NEG_INF)

        for p, d in enumerate(DILATIONS):
            nb = ATTN_CHUNK // (ATTN_BLOCK * d)

            def prep(t, d=d, nb=nb):
                _, _, crows, prows = _attn_rows(t, d, nb)
                kp, kc = kk[prows, :], kk[crows, :]
                kt_s[t, :, pl.ds(0, LANES)] = kp.T.astype(BF16)
                kt_s[t, :, pl.ds(LANES, LANES)] = kc.T.astype(BF16)
                kn_s[t, pl.ds(0, ATTN_BLOCK), :] = (kp * scale).astype(BF16)
                kn_s[t, pl.ds(ATTN_BLOCK, ATTN_BLOCK), :] = (kc * scale).astype(BF16)
                vt_s[t, :, pl.ds(0, LANES)] = vv[prows, :].T.astype(BF16)
                vt_s[t, :, pl.ds(LANES, LANES)] = vv[crows, :].T.astype(BF16)

            def main(tg, p=p, d=d, nb=nb):
                st = []
                for g in range(group):
                    t = tg * group + g
                    b, qrows, _, _ = _attn_rows(t, d, nb)
                    q = q_ref[qrows, :] * scale
                    dout = do_ref[qrows, :]
                    lse_b = lse_ref[qrows, :]
                    dl_b = dl_s[qrows, :]
                    for h in range(2):
                        hm = head0 if h == 0 else jnp.logical_not(head0)
                        c0 = h * HEAD_DIM
                        qh = jnp.where(hm, q, 0.0).astype(BF16)
                        doh = jnp.where(hm, dout, 0.0).astype(BF16)
                        st.append(dict(t=t, b=b, qrows=qrows, qh=qh, doh=doh, lse=lse_b[:, c0:c0 + 1],
                                       dl=dl_b[:, c0:c0 + 1], sc=_dot(qh, kt_s[t]), dp=_dot(doh, vt_s[t])))
                for e in st:
                    sc = e["sc"] + bias_s[...]
                    if e["b"] == 0:
                        sc = jnp.concatenate([sc[:, :LANES] + first_pen, sc[:, LANES:]], axis=1)
                    pr = jnp.exp(sc - e["lse"])
                    e["ds"] = (pr * (e["dp"] - e["dl"])).astype(BF16)
                    e["pr"] = pr.astype(BF16)
                for g in range(group):
                    e0, e1 = st[2 * g], st[2 * g + 1]
                    t = e0["t"]
                    dq_s[p, e0["qrows"], :] = jnp.where(head0, _dot(e0["ds"], kn_s[t]), _dot(e1["ds"], kn_s[t]))
                    dkb[t] = _dot_tn(e0["ds"], e0["qh"]) + _dot_tn(e1["ds"], e1["qh"])
                    dvb[t] = _dot_tn(e0["pr"], e0["doh"]) + _dot_tn(e1["pr"], e1["doh"])

            def scatter(t, d=d, nb=nb):
                _, _, crows, prows = _attn_rows(t, d, nb)
                dkk[prows, :] = dkk[prows, :] + dkb[t, pl.ds(0, ATTN_BLOCK), :]
                dkk[crows, :] = dkk[crows, :] + dkb[t, pl.ds(ATTN_BLOCK, ATTN_BLOCK), :]
                dvv[prows, :] = dvv[prows, :] + dvb[t, pl.ds(0, ATTN_BLOCK), :]
                dvv[crows, :] = dvv[crows, :] + dvb[t, pl.ds(ATTN_BLOCK, ATTN_BLOCK), :]

            n_groups = NBLK // group
            for g in range(group):
                prep(g)
            for tg in range(n_groups):
                if tg + 1 < n_groups:
                    for g in range(group):
                        prep((tg + 1) * group + g)
                main(tg)
                if tg >= 1:
                    for g in range(group):
                        scatter((tg - 1) * group + g)
            for g in range(group):
                scatter((n_groups - 1) * group + g)

        def finish(t, carry):
            rows = pl.ds(pl.multiple_of(t * ATTN_BLOCK, ATTN_BLOCK), ATTN_BLOCK)
            acc = dq_s[0, rows, :]
            for p in range(1, npat):
                acc = acc + dq_s[p, rows, :]
            dq_ref[rows, :] = acc
            return carry

        lax.fori_loop(0, NBLK, finish, 0, unroll=2)
        dk_ref[...] = dkk[pl.ds(ATTN_CHUNK, ATTN_CHUNK), :]
        dv_ref[...] = dvv[pl.ds(ATTN_CHUNK, ATTN_CHUNK), :]
        if ex is not None:
            @pl.when(jnp.logical_and(pl.program_id(0) == n_hp - 1, step == nch - 1))
            def _():
                ex.wait(*hx)

    cur = pl.BlockSpec((ATTN_CHUNK, LANES), lambda h, t: (nch - 1 - t, h))
    prev = pl.BlockSpec((ATTN_CHUNK, LANES), lambda h, t: (jnp.maximum(nch - 2 - t, 0), h))
    big = pltpu.VMEM((2 * ATTN_CHUNK, LANES), F32)
    pair_t = pltpu.VMEM((NBLK, LANES, 2 * LANES), BF16)
    hosted = ex is not None
    return pl.pallas_call(
        body, name="attn_bwd", grid=(n_hp, nch),
        in_specs=[cur, prev, cur, prev, cur, cur, cur, cur] + (ex.specs() if hosted else []),
        out_specs=[cur, cur, cur] + (ex.specs() if hosted else []),
        out_shape=[_sds((s, ATTN_W))] * 3 + (ex.out_shape() if hosted else []),
        scratch_shapes=[big, big, big, big, pair_t, pair_t, pltpu.VMEM((NBLK, 2 * ATTN_BLOCK, LANES), BF16),
                        pltpu.VMEM((ATTN_BLOCK, 2 * LANES), F32),
                        pltpu.VMEM((npat, ATTN_CHUNK, LANES), F32), pltpu.VMEM((ATTN_CHUNK, LANES), F32),
                        pltpu.VMEM((NBLK, 2 * ATTN_BLOCK, LANES), F32), pltpu.VMEM((NBLK, 2 * ATTN_BLOCK, LANES), F32)]
        + (ex.scratch() if hosted else []),
        compiler_params=_cp(("arbitrary", "arbitrary")),
    )(qn, kn, kn, v, v, o, lse, do, *(ex.srcs if hosted else []))


def _discretize(lr, li, dt):
    mag = jnp.exp(lr * dt)
    abr = mag * jnp.cos(li * dt)
    abi = mag * jnp.sin(li * dt)
    den = lr * lr + li * li
    nr, ni = abr - 1.0, abi
    cr = (nr * lr + ni * li) / den
    ci = (ni * lr - nr * li) / den
    return abr, abi, den, nr, ni, cr, ci


def _ssm_discretize(a_re, a_im, log_dt, b_re_t, b_im_t):
    def body(ar_ref, ai_ref, ldt_ref, br_ref, bi_ref, abr_ref, abi_ref, bbr_ref, bbi_ref):
        abr, abi, _, _, _, cr, ci = _discretize(ar_ref[...], ai_ref[...], jnp.exp(ldt_ref[...]))
        br, bi = br_ref[...], bi_ref[...]
        abr_ref[...] = abr
        abi_ref[...] = abi
        bbr_ref[...] = cr * br - ci * bi
        bbi_ref[...] = cr * bi + ci * br

    return pl.pallas_call(
        body, name="ssm_discretize",
        out_shape=[_sds(a_re.shape)] * 2 + [_sds(b_re_t.shape)] * 2,
    )(a_re, a_im, log_dt, b_re_t, b_im_t)


def _ssm_discretize_bwd(a_re, a_im, log_dt, b_re_t, b_im_t, dabr, dabi, dbbr, dbbi):
    def body(ar_ref, ai_ref, ldt_ref, br_ref, bi_ref, dabr_ref, dabi_ref, dbbr_ref, dbbi_ref,
             dar_ref, dai_ref, dldt_ref, dbr_ref, dbi_ref):
        lr, li = ar_ref[...], ai_ref[...]
        dt = jnp.exp(ldt_ref[...])
        abr, abi, den, nr, ni, cr, ci = _discretize(lr, li, dt)
        br, bi = br_ref[...], bi_ref[...]
        gbr, gbi = dbbr_ref[...], dbbi_ref[...]
        dcr = jnp.sum(gbr * br + gbi * bi, axis=1, keepdims=True)
        dci = jnp.sum(gbi * br - gbr * bi, axis=1, keepdims=True)
        dbr_ref[...] = cr * gbr + ci * gbi
        dbi_ref[...] = cr * gbi - ci * gbr
        dnr = (dcr * lr - dci * li) / den
        dni = (dcr * li + dci * lr) / den
        dden = -(dcr * cr + dci * ci) / den
        dlr = (dcr * nr + dci * ni) / den + dden * 2.0 * lr
        dli = (dcr * ni - dci * nr) / den + dden * 2.0 * li
        gabr = dabr_ref[...] + dnr
        gabi = dabi_ref[...] + dni
        dphi = gabr * abr + gabi * abi
        dth = gabi * abr - gabr * abi
        dar_ref[...] = dlr + dphi * dt
        dai_ref[...] = dli + dth * dt
        dldt_ref[...] = jnp.sum(dphi * lr + dth * li, axis=2, keepdims=True) * dt

    return pl.pallas_call(
        body, name="ssm_discretize_bwd",
        out_shape=[_sds(a_re.shape)] * 2 + [_sds(log_dt.shape)] + [_sds(b_re_t.shape)] * 2,
    )(a_re, a_im, log_dt, b_re_t, b_im_t, dabr, dabi, dbbr, dbbi)


def _ssm_power(abr, abi, n_sq):
    def body(r_ref, i_ref, or_ref, oi_ref):
        r, i = r_ref[...], i_ref[...]
        for _ in range(n_sq):
            r, i = r * r - i * i, 2.0 * r * i
        or_ref[...] = r
        oi_ref[...] = i

    return pl.pallas_call(body, name="ssm_power", out_shape=[_sds(abr.shape)] * 2)(abr, abi)


N_CB = SSM_W // LANES
CB_STATES = N_STATE // N_CB
ROWS = N_SEG * SSM_LK


class _Neg:
    def __init__(self, ref):
        self.ref = ref

    def __getitem__(self, idx):
        return -self.ref[idx]


def _seg_init(fin_r, fin_i, pw_r, pw_i, x_r, x_i, reverse):
    zero = jnp.zeros((1, N_STATE), F32)
    cr, ci = zero, zero
    order = range(N_SEG - 1, -1, -1) if reverse else range(N_SEG)
    pr = pw_r[...]
    pi = -pw_i[...] if reverse else pw_i[...]
    for j in order:
        x_r[pl.ds(j, 1), :] = cr
        x_i[pl.ds(j, 1), :] = ci
        fr, fi = fin_r[pl.ds(j, 1), :], fin_i[pl.ds(j, 1), :]
        cr, ci = fr + pr * cr - pi * ci, fi + pr * ci + pi * cr


def _permute_in(src_ref, dst):
    for c in range(N_CB):
        dst[c] = src_ref[:, :, pl.ds(c * LANES, LANES)].reshape(ROWS, LANES)


def _permute_out(src, dst_ref):
    for c in range(N_CB):
        dst_ref[:, :, pl.ds(c * LANES, LANES)] = src[c].reshape(SSM_LK, N_SEG, LANES)


def _scan_block(a_r, a_i, c, b_r, b_i, b_off, x_r, x_i, reverse, acc=None):
    cols = pl.ds(c * CB_STATES, CB_STATES)
    ar = jnp.broadcast_to(a_r[:, cols], (N_SEG, CB_STATES))
    ai = jnp.broadcast_to(a_i[:, cols], (N_SEG, CB_STATES))
    xr, xi = x_r[:, cols], x_i[:, cols]
    if acc is not None:
        sr = jnp.zeros((N_SEG, CB_STATES), F32)
        si = jnp.zeros((N_SEG, CB_STATES), F32)
    for t in range(SSM_LK):
        k = (SSM_LK - 1 - t) if reverse else t
        rows = pl.ds(k * N_SEG + b_off, N_SEG)
        xr, xi = ar * xr - ai * xi + b_r[rows, :], ar * xi + ai * xr + b_i[rows, :]
        b_r[rows, :] = xr
        b_i[rows, :] = xi
        if acc is not None:
            pr, pi = acc[0][pl.ds(k * N_SEG, N_SEG), :], acc[1][pl.ds(k * N_SEG, N_SEG), :]
            sr = sr + xr * pr + xi * pi
            si = si + xi * pr - xr * pi
    x_r[:, cols] = xr
    x_i[:, cols] = xi
    if acc is not None:
        acc[2][:, cols] += sr
        acc[3][:, cols] += si


def _ssm_fwd(u3, abr, abi, pw_r, pw_i, fin_r, fin_i, bb_r, bb_i, cc_r, cc_i, dskip, finals_only):
    sl = u3.shape[0]
    nch = sl // SSM_LK

    def body(u_ref, abr_ref, abi_ref, pwr_ref, pwi_ref, finr_ref, fini_ref, bbr_ref, bbi_ref,
             ccr_ref, cci_ref, d_ref, *rest):
        if finals_only:
            xfr_ref, xfi_ref, up, x_r, x_i = rest[:5]
        else:
            y_ref, xsr_ref, xsi_ref, up, yp, x_r, x_i = rest[:7]
        xs_r, xs_i = rest[-2 * N_CB:-N_CB], rest[-N_CB:]
        k = pl.program_id(0)

        @pl.when(k == 0)
        def _():
            _seg_init(finr_ref, fini_ref, pwr_ref, pwi_ref, x_r, x_i, False)

        if not finals_only:
            xsr_ref[0] = x_r[...]
            xsi_ref[0] = x_i[...]
        _permute_in(u_ref, up)

        def drive(c):
            lhs = up[c].astype(BF16)
            xs_r[c][...] = _dot(lhs, bbr_ref[c])
            xs_i[c][...] = _dot(lhs, bbi_ref[c])

        def readout(c):
            yp[c] = (_dot(xs_r[c][...].astype(BF16), ccr_ref[c]) - _dot(xs_i[c][...].astype(BF16), cci_ref[c])
                     + d_ref[:, pl.ds(c * LANES, LANES)] * up[c])

        drive(0)
        for c in range(N_CB):
            if c + 1 < N_CB:
                drive(c + 1)
            if c >= 1 and not finals_only:
                readout(c - 1)
            _scan_block(abr_ref, abi_ref, c, xs_r[c], xs_i[c], 0, x_r, x_i, False)
        if finals_only:
            @pl.when(k == nch - 1)
            def _():
                xfr_ref[...] = x_r[...]
                xfi_ref[...] = x_i[...]
        else:
            readout(N_CB - 1)
            _permute_out(yp, y_ref)

    ublk = pl.BlockSpec((SSM_LK, N_SEG, SSM_W), lambda k: (k, 0, 0))
    st = pl.BlockSpec((1, N_SEG, N_STATE), lambda k: (k, 0, 0))
    vec = _full((1, N_STATE))
    mat = _full((N_SEG, N_STATE))
    chunk = pltpu.VMEM((N_CB, ROWS, LANES), F32)
    blocks = [pltpu.VMEM((ROWS, CB_STATES), F32)] * (2 * N_CB)
    small = pltpu.VMEM((N_SEG, N_STATE), F32)
    if finals_only:
        out_specs, out_shape = [mat, mat], [_sds((N_SEG, N_STATE))] * 2
        scratch, name = [chunk, small, small] + blocks, "ssm_fwd_finals"
    else:
        out_specs = [ublk, st, st]
        out_shape = [_sds(u3.shape)] + [_sds((nch, N_SEG, N_STATE))] * 2
        scratch, name = [chunk, chunk, small, small] + blocks, "ssm_fwd"
    return pl.pallas_call(
        body, name=name, grid=(nch,),
        in_specs=[ublk, vec, vec, vec, vec, mat, mat,
                  _full((N_CB, LANES, CB_STATES)), _full((N_CB, LANES, CB_STATES)),
                  _full((N_CB, CB_STATES, LANES)), _full((N_CB, CB_STATES, LANES)), _full((1, SSM_W))],
        out_specs=out_specs, out_shape=out_shape, scratch_shapes=scratch,
        compiler_params=_cp(("arbitrary",)),
    )(u3, abr, abi, pw_r, pw_i, fin_r, fin_i, bb_r, bb_i, cc_r, cc_i, dskip)


def _ssm_bwd(u3, dy3, xst_r, xst_i, abr, abi, pw_r, pw_i, fin_r, fin_i, bb_r, bb_i, bbt_r, bbt_i,
             cct_r, cct_i, dskip, finals_only):
    sl = u3.shape[0]
    nch = sl // SSM_LK

    def body(u_ref, g_ref, xsr_ref, xsi_ref, abr_ref, abi_ref, pwr_ref, pwi_ref,
             finr_ref, fini_ref, bbr_ref, bbi_ref, btr_ref, bti_ref, ctr_ref, cti_ref, d_ref, *rest):
        if finals_only:
            lfr_ref, lfi_ref, gp, lam_r, lam_i = rest[:5]
            l_r, l_i = rest[-2 * N_CB:-N_CB], rest[-N_CB:]
        else:
            (du_ref, dar_ref, dai_ref, dbr_ref, dbi_ref, dcr_ref, dci_ref, dd_ref,
             gp, up, yp, lam_r, lam_i, x_r, x_i, sar, sai, sdd) = rest[:18]
            l_r, l_i = rest[18:18 + N_CB], rest[18 + N_CB:18 + 2 * N_CB]
            xx_r, xx_i = rest[18 + 2 * N_CB:18 + 3 * N_CB], rest[18 + 3 * N_CB:]
        t = pl.program_id(0)

        @pl.when(t == 0)
        def _():
            _seg_init(finr_ref, fini_ref, pwr_ref, pwi_ref, lam_r, lam_i, True)
            if not finals_only:
                sar[...] = jnp.zeros_like(sar)
                sai[...] = jnp.zeros_like(sai)
                sdd[...] = jnp.zeros_like(sdd)
                dbr_ref[...] = jnp.zeros_like(dbr_ref)
                dbi_ref[...] = jnp.zeros_like(dbi_ref)
                dcr_ref[...] = jnp.zeros_like(dcr_ref)
                dci_ref[...] = jnp.zeros_like(dci_ref)

        _permute_in(g_ref, gp)
        if not finals_only:
            _permute_in(u_ref, up)
            x_r[...] = xsr_ref[0]
            x_i[...] = xsi_ref[0]

        def drive(c):
            lhs = gp[c].astype(BF16)
            l_r[c][...] = _dot(lhs, ctr_ref[c])
            l_i[c][...] = -_dot(lhs, cti_ref[c])
            if not finals_only:
                cols = pl.ds(c * CB_STATES, CB_STATES)
                xx_r[c][pl.ds(0, N_SEG), :] = x_r[:, cols]
                xx_i[c][pl.ds(0, N_SEG), :] = x_i[:, cols]
                ub = up[c].astype(BF16)
                xx_r[c][pl.ds(N_SEG, ROWS), :] = _dot(ub, bbr_ref[c])
                xx_i[c][pl.ds(N_SEG, ROWS), :] = _dot(ub, bbi_ref[c])

        def collect(c):
            lrb = l_r[c][...].astype(BF16)
            lib = l_i[c][...].astype(BF16)
            ub = up[c].astype(BF16)
            gb = gp[c].astype(BF16)
            dbr_ref[c] += _dot_tn(lrb, ub)
            dbi_ref[c] += _dot_tn(lib, ub)
            dcr_ref[c] += _dot_tn(gb, xx_r[c][pl.ds(N_SEG, ROWS), :].astype(BF16))
            dci_ref[c] += -_dot_tn(gb, xx_i[c][pl.ds(N_SEG, ROWS), :].astype(BF16))
            yp[c] = _dot(lrb, btr_ref[c]) + _dot(lib, bti_ref[c]) + d_ref[:, pl.ds(c * LANES, LANES)] * gp[c]
            prod = gp[c] * up[c]
            sdd[:, pl.ds(c * LANES, LANES)] += jnp.sum(prod.reshape(SSM_LK, N_SEG, LANES), axis=0)

        drive(0)
        for c in range(N_CB):
            if c + 1 < N_CB:
                drive(c + 1)
            if finals_only:
                _scan_block(abr_ref, _Neg(abi_ref), c, l_r[c], l_i[c], 0, lam_r, lam_i, True)
            else:
                if c >= 1:
                    collect(c - 1)
                _scan_block(abr_ref, abi_ref, c, xx_r[c], xx_i[c], N_SEG, x_r, x_i, False)
                _scan_block(abr_ref, _Neg(abi_ref), c, l_r[c], l_i[c], 0, lam_r, lam_i, True,
                            acc=(xx_r[c], xx_i[c], sar, sai))
        if finals_only:
            @pl.when(t == nch - 1)
            def _():
                lfr_ref[...] = lam_r[...]
                lfi_ref[...] = lam_i[...]
        else:
            collect(N_CB - 1)
            _permute_out(yp, du_ref)

            @pl.when(t == nch - 1)
            def _():
                dar_ref[...] = jnp.sum(sar[...], axis=0, keepdims=True)
                dai_ref[...] = jnp.sum(sai[...], axis=0, keepdims=True)
                dd_ref[...] = jnp.sum(sdd[...], axis=0, keepdims=True)

    ublk = pl.BlockSpec((SSM_LK, N_SEG, SSM_W), lambda t: (nch - 1 - t, 0, 0))
    st = pl.BlockSpec((1, N_SEG, N_STATE), lambda t: (nch - 1 - t, 0, 0))
    vec = _full((1, N_STATE))
    mat = _full((N_SEG, N_STATE))
    cs = _full((N_CB, LANES, CB_STATES))
    sc = _full((N_CB, CB_STATES, LANES))
    in_specs = [ublk, ublk, st, st, vec, vec, vec, vec, mat, mat, cs, cs, sc, sc, cs, cs, _full((1, SSM_W))]
    chunk = pltpu.VMEM((N_CB, ROWS, LANES), F32)
    blocks = [pltpu.VMEM((ROWS, CB_STATES), F32)] * (2 * N_CB)
    small = pltpu.VMEM((N_SEG, N_STATE), F32)
    if finals_only:
        out_specs, out_shape = [mat, mat], [_sds((N_SEG, N_STATE))] * 2
        scratch, name = [chunk, small, small] + blocks, "ssm_bwd_finals"
    else:
        out_specs = [ublk, vec, vec, sc, sc, cs, cs, _full((1, SSM_W))]
        out_shape = ([_sds(u3.shape), _sds((1, N_STATE)), _sds((1, N_STATE))]
                     + [_sds((N_CB, CB_STATES, LANES))] * 2 + [_sds((N_CB, LANES, CB_STATES))] * 2
                     + [_sds((1, SSM_W))])
        scratch = ([chunk, chunk, chunk, small, small, small, small, small, small, pltpu.VMEM((N_SEG, SSM_W), F32)]
                   + blocks + [pltpu.VMEM((ROWS + N_SEG, CB_STATES), F32)] * (2 * N_CB))
        name = "ssm_bwd"
    return pl.pallas_call(
        body, name=name, grid=(nch,), in_specs=in_specs, out_specs=out_specs, out_shape=out_shape,
        scratch_shapes=scratch, compiler_params=_cp(("arbitrary",)),
    )(u3, dy3, xst_r, xst_i, abr, abi, pw_r, pw_i, fin_r, fin_i, bb_r, bb_i, bbt_r, bbt_i, cct_r, cct_i, dskip)


def _row(tm, w):
    return pl.BlockSpec((tm, w), lambda i: (i, 0))


def _acc_rows(ref, rows, first):
    @pl.when(first)
    def _():
        ref[...] = jnp.zeros_like(ref)

    ref[...] += jnp.sum(rows, axis=0, keepdims=True)


def _fwd_mix(attn, y, x, glu_w, glu_b, ga, gs, w_out, g2, tm=1024):
    s = x.shape[0]

    def body(a_ref, y_ref, x_ref, gw_ref, gb_ref, ga_ref, gs_ref, wo_ref, g2_ref, mix_ref, x2_ref, h_ref):
        a = a_ref[...]
        anb = ((a * _rms(a)) * ga_ref[...]).astype(BF16)
        z, _ = _gelu(y_ref[...])
        so = z * jax.nn.sigmoid(_dot(z.astype(BF16), gw_ref[...]) + gb_ref[...])
        snb = ((so * _rms(so)) * gs_ref[...]).astype(BF16)
        mix_ref[:, pl.ds(0, ATTN_W)] = anb
        mix_ref[:, pl.ds(ATTN_W, SSM_W)] = snb
        x2 = x_ref[...] + (_dot(anb, wo_ref[pl.ds(0, ATTN_W), :]) + _dot(snb, wo_ref[pl.ds(ATTN_W, SSM_W), :]))
        x2_ref[...] = x2
        h_ref[...] = ((x2 * _rms(x2)) * g2_ref[...]).astype(BF16)

    return pl.pallas_call(
        body, name="fwd_mix", grid=(s // tm,),
        in_specs=[_row(tm, ATTN_W), _row(tm, SSM_W), _row(tm, D_MODEL), _full((SSM_W, SSM_W)), _full((1, SSM_W)),
                  _full((1, ATTN_W)), _full((1, SSM_W)), _full((D_MODEL, D_MODEL)), _full((1, D_MODEL))],
        out_specs=[_row(tm, D_MODEL), _row(tm, D_MODEL), _row(tm, D_MODEL)],
        out_shape=[_sds((s, D_MODEL), BF16), _sds((s, D_MODEL)), _sds((s, D_MODEL), BF16)],
        compiler_params=_cp(("parallel",)),
    )(attn, y, x, glu_w, glu_b, ga, gs, w_out, g2)


def _mlp_up(h, w_up, tm=512, bn=1024):
    s = h.shape[0]

    def body(h_ref, w_ref, r_ref, hdn_ref):
        hv = h_ref[...]
        for j in range(D_FF // bn):
            cols = pl.ds(j * bn, bn)
            r = jnp.maximum(_dot(hv, w_ref[:, cols]), 0.0)
            r_ref[:, cols] = r.astype(BF16)
            hdn_ref[:, cols] = (r * r).astype(BF16)

    return pl.pallas_call(
        body, name="mlp_up", grid=(s // tm,),
        in_specs=[_row(tm, D_MODEL), _full((D_MODEL, D_FF))],
        out_specs=[_row(tm, D_FF), _row(tm, D_FF)], out_shape=[_sds((s, D_FF), BF16)] * 2,
        compiler_params=_cp(("parallel",)),
    )(h, w_up)


def _mlp_down_loss(hdn, w_down, x2, tgt, tm=512):
    s = x2.shape[0]

    def body(hdn_ref, w_ref, x2_ref, t_ref, dy_ref, dyb_ref, sse_ref):
        err = (x2_ref[...] + _dot(hdn_ref[...], w_ref[...])) - t_ref[...]
        dy = err * (1.0 / D_MODEL)
        dy_ref[...] = dy
        dyb_ref[...] = dy.astype(BF16)

        @pl.when(pl.program_id(0) == 0)
        def _():
            sse_ref[...] = jnp.zeros_like(sse_ref)

        sse_ref[...] += jnp.sum(jnp.sum(err * err, axis=0, keepdims=True), axis=1, keepdims=True)

    return pl.pallas_call(
        body, name="mlp_down_loss", grid=(s // tm,),
        in_specs=[_row(tm, D_FF), _full((D_FF, D_MODEL)), _row(tm, D_MODEL), _row(tm, D_MODEL)],
        out_specs=[_row(tm, D_MODEL), _row(tm, D_MODEL), _full((1, 1))],
        out_shape=[_sds((s, D_MODEL)), _sds((s, D_MODEL), BF16), _sds((1, 1))],
        compiler_params=_cp(("arbitrary",)),
    )(hdn, w_down, x2, tgt)


def _mlp_down_bwd(dyb, w_down_t, r, tm=512, bn=1024):
    s = dyb.shape[0]

    def body(dy_ref, w_ref, r_ref, dup_ref):
        dyv = dy_ref[...]
        for j in range(D_FF // bn):
            cols = pl.ds(j * bn, bn)
            dup_ref[:, cols] = (_dot(dyv, w_ref[:, cols]) * (2.0 * r_ref[:, cols].astype(F32))).astype(BF16)

    return pl.pallas_call(
        body, name="mlp_down_bwd", grid=(s // tm,),
        in_specs=[_row(tm, D_MODEL), _full((D_MODEL, D_FF)), _row(tm, D_FF)],
        out_specs=_row(tm, D_FF), out_shape=_sds((s, D_FF), BF16),
        compiler_params=_cp(("parallel",)),
    )(dyb, w_down_t, r)


def _mlp_up_bwd(dup, w_up_t, x2, g2, dy, tm=512):
    s = x2.shape[0]

    def body(dup_ref, w_ref, x2_ref, g2_ref, dy_ref, dx2_ref, dx2b_ref, dg_ref):
        dx, dg_rows = _rms_bwd(_dot(dup_ref[...], w_ref[...]), x2_ref[...], g2_ref[...], D_MODEL)
        dx2 = dy_ref[...] + dx
        dx2_ref[...] = dx2
        dx2b_ref[...] = dx2.astype(BF16)
        _acc_rows(dg_ref, dg_rows, pl.program_id(0) == 0)

    return pl.pallas_call(
        body, name="mlp_up_bwd", grid=(s // tm,),
        in_specs=[_row(tm, D_FF), _full((D_FF, D_MODEL)), _row(tm, D_MODEL), _full((1, D_MODEL)), _row(tm, D_MODEL)],
        out_specs=[_row(tm, D_MODEL), _row(tm, D_MODEL), _full((1, D_MODEL))],
        out_shape=[_sds((s, D_MODEL)), _sds((s, D_MODEL), BF16), _sds((1, D_MODEL))],
        compiler_params=_cp(("arbitrary",)),
    )(dup, w_up_t, x2, g2, dy)


def _mix_bwd(dx2b, w_out_t, attn, y, glu_w, glu_b, glu_w_t, ga, gs, tm=1024):
    s = attn.shape[0]

    def body(dx2_ref, wot_ref, a_ref, y_ref, gw_ref, gb_ref, gwt_ref, ga_ref, gs_ref,
             da_ref, dys_ref, z_ref, dpre_ref, dga_ref, dgs_ref, dgb_ref):
        first = pl.program_id(0) == 0
        dmix = _dot(dx2_ref[...], wot_ref[...])
        da, dga_rows = _rms_bwd(dmix[:, :ATTN_W], a_ref[...], ga_ref[...], ATTN_W)
        da_ref[...] = da
        yv = y_ref[...]
        z, t = _gelu(yv)
        gate = jax.nn.sigmoid(_dot(z.astype(BF16), gw_ref[...]) + gb_ref[...])
        dso, dgs_rows = _rms_bwd(dmix[:, ATTN_W:], z * gate, gs_ref[...], SSM_W)
        dpre = dso * z * gate * (1.0 - gate)
        dpre_b = dpre.astype(BF16)
        dz = dso * gate + _dot(dpre_b, gwt_ref[...])
        dgelu = 0.5 * (1.0 + t) + 0.5 * yv * (1.0 - t * t) * (GELU_C * (1.0 + 3.0 * 0.044715 * (yv * yv)))
        dys_ref[...] = dz * dgelu
        z_ref[...] = z.astype(BF16)
        dpre_ref[...] = dpre_b
        _acc_rows(dga_ref, dga_rows, first)
        _acc_rows(dgs_ref, dgs_rows, first)
        _acc_rows(dgb_ref, dpre, first)

    vec = _full((1, SSM_W))
    return pl.pallas_call(
        body, name="mix_bwd", grid=(s // tm,),
        in_specs=[_row(tm, D_MODEL), _full((D_MODEL, D_MODEL)), _row(tm, ATTN_W), _row(tm, SSM_W),
                  _full((SSM_W, SSM_W)), vec, _full((SSM_W, SSM_W)), vec, vec],
        out_specs=[_row(tm, ATTN_W), _row(tm, SSM_W), _row(tm, SSM_W), _row(tm, SSM_W), vec, vec, vec],
        out_shape=[_sds((s, ATTN_W)), _sds((s, SSM_W)), _sds((s, SSM_W), BF16), _sds((s, SSM_W), BF16),
                   _sds((1, ATTN_W)), _sds((1, SSM_W)), _sds((1, SSM_W))],
        compiler_params=_cp(("arbitrary",)),
    )(dx2b, w_out_t, attn, y, glu_w, glu_b, glu_w_t, ga, gs)


def _qk_bwd(dqn, dkn, qk, dv, du, gq, gk, gmat, tm=1024):
    s = qk.shape[0]

    def body(dq_ref, dk_ref, qk_ref, dv_ref, du_ref, gq_ref, gk_ref, gm_ref, dp_ref, dgq_ref, dgk_ref):
        first = pl.program_id(0) == 0
        gm = gm_ref[...]
        for idx, (d_ref, g_ref, dg_ref) in enumerate(((dq_ref, gq_ref, dgq_ref), (dk_ref, gk_ref, dgk_ref))):
            xv = qk_ref[:, pl.ds(idx * ATTN_W, ATTN_W)]
            dyv = d_ref[...]
            r = lax.rsqrt(_group_mean(xv * xv, gm) + EPS)
            gdy = dyv * g_ref[...]
            dx = r * gdy - xv * (r * r * r) * _group_mean(gdy * xv, gm)
            dp_ref[:, pl.ds(idx * ATTN_W, ATTN_W)] = dx.astype(BF16)
            _acc_rows(dg_ref, dyv * (xv * r), first)
        dp_ref[:, pl.ds(2 * ATTN_W, ATTN_W)] = dv_ref[...].astype(BF16)
        dp_ref[:, pl.ds(3 * ATTN_W, SSM_W)] = du_ref[...].astype(BF16)

    vec = _full((1, ATTN_W))
    return pl.pallas_call(
        body, name="qk_bwd", grid=(s // tm,),
        in_specs=[_row(tm, ATTN_W), _row(tm, ATTN_W), _row(tm, 2 * ATTN_W), _row(tm, ATTN_W), _row(tm, SSM_W),
                  vec, vec, _full((ATTN_W, ATTN_W))],
        out_specs=[_row(tm, 4 * ATTN_W), vec, vec],
        out_shape=[_sds((s, 4 * ATTN_W), BF16), _sds((1, ATTN_W)), _sds((1, ATTN_W))],
        compiler_params=_cp(("arbitrary",)),
    )(dqn, dkn, qk, dv, du, gq, gk, gmat)


def _in_bwd(dproj, w_in_t, x, g1, dx2, ex=None, tm=1024):
    s = x.shape[0]
    steps = s // tm

    def body(*refs):
        (dp_ref, w_ref, x_ref, g1_ref, dx2_ref), (gx_ref, dg_ref), _, hx = _carry(ex, refs, 5, 2)
        if ex is not None:
            @pl.when(pl.program_id(0) == 0)
            def _():
                ex.start(*hx)

        dx, dg_rows = _rms_bwd(_dot(dp_ref[...], w_ref[...]), x_ref[...], g1_ref[...], D_MODEL)
        gx_ref[...] = dx2_ref[...] + dx
        _acc_rows(dg_ref, dg_rows, pl.program_id(0) == 0)
        if ex is not None:
            @pl.when(pl.program_id(0) == steps - 1)
            def _():
                ex.wait(*hx)

    hosted = ex is not None
    return pl.pallas_call(
        body, name="in_bwd", grid=(steps,),
        in_specs=[_row(tm, 4 * ATTN_W), _full((4 * ATTN_W, D_MODEL)), _row(tm, D_MODEL), _full((1, D_MODEL)),
                  _row(tm, D_MODEL)] + (ex.specs() if hosted else []),
        out_specs=[_row(tm, D_MODEL), _full((1, D_MODEL))] + (ex.specs() if hosted else []),
        out_shape=[_sds((s, D_MODEL)), _sds((1, D_MODEL))] + (ex.out_shape() if hosted else []),
        scratch_shapes=ex.scratch() if hosted else [],
        compiler_params=_cp(("arbitrary",)),
    )(dproj, w_in_t, x, g1, dx2, *(ex.srcs if hosted else []))


def _mm_tn(a, b, name, ts=2048):
    s, k = a.shape
    n = b.shape[1]
    bk, bn = min(k, 1024), min(n, 1024)
    steps = s // ts

    def body(a_ref, b_ref, o_ref, acc):
        t = pl.program_id(2)

        @pl.when(t == 0)
        def _():
            acc[...] = jnp.zeros_like(acc)

        acc[...] += _dot_tn(a_ref[...], b_ref[...])

        @pl.when(t == steps - 1)
        def _():
            o_ref[...] = acc[...].astype(BF16)

    return pl.pallas_call(
        body, name=name, grid=(k // bk, n // bn, steps),
        in_specs=[pl.BlockSpec((ts, bk), lambda i, j, t: (t, i)), pl.BlockSpec((ts, bn), lambda i, j, t: (t, j))],
        out_specs=pl.BlockSpec((bk, bn), lambda i, j, t: (i, j)), out_shape=_sds((k, n), BF16),
        scratch_shapes=[pltpu.VMEM((bk, bn), F32)],
        compiler_params=_cp(("parallel", "parallel", "arbitrary")),
    )(a, b)


def _peer(k):
    x, y, c = lax.axis_index("x"), lax.axis_index("y"), lax.axis_index("c")
    px = 1 - x if k & 4 else x
    py = 1 - y if k & 2 else y
    pc = 1 - c if k & 1 else c
    return (px, py, pc), 4 * px + 2 * py + pc


def _gather_rows(x_shard):
    m_per, n = x_shard.shape

    def body(x_ref, out_ref, send_sems, recv_sems, local_sem):
        x, y, c = lax.axis_index("x"), lax.axis_index("y"), lax.axis_index("c")
        me, sibling = (x, y, c), (x, y, 1 - c)
        chips = [(1 - x, y), (x, 1 - y), (1 - x, 1 - y)]

        def rows(px, py, pc):
            return out_ref.at[pl.ds((4 * px + 2 * py + pc) * m_per, m_per), :]

        def copy(k, block, to, src=None):
            return pltpu.make_async_remote_copy(
                src_ref=rows(*block) if src is None else src, dst_ref=rows(*block),
                send_sem=send_sems.at[k], recv_sem=recv_sems.at[k], device_id=to, device_id_type=MESH)

        mine = pltpu.make_async_copy(x_ref, rows(*me), local_sem)
        mine.start()
        first = [copy(0, me, sibling, src=x_ref)]
        first += [copy(1 + j, me, (*chip, c), src=x_ref) for j, chip in enumerate(chips)]
        for cp in first:
            cp.start()
        passed = [copy(4 + j, (*chip, c), sibling) for j, chip in enumerate(chips)]
        for j, chip in enumerate(chips):
            copy(1 + j, (*chip, c), me).wait_recv()
            passed[j].start()
        copy(0, sibling, me).wait_recv()
        for j, chip in enumerate(chips):
            copy(4 + j, (*chip, 1 - c), me).wait_recv()
        for cp in first + passed:
            cp.wait_send()
        mine.wait()

    return pl.pallas_call(
        body, name="gather_weights", out_shape=_sds((N_DEV * m_per, n), x_shard.dtype),
        in_specs=[pl.BlockSpec(memory_space=pltpu.VMEM)], out_specs=pl.BlockSpec(memory_space=pltpu.VMEM),
        scratch_shapes=[pltpu.SemaphoreType.DMA((7,)), pltpu.SemaphoreType.DMA((7,)), pltpu.SemaphoreType.DMA],
        compiler_params=pltpu.CompilerParams(vmem_limit_bytes=VMEM_LIMIT),
    )(x_shard)


class _Exchange:
    def __init__(self, srcs, whole):
        self.srcs, self.whole, self.n = list(srcs), list(whole), len(srcs)
        self.rows = [a.shape[0] // N_DEV if w is False else a.shape[0] for a, w in zip(self.srcs, self.whole)]
        self.cols = [a.shape[1] // N_DEV if w == "col" else a.shape[1] for a, w in zip(self.srcs, self.whole)]

    def specs(self):
        return [pl.BlockSpec(memory_space=pl.ANY)] * self.n

    def out_shape(self):
        return [_sds((N_DEV, r, c), a.dtype) for r, c, a in zip(self.rows, self.cols, self.srcs)]

    def scratch(self):
        return [pltpu.SemaphoreType.DMA((self.n * 7,)), pltpu.SemaphoreType.DMA((self.n * 7,)),
                pltpu.SemaphoreType.DMA((self.n,))]

    def _copies(self, ins, outs, sems):
        send_sems, recv_sems, local_sems = sems
        _, me = _peer(0)
        for w in range(self.n):
            for k in range(N_DEV):
                peer, pidx = _peer(k)
                if self.whole[w] is True:
                    src = ins[w]
                elif self.whole[w] == "col":
                    src = ins[w].at[:, pl.ds(pl.multiple_of(pidx * self.cols[w], self.cols[w]), self.cols[w])]
                else:
                    src = ins[w].at[pl.ds(pidx * self.rows[w], self.rows[w]), :]
                if k == 0:
                    yield k, pltpu.make_async_copy(src, outs[w].at[me], local_sems.at[w]), None
                else:
                    sem = w * 7 + k - 1
                    out = pltpu.make_async_remote_copy(src_ref=src, dst_ref=outs[w].at[me], send_sem=send_sems.at[sem],
                                                       recv_sem=recv_sems.at[sem], device_id=peer, device_id_type=MESH)
                    back = pltpu.make_async_remote_copy(src_ref=src, dst_ref=outs[w].at[pidx], send_sem=send_sems.at[sem],
                                                        recv_sem=recv_sems.at[sem], device_id=peer, device_id_type=MESH)
                    yield k, out, back

    def start(self, ins, outs, sems):
        for _, out, _ in self._copies(ins, outs, sems):
            out.start()

    def wait(self, ins, outs, sems):
        for k, out, back in self._copies(ins, outs, sems):
            if k == 0:
                out.wait()
            else:
                back.wait_recv()
                out.wait_send()


def _carry(ex, refs, n_in, n_out):
    nh = ex.n if ex is not None else 0
    ins, hin = refs[:n_in], refs[n_in:n_in + nh]
    outs = refs[n_in + nh:n_in + nh + n_out]
    hout = refs[n_in + nh + n_out:n_in + 2 * nh + n_out]
    rest = refs[n_in + 2 * nh + n_out:]
    if ex is None:
        return ins, outs, rest, None
    return ins, outs, rest[:-3], (hin, hout, rest[-3:])


def _exchange_now(srcs, whole, name):
    ex = _Exchange(srcs, whole)

    def body(*refs):
        _, _, _, (hin, hout, sems) = _carry(ex, refs, 0, 0)
        ex.start(hin, hout, sems)
        ex.wait(hin, hout, sems)

    return pl.pallas_call(body, name=name, out_shape=ex.out_shape(), in_specs=ex.specs(), out_specs=ex.specs(),
                          scratch_shapes=ex.scratch())(*srcs)


def _split_exchange_start(src):
    rows, width = src.shape[0], src.shape[1] // N_DEV

    def body(src_ref, land_ref, send_sems, recv_sems, src_thru, land_thru, token):
        barrier = pltpu.get_barrier_semaphore()
        for k in range(1, N_DEV):
            pl.semaphore_signal(barrier, inc=1, device_id=_peer(k)[0], device_id_type=MESH)
        pl.semaphore_wait(barrier, N_DEV - 1)
        _, me = _peer(0)
        for k in range(1, N_DEV):
            peer, pidx = _peer(k)
            pltpu.make_async_remote_copy(
                src_ref=src_ref.at[:, pl.ds(pl.multiple_of(pidx * width, width), width)], dst_ref=land_ref.at[me],
                send_sem=send_sems.at[k - 1], recv_sem=recv_sems.at[k - 1], device_id=peer, device_id_type=MESH).start()
        token[...] = jnp.zeros_like(token)

    hbm = pl.BlockSpec(memory_space=pltpu.HBM)
    sem = pl.BlockSpec(memory_space=pltpu.SEMAPHORE)
    land = lax.empty((N_DEV, rows, width), src.dtype)
    return pl.pallas_call(
        body, name="w_in_exchange_start",
        out_shape=(pltpu.SemaphoreType.DMA((N_DEV - 1,)), pltpu.SemaphoreType.DMA((N_DEV - 1,)),
                   pltpu.HBM(src.shape, src.dtype), pltpu.HBM(land.shape, land.dtype), _sds((8, LANES))),
        in_specs=(hbm, hbm), out_specs=(sem, sem, hbm, hbm, pl.BlockSpec(memory_space=pltpu.VMEM)),
        input_output_aliases={0: 2, 1: 3},
        compiler_params=pltpu.CompilerParams(has_side_effects=pltpu.SideEffectType.DATAFLOW_SIDE_EFFECTING, collective_id=0),
    )(pltpu.with_memory_space_constraint(src, pltpu.HBM), pltpu.with_memory_space_constraint(land, pltpu.HBM))


def _split_exchange_wait(send_sems, recv_sems, src_thru, land_thru, after):
    width = land_thru.shape[2]

    def body(src_ref, land_ref, send_sems, recv_sems, after_ref, src_dead, got_ref):
        _, me = _peer(0)
        for k in range(1, N_DEV):
            peer, pidx = _peer(k)
            copy = pltpu.make_async_remote_copy(
                src_ref=src_ref.at[:, pl.ds(pl.multiple_of(pidx * width, width), width)], dst_ref=land_ref.at[pidx],
                send_sem=send_sems.at[k - 1], recv_sem=recv_sems.at[k - 1], device_id=peer, device_id_type=MESH)
            copy.wait_send()
            copy.wait_recv()

    hbm = pl.BlockSpec(memory_space=pltpu.HBM)
    sem = pl.BlockSpec(memory_space=pltpu.SEMAPHORE)
    return pl.pallas_call(
        body, name="w_in_exchange_wait",
        out_shape=(pltpu.HBM(src_thru.shape, src_thru.dtype), pltpu.HBM(land_thru.shape, land_thru.dtype)),
        in_specs=(hbm, hbm, sem, sem, pl.BlockSpec(memory_space=pl.ANY)), out_specs=(hbm, hbm),
        input_output_aliases={0: 0, 1: 1},
        compiler_params=pltpu.CompilerParams(has_side_effects=pltpu.SideEffectType.DATAFLOW_SIDE_EFFECTING),
    )(src_thru, land_thru, send_sems, recv_sems, after)


def _adamw(w, m, v, gparts, name):
    r, c = w.shape
    tr = r if r * c <= 256 * 1024 else 128 * 1024 // c

    def body(w_ref, m_ref, v_ref, g_ref, go_ref, d_ref, mo_ref, vo_ref):
        g = g_ref[0].astype(F32)
        for i in range(1, N_DEV):
            g = g + g_ref[i].astype(F32)
        go_ref[...] = g
        d_ref[...], mo_ref[...], vo_ref[...] = _adamw_step(w_ref[...], m_ref[...], v_ref[...], g)

    blk = pl.BlockSpec((tr, c), lambda i: (i, 0))
    return pl.pallas_call(
        body, name=name, grid=(r // tr,),
        in_specs=[blk, blk, blk, pl.BlockSpec((N_DEV, tr, c), lambda i: (0, i, 0))],
        out_specs=[blk] * 4, out_shape=[_sds((r, c))] * 4,
        compiler_params=_cp(("parallel",)),
    )(w, m, v, gparts)


def _adamw_step(w, m, v, g):
    nm = ADAM_B1 * m + (1.0 - ADAM_B1) * g
    nv = ADAM_B2 * v + (1.0 - ADAM_B2) * (g * g)
    m_hat = nm / (1.0 - ADAM_B1 ** ADAM_STEP)
    v_hat = nv / (1.0 - ADAM_B2 ** ADAM_STEP)
    return -ADAM_LR * (m_hat / (jnp.sqrt(v_hat) + ADAM_EPS) + ADAM_WD * w), nm, nv


def _sum_slots(parts):
    def body(p_ref, o_ref):
        g = p_ref[0]
        for i in range(1, N_DEV):
            g = g + p_ref[i]
        o_ref[...] = g

    return pl.pallas_call(body, name="sum_small_grads", out_shape=_sds(parts.shape[1:]))(parts)


def _adamw_native(ws, ms, vs, gs):
    n = len(ws)

    def body(*refs):
        for i in range(n):
            w_ref, m_ref, v_ref, g_ref = refs[i], refs[n + i], refs[2 * n + i], refs[3 * n + i]
            d, nm, nv = _adamw_step(w_ref[...], m_ref[...], v_ref[...], g_ref[...])
            refs[4 * n + i][...] = d
            refs[5 * n + i][...] = nm
            refs[6 * n + i][...] = nv

    outs = pl.pallas_call(body, name="adamw_small", out_shape=[_sds(w.shape) for w in ws] * 3,
                          compiler_params=pltpu.CompilerParams(vmem_limit_bytes=VMEM_LIMIT))(*ws, *ms, *vs, *gs)
    return outs[:n], outs[n:2 * n], outs[2 * n:]


def _block_diag(a, states_first):
    a4 = a.reshape(N_CB, 8, SSM_GROUP, SSM_STATE)
    eye = jnp.eye(8, dtype=a.dtype)
    if states_first:
        return jnp.einsum("bgcp,gh->bgphc", a4, eye).reshape(N_CB, CB_STATES, LANES)
    return jnp.einsum("bgcp,gh->bgchp", a4, eye).reshape(N_CB, LANES, CB_STATES)


def _block_diag_of(full, states_first):
    if states_first:
        picked = jnp.einsum("bgphc,gh->bgcp", full.reshape(N_CB, 8, SSM_STATE, 8, SSM_GROUP), jnp.eye(8, dtype=full.dtype))
    else:
        picked = jnp.einsum("bgchp,gh->bgcp", full.reshape(N_CB, 8, SSM_GROUP, 8, SSM_STATE), jnp.eye(8, dtype=full.dtype))
    return picked.reshape(SSM_GROUPS, SSM_GROUP, SSM_STATE)


SMALL_EARLY = ("ssm_a_re", "ssm_a_im", "ssm_log_dt", "ssm_b_re", "ssm_b_im", "ssm_c_re", "ssm_c_im", "ssm_d", "glu_b",
               "attn_out_norm_g", "ssm_out_norm_g", "norm2_g")
SMALL_MID = ("q_norm_g", "k_norm_g")
SMALL_LATE = ("norm1_g",)
SMALL = SMALL_EARLY + SMALL_MID + SMALL_LATE


def _pack_small(arrs):
    parts = []
    for a in arrs:
        flat = a.reshape(-1)
        rows = -(-flat.shape[0] // (8 * LANES)) * 8
        parts.append(jnp.pad(flat, (0, rows * LANES - flat.shape[0])).reshape(rows, LANES))
    return jnp.concatenate(parts, axis=0)


def _unpack_small(packed, shapes):
    out, r0 = [], 0
    for shp in shapes:
        size = math.prod(shp)
        rows = -(-size // (8 * LANES)) * 8
        out.append(packed[r0:r0 + rows].reshape(-1)[:size].reshape(shp))
        r0 += rows
    return out


def kernel(x, norm1_g, w_in, q_norm_g, k_norm_g, ssm_a_re, ssm_a_im, ssm_log_dt, ssm_b_re, ssm_b_im, ssm_c_re, ssm_c_im, ssm_d, glu_w, glu_b, attn_out_norm_g, ssm_out_norm_g, w_out, norm2_g, w_mlp_up, w_mlp_down, loss_target, m_norm1_g, m_w_in, m_q_norm_g, m_k_norm_g, m_ssm_a_re, m_ssm_a_im, m_ssm_log_dt, m_ssm_b_re, m_ssm_b_im, m_ssm_c_re, m_ssm_c_im, m_ssm_d, m_glu_w, m_glu_b, m_attn_out_norm_g, m_ssm_out_norm_g, m_w_out, m_norm2_g, m_w_mlp_up, m_w_mlp_down, v_norm1_g, v_w_in, v_q_norm_g, v_k_norm_g, v_ssm_a_re, v_ssm_a_im, v_ssm_log_dt, v_ssm_b_re, v_ssm_b_im, v_ssm_c_re, v_ssm_c_im, v_ssm_d, v_glu_w, v_glu_b, v_attn_out_norm_g, v_ssm_out_norm_g, v_w_out, v_norm2_g, v_w_mlp_up, v_w_mlp_down):
    weights = dict(norm1_g=norm1_g, w_in=w_in, q_norm_g=q_norm_g, k_norm_g=k_norm_g, ssm_a_re=ssm_a_re,
                   ssm_a_im=ssm_a_im, ssm_log_dt=ssm_log_dt, ssm_b_re=ssm_b_re, ssm_b_im=ssm_b_im,
                   ssm_c_re=ssm_c_re, ssm_c_im=ssm_c_im, ssm_d=ssm_d, glu_w=glu_w, glu_b=glu_b,
                   attn_out_norm_g=attn_out_norm_g, ssm_out_norm_g=ssm_out_norm_g, w_out=w_out, norm2_g=norm2_g,
                   w_mlp_up=w_mlp_up, w_mlp_down=w_mlp_down)
    mom_m = dict(norm1_g=m_norm1_g, w_in=m_w_in, q_norm_g=m_q_norm_g, k_norm_g=m_k_norm_g, ssm_a_re=m_ssm_a_re,
                 ssm_a_im=m_ssm_a_im, ssm_log_dt=m_ssm_log_dt, ssm_b_re=m_ssm_b_re, ssm_b_im=m_ssm_b_im,
                 ssm_c_re=m_ssm_c_re, ssm_c_im=m_ssm_c_im, ssm_d=m_ssm_d, glu_w=m_glu_w, glu_b=m_glu_b,
                 attn_out_norm_g=m_attn_out_norm_g, ssm_out_norm_g=m_ssm_out_norm_g, w_out=m_w_out,
                 norm2_g=m_norm2_g, w_mlp_up=m_w_mlp_up, w_mlp_down=m_w_mlp_down)
    mom_v = dict(norm1_g=v_norm1_g, w_in=v_w_in, q_norm_g=v_q_norm_g, k_norm_g=v_k_norm_g, ssm_a_re=v_ssm_a_re,
                 ssm_a_im=v_ssm_a_im, ssm_log_dt=v_ssm_log_dt, ssm_b_re=v_ssm_b_re, ssm_b_im=v_ssm_b_im,
                 ssm_c_re=v_ssm_c_re, ssm_c_im=v_ssm_c_im, ssm_d=v_ssm_d, glu_w=v_glu_w, glu_b=v_glu_b,
                 attn_out_norm_g=v_attn_out_norm_g, ssm_out_norm_g=v_ssm_out_norm_g, w_out=v_w_out,
                 norm2_g=v_norm2_g, w_mlp_up=v_w_mlp_up, w_mlp_down=v_w_mlp_down)
    order = list(weights)

    xs, tgt = x[0], loss_target[0]
    s = xs.shape[0]
    assert s % ATTN_CHUNK == 0 and (s // N_SEG) % SSM_LK == 0
    seg_len = s // N_SEG
    n_sq = seg_len.bit_length() - 1
    assert 1 << n_sq == seg_len

    w_in_t = _gather_rows(w_in[0].T.astype(BF16))
    w_in_full = w_in_t.T
    later = _Exchange([glu_w[0].astype(BF16), w_out[0].astype(BF16), w_mlp_up[0].T.astype(BF16),
                       w_mlp_down[0].astype(BF16)], [True] * 4)

    gq = jnp.tile(q_norm_g[0], ATTN_W // HEAD_DIM)[None]
    gk = jnp.tile(k_norm_g[0], ATTN_W // HEAD_DIM)[None]
    lane = jnp.arange(ATTN_W) // HEAD_DIM
    gmat = jnp.where(lane[:, None] == lane[None, :], 1.0 / HEAD_DIM, 0.0).astype(BF16)
    a_re3 = ssm_a_re[0][:, None, :]
    a_im3 = ssm_a_im[0][:, None, :]
    ldt3 = ssm_log_dt[0][:, None, None]
    b_re_t = jnp.swapaxes(ssm_b_re[0], 1, 2)
    b_im_t = jnp.swapaxes(ssm_b_im[0], 1, 2)
    c_re, c_im = ssm_c_re[0], ssm_c_im[0]
    dskip = ssm_d[0].reshape(1, SSM_W)

    qk, qn, kn, v, u, xn = _fwd_proj(xs, norm1_g, w_in_full, gq, gk, gmat)
    attn, lse, glu_g, w_out_g, w_up_g, w_down_g = _attn_fwd(qn, kn, v, later)
    glu_full = glu_g.reshape(SSM_W, SSM_W)
    w_out_full = w_out_g.reshape(D_MODEL, D_MODEL)
    w_up_t = w_up_g.reshape(D_FF, D_MODEL)
    w_down_full = w_down_g.reshape(D_FF, D_MODEL)
    w_up_full, w_out_t, w_down_t, glu_t = w_up_t.T, w_out_full.T, w_down_full.T, glu_full.T

    abr3, abi3, bbr, bbi = _ssm_discretize(a_re3, a_im3, ldt3, b_re_t, b_im_t)
    abr, abi = abr3.reshape(1, N_STATE), abi3.reshape(1, N_STATE)
    pw_r, pw_i = _ssm_power(abr, abi, n_sq)
    bb_r, bb_i = _block_diag(bbr, False).astype(BF16), _block_diag(bbi, False).astype(BF16)
    bbt_r, bbt_i = _block_diag(bbr, True).astype(BF16), _block_diag(bbi, True).astype(BF16)
    cc_r, cc_i = _block_diag(c_re, True).astype(BF16), _block_diag(c_im, True).astype(BF16)
    cct_r, cct_i = _block_diag(c_re, False).astype(BF16), _block_diag(c_im, False).astype(BF16)
    seg_major = lambda a: jnp.swapaxes(a.reshape(N_SEG, seg_len, SSM_W), 0, 1)
    seg_minor = lambda a: jnp.swapaxes(a, 0, 1).reshape(s, SSM_W)
    u3 = seg_major(u)
    zero_fin = jnp.zeros((N_SEG, N_STATE), F32)
    ssm_args = (abr, abi, pw_r, pw_i)
    xf_r, xf_i = _ssm_fwd(u3, *ssm_args, zero_fin, zero_fin, bb_r, bb_i, cc_r, cc_i, dskip, True)
    y3, xst_r, xst_i = _ssm_fwd(u3, *ssm_args, xf_r, xf_i, bb_r, bb_i, cc_r, cc_i, dskip, False)
    y = seg_minor(y3)

    mix, x2, h = _fwd_mix(attn, y, xs, glu_full, glu_b, attn_out_norm_g, ssm_out_norm_g, w_out_full, norm2_g)
    r_act, hdn = _mlp_up(h, w_up_full)
    dy, dyb, sse = _mlp_down_loss(hdn, w_down_full, x2, tgt)

    dup = _mlp_down_bwd(dyb, w_down_t, r_act)
    g_w_down = _mm_tn(hdn, dyb, "grad_w_down")
    dx2, dx2b, g_norm2 = _mlp_up_bwd(dup, w_up_t, x2, norm2_g, dy)
    g_w_up = _mm_tn(h, dup, "grad_w_up")
    dattn, dys, z_b, dpre_b, g_ga, g_gs, g_glu_b = _mix_bwd(dx2b, w_out_t, attn, y, glu_full, glu_b, glu_t,
                                                             attn_out_norm_g, ssm_out_norm_g)
    g_w_out = _mm_tn(mix, dx2b, "grad_w_out")
    g_glu_w = _mm_tn(z_b, dpre_b, "grad_glu_w")
    dy3 = seg_major(dys)
    bwd_args = (u3, dy3, xst_r, xst_i, abr, abi, pw_r, pw_i)
    lf_r, lf_i = _ssm_bwd(*bwd_args, zero_fin, zero_fin, bb_r, bb_i, bbt_r, bbt_i, cct_r, cct_i, dskip, True)
    du3, dab_r, dab_i, dbb_r, dbb_i, dcc_r, dcc_i, g_d = _ssm_bwd(*bwd_args, lf_r, lf_i, bb_r, bb_i, bbt_r, bbt_i,
                                                                 cct_r, cct_i, dskip, False)
    g_a_re3, g_a_im3, g_ldt3, g_b_re_t, g_b_im_t = _ssm_discretize_bwd(
        a_re3, a_im3, ldt3, b_re_t, b_im_t, dab_r.reshape(a_re3.shape), dab_i.reshape(a_re3.shape),
        _block_diag_of(dbb_r, True), _block_diag_of(dbb_i, True))
    g_c_re, g_c_im = _block_diag_of(dcc_r, False), _block_diag_of(dcc_i, False)
    small_grads = dict(
        ssm_a_re=g_a_re3.reshape(ssm_a_re.shape), ssm_a_im=g_a_im3.reshape(ssm_a_im.shape),
        ssm_log_dt=g_ldt3.reshape(ssm_log_dt.shape), ssm_b_re=jnp.swapaxes(g_b_re_t, 1, 2)[None],
        ssm_b_im=jnp.swapaxes(g_b_im_t, 1, 2)[None], ssm_c_re=g_c_re[None], ssm_c_im=g_c_im[None],
        ssm_d=g_d.reshape(ssm_d.shape), glu_b=g_glu_b, attn_out_norm_g=g_ga, ssm_out_norm_g=g_gs, norm2_g=g_norm2)

    early = _Exchange([g_glu_w, g_w_out, g_w_up, g_w_down, _pack_small([small_grads[n] for n in SMALL_EARLY] + [sse])],
                      [False, False, "col", False, True])
    dqn, dkn, dv, p_glu, p_w_out, p_w_up, p_w_down, p_early = _attn_bwd(qn, kn, v, attn, lse, dattn, early)

    dproj, g_gq, g_gk = _qk_bwd(dqn, dkn, qk, dv, seg_minor(du3), gq, gk, gmat)
    g_w_in = _mm_tn(xn, dproj, "grad_w_in")
    small_grads["q_norm_g"] = g_gq.reshape(ATTN_W // HEAD_DIM, HEAD_DIM).sum(0)[None]
    small_grads["k_norm_g"] = g_gk.reshape(ATTN_W // HEAD_DIM, HEAD_DIM).sum(0)[None]
    send_sems, recv_sems, g_thru, land, token = _split_exchange_start(g_w_in)
    grad_x, g_norm1 = _in_bwd(dproj, w_in_t, xs, norm1_g + token[:1, :1], dx2)
    g_w_in, landed = _split_exchange_wait(send_sems, recv_sems, g_thru, land, g_norm1)
    me = 4 * lax.axis_index("x") + 2 * lax.axis_index("y") + lax.axis_index("c")
    shard_w = g_w_in.shape[1] // N_DEV
    own = lax.dynamic_slice(g_w_in, (0, me * shard_w), (g_w_in.shape[0], shard_w))
    p_w_in = lax.dynamic_update_slice(landed, own[None], (me, 0, 0))
    p_mid, p_late = _exchange_now([_pack_small([small_grads[n] for n in SMALL_MID]), _pack_small([g_norm1])],
                                  [True, True], "exchange_tail")

    res = {}
    for name, gp in (("w_in", p_w_in), ("glu_w", p_glu), ("w_out", p_w_out), ("w_mlp_up", p_w_up), ("w_mlp_down", p_w_down)):
        outs = _adamw(weights[name][0], mom_m[name][0], mom_v[name][0], gp, "adamw_" + name)
        res[name] = [o[None] for o in outs]
    shapes = [weights[n].shape for n in SMALL_EARLY] + [sse.shape] + [weights[n].shape for n in SMALL_MID + SMALL_LATE]
    g_small = _unpack_small(_sum_slots(jnp.concatenate([p_early, p_mid, p_late], axis=1)), shapes)
    loss = 0.5 * g_small.pop(len(SMALL_EARLY))[0, 0] / D_MODEL
    d_small, m_small, v_small = _adamw_native([weights[n] for n in SMALL], [mom_m[n] for n in SMALL],
                                              [mom_v[n] for n in SMALL], g_small)
    for i, n in enumerate(SMALL):
        res[n] = [g_small[i], d_small[i], m_small[i], v_small[i]]

    return (loss, grad_x[None], *[res[n][0] for n in order], *[res[n][1] for n in order],
            *[res[n][2] for n in order], *[res[n][3] for n in order])
```

```python
import math

import jax
import jax.numpy as jnp
from jax import lax
from jax.experimental import pallas as pl
from jax.experimental.pallas import tpu as pltpu

F32 = jnp.float32
BF16 = jnp.bfloat16

D_MODEL = 1024
ATTN_W = 512
HEAD_DIM = 64
SSM_W = 512
SSM_GROUP = 16
SSM_GROUPS = 32
SSM_STATE = 64
N_STATE = SSM_GROUPS * SSM_STATE
D_FF = 4096
EPS = 1e-6
NEG_INF = -1e30
ATTN_CHUNK = 2048
ATTN_BLOCK = 128
DILATIONS = (1, 4, 16)
N_SEG = 8
SSM_LK = 64
N_DEV = 8
LANES = 128

ADAM_LR = 0.001
ADAM_B1 = 0.9
ADAM_B2 = 0.999
ADAM_EPS = 1e-08
ADAM_WD = 0.01
ADAM_STEP = 10

VMEM_LIMIT = 56 * 1024 * 1024
GELU_C = math.sqrt(2.0 / math.pi)
MESH = pl.DeviceIdType.MESH


def _cp(sem, vmem=VMEM_LIMIT):
    return pltpu.CompilerParams(dimension_semantics=sem, vmem_limit_bytes=vmem)


def _dot(a, b):
    return jnp.dot(a, b, preferred_element_type=F32)


def _dot_tn(a, b):
    return lax.dot_general(a, b, (((0,), (0,)), ((), ())), preferred_element_type=F32)


def _group_mean(x2, gmat):
    hi = x2.astype(BF16)
    lo = (x2 - hi.astype(F32)).astype(BF16)
    return _dot(hi, gmat) + _dot(lo, gmat)


def _rms(x):
    return lax.rsqrt(jnp.mean(x * x, axis=-1, keepdims=True) + EPS)


def _rms_bwd(dy, x, g, n):
    r = _rms(x)
    gdy = dy * g
    dx = r * gdy - x * (r * r * r) * (jnp.sum(gdy * x, axis=-1, keepdims=True) / n)
    return dx, dy * (x * r)


def _gelu(y):
    t = jnp.tanh(GELU_C * (y + 0.044715 * (y * y * y)))
    return 0.5 * y * (1.0 + t), t


def _full(shape):
    nd = len(shape)
    return pl.BlockSpec(shape, lambda *_: (0,) * nd)


def _sds(shape, dtype=F32):
    return jax.ShapeDtypeStruct(shape, dtype)


def _fwd_proj(x, g1, w_in, gq, gk, gmat, tm=512):
    s = x.shape[0]

    def body(x_ref, g1_ref, w_ref, gq_ref, gk_ref, gm_ref, qk_ref, qn_ref, kn_ref, v_ref, u_ref, xn_ref):
        xv = x_ref[...]
        xnb = ((xv * _rms(xv)) * g1_ref[...]).astype(BF16)
        xn_ref[...] = xnb
        proj = _dot(xnb, w_ref[...])
        q = proj[:, :ATTN_W]
        k = proj[:, ATTN_W:2 * ATTN_W]
        qk_ref[...] = proj[:, :2 * ATTN_W]
        v_ref[...] = proj[:, 2 * ATTN_W:3 * ATTN_W]
        u_ref[...] = proj[:, 3 * ATTN_W:]
        gm = gm_ref[...]
        qn_ref[...] = (q * lax.rsqrt(_group_mean(q * q, gm) + EPS)) * gq_ref[...]
        kn_ref[...] = (k * lax.rsqrt(_group_mean(k * k, gm) + EPS)) * gk_ref[...]

    row = lambda w: pl.BlockSpec((tm, w), lambda i: (i, 0))
    return pl.pallas_call(
        body, name="fwd_proj", grid=(s // tm,),
        in_specs=[row(D_MODEL), _full((1, D_MODEL)), _full((D_MODEL, 4 * ATTN_W)), _full((1, ATTN_W)),
                  _full((1, ATTN_W)), _full((ATTN_W, ATTN_W))],
        out_specs=[row(2 * ATTN_W), row(ATTN_W), row(ATTN_W), row(ATTN_W), row(ATTN_W), row(D_MODEL)],
        out_shape=[_sds((s, 2 * ATTN_W)), _sds((s, ATTN_W)), _sds((s, ATTN_W)), _sds((s, ATTN_W)),
                   _sds((s, ATTN_W)), _sds((s, D_MODEL), BF16)],
        compiler_params=_cp(("parallel",)),
    )(x, g1, w_in, gq, gk, gmat)


def _attn_rows(t, d, nb):
    if d == 1:
        q0 = t * ATTN_BLOCK
        return (t, pl.ds(q0, ATTN_BLOCK), pl.ds(ATTN_CHUNK + q0, ATTN_BLOCK),
                pl.ds(ATTN_CHUNK - ATTN_BLOCK + q0, ATTN_BLOCK))
    r = t // nb
    b = t % nb
    return (b, pl.ds(ATTN_BLOCK * b * d + r, ATTN_BLOCK, stride=d),
            pl.ds(ATTN_CHUNK + ATTN_BLOCK * b * d + r, ATTN_BLOCK, stride=d),
            pl.ds(ATTN_CHUNK + ATTN_BLOCK * (b - 1) * d + r, ATTN_BLOCK, stride=d))


def _attn_masks():
    row = lax.broadcasted_iota(jnp.int32, (ATTN_BLOCK, LANES), 0)
    col = lax.broadcasted_iota(jnp.int32, (ATTN_BLOCK, LANES), 1)
    return row, col


NBLK = ATTN_CHUNK // ATTN_BLOCK


def _attn_bias(bias_s):
    row, col = _attn_masks()
    bias_s[:, pl.ds(0, LANES)] = jnp.where(col >= row, 0.0, NEG_INF)
    bias_s[:, pl.ds(LANES, LANES)] = jnp.where(col <= row, 0.0, NEG_INF)
    return col < HEAD_DIM


def _attn_fwd(qn, kn, v, ex=None, group=4):
    s = qn.shape[0]
    nch = s // ATTN_CHUNK
    scale = HEAD_DIM ** -0.5
    npat = len(DILATIONS)
    n_hp = ATTN_W // LANES

    def body(*refs):
        ((q_ref, kp_ref, kc_ref, vp_ref, vc_ref), (o_ref, lse_ref),
         (kk, vv, kt_s, vb_s, bias_s, m_s, a_s), hx) = _carry(ex, refs, 5, 2)
        i = pl.program_id(1)
        if ex is not None:
            @pl.when(jnp.logical_and(pl.program_id(0) == 0, i == 0))
            def _():
                ex.start(*hx)

        kk[pl.ds(0, ATTN_CHUNK), :] = kp_ref[...]
        kk[pl.ds(ATTN_CHUNK, ATTN_CHUNK), :] = kc_ref[...]
        vv[pl.ds(0, ATTN_CHUNK), :] = vp_ref[...]
        vv[pl.ds(ATTN_CHUNK, ATTN_CHUNK), :] = vc_ref[...]
        head0 = _attn_bias(bias_s)
        first_pen = jnp.where(i > 0, 0.0, NEG_INF)

        for p, d in enumerate(DILATIONS):
            nb = ATTN_CHUNK // (ATTN_BLOCK * d)

            def prep(t, d=d, nb=nb):
                _, _, crows, prows = _attn_rows(t, d, nb)
                kt_s[t, :, pl.ds(0, LANES)] = kk[prows, :].T.astype(BF16)
                kt_s[t, :, pl.ds(LANES, LANES)] = kk[crows, :].T.astype(BF16)
                vp = vv[prows, :]
                vc = vv[crows, :]
                vb_s[2 * t, pl.ds(0, ATTN_BLOCK), :] = jnp.where(head0, vp, 1.0).astype(BF16)
                vb_s[2 * t, pl.ds(ATTN_BLOCK, ATTN_BLOCK), :] = jnp.where(head0, vc, 1.0).astype(BF16)
                vb_s[2 * t + 1, pl.ds(0, ATTN_BLOCK), :] = jnp.where(head0, 1.0, vp).astype(BF16)
                vb_s[2 * t + 1, pl.ds(ATTN_BLOCK, ATTN_BLOCK), :] = jnp.where(head0, 1.0, vc).astype(BF16)

            def main(tg, p=p, d=d, nb=nb):
                st = []
                for g in range(group):
                    t = tg * group + g
                    b, qrows, _, _ = _attn_rows(t, d, nb)
                    q = q_ref[qrows, :] * scale
                    for h in range(2):
                        hm = head0 if h == 0 else jnp.logical_not(head0)
                        st.append(dict(t=t, b=b, qrows=qrows, sc=_dot(jnp.where(hm, q, 0.0).astype(BF16), kt_s[t])))
                for e in st:
                    sc = e["sc"] + bias_s[...]
                    s_p = sc[:, :LANES] + first_pen if e["b"] == 0 else sc[:, :LANES]
                    s_c = sc[:, LANES:]
                    m = jnp.max(jnp.maximum(s_p, s_c), axis=-1, keepdims=True)
                    e["eb"] = jnp.concatenate([jnp.exp(s_p - m), jnp.exp(s_c - m)], axis=1).astype(BF16)
                    e["m"] = jnp.broadcast_to(m, (ATTN_BLOCK, LANES))
                for g in range(group):
                    e0, e1 = st[2 * g], st[2 * g + 1]
                    t = e0["t"]
                    m_s[p, e0["qrows"], :] = jnp.where(head0, e0["m"], e1["m"])
                    a_s[2 * p, e0["qrows"], :] = _dot(e0["eb"], vb_s[2 * t])
                    a_s[2 * p + 1, e0["qrows"], :] = _dot(e1["eb"], vb_s[2 * t + 1])

            for g in range(group):
                prep(g)
            for tg in range(NBLK // group):
                if tg + 1 < NBLK // group:
                    for g in range(group):
                        prep((tg + 1) * group + g)
                main(tg)

        def merge(t, carry):
            rows = pl.ds(pl.multiple_of(t * ATTN_BLOCK, ATTN_BLOCK), ATTN_BLOCK)
            m_all = m_s[0, rows, :]
            for p in range(1, npat):
                m_all = jnp.maximum(m_all, m_s[p, rows, :])
            num = jnp.zeros((ATTN_BLOCK, LANES), F32)
            den = jnp.zeros((ATTN_BLOCK, LANES), F32)
            for p in range(npat):
                w = jnp.exp(m_s[p, rows, :] - m_all)
                a0, a1 = a_s[2 * p, rows, :], a_s[2 * p + 1, rows, :]
                num = num + jnp.where(head0, a0, a1) * w
                den = den + pltpu.roll(jnp.where(head0, a1, a0), HEAD_DIM, 1) * w
            o_ref[rows, :] = num / den
            lse_ref[rows, :] = m_all + jnp.log(den)
            return carry

        lax.fori_loop(0, NBLK, merge, 0, unroll=2)
        if ex is not None:
            @pl.when(jnp.logical_and(pl.program_id(0) == n_hp - 1, i == nch - 1))
            def _():
                ex.wait(*hx)

    cur = pl.BlockSpec((ATTN_CHUNK, LANES), lambda h, i: (i, h))
    prev = pl.BlockSpec((ATTN_CHUNK, LANES), lambda h, i: (jnp.maximum(i - 1, 0), h))
    hosted = ex is not None
    return pl.pallas_call(
        body, name="attn_fwd", grid=(n_hp, nch),
        in_specs=[cur, prev, cur, prev, cur] + (ex.specs() if hosted else []),
        out_specs=[cur, cur] + (ex.specs() if hosted else []),
        out_shape=[_sds((s, ATTN_W)), _sds((s, ATTN_W))] + (ex.out_shape() if hosted else []),
        scratch_shapes=[pltpu.VMEM((2 * ATTN_CHUNK, LANES), F32), pltpu.VMEM((2 * ATTN_CHUNK, LANES), F32),
                        pltpu.VMEM((NBLK, LANES, 2 * LANES), BF16), pltpu.VMEM((2 * NBLK, 2 * ATTN_BLOCK, LANES), BF16),
                        pltpu.VMEM((ATTN_BLOCK, 2 * LANES), F32),
                        pltpu.VMEM((npat, ATTN_CHUNK, LANES), F32), pltpu.VMEM((2 * npat, ATTN_CHUNK, LANES), F32)]
        + (ex.scratch() if hosted else []),
        compiler_params=_cp(("arbitrary", "arbitrary")),
    )(qn, kn, kn, v, v, *(ex.srcs if hosted else []))


def _attn_bwd(qn, kn, v, o, lse, do, ex=None, group=4):
    s = qn.shape[0]
    nch = s // ATTN_CHUNK
    scale = HEAD_DIM ** -0.5
    npat = len(DILATIONS)
    n_hp = ATTN_W // LANES

    def body(*refs):
        ((q_ref, kp_ref, kc_ref, vp_ref, vc_ref, o_ref, lse_ref, do_ref), (dq_ref, dk_ref, dv_ref),
         (kk, vv, dkk, dvv, kt_s, vt_s, kn_s, bias_s, dq_s, dl_s, dkb, dvb), hx) = _carry(ex, refs, 8, 3)
        step = pl.program_id(1)
        i = nch - 1 - step
        if ex is not None:
            @pl.when(jnp.logical_and(pl.program_id(0) == 0, step == 0))
            def _():
                ex.start(*hx)

        kk[pl.ds(0, ATTN_CHUNK), :] = kp_ref[...]
        kk[pl.ds(ATTN_CHUNK, ATTN_CHUNK), :] = kc_ref[...]
        vv[pl.ds(0, ATTN_CHUNK), :] = vp_ref[...]
        vv[pl.ds(ATTN_CHUNK, ATTN_CHUNK), :] = vc_ref[...]

        @pl.when(step == 0)
        def _():
            dkk[pl.ds(ATTN_CHUNK, ATTN_CHUNK), :] = jnp.zeros((ATTN_CHUNK, LANES), F32)
            dvv[pl.ds(ATTN_CHUNK, ATTN_CHUNK), :] = jnp.zeros((ATTN_CHUNK, LANES), F32)

        @pl.when(step > 0)
        def _():
            dkk[pl.ds(ATTN_CHUNK, ATTN_CHUNK), :] = dkk[pl.ds(0, ATTN_CHUNK), :]
            dvv[pl.ds(ATTN_CHUNK, ATTN_CHUNK), :] = dvv[pl.ds(0, ATTN_CHUNK), :]

        dkk[pl.ds(0, ATTN_CHUNK), :] = jnp.zeros((ATTN_CHUNK, LANES), F32)
        dvv[pl.ds(0, ATTN_CHUNK), :] = jnp.zeros((ATTN_CHUNK, LANES), F32)
        head0 = _attn_bias(bias_s)

        def delta(t, carry):
            rows = pl.ds(pl.multiple_of(t * ATTN_BLOCK, ATTN_BLOCK), ATTN_BLOCK)
            prod = do_ref[rows, :] * o_ref[rows, :]
            d0 = jnp.sum(jnp.where(head0, prod, 0.0), axis=-1, keepdims=True)
            d1 = jnp.sum(jnp.where(head0, 0.0, prod), axis=-1, keepdims=True)
            dl_s[rows, :] = jnp.where(head0, d0, d1)
            return carry

        lax.fori_loop(0, NBLK, delta, 0, unroll=2)

        first_pen = jnp.where(i > 0, 0.0, NEG_INF)

        for p, d in enumerate(DILATIONS):
            nb = ATTN_CHUNK // (ATTN_BLOCK * d)

            def prep(t, d=d, nb=nb):
                _, _, crows, prows = _attn_rows(t, d, nb)
                kp, kc = kk[prows, :], kk[crows, :]
                kt_s[t, :, pl.ds(0, LANES)] = kp.T.astype(BF16)
                kt_s[t, :, pl.ds(LANES, LANES)] = kc.T.astype(BF16)
                kn_s[t, pl.ds(0, ATTN_BLOCK), :] = (kp * scale).astype(BF16)
                kn_s[t, pl.ds(ATTN_BLOCK, ATTN_BLOCK), :] = (kc * scale).astype(BF16)
                vt_s[t, :, pl.ds(0, LANES)] = vv[prows, :].T.astype(BF16)
                vt_s[t, :, pl.ds(LANES, LANES)] = vv[crows, :].T.astype(BF16)

            def main(tg, p=p, d=d, nb=nb):
                st = []
                for g in range(group):
                    t = tg * group + g
                    b, qrows, _, _ = _attn_rows(t, d, nb)
                    q = q_ref[qrows, :] * scale
                    dout = do_ref[qrows, :]
                    lse_b = lse_ref[qrows, :]
                    dl_b = dl_s[qrows, :]
                    for h in range(2):
                        hm = head0 if h == 0 else jnp.logical_not(head0)
                        c0 = h * HEAD_DIM
                        qh = jnp.where(hm, q, 0.0).astype(BF16)
                        doh = jnp.where(hm, dout, 0.0).astype(BF16)
                        st.append(dict(t=t, b=b, qrows=qrows, qh=qh, doh=doh, lse=lse_b[:, c0:c0 + 1],
                                       dl=dl_b[:, c0:c0 + 1], sc=_dot(qh, kt_s[t]), dp=_dot(doh, vt_s[t])))
                for e in st:
                    sc = e["sc"] + bias_s[...]
                    if e["b"] == 0:
                        sc = jnp.concatenate([sc[:, :LANES] + first_pen, sc[:, LANES:]], axis=1)
                    pr = jnp.exp(sc - e["lse"])
                    e["ds"] = (pr * (e["dp"] - e["dl"])).astype(BF16)
                    e["pr"] = pr.astype(BF16)
                for g in range(group):
                    e0, e1 = st[2 * g], st[2 * g + 1]
                    t = e0["t"]
                    dq_s[p, e0["qrows"], :] = jnp.where(head0, _dot(e0["ds"], kn_s[t]), _dot(e1["ds"], kn_s[t]))
                    dkb[t] = _dot_tn(e0["ds"], e0["qh"]) + _dot_tn(e1["ds"], e1["qh"])
                    dvb[t] = _dot_tn(e0["pr"], e0["doh"]) + _dot_tn(e1["pr"], e1["doh"])

            def scatter(t, d=d, nb=nb):
                _, _, crows, prows = _attn_rows(t, d, nb)
                dkk[prows, :] = dkk[prows, :] + dkb[t, pl.ds(0, ATTN_BLOCK), :]
                dkk[crows, :] = dkk[crows, :] + dkb[t, pl.ds(ATTN_BLOCK, ATTN_BLOCK), :]
                dvv[prows, :] = dvv[prows, :] + dvb[t, pl.ds(0, ATTN_BLOCK), :]
                dvv[crows, :] = dvv[crows, :] + dvb[t, pl.ds(ATTN_BLOCK, ATTN_BLOCK), :]

            n_groups = NBLK // group
            for g in range(group):
                prep(g)
            for tg in range(n_groups):
                if tg + 1 < n_groups:
                    for g in range(group):
                        prep((tg + 1) * group + g)
                main(tg)
                if tg >= 1:
                    for g in range(group):
                        scatter((tg - 1) * group + g)
            for g in range(group):
                scatter((n_groups - 1) * group + g)

        def finish(t, carry):
            rows = pl.ds(pl.multiple_of(t * ATTN_BLOCK, ATTN_BLOCK), ATTN_BLOCK)
            acc = dq_s[0, rows, :]
            for p in range(1, npat):
                acc = acc + dq_s[p, rows, :]
            dq_ref[rows, :] = acc
            return carry

        lax.fori_loop(0, NBLK, finish, 0, unroll=2)
        dk_ref[...] = dkk[pl.ds(ATTN_CHUNK, ATTN_CHUNK), :]
        dv_ref[...] = dvv[pl.ds(ATTN_CHUNK, ATTN_CHUNK), :]
        if ex is not None:
            @pl.when(jnp.logical_and(pl.program_id(0) == n_hp - 1, step == nch - 1))
            def _():
                ex.wait(*hx)

    cur = pl.BlockSpec((ATTN_CHUNK, LANES), lambda h, t: (nch - 1 - t, h))
    prev = pl.BlockSpec((ATTN_CHUNK, LANES), lambda h, t: (jnp.maximum(nch - 2 - t, 0), h))
    big = pltpu.VMEM((2 * ATTN_CHUNK, LANES), F32)
    pair_t = pltpu.VMEM((NBLK, LANES, 2 * LANES), BF16)
    hosted = ex is not None
    return pl.pallas_call(
        body, name="attn_bwd", grid=(n_hp, nch),
        in_specs=[cur, prev, cur, prev, cur, cur, cur, cur] + (ex.specs() if hosted else []),
        out_specs=[cur, cur, cur] + (ex.specs() if hosted else []),
        out_shape=[_sds((s, ATTN_W))] * 3 + (ex.out_shape() if hosted else []),
        scratch_shapes=[big, big, big, big, pair_t, pair_t, pltpu.VMEM((NBLK, 2 * ATTN_BLOCK, LANES), BF16),
                        pltpu.VMEM((ATTN_BLOCK, 2 * LANES), F32),
                        pltpu.VMEM((npat, ATTN_CHUNK, LANES), F32), pltpu.VMEM((ATTN_CHUNK, LANES), F32),
                        pltpu.VMEM((NBLK, 2 * ATTN_BLOCK, LANES), F32), pltpu.VMEM((NBLK, 2 * ATTN_BLOCK, LANES), F32)]
        + (ex.scratch() if hosted else []),
        compiler_params=_cp(("arbitrary", "arbitrary")),
    )(qn, kn, kn, v, v, o, lse, do, *(ex.srcs if hosted else []))


def _discretize(lr, li, dt):
    mag = jnp.exp(lr * dt)
    abr = mag * jnp.cos(li * dt)
    abi = mag * jnp.sin(li * dt)
    den = lr * lr + li * li
    nr, ni = abr - 1.0, abi
    cr = (nr * lr + ni * li) / den
    ci = (ni * lr - nr * li) / den
    return abr, abi, den, nr, ni, cr, ci


def _ssm_discretize(a_re, a_im, log_dt, b_re_t, b_im_t):
    def body(ar_ref, ai_ref, ldt_ref, br_ref, bi_ref, abr_ref, abi_ref, bbr_ref, bbi_ref):
        abr, abi, _, _, _, cr, ci = _discretize(ar_ref[...], ai_ref[...], jnp.exp(ldt_ref[...]))
        br, bi = br_ref[...], bi_ref[...]
        abr_ref[...] = abr
        abi_ref[...] = abi
        bbr_ref[...] = cr * br - ci * bi
        bbi_ref[...] = cr * bi + ci * br

    return pl.pallas_call(
        body, name="ssm_discretize",
        out_shape=[_sds(a_re.shape)] * 2 + [_sds(b_re_t.shape)] * 2,
    )(a_re, a_im, log_dt, b_re_t, b_im_t)


def _ssm_discretize_bwd(a_re, a_im, log_dt, b_re_t, b_im_t, dabr, dabi, dbbr, dbbi):
    def body(ar_ref, ai_ref, ldt_ref, br_ref, bi_ref, dabr_ref, dabi_ref, dbbr_ref, dbbi_ref,
             dar_ref, dai_ref, dldt_ref, dbr_ref, dbi_ref):
        lr, li = ar_ref[...], ai_ref[...]
        dt = jnp.exp(ldt_ref[...])
        abr, abi, den, nr, ni, cr, ci = _discretize(lr, li, dt)
        br, bi = br_ref[...], bi_ref[...]
        gbr, gbi = dbbr_ref[...], dbbi_ref[...]
        dcr = jnp.sum(gbr * br + gbi * bi, axis=1, keepdims=True)
        dci = jnp.sum(gbi * br - gbr * bi, axis=1, keepdims=True)
        dbr_ref[...] = cr * gbr + ci * gbi
        dbi_ref[...] = cr * gbi - ci * gbr
        dnr = (dcr * lr - dci * li) / den
        dni = (dcr * li + dci * lr) / den
        dden = -(dcr * cr + dci * ci) / den
        dlr = (dcr * nr + dci * ni) / den + dden * 2.0 * lr
        dli = (dcr * ni - dci * nr) / den + dden * 2.0 * li
        gabr = dabr_ref[...] + dnr
        gabi = dabi_ref[...] + dni
        dphi = gabr * abr + gabi * abi
        dth = gabi * abr - gabr * abi
        dar_ref[...] = dlr + dphi * dt
        dai_ref[...] = dli + dth * dt
        dldt_ref[...] = jnp.sum(dphi * lr + dth * li, axis=2, keepdims=True) * dt

    return pl.pallas_call(
        body, name="ssm_discretize_bwd",
        out_shape=[_sds(a_re.shape)] * 2 + [_sds(log_dt.shape)] + [_sds(b_re_t.shape)] * 2,
    )(a_re, a_im, log_dt, b_re_t, b_im_t, dabr, dabi, dbbr, dbbi)


def _ssm_power(abr, abi, n_sq):
    def body(r_ref, i_ref, or_ref, oi_ref):
        r, i = r_ref[...], i_ref[...]
        for _ in range(n_sq):
            r, i = r * r - i * i, 2.0 * r * i
        or_ref[...] = r
        oi_ref[...] = i

    return pl.pallas_call(body, name="ssm_power", out_shape=[_sds(abr.shape)] * 2)(abr, abi)


N_CB = SSM_W // LANES
CB_STATES = N_STATE // N_CB
ROWS = N_SEG * SSM_LK


class _Neg:
    def __init__(self, ref):
        self.ref = ref

    def __getitem__(self, idx):
        return -self.ref[idx]


def _seg_init(fin_r, fin_i, pw_r, pw_i, x_r, x_i, reverse):
    zero = jnp.zeros((1, N_STATE), F32)
    cr, ci = zero, zero
    order = range(N_SEG - 1, -1, -1) if reverse else range(N_SEG)
    pr = pw_r[...]
    pi = -pw_i[...] if reverse else pw_i[...]
    for j in order:
        x_r[pl.ds(j, 1), :] = cr
        x_i[pl.ds(j, 1), :] = ci
        fr, fi = fin_r[pl.ds(j, 1), :], fin_i[pl.ds(j, 1), :]
        cr, ci = fr + pr * cr - pi * ci, fi + pr * ci + pi * cr


def _permute_in(src_ref, dst):
    for c in range(N_CB):
        dst[c] = src_ref[:, :, pl.ds(c * LANES, LANES)].reshape(ROWS, LANES)


def _permute_out(src, dst_ref):
    for c in range(N_CB):
        dst_ref[:, :, pl.ds(c * LANES, LANES)] = src[c].reshape(SSM_LK, N_SEG, LANES)


def _scan_block(a_r, a_i, c, b_r, b_i, b_off, x_r, x_i, reverse, acc=None):
    cols = pl.ds(c * CB_STATES, CB_STATES)
    ar = jnp.broadcast_to(a_r[:, cols], (N_SEG, CB_STATES))
    ai = jnp.broadcast_to(a_i[:, cols], (N_SEG, CB_STATES))
    xr, xi = x_r[:, cols], x_i[:, cols]
    if acc is not None:
        sr = jnp.zeros((N_SEG, CB_STATES), F32)
        si = jnp.zeros((N_SEG, CB_STATES), F32)
    for t in range(SSM_LK):
        k = (SSM_LK - 1 - t) if reverse else t
        rows = pl.ds(k * N_SEG + b_off, N_SEG)
        xr, xi = ar * xr - ai * xi + b_r[rows, :], ar * xi + ai * xr + b_i[rows, :]
        b_r[rows, :] = xr
        b_i[rows, :] = xi
        if acc is not None:
            pr, pi = acc[0][pl.ds(k * N_SEG, N_SEG), :], acc[1][pl.ds(k * N_SEG, N_SEG), :]
            sr = sr + xr * pr + xi * pi
            si = si + xi * pr - xr * pi
    x_r[:, cols] = xr
    x_i[:, cols] = xi
    if acc is not None:
        acc[2][:, cols] += sr
        acc[3][:, cols] += si


def _ssm_fwd(u3, abr, abi, pw_r, pw_i, fin_r, fin_i, bb_r, bb_i, cc_r, cc_i, dskip, finals_only):
    sl = u3.shape[0]
    nch = sl // SSM_LK

    def body(u_ref, abr_ref, abi_ref, pwr_ref, pwi_ref, finr_ref, fini_ref, bbr_ref, bbi_ref,
             ccr_ref, cci_ref, d_ref, *rest):
        if finals_only:
            xfr_ref, xfi_ref, up, x_r, x_i = rest[:5]
        else:
            y_ref, xsr_ref, xsi_ref, up, yp, x_r, x_i = rest[:7]
        xs_r, xs_i = rest[-2 * N_CB:-N_CB], rest[-N_CB:]
        k = pl.program_id(0)

        @pl.when(k == 0)
        def _():
            _seg_init(finr_ref, fini_ref, pwr_ref, pwi_ref, x_r, x_i, False)

        if not finals_only:
            xsr_ref[0] = x_r[...]
            xsi_ref[0] = x_i[...]
        _permute_in(u_ref, up)

        def drive(c):
            lhs = up[c].astype(BF16)
            xs_r[c][...] = _dot(lhs, bbr_ref[c])
            xs_i[c][...] = _dot(lhs, bbi_ref[c])

        def readout(c):
            yp[c] = (_dot(xs_r[c][...].astype(BF16), ccr_ref[c]) - _dot(xs_i[c][...].astype(BF16), cci_ref[c])
                     + d_ref[:, pl.ds(c * LANES, LANES)] * up[c])

        drive(0)
        for c in range(N_CB):
            if c + 1 < N_CB:
                drive(c + 1)
            if c >= 1 and not finals_only:
                readout(c - 1)
            _scan_block(abr_ref, abi_ref, c, xs_r[c], xs_i[c], 0, x_r, x_i, False)
        if finals_only:
            @pl.when(k == nch - 1)
            def _():
                xfr_ref[...] = x_r[...]
                xfi_ref[...] = x_i[...]
        else:
            readout(N_CB - 1)
            _permute_out(yp, y_ref)

    ublk = pl.BlockSpec((SSM_LK, N_SEG, SSM_W), lambda k: (k, 0, 0))
    st = pl.BlockSpec((1, N_SEG, N_STATE), lambda k: (k, 0, 0))
    vec = _full((1, N_STATE))
    mat = _full((N_SEG, N_STATE))
    chunk = pltpu.VMEM((N_CB, ROWS, LANES), F32)
    blocks = [pltpu.VMEM((ROWS, CB_STATES), F32)] * (2 * N_CB)
    small = pltpu.VMEM((N_SEG, N_STATE), F32)
    if finals_only:
        out_specs, out_shape = [mat, mat], [_sds((N_SEG, N_STATE))] * 2
        scratch, name = [chunk, small, small] + blocks, "ssm_fwd_finals"
    else:
        out_specs = [ublk, st, st]
        out_shape = [_sds(u3.shape)] + [_sds((nch, N_SEG, N_STATE))] * 2
        scratch, name = [chunk, chunk, small, small] + blocks, "ssm_fwd"
    return pl.pallas_call(
        body, name=name, grid=(nch,),
        in_specs=[ublk, vec, vec, vec, vec, mat, mat,
                  _full((N_CB, LANES, CB_STATES)), _full((N_CB, LANES, CB_STATES)),
                  _full((N_CB, CB_STATES, LANES)), _full((N_CB, CB_STATES, LANES)), _full((1, SSM_W))],
        out_specs=out_specs, out_shape=out_shape, scratch_shapes=scratch,
        compiler_params=_cp(("arbitrary",)),
    )(u3, abr, abi, pw_r, pw_i, fin_r, fin_i, bb_r, bb_i, cc_r, cc_i, dskip)


def _ssm_bwd(u3, dy3, xst_r, xst_i, abr, abi, pw_r, pw_i, fin_r, fin_i, bb_r, bb_i, bbt_r, bbt_i,
             cct_r, cct_i, dskip, finals_only):
    sl = u3.shape[0]
    nch = sl // SSM_LK

    def body(u_ref, g_ref, xsr_ref, xsi_ref, abr_ref, abi_ref, pwr_ref, pwi_ref,
             finr_ref, fini_ref, bbr_ref, bbi_ref, btr_ref, bti_ref, ctr_ref, cti_ref, d_ref, *rest):
        if finals_only:
            lfr_ref, lfi_ref, gp, lam_r, lam_i = rest[:5]
            l_r, l_i = rest[-2 * N_CB:-N_CB], rest[-N_CB:]
        else:
            (du_ref, dar_ref, dai_ref, dbr_ref, dbi_ref, dcr_ref, dci_ref, dd_ref,
             gp, up, yp, lam_r, lam_i, x_r, x_i, sar, sai, sdd) = rest[:18]
            l_r, l_i = rest[18:18 + N_CB], rest[18 + N_CB:18 + 2 * N_CB]
            xx_r, xx_i = rest[18 + 2 * N_CB:18 + 3 * N_CB], rest[18 + 3 * N_CB:]
        t = pl.program_id(0)

        @pl.when(t == 0)
        def _():
            _seg_init(finr_ref, fini_ref, pwr_ref, pwi_ref, lam_r, lam_i, True)
            if not finals_only:
                sar[...] = jnp.zeros_like(sar)
                sai[...] = jnp.zeros_like(sai)
                sdd[...] = jnp.zeros_like(sdd)
                dbr_ref[...] = jnp.zeros_like(dbr_ref)
                dbi_ref[...] = jnp.zeros_like(dbi_ref)
                dcr_ref[...] = jnp.zeros_like(dcr_ref)
                dci_ref[...] = jnp.zeros_like(dci_ref)

        _permute_in(g_ref, gp)
        if not finals_only:
            _permute_in(u_ref, up)
            x_r[...] = xsr_ref[0]
            x_i[...] = xsi_ref[0]

        def drive(c):
            lhs = gp[c].astype(BF16)
            l_r[c][...] = _dot(lhs, ctr_ref[c])
            l_i[c][...] = -_dot(lhs, cti_ref[c])
            if not finals_only:
                cols = pl.ds(c * CB_STATES, CB_STATES)
                xx_r[c][pl.ds(0, N_SEG), :] = x_r[:, cols]
                xx_i[c][pl.ds(0, N_SEG), :] = x_i[:, cols]
                ub = up[c].astype(BF16)
                xx_r[c][pl.ds(N_SEG, ROWS), :] = _dot(ub, bbr_ref[c])
                xx_i[c][pl.ds(N_SEG, ROWS), :] = _dot(ub, bbi_ref[c])

        def collect(c):
            lrb = l_r[c][...].astype(BF16)
            lib = l_i[c][...].astype(BF16)
            ub = up[c].astype(BF16)
            gb = gp[c].astype(BF16)
            dbr_ref[c] += _dot_tn(lrb, ub)
            dbi_ref[c] += _dot_tn(lib, ub)
            dcr_ref[c] += _dot_tn(gb, xx_r[c][pl.ds(N_SEG, ROWS), :].astype(BF16))
            dci_ref[c] += -_dot_tn(gb, xx_i[c][pl.ds(N_SEG, ROWS), :].astype(BF16))
            yp[c] = _dot(lrb, btr_ref[c]) + _dot(lib, bti_ref[c]) + d_ref[:, pl.ds(c * LANES, LANES)] * gp[c]
            prod = gp[c] * up[c]
            sdd[:, pl.ds(c * LANES, LANES)] += jnp.sum(prod.reshape(SSM_LK, N_SEG, LANES), axis=0)

        drive(0)
        for c in range(N_CB):
            if c + 1 < N_CB:
                drive(c + 1)
            if finals_only:
                _scan_block(abr_ref, _Neg(abi_ref), c, l_r[c], l_i[c], 0, lam_r, lam_i, True)
            else:
                if c >= 1:
                    collect(c - 1)
                _scan_block(abr_ref, abi_ref, c, xx_r[c], xx_i[c], N_SEG, x_r, x_i, False)
                _scan_block(abr_ref, _Neg(abi_ref), c, l_r[c], l_i[c], 0, lam_r, lam_i, True,
                            acc=(xx_r[c], xx_i[c], sar, sai))
        if finals_only:
            @pl.when(t == nch - 1)
            def _():
                lfr_ref[...] = lam_r[...]
                lfi_ref[...] = lam_i[...]
        else:
            collect(N_CB - 1)
            _permute_out(yp, du_ref)

            @pl.when(t == nch - 1)
            def _():
                dar_ref[...] = jnp.sum(sar[...], axis=0, keepdims=True)
                dai_ref[...] = jnp.sum(sai[...], axis=0, keepdims=True)
                dd_ref[...] = jnp.sum(sdd[...], axis=0, keepdims=True)

    ublk = pl.BlockSpec((SSM_LK, N_SEG, SSM_W), lambda t: (nch - 1 - t, 0, 0))
    st = pl.BlockSpec((1, N_SEG, N_STATE), lambda t: (nch - 1 - t, 0, 0))
    vec = _full((1, N_STATE))
    mat = _full((N_SEG, N_STATE))
    cs = _full((N_CB, LANES, CB_STATES))
    sc = _full((N_CB, CB_STATES, LANES))
    in_specs = [ublk, ublk, st, st, vec, vec, vec, vec, mat, mat, cs, cs, sc, sc, cs, cs, _full((1, SSM_W))]
    chunk = pltpu.VMEM((N_CB, ROWS, LANES), F32)
    blocks = [pltpu.VMEM((ROWS, CB_STATES), F32)] * (2 * N_CB)
    small = pltpu.VMEM((N_SEG, N_STATE), F32)
    if finals_only:
        out_specs, out_shape = [mat, mat], [_sds((N_SEG, N_STATE))] * 2
        scratch, name = [chunk, small, small] + blocks, "ssm_bwd_finals"
    else:
        out_specs = [ublk, vec, vec, sc, sc, cs, cs, _full((1, SSM_W))]
        out_shape = ([_sds(u3.shape), _sds((1, N_STATE)), _sds((1, N_STATE))]
                     + [_sds((N_CB, CB_STATES, LANES))] * 2 + [_sds((N_CB, LANES, CB_STATES))] * 2
                     + [_sds((1, SSM_W))])
        scratch = ([chunk, chunk, chunk, small, small, small, small, small, small, pltpu.VMEM((N_SEG, SSM_W), F32)]
                   + blocks + [pltpu.VMEM((ROWS + N_SEG, CB_STATES), F32)] * (2 * N_CB))
        name = "ssm_bwd"
    return pl.pallas_call(
        body, name=name, grid=(nch,), in_specs=in_specs, out_specs=out_specs, out_shape=out_shape,
        scratch_shapes=scratch, compiler_params=_cp(("arbitrary",)),
    )(u3, dy3, xst_r, xst_i, abr, abi, pw_r, pw_i, fin_r, fin_i, bb_r, bb_i, bbt_r, bbt_i, cct_r, cct_i, dskip)


def _row(tm, w):
    return pl.BlockSpec((tm, w), lambda i: (i, 0))


def _acc_rows(ref, rows, first):
    @pl.when(first)
    def _():
        ref[...] = jnp.zeros_like(ref)

    ref[...] += jnp.sum(rows, axis=0, keepdims=True)


def _fwd_mix(attn, y, x, glu_w, glu_b, ga, gs, w_out, g2, tm=1024):
    s = x.shape[0]

    def body(a_ref, y_ref, x_ref, gw_ref, gb_ref, ga_ref, gs_ref, wo_ref, g2_ref, mix_ref, x2_ref, h_ref):
        a = a_ref[...]
        anb = ((a * _rms(a)) * ga_ref[...]).astype(BF16)
        z, _ = _gelu(y_ref[...])
        so = z * jax.nn.sigmoid(_dot(z.astype(BF16), gw_ref[...]) + gb_ref[...])
        snb = ((so * _rms(so)) * gs_ref[...]).astype(BF16)
        mix_ref[:, pl.ds(0, ATTN_W)] = anb
        mix_ref[:, pl.ds(ATTN_W, SSM_W)] = snb
        x2 = x_ref[...] + (_dot(anb, wo_ref[pl.ds(0, ATTN_W), :]) + _dot(snb, wo_ref[pl.ds(ATTN_W, SSM_W), :]))
        x2_ref[...] = x2
        h_ref[...] = ((x2 * _rms(x2)) * g2_ref[...]).astype(BF16)

    return pl.pallas_call(
        body, name="fwd_mix", grid=(s // tm,),
        in_specs=[_row(tm, ATTN_W), _row(tm, SSM_W), _row(tm, D_MODEL), _full((SSM_W, SSM_W)), _full((1, SSM_W)),
                  _full((1, ATTN_W)), _full((1, SSM_W)), _full((D_MODEL, D_MODEL)), _full((1, D_MODEL))],
        out_specs=[_row(tm, D_MODEL), _row(tm, D_MODEL), _row(tm, D_MODEL)],
        out_shape=[_sds((s, D_MODEL), BF16), _sds((s, D_MODEL)), _sds((s, D_MODEL), BF16)],
        compiler_params=_cp(("parallel",)),
    )(attn, y, x, glu_w, glu_b, ga, gs, w_out, g2)


def _mlp_up(h, w_up, tm=512, bn=1024):
    s = h.shape[0]

    def body(h_ref, w_ref, r_ref, hdn_ref):
        hv = h_ref[...]
        for j in range(D_FF // bn):
            cols = pl.ds(j * bn, bn)
            r = jnp.maximum(_dot(hv, w_ref[:, cols]), 0.0)
            r_ref[:, cols] = r.astype(BF16)
            hdn_ref[:, cols] = (r * r).astype(BF16)

    return pl.pallas_call(
        body, name="mlp_up", grid=(s // tm,),
        in_specs=[_row(tm, D_MODEL), _full((D_MODEL, D_FF))],
        out_specs=[_row(tm, D_FF), _row(tm, D_FF)], out_shape=[_sds((s, D_FF), BF16)] * 2,
        compiler_params=_cp(("parallel",)),
    )(h, w_up)


def _mlp_down_loss(hdn, w_down, x2, tgt, tm=512):
    s = x2.shape[0]

    def body(hdn_ref, w_ref, x2_ref, t_ref, dy_ref, dyb_ref, sse_ref):
        err = (x2_ref[...] + _dot(hdn_ref[...], w_ref[...])) - t_ref[...]
        dy = err * (1.0 / D_MODEL)
        dy_ref[...] = dy
        dyb_ref[...] = dy.astype(BF16)

        @pl.when(pl.program_id(0) == 0)
        def _():
            sse_ref[...] = jnp.zeros_like(sse_ref)

        sse_ref[...] += jnp.sum(jnp.sum(err * err, axis=0, keepdims=True), axis=1, keepdims=True)

    return pl.pallas_call(
        body, name="mlp_down_loss", grid=(s // tm,),
        in_specs=[_row(tm, D_FF), _full((D_FF, D_MODEL)), _row(tm, D_MODEL), _row(tm, D_MODEL)],
        out_specs=[_row(tm, D_MODEL), _row(tm, D_MODEL), _full((1, 1))],
        out_shape=[_sds((s, D_MODEL)), _sds((s, D_MODEL), BF16), _sds((1, 1))],
        compiler_params=_cp(("arbitrary",)),
    )(hdn, w_down, x2, tgt)


def _mlp_down_bwd(dyb, w_down_t, r, tm=512, bn=1024):
    s = dyb.shape[0]

    def body(dy_ref, w_ref, r_ref, dup_ref):
        dyv = dy_ref[...]
        for j in range(D_FF // bn):
            cols = pl.ds(j * bn, bn)
            dup_ref[:, cols] = (_dot(dyv, w_ref[:, cols]) * (2.0 * r_ref[:, cols].astype(F32))).astype(BF16)

    return pl.pallas_call(
        body, name="mlp_down_bwd", grid=(s // tm,),
        in_specs=[_row(tm, D_MODEL), _full((D_MODEL, D_FF)), _row(tm, D_FF)],
        out_specs=_row(tm, D_FF), out_shape=_sds((s, D_FF), BF16),
        compiler_params=_cp(("parallel",)),
    )(dyb, w_down_t, r)


def _mlp_up_bwd(dup, w_up_t, x2, g2, dy, tm=512):
    s = x2.shape[0]

    def body(dup_ref, w_ref, x2_ref, g2_ref, dy_ref, dx2_ref, dx2b_ref, dg_ref):
        dx, dg_rows = _rms_bwd(_dot(dup_ref[...], w_ref[...]), x2_ref[...], g2_ref[...], D_MODEL)
        dx2 = dy_ref[...] + dx
        dx2_ref[...] = dx2
        dx2b_ref[...] = dx2.astype(BF16)
        _acc_rows(dg_ref, dg_rows, pl.program_id(0) == 0)

    return pl.pallas_call(
        body, name="mlp_up_bwd", grid=(s // tm,),
        in_specs=[_row(tm, D_FF), _full((D_FF, D_MODEL)), _row(tm, D_MODEL), _full((1, D_MODEL)), _row(tm, D_MODEL)],
        out_specs=[_row(tm, D_MODEL), _row(tm, D_MODEL), _full((1, D_MODEL))],
        out_shape=[_sds((s, D_MODEL)), _sds((s, D_MODEL), BF16), _sds((1, D_MODEL))],
        compiler_params=_cp(("arbitrary",)),
    )(dup, w_up_t, x2, g2, dy)


def _mix_bwd(dx2b, w_out_t, attn, y, glu_w, glu_b, glu_w_t, ga, gs, tm=1024):
    s = attn.shape[0]

    def body(dx2_ref, wot_ref, a_ref, y_ref, gw_ref, gb_ref, gwt_ref, ga_ref, gs_ref,
             da_ref, dys_ref, z_ref, dpre_ref, dga_ref, dgs_ref, dgb_ref):
        first = pl.program_id(0) == 0
        dmix = _dot(dx2_ref[...], wot_ref[...])
        da, dga_rows = _rms_bwd(dmix[:, :ATTN_W], a_ref[...], ga_ref[...], ATTN_W)
        da_ref[...] = da
        yv = y_ref[...]
        z, t = _gelu(yv)
        gate = jax.nn.sigmoid(_dot(z.astype(BF16), gw_ref[...]) + gb_ref[...])
        dso, dgs_rows = _rms_bwd(dmix[:, ATTN_W:], z * gate, gs_ref[...], SSM_W)
        dpre = dso * z * gate * (1.0 - gate)
        dpre_b = dpre.astype(BF16)
        dz = dso * gate + _dot(dpre_b, gwt_ref[...])
        dgelu = 0.5 * (1.0 + t) + 0.5 * yv * (1.0 - t * t) * (GELU_C * (1.0 + 3.0 * 0.044715 * (yv * yv)))
        dys_ref[...] = dz * dgelu
        z_ref[...] = z.astype(BF16)
        dpre_ref[...] = dpre_b
        _acc_rows(dga_ref, dga_rows, first)
        _acc_rows(dgs_ref, dgs_rows, first)
        _acc_rows(dgb_ref, dpre, first)

    vec = _full((1, SSM_W))
    return pl.pallas_call(
        body, name="mix_bwd", grid=(s // tm,),
        in_specs=[_row(tm, D_MODEL), _full((D_MODEL, D_MODEL)), _row(tm, ATTN_W), _row(tm, SSM_W),
                  _full((SSM_W, SSM_W)), vec, _full((SSM_W, SSM_W)), vec, vec],
        out_specs=[_row(tm, ATTN_W), _row(tm, SSM_W), _row(tm, SSM_W), _row(tm, SSM_W), vec, vec, vec],
        out_shape=[_sds((s, ATTN_W)), _sds((s, SSM_W)), _sds((s, SSM_W), BF16), _sds((s, SSM_W), BF16),
                   _sds((1, ATTN_W)), _sds((1, SSM_W)), _sds((1, SSM_W))],
        compiler_params=_cp(("arbitrary",)),
    )(dx2b, w_out_t, attn, y, glu_w, glu_b, glu_w_t, ga, gs)


def _qk_bwd(dqn, dkn, qk, dv, du, gq, gk, gmat, tm=1024):
    s = qk.shape[0]

    def body(dq_ref, dk_ref, qk_ref, dv_ref, du_ref, gq_ref, gk_ref, gm_ref, dp_ref, dgq_ref, dgk_ref):
        first = pl.program_id(0) == 0
        gm = gm_ref[...]
        for idx, (d_ref, g_ref, dg_ref) in enumerate(((dq_ref, gq_ref, dgq_ref), (dk_ref, gk_ref, dgk_ref))):
            xv = qk_ref[:, pl.ds(idx * ATTN_W, ATTN_W)]
            dyv = d_ref[...]
            r = lax.rsqrt(_group_mean(xv * xv, gm) + EPS)
            gdy = dyv * g_ref[...]
            dx = r * gdy - xv * (r * r * r) * _group_mean(gdy * xv, gm)
            dp_ref[:, pl.ds(idx * ATTN_W, ATTN_W)] = dx.astype(BF16)
            _acc_rows(dg_ref, dyv * (xv * r), first)
        dp_ref[:, pl.ds(2 * ATTN_W, ATTN_W)] = dv_ref[...].astype(BF16)
        dp_ref[:, pl.ds(3 * ATTN_W, SSM_W)] = du_ref[...].astype(BF16)

    vec = _full((1, ATTN_W))
    return pl.pallas_call(
        body, name="qk_bwd", grid=(s // tm,),
        in_specs=[_row(tm, ATTN_W), _row(tm, ATTN_W), _row(tm, 2 * ATTN_W), _row(tm, ATTN_W), _row(tm, SSM_W),
                  vec, vec, _full((ATTN_W, ATTN_W))],
        out_specs=[_row(tm, 4 * ATTN_W), vec, vec],
        out_shape=[_sds((s, 4 * ATTN_W), BF16), _sds((1, ATTN_W)), _sds((1, ATTN_W))],
        compiler_params=_cp(("arbitrary",)),
    )(dqn, dkn, qk, dv, du, gq, gk, gmat)


def _in_bwd(dproj, w_in_t, x, g1, dx2, ex=None, tm=1024):
    s = x.shape[0]
    steps = s // tm

    def body(*refs):
        (dp_ref, w_ref, x_ref, g1_ref, dx2_ref), (gx_ref, dg_ref), _, hx = _carry(ex, refs, 5, 2)
        if ex is not None:
            @pl.when(pl.program_id(0) == 0)
            def _():
                ex.start(*hx)

        dx, dg_rows = _rms_bwd(_dot(dp_ref[...], w_ref[...]), x_ref[...], g1_ref[...], D_MODEL)
        gx_ref[...] = dx2_ref[...] + dx
        _acc_rows(dg_ref, dg_rows, pl.program_id(0) == 0)
        if ex is not None:
            @pl.when(pl.program_id(0) == steps - 1)
            def _():
                ex.wait(*hx)

    hosted = ex is not None
    return pl.pallas_call(
        body, name="in_bwd", grid=(steps,),
        in_specs=[_row(tm, 4 * ATTN_W), _full((4 * ATTN_W, D_MODEL)), _row(tm, D_MODEL), _full((1, D_MODEL)),
                  _row(tm, D_MODEL)] + (ex.specs() if hosted else []),
        out_specs=[_row(tm, D_MODEL), _full((1, D_MODEL))] + (ex.specs() if hosted else []),
        out_shape=[_sds((s, D_MODEL)), _sds((1, D_MODEL))] + (ex.out_shape() if hosted else []),
        scratch_shapes=ex.scratch() if hosted else [],
        compiler_params=_cp(("arbitrary",)),
    )(dproj, w_in_t, x, g1, dx2, *(ex.srcs if hosted else []))


def _mm_tn(a, b, name, ts=4096):
    s, k = a.shape
    n = b.shape[1]
    bk, bn = min(k, 512), min(n, 1024)
    steps = s // ts

    def body(a_ref, b_ref, o_ref, acc):
        t = pl.program_id(2)

        @pl.when(t == 0)
        def _():
            acc[...] = jnp.zeros_like(acc)

        acc[...] += _dot_tn(a_ref[...], b_ref[...])

        @pl.when(t == steps - 1)
        def _():
            o_ref[...] = acc[...].astype(BF16)

    return pl.pallas_call(
        body, name=name, grid=(k // bk, n // bn, steps),
        in_specs=[pl.BlockSpec((ts, bk), lambda i, j, t: (t, i)), pl.BlockSpec((ts, bn), lambda i, j, t: (t, j))],
        out_specs=pl.BlockSpec((bk, bn), lambda i, j, t: (i, j)), out_shape=_sds((k, n), BF16),
        scratch_shapes=[pltpu.VMEM((bk, bn), F32)],
        compiler_params=_cp(("parallel", "parallel", "arbitrary")),
    )(a, b)


def _peer(k):
    x, y, c = lax.axis_index("x"), lax.axis_index("y"), lax.axis_index("c")
    px = 1 - x if k & 4 else x
    py = 1 - y if k & 2 else y
    pc = 1 - c if k & 1 else c
    return (px, py, pc), 4 * px + 2 * py + pc


def _gather_rows(x_shard):
    m_per, n = x_shard.shape

    def body(x_ref, out_ref, send_sems, recv_sems, local_sem):
        x, y, c = lax.axis_index("x"), lax.axis_index("y"), lax.axis_index("c")
        me, sibling = (x, y, c), (x, y, 1 - c)
        chips = [(1 - x, y), (x, 1 - y), (1 - x, 1 - y)]

        def rows(px, py, pc):
            return out_ref.at[pl.ds((4 * px + 2 * py + pc) * m_per, m_per), :]

        def copy(k, block, to, src=None):
            return pltpu.make_async_remote_copy(
                src_ref=rows(*block) if src is None else src, dst_ref=rows(*block),
                send_sem=send_sems.at[k], recv_sem=recv_sems.at[k], device_id=to, device_id_type=MESH)

        mine = pltpu.make_async_copy(x_ref, rows(*me), local_sem)
        mine.start()
        first = [copy(0, me, sibling, src=x_ref)]
        first += [copy(1 + j, me, (*chip, c), src=x_ref) for j, chip in enumerate(chips)]
        for cp in first:
            cp.start()
        passed = [copy(4 + j, (*chip, c), sibling) for j, chip in enumerate(chips)]
        for j, chip in enumerate(chips):
            copy(1 + j, (*chip, c), me).wait_recv()
            passed[j].start()
        copy(0, sibling, me).wait_recv()
        for j, chip in enumerate(chips):
            copy(4 + j, (*chip, 1 - c), me).wait_recv()
        for cp in first + passed:
            cp.wait_send()
        mine.wait()

    return pl.pallas_call(
        body, name="gather_weights", out_shape=_sds((N_DEV * m_per, n), x_shard.dtype),
        in_specs=[pl.BlockSpec(memory_space=pltpu.VMEM)], out_specs=pl.BlockSpec(memory_space=pltpu.VMEM),
        scratch_shapes=[pltpu.SemaphoreType.DMA((7,)), pltpu.SemaphoreType.DMA((7,)), pltpu.SemaphoreType.DMA],
        compiler_params=pltpu.CompilerParams(vmem_limit_bytes=VMEM_LIMIT),
    )(x_shard)


class _Exchange:
    def __init__(self, srcs, whole):
        self.srcs, self.whole, self.n = list(srcs), list(whole), len(srcs)
        self.rows = [a.shape[0] // N_DEV if w is False else a.shape[0] for a, w in zip(self.srcs, self.whole)]
        self.cols = [a.shape[1] // N_DEV if w == "col" else a.shape[1] for a, w in zip(self.srcs, self.whole)]

    def specs(self):
        return [pl.BlockSpec(memory_space=pl.ANY)] * self.n

    def out_shape(self):
        return [_sds((N_DEV, r, c), a.dtype) for r, c, a in zip(self.rows, self.cols, self.srcs)]

    def scratch(self):
        return [pltpu.SemaphoreType.DMA((self.n * 7,)), pltpu.SemaphoreType.DMA((self.n * 7,)),
                pltpu.SemaphoreType.DMA((self.n,))]

    def _copies(self, ins, outs, sems):
        send_sems, recv_sems, local_sems = sems
        _, me = _peer(0)
        for w in range(self.n):
            for k in range(N_DEV):
                peer, pidx = _peer(k)
                if self.whole[w] is True:
                    src = ins[w]
                elif self.whole[w] == "col":
                    src = ins[w].at[:, pl.ds(pl.multiple_of(pidx * self.cols[w], self.cols[w]), self.cols[w])]
                else:
                    src = ins[w].at[pl.ds(pidx * self.rows[w], self.rows[w]), :]
                if k == 0:
                    yield k, pltpu.make_async_copy(src, outs[w].at[me], local_sems.at[w]), None
                else:
                    sem = w * 7 + k - 1
                    out = pltpu.make_async_remote_copy(src_ref=src, dst_ref=outs[w].at[me], send_sem=send_sems.at[sem],
                                                       recv_sem=recv_sems.at[sem], device_id=peer, device_id_type=MESH)
                    back = pltpu.make_async_remote_copy(src_ref=src, dst_ref=outs[w].at[pidx], send_sem=send_sems.at[sem],
                                                        recv_sem=recv_sems.at[sem], device_id=peer, device_id_type=MESH)
                    yield k, out, back

    def start(self, ins, outs, sems):
        for _, out, _ in self._copies(ins, outs, sems):
            out.start()

    def wait(self, ins, outs, sems):
        for k, out, back in self._copies(ins, outs, sems):
            if k == 0:
                out.wait()
            else:
                back.wait_recv()
                out.wait_send()


def _carry(ex, refs, n_in, n_out):
    nh = ex.n if ex is not None else 0
    ins, hin = refs[:n_in], refs[n_in:n_in + nh]
    outs = refs[n_in + nh:n_in + nh + n_out]
    hout = refs[n_in + nh + n_out:n_in + 2 * nh + n_out]
    rest = refs[n_in + 2 * nh + n_out:]
    if ex is None:
        return ins, outs, rest, None
    return ins, outs, rest[:-3], (hin, hout, rest[-3:])


def _exchange_now(srcs, whole, name):
    ex = _Exchange(srcs, whole)

    def body(*refs):
        _, _, _, (hin, hout, sems) = _carry(ex, refs, 0, 0)
        ex.start(hin, hout, sems)
        ex.wait(hin, hout, sems)

    return pl.pallas_call(body, name=name, out_shape=ex.out_shape(), in_specs=ex.specs(), out_specs=ex.specs(),
                          scratch_shapes=ex.scratch())(*srcs)


def _split_exchange_start(src):
    rows, width = src.shape[0], src.shape[1] // N_DEV

    def body(src_ref, land_ref, send_sems, recv_sems, src_thru, land_thru, token):
        barrier = pltpu.get_barrier_semaphore()
        for k in range(1, N_DEV):
            pl.semaphore_signal(barrier, inc=1, device_id=_peer(k)[0], device_id_type=MESH)
        pl.semaphore_wait(barrier, N_DEV - 1)
        _, me = _peer(0)
        for k in range(1, N_DEV):
            peer, pidx = _peer(k)
            pltpu.make_async_remote_copy(
                src_ref=src_ref.at[:, pl.ds(pl.multiple_of(pidx * width, width), width)], dst_ref=land_ref.at[me],
                send_sem=send_sems.at[k - 1], recv_sem=recv_sems.at[k - 1], device_id=peer, device_id_type=MESH).start()
        token[...] = jnp.zeros_like(token)

    hbm = pl.BlockSpec(memory_space=pltpu.HBM)
    sem = pl.BlockSpec(memory_space=pltpu.SEMAPHORE)
    land = lax.empty((N_DEV, rows, width), src.dtype)
    return pl.pallas_call(
        body, name="w_in_exchange_start",
        out_shape=(pltpu.SemaphoreType.DMA((N_DEV - 1,)), pltpu.SemaphoreType.DMA((N_DEV - 1,)),
                   pltpu.HBM(src.shape, src.dtype), pltpu.HBM(land.shape, land.dtype), _sds((8, LANES))),
        in_specs=(hbm, hbm), out_specs=(sem, sem, hbm, hbm, pl.BlockSpec(memory_space=pltpu.VMEM)),
        input_output_aliases={0: 2, 1: 3},
        compiler_params=pltpu.CompilerParams(has_side_effects=pltpu.SideEffectType.DATAFLOW_SIDE_EFFECTING, collective_id=0),
    )(pltpu.with_memory_space_constraint(src, pltpu.HBM), pltpu.with_memory_space_constraint(land, pltpu.HBM))


def _split_exchange_wait(send_sems, recv_sems, src_thru, land_thru, after):
    width = land_thru.shape[2]

    def body(src_ref, land_ref, send_sems, recv_sems, after_ref, src_dead, got_ref):
        _, me = _peer(0)
        for k in range(1, N_DEV):
            peer, pidx = _peer(k)
            copy = pltpu.make_async_remote_copy(
                src_ref=src_ref.at[:, pl.ds(pl.multiple_of(pidx * width, width), width)], dst_ref=land_ref.at[pidx],
                send_sem=send_sems.at[k - 1], recv_sem=recv_sems.at[k - 1], device_id=peer, device_id_type=MESH)
            copy.wait_send()
            copy.wait_recv()

    hbm = pl.BlockSpec(memory_space=pltpu.HBM)
    sem = pl.BlockSpec(memory_space=pltpu.SEMAPHORE)
    return pl.pallas_call(
        body, name="w_in_exchange_wait",
        out_shape=(pltpu.HBM(src_thru.shape, src_thru.dtype), pltpu.HBM(land_thru.shape, land_thru.dtype)),
        in_specs=(hbm, hbm, sem, sem, pl.BlockSpec(memory_space=pl.ANY)), out_specs=(hbm, hbm),
        input_output_aliases={0: 0, 1: 1},
        compiler_params=pltpu.CompilerParams(has_side_effects=pltpu.SideEffectType.DATAFLOW_SIDE_EFFECTING),
    )(src_thru, land_thru, send_sems, recv_sems, after)


def _adamw(w, m, v, gparts, name):
    r, c = w.shape
    tr = r if r * c <= 256 * 1024 else 128 * 1024 // c

    def body(w_ref, m_ref, v_ref, g_ref, go_ref, d_ref, mo_ref, vo_ref):
        g = g_ref[0].astype(F32)
        for i in range(1, N_DEV):
            g = g + g_ref[i].astype(F32)
        go_ref[...] = g
        d_ref[...], mo_ref[...], vo_ref[...] = _adamw_step(w_ref[...], m_ref[...], v_ref[...], g)

    blk = pl.BlockSpec((tr, c), lambda i: (i, 0))
    return pl.pallas_call(
        body, name=name, grid=(r // tr,),
        in_specs=[blk, blk, blk, pl.BlockSpec((N_DEV, tr, c), lambda i: (0, i, 0))],
        out_specs=[blk] * 4, out_shape=[_sds((r, c))] * 4,
        compiler_params=_cp(("parallel",)),
    )(w, m, v, gparts)


def _adamw_step(w, m, v, g):
    nm = ADAM_B1 * m + (1.0 - ADAM_B1) * g
    nv = ADAM_B2 * v + (1.0 - ADAM_B2) * (g * g)
    m_hat = nm / (1.0 - ADAM_B1 ** ADAM_STEP)
    v_hat = nv / (1.0 - ADAM_B2 ** ADAM_STEP)
    return -ADAM_LR * (m_hat / (jnp.sqrt(v_hat) + ADAM_EPS) + ADAM_WD * w), nm, nv


def _sum_slots(parts):
    def body(p_ref, o_ref):
        g = p_ref[0]
        for i in range(1, N_DEV):
            g = g + p_ref[i]
        o_ref[...] = g

    return pl.pallas_call(body, name="sum_small_grads", out_shape=_sds(parts.shape[1:]))(parts)


def _adamw_native(ws, ms, vs, gs):
    n = len(ws)

    def body(*refs):
        for i in range(n):
            w_ref, m_ref, v_ref, g_ref = refs[i], refs[n + i], refs[2 * n + i], refs[3 * n + i]
            d, nm, nv = _adamw_step(w_ref[...], m_ref[...], v_ref[...], g_ref[...])
            refs[4 * n + i][...] = d
            refs[5 * n + i][...] = nm
            refs[6 * n + i][...] = nv

    outs = pl.pallas_call(body, name="adamw_small", out_shape=[_sds(w.shape) for w in ws] * 3,
                          compiler_params=pltpu.CompilerParams(vmem_limit_bytes=VMEM_LIMIT))(*ws, *ms, *vs, *gs)
    return outs[:n], outs[n:2 * n], outs[2 * n:]


def _block_diag(a, states_first):
    a4 = a.reshape(N_CB, 8, SSM_GROUP, SSM_STATE)
    eye = jnp.eye(8, dtype=a.dtype)
    if states_first:
        return jnp.einsum("bgcp,gh->bgphc", a4, eye).reshape(N_CB, CB_STATES, LANES)
    return jnp.einsum("bgcp,gh->bgchp", a4, eye).reshape(N_CB, LANES, CB_STATES)


def _block_diag_of(full, states_first):
    if states_first:
        picked = jnp.einsum("bgphc,gh->bgcp", full.reshape(N_CB, 8, SSM_STATE, 8, SSM_GROUP), jnp.eye(8, dtype=full.dtype))
    else:
        picked = jnp.einsum("bgchp,gh->bgcp", full.reshape(N_CB, 8, SSM_GROUP, 8, SSM_STATE), jnp.eye(8, dtype=full.dtype))
    return picked.reshape(SSM_GROUPS, SSM_GROUP, SSM_STATE)


SMALL_EARLY = ("ssm_a_re", "ssm_a_im", "ssm_log_dt", "ssm_b_re", "ssm_b_im", "ssm_c_re", "ssm_c_im", "ssm_d", "glu_b",
               "attn_out_norm_g", "ssm_out_norm_g", "norm2_g")
SMALL_MID = ("q_norm_g", "k_norm_g")
SMALL_LATE = ("norm1_g",)
SMALL = SMALL_EARLY + SMALL_MID + SMALL_LATE


def _pack_small(arrs):
    parts = []
    for a in arrs:
        flat = a.reshape(-1)
        rows = -(-flat.shape[0] // (8 * LANES)) * 8
        parts.append(jnp.pad(flat, (0, rows * LANES - flat.shape[0])).reshape(rows, LANES))
    return jnp.concatenate(parts, axis=0)


def _unpack_small(packed, shapes):
    out, r0 = [], 0
    for shp in shapes:
        size = math.prod(shp)
        rows = -(-size // (8 * LANES)) * 8
        out.append(packed[r0:r0 + rows].reshape(-1)[:size].reshape(shp))
        r0 += rows
    return out


def kernel(x, norm1_g, w_in, q_norm_g, k_norm_g, ssm_a_re, ssm_a_im, ssm_log_dt, ssm_b_re, ssm_b_im, ssm_c_re, ssm_c_im, ssm_d, glu_w, glu_b, attn_out_norm_g, ssm_out_norm_g, w_out, norm2_g, w_mlp_up, w_mlp_down, loss_target, m_norm1_g, m_w_in, m_q_norm_g, m_k_norm_g, m_ssm_a_re, m_ssm_a_im, m_ssm_log_dt, m_ssm_b_re, m_ssm_b_im, m_ssm_c_re, m_ssm_c_im, m_ssm_d, m_glu_w, m_glu_b, m_attn_out_norm_g, m_ssm_out_norm_g, m_w_out, m_norm2_g, m_w_mlp_up, m_w_mlp_down, v_norm1_g, v_w_in, v_q_norm_g, v_k_norm_g, v_ssm_a_re, v_ssm_a_im, v_ssm_log_dt, v_ssm_b_re, v_ssm_b_im, v_ssm_c_re, v_ssm_c_im, v_ssm_d, v_glu_w, v_glu_b, v_attn_out_norm_g, v_ssm_out_norm_g, v_w_out, v_norm2_g, v_w_mlp_up, v_w_mlp_down):
    weights = dict(norm1_g=norm1_g, w_in=w_in, q_norm_g=q_norm_g, k_norm_g=k_norm_g, ssm_a_re=ssm_a_re,
                   ssm_a_im=ssm_a_im, ssm_log_dt=ssm_log_dt, ssm_b_re=ssm_b_re, ssm_b_im=ssm_b_im,
                   ssm_c_re=ssm_c_re, ssm_c_im=ssm_c_im, ssm_d=ssm_d, glu_w=glu_w, glu_b=glu_b,
                   attn_out_norm_g=attn_out_norm_g, ssm_out_norm_g=ssm_out_norm_g, w_out=w_out, norm2_g=norm2_g,
                   w_mlp_up=w_mlp_up, w_mlp_down=w_mlp_down)
    mom_m = dict(norm1_g=m_norm1_g, w_in=m_w_in, q_norm_g=m_q_norm_g, k_norm_g=m_k_norm_g, ssm_a_re=m_ssm_a_re,
                 ssm_a_im=m_ssm_a_im, ssm_log_dt=m_ssm_log_dt, ssm_b_re=m_ssm_b_re, ssm_b_im=m_ssm_b_im,
                 ssm_c_re=m_ssm_c_re, ssm_c_im=m_ssm_c_im, ssm_d=m_ssm_d, glu_w=m_glu_w, glu_b=m_glu_b,
                 attn_out_norm_g=m_attn_out_norm_g, ssm_out_norm_g=m_ssm_out_norm_g, w_out=m_w_out,
                 norm2_g=m_norm2_g, w_mlp_up=m_w_mlp_up, w_mlp_down=m_w_mlp_down)
    mom_v = dict(norm1_g=v_norm1_g, w_in=v_w_in, q_norm_g=v_q_norm_g, k_norm_g=v_k_norm_g, ssm_a_re=v_ssm_a_re,
                 ssm_a_im=v_ssm_a_im, ssm_log_dt=v_ssm_log_dt, ssm_b_re=v_ssm_b_re, ssm_b_im=v_ssm_b_im,
                 ssm_c_re=v_ssm_c_re, ssm_c_im=v_ssm_c_im, ssm_d=v_ssm_d, glu_w=v_glu_w, glu_b=v_glu_b,
                 attn_out_norm_g=v_attn_out_norm_g, ssm_out_norm_g=v_ssm_out_norm_g, w_out=v_w_out,
                 norm2_g=v_norm2_g, w_mlp_up=v_w_mlp_up, w_mlp_down=v_w_mlp_down)
    order = list(weights)

    xs, tgt = x[0], loss_target[0]
    s = xs.shape[0]
    assert s % ATTN_CHUNK == 0 and (s // N_SEG) % SSM_LK == 0
    seg_len = s // N_SEG
    n_sq = seg_len.bit_length() - 1
    assert 1 << n_sq == seg_len

    w_in_t = _gather_rows(w_in[0].T.astype(BF16))
    w_in_full = w_in_t.T
    later = _Exchange([glu_w[0].astype(BF16), w_out[0].astype(BF16), w_mlp_up[0].T.astype(BF16),
                       w_mlp_down[0].astype(BF16)], [True] * 4)

    gq = jnp.tile(q_norm_g[0], ATTN_W // HEAD_DIM)[None]
    gk = jnp.tile(k_norm_g[0], ATTN_W // HEAD_DIM)[None]
    lane = jnp.arange(ATTN_W) // HEAD_DIM
    gmat = jnp.where(lane[:, None] == lane[None, :], 1.0 / HEAD_DIM, 0.0).astype(BF16)
    a_re3 = ssm_a_re[0][:, None, :]
    a_im3 = ssm_a_im[0][:, None, :]
    ldt3 = ssm_log_dt[0][:, None, None]
    b_re_t = jnp.swapaxes(ssm_b_re[0], 1, 2)
    b_im_t = jnp.swapaxes(ssm_b_im[0], 1, 2)
    c_re, c_im = ssm_c_re[0], ssm_c_im[0]
    dskip = ssm_d[0].reshape(1, SSM_W)

    qk, qn, kn, v, u, xn = _fwd_proj(xs, norm1_g, w_in_full, gq, gk, gmat)
    attn, lse, glu_g, w_out_g, w_up_g, w_down_g = _attn_fwd(qn, kn, v, later)
    glu_full = glu_g.reshape(SSM_W, SSM_W)
    w_out_full = w_out_g.reshape(D_MODEL, D_MODEL)
    w_up_t = w_up_g.reshape(D_FF, D_MODEL)
    w_down_full = w_down_g.reshape(D_FF, D_MODEL)
    w_up_full, w_out_t, w_down_t, glu_t = w_up_t.T, w_out_full.T, w_down_full.T, glu_full.T

    abr3, abi3, bbr, bbi = _ssm_discretize(a_re3, a_im3, ldt3, b_re_t, b_im_t)
    abr, abi = abr3.reshape(1, N_STATE), abi3.reshape(1, N_STATE)
    pw_r, pw_i = _ssm_power(abr, abi, n_sq)
    bb_r, bb_i = _block_diag(bbr, False).astype(BF16), _block_diag(bbi, False).astype(BF16)
    bbt_r, bbt_i = _block_diag(bbr, True).astype(BF16), _block_diag(bbi, True).astype(BF16)
    cc_r, cc_i = _block_diag(c_re, True).astype(BF16), _block_diag(c_im, True).astype(BF16)
    cct_r, cct_i = _block_diag(c_re, False).astype(BF16), _block_diag(c_im, False).astype(BF16)
    seg_major = lambda a: jnp.swapaxes(a.reshape(N_SEG, seg_len, SSM_W), 0, 1)
    seg_minor = lambda a: jnp.swapaxes(a, 0, 1).reshape(s, SSM_W)
    u3 = seg_major(u)
    zero_fin = jnp.zeros((N_SEG, N_STATE), F32)
    ssm_args = (abr, abi, pw_r, pw_i)
    xf_r, xf_i = _ssm_fwd(u3, *ssm_args, zero_fin, zero_fin, bb_r, bb_i, cc_r, cc_i, dskip, True)
    y3, xst_r, xst_i = _ssm_fwd(u3, *ssm_args, xf_r, xf_i, bb_r, bb_i, cc_r, cc_i, dskip, False)
    y = seg_minor(y3)

    mix, x2, h = _fwd_mix(attn, y, xs, glu_full, glu_b, attn_out_norm_g, ssm_out_norm_g, w_out_full, norm2_g)
    r_act, hdn = _mlp_up(h, w_up_full)
    dy, dyb, sse = _mlp_down_loss(hdn, w_down_full, x2, tgt)

    dup = _mlp_down_bwd(dyb, w_down_t, r_act)
    g_w_down = _mm_tn(hdn, dyb, "grad_w_down")
    dx2, dx2b, g_norm2 = _mlp_up_bwd(dup, w_up_t, x2, norm2_g, dy)
    g_w_up = _mm_tn(h, dup, "grad_w_up")
    dattn, dys, z_b, dpre_b, g_ga, g_gs, g_glu_b = _mix_bwd(dx2b, w_out_t, attn, y, glu_full, glu_b, glu_t,
                                                             attn_out_norm_g, ssm_out_norm_g)
    g_w_out = _mm_tn(mix, dx2b, "grad_w_out")
    g_glu_w = _mm_tn(z_b, dpre_b, "grad_glu_w")
    dy3 = seg_major(dys)
    bwd_args = (u3, dy3, xst_r, xst_i, abr, abi, pw_r, pw_i)
    lf_r, lf_i = _ssm_bwd(*bwd_args, zero_fin, zero_fin, bb_r, bb_i, bbt_r, bbt_i, cct_r, cct_i, dskip, True)
    du3, dab_r, dab_i, dbb_r, dbb_i, dcc_r, dcc_i, g_d = _ssm_bwd(*bwd_args, lf_r, lf_i, bb_r, bb_i, bbt_r, bbt_i,
                                                                 cct_r, cct_i, dskip, False)
    g_a_re3, g_a_im3, g_ldt3, g_b_re_t, g_b_im_t = _ssm_discretize_bwd(
        a_re3, a_im3, ldt3, b_re_t, b_im_t, dab_r.reshape(a_re3.shape), dab_i.reshape(a_re3.shape),
        _block_diag_of(dbb_r, True), _block_diag_of(dbb_i, True))
    g_c_re, g_c_im = _block_diag_of(dcc_r, False), _block_diag_of(dcc_i, False)
    small_grads = dict(
        ssm_a_re=g_a_re3.reshape(ssm_a_re.shape), ssm_a_im=g_a_im3.reshape(ssm_a_im.shape),
        ssm_log_dt=g_ldt3.reshape(ssm_log_dt.shape), ssm_b_re=jnp.swapaxes(g_b_re_t, 1, 2)[None],
        ssm_b_im=jnp.swapaxes(g_b_im_t, 1, 2)[None], ssm_c_re=g_c_re[None], ssm_c_im=g_c_im[None],
        ssm_d=g_d.reshape(ssm_d.shape), glu_b=g_glu_b, attn_out_norm_g=g_ga, ssm_out_norm_g=g_gs, norm2_g=g_norm2)

    early = _Exchange([g_glu_w, g_w_out, g_w_up, g_w_down, _pack_small([small_grads[n] for n in SMALL_EARLY] + [sse])],
                      [False, False, "col", False, True])
    dqn, dkn, dv, p_glu, p_w_out, p_w_up, p_w_down, p_early = _attn_bwd(qn, kn, v, attn, lse, dattn, early)

    dproj, g_gq, g_gk = _qk_bwd(dqn, dkn, qk, dv, seg_minor(du3), gq, gk, gmat)
    g_w_in = _mm_tn(xn, dproj, "grad_w_in")
    small_grads["q_norm_g"] = g_gq.reshape(ATTN_W // HEAD_DIM, HEAD_DIM).sum(0)[None]
    small_grads["k_norm_g"] = g_gk.reshape(ATTN_W // HEAD_DIM, HEAD_DIM).sum(0)[None]
    send_sems, recv_sems, g_thru, land, token = _split_exchange_start(g_w_in)
    grad_x, g_norm1 = _in_bwd(dproj, w_in_t, xs, norm1_g + token[:1, :1], dx2)
    g_w_in, landed = _split_exchange_wait(send_sems, recv_sems, g_thru, land, g_norm1)
    me = 4 * lax.axis_index("x") + 2 * lax.axis_index("y") + lax.axis_index("c")
    shard_w = g_w_in.shape[1] // N_DEV
    own = lax.dynamic_slice(g_w_in, (0, me * shard_w), (g_w_in.shape[0], shard_w))
    p_w_in = lax.dynamic_update_slice(landed, own[None], (me, 0, 0))
    p_mid, p_late = _exchange_now([_pack_small([small_grads[n] for n in SMALL_MID]), _pack_small([g_norm1])],
                                  [True, True], "exchange_tail")

    res = {}
    for name, gp in (("w_in", p_w_in), ("glu_w", p_glu), ("w_out", p_w_out), ("w_mlp_up", p_w_up), ("w_mlp_down", p_w_down)):
        outs = _adamw(weights[name][0], mom_m[name][0], mom_v[name][0], gp, "adamw_" + name)
        res[name] = [o[None] for o in outs]
    shapes = [weights[n].shape for n in SMALL_EARLY] + [sse.shape] + [weights[n].shape for n in SMALL_MID + SMALL_LATE]
    g_small = _unpack_small(_sum_slots(jnp.concatenate([p_early, p_mid, p_late], axis=1)), shapes)
    loss = 0.5 * g_small.pop(len(SMALL_EARLY))[0, 0] / D_MODEL
    d_small, m_small, v_small = _adamw_native([weights[n] for n in SMALL], [mom_m[n] for n in SMALL],
                                              [mom_v[n] for n in SMALL], g_small)
    for i, n in enumerate(SMALL):
        res[n] = [g_small[i], d_small[i], m_small[i], v_small[i]]

    return (loss, grad_x[None], *[res[n][0] for n in order], *[res[n][1] for n in order],
            *[res[n][2] for n in order], *[res[n][3] for n in order])
```

```python
import math

import jax
import jax.numpy as jnp
from jax import lax
from jax.experimental import pallas as pl
from jax.experimental.pallas import tpu as pltpu

F32 = jnp.float32
BF16 = jnp.bfloat16

D_MODEL = 1024
ATTN_W = 512
HEAD_DIM = 64
SSM_W = 512
SSM_GROUP = 16
SSM_GROUPS = 32
SSM_STATE = 64
N_STATE = SSM_GROUPS * SSM_STATE
D_FF = 4096
EPS = 1e-6
NEG_INF = -1e30
ATTN_CHUNK = 2048
ATTN_BLOCK = 128
DILATIONS = (1, 4, 16)
N_SEG = 8
SSM_LK = 64
N_DEV = 8
LANES = 128
MXU_W = 256

ADAM_LR = 0.001
ADAM_B1 = 0.9
ADAM_B2 = 0.999
ADAM_EPS = 1e-08
ADAM_WD = 0.01
ADAM_STEP = 10

VMEM_LIMIT = 56 * 1024 * 1024
GELU_C = math.sqrt(2.0 / math.pi)
MESH = pl.DeviceIdType.MESH


def _cp(sem, vmem=VMEM_LIMIT):
    return pltpu.CompilerParams(dimension_semantics=sem, vmem_limit_bytes=vmem)


def _dot(a, b):
    return jnp.dot(a, b, preferred_element_type=F32)


def _dot_tn(a, b):
    return lax.dot_general(a, b, (((0,), (0,)), ((), ())), preferred_element_type=F32)


def _group_mean(x2, gmat):
    hi = x2.astype(BF16)
    lo = (x2 - hi.astype(F32)).astype(BF16)
    w = gmat.shape[0]
    return jnp.concatenate([_dot(hi[:, c:c + w], gmat) + _dot(lo[:, c:c + w], gmat) for c in range(0, x2.shape[1], w)],
                           axis=1)


def _rms(x):
    return lax.rsqrt(jnp.mean(x * x, axis=-1, keepdims=True) + EPS)


def _rms_bwd(dy, x, g, n):
    r = _rms(x)
    gdy = dy * g
    dx = r * gdy - x * (r * r * r) * (jnp.sum(gdy * x, axis=-1, keepdims=True) / n)
    return dx, dy * (x * r)


def _gelu(y):
    t = jnp.tanh(GELU_C * (y + 0.044715 * (y * y * y)))
    return 0.5 * y * (1.0 + t), t


def _full(shape):
    nd = len(shape)
    return pl.BlockSpec(shape, lambda *_: (0,) * nd)


def _sds(shape, dtype=F32):
    return jax.ShapeDtypeStruct(shape, dtype)


def _fwd_proj(x, g1, w_in, gq, gk, gmat, tm=512):
    s = x.shape[0]

    def body(x_ref, g1_ref, w_ref, gq_ref, gk_ref, gm_ref, qk_ref, qn_ref, kn_ref, v_ref, u_ref, xn_ref):
        xv = x_ref[...]
        xnb = ((xv * _rms(xv)) * g1_ref[...]).astype(BF16)
        xn_ref[...] = xnb
        proj = _dot(xnb, w_ref[...])
        q = proj[:, :ATTN_W]
        k = proj[:, ATTN_W:2 * ATTN_W]
        qk_ref[...] = proj[:, :2 * ATTN_W]
        v_ref[...] = proj[:, 2 * ATTN_W:3 * ATTN_W]
        u_ref[...] = proj[:, 3 * ATTN_W:]
        gm = gm_ref[...]
        qn_ref[...] = (q * lax.rsqrt(_group_mean(q * q, gm) + EPS)) * gq_ref[...]
        kn_ref[...] = (k * lax.rsqrt(_group_mean(k * k, gm) + EPS)) * gk_ref[...]

    row = lambda w: pl.BlockSpec((tm, w), lambda i: (i, 0))
    return pl.pallas_call(
        body, name="fwd_proj", grid=(s // tm,),
        in_specs=[row(D_MODEL), _full((1, D_MODEL)), _full((D_MODEL, 4 * ATTN_W)), _full((1, ATTN_W)),
                  _full((1, ATTN_W)), _full((MXU_W, MXU_W))],
        out_specs=[row(2 * ATTN_W), row(ATTN_W), row(ATTN_W), row(ATTN_W), row(ATTN_W), row(D_MODEL)],
        out_shape=[_sds((s, 2 * ATTN_W)), _sds((s, ATTN_W)), _sds((s, ATTN_W)), _sds((s, ATTN_W)),
                   _sds((s, ATTN_W)), _sds((s, D_MODEL), BF16)],
        compiler_params=_cp(("parallel",)),
    )(x, g1, w_in, gq, gk, gmat)


def _attn_rows(t, d, nb):
    if d == 1:
        q0 = t * ATTN_BLOCK
        return (t, pl.ds(q0, ATTN_BLOCK), pl.ds(ATTN_CHUNK + q0, ATTN_BLOCK),
                pl.ds(ATTN_CHUNK - ATTN_BLOCK + q0, ATTN_BLOCK))
    r = t // nb
    b = t % nb
    return (b, pl.ds(ATTN_BLOCK * b * d + r, ATTN_BLOCK, stride=d),
            pl.ds(ATTN_CHUNK + ATTN_BLOCK * b * d + r, ATTN_BLOCK, stride=d),
            pl.ds(ATTN_CHUNK + ATTN_BLOCK * (b - 1) * d + r, ATTN_BLOCK, stride=d))


def _attn_masks():
    row = lax.broadcasted_iota(jnp.int32, (ATTN_BLOCK, LANES), 0)
    col = lax.broadcasted_iota(jnp.int32, (ATTN_BLOCK, LANES), 1)
    return row, col


NBLK = ATTN_CHUNK // ATTN_BLOCK


def _attn_bias(bias_s):
    row, col = _attn_masks()
    bias_s[:, pl.ds(0, LANES)] = jnp.where(col >= row, 0.0, NEG_INF)
    bias_s[:, pl.ds(LANES, LANES)] = jnp.where(col <= row, 0.0, NEG_INF)
    return col < HEAD_DIM


def _attn_fwd(qn, kn, v, ex=None, group=4):
    s = qn.shape[0]
    nch = s // ATTN_CHUNK
    scale = HEAD_DIM ** -0.5
    npat = len(DILATIONS)
    n_hp = ATTN_W // LANES

    def body(*refs):
        ((q_ref, kp_ref, kc_ref, vp_ref, vc_ref), (o_ref, lse_ref),
         (kk, vv, kt_s, vb_s, bias_s, m_s, a_s), hx) = _carry(ex, refs, 5, 2)
        i = pl.program_id(1)
        if ex is not None:
            @pl.when(jnp.logical_and(pl.program_id(0) == 0, i == 0))
            def _():
                ex.start(*hx)

        kk[pl.ds(0, ATTN_CHUNK), :] = kp_ref[...]
        kk[pl.ds(ATTN_CHUNK, ATTN_CHUNK), :] = kc_ref[...]
        vv[pl.ds(0, ATTN_CHUNK), :] = vp_ref[...]
        vv[pl.ds(ATTN_CHUNK, ATTN_CHUNK), :] = vc_ref[...]
        head0 = _attn_bias(bias_s)
        first_pen = jnp.where(i > 0, 0.0, NEG_INF)

        for p, d in enumerate(DILATIONS):
            nb = ATTN_CHUNK // (ATTN_BLOCK * d)

            def prep(t, d=d, nb=nb):
                _, _, crows, prows = _attn_rows(t, d, nb)
                kt_s[t, :, pl.ds(0, LANES)] = kk[prows, :].T.astype(BF16)
                kt_s[t, :, pl.ds(LANES, LANES)] = kk[crows, :].T.astype(BF16)
                vp = vv[prows, :]
                vc = vv[crows, :]
                vb_s[2 * t, pl.ds(0, ATTN_BLOCK), :] = jnp.where(head0, vp, 1.0).astype(BF16)
                vb_s[2 * t, pl.ds(ATTN_BLOCK, ATTN_BLOCK), :] = jnp.where(head0, vc, 1.0).astype(BF16)
                vb_s[2 * t + 1, pl.ds(0, ATTN_BLOCK), :] = jnp.where(head0, 1.0, vp).astype(BF16)
                vb_s[2 * t + 1, pl.ds(ATTN_BLOCK, ATTN_BLOCK), :] = jnp.where(head0, 1.0, vc).astype(BF16)

            def main(tg, p=p, d=d, nb=nb):
                st = []
                for g in range(group):
                    t = tg * group + g
                    b, qrows, _, _ = _attn_rows(t, d, nb)
                    q = q_ref[qrows, :] * scale
                    for h in range(2):
                        hm = head0 if h == 0 else jnp.logical_not(head0)
                        st.append(dict(t=t, b=b, qrows=qrows, sc=_dot(jnp.where(hm, q, 0.0).astype(BF16), kt_s[t])))
                for e in st:
                    sc = e["sc"] + bias_s[...]
                    s_p = sc[:, :LANES] + first_pen if e["b"] == 0 else sc[:, :LANES]
                    s_c = sc[:, LANES:]
                    m = jnp.max(jnp.maximum(s_p, s_c), axis=-1, keepdims=True)
                    e["eb"] = jnp.concatenate([jnp.exp(s_p - m), jnp.exp(s_c - m)], axis=1).astype(BF16)
                    e["m"] = jnp.broadcast_to(m, (ATTN_BLOCK, LANES))
                for g in range(group):
                    e0, e1 = st[2 * g], st[2 * g + 1]
                    t = e0["t"]
                    m_s[p, e0["qrows"], :] = jnp.where(head0, e0["m"], e1["m"])
                    a_s[2 * p, e0["qrows"], :] = _dot(e0["eb"], vb_s[2 * t])
                    a_s[2 * p + 1, e0["qrows"], :] = _dot(e1["eb"], vb_s[2 * t + 1])

            for g in range(group):
                prep(g)
            for tg in range(NBLK // group):
                if tg + 1 < NBLK // group:
                    for g in range(group):
                        prep((tg + 1) * group + g)
                main(tg)

        def merge(t, carry):
            rows = pl.ds(pl.multiple_of(t * ATTN_BLOCK, ATTN_BLOCK), ATTN_BLOCK)
            m_all = m_s[0, rows, :]
            for p in range(1, npat):
                m_all = jnp.maximum(m_all, m_s[p, rows, :])
            num = jnp.zeros((ATTN_BLOCK, LANES), F32)
            den = jnp.zeros((ATTN_BLOCK, LANES), F32)
            for p in range(npat):
                w = jnp.exp(m_s[p, rows, :] - m_all)
                a0, a1 = a_s[2 * p, rows, :], a_s[2 * p + 1, rows, :]
                num = num + jnp.where(head0, a0, a1) * w
                den = den + pltpu.roll(jnp.where(head0, a1, a0), HEAD_DIM, 1) * w
            o_ref[rows, :] = num / den
            lse_ref[rows, :] = m_all + jnp.log(den)
            return carry

        lax.fori_loop(0, NBLK, merge, 0, unroll=2)
        if ex is not None:
            @pl.when(jnp.logical_and(pl.program_id(0) == n_hp - 1, i == nch - 1))
            def _():
                ex.wait(*hx)

    cur = pl.BlockSpec((ATTN_CHUNK, LANES), lambda h, i: (i, h))
    prev = pl.BlockSpec((ATTN_CHUNK, LANES), lambda h, i: (jnp.maximum(i - 1, 0), h))
    hosted = ex is not None
    return pl.pallas_call(
        body, name="attn_fwd", grid=(n_hp, nch),
        in_specs=[cur, prev, cur, prev, cur] + (ex.specs() if hosted else []),
        out_specs=[cur, cur] + (ex.specs() if hosted else []),
        out_shape=[_sds((s, ATTN_W)), _sds((s, ATTN_W))] + (ex.out_shape() if hosted else []),
        scratch_shapes=[pltpu.VMEM((2 * ATTN_CHUNK, LANES), F32), pltpu.VMEM((2 * ATTN_CHUNK, LANES), F32),
                        pltpu.VMEM((NBLK, LANES, 2 * LANES), BF16), pltpu.VMEM((2 * NBLK, 2 * ATTN_BLOCK, LANES), BF16),
                        pltpu.VMEM((ATTN_BLOCK, 2 * LANES), F32),
                        pltpu.VMEM((npat, ATTN_CHUNK, LANES), F32), pltpu.VMEM((2 * npat, ATTN_CHUNK, LANES), F32)]
        + (ex.scratch() if hosted else []),
        compiler_params=_cp(("arbitrary", "arbitrary")),
    )(qn, kn, kn, v, v, *(ex.srcs if hosted else []))


def _attn_bwd(qn, kn, v, o, lse, do, ex=None, group=4):
    s = qn.shape[0]
    nch = s // ATTN_CHUNK
    scale = HEAD_DIM ** -0.5
    npat = len(DILATIONS)
    n_hp = ATTN_W // LANES

    def body(*refs):
        ((q_ref, kp_ref, kc_ref, vp_ref, vc_ref, o_ref, lse_ref, do_ref), (dq_ref, dk_ref, dv_ref),
         (kk, vv, dkk, dvv, kt_s, vt_s, kn_s, bias_s, dq_s, dl_s, dkb, dvb), hx) = _carry(ex, refs, 8, 3)
        step = pl.program_id(1)
        i = nch - 1 - step
        if ex is not None:
            @pl.when(jnp.logical_and(pl.program_id(0) == 0, step == 0))
            def _():
                ex.start(*hx)

        kk[pl.ds(0, ATTN_CHUNK), :] = kp_ref[...]
        kk[pl.ds(ATTN_CHUNK, ATTN_CHUNK), :] = kc_ref[...]
        vv[pl.ds(0, ATTN_CHUNK), :] = vp_ref[...]
        vv[pl.ds(ATTN_CHUNK, ATTN_CHUNK), :] = vc_ref[...]

        @pl.when(step == 0)
        def _():
            dkk[pl.ds(ATTN_CHUNK, ATTN_CHUNK), :] = jnp.zeros((ATTN_CHUNK, LANES), F32)
            dvv[pl.ds(ATTN_CHUNK, ATTN_CHUNK), :] = jnp.zeros((ATTN_CHUNK, LANES), F32)

        @pl.when(step > 0)
        def _():
            dkk[pl.ds(ATTN_CHUNK, ATTN_CHUNK), :] = dkk[pl.ds(0, ATTN_CHUNK), :]
            dvv[pl.ds(ATTN_CHUNK, ATTN_CHUNK), :] = dvv[pl.ds(0, ATTN_CHUNK), :]

        dkk[pl.ds(0, ATTN_CHUNK), :] = jnp.zeros((ATTN_CHUNK, LANES), F32)
        dvv[pl.ds(0, ATTN_CHUNK), :] = jnp.zeros((ATTN_CHUNK, LANES), F32)
        head0 = _attn_bias(bias_s)

        def delta(t, carry):
            rows = pl.ds(pl.multiple_of(t * ATTN_BLOCK, ATTN_BLOCK), ATTN_BLOCK)
            prod = do_ref[rows, :] * o_ref[rows, :]
            d0 = jnp.sum(jnp.where(head0, prod, 0.0), axis=-1, keepdims=True)
            d1 = jnp.sum(jnp.where(head0, 0.0, prod), axis=-1, keepdims=True)
            dl_s[rows, :] = jnp.where(head0, d0, d1)
            return carry

        lax.fori_loop(0, NBLK, delta, 0, unroll=2)

        first_pen = jnp.where(i > 0, 0.0, NEG_INF)

        for p, d in enumerate(DILATIONS):
            nb = ATTN_CHUNK // (ATTN_BLOCK * d)

            def prep(t, d=d, nb=nb):
                _, _, crows, prows = _attn_rows(t, d, nb)
                kp, kc = kk[prows, :], kk[crows, :]
                kt_s[t, :, pl.ds(0, LANES)] = kp.T.astype(BF16)
                kt_s[t, :, pl.ds(LANES, LANES)] = kc.T.astype(BF16)
                kn_s[t, pl.ds(0, ATTN_BLOCK), :] = (kp * scale).astype(BF16)
                kn_s[t, pl.ds(ATTN_BLOCK, ATTN_BLOCK), :] = (kc * scale).astype(BF16)
                vt_s[t, :, pl.ds(0, LANES)] = vv[prows, :].T.astype(BF16)
                vt_s[t, :, pl.ds(LANES, LANES)] = vv[crows, :].T.astype(BF16)

            def main(tg, p=p, d=d, nb=nb):
                st = []
                for g in range(group):
                    t = tg * group + g
                    b, qrows, _, _ = _attn_rows(t, d, nb)
                    q = q_ref[qrows, :] * scale
                    dout = do_ref[qrows, :]
                    lse_b = lse_ref[qrows, :]
                    dl_b = dl_s[qrows, :]
                    for h in range(2):
                        hm = head0 if h == 0 else jnp.logical_not(head0)
                        c0 = h * HEAD_DIM
                        qh = jnp.where(hm, q, 0.0).astype(BF16)
                        doh = jnp.where(hm, dout, 0.0).astype(BF16)
                        st.append(dict(t=t, b=b, qrows=qrows, qh=qh, doh=doh, lse=lse_b[:, c0:c0 + 1],
                                       dl=dl_b[:, c0:c0 + 1], sc=_dot(qh, kt_s[t]), dp=_dot(doh, vt_s[t])))
                for e in st:
                    sc = e["sc"] + bias_s[...]
                    if e["b"] == 0:
                        sc = jnp.concatenate([sc[:, :LANES] + first_pen, sc[:, LANES:]], axis=1)
                    pr = jnp.exp(sc - e["lse"])
                    e["ds"] = (pr * (e["dp"] - e["dl"])).astype(BF16)
                    e["pr"] = pr.astype(BF16)
                for g in range(group):
                    e0, e1 = st[2 * g], st[2 * g + 1]
                    t = e0["t"]
                    dq_s[p, e0["qrows"], :] = jnp.where(head0, _dot(e0["ds"], kn_s[t]), _dot(e1["ds"], kn_s[t]))
                    dkb[t] = _dot_tn(e0["ds"], e0["qh"]) + _dot_tn(e1["ds"], e1["qh"])
                    dvb[t] = _dot_tn(e0["pr"], e0["doh"]) + _dot_tn(e1["pr"], e1["doh"])

            def scatter(t, d=d, nb=nb):
                _, _, crows, prows = _attn_rows(t, d, nb)
                dkk[prows, :] = dkk[prows, :] + dkb[t, pl.ds(0, ATTN_BLOCK), :]
                dkk[crows, :] = dkk[crows, :] + dkb[t, pl.ds(ATTN_BLOCK, ATTN_BLOCK), :]
                dvv[prows, :] = dvv[prows, :] + dvb[t, pl.ds(0, ATTN_BLOCK), :]
                dvv[crows, :] = dvv[crows, :] + dvb[t, pl.ds(ATTN_BLOCK, ATTN_BLOCK), :]

            n_groups = NBLK // group
            for g in range(group):
                prep(g)
            for tg in range(n_groups):
                if tg + 1 < n_groups:
                    for g in range(group):
                        prep((tg + 1) * group + g)
                main(tg)
                if tg >= 1:
                    for g in range(group):
                        scatter((tg - 1) * group + g)
            for g in range(group):
                scatter((n_groups - 1) * group + g)

        def finish(t, carry):
            rows = pl.ds(pl.multiple_of(t * ATTN_BLOCK, ATTN_BLOCK), ATTN_BLOCK)
            acc = dq_s[0, rows, :]
            for p in range(1, npat):
                acc = acc + dq_s[p, rows, :]
            dq_ref[rows, :] = acc
            return carry

        lax.fori_loop(0, NBLK, finish, 0, unroll=2)
        dk_ref[...] = dkk[pl.ds(ATTN_CHUNK, ATTN_CHUNK), :]
        dv_ref[...] = dvv[pl.ds(ATTN_CHUNK, ATTN_CHUNK), :]
        if ex is not None:
            @pl.when(jnp.logical_and(pl.program_id(0) == n_hp - 1, step == nch - 1))
            def _():
                ex.wait(*hx)

    cur = pl.BlockSpec((ATTN_CHUNK, LANES), lambda h, t: (nch - 1 - t, h))
    prev = pl.BlockSpec((ATTN_CHUNK, LANES), lambda h, t: (jnp.maximum(nch - 2 - t, 0), h))
    big = pltpu.VMEM((2 * ATTN_CHUNK, LANES), F32)
    pair_t = pltpu.VMEM((NBLK, LANES, 2 * LANES), BF16)
    hosted = ex is not None
    return pl.pallas_call(
        body, name="attn_bwd", grid=(n_hp, nch),
        in_specs=[cur, prev, cur, prev, cur, cur, cur, cur] + (ex.specs() if hosted else []),
        out_specs=[cur, cur, cur] + (ex.specs() if hosted else []),
        out_shape=[_sds((s, ATTN_W))] * 3 + (ex.out_shape() if hosted else []),
        scratch_shapes=[big, big, big, big, pair_t, pair_t, pltpu.VMEM((NBLK, 2 * ATTN_BLOCK, LANES), BF16),
                        pltpu.VMEM((ATTN_BLOCK, 2 * LANES), F32),
                        pltpu.VMEM((npat, ATTN_CHUNK, LANES), F32), pltpu.VMEM((ATTN_CHUNK, LANES), F32),
                        pltpu.VMEM((NBLK, 2 * ATTN_BLOCK, LANES), F32), pltpu.VMEM((NBLK, 2 * ATTN_BLOCK, LANES), F32)]
        + (ex.scratch() if hosted else []),
        compiler_params=_cp(("arbitrary", "arbitrary")),
    )(qn, kn, kn, v, v, o, lse, do, *(ex.srcs if hosted else []))


def _discretize(lr, li, dt):
    mag = jnp.exp(lr * dt)
    abr = mag * jnp.cos(li * dt)
    abi = mag * jnp.sin(li * dt)
    den = lr * lr + li * li
    nr, ni = abr - 1.0, abi
    cr = (nr * lr + ni * li) / den
    ci = (ni * lr - nr * li) / den
    return abr, abi, den, nr, ni, cr, ci


def _ssm_discretize(a_re, a_im, log_dt, b_re_t, b_im_t):
    def body(ar_ref, ai_ref, ldt_ref, br_ref, bi_ref, abr_ref, abi_ref, bbr_ref, bbi_ref):
        abr, abi, _, _, _, cr, ci = _discretize(ar_ref[...], ai_ref[...], jnp.exp(ldt_ref[...]))
        br, bi = br_ref[...], bi_ref[...]
        abr_ref[...] = abr
        abi_ref[...] = abi
        bbr_ref[...] = cr * br - ci * bi
        bbi_ref[...] = cr * bi + ci * br

    return pl.pallas_call(
        body, name="ssm_discretize",
        out_shape=[_sds(a_re.shape)] * 2 + [_sds(b_re_t.shape)] * 2,
    )(a_re, a_im, log_dt, b_re_t, b_im_t)


def _ssm_discretize_bwd(a_re, a_im, log_dt, b_re_t, b_im_t, dabr, dabi, dbbr, dbbi):
    def body(ar_ref, ai_ref, ldt_ref, br_ref, bi_ref, dabr_ref, dabi_ref, dbbr_ref, dbbi_ref,
             dar_ref, dai_ref, dldt_ref, dbr_ref, dbi_ref):
        lr, li = ar_ref[...], ai_ref[...]
        dt = jnp.exp(ldt_ref[...])
        abr, abi, den, nr, ni, cr, ci = _discretize(lr, li, dt)
        br, bi = br_ref[...], bi_ref[...]
        gbr, gbi = dbbr_ref[...], dbbi_ref[...]
        dcr = jnp.sum(gbr * br + gbi * bi, axis=1, keepdims=True)
        dci = jnp.sum(gbi * br - gbr * bi, axis=1, keepdims=True)
        dbr_ref[...] = cr * gbr + ci * gbi
        dbi_ref[...] = cr * gbi - ci * gbr
        dnr = (dcr * lr - dci * li) / den
        dni = (dcr * li + dci * lr) / den
        dden = -(dcr * cr + dci * ci) / den
        dlr = (dcr * nr + dci * ni) / den + dden * 2.0 * lr
        dli = (dcr * ni - dci * nr) / den + dden * 2.0 * li
        gabr = dabr_ref[...] + dnr
        gabi = dabi_ref[...] + dni
        dphi = gabr * abr + gabi * abi
        dth = gabi * abr - gabr * abi
        dar_ref[...] = dlr + dphi * dt
        dai_ref[...] = dli + dth * dt
        dldt_ref[...] = jnp.sum(dphi * lr + dth * li, axis=2, keepdims=True) * dt

    return pl.pallas_call(
        body, name="ssm_discretize_bwd",
        out_shape=[_sds(a_re.shape)] * 2 + [_sds(log_dt.shape)] + [_sds(b_re_t.shape)] * 2,
    )(a_re, a_im, log_dt, b_re_t, b_im_t, dabr, dabi, dbbr, dbbi)


def _ssm_power(abr, abi, n_sq):
    def body(r_ref, i_ref, or_ref, oi_ref):
        r, i = r_ref[...], i_ref[...]
        for _ in range(n_sq):
            r, i = r * r - i * i, 2.0 * r * i
        or_ref[...] = r
        oi_ref[...] = i

    return pl.pallas_call(body, name="ssm_power", out_shape=[_sds(abr.shape)] * 2)(abr, abi)


N_CB = SSM_W // LANES
CB_STATES = N_STATE // N_CB
ROWS = N_SEG * SSM_LK


class _Neg:
    def __init__(self, ref):
        self.ref = ref

    def __getitem__(self, idx):
        return -self.ref[idx]


def _seg_init(fin_r, fin_i, pw_r, pw_i, x_r, x_i, reverse):
    zero = jnp.zeros((1, N_STATE), F32)
    cr, ci = zero, zero
    order = range(N_SEG - 1, -1, -1) if reverse else range(N_SEG)
    pr = pw_r[...]
    pi = -pw_i[...] if reverse else pw_i[...]
    for j in order:
        x_r[pl.ds(j, 1), :] = cr
        x_i[pl.ds(j, 1), :] = ci
        fr, fi = fin_r[pl.ds(j, 1), :], fin_i[pl.ds(j, 1), :]
        cr, ci = fr + pr * cr - pi * ci, fi + pr * ci + pi * cr


def _permute_in(src_ref, dst):
    for c in range(N_CB):
        dst[c] = src_ref[:, :, pl.ds(c * LANES, LANES)].reshape(ROWS, LANES)


def _permute_out(src, dst_ref):
    for c in range(N_CB):
        dst_ref[:, :, pl.ds(c * LANES, LANES)] = src[c].reshape(SSM_LK, N_SEG, LANES)


def _scan_block(a_r, a_i, c, b_r, b_i, b_off, x_r, x_i, reverse, acc=None):
    cols = pl.ds(c * CB_STATES, CB_STATES)
    ar = jnp.broadcast_to(a_r[:, cols], (N_SEG, CB_STATES))
    ai = jnp.broadcast_to(a_i[:, cols], (N_SEG, CB_STATES))
    xr, xi = x_r[:, cols], x_i[:, cols]
    if acc is not None:
        sr = jnp.zeros((N_SEG, CB_STATES), F32)
        si = jnp.zeros((N_SEG, CB_STATES), F32)
    for t in range(SSM_LK):
        k = (SSM_LK - 1 - t) if reverse else t
        rows = pl.ds(k * N_SEG + b_off, N_SEG)
        xr, xi = ar * xr - ai * xi + b_r[rows, :], ar * xi + ai * xr + b_i[rows, :]
        b_r[rows, :] = xr
        b_i[rows, :] = xi
        if acc is not None:
            pr, pi = acc[0][pl.ds(k * N_SEG, N_SEG), :], acc[1][pl.ds(k * N_SEG, N_SEG), :]
            sr = sr + xr * pr + xi * pi
            si = si + xi * pr - xr * pi
    x_r[:, cols] = xr
    x_i[:, cols] = xi
    if acc is not None:
        acc[2][:, cols] += sr
        acc[3][:, cols] += si


def _ssm_fwd(u3, abr, abi, pw_r, pw_i, fin_r, fin_i, bb_r, bb_i, cc_r, cc_i, dskip, finals_only):
    sl = u3.shape[0]
    nch = sl // SSM_LK

    def body(u_ref, abr_ref, abi_ref, pwr_ref, pwi_ref, finr_ref, fini_ref, bbr_ref, bbi_ref,
             ccr_ref, cci_ref, d_ref, *rest):
        if finals_only:
            xfr_ref, xfi_ref, up, x_r, x_i = rest[:5]
        else:
            y_ref, xsr_ref, xsi_ref, up, yp, x_r, x_i = rest[:7]
        xs_r, xs_i = rest[-2 * N_CB:-N_CB], rest[-N_CB:]
        k = pl.program_id(0)

        @pl.when(k == 0)
        def _():
            _seg_init(finr_ref, fini_ref, pwr_ref, pwi_ref, x_r, x_i, False)

        if not finals_only:
            xsr_ref[0] = x_r[...]
            xsi_ref[0] = x_i[...]
        _permute_in(u_ref, up)

        def drive(c):
            lhs = up[c].astype(BF16)
            xs_r[c][...] = _dot(lhs, bbr_ref[c])
            xs_i[c][...] = _dot(lhs, bbi_ref[c])

        def readout(c):
            yp[c] = (_dot(xs_r[c][...].astype(BF16), ccr_ref[c]) - _dot(xs_i[c][...].astype(BF16), cci_ref[c])
                     + d_ref[:, pl.ds(c * LANES, LANES)] * up[c])

        drive(0)
        for c in range(N_CB):
            if c + 1 < N_CB:
                drive(c + 1)
            if c >= 1 and not finals_only:
                readout(c - 1)
            _scan_block(abr_ref, abi_ref, c, xs_r[c], xs_i[c], 0, x_r, x_i, False)
        if finals_only:
            @pl.when(k == nch - 1)
            def _():
                xfr_ref[...] = x_r[...]
                xfi_ref[...] = x_i[...]
        else:
            readout(N_CB - 1)
            _permute_out(yp, y_ref)

    ublk = pl.BlockSpec((SSM_LK, N_SEG, SSM_W), lambda k: (k, 0, 0))
    st = pl.BlockSpec((1, N_SEG, N_STATE), lambda k: (k, 0, 0))
    vec = _full((1, N_STATE))
    mat = _full((N_SEG, N_STATE))
    chunk = pltpu.VMEM((N_CB, ROWS, LANES), F32)
    blocks = [pltpu.VMEM((ROWS, CB_STATES), F32)] * (2 * N_CB)
    small = pltpu.VMEM((N_SEG, N_STATE), F32)
    if finals_only:
        out_specs, out_shape = [mat, mat], [_sds((N_SEG, N_STATE))] * 2
        scratch, name = [chunk, small, small] + blocks, "ssm_fwd_finals"
    else:
        out_specs = [ublk, st, st]
        out_shape = [_sds(u3.shape)] + [_sds((nch, N_SEG, N_STATE))] * 2
        scratch, name = [chunk, chunk, small, small] + blocks, "ssm_fwd"
    return pl.pallas_call(
        body, name=name, grid=(nch,),
        in_specs=[ublk, vec, vec, vec, vec, mat, mat,
                  _full((N_CB, LANES, CB_STATES)), _full((N_CB, LANES, CB_STATES)),
                  _full((N_CB, CB_STATES, LANES)), _full((N_CB, CB_STATES, LANES)), _full((1, SSM_W))],
        out_specs=out_specs, out_shape=out_shape, scratch_shapes=scratch,
        compiler_params=_cp(("arbitrary",)),
    )(u3, abr, abi, pw_r, pw_i, fin_r, fin_i, bb_r, bb_i, cc_r, cc_i, dskip)


def _ssm_bwd(u3, dy3, xst_r, xst_i, abr, abi, pw_r, pw_i, fin_r, fin_i, bb_r, bb_i, bbt_r, bbt_i,
             cct_r, cct_i, dskip, finals_only):
    sl = u3.shape[0]
    nch = sl // SSM_LK

    def body(u_ref, g_ref, xsr_ref, xsi_ref, abr_ref, abi_ref, pwr_ref, pwi_ref,
             finr_ref, fini_ref, bbr_ref, bbi_ref, btr_ref, bti_ref, ctr_ref, cti_ref, d_ref, *rest):
        if finals_only:
            lfr_ref, lfi_ref, gp, lam_r, lam_i = rest[:5]
            l_r, l_i = rest[-2 * N_CB:-N_CB], rest[-N_CB:]
        else:
            (du_ref, dar_ref, dai_ref, dbr_ref, dbi_ref, dcr_ref, dci_ref, dd_ref,
             gp, up, yp, lam_r, lam_i, x_r, x_i, sar, sai, sdd) = rest[:18]
            l_r, l_i = rest[18:18 + N_CB], rest[18 + N_CB:18 + 2 * N_CB]
            xx_r, xx_i = rest[18 + 2 * N_CB:18 + 3 * N_CB], rest[18 + 3 * N_CB:]
        t = pl.program_id(0)

        @pl.when(t == 0)
        def _():
            _seg_init(finr_ref, fini_ref, pwr_ref, pwi_ref, lam_r, lam_i, True)
            if not finals_only:
                sar[...] = jnp.zeros_like(sar)
                sai[...] = jnp.zeros_like(sai)
                sdd[...] = jnp.zeros_like(sdd)
                dbr_ref[...] = jnp.zeros_like(dbr_ref)
                dbi_ref[...] = jnp.zeros_like(dbi_ref)
                dcr_ref[...] = jnp.zeros_like(dcr_ref)
                dci_ref[...] = jnp.zeros_like(dci_ref)

        _permute_in(g_ref, gp)
        if not finals_only:
            _permute_in(u_ref, up)
            x_r[...] = xsr_ref[0]
            x_i[...] = xsi_ref[0]

        def drive(c):
            lhs = gp[c].astype(BF16)
            l_r[c][...] = _dot(lhs, ctr_ref[c])
            l_i[c][...] = -_dot(lhs, cti_ref[c])
            if not finals_only:
                cols = pl.ds(c * CB_STATES, CB_STATES)
                xx_r[c][pl.ds(0, N_SEG), :] = x_r[:, cols]
                xx_i[c][pl.ds(0, N_SEG), :] = x_i[:, cols]
                ub = up[c].astype(BF16)
                xx_r[c][pl.ds(N_SEG, ROWS), :] = _dot(ub, bbr_ref[c])
                xx_i[c][pl.ds(N_SEG, ROWS), :] = _dot(ub, bbi_ref[c])

        def collect(c):
            lrb = l_r[c][...].astype(BF16)
            lib = l_i[c][...].astype(BF16)
            ub = up[c].astype(BF16)
            gb = gp[c].astype(BF16)
            dbr_ref[c] += _dot_tn(lrb, ub)
            dbi_ref[c] += _dot_tn(lib, ub)
            dcr_ref[c] += _dot_tn(gb, xx_r[c][pl.ds(N_SEG, ROWS), :].astype(BF16))
            dci_ref[c] += -_dot_tn(gb, xx_i[c][pl.ds(N_SEG, ROWS), :].astype(BF16))
            yp[c] = _dot(lrb, btr_ref[c]) + _dot(lib, bti_ref[c]) + d_ref[:, pl.ds(c * LANES, LANES)] * gp[c]
            prod = gp[c] * up[c]
            sdd[:, pl.ds(c * LANES, LANES)] += jnp.sum(prod.reshape(SSM_LK, N_SEG, LANES), axis=0)

        drive(0)
        for c in range(N_CB):
            if c + 1 < N_CB:
                drive(c + 1)
            if finals_only:
                _scan_block(abr_ref, _Neg(abi_ref), c, l_r[c], l_i[c], 0, lam_r, lam_i, True)
            else:
                if c >= 1:
                    collect(c - 1)
                _scan_block(abr_ref, abi_ref, c, xx_r[c], xx_i[c], N_SEG, x_r, x_i, False)
                _scan_block(abr_ref, _Neg(abi_ref), c, l_r[c], l_i[c], 0, lam_r, lam_i, True,
                            acc=(xx_r[c], xx_i[c], sar, sai))
        if finals_only:
            @pl.when(t == nch - 1)
            def _():
                lfr_ref[...] = lam_r[...]
                lfi_ref[...] = lam_i[...]
        else:
            collect(N_CB - 1)
            _permute_out(yp, du_ref)

            @pl.when(t == nch - 1)
            def _():
                dar_ref[...] = jnp.sum(sar[...], axis=0, keepdims=True)
                dai_ref[...] = jnp.sum(sai[...], axis=0, keepdims=True)
                dd_ref[...] = jnp.sum(sdd[...], axis=0, keepdims=True)

    ublk = pl.BlockSpec((SSM_LK, N_SEG, SSM_W), lambda t: (nch - 1 - t, 0, 0))
    st = pl.BlockSpec((1, N_SEG, N_STATE), lambda t: (nch - 1 - t, 0, 0))
    vec = _full((1, N_STATE))
    mat = _full((N_SEG, N_STATE))
    cs = _full((N_CB, LANES, CB_STATES))
    sc = _full((N_CB, CB_STATES, LANES))
    in_specs = [ublk, ublk, st, st, vec, vec, vec, vec, mat, mat, cs, cs, sc, sc, cs, cs, _full((1, SSM_W))]
    chunk = pltpu.VMEM((N_CB, ROWS, LANES), F32)
    blocks = [pltpu.VMEM((ROWS, CB_STATES), F32)] * (2 * N_CB)
    small = pltpu.VMEM((N_SEG, N_STATE), F32)
    if finals_only:
        out_specs, out_shape = [mat, mat], [_sds((N_SEG, N_STATE))] * 2
        scratch, name = [chunk, small, small] + blocks, "ssm_bwd_finals"
    else:
        out_specs = [ublk, vec, vec, sc, sc, cs, cs, _full((1, SSM_W))]
        out_shape = ([_sds(u3.shape), _sds((1, N_STATE)), _sds((1, N_STATE))]
                     + [_sds((N_CB, CB_STATES, LANES))] * 2 + [_sds((N_CB, LANES, CB_STATES))] * 2
                     + [_sds((1, SSM_W))])
        scratch = ([chunk, chunk, chunk, small, small, small, small, small, small, pltpu.VMEM((N_SEG, SSM_W), F32)]
                   + blocks + [pltpu.VMEM((ROWS + N_SEG, CB_STATES), F32)] * (2 * N_CB))
        name = "ssm_bwd"
    return pl.pallas_call(
        body, name=name, grid=(nch,), in_specs=in_specs, out_specs=out_specs, out_shape=out_shape,
        scratch_shapes=scratch, compiler_params=_cp(("arbitrary",)),
    )(u3, dy3, xst_r, xst_i, abr, abi, pw_r, pw_i, fin_r, fin_i, bb_r, bb_i, bbt_r, bbt_i, cct_r, cct_i, dskip)


def _row(tm, w):
    return pl.BlockSpec((tm, w), lambda i: (i, 0))


def _acc_rows(ref, rows, first):
    @pl.when(first)
    def _():
        ref[...] = jnp.zeros_like(ref)

    ref[...] += jnp.sum(rows, axis=0, keepdims=True)


def _fwd_mix(attn, y, x, glu_w, glu_b, ga, gs, w_out, g2, tm=1024):
    s = x.shape[0]

    def body(a_ref, y_ref, x_ref, gw_ref, gb_ref, ga_ref, gs_ref, wo_ref, g2_ref, mix_ref, x2_ref, h_ref):
        a = a_ref[...]
        anb = ((a * _rms(a)) * ga_ref[...]).astype(BF16)
        z, _ = _gelu(y_ref[...])
        so = z * jax.nn.sigmoid(_dot(z.astype(BF16), gw_ref[...]) + gb_ref[...])
        snb = ((so * _rms(so)) * gs_ref[...]).astype(BF16)
        mix_ref[:, pl.ds(0, ATTN_W)] = anb
        mix_ref[:, pl.ds(ATTN_W, SSM_W)] = snb
        x2 = x_ref[...] + (_dot(anb, wo_ref[pl.ds(0, ATTN_W), :]) + _dot(snb, wo_ref[pl.ds(ATTN_W, SSM_W), :]))
        x2_ref[...] = x2
        h_ref[...] = ((x2 * _rms(x2)) * g2_ref[...]).astype(BF16)

    return pl.pallas_call(
        body, name="fwd_mix", grid=(s // tm,),
        in_specs=[_row(tm, ATTN_W), _row(tm, SSM_W), _row(tm, D_MODEL), _full((SSM_W, SSM_W)), _full((1, SSM_W)),
                  _full((1, ATTN_W)), _full((1, SSM_W)), _full((D_MODEL, D_MODEL)), _full((1, D_MODEL))],
        out_specs=[_row(tm, D_MODEL), _row(tm, D_MODEL), _row(tm, D_MODEL)],
        out_shape=[_sds((s, D_MODEL), BF16), _sds((s, D_MODEL)), _sds((s, D_MODEL), BF16)],
        compiler_params=_cp(("parallel",)),
    )(attn, y, x, glu_w, glu_b, ga, gs, w_out, g2)


def _mlp_up(h, w_up, tm=512, bn=1024):
    s = h.shape[0]

    def body(h_ref, w_ref, r_ref, hdn_ref):
        hv = h_ref[...]
        for j in range(D_FF // bn):
            cols = pl.ds(j * bn, bn)
            r = jnp.maximum(_dot(hv, w_ref[:, cols]), 0.0)
            r_ref[:, cols] = r.astype(BF16)
            hdn_ref[:, cols] = (r * r).astype(BF16)

    return pl.pallas_call(
        body, name="mlp_up", grid=(s // tm,),
        in_specs=[_row(tm, D_MODEL), _full((D_MODEL, D_FF))],
        out_specs=[_row(tm, D_FF), _row(tm, D_FF)], out_shape=[_sds((s, D_FF), BF16)] * 2,
        compiler_params=_cp(("parallel",)),
    )(h, w_up)


def _mlp_down_loss(hdn, w_down, x2, tgt, tm=512):
    s = x2.shape[0]

    def body(hdn_ref, w_ref, x2_ref, t_ref, dy_ref, dyb_ref, sse_ref):
        err = (x2_ref[...] + _dot(hdn_ref[...], w_ref[...])) - t_ref[...]
        dy = err * (1.0 / D_MODEL)
        dy_ref[...] = dy
        dyb_ref[...] = dy.astype(BF16)

        @pl.when(pl.program_id(0) == 0)
        def _():
            sse_ref[...] = jnp.zeros_like(sse_ref)

        sse_ref[...] += jnp.sum(jnp.sum(err * err, axis=0, keepdims=True), axis=1, keepdims=True)

    return pl.pallas_call(
        body, name="mlp_down_loss", grid=(s // tm,),
        in_specs=[_row(tm, D_FF), _full((D_FF, D_MODEL)), _row(tm, D_MODEL), _row(tm, D_MODEL)],
        out_specs=[_row(tm, D_MODEL), _row(tm, D_MODEL), _full((1, 1))],
        out_shape=[_sds((s, D_MODEL)), _sds((s, D_MODEL), BF16), _sds((1, 1))],
        compiler_params=_cp(("arbitrary",)),
    )(hdn, w_down, x2, tgt)


def _mlp_down_bwd(dyb, w_down_t, r, tm=512, bn=1024):
    s = dyb.shape[0]

    def body(dy_ref, w_ref, r_ref, dup_ref):
        dyv = dy_ref[...]
        for j in range(D_FF // bn):
            cols = pl.ds(j * bn, bn)
            dup_ref[:, cols] = (_dot(dyv, w_ref[:, cols]) * (2.0 * r_ref[:, cols].astype(F32))).astype(BF16)

    return pl.pallas_call(
        body, name="mlp_down_bwd", grid=(s // tm,),
        in_specs=[_row(tm, D_MODEL), _full((D_MODEL, D_FF)), _row(tm, D_FF)],
        out_specs=_row(tm, D_FF), out_shape=_sds((s, D_FF), BF16),
        compiler_params=_cp(("parallel",)),
    )(dyb, w_down_t, r)


def _mlp_up_bwd(dup, w_up_t, x2, g2, dy, tm=512):
    s = x2.shape[0]

    def body(dup_ref, w_ref, x2_ref, g2_ref, dy_ref, dx2_ref, dx2b_ref, dg_ref):
        dx, dg_rows = _rms_bwd(_dot(dup_ref[...], w_ref[...]), x2_ref[...], g2_ref[...], D_MODEL)
        dx2 = dy_ref[...] + dx
        dx2_ref[...] = dx2
        dx2b_ref[...] = dx2.astype(BF16)
        _acc_rows(dg_ref, dg_rows, pl.program_id(0) == 0)

    return pl.pallas_call(
        body, name="mlp_up_bwd", grid=(s // tm,),
        in_specs=[_row(tm, D_FF), _full((D_FF, D_MODEL)), _row(tm, D_MODEL), _full((1, D_MODEL)), _row(tm, D_MODEL)],
        out_specs=[_row(tm, D_MODEL), _row(tm, D_MODEL), _full((1, D_MODEL))],
        out_shape=[_sds((s, D_MODEL)), _sds((s, D_MODEL), BF16), _sds((1, D_MODEL))],
        compiler_params=_cp(("arbitrary",)),
    )(dup, w_up_t, x2, g2, dy)


def _mix_bwd(dx2b, w_out_t, attn, y, glu_w, glu_b, glu_w_t, ga, gs, tm=1024):
    s = attn.shape[0]

    def body(dx2_ref, wot_ref, a_ref, y_ref, gw_ref, gb_ref, gwt_ref, ga_ref, gs_ref,
             da_ref, dys_ref, z_ref, dpre_ref, dga_ref, dgs_ref, dgb_ref):
        first = pl.program_id(0) == 0
        dmix = _dot(dx2_ref[...], wot_ref[...])
        da, dga_rows = _rms_bwd(dmix[:, :ATTN_W], a_ref[...], ga_ref[...], ATTN_W)
        da_ref[...] = da
        yv = y_ref[...]
        z, t = _gelu(yv)
        gate = jax.nn.sigmoid(_dot(z.astype(BF16), gw_ref[...]) + gb_ref[...])
        dso, dgs_rows = _rms_bwd(dmix[:, ATTN_W:], z * gate, gs_ref[...], SSM_W)
        dpre = dso * z * gate * (1.0 - gate)
        dpre_b = dpre.astype(BF16)
        dz = dso * gate + _dot(dpre_b, gwt_ref[...])
        dgelu = 0.5 * (1.0 + t) + 0.5 * yv * (1.0 - t * t) * (GELU_C * (1.0 + 3.0 * 0.044715 * (yv * yv)))
        dys_ref[...] = dz * dgelu
        z_ref[...] = z.astype(BF16)
        dpre_ref[...] = dpre_b
        _acc_rows(dga_ref, dga_rows, first)
        _acc_rows(dgs_ref, dgs_rows, first)
        _acc_rows(dgb_ref, dpre, first)

    vec = _full((1, SSM_W))
    return pl.pallas_call(
        body, name="mix_bwd", grid=(s // tm,),
        in_specs=[_row(tm, D_MODEL), _full((D_MODEL, D_MODEL)), _row(tm, ATTN_W), _row(tm, SSM_W),
                  _full((SSM_W, SSM_W)), vec, _full((SSM_W, SSM_W)), vec, vec],
        out_specs=[_row(tm, ATTN_W), _row(tm, SSM_W), _row(tm, SSM_W), _row(tm, SSM_W), vec, vec, vec],
        out_shape=[_sds((s, ATTN_W)), _sds((s, SSM_W)), _sds((s, SSM_W), BF16), _sds((s, SSM_W), BF16),
                   _sds((1, ATTN_W)), _sds((1, SSM_W)), _sds((1, SSM_W))],
        compiler_params=_cp(("arbitrary",)),
    )(dx2b, w_out_t, attn, y, glu_w, glu_b, glu_w_t, ga, gs)


def _qk_bwd(dqn, dkn, qk, dv, du, gq, gk, gmat, tm=1024):
    s = qk.shape[0]

    def body(dq_ref, dk_ref, qk_ref, dv_ref, du_ref, gq_ref, gk_ref, gm_ref, dp_ref, dgq_ref, dgk_ref):
        first = pl.program_id(0) == 0
        gm = gm_ref[...]
        for idx, (d_ref, g_ref, dg_ref) in enumerate(((dq_ref, gq_ref, dgq_ref), (dk_ref, gk_ref, dgk_ref))):
            xv = qk_ref[:, pl.ds(idx * ATTN_W, ATTN_W)]
            dyv = d_ref[...]
            r = lax.rsqrt(_group_mean(xv * xv, gm) + EPS)
            gdy = dyv * g_ref[...]
            dx = r * gdy - xv * (r * r * r) * _group_mean(gdy * xv, gm)
            dp_ref[:, pl.ds(idx * ATTN_W, ATTN_W)] = dx.astype(BF16)
            _acc_rows(dg_ref, dyv * (xv * r), first)
        dp_ref[:, pl.ds(2 * ATTN_W, ATTN_W)] = dv_ref[...].astype(BF16)
        dp_ref[:, pl.ds(3 * ATTN_W, SSM_W)] = du_ref[...].astype(BF16)

    vec = _full((1, ATTN_W))
    return pl.pallas_call(
        body, name="qk_bwd", grid=(s // tm,),
        in_specs=[_row(tm, ATTN_W), _row(tm, ATTN_W), _row(tm, 2 * ATTN_W), _row(tm, ATTN_W), _row(tm, SSM_W),
                  vec, vec, _full((MXU_W, MXU_W))],
        out_specs=[_row(tm, 4 * ATTN_W), vec, vec],
        out_shape=[_sds((s, 4 * ATTN_W), BF16), _sds((1, ATTN_W)), _sds((1, ATTN_W))],
        compiler_params=_cp(("arbitrary",)),
    )(dqn, dkn, qk, dv, du, gq, gk, gmat)


def _in_bwd(dproj, w_in_t, x, g1, dx2, ex=None, tm=1024):
    s = x.shape[0]
    steps = s // tm

    def body(*refs):
        (dp_ref, w_ref, x_ref, g1_ref, dx2_ref), (gx_ref, dg_ref), _, hx = _carry(ex, refs, 5, 2)
        if ex is not None:
            @pl.when(pl.program_id(0) == 0)
            def _():
                ex.start(*hx)

        dx, dg_rows = _rms_bwd(_dot(dp_ref[...], w_ref[...]), x_ref[...], g1_ref[...], D_MODEL)
        gx_ref[...] = dx2_ref[...] + dx
        _acc_rows(dg_ref, dg_rows, pl.program_id(0) == 0)
        if ex is not None:
            @pl.when(pl.program_id(0) == steps - 1)
            def _():
                ex.wait(*hx)

    hosted = ex is not None
    return pl.pallas_call(
        body, name="in_bwd", grid=(steps,),
        in_specs=[_row(tm, 4 * ATTN_W), _full((4 * ATTN_W, D_MODEL)), _row(tm, D_MODEL), _full((1, D_MODEL)),
                  _row(tm, D_MODEL)] + (ex.specs() if hosted else []),
        out_specs=[_row(tm, D_MODEL), _full((1, D_MODEL))] + (ex.specs() if hosted else []),
        out_shape=[_sds((s, D_MODEL)), _sds((1, D_MODEL))] + (ex.out_shape() if hosted else []),
        scratch_shapes=ex.scratch() if hosted else [],
        compiler_params=_cp(("arbitrary",)),
    )(dproj, w_in_t, x, g1, dx2, *(ex.srcs if hosted else []))


def _mm_tn(a, b, name, ts=2048):
    s, k = a.shape
    n = b.shape[1]
    bk, bn = min(k, 1024), min(n, 1024)
    steps = s // ts

    def body(a_ref, b_ref, o_ref, acc):
        t = pl.program_id(2)

        @pl.when(t == 0)
        def _():
            acc[...] = jnp.zeros_like(acc)

        acc[...] += _dot_tn(a_ref[...], b_ref[...])

        @pl.when(t == steps - 1)
        def _():
            o_ref[...] = acc[...].astype(BF16)

    return pl.pallas_call(
        body, name=name, grid=(k // bk, n // bn, steps),
        in_specs=[pl.BlockSpec((ts, bk), lambda i, j, t: (t, i)), pl.BlockSpec((ts, bn), lambda i, j, t: (t, j))],
        out_specs=pl.BlockSpec((bk, bn), lambda i, j, t: (i, j)), out_shape=_sds((k, n), BF16),
        scratch_shapes=[pltpu.VMEM((bk, bn), F32)],
        compiler_params=_cp(("parallel", "parallel", "arbitrary")),
    )(a, b)


def _peer(k):
    x, y, c = lax.axis_index("x"), lax.axis_index("y"), lax.axis_index("c")
    px = 1 - x if k & 4 else x
    py = 1 - y if k & 2 else y
    pc = 1 - c if k & 1 else c
    return (px, py, pc), 4 * px + 2 * py + pc


def _gather_rows(x_shard):
    m_per, n = x_shard.shape

    def body(x_ref, out_ref, send_sems, recv_sems, local_sem):
        x, y, c = lax.axis_index("x"), lax.axis_index("y"), lax.axis_index("c")
        me, sibling = (x, y, c), (x, y, 1 - c)
        chips = [(1 - x, y), (x, 1 - y), (1 - x, 1 - y)]

        def rows(px, py, pc):
            return out_ref.at[pl.ds((4 * px + 2 * py + pc) * m_per, m_per), :]

        def copy(k, block, to, src=None):
            return pltpu.make_async_remote_copy(
                src_ref=rows(*block) if src is None else src, dst_ref=rows(*block),
                send_sem=send_sems.at[k], recv_sem=recv_sems.at[k], device_id=to, device_id_type=MESH)

        mine = pltpu.make_async_copy(x_ref, rows(*me), local_sem)
        mine.start()
        first = [copy(0, me, sibling, src=x_ref)]
        first += [copy(1 + j, me, (*chip, c), src=x_ref) for j, chip in enumerate(chips)]
        for cp in first:
            cp.start()
        passed = [copy(4 + j, (*chip, c), sibling) for j, chip in enumerate(chips)]
        for j, chip in enumerate(chips):
            copy(1 + j, (*chip, c), me).wait_recv()
            passed[j].start()
        copy(0, sibling, me).wait_recv()
        for j, chip in enumerate(chips):
            copy(4 + j, (*chip, 1 - c), me).wait_recv()
        for cp in first + passed:
            cp.wait_send()
        mine.wait()

    return pl.pallas_call(
        body, name="gather_weights", out_shape=_sds((N_DEV * m_per, n), x_shard.dtype),
        in_specs=[pl.BlockSpec(memory_space=pltpu.VMEM)], out_specs=pl.BlockSpec(memory_space=pltpu.VMEM),
        scratch_shapes=[pltpu.SemaphoreType.DMA((7,)), pltpu.SemaphoreType.DMA((7,)), pltpu.SemaphoreType.DMA],
        compiler_params=pltpu.CompilerParams(vmem_limit_bytes=VMEM_LIMIT),
    )(x_shard)


class _Exchange:
    def __init__(self, srcs, whole):
        self.srcs, self.whole, self.n = list(srcs), list(whole), len(srcs)
        self.rows = [a.shape[0] // N_DEV if w is False else a.shape[0] for a, w in zip(self.srcs, self.whole)]
        self.cols = [a.shape[1] // N_DEV if w == "col" else a.shape[1] for a, w in zip(self.srcs, self.whole)]

    def specs(self):
        return [pl.BlockSpec(memory_space=pl.ANY)] * self.n

    def out_shape(self):
        return [_sds((N_DEV, r, c), a.dtype) for r, c, a in zip(self.rows, self.cols, self.srcs)]

    def scratch(self):
        return [pltpu.SemaphoreType.DMA((self.n * 7,)), pltpu.SemaphoreType.DMA((self.n * 7,)),
                pltpu.SemaphoreType.DMA((self.n,))]

    def _copies(self, ins, outs, sems):
        send_sems, recv_sems, local_sems = sems
        _, me = _peer(0)
        for w in range(self.n):
            for k in range(N_DEV):
                peer, pidx = _peer(k)
                if self.whole[w] is True:
                    src = ins[w]
                elif self.whole[w] == "col":
                    src = ins[w].at[:, pl.ds(pl.multiple_of(pidx * self.cols[w], self.cols[w]), self.cols[w])]
                else:
                    src = ins[w].at[pl.ds(pidx * self.rows[w], self.rows[w]), :]
                if k == 0:
                    yield k, pltpu.make_async_copy(src, outs[w].at[me], local_sems.at[w]), None
                else:
                    sem = w * 7 + k - 1
                    out = pltpu.make_async_remote_copy(src_ref=src, dst_ref=outs[w].at[me], send_sem=send_sems.at[sem],
                                                       recv_sem=recv_sems.at[sem], device_id=peer, device_id_type=MESH)
                    back = pltpu.make_async_remote_copy(src_ref=src, dst_ref=outs[w].at[pidx], send_sem=send_sems.at[sem],
                                                        recv_sem=recv_sems.at[sem], device_id=peer, device_id_type=MESH)
                    yield k, out, back

    def start(self, ins, outs, sems):
        for _, out, _ in self._copies(ins, outs, sems):
            out.start()

    def wait(self, ins, outs, sems):
        for k, out, back in self._copies(ins, outs, sems):
            if k == 0:
                out.wait()
            else:
                back.wait_recv()
                out.wait_send()


def _carry(ex, refs, n_in, n_out):
    nh = ex.n if ex is not None else 0
    ins, hin = refs[:n_in], refs[n_in:n_in + nh]
    outs = refs[n_in + nh:n_in + nh + n_out]
    hout = refs[n_in + nh + n_out:n_in + 2 * nh + n_out]
    rest = refs[n_in + 2 * nh + n_out:]
    if ex is None:
        return ins, outs, rest, None
    return ins, outs, rest[:-3], (hin, hout, rest[-3:])


def _exchange_now(srcs, whole, name):
    ex = _Exchange(srcs, whole)

    def body(*refs):
        _, _, _, (hin, hout, sems) = _carry(ex, refs, 0, 0)
        ex.start(hin, hout, sems)
        ex.wait(hin, hout, sems)

    return pl.pallas_call(body, name=name, out_shape=ex.out_shape(), in_specs=ex.specs(), out_specs=ex.specs(),
                          scratch_shapes=ex.scratch())(*srcs)


def _split_exchange_start(src):
    rows, width = src.shape[0], src.shape[1] // N_DEV

    def body(src_ref, land_ref, send_sems, recv_sems, src_thru, land_thru, token):
        barrier = pltpu.get_barrier_semaphore()
        for k in range(1, N_DEV):
            pl.semaphore_signal(barrier, inc=1, device_id=_peer(k)[0], device_id_type=MESH)
        pl.semaphore_wait(barrier, N_DEV - 1)
        _, me = _peer(0)
        for k in range(1, N_DEV):
            peer, pidx = _peer(k)
            pltpu.make_async_remote_copy(
                src_ref=src_ref.at[:, pl.ds(pl.multiple_of(pidx * width, width), width)], dst_ref=land_ref.at[me],
                send_sem=send_sems.at[k - 1], recv_sem=recv_sems.at[k - 1], device_id=peer, device_id_type=MESH).start()
        token[...] = jnp.zeros_like(token)

    hbm = pl.BlockSpec(memory_space=pltpu.HBM)
    sem = pl.BlockSpec(memory_space=pltpu.SEMAPHORE)
    land = lax.empty((N_DEV, rows, width), src.dtype)
    return pl.pallas_call(
        body, name="w_in_exchange_start",
        out_shape=(pltpu.SemaphoreType.DMA((N_DEV - 1,)), pltpu.SemaphoreType.DMA((N_DEV - 1,)),
                   pltpu.HBM(src.shape, src.dtype), pltpu.HBM(land.shape, land.dtype), _sds((8, LANES))),
        in_specs=(hbm, hbm), out_specs=(sem, sem, hbm, hbm, pl.BlockSpec(memory_space=pltpu.VMEM)),
        input_output_aliases={0: 2, 1: 3},
        compiler_params=pltpu.CompilerParams(has_side_effects=pltpu.SideEffectType.DATAFLOW_SIDE_EFFECTING, collective_id=0),
    )(pltpu.with_memory_space_constraint(src, pltpu.HBM), pltpu.with_memory_space_constraint(land, pltpu.HBM))


def _split_exchange_wait(send_sems, recv_sems, src_thru, land_thru, after):
    width = land_thru.shape[2]

    def body(src_ref, land_ref, send_sems, recv_sems, after_ref, src_dead, got_ref):
        _, me = _peer(0)
        for k in range(1, N_DEV):
            peer, pidx = _peer(k)
            copy = pltpu.make_async_remote_copy(
                src_ref=src_ref.at[:, pl.ds(pl.multiple_of(pidx * width, width), width)], dst_ref=land_ref.at[pidx],
                send_sem=send_sems.at[k - 1], recv_sem=recv_sems.at[k - 1], device_id=peer, device_id_type=MESH)
            copy.wait_send()
            copy.wait_recv()

    hbm = pl.BlockSpec(memory_space=pltpu.HBM)
    sem = pl.BlockSpec(memory_space=pltpu.SEMAPHORE)
    return pl.pallas_call(
        body, name="w_in_exchange_wait",
        out_shape=(pltpu.HBM(src_thru.shape, src_thru.dtype), pltpu.HBM(land_thru.shape, land_thru.dtype)),
        in_specs=(hbm, hbm, sem, sem, pl.BlockSpec(memory_space=pl.ANY)), out_specs=(hbm, hbm),
        input_output_aliases={0: 0, 1: 1},
        compiler_params=pltpu.CompilerParams(has_side_effects=pltpu.SideEffectType.DATAFLOW_SIDE_EFFECTING),
    )(src_thru, land_thru, send_sems, recv_sems, after)


def _adamw(w, m, v, gparts, name):
    r, c = w.shape
    tr = r if r * c <= 256 * 1024 else 128 * 1024 // c

    def body(w_ref, m_ref, v_ref, g_ref, go_ref, d_ref, mo_ref, vo_ref):
        g = g_ref[0].astype(F32)
        for i in range(1, N_DEV):
            g = g + g_ref[i].astype(F32)
        go_ref[...] = g
        d_ref[...], mo_ref[...], vo_ref[...] = _adamw_step(w_ref[...], m_ref[...], v_ref[...], g)

    blk = pl.BlockSpec((tr, c), lambda i: (i, 0))
    return pl.pallas_call(
        body, name=name, grid=(r // tr,),
        in_specs=[blk, blk, blk, pl.BlockSpec((N_DEV, tr, c), lambda i: (0, i, 0))],
        out_specs=[blk] * 4, out_shape=[_sds((r, c))] * 4,
        compiler_params=_cp(("parallel",)),
    )(w, m, v, gparts)


def _adamw_step(w, m, v, g):
    nm = ADAM_B1 * m + (1.0 - ADAM_B1) * g
    nv = ADAM_B2 * v + (1.0 - ADAM_B2) * (g * g)
    m_hat = nm / (1.0 - ADAM_B1 ** ADAM_STEP)
    v_hat = nv / (1.0 - ADAM_B2 ** ADAM_STEP)
    return -ADAM_LR * (m_hat / (jnp.sqrt(v_hat) + ADAM_EPS) + ADAM_WD * w), nm, nv


def _sum_slots(parts):
    def body(p_ref, o_ref):
        g = p_ref[0]
        for i in range(1, N_DEV):
            g = g + p_ref[i]
        o_ref[...] = g

    return pl.pallas_call(body, name="sum_small_grads", out_shape=_sds(parts.shape[1:]))(parts)


def _adamw_native(ws, ms, vs, gs):
    n = len(ws)

    def body(*refs):
        for i in range(n):
            w_ref, m_ref, v_ref, g_ref = refs[i], refs[n + i], refs[2 * n + i], refs[3 * n + i]
            d, nm, nv = _adamw_step(w_ref[...], m_ref[...], v_ref[...], g_ref[...])
            refs[4 * n + i][...] = d
            refs[5 * n + i][...] = nm
            refs[6 * n + i][...] = nv

    outs = pl.pallas_call(body, name="adamw_small", out_shape=[_sds(w.shape) for w in ws] * 3,
                          compiler_params=pltpu.CompilerParams(vmem_limit_bytes=VMEM_LIMIT))(*ws, *ms, *vs, *gs)
    return outs[:n], outs[n:2 * n], outs[2 * n:]


def _block_diag(a, states_first):
    a4 = a.reshape(N_CB, 8, SSM_GROUP, SSM_STATE)
    eye = jnp.eye(8, dtype=a.dtype)
    if states_first:
        return jnp.einsum("bgcp,gh->bgphc", a4, eye).reshape(N_CB, CB_STATES, LANES)
    return jnp.einsum("bgcp,gh->bgchp", a4, eye).reshape(N_CB, LANES, CB_STATES)


def _block_diag_of(full, states_first):
    if states_first:
        picked = jnp.einsum("bgphc,gh->bgcp", full.reshape(N_CB, 8, SSM_STATE, 8, SSM_GROUP), jnp.eye(8, dtype=full.dtype))
    else:
        picked = jnp.einsum("bgchp,gh->bgcp", full.reshape(N_CB, 8, SSM_GROUP, 8, SSM_STATE), jnp.eye(8, dtype=full.dtype))
    return picked.reshape(SSM_GROUPS, SSM_GROUP, SSM_STATE)


SMALL_EARLY = ("ssm_a_re", "ssm_a_im", "ssm_log_dt", "ssm_b_re", "ssm_b_im", "ssm_c_re", "ssm_c_im", "ssm_d", "glu_b",
               "attn_out_norm_g", "ssm_out_norm_g", "norm2_g")
SMALL_MID = ("q_norm_g", "k_norm_g")
SMALL_LATE = ("norm1_g",)
SMALL = SMALL_EARLY + SMALL_MID + SMALL_LATE


def _pack_small(arrs):
    parts = []
    for a in arrs:
        flat = a.reshape(-1)
        rows = -(-flat.shape[0] // (8 * LANES)) * 8
        parts.append(jnp.pad(flat, (0, rows * LANES - flat.shape[0])).reshape(rows, LANES))
    return jnp.concatenate(parts, axis=0)


def _unpack_small(packed, shapes):
    out, r0 = [], 0
    for shp in shapes:
        size = math.prod(shp)
        rows = -(-size // (8 * LANES)) * 8
        out.append(packed[r0:r0 + rows].reshape(-1)[:size].reshape(shp))
        r0 += rows
    return out


def kernel(x, norm1_g, w_in, q_norm_g, k_norm_g, ssm_a_re, ssm_a_im, ssm_log_dt, ssm_b_re, ssm_b_im, ssm_c_re, ssm_c_im, ssm_d, glu_w, glu_b, attn_out_norm_g, ssm_out_norm_g, w_out, norm2_g, w_mlp_up, w_mlp_down, loss_target, m_norm1_g, m_w_in, m_q_norm_g, m_k_norm_g, m_ssm_a_re, m_ssm_a_im, m_ssm_log_dt, m_ssm_b_re, m_ssm_b_im, m_ssm_c_re, m_ssm_c_im, m_ssm_d, m_glu_w, m_glu_b, m_attn_out_norm_g, m_ssm_out_norm_g, m_w_out, m_norm2_g, m_w_mlp_up, m_w_mlp_down, v_norm1_g, v_w_in, v_q_norm_g, v_k_norm_g, v_ssm_a_re, v_ssm_a_im, v_ssm_log_dt, v_ssm_b_re, v_ssm_b_im, v_ssm_c_re, v_ssm_c_im, v_ssm_d, v_glu_w, v_glu_b, v_attn_out_norm_g, v_ssm_out_norm_g, v_w_out, v_norm2_g, v_w_mlp_up, v_w_mlp_down):
    weights = dict(norm1_g=norm1_g, w_in=w_in, q_norm_g=q_norm_g, k_norm_g=k_norm_g, ssm_a_re=ssm_a_re,
                   ssm_a_im=ssm_a_im, ssm_log_dt=ssm_log_dt, ssm_b_re=ssm_b_re, ssm_b_im=ssm_b_im,
                   ssm_c_re=ssm_c_re, ssm_c_im=ssm_c_im, ssm_d=ssm_d, glu_w=glu_w, glu_b=glu_b,
                   attn_out_norm_g=attn_out_norm_g, ssm_out_norm_g=ssm_out_norm_g, w_out=w_out, norm2_g=norm2_g,
                   w_mlp_up=w_mlp_up, w_mlp_down=w_mlp_down)
    mom_m = dict(norm1_g=m_norm1_g, w_in=m_w_in, q_norm_g=m_q_norm_g, k_norm_g=m_k_norm_g, ssm_a_re=m_ssm_a_re,
                 ssm_a_im=m_ssm_a_im, ssm_log_dt=m_ssm_log_dt, ssm_b_re=m_ssm_b_re, ssm_b_im=m_ssm_b_im,
                 ssm_c_re=m_ssm_c_re, ssm_c_im=m_ssm_c_im, ssm_d=m_ssm_d, glu_w=m_glu_w, glu_b=m_glu_b,
                 attn_out_norm_g=m_attn_out_norm_g, ssm_out_norm_g=m_ssm_out_norm_g, w_out=m_w_out,
                 norm2_g=m_norm2_g, w_mlp_up=m_w_mlp_up, w_mlp_down=m_w_mlp_down)
    mom_v = dict(norm1_g=v_norm1_g, w_in=v_w_in, q_norm_g=v_q_norm_g, k_norm_g=v_k_norm_g, ssm_a_re=v_ssm_a_re,
                 ssm_a_im=v_ssm_a_im, ssm_log_dt=v_ssm_log_dt, ssm_b_re=v_ssm_b_re, ssm_b_im=v_ssm_b_im,
                 ssm_c_re=v_ssm_c_re, ssm_c_im=v_ssm_c_im, ssm_d=v_ssm_d, glu_w=v_glu_w, glu_b=v_glu_b,
                 attn_out_norm_g=v_attn_out_norm_g, ssm_out_norm_g=v_ssm_out_norm_g, w_out=v_w_out,
                 norm2_g=v_norm2_g, w_mlp_up=v_w_mlp_up, w_mlp_down=v_w_mlp_down)
    order = list(weights)

    xs, tgt = x[0], loss_target[0]
    s = xs.shape[0]
    assert s % ATTN_CHUNK == 0 and (s // N_SEG) % SSM_LK == 0
    seg_len = s // N_SEG
    n_sq = seg_len.bit_length() - 1
    assert 1 << n_sq == seg_len

    w_in_t = _gather_rows(w_in[0].T.astype(BF16))
    w_in_full = w_in_t.T
    later = _Exchange([glu_w[0].astype(BF16), w_out[0].astype(BF16), w_mlp_up[0].T.astype(BF16),
                       w_mlp_down[0].astype(BF16)], [True] * 4)

    gq = jnp.tile(q_norm_g[0], ATTN_W // HEAD_DIM)[None]
    gk = jnp.tile(k_norm_g[0], ATTN_W // HEAD_DIM)[None]
    lane = jnp.arange(MXU_W) // HEAD_DIM
    gmat = jnp.where(lane[:, None] == lane[None, :], 1.0 / HEAD_DIM, 0.0).astype(BF16)
    a_re3 = ssm_a_re[0][:, None, :]
    a_im3 = ssm_a_im[0][:, None, :]
    ldt3 = ssm_log_dt[0][:, None, None]
    b_re_t = jnp.swapaxes(ssm_b_re[0], 1, 2)
    b_im_t = jnp.swapaxes(ssm_b_im[0], 1, 2)
    c_re, c_im = ssm_c_re[0], ssm_c_im[0]
    dskip = ssm_d[0].reshape(1, SSM_W)

    qk, qn, kn, v, u, xn = _fwd_proj(xs, norm1_g, w_in_full, gq, gk, gmat)
    attn, lse, glu_g, w_out_g, w_up_g, w_down_g = _attn_fwd(qn, kn, v, later)
    glu_full = glu_g.reshape(SSM_W, SSM_W)
    w_out_full = w_out_g.reshape(D_MODEL, D_MODEL)
    w_up_t = w_up_g.reshape(D_FF, D_MODEL)
    w_down_full = w_down_g.reshape(D_FF, D_MODEL)
    w_up_full, w_out_t, w_down_t, glu_t = w_up_t.T, w_out_full.T, w_down_full.T, glu_full.T

    abr3, abi3, bbr, bbi = _ssm_discretize(a_re3, a_im3, ldt3, b_re_t, b_im_t)
    abr, abi = abr3.reshape(1, N_STATE), abi3.reshape(1, N_STATE)
    pw_r, pw_i = _ssm_power(abr, abi, n_sq)
    bb_r, bb_i = _block_diag(bbr, False).astype(BF16), _block_diag(bbi, False).astype(BF16)
    bbt_r, bbt_i = _block_diag(bbr, True).astype(BF16), _block_diag(bbi, True).astype(BF16)
    cc_r, cc_i = _block_diag(c_re, True).astype(BF16), _block_diag(c_im, True).astype(BF16)
    cct_r, cct_i = _block_diag(c_re, False).astype(BF16), _block_diag(c_im, False).astype(BF16)
    seg_major = lambda a: jnp.swapaxes(a.reshape(N_SEG, seg_len, SSM_W), 0, 1)
    seg_minor = lambda a: jnp.swapaxes(a, 0, 1).reshape(s, SSM_W)
    u3 = seg_major(u)
    zero_fin = jnp.zeros((N_SEG, N_STATE), F32)
    ssm_args = (abr, abi, pw_r, pw_i)
    xf_r, xf_i = _ssm_fwd(u3, *ssm_args, zero_fin, zero_fin, bb_r, bb_i, cc_r, cc_i, dskip, True)
    y3, xst_r, xst_i = _ssm_fwd(u3, *ssm_args, xf_r, xf_i, bb_r, bb_i, cc_r, cc_i, dskip, False)
    y = seg_minor(y3)

    mix, x2, h = _fwd_mix(attn, y, xs, glu_full, glu_b, attn_out_norm_g, ssm_out_norm_g, w_out_full, norm2_g)
    r_act, hdn = _mlp_up(h, w_up_full)
    dy, dyb, sse = _mlp_down_loss(hdn, w_down_full, x2, tgt)

    dup = _mlp_down_bwd(dyb, w_down_t, r_act)
    g_w_down = _mm_tn(hdn, dyb, "grad_w_down")
    dx2, dx2b, g_norm2 = _mlp_up_bwd(dup, w_up_t, x2, norm2_g, dy)
    g_w_up = _mm_tn(h, dup, "grad_w_up")
    dattn, dys, z_b, dpre_b, g_ga, g_gs, g_glu_b = _mix_bwd(dx2b, w_out_t, attn, y, glu_full, glu_b, glu_t,
                                                             attn_out_norm_g, ssm_out_norm_g)
    g_w_out = _mm_tn(mix, dx2b, "grad_w_out")
    g_glu_w = _mm_tn(z_b, dpre_b, "grad_glu_w")
    dy3 = seg_major(dys)
    bwd_args = (u3, dy3, xst_r, xst_i, abr, abi, pw_r, pw_i)
    lf_r, lf_i = _ssm_bwd(*bwd_args, zero_fin, zero_fin, bb_r, bb_i, bbt_r, bbt_i, cct_r, cct_i, dskip, True)
    du3, dab_r, dab_i, dbb_r, dbb_i, dcc_r, dcc_i, g_d = _ssm_bwd(*bwd_args, lf_r, lf_i, bb_r, bb_i, bbt_r, bbt_i,
                                                                 cct_r, cct_i, dskip, False)
    g_a_re3, g_a_im3, g_ldt3, g_b_re_t, g_b_im_t = _ssm_discretize_bwd(
        a_re3, a_im3, ldt3, b_re_t, b_im_t, dab_r.reshape(a_re3.shape), dab_i.reshape(a_re3.shape),
        _block_diag_of(dbb_r, True), _block_diag_of(dbb_i, True))
    g_c_re, g_c_im = _block_diag_of(dcc_r, False), _block_diag_of(dcc_i, False)
    small_grads = dict(
        ssm_a_re=g_a_re3.reshape(ssm_a_re.shape), ssm_a_im=g_a_im3.reshape(ssm_a_im.shape),
        ssm_log_dt=g_ldt3.reshape(ssm_log_dt.shape), ssm_b_re=jnp.swapaxes(g_b_re_t, 1, 2)[None],
        ssm_b_im=jnp.swapaxes(g_b_im_t, 1, 2)[None], ssm_c_re=g_c_re[None], ssm_c_im=g_c_im[None],
        ssm_d=g_d.reshape(ssm_d.shape), glu_b=g_glu_b, attn_out_norm_g=g_ga, ssm_out_norm_g=g_gs, norm2_g=g_norm2)

    early = _Exchange([g_glu_w, g_w_out, g_w_up, g_w_down, _pack_small([small_grads[n] for n in SMALL_EARLY] + [sse])],
                      [False, False, "col", False, True])
    dqn, dkn, dv, p_glu, p_w_out, p_w_up, p_w_down, p_early = _attn_bwd(qn, kn, v, attn, lse, dattn, early)

    dproj, g_gq, g_gk = _qk_bwd(dqn, dkn, qk, dv, seg_minor(du3), gq, gk, gmat)
    g_w_in = _mm_tn(xn, dproj, "grad_w_in")
    small_grads["q_norm_g"] = g_gq.reshape(ATTN_W // HEAD_DIM, HEAD_DIM).sum(0)[None]
    small_grads["k_norm_g"] = g_gk.reshape(ATTN_W // HEAD_DIM, HEAD_DIM).sum(0)[None]
    send_sems, recv_sems, g_thru, land, token = _split_exchange_start(g_w_in)
    grad_x, g_norm1 = _in_bwd(dproj, w_in_t, xs, norm1_g + token[:1, :1], dx2)
    g_w_in, landed = _split_exchange_wait(send_sems, recv_sems, g_thru, land, g_norm1)
    me = 4 * lax.axis_index("x") + 2 * lax.axis_index("y") + lax.axis_index("c")
    shard_w = g_w_in.shape[1] // N_DEV
    own = lax.dynamic_slice(g_w_in, (0, me * shard_w), (g_w_in.shape[0], shard_w))
    p_w_in = lax.dynamic_update_slice(landed, own[None], (me, 0, 0))
    p_mid, p_late = _exchange_now([_pack_small([small_grads[n] for n in SMALL_MID]), _pack_small([g_norm1])],
                                  [True, True], "exchange_tail")

    res = {}
    for name, gp in (("w_in", p_w_in), ("glu_w", p_glu), ("w_out", p_w_out), ("w_mlp_up", p_w_up), ("w_mlp_down", p_w_down)):
        outs = _adamw(weights[name][0], mom_m[name][0], mom_v[name][0], gp, "adamw_" + name)
        res[name] = [o[None] for o in outs]
    shapes = [weights[n].shape for n in SMALL_EARLY] + [sse.shape] + [weights[n].shape for n in SMALL_MID + SMALL_LATE]
    g_small = _unpack_small(_sum_slots(jnp.concatenate([p_early, p_mid, p_late], axis=1)), shapes)
    loss = 0.5 * g_small.pop(len(SMALL_EARLY))[0, 0] / D_MODEL
    d_small, m_small, v_small = _adamw_native([weights[n] for n in SMALL], [mom_m[n] for n in SMALL],
                                              [mom_v[n] for n in SMALL], g_small)
    for i, n in enumerate(SMALL):
        res[n] = [g_small[i], d_small[i], m_small[i], v_small[i]]

    return (loss, grad_x[None], *[res[n][0] for n in order], *[res[n][1] for n in order],
            *[res[n][2] for n in order], *[res[n][3] for n in order])
```

```python
import math

import jax
import jax.numpy as jnp
from jax import lax
from jax.experimental import pallas as pl
from jax.experimental.pallas import tpu as pltpu

F32 = jnp.float32
BF16 = jnp.bfloat16

D_MODEL = 1024
ATTN_W = 512
HEAD_DIM = 64
SSM_W = 512
SSM_GROUP = 16
SSM_GROUPS = 32
SSM_STATE = 64
N_STATE = SSM_GROUPS * SSM_STATE
D_FF = 4096
EPS = 1e-6
NEG_INF = -1e30
ATTN_CHUNK = 2048
ATTN_BLOCK = 128
DILATIONS = (1, 4, 16)
N_SEG = 8
SSM_LK = 64
N_DEV = 8
LANES = 128
MXU_W = 256

ADAM_LR = 0.001
ADAM_B1 = 0.9
ADAM_B2 = 0.999
ADAM_EPS = 1e-08
ADAM_WD = 0.01
ADAM_STEP = 10

VMEM_LIMIT = 56 * 1024 * 1024
GELU_C = math.sqrt(2.0 / math.pi)
MESH = pl.DeviceIdType.MESH


def _cp(sem, vmem=VMEM_LIMIT):
    return pltpu.CompilerParams(dimension_semantics=sem, vmem_limit_bytes=vmem)


def _dot(a, b):
    return jnp.dot(a, b, preferred_element_type=F32)


def _dot_tn(a, b):
    return lax.dot_general(a, b, (((0,), (0,)), ((), ())), preferred_element_type=F32)


def _group_mean(x2, gmat):
    hi = x2.astype(BF16)
    lo = (x2 - hi.astype(F32)).astype(BF16)
    w = gmat.shape[0]
    return jnp.concatenate([_dot(hi[:, c:c + w], gmat) + _dot(lo[:, c:c + w], gmat) for c in range(0, x2.shape[1], w)],
                           axis=1)


def _rms(x):
    return lax.rsqrt(jnp.mean(x * x, axis=-1, keepdims=True) + EPS)


def _rms_bwd(dy, x, g, n):
    r = _rms(x)
    gdy = dy * g
    dx = r * gdy - x * (r * r * r) * (jnp.sum(gdy * x, axis=-1, keepdims=True) / n)
    return dx, dy * (x * r)


def _gelu(y):
    t = jnp.tanh(GELU_C * (y + 0.044715 * (y * y * y)))
    return 0.5 * y * (1.0 + t), t


def _full(shape):
    nd = len(shape)
    return pl.BlockSpec(shape, lambda *_: (0,) * nd)


def _sds(shape, dtype=F32):
    return jax.ShapeDtypeStruct(shape, dtype)


def _fwd_proj(x, g1, w_in, gq, gk, gmat, tm=512):
    s = x.shape[0]

    def body(x_ref, g1_ref, w_ref, gq_ref, gk_ref, gm_ref, qk_ref, qn_ref, kn_ref, v_ref, u_ref, xn_ref):
        xv = x_ref[...]
        xnb = ((xv * _rms(xv)) * g1_ref[...]).astype(BF16)
        xn_ref[...] = xnb
        proj = _dot(xnb, w_ref[...])
        q = proj[:, :ATTN_W]
        k = proj[:, ATTN_W:2 * ATTN_W]
        qk_ref[...] = proj[:, :2 * ATTN_W]
        v_ref[...] = proj[:, 2 * ATTN_W:3 * ATTN_W]
        u_ref[...] = proj[:, 3 * ATTN_W:]
        gm = gm_ref[...]
        qn_ref[...] = (q * lax.rsqrt(_group_mean(q * q, gm) + EPS)) * gq_ref[...]
        kn_ref[...] = (k * lax.rsqrt(_group_mean(k * k, gm) + EPS)) * gk_ref[...]

    row = lambda w: pl.BlockSpec((tm, w), lambda i: (i, 0))
    return pl.pallas_call(
        body, name="fwd_proj", grid=(s // tm,),
        in_specs=[row(D_MODEL), _full((1, D_MODEL)), _full((D_MODEL, 4 * ATTN_W)), _full((1, ATTN_W)),
                  _full((1, ATTN_W)), _full((MXU_W, MXU_W))],
        out_specs=[row(2 * ATTN_W), row(ATTN_W), row(ATTN_W), row(ATTN_W), row(ATTN_W), row(D_MODEL)],
        out_shape=[_sds((s, 2 * ATTN_W)), _sds((s, ATTN_W)), _sds((s, ATTN_W)), _sds((s, ATTN_W)),
                   _sds((s, ATTN_W)), _sds((s, D_MODEL), BF16)],
        compiler_params=_cp(("parallel",)),
    )(x, g1, w_in, gq, gk, gmat)


def _attn_rows(t, d, nb):
    if d == 1:
        q0 = t * ATTN_BLOCK
        return (t, pl.ds(q0, ATTN_BLOCK), pl.ds(ATTN_CHUNK + q0, ATTN_BLOCK),
                pl.ds(ATTN_CHUNK - ATTN_BLOCK + q0, ATTN_BLOCK))
    r = t // nb
    b = t % nb
    return (b, pl.ds(ATTN_BLOCK * b * d + r, ATTN_BLOCK, stride=d),
            pl.ds(ATTN_CHUNK + ATTN_BLOCK * b * d + r, ATTN_BLOCK, stride=d),
            pl.ds(ATTN_CHUNK + ATTN_BLOCK * (b - 1) * d + r, ATTN_BLOCK, stride=d))


def _attn_masks():
    row = lax.broadcasted_iota(jnp.int32, (ATTN_BLOCK, LANES), 0)
    col = lax.broadcasted_iota(jnp.int32, (ATTN_BLOCK, LANES), 1)
    return row, col


NBLK = ATTN_CHUNK // ATTN_BLOCK


def _attn_bias(bias_s):
    row, col = _attn_masks()
    bias_s[:, pl.ds(0, LANES)] = jnp.where(col >= row, 0.0, NEG_INF)
    bias_s[:, pl.ds(LANES, LANES)] = jnp.where(col <= row, 0.0, NEG_INF)
    return col < HEAD_DIM


def _attn_fwd(qn, kn, v, ex=None, group=4):
    s = qn.shape[0]
    nch = s // ATTN_CHUNK
    scale = HEAD_DIM ** -0.5
    npat = len(DILATIONS)
    n_hp = ATTN_W // LANES

    def body(*refs):
        ((q_ref, kp_ref, kc_ref, vp_ref, vc_ref), (o_ref, lse_ref),
         (kk, vv, kt_s, vb_s, bias_s, m_s, a_s), hx) = _carry(ex, refs, 5, 2)
        i = pl.program_id(1)
        if ex is not None:
            @pl.when(jnp.logical_and(pl.program_id(0) == 0, i == 0))
            def _():
                ex.start(*hx)

        kk[pl.ds(0, ATTN_CHUNK), :] = kp_ref[...]
        kk[pl.ds(ATTN_CHUNK, ATTN_CHUNK), :] = kc_ref[...]
        vv[pl.ds(0, ATTN_CHUNK), :] = vp_ref[...]
        vv[pl.ds(ATTN_CHUNK, ATTN_CHUNK), :] = vc_ref[...]
        head0 = _attn_bias(bias_s)
        first_pen = jnp.where(i > 0, 0.0, NEG_INF)

        for p, d in enumerate(DILATIONS):
            nb = ATTN_CHUNK // (ATTN_BLOCK * d)

            def prep(t, d=d, nb=nb):
                b, _, crows, prows = _attn_rows(t, d, nb)
                nxt = t + 1 if b + 1 < nb else None
                ktc = kk[crows, :].T.astype(BF16)
                vc = vv[crows, :]
                v0 = jnp.where(head0, vc, 1.0).astype(BF16)
                v1 = jnp.where(head0, 1.0, vc).astype(BF16)
                kt_s[t, :, pl.ds(LANES, LANES)] = ktc
                vb_s[2 * t, pl.ds(ATTN_BLOCK, ATTN_BLOCK), :] = v0
                vb_s[2 * t + 1, pl.ds(ATTN_BLOCK, ATTN_BLOCK), :] = v1
                if nxt is not None:
                    kt_s[nxt, :, pl.ds(0, LANES)] = ktc
                    vb_s[2 * nxt, pl.ds(0, ATTN_BLOCK), :] = v0
                    vb_s[2 * nxt + 1, pl.ds(0, ATTN_BLOCK), :] = v1
                if b == 0:
                    kt_s[t, :, pl.ds(0, LANES)] = kk[prows, :].T.astype(BF16)
                    vp = vv[prows, :]
                    vb_s[2 * t, pl.ds(0, ATTN_BLOCK), :] = jnp.where(head0, vp, 1.0).astype(BF16)
                    vb_s[2 * t + 1, pl.ds(0, ATTN_BLOCK), :] = jnp.where(head0, 1.0, vp).astype(BF16)

            def main(tg, p=p, d=d, nb=nb):
                st = []
                for g in range(group):
                    t = tg * group + g
                    b, qrows, _, _ = _attn_rows(t, d, nb)
                    q = q_ref[qrows, :] * scale
                    for h in range(2):
                        hm = head0 if h == 0 else jnp.logical_not(head0)
                        st.append(dict(t=t, b=b, qrows=qrows, sc=_dot(jnp.where(hm, q, 0.0).astype(BF16), kt_s[t])))
                for e in st:
                    sc = e["sc"] + bias_s[...]
                    s_p = sc[:, :LANES] + first_pen if e["b"] == 0 else sc[:, :LANES]
                    s_c = sc[:, LANES:]
                    m = jnp.max(jnp.maximum(s_p, s_c), axis=-1, keepdims=True)
                    e["eb"] = jnp.concatenate([jnp.exp(s_p - m), jnp.exp(s_c - m)], axis=1).astype(BF16)
                    e["m"] = jnp.broadcast_to(m, (ATTN_BLOCK, LANES))
                for g in range(group):
                    e0, e1 = st[2 * g], st[2 * g + 1]
                    t = e0["t"]
                    m_s[p, e0["qrows"], :] = jnp.where(head0, e0["m"], e1["m"])
                    a_s[2 * p, e0["qrows"], :] = _dot(e0["eb"], vb_s[2 * t])
                    a_s[2 * p + 1, e0["qrows"], :] = _dot(e1["eb"], vb_s[2 * t + 1])

            for g in range(group):
                prep(g)
            for tg in range(NBLK // group):
                if tg + 1 < NBLK // group:
                    for g in range(group):
                        prep((tg + 1) * group + g)
                main(tg)

        def merge(t, carry):
            rows = pl.ds(pl.multiple_of(t * ATTN_BLOCK, ATTN_BLOCK), ATTN_BLOCK)
            m_all = m_s[0, rows, :]
            for p in range(1, npat):
                m_all = jnp.maximum(m_all, m_s[p, rows, :])
            num = jnp.zeros((ATTN_BLOCK, LANES), F32)
            den = jnp.zeros((ATTN_BLOCK, LANES), F32)
            for p in range(npat):
                w = jnp.exp(m_s[p, rows, :] - m_all)
                a0, a1 = a_s[2 * p, rows, :], a_s[2 * p + 1, rows, :]
                num = num + jnp.where(head0, a0, a1) * w
                den = den + pltpu.roll(jnp.where(head0, a1, a0), HEAD_DIM, 1) * w
            o_ref[rows, :] = num / den
            lse_ref[rows, :] = m_all + jnp.log(den)
            return carry

        lax.fori_loop(0, NBLK, merge, 0, unroll=2)
        if ex is not None:
            @pl.when(jnp.logical_and(pl.program_id(0) == n_hp - 1, i == nch - 1))
            def _():
                ex.wait(*hx)

    cur = pl.BlockSpec((ATTN_CHUNK, LANES), lambda h, i: (i, h))
    prev = pl.BlockSpec((ATTN_CHUNK, LANES), lambda h, i: (jnp.maximum(i - 1, 0), h))
    hosted = ex is not None
    return pl.pallas_call(
        body, name="attn_fwd", grid=(n_hp, nch),
        in_specs=[cur, prev, cur, prev, cur] + (ex.specs() if hosted else []),
        out_specs=[cur, cur] + (ex.specs() if hosted else []),
        out_shape=[_sds((s, ATTN_W)), _sds((s, ATTN_W))] + (ex.out_shape() if hosted else []),
        scratch_shapes=[pltpu.VMEM((2 * ATTN_CHUNK, LANES), F32), pltpu.VMEM((2 * ATTN_CHUNK, LANES), F32),
                        pltpu.VMEM((NBLK, LANES, 2 * LANES), BF16), pltpu.VMEM((2 * NBLK, 2 * ATTN_BLOCK, LANES), BF16),
                        pltpu.VMEM((ATTN_BLOCK, 2 * LANES), F32),
                        pltpu.VMEM((npat, ATTN_CHUNK, LANES), F32), pltpu.VMEM((2 * npat, ATTN_CHUNK, LANES), F32)]
        + (ex.scratch() if hosted else []),
        compiler_params=_cp(("arbitrary", "arbitrary")),
    )(qn, kn, kn, v, v, *(ex.srcs if hosted else []))


def _attn_bwd(qn, kn, v, o, lse, do, ex=None, group=4):
    s = qn.shape[0]
    nch = s // ATTN_CHUNK
    scale = HEAD_DIM ** -0.5
    npat = len(DILATIONS)
    n_hp = ATTN_W // LANES

    def body(*refs):
        ((q_ref, kp_ref, kc_ref, vp_ref, vc_ref, o_ref, lse_ref, do_ref), (dq_ref, dk_ref, dv_ref),
         (kk, vv, dkk, dvv, kt_s, vt_s, kn_s, bias_s, dq_s, dl_s, dkb, dvb), hx) = _carry(ex, refs, 8, 3)
        step = pl.program_id(1)
        i = nch - 1 - step
        if ex is not None:
            @pl.when(jnp.logical_and(pl.program_id(0) == 0, step == 0))
            def _():
                ex.start(*hx)

        kk[pl.ds(0, ATTN_CHUNK), :] = kp_ref[...]
        kk[pl.ds(ATTN_CHUNK, ATTN_CHUNK), :] = kc_ref[...]
        vv[pl.ds(0, ATTN_CHUNK), :] = vp_ref[...]
        vv[pl.ds(ATTN_CHUNK, ATTN_CHUNK), :] = vc_ref[...]

        @pl.when(step == 0)
        def _():
            dkk[pl.ds(ATTN_CHUNK, ATTN_CHUNK), :] = jnp.zeros((ATTN_CHUNK, LANES), F32)
            dvv[pl.ds(ATTN_CHUNK, ATTN_CHUNK), :] = jnp.zeros((ATTN_CHUNK, LANES), F32)

        @pl.when(step > 0)
        def _():
            dkk[pl.ds(ATTN_CHUNK, ATTN_CHUNK), :] = dkk[pl.ds(0, ATTN_CHUNK), :]
            dvv[pl.ds(ATTN_CHUNK, ATTN_CHUNK), :] = dvv[pl.ds(0, ATTN_CHUNK), :]

        dkk[pl.ds(0, ATTN_CHUNK), :] = jnp.zeros((ATTN_CHUNK, LANES), F32)
        dvv[pl.ds(0, ATTN_CHUNK), :] = jnp.zeros((ATTN_CHUNK, LANES), F32)
        head0 = _attn_bias(bias_s)

        def delta(t, carry):
            rows = pl.ds(pl.multiple_of(t * ATTN_BLOCK, ATTN_BLOCK), ATTN_BLOCK)
            prod = do_ref[rows, :] * o_ref[rows, :]
            d0 = jnp.sum(jnp.where(head0, prod, 0.0), axis=-1, keepdims=True)
            d1 = jnp.sum(jnp.where(head0, 0.0, prod), axis=-1, keepdims=True)
            dl_s[rows, :] = jnp.where(head0, d0, d1)
            return carry

        lax.fori_loop(0, NBLK, delta, 0, unroll=2)

        first_pen = jnp.where(i > 0, 0.0, NEG_INF)

        for p, d in enumerate(DILATIONS):
            nb = ATTN_CHUNK // (ATTN_BLOCK * d)

            def prep(t, d=d, nb=nb):
                b, _, crows, prows = _attn_rows(t, d, nb)
                nxt = t + 1 if b + 1 < nb else None
                kc = kk[crows, :]
                ktc = kc.T.astype(BF16)
                knc = (kc * scale).astype(BF16)
                vtc = vv[crows, :].T.astype(BF16)
                kt_s[t, :, pl.ds(LANES, LANES)] = ktc
                kn_s[t, pl.ds(ATTN_BLOCK, ATTN_BLOCK), :] = knc
                vt_s[t, :, pl.ds(LANES, LANES)] = vtc
                if nxt is not None:
                    kt_s[nxt, :, pl.ds(0, LANES)] = ktc
                    kn_s[nxt, pl.ds(0, ATTN_BLOCK), :] = knc
                    vt_s[nxt, :, pl.ds(0, LANES)] = vtc
                if b == 0:
                    kp = kk[prows, :]
                    kt_s[t, :, pl.ds(0, LANES)] = kp.T.astype(BF16)
                    kn_s[t, pl.ds(0, ATTN_BLOCK), :] = (kp * scale).astype(BF16)
                    vt_s[t, :, pl.ds(0, LANES)] = vv[prows, :].T.astype(BF16)

            def main(tg, p=p, d=d, nb=nb):
                st = []
                for g in range(group):
                    t = tg * group + g
                    b, qrows, _, _ = _attn_rows(t, d, nb)
                    q = q_ref[qrows, :] * scale
                    dout = do_ref[qrows, :]
                    lse_b = lse_ref[qrows, :]
                    dl_b = dl_s[qrows, :]
                    for h in range(2):
                        hm = head0 if h == 0 else jnp.logical_not(head0)
                        c0 = h * HEAD_DIM
                        qh = jnp.where(hm, q, 0.0).astype(BF16)
                        doh = jnp.where(hm, dout, 0.0).astype(BF16)
                        st.append(dict(t=t, b=b, qrows=qrows, qh=qh, doh=doh, lse=lse_b[:, c0:c0 + 1],
                                       dl=dl_b[:, c0:c0 + 1], sc=_dot(qh, kt_s[t]), dp=_dot(doh, vt_s[t])))
                for e in st:
                    sc = e["sc"] + bias_s[...]
                    if e["b"] == 0:
                        sc = jnp.concatenate([sc[:, :LANES] + first_pen, sc[:, LANES:]], axis=1)
                    pr = jnp.exp(sc - e["lse"])
                    e["ds"] = (pr * (e["dp"] - e["dl"])).astype(BF16)
                    e["pr"] = pr.astype(BF16)
                for g in range(group):
                    e0, e1 = st[2 * g], st[2 * g + 1]
                    t = e0["t"]
                    dq_s[p, e0["qrows"], :] = jnp.where(head0, _dot(e0["ds"], kn_s[t]), _dot(e1["ds"], kn_s[t]))
                    dkb[t] = _dot_tn(e0["ds"], e0["qh"]) + _dot_tn(e1["ds"], e1["qh"])
                    dvb[t] = _dot_tn(e0["pr"], e0["doh"]) + _dot_tn(e1["pr"], e1["doh"])

            def scatter(t, d=d, nb=nb):
                _, _, crows, prows = _attn_rows(t, d, nb)
                dkk[prows, :] = dkk[prows, :] + dkb[t, pl.ds(0, ATTN_BLOCK), :]
                dkk[crows, :] = dkk[crows, :] + dkb[t, pl.ds(ATTN_BLOCK, ATTN_BLOCK), :]
                dvv[prows, :] = dvv[prows, :] + dvb[t, pl.ds(0, ATTN_BLOCK), :]
                dvv[crows, :] = dvv[crows, :] + dvb[t, pl.ds(ATTN_BLOCK, ATTN_BLOCK), :]

            n_groups = NBLK // group
            for g in range(group):
                prep(g)
            for tg in range(n_groups):
                if tg + 1 < n_groups:
                    for g in range(group):
                        prep((tg + 1) * group + g)
                main(tg)
                if tg >= 1:
                    for g in range(group):
                        scatter((tg - 1) * group + g)
            for g in range(group):
                scatter((n_groups - 1) * group + g)

        def finish(t, carry):
            rows = pl.ds(pl.multiple_of(t * ATTN_BLOCK, ATTN_BLOCK), ATTN_BLOCK)
            acc = dq_s[0, rows, :]
            for p in range(1, npat):
                acc = acc + dq_s[p, rows, :]
            dq_ref[rows, :] = acc
            return carry

        lax.fori_loop(0, NBLK, finish, 0, unroll=2)
        dk_ref[...] = dkk[pl.ds(ATTN_CHUNK, ATTN_CHUNK), :]
        dv_ref[...] = dvv[pl.ds(ATTN_CHUNK, ATTN_CHUNK), :]
        if ex is not None:
            @pl.when(jnp.logical_and(pl.program_id(0) == n_hp - 1, step == nch - 1))
            def _():
                ex.wait(*hx)

    cur = pl.BlockSpec((ATTN_CHUNK, LANES), lambda h, t: (nch - 1 - t, h))
    prev = pl.BlockSpec((ATTN_CHUNK, LANES), lambda h, t: (jnp.maximum(nch - 2 - t, 0), h))
    big = pltpu.VMEM((2 * ATTN_CHUNK, LANES), F32)
    pair_t = pltpu.VMEM((NBLK, LANES, 2 * LANES), BF16)
    hosted = ex is not None
    return pl.pallas_call(
        body, name="attn_bwd", grid=(n_hp, nch),
        in_specs=[cur, prev, cur, prev, cur, cur, cur, cur] + (ex.specs() if hosted else []),
        out_specs=[cur, cur, cur] + (ex.specs() if hosted else []),
        out_shape=[_sds((s, ATTN_W))] * 3 + (ex.out_shape() if hosted else []),
        scratch_shapes=[big, big, big, big, pair_t, pair_t, pltpu.VMEM((NBLK, 2 * ATTN_BLOCK, LANES), BF16),
                        pltpu.VMEM((ATTN_BLOCK, 2 * LANES), F32),
                        pltpu.VMEM((npat, ATTN_CHUNK, LANES), F32), pltpu.VMEM((ATTN_CHUNK, LANES), F32),
                        pltpu.VMEM((NBLK, 2 * ATTN_BLOCK, LANES), F32), pltpu.VMEM((NBLK, 2 * ATTN_BLOCK, LANES), F32)]
        + (ex.scratch() if hosted else []),
        compiler_params=_cp(("arbitrary", "arbitrary")),
    )(qn, kn, kn, v, v, o, lse, do, *(ex.srcs if hosted else []))


def _discretize(lr, li, dt):
    mag = jnp.exp(lr * dt)
    abr = mag * jnp.cos(li * dt)
    abi = mag * jnp.sin(li * dt)
    den = lr * lr + li * li
    nr, ni = abr - 1.0, abi
    cr = (nr * lr + ni * li) / den
    ci = (ni * lr - nr * li) / den
    return abr, abi, den, nr, ni, cr, ci


def _ssm_discretize(a_re, a_im, log_dt, b_re_t, b_im_t):
    def body(ar_ref, ai_ref, ldt_ref, br_ref, bi_ref, abr_ref, abi_ref, bbr_ref, bbi_ref):
        abr, abi, _, _, _, cr, ci = _discretize(ar_ref[...], ai_ref[...], jnp.exp(ldt_ref[...]))
        br, bi = br_ref[...], bi_ref[...]
        abr_ref[...] = abr
        abi_ref[...] = abi
        bbr_ref[...] = cr * br - ci * bi
        bbi_ref[...] = cr * bi + ci * br

    return pl.pallas_call(
        body, name="ssm_discretize",
        out_shape=[_sds(a_re.shape)] * 2 + [_sds(b_re_t.shape)] * 2,
    )(a_re, a_im, log_dt, b_re_t, b_im_t)


def _ssm_discretize_bwd(a_re, a_im, log_dt, b_re_t, b_im_t, dabr, dabi, dbbr, dbbi):
    def body(ar_ref, ai_ref, ldt_ref, br_ref, bi_ref, dabr_ref, dabi_ref, dbbr_ref, dbbi_ref,
             dar_ref, dai_ref, dldt_ref, dbr_ref, dbi_ref):
        lr, li = ar_ref[...], ai_ref[...]
        dt = jnp.exp(ldt_ref[...])
        abr, abi, den, nr, ni, cr, ci = _discretize(lr, li, dt)
        br, bi = br_ref[...], bi_ref[...]
        gbr, gbi = dbbr_ref[...], dbbi_ref[...]
        dcr = jnp.sum(gbr * br + gbi * bi, axis=1, keepdims=True)
        dci = jnp.sum(gbi * br - gbr * bi, axis=1, keepdims=True)
        dbr_ref[...] = cr * gbr + ci * gbi
        dbi_ref[...] = cr * gbi - ci * gbr
        dnr = (dcr * lr - dci * li) / den
        dni = (dcr * li + dci * lr) / den
        dden = -(dcr * cr + dci * ci) / den
        dlr = (dcr * nr + dci * ni) / den + dden * 2.0 * lr
        dli = (dcr * ni - dci * nr) / den + dden * 2.0 * li
        gabr = dabr_ref[...] + dnr
        gabi = dabi_ref[...] + dni
        dphi = gabr * abr + gabi * abi
        dth = gabi * abr - gabr * abi
        dar_ref[...] = dlr + dphi * dt
        dai_ref[...] = dli + dth * dt
        dldt_ref[...] = jnp.sum(dphi * lr + dth * li, axis=2, keepdims=True) * dt

    return pl.pallas_call(
        body, name="ssm_discretize_bwd",
        out_shape=[_sds(a_re.shape)] * 2 + [_sds(log_dt.shape)] + [_sds(b_re_t.shape)] * 2,
    )(a_re, a_im, log_dt, b_re_t, b_im_t, dabr, dabi, dbbr, dbbi)


def _ssm_power(abr, abi, n_sq):
    def body(r_ref, i_ref, or_ref, oi_ref):
        r, i = r_ref[...], i_ref[...]
        for _ in range(n_sq):
            r, i = r * r - i * i, 2.0 * r * i
        or_ref[...] = r
        oi_ref[...] = i

    return pl.pallas_call(body, name="ssm_power", out_shape=[_sds(abr.shape)] * 2)(abr, abi)


N_CB = SSM_W // LANES
CB_STATES = N_STATE // N_CB
ROWS = N_SEG * SSM_LK


class _Neg:
    def __init__(self, ref):
        self.ref = ref

    def __getitem__(self, idx):
        return -self.ref[idx]


def _seg_init(fin_r, fin_i, pw_r, pw_i, x_r, x_i, reverse):
    zero = jnp.zeros((1, N_STATE), F32)
    cr, ci = zero, zero
    order = range(N_SEG - 1, -1, -1) if reverse else range(N_SEG)
    pr = pw_r[...]
    pi = -pw_i[...] if reverse else pw_i[...]
    for j in order:
        x_r[pl.ds(j, 1), :] = cr
        x_i[pl.ds(j, 1), :] = ci
        fr, fi = fin_r[pl.ds(j, 1), :], fin_i[pl.ds(j, 1), :]
        cr, ci = fr + pr * cr - pi * ci, fi + pr * ci + pi * cr


def _permute_in(src_ref, dst):
    for c in range(N_CB):
        dst[c] = src_ref[:, :, pl.ds(c * LANES, LANES)].reshape(ROWS, LANES)


def _permute_out(src, dst_ref):
    for c in range(N_CB):
        dst_ref[:, :, pl.ds(c * LANES, LANES)] = src[c].reshape(SSM_LK, N_SEG, LANES)


def _scan_block(a_r, a_i, c, b_r, b_i, b_off, x_r, x_i, reverse, acc=None):
    cols = pl.ds(c * CB_STATES, CB_STATES)
    ar = jnp.broadcast_to(a_r[:, cols], (N_SEG, CB_STATES))
    ai = jnp.broadcast_to(a_i[:, cols], (N_SEG, CB_STATES))
    xr, xi = x_r[:, cols], x_i[:, cols]
    if acc is not None:
        sr = jnp.zeros((N_SEG, CB_STATES), F32)
        si = jnp.zeros((N_SEG, CB_STATES), F32)
    for t in range(SSM_LK):
        k = (SSM_LK - 1 - t) if reverse else t
        rows = pl.ds(k * N_SEG + b_off, N_SEG)
        xr, xi = ar * xr - ai * xi + b_r[rows, :], ar * xi + ai * xr + b_i[rows, :]
        b_r[rows, :] = xr
        b_i[rows, :] = xi
        if acc is not None:
            pr, pi = acc[0][pl.ds(k * N_SEG, N_SEG), :], acc[1][pl.ds(k * N_SEG, N_SEG), :]
            sr = sr + xr * pr + xi * pi
            si = si + xi * pr - xr * pi
    x_r[:, cols] = xr
    x_i[:, cols] = xi
    if acc is not None:
        acc[2][:, cols] += sr
        acc[3][:, cols] += si


def _ssm_fwd(u3, abr, abi, pw_r, pw_i, fin_r, fin_i, bb_r, bb_i, cc_r, cc_i, dskip, finals_only):
    sl = u3.shape[0]
    nch = sl // SSM_LK

    def body(u_ref, abr_ref, abi_ref, pwr_ref, pwi_ref, finr_ref, fini_ref, bbr_ref, bbi_ref,
             ccr_ref, cci_ref, d_ref, *rest):
        if finals_only:
            xfr_ref, xfi_ref, up, x_r, x_i = rest[:5]
        else:
            y_ref, xsr_ref, xsi_ref, up, yp, x_r, x_i = rest[:7]
        xs_r, xs_i = rest[-2 * N_CB:-N_CB], rest[-N_CB:]
        k = pl.program_id(0)

        @pl.when(k == 0)
        def _():
            _seg_init(finr_ref, fini_ref, pwr_ref, pwi_ref, x_r, x_i, False)

        if not finals_only:
            xsr_ref[0] = x_r[...]
            xsi_ref[0] = x_i[...]
        _permute_in(u_ref, up)

        def drive(c):
            lhs = up[c].astype(BF16)
            xs_r[c][...] = _dot(lhs, bbr_ref[c])
            xs_i[c][...] = _dot(lhs, bbi_ref[c])

        def readout(c):
            yp[c] = (_dot(xs_r[c][...].astype(BF16), ccr_ref[c]) - _dot(xs_i[c][...].astype(BF16), cci_ref[c])
                     + d_ref[:, pl.ds(c * LANES, LANES)] * up[c])

        drive(0)
        for c in range(N_CB):
            if c + 1 < N_CB:
                drive(c + 1)
            if c >= 1 and not finals_only:
                readout(c - 1)
            _scan_block(abr_ref, abi_ref, c, xs_r[c], xs_i[c], 0, x_r, x_i, False)
        if finals_only:
            @pl.when(k == nch - 1)
            def _():
                xfr_ref[...] = x_r[...]
                xfi_ref[...] = x_i[...]
        else:
            readout(N_CB - 1)
            _permute_out(yp, y_ref)

    ublk = pl.BlockSpec((SSM_LK, N_SEG, SSM_W), lambda k: (k, 0, 0))
    st = pl.BlockSpec((1, N_SEG, N_STATE), lambda k: (k, 0, 0))
    vec = _full((1, N_STATE))
    mat = _full((N_SEG, N_STATE))
    chunk = pltpu.VMEM((N_CB, ROWS, LANES), F32)
    blocks = [pltpu.VMEM((ROWS, CB_STATES), F32)] * (2 * N_CB)
    small = pltpu.VMEM((N_SEG, N_STATE), F32)
    if finals_only:
        out_specs, out_shape = [mat, mat], [_sds((N_SEG, N_STATE))] * 2
        scratch, name = [chunk, small, small] + blocks, "ssm_fwd_finals"
    else:
        out_specs = [ublk, st, st]
        out_shape = [_sds(u3.shape)] + [_sds((nch, N_SEG, N_STATE))] * 2
        scratch, name = [chunk, chunk, small, small] + blocks, "ssm_fwd"
    return pl.pallas_call(
        body, name=name, grid=(nch,),
        in_specs=[ublk, vec, vec, vec, vec, mat, mat,
                  _full((N_CB, LANES, CB_STATES)), _full((N_CB, LANES, CB_STATES)),
                  _full((N_CB, CB_STATES, LANES)), _full((N_CB, CB_STATES, LANES)), _full((1, SSM_W))],
        out_specs=out_specs, out_shape=out_shape, scratch_shapes=scratch,
        compiler_params=_cp(("arbitrary",)),
    )(u3, abr, abi, pw_r, pw_i, fin_r, fin_i, bb_r, bb_i, cc_r, cc_i, dskip)


def _ssm_bwd(u3, dy3, xst_r, xst_i, abr, abi, pw_r, pw_i, fin_r, fin_i, bb_r, bb_i, bbt_r, bbt_i,
             cct_r, cct_i, dskip, finals_only):
    sl = u3.shape[0]
    nch = sl // SSM_LK

    def body(u_ref, g_ref, xsr_ref, xsi_ref, abr_ref, abi_ref, pwr_ref, pwi_ref,
             finr_ref, fini_ref, bbr_ref, bbi_ref, btr_ref, bti_ref, ctr_ref, cti_ref, d_ref, *rest):
        if finals_only:
            lfr_ref, lfi_ref, gp, lam_r, lam_i = rest[:5]
            l_r, l_i = rest[-2 * N_CB:-N_CB], rest[-N_CB:]
        else:
            (du_ref, dar_ref, dai_ref, dbr_ref, dbi_ref, dcr_ref, dci_ref, dd_ref,
             gp, up, yp, lam_r, lam_i, x_r, x_i, sar, sai, sdd) = rest[:18]
            l_r, l_i = rest[18:18 + N_CB], rest[18 + N_CB:18 + 2 * N_CB]
            xx_r, xx_i = rest[18 + 2 * N_CB:18 + 3 * N_CB], rest[18 + 3 * N_CB:]
        t = pl.program_id(0)

        @pl.when(t == 0)
        def _():
            _seg_init(finr_ref, fini_ref, pwr_ref, pwi_ref, lam_r, lam_i, True)
            if not finals_only:
                sar[...] = jnp.zeros_like(sar)
                sai[...] = jnp.zeros_like(sai)
                sdd[...] = jnp.zeros_like(sdd)
                dbr_ref[...] = jnp.zeros_like(dbr_ref)
                dbi_ref[...] = jnp.zeros_like(dbi_ref)
                dcr_ref[...] = jnp.zeros_like(dcr_ref)
                dci_ref[...] = jnp.zeros_like(dci_ref)

        _permute_in(g_ref, gp)
        if not finals_only:
            _permute_in(u_ref, up)
            x_r[...] = xsr_ref[0]
            x_i[...] = xsi_ref[0]

        def drive(c):
            lhs = gp[c].astype(BF16)
            l_r[c][...] = _dot(lhs, ctr_ref[c])
            l_i[c][...] = -_dot(lhs, cti_ref[c])
            if not finals_only:
                cols = pl.ds(c * CB_STATES, CB_STATES)
                xx_r[c][pl.ds(0, N_SEG), :] = x_r[:, cols]
                xx_i[c][pl.ds(0, N_SEG), :] = x_i[:, cols]
                ub = up[c].astype(BF16)
                xx_r[c][pl.ds(N_SEG, ROWS), :] = _dot(ub, bbr_ref[c])
                xx_i[c][pl.ds(N_SEG, ROWS), :] = _dot(ub, bbi_ref[c])

        def collect(c):
            lrb = l_r[c][...].astype(BF16)
            lib = l_i[c][...].astype(BF16)
            ub = up[c].astype(BF16)
            gb = gp[c].astype(BF16)
            dbr_ref[c] += _dot_tn(lrb, ub)
            dbi_ref[c] += _dot_tn(lib, ub)
            dcr_ref[c] += _dot_tn(gb, xx_r[c][pl.ds(N_SEG, ROWS), :].astype(BF16))
            dci_ref[c] += -_dot_tn(gb, xx_i[c][pl.ds(N_SEG, ROWS), :].astype(BF16))
            yp[c] = _dot(lrb, btr_ref[c]) + _dot(lib, bti_ref[c]) + d_ref[:, pl.ds(c * LANES, LANES)] * gp[c]
            prod = gp[c] * up[c]
            sdd[:, pl.ds(c * LANES, LANES)] += jnp.sum(prod.reshape(SSM_LK, N_SEG, LANES), axis=0)

        drive(0)
        for c in range(N_CB):
            if c + 1 < N_CB:
                drive(c + 1)
            if finals_only:
                _scan_block(abr_ref, _Neg(abi_ref), c, l_r[c], l_i[c], 0, lam_r, lam_i, True)
            else:
                if c >= 1:
                    collect(c - 1)
                _scan_block(abr_ref, abi_ref, c, xx_r[c], xx_i[c], N_SEG, x_r, x_i, False)
                _scan_block(abr_ref, _Neg(abi_ref), c, l_r[c], l_i[c], 0, lam_r, lam_i, True,
                            acc=(xx_r[c], xx_i[c], sar, sai))
        if finals_only:
            @pl.when(t == nch - 1)
            def _():
                lfr_ref[...] = lam_r[...]
                lfi_ref[...] = lam_i[...]
        else:
            collect(N_CB - 1)
            _permute_out(yp, du_ref)

            @pl.when(t == nch - 1)
            def _():
                dar_ref[...] = jnp.sum(sar[...], axis=0, keepdims=True)
                dai_ref[...] = jnp.sum(sai[...], axis=0, keepdims=True)
                dd_ref[...] = jnp.sum(sdd[...], axis=0, keepdims=True)

    ublk = pl.BlockSpec((SSM_LK, N_SEG, SSM_W), lambda t: (nch - 1 - t, 0, 0))
    st = pl.BlockSpec((1, N_SEG, N_STATE), lambda t: (nch - 1 - t, 0, 0))
    vec = _full((1, N_STATE))
    mat = _full((N_SEG, N_STATE))
    cs = _full((N_CB, LANES, CB_STATES))
    sc = _full((N_CB, CB_STATES, LANES))
    in_specs = [ublk, ublk, st, st, vec, vec, vec, vec, mat, mat, cs, cs, sc, sc, cs, cs, _full((1, SSM_W))]
    chunk = pltpu.VMEM((N_CB, ROWS, LANES), F32)
    blocks = [pltpu.VMEM((ROWS, CB_STATES), F32)] * (2 * N_CB)
    small = pltpu.VMEM((N_SEG, N_STATE), F32)
    if finals_only:
        out_specs, out_shape = [mat, mat], [_sds((N_SEG, N_STATE))] * 2
        scratch, name = [chunk, small, small] + blocks, "ssm_bwd_finals"
    else:
        out_specs = [ublk, vec, vec, sc, sc, cs, cs, _full((1, SSM_W))]
        out_shape = ([_sds(u3.shape), _sds((1, N_STATE)), _sds((1, N_STATE))]
                     + [_sds((N_CB, CB_STATES, LANES))] * 2 + [_sds((N_CB, LANES, CB_STATES))] * 2
                     + [_sds((1, SSM_W))])
        scratch = ([chunk, chunk, chunk, small, small, small, small, small, small, pltpu.VMEM((N_SEG, SSM_W), F32)]
                   + blocks + [pltpu.VMEM((ROWS + N_SEG, CB_STATES), F32)] * (2 * N_CB))
        name = "ssm_bwd"
    return pl.pallas_call(
        body, name=name, grid=(nch,), in_specs=in_specs, out_specs=out_specs, out_shape=out_shape,
        scratch_shapes=scratch, compiler_params=_cp(("arbitrary",)),
    )(u3, dy3, xst_r, xst_i, abr, abi, pw_r, pw_i, fin_r, fin_i, bb_r, bb_i, bbt_r, bbt_i, cct_r, cct_i, dskip)


def _row(tm, w):
    return pl.BlockSpec((tm, w), lambda i: (i, 0))


def _acc_rows(ref, rows, first):
    @pl.when(first)
    def _():
        ref[...] = jnp.zeros_like(ref)

    ref[...] += jnp.sum(rows, axis=0, keepdims=True)


def _fwd_mix(attn, y, x, glu_w, glu_b, ga, gs, w_out, g2, tm=1024):
    s = x.shape[0]

    def body(a_ref, y_ref, x_ref, gw_ref, gb_ref, ga_ref, gs_ref, wo_ref, g2_ref, mix_ref, x2_ref, h_ref):
        a = a_ref[...]
        anb = ((a * _rms(a)) * ga_ref[...]).astype(BF16)
        z, _ = _gelu(y_ref[...])
        so = z * jax.nn.sigmoid(_dot(z.astype(BF16), gw_ref[...]) + gb_ref[...])
        snb = ((so * _rms(so)) * gs_ref[...]).astype(BF16)
        mix_ref[:, pl.ds(0, ATTN_W)] = anb
        mix_ref[:, pl.ds(ATTN_W, SSM_W)] = snb
        x2 = x_ref[...] + (_dot(anb, wo_ref[pl.ds(0, ATTN_W), :]) + _dot(snb, wo_ref[pl.ds(ATTN_W, SSM_W), :]))
        x2_ref[...] = x2
        h_ref[...] = ((x2 * _rms(x2)) * g2_ref[...]).astype(BF16)

    return pl.pallas_call(
        body, name="fwd_mix", grid=(s // tm,),
        in_specs=[_row(tm, ATTN_W), _row(tm, SSM_W), _row(tm, D_MODEL), _full((SSM_W, SSM_W)), _full((1, SSM_W)),
                  _full((1, ATTN_W)), _full((1, SSM_W)), _full((D_MODEL, D_MODEL)), _full((1, D_MODEL))],
        out_specs=[_row(tm, D_MODEL), _row(tm, D_MODEL), _row(tm, D_MODEL)],
        out_shape=[_sds((s, D_MODEL), BF16), _sds((s, D_MODEL)), _sds((s, D_MODEL), BF16)],
        compiler_params=_cp(("parallel",)),
    )(attn, y, x, glu_w, glu_b, ga, gs, w_out, g2)


def _mlp_up(h, w_up, tm=512, bn=1024):
    s = h.shape[0]

    def body(h_ref, w_ref, r_ref, hdn_ref):
        hv = h_ref[...]
        for j in range(D_FF // bn):
            cols = pl.ds(j * bn, bn)
            r = jnp.maximum(_dot(hv, w_ref[:, cols]), 0.0)
            r_ref[:, cols] = r.astype(BF16)
            hdn_ref[:, cols] = (r * r).astype(BF16)

    return pl.pallas_call(
        body, name="mlp_up", grid=(s // tm,),
        in_specs=[_row(tm, D_MODEL), _full((D_MODEL, D_FF))],
        out_specs=[_row(tm, D_FF), _row(tm, D_FF)], out_shape=[_sds((s, D_FF), BF16)] * 2,
        compiler_params=_cp(("parallel",)),
    )(h, w_up)


def _mlp_down_loss(hdn, w_down, x2, tgt, tm=512):
    s = x2.shape[0]

    def body(hdn_ref, w_ref, x2_ref, t_ref, dy_ref, dyb_ref, sse_ref):
        err = (x2_ref[...] + _dot(hdn_ref[...], w_ref[...])) - t_ref[...]
        dy = err * (1.0 / D_MODEL)
        dy_ref[...] = dy
        dyb_ref[...] = dy.astype(BF16)

        @pl.when(pl.program_id(0) == 0)
        def _():
            sse_ref[...] = jnp.zeros_like(sse_ref)

        sse_ref[...] += jnp.sum(jnp.sum(err * err, axis=0, keepdims=True), axis=1, keepdims=True)

    return pl.pallas_call(
        body, name="mlp_down_loss", grid=(s // tm,),
        in_specs=[_row(tm, D_FF), _full((D_FF, D_MODEL)), _row(tm, D_MODEL), _row(tm, D_MODEL)],
        out_specs=[_row(tm, D_MODEL), _row(tm, D_MODEL), _full((1, 1))],
        out_shape=[_sds((s, D_MODEL)), _sds((s, D_MODEL), BF16), _sds((1, 1))],
        compiler_params=_cp(("arbitrary",)),
    )(hdn, w_down, x2, tgt)


def _mlp_down_bwd(dyb, w_down_t, r, tm=512, bn=1024):
    s = dyb.shape[0]

    def body(dy_ref, w_ref, r_ref, dup_ref):
        dyv = dy_ref[...]
        for j in range(D_FF // bn):
            cols = pl.ds(j * bn, bn)
            dup_ref[:, cols] = (_dot(dyv, w_ref[:, cols]) * (2.0 * r_ref[:, cols].astype(F32))).astype(BF16)

    return pl.pallas_call(
        body, name="mlp_down_bwd", grid=(s // tm,),
        in_specs=[_row(tm, D_MODEL), _full((D_MODEL, D_FF)), _row(tm, D_FF)],
        out_specs=_row(tm, D_FF), out_shape=_sds((s, D_FF), BF16),
        compiler_params=_cp(("parallel",)),
    )(dyb, w_down_t, r)


def _mlp_up_bwd(dup, w_up_t, x2, g2, dy, tm=512):
    s = x2.shape[0]

    def body(dup_ref, w_ref, x2_ref, g2_ref, dy_ref, dx2_ref, dx2b_ref, dg_ref):
        dx, dg_rows = _rms_bwd(_dot(dup_ref[...], w_ref[...]), x2_ref[...], g2_ref[...], D_MODEL)
        dx2 = dy_ref[...] + dx
        dx2_ref[...] = dx2
        dx2b_ref[...] = dx2.astype(BF16)
        _acc_rows(dg_ref, dg_rows, pl.program_id(0) == 0)

    return pl.pallas_call(
        body, name="mlp_up_bwd", grid=(s // tm,),
        in_specs=[_row(tm, D_FF), _full((D_FF, D_MODEL)), _row(tm, D_MODEL), _full((1, D_MODEL)), _row(tm, D_MODEL)],
        out_specs=[_row(tm, D_MODEL), _row(tm, D_MODEL), _full((1, D_MODEL))],
        out_shape=[_sds((s, D_MODEL)), _sds((s, D_MODEL), BF16), _sds((1, D_MODEL))],
        compiler_params=_cp(("arbitrary",)),
    )(dup, w_up_t, x2, g2, dy)


def _mix_bwd(dx2b, w_out_t, attn, y, glu_w, glu_b, glu_w_t, ga, gs, tm=1024):
    s = attn.shape[0]

    def body(dx2_ref, wot_ref, a_ref, y_ref, gw_ref, gb_ref, gwt_ref, ga_ref, gs_ref,
             da_ref, dys_ref, z_ref, dpre_ref, dga_ref, dgs_ref, dgb_ref):
        first = pl.program_id(0) == 0
        dmix = _dot(dx2_ref[...], wot_ref[...])
        da, dga_rows = _rms_bwd(dmix[:, :ATTN_W], a_ref[...], ga_ref[...], ATTN_W)
        da_ref[...] = da
        yv = y_ref[...]
        z, t = _gelu(yv)
        gate = jax.nn.sigmoid(_dot(z.astype(BF16), gw_ref[...]) + gb_ref[...])
        dso, dgs_rows = _rms_bwd(dmix[:, ATTN_W:], z * gate, gs_ref[...], SSM_W)
        dpre = dso * z * gate * (1.0 - gate)
        dpre_b = dpre.astype(BF16)
        dz = dso * gate + _dot(dpre_b, gwt_ref[...])
        dgelu = 0.5 * (1.0 + t) + 0.5 * yv * (1.0 - t * t) * (GELU_C * (1.0 + 3.0 * 0.044715 * (yv * yv)))
        dys_ref[...] = dz * dgelu
        z_ref[...] = z.astype(BF16)
        dpre_ref[...] = dpre_b
        _acc_rows(dga_ref, dga_rows, first)
        _acc_rows(dgs_ref, dgs_rows, first)
        _acc_rows(dgb_ref, dpre, first)

    vec = _full((1, SSM_W))
    return pl.pallas_call(
        body, name="mix_bwd", grid=(s // tm,),
        in_specs=[_row(tm, D_MODEL), _full((D_MODEL, D_MODEL)), _row(tm, ATTN_W), _row(tm, SSM_W),
                  _full((SSM_W, SSM_W)), vec, _full((SSM_W, SSM_W)), vec, vec],
        out_specs=[_row(tm, ATTN_W), _row(tm, SSM_W), _row(tm, SSM_W), _row(tm, SSM_W), vec, vec, vec],
        out_shape=[_sds((s, ATTN_W)), _sds((s, SSM_W)), _sds((s, SSM_W), BF16), _sds((s, SSM_W), BF16),
                   _sds((1, ATTN_W)), _sds((1, SSM_W)), _sds((1, SSM_W))],
        compiler_params=_cp(("arbitrary",)),
    )(dx2b, w_out_t, attn, y, glu_w, glu_b, glu_w_t, ga, gs)


def _qk_bwd(dqn, dkn, qk, dv, du, gq, gk, gmat, tm=1024):
    s = qk.shape[0]

    def body(dq_ref, dk_ref, qk_ref, dv_ref, du_ref, gq_ref, gk_ref, gm_ref, dp_ref, dgq_ref, dgk_ref):
        first = pl.program_id(0) == 0
        gm = gm_ref[...]
        for idx, (d_ref, g_ref, dg_ref) in enumerate(((dq_ref, gq_ref, dgq_ref), (dk_ref, gk_ref, dgk_ref))):
            xv = qk_ref[:, pl.ds(idx * ATTN_W, ATTN_W)]
            dyv = d_ref[...]
            r = lax.rsqrt(_group_mean(xv * xv, gm) + EPS)
            gdy = dyv * g_ref[...]
            dx = r * gdy - xv * (r * r * r) * _group_mean(gdy * xv, gm)
            dp_ref[:, pl.ds(idx * ATTN_W, ATTN_W)] = dx.astype(BF16)
            _acc_rows(dg_ref, dyv * (xv * r), first)
        dp_ref[:, pl.ds(2 * ATTN_W, ATTN_W)] = dv_ref[...].astype(BF16)
        dp_ref[:, pl.ds(3 * ATTN_W, SSM_W)] = du_ref[...].astype(BF16)

    vec = _full((1, ATTN_W))
    return pl.pallas_call(
        body, name="qk_bwd", grid=(s // tm,),
        in_specs=[_row(tm, ATTN_W), _row(tm, ATTN_W), _row(tm, 2 * ATTN_W), _row(tm, ATTN_W), _row(tm, SSM_W),
                  vec, vec, _full((MXU_W, MXU_W))],
        out_specs=[_row(tm, 4 * ATTN_W), vec, vec],
        out_shape=[_sds((s, 4 * ATTN_W), BF16), _sds((1, ATTN_W)), _sds((1, ATTN_W))],
        compiler_params=_cp(("arbitrary",)),
    )(dqn, dkn, qk, dv, du, gq, gk, gmat)


def _in_bwd(dproj, w_in_t, x, g1, dx2, ex=None, tm=1024):
    s = x.shape[0]
    steps = s // tm

    def body(*refs):
        (dp_ref, w_ref, x_ref, g1_ref, dx2_ref), (gx_ref, dg_ref), _, hx = _carry(ex, refs, 5, 2)
        if ex is not None:
            @pl.when(pl.program_id(0) == 0)
            def _():
                ex.start(*hx)

        dx, dg_rows = _rms_bwd(_dot(dp_ref[...], w_ref[...]), x_ref[...], g1_ref[...], D_MODEL)
        gx_ref[...] = dx2_ref[...] + dx
        _acc_rows(dg_ref, dg_rows, pl.program_id(0) == 0)
        if ex is not None:
            @pl.when(pl.program_id(0) == steps - 1)
            def _():
                ex.wait(*hx)

    hosted = ex is not None
    return pl.pallas_call(
        body, name="in_bwd", grid=(steps,),
        in_specs=[_row(tm, 4 * ATTN_W), _full((4 * ATTN_W, D_MODEL)), _row(tm, D_MODEL), _full((1, D_MODEL)),
                  _row(tm, D_MODEL)] + (ex.specs() if hosted else []),
        out_specs=[_row(tm, D_MODEL), _full((1, D_MODEL))] + (ex.specs() if hosted else []),
        out_shape=[_sds((s, D_MODEL)), _sds((1, D_MODEL))] + (ex.out_shape() if hosted else []),
        scratch_shapes=ex.scratch() if hosted else [],
        compiler_params=_cp(("arbitrary",)),
    )(dproj, w_in_t, x, g1, dx2, *(ex.srcs if hosted else []))


def _mm_tn(a, b, name, ts=2048):
    s, k = a.shape
    n = b.shape[1]
    bk, bn = min(k, 1024), min(n, 1024)
    steps = s // ts

    def body(a_ref, b_ref, o_ref, acc):
        t = pl.program_id(2)

        @pl.when(t == 0)
        def _():
            acc[...] = jnp.zeros_like(acc)

        acc[...] += _dot_tn(a_ref[...], b_ref[...])

        @pl.when(t == steps - 1)
        def _():
            o_ref[...] = acc[...].astype(BF16)

    return pl.pallas_call(
        body, name=name, grid=(k // bk, n // bn, steps),
        in_specs=[pl.BlockSpec((ts, bk), lambda i, j, t: (t, i)), pl.BlockSpec((ts, bn), lambda i, j, t: (t, j))],
        out_specs=pl.BlockSpec((bk, bn), lambda i, j, t: (i, j)), out_shape=_sds((k, n), BF16),
        scratch_shapes=[pltpu.VMEM((bk, bn), F32)],
        compiler_params=_cp(("parallel", "parallel", "arbitrary")),
    )(a, b)


def _peer(k):
    x, y, c = lax.axis_index("x"), lax.axis_index("y"), lax.axis_index("c")
    px = 1 - x if k & 4 else x
    py = 1 - y if k & 2 else y
    pc = 1 - c if k & 1 else c
    return (px, py, pc), 4 * px + 2 * py + pc


def _gather_rows(x_shard):
    m_per, n = x_shard.shape

    def body(x_ref, out_ref, send_sems, recv_sems, local_sem):
        x, y, c = lax.axis_index("x"), lax.axis_index("y"), lax.axis_index("c")
        me, sibling = (x, y, c), (x, y, 1 - c)
        chips = [(1 - x, y), (x, 1 - y), (1 - x, 1 - y)]

        def rows(px, py, pc):
            return out_ref.at[pl.ds((4 * px + 2 * py + pc) * m_per, m_per), :]

        def copy(k, block, to, src=None):
            return pltpu.make_async_remote_copy(
                src_ref=rows(*block) if src is None else src, dst_ref=rows(*block),
                send_sem=send_sems.at[k], recv_sem=recv_sems.at[k], device_id=to, device_id_type=MESH)

        mine = pltpu.make_async_copy(x_ref, rows(*me), local_sem)
        mine.start()
        first = [copy(0, me, sibling, src=x_ref)]
        first += [copy(1 + j, me, (*chip, c), src=x_ref) for j, chip in enumerate(chips)]
        for cp in first:
            cp.start()
        passed = [copy(4 + j, (*chip, c), sibling) for j, chip in enumerate(chips)]
        for j, chip in enumerate(chips):
            copy(1 + j, (*chip, c), me).wait_recv()
            passed[j].start()
        copy(0, sibling, me).wait_recv()
        for j, chip in enumerate(chips):
            copy(4 + j, (*chip, 1 - c), me).wait_recv()
        for cp in first + passed:
            cp.wait_send()
        mine.wait()

    return pl.pallas_call(
        body, name="gather_weights", out_shape=_sds((N_DEV * m_per, n), x_shard.dtype),
        in_specs=[pl.BlockSpec(memory_space=pltpu.VMEM)], out_specs=pl.BlockSpec(memory_space=pltpu.VMEM),
        scratch_shapes=[pltpu.SemaphoreType.DMA((7,)), pltpu.SemaphoreType.DMA((7,)), pltpu.SemaphoreType.DMA],
        compiler_params=pltpu.CompilerParams(vmem_limit_bytes=VMEM_LIMIT),
    )(x_shard)


class _Exchange:
    def __init__(self, srcs, whole):
        self.srcs, self.whole, self.n = list(srcs), list(whole), len(srcs)
        self.rows = [a.shape[0] // N_DEV if w is False else a.shape[0] for a, w in zip(self.srcs, self.whole)]
        self.cols = [a.shape[1] // N_DEV if w == "col" else a.shape[1] for a, w in zip(self.srcs, self.whole)]

    def specs(self):
        return [pl.BlockSpec(memory_space=pl.ANY)] * self.n

    def out_shape(self):
        return [_sds((N_DEV, r, c), a.dtype) for r, c, a in zip(self.rows, self.cols, self.srcs)]

    def scratch(self):
        return [pltpu.SemaphoreType.DMA((self.n * 7,)), pltpu.SemaphoreType.DMA((self.n * 7,)),
                pltpu.SemaphoreType.DMA((self.n,))]

    def _copies(self, ins, outs, sems):
        send_sems, recv_sems, local_sems = sems
        _, me = _peer(0)
        for w in range(self.n):
            for k in range(N_DEV):
                peer, pidx = _peer(k)
                if self.whole[w] is True:
                    src = ins[w]
                elif self.whole[w] == "col":
                    src = ins[w].at[:, pl.ds(pl.multiple_of(pidx * self.cols[w], self.cols[w]), self.cols[w])]
                else:
                    src = ins[w].at[pl.ds(pidx * self.rows[w], self.rows[w]), :]
                if k == 0:
                    yield k, pltpu.make_async_copy(src, outs[w].at[me], local_sems.at[w]), None
                else:
                    sem = w * 7 + k - 1
                    out = pltpu.make_async_remote_copy(src_ref=src, dst_ref=outs[w].at[me], send_sem=send_sems.at[sem],
                                                       recv_sem=recv_sems.at[sem], device_id=peer, device_id_type=MESH)
                    back = pltpu.make_async_remote_copy(src_ref=src, dst_ref=outs[w].at[pidx], send_sem=send_sems.at[sem],
                                                        recv_sem=recv_sems.at[sem], device_id=peer, device_id_type=MESH)
                    yield k, out, back

    def start(self, ins, outs, sems):
        for _, out, _ in self._copies(ins, outs, sems):
            out.start()

    def wait(self, ins, outs, sems):
        for k, out, back in self._copies(ins, outs, sems):
            if k == 0:
                out.wait()
            else:
                back.wait_recv()
                out.wait_send()


def _carry(ex, refs, n_in, n_out):
    nh = ex.n if ex is not None else 0
    ins, hin = refs[:n_in], refs[n_in:n_in + nh]
    outs = refs[n_in + nh:n_in + nh + n_out]
    hout = refs[n_in + nh + n_out:n_in + 2 * nh + n_out]
    rest = refs[n_in + 2 * nh + n_out:]
    if ex is None:
        return ins, outs, rest, None
    return ins, outs, rest[:-3], (hin, hout, rest[-3:])


def _exchange_now(srcs, whole, name):
    ex = _Exchange(srcs, whole)

    def body(*refs):
        _, _, _, (hin, hout, sems) = _carry(ex, refs, 0, 0)
        ex.start(hin, hout, sems)
        ex.wait(hin, hout, sems)

    return pl.pallas_call(body, name=name, out_shape=ex.out_shape(), in_specs=ex.specs(), out_specs=ex.specs(),
                          scratch_shapes=ex.scratch())(*srcs)


def _split_exchange_start(src):
    rows, width = src.shape[0], src.shape[1] // N_DEV

    def body(src_ref, land_ref, send_sems, recv_sems, src_thru, land_thru, token):
        barrier = pltpu.get_barrier_semaphore()
        for k in range(1, N_DEV):
            pl.semaphore_signal(barrier, inc=1, device_id=_peer(k)[0], device_id_type=MESH)
        pl.semaphore_wait(barrier, N_DEV - 1)
        _, me = _peer(0)
        for k in range(1, N_DEV):
            peer, pidx = _peer(k)
            pltpu.make_async_remote_copy(
                src_ref=src_ref.at[:, pl.ds(pl.multiple_of(pidx * width, width), width)], dst_ref=land_ref.at[me],
                send_sem=send_sems.at[k - 1], recv_sem=recv_sems.at[k - 1], device_id=peer, device_id_type=MESH).start()
        token[...] = jnp.zeros_like(token)

    hbm = pl.BlockSpec(memory_space=pltpu.HBM)
    sem = pl.BlockSpec(memory_space=pltpu.SEMAPHORE)
    land = lax.empty((N_DEV, rows, width), src.dtype)
    return pl.pallas_call(
        body, name="w_in_exchange_start",
        out_shape=(pltpu.SemaphoreType.DMA((N_DEV - 1,)), pltpu.SemaphoreType.DMA((N_DEV - 1,)),
                   pltpu.HBM(src.shape, src.dtype), pltpu.HBM(land.shape, land.dtype), _sds((8, LANES))),
        in_specs=(hbm, hbm), out_specs=(sem, sem, hbm, hbm, pl.BlockSpec(memory_space=pltpu.VMEM)),
        input_output_aliases={0: 2, 1: 3},
        compiler_params=pltpu.CompilerParams(has_side_effects=pltpu.SideEffectType.DATAFLOW_SIDE_EFFECTING, collective_id=0),
    )(pltpu.with_memory_space_constraint(src, pltpu.HBM), pltpu.with_memory_space_constraint(land, pltpu.HBM))


def _split_exchange_wait(send_sems, recv_sems, src_thru, land_thru, after):
    width = land_thru.shape[2]

    def body(src_ref, land_ref, send_sems, recv_sems, after_ref, src_dead, got_ref):
        _, me = _peer(0)
        for k in range(1, N_DEV):
            peer, pidx = _peer(k)
            copy = pltpu.make_async_remote_copy(
                src_ref=src_ref.at[:, pl.ds(pl.multiple_of(pidx * width, width), width)], dst_ref=land_ref.at[pidx],
                send_sem=send_sems.at[k - 1], recv_sem=recv_sems.at[k - 1], device_id=peer, device_id_type=MESH)
            copy.wait_send()
            copy.wait_recv()

    hbm = pl.BlockSpec(memory_space=pltpu.HBM)
    sem = pl.BlockSpec(memory_space=pltpu.SEMAPHORE)
    return pl.pallas_call(
        body, name="w_in_exchange_wait",
        out_shape=(pltpu.HBM(src_thru.shape, src_thru.dtype), pltpu.HBM(land_thru.shape, land_thru.dtype)),
        in_specs=(hbm, hbm, sem, sem, pl.BlockSpec(memory_space=pl.ANY)), out_specs=(hbm, hbm),
        input_output_aliases={0: 0, 1: 1},
        compiler_params=pltpu.CompilerParams(has_side_effects=pltpu.SideEffectType.DATAFLOW_SIDE_EFFECTING),
    )(src_thru, land_thru, send_sems, recv_sems, after)


def _adamw(w, m, v, gparts, name):
    r, c = w.shape
    tr = r if r * c <= 256 * 1024 else 128 * 1024 // c

    def body(w_ref, m_ref, v_ref, g_ref, go_ref, d_ref, mo_ref, vo_ref):
        g = g_ref[0].astype(F32)
        for i in range(1, N_DEV):
            g = g + g_ref[i].astype(F32)
        go_ref[...] = g
        d_ref[...], mo_ref[...], vo_ref[...] = _adamw_step(w_ref[...], m_ref[...], v_ref[...], g)

    blk = pl.BlockSpec((tr, c), lambda i: (i, 0))
    return pl.pallas_call(
        body, name=name, grid=(r // tr,),
        in_specs=[blk, blk, blk, pl.BlockSpec((N_DEV, tr, c), lambda i: (0, i, 0))],
        out_specs=[blk] * 4, out_shape=[_sds((r, c))] * 4,
        compiler_params=_cp(("parallel",)),
    )(w, m, v, gparts)


def _adamw_step(w, m, v, g):
    nm = ADAM_B1 * m + (1.0 - ADAM_B1) * g
    nv = ADAM_B2 * v + (1.0 - ADAM_B2) * (g * g)
    m_hat = nm / (1.0 - ADAM_B1 ** ADAM_STEP)
    v_hat = nv / (1.0 - ADAM_B2 ** ADAM_STEP)
    return -ADAM_LR * (m_hat / (jnp.sqrt(v_hat) + ADAM_EPS) + ADAM_WD * w), nm, nv


def _sum_slots(parts):
    def body(p_ref, o_ref):
        g = p_ref[0]
        for i in range(1, N_DEV):
            g = g + p_ref[i]
        o_ref[...] = g

    return pl.pallas_call(body, name="sum_small_grads", out_shape=_sds(parts.shape[1:]))(parts)


def _adamw_native(ws, ms, vs, gs):
    n = len(ws)

    def body(*refs):
        for i in range(n):
            w_ref, m_ref, v_ref, g_ref = refs[i], refs[n + i], refs[2 * n + i], refs[3 * n + i]
            d, nm, nv = _adamw_step(w_ref[...], m_ref[...], v_ref[...], g_ref[...])
            refs[4 * n + i][...] = d
            refs[5 * n + i][...] = nm
            refs[6 * n + i][...] = nv

    outs = pl.pallas_call(body, name="adamw_small", out_shape=[_sds(w.shape) for w in ws] * 3,
                          compiler_params=pltpu.CompilerParams(vmem_limit_bytes=VMEM_LIMIT))(*ws, *ms, *vs, *gs)
    return outs[:n], outs[n:2 * n], outs[2 * n:]


def _block_diag(a, states_first):
    a4 = a.reshape(N_CB, 8, SSM_GROUP, SSM_STATE)
    eye = jnp.eye(8, dtype=a.dtype)
    if states_first:
        return jnp.einsum("bgcp,gh->bgphc", a4, eye).reshape(N_CB, CB_STATES, LANES)
    return jnp.einsum("bgcp,gh->bgchp", a4, eye).reshape(N_CB, LANES, CB_STATES)


def _block_diag_of(full, states_first):
    if states_first:
        picked = jnp.einsum("bgphc,gh->bgcp", full.reshape(N_CB, 8, SSM_STATE, 8, SSM_GROUP), jnp.eye(8, dtype=full.dtype))
    else:
        picked = jnp.einsum("bgchp,gh->bgcp", full.reshape(N_CB, 8, SSM_GROUP, 8, SSM_STATE), jnp.eye(8, dtype=full.dtype))
    return picked.reshape(SSM_GROUPS, SSM_GROUP, SSM_STATE)


SMALL_EARLY = ("ssm_a_re", "ssm_a_im", "ssm_log_dt", "ssm_b_re", "ssm_b_im", "ssm_c_re", "ssm_c_im", "ssm_d", "glu_b",
               "attn_out_norm_g", "ssm_out_norm_g", "norm2_g")
SMALL_MID = ("q_norm_g", "k_norm_g")
SMALL_LATE = ("norm1_g",)
SMALL = SMALL_EARLY + SMALL_MID + SMALL_LATE


def _pack_small(arrs):
    parts = []
    for a in arrs:
        flat = a.reshape(-1)
        rows = -(-flat.shape[0] // (8 * LANES)) * 8
        parts.append(jnp.pad(flat, (0, rows * LANES - flat.shape[0])).reshape(rows, LANES))
    return jnp.concatenate(parts, axis=0)


def _unpack_small(packed, shapes):
    out, r0 = [], 0
    for shp in shapes:
        size = math.prod(shp)
        rows = -(-size // (8 * LANES)) * 8
        out.append(packed[r0:r0 + rows].reshape(-1)[:size].reshape(shp))
        r0 += rows
    return out


def kernel(x, norm1_g, w_in, q_norm_g, k_norm_g, ssm_a_re, ssm_a_im, ssm_log_dt, ssm_b_re, ssm_b_im, ssm_c_re, ssm_c_im, ssm_d, glu_w, glu_b, attn_out_norm_g, ssm_out_norm_g, w_out, norm2_g, w_mlp_up, w_mlp_down, loss_target, m_norm1_g, m_w_in, m_q_norm_g, m_k_norm_g, m_ssm_a_re, m_ssm_a_im, m_ssm_log_dt, m_ssm_b_re, m_ssm_b_im, m_ssm_c_re, m_ssm_c_im, m_ssm_d, m_glu_w, m_glu_b, m_attn_out_norm_g, m_ssm_out_norm_g, m_w_out, m_norm2_g, m_w_mlp_up, m_w_mlp_down, v_norm1_g, v_w_in, v_q_norm_g, v_k_norm_g, v_ssm_a_re, v_ssm_a_im, v_ssm_log_dt, v_ssm_b_re, v_ssm_b_im, v_ssm_c_re, v_ssm_c_im, v_ssm_d, v_glu_w, v_glu_b, v_attn_out_norm_g, v_ssm_out_norm_g, v_w_out, v_norm2_g, v_w_mlp_up, v_w_mlp_down):
    weights = dict(norm1_g=norm1_g, w_in=w_in, q_norm_g=q_norm_g, k_norm_g=k_norm_g, ssm_a_re=ssm_a_re,
                   ssm_a_im=ssm_a_im, ssm_log_dt=ssm_log_dt, ssm_b_re=ssm_b_re, ssm_b_im=ssm_b_im,
                   ssm_c_re=ssm_c_re, ssm_c_im=ssm_c_im, ssm_d=ssm_d, glu_w=glu_w, glu_b=glu_b,
                   attn_out_norm_g=attn_out_norm_g, ssm_out_norm_g=ssm_out_norm_g, w_out=w_out, norm2_g=norm2_g,
                   w_mlp_up=w_mlp_up, w_mlp_down=w_mlp_down)
    mom_m = dict(norm1_g=m_norm1_g, w_in=m_w_in, q_norm_g=m_q_norm_g, k_norm_g=m_k_norm_g, ssm_a_re=m_ssm_a_re,
                 ssm_a_im=m_ssm_a_im, ssm_log_dt=m_ssm_log_dt, ssm_b_re=m_ssm_b_re, ssm_b_im=m_ssm_b_im,
                 ssm_c_re=m_ssm_c_re, ssm_c_im=m_ssm_c_im, ssm_d=m_ssm_d, glu_w=m_glu_w, glu_b=m_glu_b,
                 attn_out_norm_g=m_attn_out_norm_g, ssm_out_norm_g=m_ssm_out_norm_g, w_out=m_w_out,
                 norm2_g=m_norm2_g, w_mlp_up=m_w_mlp_up, w_mlp_down=m_w_mlp_down)
    mom_v = dict(norm1_g=v_norm1_g, w_in=v_w_in, q_norm_g=v_q_norm_g, k_norm_g=v_k_norm_g, ssm_a_re=v_ssm_a_re,
                 ssm_a_im=v_ssm_a_im, ssm_log_dt=v_ssm_log_dt, ssm_b_re=v_ssm_b_re, ssm_b_im=v_ssm_b_im,
                 ssm_c_re=v_ssm_c_re, ssm_c_im=v_ssm_c_im, ssm_d=v_ssm_d, glu_w=v_glu_w, glu_b=v_glu_b,
                 attn_out_norm_g=v_attn_out_norm_g, ssm_out_norm_g=v_ssm_out_norm_g, w_out=v_w_out,
                 norm2_g=v_norm2_g, w_mlp_up=v_w_mlp_up, w_mlp_down=v_w_mlp_down)
    order = list(weights)

    xs, tgt = x[0], loss_target[0]
    s = xs.shape[0]
    assert s % ATTN_CHUNK == 0 and (s // N_SEG) % SSM_LK == 0
    seg_len = s // N_SEG
    n_sq = seg_len.bit_length() - 1
    assert 1 << n_sq == seg_len

    w_in_t = _gather_rows(w_in[0].T.astype(BF16))
    w_in_full = w_in_t.T
    later = _Exchange([glu_w[0].astype(BF16), w_out[0].astype(BF16), w_mlp_up[0].T.astype(BF16),
                       w_mlp_down[0].astype(BF16)], [True] * 4)

    gq = jnp.tile(q_norm_g[0], ATTN_W // HEAD_DIM)[None]
    gk = jnp.tile(k_norm_g[0], ATTN_W // HEAD_DIM)[None]
    lane = jnp.arange(MXU_W) // HEAD_DIM
    gmat = jnp.where(lane[:, None] == lane[None, :], 1.0 / HEAD_DIM, 0.0).astype(BF16)
    a_re3 = ssm_a_re[0][:, None, :]
    a_im3 = ssm_a_im[0][:, None, :]
    ldt3 = ssm_log_dt[0][:, None, None]
    b_re_t = jnp.swapaxes(ssm_b_re[0], 1, 2)
    b_im_t = jnp.swapaxes(ssm_b_im[0], 1, 2)
    c_re, c_im = ssm_c_re[0], ssm_c_im[0]
    dskip = ssm_d[0].reshape(1, SSM_W)

    qk, qn, kn, v, u, xn = _fwd_proj(xs, norm1_g, w_in_full, gq, gk, gmat)
    attn, lse, glu_g, w_out_g, w_up_g, w_down_g = _attn_fwd(qn, kn, v, later)
    glu_full = glu_g.reshape(SSM_W, SSM_W)
    w_out_full = w_out_g.reshape(D_MODEL, D_MODEL)
    w_up_t = w_up_g.reshape(D_FF, D_MODEL)
    w_down_full = w_down_g.reshape(D_FF, D_MODEL)
    w_up_full, w_out_t, w_down_t, glu_t = w_up_t.T, w_out_full.T, w_down_full.T, glu_full.T

    abr3, abi3, bbr, bbi = _ssm_discretize(a_re3, a_im3, ldt3, b_re_t, b_im_t)
    abr, abi = abr3.reshape(1, N_STATE), abi3.reshape(1, N_STATE)
    pw_r, pw_i = _ssm_power(abr, abi, n_sq)
    bb_r, bb_i = _block_diag(bbr, False).astype(BF16), _block_diag(bbi, False).astype(BF16)
    bbt_r, bbt_i = _block_diag(bbr, True).astype(BF16), _block_diag(bbi, True).astype(BF16)
    cc_r, cc_i = _block_diag(c_re, True).astype(BF16), _block_diag(c_im, True).astype(BF16)
    cct_r, cct_i = _block_diag(c_re, False).astype(BF16), _block_diag(c_im, False).astype(BF16)
    seg_major = lambda a: jnp.swapaxes(a.reshape(N_SEG, seg_len, SSM_W), 0, 1)
    seg_minor = lambda a: jnp.swapaxes(a, 0, 1).reshape(s, SSM_W)
    u3 = seg_major(u)
    zero_fin = jnp.zeros((N_SEG, N_STATE), F32)
    ssm_args = (abr, abi, pw_r, pw_i)
    xf_r, xf_i = _ssm_fwd(u3, *ssm_args, zero_fin, zero_fin, bb_r, bb_i, cc_r, cc_i, dskip, True)
    y3, xst_r, xst_i = _ssm_fwd(u3, *ssm_args, xf_r, xf_i, bb_r, bb_i, cc_r, cc_i, dskip, False)
    y = seg_minor(y3)

    mix, x2, h = _fwd_mix(attn, y, xs, glu_full, glu_b, attn_out_norm_g, ssm_out_norm_g, w_out_full, norm2_g)
    r_act, hdn = _mlp_up(h, w_up_full)
    dy, dyb, sse = _mlp_down_loss(hdn, w_down_full, x2, tgt)

    dup = _mlp_down_bwd(dyb, w_down_t, r_act)
    g_w_down = _mm_tn(hdn, dyb, "grad_w_down")
    dx2, dx2b, g_norm2 = _mlp_up_bwd(dup, w_up_t, x2, norm2_g, dy)
    g_w_up = _mm_tn(h, dup, "grad_w_up")
    dattn, dys, z_b, dpre_b, g_ga, g_gs, g_glu_b = _mix_bwd(dx2b, w_out_t, attn, y, glu_full, glu_b, glu_t,
                                                             attn_out_norm_g, ssm_out_norm_g)
    g_w_out = _mm_tn(mix, dx2b, "grad_w_out")
    g_glu_w = _mm_tn(z_b, dpre_b, "grad_glu_w")
    dy3 = seg_major(dys)
    bwd_args = (u3, dy3, xst_r, xst_i, abr, abi, pw_r, pw_i)
    lf_r, lf_i = _ssm_bwd(*bwd_args, zero_fin, zero_fin, bb_r, bb_i, bbt_r, bbt_i, cct_r, cct_i, dskip, True)
    du3, dab_r, dab_i, dbb_r, dbb_i, dcc_r, dcc_i, g_d = _ssm_bwd(*bwd_args, lf_r, lf_i, bb_r, bb_i, bbt_r, bbt_i,
                                                                 cct_r, cct_i, dskip, False)
    g_a_re3, g_a_im3, g_ldt3, g_b_re_t, g_b_im_t = _ssm_discretize_bwd(
        a_re3, a_im3, ldt3, b_re_t, b_im_t, dab_r.reshape(a_re3.shape), dab_i.reshape(a_re3.shape),
        _block_diag_of(dbb_r, True), _block_diag_of(dbb_i, True))
    g_c_re, g_c_im = _block_diag_of(dcc_r, False), _block_diag_of(dcc_i, False)
    small_grads = dict(
        ssm_a_re=g_a_re3.reshape(ssm_a_re.shape), ssm_a_im=g_a_im3.reshape(ssm_a_im.shape),
        ssm_log_dt=g_ldt3.reshape(ssm_log_dt.shape), ssm_b_re=jnp.swapaxes(g_b_re_t, 1, 2)[None],
        ssm_b_im=jnp.swapaxes(g_b_im_t, 1, 2)[None], ssm_c_re=g_c_re[None], ssm_c_im=g_c_im[None],
        ssm_d=g_d.reshape(ssm_d.shape), glu_b=g_glu_b, attn_out_norm_g=g_ga, ssm_out_norm_g=g_gs, norm2_g=g_norm2)

    early = _Exchange([g_glu_w, g_w_out, g_w_up, g_w_down, _pack_small([small_grads[n] for n in SMALL_EARLY] + [sse])],
                      [False, False, "col", False, True])
    dqn, dkn, dv, p_glu, p_w_out, p_w_up, p_w_down, p_early = _attn_bwd(qn, kn, v, attn, lse, dattn, early)

    dproj, g_gq, g_gk = _qk_bwd(dqn, dkn, qk, dv, seg_minor(du3), gq, gk, gmat)
    g_w_in = _mm_tn(xn, dproj, "grad_w_in")
    small_grads["q_norm_g"] = g_gq.reshape(ATTN_W // HEAD_DIM, HEAD_DIM).sum(0)[None]
    small_grads["k_norm_g"] = g_gk.reshape(ATTN_W // HEAD_DIM, HEAD_DIM).sum(0)[None]
    send_sems, recv_sems, g_thru, land, token = _split_exchange_start(g_w_in)
    grad_x, g_norm1 = _in_bwd(dproj, w_in_t, xs, norm1_g + token[:1, :1], dx2)
    g_w_in, landed = _split_exchange_wait(send_sems, recv_sems, g_thru, land, g_norm1)
    me = 4 * lax.axis_index("x") + 2 * lax.axis_index("y") + lax.axis_index("c")
    shard_w = g_w_in.shape[1] // N_DEV
    own = lax.dynamic_slice(g_w_in, (0, me * shard_w), (g_w_in.shape[0], shard_w))
    p_w_in = lax.dynamic_update_slice(landed, own[None], (me, 0, 0))
    p_mid, p_late = _exchange_now([_pack_small([small_grads[n] for n in SMALL_MID]), _pack_small([g_norm1])],
                                  [True, True], "exchange_tail")

    res = {}
    for name, gp in (("w_in", p_w_in), ("glu_w", p_glu), ("w_out", p_w_out), ("w_mlp_up", p_w_up), ("w_mlp_down", p_w_down)):
        outs = _adamw(weights[name][0], mom_m[name][0], mom_v[name][0], gp, "adamw_" + name)
        res[name] = [o[None] for o in outs]
    shapes = [weights[n].shape for n in SMALL_EARLY] + [sse.shape] + [weights[n].shape for n in SMALL_MID + SMALL_LATE]
    g_small = _unpack_small(_sum_slots(jnp.concatenate([p_early, p_mid, p_late], axis=1)), shapes)
    loss = 0.5 * g_small.pop(len(SMALL_EARLY))[0, 0] / D_MODEL
    d_small, m_small, v_small = _adamw_native([weights[n] for n in SMALL], [mom_m[n] for n in SMALL],
                                              [mom_v[n] for n in SMALL], g_small)
    for i, n in enumerate(SMALL):
        res[n] = [g_small[i], d_small[i], m_small[i], v_small[i]]

    return (loss, grad_x[None], *[res[n][0] for n in order], *[res[n][1] for n in order],
            *[res[n][2] for n in order], *[res[n][3] for n in order])
```

```python
import math

import jax
import jax.numpy as jnp
from jax import lax
from jax.experimental import pallas as pl
from jax.experimental.pallas import tpu as pltpu

F32 = jnp.float32
BF16 = jnp.bfloat16

D_MODEL = 1024
ATTN_W = 512
HEAD_DIM = 64
SSM_W = 512
SSM_GROUP = 16
SSM_GROUPS = 32
SSM_STATE = 64
N_STATE = SSM_GROUPS * SSM_STATE
D_FF = 4096
EPS = 1e-6
NEG_INF = -1e30
ATTN_CHUNK = 2048
ATTN_BLOCK = 128
DILATIONS = (1, 4, 16)
N_SEG = 8
SSM_LK = 64
N_DEV = 8
LANES = 128
MXU_W = 256

ADAM_LR = 0.001
ADAM_B1 = 0.9
ADAM_B2 = 0.999
ADAM_EPS = 1e-08
ADAM_WD = 0.01
ADAM_STEP = 10

VMEM_LIMIT = 56 * 1024 * 1024
GELU_C = math.sqrt(2.0 / math.pi)
MESH = pl.DeviceIdType.MESH


def _cp(sem, vmem=VMEM_LIMIT):
    return pltpu.CompilerParams(dimension_semantics=sem, vmem_limit_bytes=vmem)


def _dot(a, b):
    return jnp.dot(a, b, preferred_element_type=F32)


def _dot_tn(a, b):
    return lax.dot_general(a, b, (((0,), (0,)), ((), ())), preferred_element_type=F32)


def _group_mean(x2, gmat):
    hi = x2.astype(BF16)
    lo = (x2 - hi.astype(F32)).astype(BF16)
    w = gmat.shape[0]
    return jnp.concatenate([_dot(hi[:, c:c + w], gmat) + _dot(lo[:, c:c + w], gmat) for c in range(0, x2.shape[1], w)],
                           axis=1)


def _rms(x):
    return lax.rsqrt(jnp.mean(x * x, axis=-1, keepdims=True) + EPS)


def _rms_bwd(dy, x, g, n):
    r = _rms(x)
    gdy = dy * g
    dx = r * gdy - x * (r * r * r) * (jnp.sum(gdy * x, axis=-1, keepdims=True) / n)
    return dx, dy * (x * r)


def _gelu(y):
    t = jnp.tanh(GELU_C * (y + 0.044715 * (y * y * y)))
    return 0.5 * y * (1.0 + t), t


def _full(shape):
    nd = len(shape)
    return pl.BlockSpec(shape, lambda *_: (0,) * nd)


def _sds(shape, dtype=F32):
    return jax.ShapeDtypeStruct(shape, dtype)


def _fwd_proj(x, g1, w_in, gq, gk, gmat, tm=512):
    s = x.shape[0]

    def body(x_ref, g1_ref, w_ref, gq_ref, gk_ref, gm_ref, qk_ref, qn_ref, kn_ref, v_ref, u_ref, xn_ref):
        xv = x_ref[...]
        xnb = ((xv * _rms(xv)) * g1_ref[...]).astype(BF16)
        xn_ref[...] = xnb
        proj = _dot(xnb, w_ref[...])
        q = proj[:, :ATTN_W]
        k = proj[:, ATTN_W:2 * ATTN_W]
        qk_ref[...] = proj[:, :2 * ATTN_W]
        v_ref[...] = proj[:, 2 * ATTN_W:3 * ATTN_W]
        u_ref[...] = proj[:, 3 * ATTN_W:]
        gm = gm_ref[...]
        qn_ref[...] = (q * lax.rsqrt(_group_mean(q * q, gm) + EPS)) * gq_ref[...]
        kn_ref[...] = (k * lax.rsqrt(_group_mean(k * k, gm) + EPS)) * gk_ref[...]

    row = lambda w: pl.BlockSpec((tm, w), lambda i: (i, 0))
    return pl.pallas_call(
        body, name="fwd_proj", grid=(s // tm,),
        in_specs=[row(D_MODEL), _full((1, D_MODEL)), _full((D_MODEL, 4 * ATTN_W)), _full((1, ATTN_W)),
                  _full((1, ATTN_W)), _full((MXU_W, MXU_W))],
        out_specs=[row(2 * ATTN_W), row(ATTN_W), row(ATTN_W), row(ATTN_W), row(ATTN_W), row(D_MODEL)],
        out_shape=[_sds((s, 2 * ATTN_W)), _sds((s, ATTN_W)), _sds((s, ATTN_W)), _sds((s, ATTN_W)),
                   _sds((s, ATTN_W)), _sds((s, D_MODEL), BF16)],
        compiler_params=_cp(("parallel",)),
    )(x, g1, w_in, gq, gk, gmat)


def _attn_rows(t, d, nb):
    if d == 1:
        q0 = t * ATTN_BLOCK
        return (t, pl.ds(q0, ATTN_BLOCK), pl.ds(ATTN_CHUNK + q0, ATTN_BLOCK),
                pl.ds(ATTN_CHUNK - ATTN_BLOCK + q0, ATTN_BLOCK))
    r = t // nb
    b = t % nb
    return (b, pl.ds(ATTN_BLOCK * b * d + r, ATTN_BLOCK, stride=d),
            pl.ds(ATTN_CHUNK + ATTN_BLOCK * b * d + r, ATTN_BLOCK, stride=d),
            pl.ds(ATTN_CHUNK + ATTN_BLOCK * (b - 1) * d + r, ATTN_BLOCK, stride=d))


def _attn_masks():
    row = lax.broadcasted_iota(jnp.int32, (ATTN_BLOCK, LANES), 0)
    col = lax.broadcasted_iota(jnp.int32, (ATTN_BLOCK, LANES), 1)
    return row, col


NBLK = ATTN_CHUNK // ATTN_BLOCK


def _attn_bias(bias_s):
    row, col = _attn_masks()
    bias_s[:, pl.ds(0, LANES)] = jnp.where(col >= row, 0.0, NEG_INF)
    bias_s[:, pl.ds(LANES, LANES)] = jnp.where(col <= row, 0.0, NEG_INF)
    return col < HEAD_DIM


def _attn_fwd(qn, kn, v, ex=None, group=4):
    s = qn.shape[0]
    nch = s // ATTN_CHUNK
    scale = HEAD_DIM ** -0.5
    npat = len(DILATIONS)
    n_hp = ATTN_W // LANES

    def body(*refs):
        ((q_ref, kp_ref, kc_ref, vp_ref, vc_ref), (o_ref, lse_ref),
         (kk, vv, kt_s, vb_s, bias_s, m_s, a_s), hx) = _carry(ex, refs, 5, 2)
        i = pl.program_id(1)
        if ex is not None:
            @pl.when(jnp.logical_and(pl.program_id(0) == 0, i == 0))
            def _():
                ex.start(*hx)

        kk[pl.ds(0, ATTN_CHUNK), :] = kp_ref[...]
        kk[pl.ds(ATTN_CHUNK, ATTN_CHUNK), :] = kc_ref[...]
        vv[pl.ds(0, ATTN_CHUNK), :] = vp_ref[...]
        vv[pl.ds(ATTN_CHUNK, ATTN_CHUNK), :] = vc_ref[...]
        head0 = _attn_bias(bias_s)
        first_pen = jnp.where(i > 0, 0.0, NEG_INF)

        for p, d in enumerate(DILATIONS):
            nb = ATTN_CHUNK // (ATTN_BLOCK * d)

            def prep(t, d=d, nb=nb):
                b, _, crows, prows = _attn_rows(t, d, nb)
                nxt = t + 1 if b + 1 < nb else None
                ktc = kk[crows, :].T.astype(BF16)
                vc = vv[crows, :]
                v0 = jnp.where(head0, vc, 1.0).astype(BF16)
                v1 = jnp.where(head0, 1.0, vc).astype(BF16)
                kt_s[t, :, pl.ds(LANES, LANES)] = ktc
                vb_s[2 * t, pl.ds(ATTN_BLOCK, ATTN_BLOCK), :] = v0
                vb_s[2 * t + 1, pl.ds(ATTN_BLOCK, ATTN_BLOCK), :] = v1
                if nxt is not None:
                    kt_s[nxt, :, pl.ds(0, LANES)] = ktc
                    vb_s[2 * nxt, pl.ds(0, ATTN_BLOCK), :] = v0
                    vb_s[2 * nxt + 1, pl.ds(0, ATTN_BLOCK), :] = v1
                if b == 0:
                    kt_s[t, :, pl.ds(0, LANES)] = kk[prows, :].T.astype(BF16)
                    vp = vv[prows, :]
                    vb_s[2 * t, pl.ds(0, ATTN_BLOCK), :] = jnp.where(head0, vp, 1.0).astype(BF16)
                    vb_s[2 * t + 1, pl.ds(0, ATTN_BLOCK), :] = jnp.where(head0, 1.0, vp).astype(BF16)

            def main(tg, p=p, d=d, nb=nb):
                st = []
                for g in range(group):
                    t = tg * group + g
                    b, qrows, _, _ = _attn_rows(t, d, nb)
                    q = q_ref[qrows, :] * scale
                    for h in range(2):
                        hm = head0 if h == 0 else jnp.logical_not(head0)
                        st.append(dict(t=t, b=b, qrows=qrows, sc=_dot(jnp.where(hm, q, 0.0).astype(BF16), kt_s[t])))
                for e in st:
                    sc = e["sc"] + bias_s[...]
                    s_p = sc[:, :LANES] + first_pen if e["b"] == 0 else sc[:, :LANES]
                    s_c = sc[:, LANES:]
                    m = jnp.max(jnp.maximum(s_p, s_c), axis=-1, keepdims=True)
                    e["eb"] = jnp.concatenate([jnp.exp(s_p - m), jnp.exp(s_c - m)], axis=1).astype(BF16)
                    e["m"] = jnp.broadcast_to(m, (ATTN_BLOCK, LANES))
                for g in range(group):
                    e0, e1 = st[2 * g], st[2 * g + 1]
                    t = e0["t"]
                    m_s[p, e0["qrows"], :] = jnp.where(head0, e0["m"], e1["m"])
                    a_s[2 * p, e0["qrows"], :] = _dot(e0["eb"], vb_s[2 * t])
                    a_s[2 * p + 1, e0["qrows"], :] = _dot(e1["eb"], vb_s[2 * t + 1])

            for g in range(group):
                prep(g)
            for tg in range(NBLK // group):
                if tg + 1 < NBLK // group:
                    for g in range(group):
                        prep((tg + 1) * group + g)
                main(tg)

        def merge(t, carry):
            rows = pl.ds(pl.multiple_of(t * ATTN_BLOCK, ATTN_BLOCK), ATTN_BLOCK)
            m_all = m_s[0, rows, :]
            for p in range(1, npat):
                m_all = jnp.maximum(m_all, m_s[p, rows, :])
            num = jnp.zeros((ATTN_BLOCK, LANES), F32)
            den = jnp.zeros((ATTN_BLOCK, LANES), F32)
            for p in range(npat):
                w = jnp.exp(m_s[p, rows, :] - m_all)
                a0, a1 = a_s[2 * p, rows, :], a_s[2 * p + 1, rows, :]
                num = num + jnp.where(head0, a0, a1) * w
                den = den + pltpu.roll(jnp.where(head0, a1, a0), HEAD_DIM, 1) * w
            o_ref[rows, :] = num / den
            lse_ref[rows, :] = m_all + jnp.log(den)
            return carry

        lax.fori_loop(0, NBLK, merge, 0, unroll=2)
        if ex is not None:
            @pl.when(jnp.logical_and(pl.program_id(0) == n_hp - 1, i == nch - 1))
            def _():
                ex.wait(*hx)

    cur = pl.BlockSpec((ATTN_CHUNK, LANES), lambda h, i: (i, h))
    prev = pl.BlockSpec((ATTN_CHUNK, LANES), lambda h, i: (jnp.maximum(i - 1, 0), h))
    hosted = ex is not None
    return pl.pallas_call(
        body, name="attn_fwd", grid=(n_hp, nch),
        in_specs=[cur, prev, cur, prev, cur] + (ex.specs() if hosted else []),
        out_specs=[cur, cur] + (ex.specs() if hosted else []),
        out_shape=[_sds((s, ATTN_W)), _sds((s, ATTN_W))] + (ex.out_shape() if hosted else []),
        scratch_shapes=[pltpu.VMEM((2 * ATTN_CHUNK, LANES), F32), pltpu.VMEM((2 * ATTN_CHUNK, LANES), F32),
                        pltpu.VMEM((NBLK, LANES, 2 * LANES), BF16), pltpu.VMEM((2 * NBLK, 2 * ATTN_BLOCK, LANES), BF16),
                        pltpu.VMEM((ATTN_BLOCK, 2 * LANES), F32),
                        pltpu.VMEM((npat, ATTN_CHUNK, LANES), F32), pltpu.VMEM((2 * npat, ATTN_CHUNK, LANES), F32)]
        + (ex.scratch() if hosted else []),
        compiler_params=_cp(("arbitrary", "arbitrary")),
    )(qn, kn, kn, v, v, *(ex.srcs if hosted else []))


def _attn_bwd(qn, kn, v, o, lse, do, ex=None, group=4):
    s = qn.shape[0]
    nch = s // ATTN_CHUNK
    scale = HEAD_DIM ** -0.5
    npat = len(DILATIONS)
    n_hp = ATTN_W // LANES

    def body(*refs):
        ((q_ref, kp_ref, kc_ref, vp_ref, vc_ref, o_ref, lse_ref, do_ref), (dq_ref, dk_ref, dv_ref),
         (kk, vv, dkk, dvv, kt_s, vt_s, kn_s, bias_s, dq_s, dl_s, dkb, dvb), hx) = _carry(ex, refs, 8, 3)
        step = pl.program_id(1)
        i = nch - 1 - step
        if ex is not None:
            @pl.when(jnp.logical_and(pl.program_id(0) == 0, step == 0))
            def _():
                ex.start(*hx)

        kk[pl.ds(0, ATTN_CHUNK), :] = kp_ref[...]
        kk[pl.ds(ATTN_CHUNK, ATTN_CHUNK), :] = kc_ref[...]
        vv[pl.ds(0, ATTN_CHUNK), :] = vp_ref[...]
        vv[pl.ds(ATTN_CHUNK, ATTN_CHUNK), :] = vc_ref[...]

        @pl.when(step == 0)
        def _():
            dkk[pl.ds(ATTN_CHUNK, ATTN_CHUNK), :] = jnp.zeros((ATTN_CHUNK, LANES), F32)
            dvv[pl.ds(ATTN_CHUNK, ATTN_CHUNK), :] = jnp.zeros((ATTN_CHUNK, LANES), F32)

        @pl.when(step > 0)
        def _():
            dkk[pl.ds(ATTN_CHUNK, ATTN_CHUNK), :] = dkk[pl.ds(0, ATTN_CHUNK), :]
            dvv[pl.ds(ATTN_CHUNK, ATTN_CHUNK), :] = dvv[pl.ds(0, ATTN_CHUNK), :]

        dkk[pl.ds(0, ATTN_CHUNK), :] = jnp.zeros((ATTN_CHUNK, LANES), F32)
        dvv[pl.ds(0, ATTN_CHUNK), :] = jnp.zeros((ATTN_CHUNK, LANES), F32)
        head0 = _attn_bias(bias_s)

        def delta(t, carry):
            rows = pl.ds(pl.multiple_of(t * ATTN_BLOCK, ATTN_BLOCK), ATTN_BLOCK)
            prod = do_ref[rows, :] * o_ref[rows, :]
            d0 = jnp.sum(jnp.where(head0, prod, 0.0), axis=-1, keepdims=True)
            d1 = jnp.sum(jnp.where(head0, 0.0, prod), axis=-1, keepdims=True)
            dl_s[rows, :] = jnp.where(head0, d0, d1)
            return carry

        lax.fori_loop(0, NBLK, delta, 0, unroll=2)

        first_pen = jnp.where(i > 0, 0.0, NEG_INF)

        for p, d in enumerate(DILATIONS):
            nb = ATTN_CHUNK // (ATTN_BLOCK * d)

            def prep(t, d=d, nb=nb):
                b, _, crows, prows = _attn_rows(t, d, nb)
                nxt = t + 1 if b + 1 < nb else None
                kc = kk[crows, :]
                ktc = kc.T.astype(BF16)
                knc = (kc * scale).astype(BF16)
                vtc = vv[crows, :].T.astype(BF16)
                kt_s[t, :, pl.ds(LANES, LANES)] = ktc
                kn_s[t, pl.ds(ATTN_BLOCK, ATTN_BLOCK), :] = knc
                vt_s[t, :, pl.ds(LANES, LANES)] = vtc
                if nxt is not None:
                    kt_s[nxt, :, pl.ds(0, LANES)] = ktc
                    kn_s[nxt, pl.ds(0, ATTN_BLOCK), :] = knc
                    vt_s[nxt, :, pl.ds(0, LANES)] = vtc
                if b == 0:
                    kp = kk[prows, :]
                    kt_s[t, :, pl.ds(0, LANES)] = kp.T.astype(BF16)
                    kn_s[t, pl.ds(0, ATTN_BLOCK), :] = (kp * scale).astype(BF16)
                    vt_s[t, :, pl.ds(0, LANES)] = vv[prows, :].T.astype(BF16)

            def main(tg, p=p, d=d, nb=nb):
                st = []
                for g in range(group):
                    t = tg * group + g
                    b, qrows, _, _ = _attn_rows(t, d, nb)
                    q = q_ref[qrows, :] * scale
                    dout = do_ref[qrows, :]
                    lse_b = lse_ref[qrows, :]
                    dl_b = dl_s[qrows, :]
                    for h in range(2):
                        hm = head0 if h == 0 else jnp.logical_not(head0)
                        c0 = h * HEAD_DIM
                        qh = jnp.where(hm, q, 0.0).astype(BF16)
                        doh = jnp.where(hm, dout, 0.0).astype(BF16)
                        st.append(dict(t=t, b=b, qrows=qrows, qh=qh, doh=doh, lse=lse_b[:, c0:c0 + 1],
                                       dl=dl_b[:, c0:c0 + 1], sc=_dot(qh, kt_s[t]), dp=_dot(doh, vt_s[t])))
                for e in st:
                    sc = e["sc"] + bias_s[...]
                    if e["b"] == 0:
                        sc = jnp.concatenate([sc[:, :LANES] + first_pen, sc[:, LANES:]], axis=1)
                    pr = jnp.exp(sc - e["lse"])
                    e["ds"] = (pr * (e["dp"] - e["dl"])).astype(BF16)
                    e["pr"] = pr.astype(BF16)
                for g in range(group):
                    e0, e1 = st[2 * g], st[2 * g + 1]
                    t = e0["t"]
                    dq_s[p, e0["qrows"], :] = jnp.where(head0, _dot(e0["ds"], kn_s[t]), _dot(e1["ds"], kn_s[t]))
                    dkb[t] = _dot_tn(e0["ds"], e0["qh"]) + _dot_tn(e1["ds"], e1["qh"])
                    dvb[t] = _dot_tn(e0["pr"], e0["doh"]) + _dot_tn(e1["pr"], e1["doh"])

            def scatter(t, d=d, nb=nb):
                b, _, crows, prows = _attn_rows(t, d, nb)
                dk_c = dkb[t, pl.ds(ATTN_BLOCK, ATTN_BLOCK), :]
                dv_c = dvb[t, pl.ds(ATTN_BLOCK, ATTN_BLOCK), :]
                if b + 1 < nb:
                    dk_c = dk_c + dkb[t + 1, pl.ds(0, ATTN_BLOCK), :]
                    dv_c = dv_c + dvb[t + 1, pl.ds(0, ATTN_BLOCK), :]
                dkk[crows, :] = dkk[crows, :] + dk_c
                dvv[crows, :] = dvv[crows, :] + dv_c
                if b == 0:
                    dkk[prows, :] = dkk[prows, :] + dkb[t, pl.ds(0, ATTN_BLOCK), :]
                    dvv[prows, :] = dvv[prows, :] + dvb[t, pl.ds(0, ATTN_BLOCK), :]

            n_groups = NBLK // group
            for g in range(group):
                prep(g)
            for tg in range(n_groups):
                if tg + 1 < n_groups:
                    for g in range(group):
                        prep((tg + 1) * group + g)
                main(tg)
                if tg >= 1:
                    for g in range(group):
                        scatter((tg - 1) * group + g)
            for g in range(group):
                scatter((n_groups - 1) * group + g)

        def finish(t, carry):
            rows = pl.ds(pl.multiple_of(t * ATTN_BLOCK, ATTN_BLOCK), ATTN_BLOCK)
            acc = dq_s[0, rows, :]
            for p in range(1, npat):
                acc = acc + dq_s[p, rows, :]
            dq_ref[rows, :] = acc
            return carry

        lax.fori_loop(0, NBLK, finish, 0, unroll=2)
        dk_ref[...] = dkk[pl.ds(ATTN_CHUNK, ATTN_CHUNK), :]
        dv_ref[...] = dvv[pl.ds(ATTN_CHUNK, ATTN_CHUNK), :]
        if ex is not None:
            @pl.when(jnp.logical_and(pl.program_id(0) == n_hp - 1, step == nch - 1))
            def _():
                ex.wait(*hx)

    cur = pl.BlockSpec((ATTN_CHUNK, LANES), lambda h, t: (nch - 1 - t, h))
    prev = pl.BlockSpec((ATTN_CHUNK, LANES), lambda h, t: (jnp.maximum(nch - 2 - t, 0), h))
    big = pltpu.VMEM((2 * ATTN_CHUNK, LANES), F32)
    pair_t = pltpu.VMEM((NBLK, LANES, 2 * LANES), BF16)
    hosted = ex is not None
    return pl.pallas_call(
        body, name="attn_bwd", grid=(n_hp, nch),
        in_specs=[cur, prev, cur, prev, cur, cur, cur, cur] + (ex.specs() if hosted else []),
        out_specs=[cur, cur, cur] + (ex.specs() if hosted else []),
        out_shape=[_sds((s, ATTN_W))] * 3 + (ex.out_shape() if hosted else []),
        scratch_shapes=[big, big, big, big, pair_t, pair_t, pltpu.VMEM((NBLK, 2 * ATTN_BLOCK, LANES), BF16),
                        pltpu.VMEM((ATTN_BLOCK, 2 * LANES), F32),
                        pltpu.VMEM((npat, ATTN_CHUNK, LANES), F32), pltpu.VMEM((ATTN_CHUNK, LANES), F32),
                        pltpu.VMEM((NBLK, 2 * ATTN_BLOCK, LANES), F32), pltpu.VMEM((NBLK, 2 * ATTN_BLOCK, LANES), F32)]
        + (ex.scratch() if hosted else []),
        compiler_params=_cp(("arbitrary", "arbitrary")),
    )(qn, kn, kn, v, v, o, lse, do, *(ex.srcs if hosted else []))


def _discretize(lr, li, dt):
    mag = jnp.exp(lr * dt)
    abr = mag * jnp.cos(li * dt)
    abi = mag * jnp.sin(li * dt)
    den = lr * lr + li * li
    nr, ni = abr - 1.0, abi
    cr = (nr * lr + ni * li) / den
    ci = (ni * lr - nr * li) / den
    return abr, abi, den, nr, ni, cr, ci


def _ssm_discretize(a_re, a_im, log_dt, b_re_t, b_im_t):
    def body(ar_ref, ai_ref, ldt_ref, br_ref, bi_ref, abr_ref, abi_ref, bbr_ref, bbi_ref):
        abr, abi, _, _, _, cr, ci = _discretize(ar_ref[...], ai_ref[...], jnp.exp(ldt_ref[...]))
        br, bi = br_ref[...], bi_ref[...]
        abr_ref[...] = abr
        abi_ref[...] = abi
        bbr_ref[...] = cr * br - ci * bi
        bbi_ref[...] = cr * bi + ci * br

    return pl.pallas_call(
        body, name="ssm_discretize",
        out_shape=[_sds(a_re.shape)] * 2 + [_sds(b_re_t.shape)] * 2,
    )(a_re, a_im, log_dt, b_re_t, b_im_t)


def _ssm_discretize_bwd(a_re, a_im, log_dt, b_re_t, b_im_t, dabr, dabi, dbbr, dbbi):
    def body(ar_ref, ai_ref, ldt_ref, br_ref, bi_ref, dabr_ref, dabi_ref, dbbr_ref, dbbi_ref,
             dar_ref, dai_ref, dldt_ref, dbr_ref, dbi_ref):
        lr, li = ar_ref[...], ai_ref[...]
        dt = jnp.exp(ldt_ref[...])
        abr, abi, den, nr, ni, cr, ci = _discretize(lr, li, dt)
        br, bi = br_ref[...], bi_ref[...]
        gbr, gbi = dbbr_ref[...], dbbi_ref[...]
        dcr = jnp.sum(gbr * br + gbi * bi, axis=1, keepdims=True)
        dci = jnp.sum(gbi * br - gbr * bi, axis=1, keepdims=True)
        dbr_ref[...] = cr * gbr + ci * gbi
        dbi_ref[...] = cr * gbi - ci * gbr
        dnr = (dcr * lr - dci * li) / den
        dni = (dcr * li + dci * lr) / den
        dden = -(dcr * cr + dci * ci) / den
        dlr = (dcr * nr + dci * ni) / den + dden * 2.0 * lr
        dli = (dcr * ni - dci * nr) / den + dden * 2.0 * li
        gabr = dabr_ref[...] + dnr
        gabi = dabi_ref[...] + dni
        dphi = gabr * abr + gabi * abi
        dth = gabi * abr - gabr * abi
        dar_ref[...] = dlr + dphi * dt
        dai_ref[...] = dli + dth * dt
        dldt_ref[...] = jnp.sum(dphi * lr + dth * li, axis=2, keepdims=True) * dt

    return pl.pallas_call(
        body, name="ssm_discretize_bwd",
        out_shape=[_sds(a_re.shape)] * 2 + [_sds(log_dt.shape)] + [_sds(b_re_t.shape)] * 2,
    )(a_re, a_im, log_dt, b_re_t, b_im_t, dabr, dabi, dbbr, dbbi)


def _ssm_power(abr, abi, n_sq):
    def body(r_ref, i_ref, or_ref, oi_ref):
        r, i = r_ref[...], i_ref[...]
        for _ in range(n_sq):
            r, i = r * r - i * i, 2.0 * r * i
        or_ref[...] = r
        oi_ref[...] = i

    return pl.pallas_call(body, name="ssm_power", out_shape=[_sds(abr.shape)] * 2)(abr, abi)


N_CB = SSM_W // LANES
CB_STATES = N_STATE // N_CB
ROWS = N_SEG * SSM_LK


class _Neg:
    def __init__(self, ref):
        self.ref = ref

    def __getitem__(self, idx):
        return -self.ref[idx]


def _seg_init(fin_r, fin_i, pw_r, pw_i, x_r, x_i, reverse):
    zero = jnp.zeros((1, N_STATE), F32)
    cr, ci = zero, zero
    order = range(N_SEG - 1, -1, -1) if reverse else range(N_SEG)
    pr = pw_r[...]
    pi = -pw_i[...] if reverse else pw_i[...]
    for j in order:
        x_r[pl.ds(j, 1), :] = cr
        x_i[pl.ds(j, 1), :] = ci
        fr, fi = fin_r[pl.ds(j, 1), :], fin_i[pl.ds(j, 1), :]
        cr, ci = fr + pr * cr - pi * ci, fi + pr * ci + pi * cr


def _permute_in(src_ref, dst):
    for c in range(N_CB):
        dst[c] = src_ref[:, :, pl.ds(c * LANES, LANES)].reshape(ROWS, LANES)


def _permute_out(src, dst_ref):
    for c in range(N_CB):
        dst_ref[:, :, pl.ds(c * LANES, LANES)] = src[c].reshape(SSM_LK, N_SEG, LANES)


def _scan_block(a_r, a_i, c, b_r, b_i, b_off, x_r, x_i, reverse, acc=None):
    cols = pl.ds(c * CB_STATES, CB_STATES)
    ar = jnp.broadcast_to(a_r[:, cols], (N_SEG, CB_STATES))
    ai = jnp.broadcast_to(a_i[:, cols], (N_SEG, CB_STATES))
    xr, xi = x_r[:, cols], x_i[:, cols]
    if acc is not None:
        sr = jnp.zeros((N_SEG, CB_STATES), F32)
        si = jnp.zeros((N_SEG, CB_STATES), F32)
    for t in range(SSM_LK):
        k = (SSM_LK - 1 - t) if reverse else t
        rows = pl.ds(k * N_SEG + b_off, N_SEG)
        xr, xi = ar * xr - ai * xi + b_r[rows, :], ar * xi + ai * xr + b_i[rows, :]
        b_r[rows, :] = xr
        b_i[rows, :] = xi
        if acc is not None:
            pr, pi = acc[0][pl.ds(k * N_SEG, N_SEG), :], acc[1][pl.ds(k * N_SEG, N_SEG), :]
            sr = sr + xr * pr + xi * pi
            si = si + xi * pr - xr * pi
    x_r[:, cols] = xr
    x_i[:, cols] = xi
    if acc is not None:
        acc[2][:, cols] += sr
        acc[3][:, cols] += si


def _ssm_fwd(u3, abr, abi, pw_r, pw_i, fin_r, fin_i, bb_r, bb_i, cc_r, cc_i, dskip, finals_only):
    sl = u3.shape[0]
    nch = sl // SSM_LK

    def body(u_ref, abr_ref, abi_ref, pwr_ref, pwi_ref, finr_ref, fini_ref, bbr_ref, bbi_ref,
             ccr_ref, cci_ref, d_ref, *rest):
        if finals_only:
            xfr_ref, xfi_ref, up, x_r, x_i = rest[:5]
        else:
            y_ref, xsr_ref, xsi_ref, up, yp, x_r, x_i = rest[:7]
        xs_r, xs_i = rest[-2 * N_CB:-N_CB], rest[-N_CB:]
        k = pl.program_id(0)

        @pl.when(k == 0)
        def _():
            _seg_init(finr_ref, fini_ref, pwr_ref, pwi_ref, x_r, x_i, False)

        if not finals_only:
            xsr_ref[0] = x_r[...]
            xsi_ref[0] = x_i[...]
        _permute_in(u_ref, up)

        def drive(c):
            lhs = up[c].astype(BF16)
            xs_r[c][...] = _dot(lhs, bbr_ref[c])
            xs_i[c][...] = _dot(lhs, bbi_ref[c])

        def readout(c):
            yp[c] = (_dot(xs_r[c][...].astype(BF16), ccr_ref[c]) - _dot(xs_i[c][...].astype(BF16), cci_ref[c])
                     + d_ref[:, pl.ds(c * LANES, LANES)] * up[c])

        drive(0)
        for c in range(N_CB):
            if c + 1 < N_CB:
                drive(c + 1)
            if c >= 1 and not finals_only:
                readout(c - 1)
            _scan_block(abr_ref, abi_ref, c, xs_r[c], xs_i[c], 0, x_r, x_i, False)
        if finals_only:
            @pl.when(k == nch - 1)
            def _():
                xfr_ref[...] = x_r[...]
                xfi_ref[...] = x_i[...]
        else:
            readout(N_CB - 1)
            _permute_out(yp, y_ref)

    ublk = pl.BlockSpec((SSM_LK, N_SEG, SSM_W), lambda k: (k, 0, 0))
    st = pl.BlockSpec((1, N_SEG, N_STATE), lambda k: (k, 0, 0))
    vec = _full((1, N_STATE))
    mat = _full((N_SEG, N_STATE))
    chunk = pltpu.VMEM((N_CB, ROWS, LANES), F32)
    blocks = [pltpu.VMEM((ROWS, CB_STATES), F32)] * (2 * N_CB)
    small = pltpu.VMEM((N_SEG, N_STATE), F32)
    if finals_only:
        out_specs, out_shape = [mat, mat], [_sds((N_SEG, N_STATE))] * 2
        scratch, name = [chunk, small, small] + blocks, "ssm_fwd_finals"
    else:
        out_specs = [ublk, st, st]
        out_shape = [_sds(u3.shape)] + [_sds((nch, N_SEG, N_STATE))] * 2
        scratch, name = [chunk, chunk, small, small] + blocks, "ssm_fwd"
    return pl.pallas_call(
        body, name=name, grid=(nch,),
        in_specs=[ublk, vec, vec, vec, vec, mat, mat,
                  _full((N_CB, LANES, CB_STATES)), _full((N_CB, LANES, CB_STATES)),
                  _full((N_CB, CB_STATES, LANES)), _full((N_CB, CB_STATES, LANES)), _full((1, SSM_W))],
        out_specs=out_specs, out_shape=out_shape, scratch_shapes=scratch,
        compiler_params=_cp(("arbitrary",)),
    )(u3, abr, abi, pw_r, pw_i, fin_r, fin_i, bb_r, bb_i, cc_r, cc_i, dskip)


def _ssm_bwd(u3, dy3, xst_r, xst_i, abr, abi, pw_r, pw_i, fin_r, fin_i, bb_r, bb_i, bbt_r, bbt_i,
             cct_r, cct_i, dskip, finals_only):
    sl = u3.shape[0]
    nch = sl // SSM_LK

    def body(u_ref, g_ref, xsr_ref, xsi_ref, abr_ref, abi_ref, pwr_ref, pwi_ref,
             finr_ref, fini_ref, bbr_ref, bbi_ref, btr_ref, bti_ref, ctr_ref, cti_ref, d_ref, *rest):
        if finals_only:
            lfr_ref, lfi_ref, gp, lam_r, lam_i = rest[:5]
            l_r, l_i = rest[-2 * N_CB:-N_CB], rest[-N_CB:]
        else:
            (du_ref, dar_ref, dai_ref, dbr_ref, dbi_ref, dcr_ref, dci_ref, dd_ref,
             gp, up, yp, lam_r, lam_i, x_r, x_i, sar, sai, sdd) = rest[:18]
            l_r, l_i = rest[18:18 + N_CB], rest[18 + N_CB:18 + 2 * N_CB]
            xx_r, xx_i = rest[18 + 2 * N_CB:18 + 3 * N_CB], rest[18 + 3 * N_CB:]
        t = pl.program_id(0)

        @pl.when(t == 0)
        def _():
            _seg_init(finr_ref, fini_ref, pwr_ref, pwi_ref, lam_r, lam_i, True)
            if not finals_only:
                sar[...] = jnp.zeros_like(sar)
                sai[...] = jnp.zeros_like(sai)
                sdd[...] = jnp.zeros_like(sdd)
                dbr_ref[...] = jnp.zeros_like(dbr_ref)
                dbi_ref[...] = jnp.zeros_like(dbi_ref)
                dcr_ref[...] = jnp.zeros_like(dcr_ref)
                dci_ref[...] = jnp.zeros_like(dci_ref)

        _permute_in(g_ref, gp)
        if not finals_only:
            _permute_in(u_ref, up)
            x_r[...] = xsr_ref[0]
            x_i[...] = xsi_ref[0]

        def drive(c):
            lhs = gp[c].astype(BF16)
            l_r[c][...] = _dot(lhs, ctr_ref[c])
            l_i[c][...] = -_dot(lhs, cti_ref[c])
            if not finals_only:
                cols = pl.ds(c * CB_STATES, CB_STATES)
                xx_r[c][pl.ds(0, N_SEG), :] = x_r[:, cols]
                xx_i[c][pl.ds(0, N_SEG), :] = x_i[:, cols]
                ub = up[c].astype(BF16)
                xx_r[c][pl.ds(N_SEG, ROWS), :] = _dot(ub, bbr_ref[c])
                xx_i[c][pl.ds(N_SEG, ROWS), :] = _dot(ub, bbi_ref[c])

        def collect(c):
            lrb = l_r[c][...].astype(BF16)
            lib = l_i[c][...].astype(BF16)
            ub = up[c].astype(BF16)
            gb = gp[c].astype(BF16)
            dbr_ref[c] += _dot_tn(lrb, ub)
            dbi_ref[c] += _dot_tn(lib, ub)
            dcr_ref[c] += _dot_tn(gb, xx_r[c][pl.ds(N_SEG, ROWS), :].astype(BF16))
            dci_ref[c] += -_dot_tn(gb, xx_i[c][pl.ds(N_SEG, ROWS), :].astype(BF16))
            yp[c] = _dot(lrb, btr_ref[c]) + _dot(lib, bti_ref[c]) + d_ref[:, pl.ds(c * LANES, LANES)] * gp[c]
            prod = gp[c] * up[c]
            sdd[:, pl.ds(c * LANES, LANES)] += jnp.sum(prod.reshape(SSM_LK, N_SEG, LANES), axis=0)

        drive(0)
        for c in range(N_CB):
            if c + 1 < N_CB:
                drive(c + 1)
            if finals_only:
                _scan_block(abr_ref, _Neg(abi_ref), c, l_r[c], l_i[c], 0, lam_r, lam_i, True)
            else:
                if c >= 1:
                    collect(c - 1)
                _scan_block(abr_ref, abi_ref, c, xx_r[c], xx_i[c], N_SEG, x_r, x_i, False)
                _scan_block(abr_ref, _Neg(abi_ref), c, l_r[c], l_i[c], 0, lam_r, lam_i, True,
                            acc=(xx_r[c], xx_i[c], sar, sai))
        if finals_only:
            @pl.when(t == nch - 1)
            def _():
                lfr_ref[...] = lam_r[...]
                lfi_ref[...] = lam_i[...]
        else:
            collect(N_CB - 1)
            _permute_out(yp, du_ref)

            @pl.when(t == nch - 1)
            def _():
                dar_ref[...] = jnp.sum(sar[...], axis=0, keepdims=True)
                dai_ref[...] = jnp.sum(sai[...], axis=0, keepdims=True)
                dd_ref[...] = jnp.sum(sdd[...], axis=0, keepdims=True)

    ublk = pl.BlockSpec((SSM_LK, N_SEG, SSM_W), lambda t: (nch - 1 - t, 0, 0))
    st = pl.BlockSpec((1, N_SEG, N_STATE), lambda t: (nch - 1 - t, 0, 0))
    vec = _full((1, N_STATE))
    mat = _full((N_SEG, N_STATE))
    cs = _full((N_CB, LANES, CB_STATES))
    sc = _full((N_CB, CB_STATES, LANES))
    in_specs = [ublk, ublk, st, st, vec, vec, vec, vec, mat, mat, cs, cs, sc, sc, cs, cs, _full((1, SSM_W))]
    chunk = pltpu.VMEM((N_CB, ROWS, LANES), F32)
    blocks = [pltpu.VMEM((ROWS, CB_STATES), F32)] * (2 * N_CB)
    small = pltpu.VMEM((N_SEG, N_STATE), F32)
    if finals_only:
        out_specs, out_shape = [mat, mat], [_sds((N_SEG, N_STATE))] * 2
        scratch, name = [chunk, small, small] + blocks, "ssm_bwd_finals"
    else:
        out_specs = [ublk, vec, vec, sc, sc, cs, cs, _full((1, SSM_W))]
        out_shape = ([_sds(u3.shape), _sds((1, N_STATE)), _sds((1, N_STATE))]
                     + [_sds((N_CB, CB_STATES, LANES))] * 2 + [_sds((N_CB, LANES, CB_STATES))] * 2
                     + [_sds((1, SSM_W))])
        scratch = ([chunk, chunk, chunk, small, small, small, small, small, small, pltpu.VMEM((N_SEG, SSM_W), F32)]
                   + blocks + [pltpu.VMEM((ROWS + N_SEG, CB_STATES), F32)] * (2 * N_CB))
        name = "ssm_bwd"
    return pl.pallas_call(
        body, name=name, grid=(nch,), in_specs=in_specs, out_specs=out_specs, out_shape=out_shape,
        scratch_shapes=scratch, compiler_params=_cp(("arbitrary",)),
    )(u3, dy3, xst_r, xst_i, abr, abi, pw_r, pw_i, fin_r, fin_i, bb_r, bb_i, bbt_r, bbt_i, cct_r, cct_i, dskip)


def _row(tm, w):
    return pl.BlockSpec((tm, w), lambda i: (i, 0))


def _acc_rows(ref, rows, first):
    @pl.when(first)
    def _():
        ref[...] = jnp.zeros_like(ref)

    ref[...] += jnp.sum(rows, axis=0, keepdims=True)


def _fwd_mix(attn, y, x, glu_w, glu_b, ga, gs, w_out, g2, tm=1024):
    s = x.shape[0]

    def body(a_ref, y_ref, x_ref, gw_ref, gb_ref, ga_ref, gs_ref, wo_ref, g2_ref, mix_ref, x2_ref, h_ref):
        a = a_ref[...]
        anb = ((a * _rms(a)) * ga_ref[...]).astype(BF16)
        z, _ = _gelu(y_ref[...])
        so = z * jax.nn.sigmoid(_dot(z.astype(BF16), gw_ref[...]) + gb_ref[...])
        snb = ((so * _rms(so)) * gs_ref[...]).astype(BF16)
        mix_ref[:, pl.ds(0, ATTN_W)] = anb
        mix_ref[:, pl.ds(ATTN_W, SSM_W)] = snb
        x2 = x_ref[...] + (_dot(anb, wo_ref[pl.ds(0, ATTN_W), :]) + _dot(snb, wo_ref[pl.ds(ATTN_W, SSM_W), :]))
        x2_ref[...] = x2
        h_ref[...] = ((x2 * _rms(x2)) * g2_ref[...]).astype(BF16)

    return pl.pallas_call(
        body, name="fwd_mix", grid=(s // tm,),
        in_specs=[_row(tm, ATTN_W), _row(tm, SSM_W), _row(tm, D_MODEL), _full((SSM_W, SSM_W)), _full((1, SSM_W)),
                  _full((1, ATTN_W)), _full((1, SSM_W)), _full((D_MODEL, D_MODEL)), _full((1, D_MODEL))],
        out_specs=[_row(tm, D_MODEL), _row(tm, D_MODEL), _row(tm, D_MODEL)],
        out_shape=[_sds((s, D_MODEL), BF16), _sds((s, D_MODEL)), _sds((s, D_MODEL), BF16)],
        compiler_params=_cp(("parallel",)),
    )(attn, y, x, glu_w, glu_b, ga, gs, w_out, g2)


def _mlp_up(h, w_up, tm=512, bn=1024):
    s = h.shape[0]

    def body(h_ref, w_ref, r_ref, hdn_ref):
        hv = h_ref[...]
        for j in range(D_FF // bn):
            cols = pl.ds(j * bn, bn)
            r = jnp.maximum(_dot(hv, w_ref[:, cols]), 0.0)
            r_ref[:, cols] = r.astype(BF16)
            hdn_ref[:, cols] = (r * r).astype(BF16)

    return pl.pallas_call(
        body, name="mlp_up", grid=(s // tm,),
        in_specs=[_row(tm, D_MODEL), _full((D_MODEL, D_FF))],
        out_specs=[_row(tm, D_FF), _row(tm, D_FF)], out_shape=[_sds((s, D_FF), BF16)] * 2,
        compiler_params=_cp(("parallel",)),
    )(h, w_up)


def _mlp_down_loss(hdn, w_down, x2, tgt, tm=512):
    s = x2.shape[0]

    def body(hdn_ref, w_ref, x2_ref, t_ref, dy_ref, dyb_ref, sse_ref):
        err = (x2_ref[...] + _dot(hdn_ref[...], w_ref[...])) - t_ref[...]
        dy = err * (1.0 / D_MODEL)
        dy_ref[...] = dy
        dyb_ref[...] = dy.astype(BF16)

        @pl.when(pl.program_id(0) == 0)
        def _():
            sse_ref[...] = jnp.zeros_like(sse_ref)

        sse_ref[...] += jnp.sum(jnp.sum(err * err, axis=0, keepdims=True), axis=1, keepdims=True)

    return pl.pallas_call(
        body, name="mlp_down_loss", grid=(s // tm,),
        in_specs=[_row(tm, D_FF), _full((D_FF, D_MODEL)), _row(tm, D_MODEL), _row(tm, D_MODEL)],
        out_specs=[_row(tm, D_MODEL), _row(tm, D_MODEL), _full((1, 1))],
        out_shape=[_sds((s, D_MODEL)), _sds((s, D_MODEL), BF16), _sds((1, 1))],
        compiler_params=_cp(("arbitrary",)),
    )(hdn, w_down, x2, tgt)


def _mlp_down_bwd(dyb, w_down_t, r, tm=512, bn=1024):
    s = dyb.shape[0]

    def body(dy_ref, w_ref, r_ref, dup_ref):
        dyv = dy_ref[...]
        for j in range(D_FF // bn):
            cols = pl.ds(j * bn, bn)
            dup_ref[:, cols] = (_dot(dyv, w_ref[:, cols]) * (2.0 * r_ref[:, cols].astype(F32))).astype(BF16)

    return pl.pallas_call(
        body, name="mlp_down_bwd", grid=(s // tm,),
        in_specs=[_row(tm, D_MODEL), _full((D_MODEL, D_FF)), _row(tm, D_FF)],
        out_specs=_row(tm, D_FF), out_shape=_sds((s, D_FF), BF16),
        compiler_params=_cp(("parallel",)),
    )(dyb, w_down_t, r)


def _mlp_up_bwd(dup, w_up_t, x2, g2, dy, tm=512):
    s = x2.shape[0]

    def body(dup_ref, w_ref, x2_ref, g2_ref, dy_ref, dx2_ref, dx2b_ref, dg_ref):
        dx, dg_rows = _rms_bwd(_dot(dup_ref[...], w_ref[...]), x2_ref[...], g2_ref[...], D_MODEL)
        dx2 = dy_ref[...] + dx
        dx2_ref[...] = dx2
        dx2b_ref[...] = dx2.astype(BF16)
        _acc_rows(dg_ref, dg_rows, pl.program_id(0) == 0)

    return pl.pallas_call(
        body, name="mlp_up_bwd", grid=(s // tm,),
        in_specs=[_row(tm, D_FF), _full((D_FF, D_MODEL)), _row(tm, D_MODEL), _full((1, D_MODEL)), _row(tm, D_MODEL)],
        out_specs=[_row(tm, D_MODEL), _row(tm, D_MODEL), _full((1, D_MODEL))],
        out_shape=[_sds((s, D_MODEL)), _sds((s, D_MODEL), BF16), _sds((1, D_MODEL))],
        compiler_params=_cp(("arbitrary",)),
    )(dup, w_up_t, x2, g2, dy)


def _mix_bwd(dx2b, w_out_t, attn, y, glu_w, glu_b, glu_w_t, ga, gs, tm=1024):
    s = attn.shape[0]

    def body(dx2_ref, wot_ref, a_ref, y_ref, gw_ref, gb_ref, gwt_ref, ga_ref, gs_ref,
             da_ref, dys_ref, z_ref, dpre_ref, dga_ref, dgs_ref, dgb_ref):
        first = pl.program_id(0) == 0
        dmix = _dot(dx2_ref[...], wot_ref[...])
        da, dga_rows = _rms_bwd(dmix[:, :ATTN_W], a_ref[...], ga_ref[...], ATTN_W)
        da_ref[...] = da
        yv = y_ref[...]
        z, t = _gelu(yv)
        gate = jax.nn.sigmoid(_dot(z.astype(BF16), gw_ref[...]) + gb_ref[...])
        dso, dgs_rows = _rms_bwd(dmix[:, ATTN_W:], z * gate, gs_ref[...], SSM_W)
        dpre = dso * z * gate * (1.0 - gate)
        dpre_b = dpre.astype(BF16)
        dz = dso * gate + _dot(dpre_b, gwt_ref[...])
        dgelu = 0.5 * (1.0 + t) + 0.5 * yv * (1.0 - t * t) * (GELU_C * (1.0 + 3.0 * 0.044715 * (yv * yv)))
        dys_ref[...] = dz * dgelu
        z_ref[...] = z.astype(BF16)
        dpre_ref[...] = dpre_b
        _acc_rows(dga_ref, dga_rows, first)
        _acc_rows(dgs_ref, dgs_rows, first)
        _acc_rows(dgb_ref, dpre, first)

    vec = _full((1, SSM_W))
    return pl.pallas_call(
        body, name="mix_bwd", grid=(s // tm,),
        in_specs=[_row(tm, D_MODEL), _full((D_MODEL, D_MODEL)), _row(tm, ATTN_W), _row(tm, SSM_W),
                  _full((SSM_W, SSM_W)), vec, _full((SSM_W, SSM_W)), vec, vec],
        out_specs=[_row(tm, ATTN_W), _row(tm, SSM_W), _row(tm, SSM_W), _row(tm, SSM_W), vec, vec, vec],
        out_shape=[_sds((s, ATTN_W)), _sds((s, SSM_W)), _sds((s, SSM_W), BF16), _sds((s, SSM_W), BF16),
                   _sds((1, ATTN_W)), _sds((1, SSM_W)), _sds((1, SSM_W))],
        compiler_params=_cp(("arbitrary",)),
    )(dx2b, w_out_t, attn, y, glu_w, glu_b, glu_w_t, ga, gs)


def _qk_bwd(dqn, dkn, qk, dv, du, gq, gk, gmat, tm=1024):
    s = qk.shape[0]

    def body(dq_ref, dk_ref, qk_ref, dv_ref, du_ref, gq_ref, gk_ref, gm_ref, dp_ref, dgq_ref, dgk_ref):
        first = pl.program_id(0) == 0
        gm = gm_ref[...]
        for idx, (d_ref, g_ref, dg_ref) in enumerate(((dq_ref, gq_ref, dgq_ref), (dk_ref, gk_ref, dgk_ref))):
            xv = qk_ref[:, pl.ds(idx * ATTN_W, ATTN_W)]
            dyv = d_ref[...]
            r = lax.rsqrt(_group_mean(xv * xv, gm) + EPS)
            gdy = dyv * g_ref[...]
            dx = r * gdy - xv * (r * r * r) * _group_mean(gdy * xv, gm)
            dp_ref[:, pl.ds(idx * ATTN_W, ATTN_W)] = dx.astype(BF16)
            _acc_rows(dg_ref, dyv * (xv * r), first)
        dp_ref[:, pl.ds(2 * ATTN_W, ATTN_W)] = dv_ref[...].astype(BF16)
        dp_ref[:, pl.ds(3 * ATTN_W, SSM_W)] = du_ref[...].astype(BF16)

    vec = _full((1, ATTN_W))
    return pl.pallas_call(
        body, name="qk_bwd", grid=(s // tm,),
        in_specs=[_row(tm, ATTN_W), _row(tm, ATTN_W), _row(tm, 2 * ATTN_W), _row(tm, ATTN_W), _row(tm, SSM_W),
                  vec, vec, _full((MXU_W, MXU_W))],
        out_specs=[_row(tm, 4 * ATTN_W), vec, vec],
        out_shape=[_sds((s, 4 * ATTN_W), BF16), _sds((1, ATTN_W)), _sds((1, ATTN_W))],
        compiler_params=_cp(("arbitrary",)),
    )(dqn, dkn, qk, dv, du, gq, gk, gmat)


def _in_bwd(dproj, w_in_t, x, g1, dx2, ex=None, tm=1024):
    s = x.shape[0]
    steps = s // tm

    def body(*refs):
        (dp_ref, w_ref, x_ref, g1_ref, dx2_ref), (gx_ref, dg_ref), _, hx = _carry(ex, refs, 5, 2)
        if ex is not None:
            @pl.when(pl.program_id(0) == 0)
            def _():
                ex.start(*hx)

        dx, dg_rows = _rms_bwd(_dot(dp_ref[...], w_ref[...]), x_ref[...], g1_ref[...], D_MODEL)
        gx_ref[...] = dx2_ref[...] + dx
        _acc_rows(dg_ref, dg_rows, pl.program_id(0) == 0)
        if ex is not None:
            @pl.when(pl.program_id(0) == steps - 1)
            def _():
                ex.wait(*hx)

    hosted = ex is not None
    return pl.pallas_call(
        body, name="in_bwd", grid=(steps,),
        in_specs=[_row(tm, 4 * ATTN_W), _full((4 * ATTN_W, D_MODEL)), _row(tm, D_MODEL), _full((1, D_MODEL)),
                  _row(tm, D_MODEL)] + (ex.specs() if hosted else []),
        out_specs=[_row(tm, D_MODEL), _full((1, D_MODEL))] + (ex.specs() if hosted else []),
        out_shape=[_sds((s, D_MODEL)), _sds((1, D_MODEL))] + (ex.out_shape() if hosted else []),
        scratch_shapes=ex.scratch() if hosted else [],
        compiler_params=_cp(("arbitrary",)),
    )(dproj, w_in_t, x, g1, dx2, *(ex.srcs if hosted else []))


def _mm_tn(a, b, name, ts=2048):
    s, k = a.shape
    n = b.shape[1]
    bk, bn = min(k, 1024), min(n, 1024)
    steps = s // ts

    def body(a_ref, b_ref, o_ref, acc):
        t = pl.program_id(2)

        @pl.when(t == 0)
        def _():
            acc[...] = jnp.zeros_like(acc)

        acc[...] += _dot_tn(a_ref[...], b_ref[...])

        @pl.when(t == steps - 1)
        def _():
            o_ref[...] = acc[...].astype(BF16)

    return pl.pallas_call(
        body, name=name, grid=(k // bk, n // bn, steps),
        in_specs=[pl.BlockSpec((ts, bk), lambda i, j, t: (t, i)), pl.BlockSpec((ts, bn), lambda i, j, t: (t, j))],
        out_specs=pl.BlockSpec((bk, bn), lambda i, j, t: (i, j)), out_shape=_sds((k, n), BF16),
        scratch_shapes=[pltpu.VMEM((bk, bn), F32)],
        compiler_params=_cp(("parallel", "parallel", "arbitrary")),
    )(a, b)


def _peer(k):
    x, y, c = lax.axis_index("x"), lax.axis_index("y"), lax.axis_index("c")
    px = 1 - x if k & 4 else x
    py = 1 - y if k & 2 else y
    pc = 1 - c if k & 1 else c
    return (px, py, pc), 4 * px + 2 * py + pc


def _gather_rows(x_shard):
    m_per, n = x_shard.shape

    def body(x_ref, out_ref, send_sems, recv_sems, local_sem):
        x, y, c = lax.axis_index("x"), lax.axis_index("y"), lax.axis_index("c")
        me, sibling = (x, y, c), (x, y, 1 - c)
        chips = [(1 - x, y), (x, 1 - y), (1 - x, 1 - y)]

        def rows(px, py, pc):
            return out_ref.at[pl.ds((4 * px + 2 * py + pc) * m_per, m_per), :]

        def copy(k, block, to, src=None):
            return pltpu.make_async_remote_copy(
                src_ref=rows(*block) if src is None else src, dst_ref=rows(*block),
                send_sem=send_sems.at[k], recv_sem=recv_sems.at[k], device_id=to, device_id_type=MESH)

        mine = pltpu.make_async_copy(x_ref, rows(*me), local_sem)
        mine.start()
        first = [copy(0, me, sibling, src=x_ref)]
        first += [copy(1 + j, me, (*chip, c), src=x_ref) for j, chip in enumerate(chips)]
        for cp in first:
            cp.start()
        passed = [copy(4 + j, (*chip, c), sibling) for j, chip in enumerate(chips)]
        for j, chip in enumerate(chips):
            copy(1 + j, (*chip, c), me).wait_recv()
            passed[j].start()
        copy(0, sibling, me).wait_recv()
        for j, chip in enumerate(chips):
            copy(4 + j, (*chip, 1 - c), me).wait_recv()
        for cp in first + passed:
            cp.wait_send()
        mine.wait()

    return pl.pallas_call(
        body, name="gather_weights", out_shape=_sds((N_DEV * m_per, n), x_shard.dtype),
        in_specs=[pl.BlockSpec(memory_space=pltpu.VMEM)], out_specs=pl.BlockSpec(memory_space=pltpu.VMEM),
        scratch_shapes=[pltpu.SemaphoreType.DMA((7,)), pltpu.SemaphoreType.DMA((7,)), pltpu.SemaphoreType.DMA],
        compiler_params=pltpu.CompilerParams(vmem_limit_bytes=VMEM_LIMIT),
    )(x_shard)


class _Exchange:
    def __init__(self, srcs, whole):
        self.srcs, self.whole, self.n = list(srcs), list(whole), len(srcs)
        self.rows = [a.shape[0] // N_DEV if w is False else a.shape[0] for a, w in zip(self.srcs, self.whole)]
        self.cols = [a.shape[1] // N_DEV if w == "col" else a.shape[1] for a, w in zip(self.srcs, self.whole)]

    def specs(self):
        return [pl.BlockSpec(memory_space=pl.ANY)] * self.n

    def out_shape(self):
        return [_sds((N_DEV, r, c), a.dtype) for r, c, a in zip(self.rows, self.cols, self.srcs)]

    def scratch(self):
        return [pltpu.SemaphoreType.DMA((self.n * 7,)), pltpu.SemaphoreType.DMA((self.n * 7,)),
                pltpu.SemaphoreType.DMA((self.n,))]

    def _copies(self, ins, outs, sems):
        send_sems, recv_sems, local_sems = sems
        _, me = _peer(0)
        for w in range(self.n):
            for k in range(N_DEV):
                peer, pidx = _peer(k)
                if self.whole[w] is True:
                    src = ins[w]
                elif self.whole[w] == "col":
                    src = ins[w].at[:, pl.ds(pl.multiple_of(pidx * self.cols[w], self.cols[w]), self.cols[w])]
                else:
                    src = ins[w].at[pl.ds(pidx * self.rows[w], self.rows[w]), :]
                if k == 0:
                    yield k, pltpu.make_async_copy(src, outs[w].at[me], local_sems.at[w]), None
                else:
                    sem = w * 7 + k - 1
                    out = pltpu.make_async_remote_copy(src_ref=src, dst_ref=outs[w].at[me], send_sem=send_sems.at[sem],
                                                       recv_sem=recv_sems.at[sem], device_id=peer, device_id_type=MESH)
                    back = pltpu.make_async_remote_copy(src_ref=src, dst_ref=outs[w].at[pidx], send_sem=send_sems.at[sem],
                                                        recv_sem=recv_sems.at[sem], device_id=peer, device_id_type=MESH)
                    yield k, out, back

    def start(self, ins, outs, sems):
        for _, out, _ in self._copies(ins, outs, sems):
            out.start()

    def wait(self, ins, outs, sems):
        for k, out, back in self._copies(ins, outs, sems):
            if k == 0:
                out.wait()
            else:
                back.wait_recv()
                out.wait_send()


def _carry(ex, refs, n_in, n_out):
    nh = ex.n if ex is not None else 0
    ins, hin = refs[:n_in], refs[n_in:n_in + nh]
    outs = refs[n_in + nh:n_in + nh + n_out]
    hout = refs[n_in + nh + n_out:n_in + 2 * nh + n_out]
    rest = refs[n_in + 2 * nh + n_out:]
    if ex is None:
        return ins, outs, rest, None
    return ins, outs, rest[:-3], (hin, hout, rest[-3:])


def _exchange_now(srcs, whole, name):
    ex = _Exchange(srcs, whole)

    def body(*refs):
        _, _, _, (hin, hout, sems) = _carry(ex, refs, 0, 0)
        ex.start(hin, hout, sems)
        ex.wait(hin, hout, sems)

    return pl.pallas_call(body, name=name, out_shape=ex.out_shape(), in_specs=ex.specs(), out_specs=ex.specs(),
                          scratch_shapes=ex.scratch())(*srcs)


def _split_exchange_start(src):
    rows, width = src.shape[0], src.shape[1] // N_DEV

    def body(src_ref, land_ref, send_sems, recv_sems, src_thru, land_thru, token):
        barrier = pltpu.get_barrier_semaphore()
        for k in range(1, N_DEV):
            pl.semaphore_signal(barrier, inc=1, device_id=_peer(k)[0], device_id_type=MESH)
        pl.semaphore_wait(barrier, N_DEV - 1)
        _, me = _peer(0)
        for k in range(1, N_DEV):
            peer, pidx = _peer(k)
            pltpu.make_async_remote_copy(
                src_ref=src_ref.at[:, pl.ds(pl.multiple_of(pidx * width, width), width)], dst_ref=land_ref.at[me],
                send_sem=send_sems.at[k - 1], recv_sem=recv_sems.at[k - 1], device_id=peer, device_id_type=MESH).start()
        token[...] = jnp.zeros_like(token)

    hbm = pl.BlockSpec(memory_space=pltpu.HBM)
    sem = pl.BlockSpec(memory_space=pltpu.SEMAPHORE)
    land = lax.empty((N_DEV, rows, width), src.dtype)
    return pl.pallas_call(
        body, name="w_in_exchange_start",
        out_shape=(pltpu.SemaphoreType.DMA((N_DEV - 1,)), pltpu.SemaphoreType.DMA((N_DEV - 1,)),
                   pltpu.HBM(src.shape, src.dtype), pltpu.HBM(land.shape, land.dtype), _sds((8, LANES))),
        in_specs=(hbm, hbm), out_specs=(sem, sem, hbm, hbm, pl.BlockSpec(memory_space=pltpu.VMEM)),
        input_output_aliases={0: 2, 1: 3},
        compiler_params=pltpu.CompilerParams(has_side_effects=pltpu.SideEffectType.DATAFLOW_SIDE_EFFECTING, collective_id=0),
    )(pltpu.with_memory_space_constraint(src, pltpu.HBM), pltpu.with_memory_space_constraint(land, pltpu.HBM))


def _split_exchange_wait(send_sems, recv_sems, src_thru, land_thru, after):
    width = land_thru.shape[2]

    def body(src_ref, land_ref, send_sems, recv_sems, after_ref, src_dead, got_ref):
        _, me = _peer(0)
        for k in range(1, N_DEV):
            peer, pidx = _peer(k)
            copy = pltpu.make_async_remote_copy(
                src_ref=src_ref.at[:, pl.ds(pl.multiple_of(pidx * width, width), width)], dst_ref=land_ref.at[pidx],
                send_sem=send_sems.at[k - 1], recv_sem=recv_sems.at[k - 1], device_id=peer, device_id_type=MESH)
            copy.wait_send()
            copy.wait_recv()

    hbm = pl.BlockSpec(memory_space=pltpu.HBM)
    sem = pl.BlockSpec(memory_space=pltpu.SEMAPHORE)
    return pl.pallas_call(
        body, name="w_in_exchange_wait",
        out_shape=(pltpu.HBM(src_thru.shape, src_thru.dtype), pltpu.HBM(land_thru.shape, land_thru.dtype)),
        in_specs=(hbm, hbm, sem, sem, pl.BlockSpec(memory_space=pl.ANY)), out_specs=(hbm, hbm),
        input_output_aliases={0: 0, 1: 1},
        compiler_params=pltpu.CompilerParams(has_side_effects=pltpu.SideEffectType.DATAFLOW_SIDE_EFFECTING),
    )(src_thru, land_thru, send_sems, recv_sems, after)


def _adamw(w, m, v, gparts, name):
    r, c = w.shape
    tr = r if r * c <= 256 * 1024 else 128 * 1024 // c

    def body(w_ref, m_ref, v_ref, g_ref, go_ref, d_ref, mo_ref, vo_ref):
        g = g_ref[0].astype(F32)
        for i in range(1, N_DEV):
            g = g + g_ref[i].astype(F32)
        go_ref[...] = g
        d_ref[...], mo_ref[...], vo_ref[...] = _adamw_step(w_ref[...], m_ref[...], v_ref[...], g)

    blk = pl.BlockSpec((tr, c), lambda i: (i, 0))
    return pl.pallas_call(
        body, name=name, grid=(r // tr,),
        in_specs=[blk, blk, blk, pl.BlockSpec((N_DEV, tr, c), lambda i: (0, i, 0))],
        out_specs=[blk] * 4, out_shape=[_sds((r, c))] * 4,
        compiler_params=_cp(("parallel",)),
    )(w, m, v, gparts)


def _adamw_step(w, m, v, g):
    nm = ADAM_B1 * m + (1.0 - ADAM_B1) * g
    nv = ADAM_B2 * v + (1.0 - ADAM_B2) * (g * g)
    m_hat = nm / (1.0 - ADAM_B1 ** ADAM_STEP)
    v_hat = nv / (1.0 - ADAM_B2 ** ADAM_STEP)
    return -ADAM_LR * (m_hat / (jnp.sqrt(v_hat) + ADAM_EPS) + ADAM_WD * w), nm, nv


def _sum_slots(parts):
    def body(p_ref, o_ref):
        g = p_ref[0]
        for i in range(1, N_DEV):
            g = g + p_ref[i]
        o_ref[...] = g

    return pl.pallas_call(body, name="sum_small_grads", out_shape=_sds(parts.shape[1:]))(parts)


def _adamw_native(ws, ms, vs, gs):
    n = len(ws)

    def body(*refs):
        for i in range(n):
            w_ref, m_ref, v_ref, g_ref = refs[i], refs[n + i], refs[2 * n + i], refs[3 * n + i]
            d, nm, nv = _adamw_step(w_ref[...], m_ref[...], v_ref[...], g_ref[...])
            refs[4 * n + i][...] = d
            refs[5 * n + i][...] = nm
            refs[6 * n + i][...] = nv

    outs = pl.pallas_call(body, name="adamw_small", out_shape=[_sds(w.shape) for w in ws] * 3,
                          compiler_params=pltpu.CompilerParams(vmem_limit_bytes=VMEM_LIMIT))(*ws, *ms, *vs, *gs)
    return outs[:n], outs[n:2 * n], outs[2 * n:]


def _block_diag(a, states_first):
    a4 = a.reshape(N_CB, 8, SSM_GROUP, SSM_STATE)
    eye = jnp.eye(8, dtype=a.dtype)
    if states_first:
        return jnp.einsum("bgcp,gh->bgphc", a4, eye).reshape(N_CB, CB_STATES, LANES)
    return jnp.einsum("bgcp,gh->bgchp", a4, eye).reshape(N_CB, LANES, CB_STATES)


def _block_diag_of(full, states_first):
    if states_first:
        picked = jnp.einsum("bgphc,gh->bgcp", full.reshape(N_CB, 8, SSM_STATE, 8, SSM_GROUP), jnp.eye(8, dtype=full.dtype))
    else:
        picked = jnp.einsum("bgchp,gh->bgcp", full.reshape(N_CB, 8, SSM_GROUP, 8, SSM_STATE), jnp.eye(8, dtype=full.dtype))
    return picked.reshape(SSM_GROUPS, SSM_GROUP, SSM_STATE)


SMALL_EARLY = ("ssm_a_re", "ssm_a_im", "ssm_log_dt", "ssm_b_re", "ssm_b_im", "ssm_c_re", "ssm_c_im", "ssm_d", "glu_b",
               "attn_out_norm_g", "ssm_out_norm_g", "norm2_g")
SMALL_MID = ("q_norm_g", "k_norm_g")
SMALL_LATE = ("norm1_g",)
SMALL = SMALL_EARLY + SMALL_MID + SMALL_LATE


def _pack_small(arrs):
    parts = []
    for a in arrs:
        flat = a.reshape(-1)
        rows = -(-flat.shape[0] // (8 * LANES)) * 8
        parts.append(jnp.pad(flat, (0, rows * LANES - flat.shape[0])).reshape(rows, LANES))
    return jnp.concatenate(parts, axis=0)


def _unpack_small(packed, shapes):
    out, r0 = [], 0
    for shp in shapes:
        size = math.prod(shp)
        rows = -(-size // (8 * LANES)) * 8
        out.append(packed[r0:r0 + rows].reshape(-1)[:size].reshape(shp))
        r0 += rows
    return out


def kernel(x, norm1_g, w_in, q_norm_g, k_norm_g, ssm_a_re, ssm_a_im, ssm_log_dt, ssm_b_re, ssm_b_im, ssm_c_re, ssm_c_im, ssm_d, glu_w, glu_b, attn_out_norm_g, ssm_out_norm_g, w_out, norm2_g, w_mlp_up, w_mlp_down, loss_target, m_norm1_g, m_w_in, m_q_norm_g, m_k_norm_g, m_ssm_a_re, m_ssm_a_im, m_ssm_log_dt, m_ssm_b_re, m_ssm_b_im, m_ssm_c_re, m_ssm_c_im, m_ssm_d, m_glu_w, m_glu_b, m_attn_out_norm_g, m_ssm_out_norm_g, m_w_out, m_norm2_g, m_w_mlp_up, m_w_mlp_down, v_norm1_g, v_w_in, v_q_norm_g, v_k_norm_g, v_ssm_a_re, v_ssm_a_im, v_ssm_log_dt, v_ssm_b_re, v_ssm_b_im, v_ssm_c_re, v_ssm_c_im, v_ssm_d, v_glu_w, v_glu_b, v_attn_out_norm_g, v_ssm_out_norm_g, v_w_out, v_norm2_g, v_w_mlp_up, v_w_mlp_down):
    weights = dict(norm1_g=norm1_g, w_in=w_in, q_norm_g=q_norm_g, k_norm_g=k_norm_g, ssm_a_re=ssm_a_re,
                   ssm_a_im=ssm_a_im, ssm_log_dt=ssm_log_dt, ssm_b_re=ssm_b_re, ssm_b_im=ssm_b_im,
                   ssm_c_re=ssm_c_re, ssm_c_im=ssm_c_im, ssm_d=ssm_d, glu_w=glu_w, glu_b=glu_b,
                   attn_out_norm_g=attn_out_norm_g, ssm_out_norm_g=ssm_out_norm_g, w_out=w_out, norm2_g=norm2_g,
                   w_mlp_up=w_mlp_up, w_mlp_down=w_mlp_down)
    mom_m = dict(norm1_g=m_norm1_g, w_in=m_w_in, q_norm_g=m_q_norm_g, k_norm_g=m_k_norm_g, ssm_a_re=m_ssm_a_re,
                 ssm_a_im=m_ssm_a_im, ssm_log_dt=m_ssm_log_dt, ssm_b_re=m_ssm_b_re, ssm_b_im=m_ssm_b_im,
                 ssm_c_re=m_ssm_c_re, ssm_c_im=m_ssm_c_im, ssm_d=m_ssm_d, glu_w=m_glu_w, glu_b=m_glu_b,
                 attn_out_norm_g=m_attn_out_norm_g, ssm_out_norm_g=m_ssm_out_norm_g, w_out=m_w_out,
                 norm2_g=m_norm2_g, w_mlp_up=m_w_mlp_up, w_mlp_down=m_w_mlp_down)
    mom_v = dict(norm1_g=v_norm1_g, w_in=v_w_in, q_norm_g=v_q_norm_g, k_norm_g=v_k_norm_g, ssm_a_re=v_ssm_a_re,
                 ssm_a_im=v_ssm_a_im, ssm_log_dt=v_ssm_log_dt, ssm_b_re=v_ssm_b_re, ssm_b_im=v_ssm_b_im,
                 ssm_c_re=v_ssm_c_re, ssm_c_im=v_ssm_c_im, ssm_d=v_ssm_d, glu_w=v_glu_w, glu_b=v_glu_b,
                 attn_out_norm_g=v_attn_out_norm_g, ssm_out_norm_g=v_ssm_out_norm_g, w_out=v_w_out,
                 norm2_g=v_norm2_g, w_mlp_up=v_w_mlp_up, w_mlp_down=v_w_mlp_down)
    order = list(weights)

    xs, tgt = x[0], loss_target[0]
    s = xs.shape[0]
    assert s % ATTN_CHUNK == 0 and (s // N_SEG) % SSM_LK == 0
    seg_len = s // N_SEG
    n_sq = seg_len.bit_length() - 1
    assert 1 << n_sq == seg_len

    w_in_t = _gather_rows(w_in[0].T.astype(BF16))
    w_in_full = w_in_t.T
    later = _Exchange([glu_w[0].astype(BF16), w_out[0].astype(BF16), w_mlp_up[0].T.astype(BF16),
                       w_mlp_down[0].astype(BF16)], [True] * 4)

    gq = jnp.tile(q_norm_g[0], ATTN_W // HEAD_DIM)[None]
    gk = jnp.tile(k_norm_g[0], ATTN_W // HEAD_DIM)[None]
    lane = jnp.arange(MXU_W) // HEAD_DIM
    gmat = jnp.where(lane[:, None] == lane[None, :], 1.0 / HEAD_DIM, 0.0).astype(BF16)
    a_re3 = ssm_a_re[0][:, None, :]
    a_im3 = ssm_a_im[0][:, None, :]
    ldt3 = ssm_log_dt[0][:, None, None]
    b_re_t = jnp.swapaxes(ssm_b_re[0], 1, 2)
    b_im_t = jnp.swapaxes(ssm_b_im[0], 1, 2)
    c_re, c_im = ssm_c_re[0], ssm_c_im[0]
    dskip = ssm_d[0].reshape(1, SSM_W)

    qk, qn, kn, v, u, xn = _fwd_proj(xs, norm1_g, w_in_full, gq, gk, gmat)
    attn, lse, glu_g, w_out_g, w_up_g, w_down_g = _attn_fwd(qn, kn, v, later)
    glu_full = glu_g.reshape(SSM_W, SSM_W)
    w_out_full = w_out_g.reshape(D_MODEL, D_MODEL)
    w_up_t = w_up_g.reshape(D_FF, D_MODEL)
    w_down_full = w_down_g.reshape(D_FF, D_MODEL)
    w_up_full, w_out_t, w_down_t, glu_t = w_up_t.T, w_out_full.T, w_down_full.T, glu_full.T

    abr3, abi3, bbr, bbi = _ssm_discretize(a_re3, a_im3, ldt3, b_re_t, b_im_t)
    abr, abi = abr3.reshape(1, N_STATE), abi3.reshape(1, N_STATE)
    pw_r, pw_i = _ssm_power(abr, abi, n_sq)
    bb_r, bb_i = _block_diag(bbr, False).astype(BF16), _block_diag(bbi, False).astype(BF16)
    bbt_r, bbt_i = _block_diag(bbr, True).astype(BF16), _block_diag(bbi, True).astype(BF16)
    cc_r, cc_i = _block_diag(c_re, True).astype(BF16), _block_diag(c_im, True).astype(BF16)
    cct_r, cct_i = _block_diag(c_re, False).astype(BF16), _block_diag(c_im, False).astype(BF16)
    seg_major = lambda a: jnp.swapaxes(a.reshape(N_SEG, seg_len, SSM_W), 0, 1)
    seg_minor = lambda a: jnp.swapaxes(a, 0, 1).reshape(s, SSM_W)
    u3 = seg_major(u)
    zero_fin = jnp.zeros((N_SEG, N_STATE), F32)
    ssm_args = (abr, abi, pw_r, pw_i)
    xf_r, xf_i = _ssm_fwd(u3, *ssm_args, zero_fin, zero_fin, bb_r, bb_i, cc_r, cc_i, dskip, True)
    y3, xst_r, xst_i = _ssm_fwd(u3, *ssm_args, xf_r, xf_i, bb_r, bb_i, cc_r, cc_i, dskip, False)
    y = seg_minor(y3)

    mix, x2, h = _fwd_mix(attn, y, xs, glu_full, glu_b, attn_out_norm_g, ssm_out_norm_g, w_out_full, norm2_g)
    r_act, hdn = _mlp_up(h, w_up_full)
    dy, dyb, sse = _mlp_down_loss(hdn, w_down_full, x2, tgt)

    dup = _mlp_down_bwd(dyb, w_down_t, r_act)
    g_w_down = _mm_tn(hdn, dyb, "grad_w_down")
    dx2, dx2b, g_norm2 = _mlp_up_bwd(dup, w_up_t, x2, norm2_g, dy)
    g_w_up = _mm_tn(h, dup, "grad_w_up")
    dattn, dys, z_b, dpre_b, g_ga, g_gs, g_glu_b = _mix_bwd(dx2b, w_out_t, attn, y, glu_full, glu_b, glu_t,
                                                             attn_out_norm_g, ssm_out_norm_g)
    g_w_out = _mm_tn(mix, dx2b, "grad_w_out")
    g_glu_w = _mm_tn(z_b, dpre_b, "grad_glu_w")
    dy3 = seg_major(dys)
    bwd_args = (u3, dy3, xst_r, xst_i, abr, abi, pw_r, pw_i)
    lf_r, lf_i = _ssm_bwd(*bwd_args, zero_fin, zero_fin, bb_r, bb_i, bbt_r, bbt_i, cct_r, cct_i, dskip, True)
    du3, dab_r, dab_i, dbb_r, dbb_i, dcc_r, dcc_i, g_d = _ssm_bwd(*bwd_args, lf_r, lf_i, bb_r, bb_i, bbt_r, bbt_i,
                                                                 cct_r, cct_i, dskip, False)
    g_a_re3, g_a_im3, g_ldt3, g_b_re_t, g_b_im_t = _ssm_discretize_bwd(
        a_re3, a_im3, ldt3, b_re_t, b_im_t, dab_r.reshape(a_re3.shape), dab_i.reshape(a_re3.shape),
        _block_diag_of(dbb_r, True), _block_diag_of(dbb_i, True))
    g_c_re, g_c_im = _block_diag_of(dcc_r, False), _block_diag_of(dcc_i, False)
    small_grads = dict(
        ssm_a_re=g_a_re3.reshape(ssm_a_re.shape), ssm_a_im=g_a_im3.reshape(ssm_a_im.shape),
        ssm_log_dt=g_ldt3.reshape(ssm_log_dt.shape), ssm_b_re=jnp.swapaxes(g_b_re_t, 1, 2)[None],
        ssm_b_im=jnp.swapaxes(g_b_im_t, 1, 2)[None], ssm_c_re=g_c_re[None], ssm_c_im=g_c_im[None],
        ssm_d=g_d.reshape(ssm_d.shape), glu_b=g_glu_b, attn_out_norm_g=g_ga, ssm_out_norm_g=g_gs, norm2_g=g_norm2)

    early = _Exchange([g_glu_w, g_w_out, g_w_up, g_w_down, _pack_small([small_grads[n] for n in SMALL_EARLY] + [sse])],
                      [False, False, "col", False, True])
    dqn, dkn, dv, p_glu, p_w_out, p_w_up, p_w_down, p_early = _attn_bwd(qn, kn, v, attn, lse, dattn, early)

    dproj, g_gq, g_gk = _qk_bwd(dqn, dkn, qk, dv, seg_minor(du3), gq, gk, gmat)
    g_w_in = _mm_tn(xn, dproj, "grad_w_in")
    small_grads["q_norm_g"] = g_gq.reshape(ATTN_W // HEAD_DIM, HEAD_DIM).sum(0)[None]
    small_grads["k_norm_g"] = g_gk.reshape(ATTN_W // HEAD_DIM, HEAD_DIM).sum(0)[None]
    send_sems, recv_sems, g_thru, land, token = _split_exchange_start(g_w_in)
    grad_x, g_norm1 = _in_bwd(dproj, w_in_t, xs, norm1_g + token[:1, :1], dx2)
    g_w_in, landed = _split_exchange_wait(send_sems, recv_sems, g_thru, land, g_norm1)
    me = 4 * lax.axis_index("x") + 2 * lax.axis_index("y") + lax.axis_index("c")
    shard_w = g_w_in.shape[1] // N_DEV
    own = lax.dynamic_slice(g_w_in, (0, me * shard_w), (g_w_in.shape[0], shard_w))
    p_w_in = lax.dynamic_update_slice(landed, own[None], (me, 0, 0))
    p_mid, p_late = _exchange_now([_pack_small([small_grads[n] for n in SMALL_MID]), _pack_small([g_norm1])],
                                  [True, True], "exchange_tail")

    res = {}
    for name, gp in (("w_in", p_w_in), ("glu_w", p_glu), ("w_out", p_w_out), ("w_mlp_up", p_w_up), ("w_mlp_down", p_w_down)):
        outs = _adamw(weights[name][0], mom_m[name][0], mom_v[name][0], gp, "adamw_" + name)
        res[name] = [o[None] for o in outs]
    shapes = [weights[n].shape for n in SMALL_EARLY] + [sse.shape] + [weights[n].shape for n in SMALL_MID + SMALL_LATE]
    g_small = _unpack_small(_sum_slots(jnp.concatenate([p_early, p_mid, p_late], axis=1)), shapes)
    loss = 0.5 * g_small.pop(len(SMALL_EARLY))[0, 0] / D_MODEL
    d_small, m_small, v_small = _adamw_native([weights[n] for n in SMALL], [mom_m[n] for n in SMALL],
                                              [mom_v[n] for n in SMALL], g_small)
    for i, n in enumerate(SMALL):
        res[n] = [g_small[i], d_small[i], m_small[i], v_small[i]]

    return (loss, grad_x[None], *[res[n][0] for n in order], *[res[n][1] for n in order],
            *[res[n][2] for n in order], *[res[n][3] for n in order])
```

```python
import math

import jax
import jax.numpy as jnp
from jax import lax
from jax.experimental import pallas as pl
from jax.experimental.pallas import tpu as pltpu

F32 = jnp.float32
BF16 = jnp.bfloat16

D_MODEL = 1024
ATTN_W = 512
HEAD_DIM = 64
SSM_W = 512
SSM_GROUP = 16
SSM_GROUPS = 32
SSM_STATE = 64
N_STATE = SSM_GROUPS * SSM_STATE
D_FF = 4096
EPS = 1e-6
NEG_INF = -1e30
ATTN_CHUNK = 2048
ATTN_BLOCK = 128
DILATIONS = (1, 4, 16)
N_SEG = 8
SSM_LK = 64
N_DEV = 8
LANES = 128
MXU_W = 256

ADAM_LR = 0.001
ADAM_B1 = 0.9
ADAM_B2 = 0.999
ADAM_EPS = 1e-08
ADAM_WD = 0.01
ADAM_STEP = 10

VMEM_LIMIT = 56 * 1024 * 1024
GELU_C = math.sqrt(2.0 / math.pi)
MESH = pl.DeviceIdType.MESH


def _cp(sem, vmem=VMEM_LIMIT):
    return pltpu.CompilerParams(dimension_semantics=sem, vmem_limit_bytes=vmem)


def _dot(a, b):
    return jnp.dot(a, b, preferred_element_type=F32)


def _dot_tn(a, b):
    return lax.dot_general(a, b, (((0,), (0,)), ((), ())), preferred_element_type=F32)


def _group_mean(x2, gmat):
    hi = x2.astype(BF16)
    lo = (x2 - hi.astype(F32)).astype(BF16)
    w = gmat.shape[0]
    return jnp.concatenate([_dot(hi[:, c:c + w], gmat) + _dot(lo[:, c:c + w], gmat) for c in range(0, x2.shape[1], w)],
                           axis=1)


def _rms(x):
    return lax.rsqrt(jnp.mean(x * x, axis=-1, keepdims=True) + EPS)


def _rms_bwd(dy, x, g, n):
    r = _rms(x)
    gdy = dy * g
    dx = r * gdy - x * (r * r * r) * (jnp.sum(gdy * x, axis=-1, keepdims=True) / n)
    return dx, dy * (x * r)


def _gelu(y):
    t = jnp.tanh(GELU_C * (y + 0.044715 * (y * y * y)))
    return 0.5 * y * (1.0 + t), t


def _full(shape):
    nd = len(shape)
    return pl.BlockSpec(shape, lambda *_: (0,) * nd)


def _sds(shape, dtype=F32):
    return jax.ShapeDtypeStruct(shape, dtype)


def _fwd_proj(x, g1, w_in, gq, gk, gmat, tm=512):
    s = x.shape[0]

    def body(x_ref, g1_ref, w_ref, gq_ref, gk_ref, gm_ref, qk_ref, qn_ref, kn_ref, v_ref, u_ref, xn_ref):
        xv = x_ref[...]
        xnb = ((xv * _rms(xv)) * g1_ref[...]).astype(BF16)
        xn_ref[...] = xnb
        proj = _dot(xnb, w_ref[...])
        q = proj[:, :ATTN_W]
        k = proj[:, ATTN_W:2 * ATTN_W]
        qk_ref[...] = proj[:, :2 * ATTN_W]
        v_ref[...] = proj[:, 2 * ATTN_W:3 * ATTN_W]
        u_ref[...] = proj[:, 3 * ATTN_W:]
        gm = gm_ref[...]
        qn_ref[...] = (q * lax.rsqrt(_group_mean(q * q, gm) + EPS)) * gq_ref[...]
        kn_ref[...] = (k * lax.rsqrt(_group_mean(k * k, gm) + EPS)) * gk_ref[...]

    row = lambda w: pl.BlockSpec((tm, w), lambda i: (i, 0))
    return pl.pallas_call(
        body, name="fwd_proj", grid=(s // tm,),
        in_specs=[row(D_MODEL), _full((1, D_MODEL)), _full((D_MODEL, 4 * ATTN_W)), _full((1, ATTN_W)),
                  _full((1, ATTN_W)), _full((MXU_W, MXU_W))],
        out_specs=[row(2 * ATTN_W), row(ATTN_W), row(ATTN_W), row(ATTN_W), row(ATTN_W), row(D_MODEL)],
        out_shape=[_sds((s, 2 * ATTN_W)), _sds((s, ATTN_W)), _sds((s, ATTN_W)), _sds((s, ATTN_W)),
                   _sds((s, ATTN_W)), _sds((s, D_MODEL), BF16)],
        compiler_params=_cp(("parallel",)),
    )(x, g1, w_in, gq, gk, gmat)


def _attn_rows(t, d, nb):
    if d == 1:
        q0 = t * ATTN_BLOCK
        return (t, pl.ds(q0, ATTN_BLOCK), pl.ds(ATTN_CHUNK + q0, ATTN_BLOCK),
                pl.ds(ATTN_CHUNK - ATTN_BLOCK + q0, ATTN_BLOCK))
    r = t // nb
    b = t % nb
    return (b, pl.ds(ATTN_BLOCK * b * d + r, ATTN_BLOCK, stride=d),
            pl.ds(ATTN_CHUNK + ATTN_BLOCK * b * d + r, ATTN_BLOCK, stride=d),
            pl.ds(ATTN_CHUNK + ATTN_BLOCK * (b - 1) * d + r, ATTN_BLOCK, stride=d))


def _attn_masks():
    row = lax.broadcasted_iota(jnp.int32, (ATTN_BLOCK, LANES), 0)
    col = lax.broadcasted_iota(jnp.int32, (ATTN_BLOCK, LANES), 1)
    return row, col


NBLK = ATTN_CHUNK // ATTN_BLOCK


def _attn_bias(bias_s):
    row, col = _attn_masks()
    bias_s[:, pl.ds(0, LANES)] = jnp.where(col >= row, 0.0, NEG_INF)
    bias_s[:, pl.ds(LANES, LANES)] = jnp.where(col <= row, 0.0, NEG_INF)
    return col < HEAD_DIM


def _attn_fwd(qn, kn, v, ex=None, group=4):
    s = qn.shape[0]
    nch = s // ATTN_CHUNK
    scale = HEAD_DIM ** -0.5
    npat = len(DILATIONS)
    n_hp = ATTN_W // LANES

    def body(*refs):
        ((q_ref, kp_ref, kc_ref, vp_ref, vc_ref), (o_ref, lse_ref),
         (kk, vv, kt_s, vb_s, bias_s, m_s, a_s), hx) = _carry(ex, refs, 5, 2)
        i = pl.program_id(1)
        if ex is not None:
            @pl.when(jnp.logical_and(pl.program_id(0) == 0, i == 0))
            def _():
                ex.start(*hx)

        kk[pl.ds(0, ATTN_CHUNK), :] = kp_ref[...]
        kk[pl.ds(ATTN_CHUNK, ATTN_CHUNK), :] = kc_ref[...]
        vv[pl.ds(0, ATTN_CHUNK), :] = vp_ref[...]
        vv[pl.ds(ATTN_CHUNK, ATTN_CHUNK), :] = vc_ref[...]
        head0 = _attn_bias(bias_s)
        first_pen = jnp.where(i > 0, 0.0, NEG_INF)

        for p, d in enumerate(DILATIONS):
            nb = ATTN_CHUNK // (ATTN_BLOCK * d)

            def prep(t, d=d, nb=nb):
                b, _, crows, prows = _attn_rows(t, d, nb)
                nxt = t + 1 if b + 1 < nb else None
                ktc = kk[crows, :].T.astype(BF16)
                vc = vv[crows, :]
                v0 = jnp.where(head0, vc, 1.0).astype(BF16)
                v1 = jnp.where(head0, 1.0, vc).astype(BF16)
                kt_s[t, :, pl.ds(LANES, LANES)] = ktc
                vb_s[2 * t, pl.ds(ATTN_BLOCK, ATTN_BLOCK), :] = v0
                vb_s[2 * t + 1, pl.ds(ATTN_BLOCK, ATTN_BLOCK), :] = v1
                if nxt is not None:
                    kt_s[nxt, :, pl.ds(0, LANES)] = ktc
                    vb_s[2 * nxt, pl.ds(0, ATTN_BLOCK), :] = v0
                    vb_s[2 * nxt + 1, pl.ds(0, ATTN_BLOCK), :] = v1
                if b == 0:
                    kt_s[t, :, pl.ds(0, LANES)] = kk[prows, :].T.astype(BF16)
                    vp = vv[prows, :]
                    vb_s[2 * t, pl.ds(0, ATTN_BLOCK), :] = jnp.where(head0, vp, 1.0).astype(BF16)
                    vb_s[2 * t + 1, pl.ds(0, ATTN_BLOCK), :] = jnp.where(head0, 1.0, vp).astype(BF16)

            def main(tg, p=p, d=d, nb=nb):
                st = []
                for g in range(group):
                    t = tg * group + g
                    b, qrows, _, _ = _attn_rows(t, d, nb)
                    q = q_ref[qrows, :] * scale
                    for h in range(2):
                        hm = head0 if h == 0 else jnp.logical_not(head0)
                        st.append(dict(t=t, b=b, qrows=qrows, sc=_dot(jnp.where(hm, q, 0.0).astype(BF16), kt_s[t])))
                for e in st:
                    sc = e["sc"] + bias_s[...]
                    s_p = sc[:, :LANES] + first_pen if e["b"] == 0 else sc[:, :LANES]
                    s_c = sc[:, LANES:]
                    m = jnp.max(jnp.maximum(s_p, s_c), axis=-1, keepdims=True)
                    e["eb"] = jnp.concatenate([jnp.exp(s_p - m), jnp.exp(s_c - m)], axis=1).astype(BF16)
                    e["m"] = jnp.broadcast_to(m, (ATTN_BLOCK, LANES))
                for g in range(group):
                    e0, e1 = st[2 * g], st[2 * g + 1]
                    t = e0["t"]
                    m_s[p, e0["qrows"], :] = jnp.where(head0, e0["m"], e1["m"])
                    a_s[2 * p, e0["qrows"], :] = _dot(e0["eb"], vb_s[2 * t])
                    a_s[2 * p + 1, e0["qrows"], :] = _dot(e1["eb"], vb_s[2 * t + 1])

            for g in range(group):
                prep(g)
            for tg in range(NBLK // group):
                if tg + 1 < NBLK // group:
                    for g in range(group):
                        prep((tg + 1) * group + g)
                main(tg)

        def merge(t, carry):
            rows = pl.ds(t * ATTN_BLOCK, ATTN_BLOCK)
            m_all = m_s[0, rows, :]
            for p in range(1, npat):
                m_all = jnp.maximum(m_all, m_s[p, rows, :])
            num = jnp.zeros((ATTN_BLOCK, LANES), F32)
            den = jnp.zeros((ATTN_BLOCK, LANES), F32)
            for p in range(npat):
                w = jnp.exp(m_s[p, rows, :] - m_all)
                a0, a1 = a_s[2 * p, rows, :], a_s[2 * p + 1, rows, :]
                num = num + jnp.where(head0, a0, a1) * w
                den = den + pltpu.roll(jnp.where(head0, a1, a0), HEAD_DIM, 1) * w
            o_ref[rows, :] = num / den
            lse_ref[rows, :] = m_all + jnp.log(den)
            return carry

        for t in range(NBLK):
            merge(t, 0)
        if ex is not None:
            @pl.when(jnp.logical_and(pl.program_id(0) == n_hp - 1, i == nch - 1))
            def _():
                ex.wait(*hx)

    cur = pl.BlockSpec((ATTN_CHUNK, LANES), lambda h, i: (i, h))
    prev = pl.BlockSpec((ATTN_CHUNK, LANES), lambda h, i: (jnp.maximum(i - 1, 0), h))
    hosted = ex is not None
    return pl.pallas_call(
        body, name="attn_fwd", grid=(n_hp, nch),
        in_specs=[cur, prev, cur, prev, cur] + (ex.specs() if hosted else []),
        out_specs=[cur, cur] + (ex.specs() if hosted else []),
        out_shape=[_sds((s, ATTN_W)), _sds((s, ATTN_W))] + (ex.out_shape() if hosted else []),
        scratch_shapes=[pltpu.VMEM((2 * ATTN_CHUNK, LANES), F32), pltpu.VMEM((2 * ATTN_CHUNK, LANES), F32),
                        pltpu.VMEM((NBLK, LANES, 2 * LANES), BF16), pltpu.VMEM((2 * NBLK, 2 * ATTN_BLOCK, LANES), BF16),
                        pltpu.VMEM((ATTN_BLOCK, 2 * LANES), F32),
                        pltpu.VMEM((npat, ATTN_CHUNK, LANES), F32), pltpu.VMEM((2 * npat, ATTN_CHUNK, LANES), F32)]
        + (ex.scratch() if hosted else []),
        compiler_params=_cp(("arbitrary", "arbitrary")),
    )(qn, kn, kn, v, v, *(ex.srcs if hosted else []))


def _attn_bwd(qn, kn, v, o, lse, do, ex=None, group=4):
    s = qn.shape[0]
    nch = s // ATTN_CHUNK
    scale = HEAD_DIM ** -0.5
    npat = len(DILATIONS)
    n_hp = ATTN_W // LANES

    def body(*refs):
        ((q_ref, kp_ref, kc_ref, vp_ref, vc_ref, o_ref, lse_ref, do_ref), (dq_ref, dk_ref, dv_ref),
         (kk, vv, dkk, dvv, kt_s, vt_s, kn_s, bias_s, dq_s, dl_s, dkb, dvb), hx) = _carry(ex, refs, 8, 3)
        step = pl.program_id(1)
        i = nch - 1 - step
        if ex is not None:
            @pl.when(jnp.logical_and(pl.program_id(0) == 0, step == 0))
            def _():
                ex.start(*hx)

        kk[pl.ds(0, ATTN_CHUNK), :] = kp_ref[...]
        kk[pl.ds(ATTN_CHUNK, ATTN_CHUNK), :] = kc_ref[...]
        vv[pl.ds(0, ATTN_CHUNK), :] = vp_ref[...]
        vv[pl.ds(ATTN_CHUNK, ATTN_CHUNK), :] = vc_ref[...]

        @pl.when(step == 0)
        def _():
            dkk[pl.ds(ATTN_CHUNK, ATTN_CHUNK), :] = jnp.zeros((ATTN_CHUNK, LANES), F32)
            dvv[pl.ds(ATTN_CHUNK, ATTN_CHUNK), :] = jnp.zeros((ATTN_CHUNK, LANES), F32)

        @pl.when(step > 0)
        def _():
            dkk[pl.ds(ATTN_CHUNK, ATTN_CHUNK), :] = dkk[pl.ds(0, ATTN_CHUNK), :]
            dvv[pl.ds(ATTN_CHUNK, ATTN_CHUNK), :] = dvv[pl.ds(0, ATTN_CHUNK), :]

        dkk[pl.ds(0, ATTN_CHUNK), :] = jnp.zeros((ATTN_CHUNK, LANES), F32)
        dvv[pl.ds(0, ATTN_CHUNK), :] = jnp.zeros((ATTN_CHUNK, LANES), F32)
        head0 = _attn_bias(bias_s)

        def delta(t, carry):
            rows = pl.ds(t * ATTN_BLOCK, ATTN_BLOCK)
            prod = do_ref[rows, :] * o_ref[rows, :]
            d0 = jnp.sum(jnp.where(head0, prod, 0.0), axis=-1, keepdims=True)
            d1 = jnp.sum(jnp.where(head0, 0.0, prod), axis=-1, keepdims=True)
            dl_s[rows, :] = jnp.where(head0, d0, d1)
            return carry

        for t in range(NBLK):
            delta(t, 0)

        first_pen = jnp.where(i > 0, 0.0, NEG_INF)

        for p, d in enumerate(DILATIONS):
            nb = ATTN_CHUNK // (ATTN_BLOCK * d)

            def prep(t, d=d, nb=nb):
                b, _, crows, prows = _attn_rows(t, d, nb)
                nxt = t + 1 if b + 1 < nb else None
                kc = kk[crows, :]
                ktc = kc.T.astype(BF16)
                knc = (kc * scale).astype(BF16)
                vtc = vv[crows, :].T.astype(BF16)
                kt_s[t, :, pl.ds(LANES, LANES)] = ktc
                kn_s[t, pl.ds(ATTN_BLOCK, ATTN_BLOCK), :] = knc
                vt_s[t, :, pl.ds(LANES, LANES)] = vtc
                if nxt is not None:
                    kt_s[nxt, :, pl.ds(0, LANES)] = ktc
                    kn_s[nxt, pl.ds(0, ATTN_BLOCK), :] = knc
                    vt_s[nxt, :, pl.ds(0, LANES)] = vtc
                if b == 0:
                    kp = kk[prows, :]
                    kt_s[t, :, pl.ds(0, LANES)] = kp.T.astype(BF16)
                    kn_s[t, pl.ds(0, ATTN_BLOCK), :] = (kp * scale).astype(BF16)
                    vt_s[t, :, pl.ds(0, LANES)] = vv[prows, :].T.astype(BF16)

            def main(tg, p=p, d=d, nb=nb):
                st = []
                for g in range(group):
                    t = tg * group + g
                    b, qrows, _, _ = _attn_rows(t, d, nb)
                    q = q_ref[qrows, :] * scale
                    dout = do_ref[qrows, :]
                    lse_b = lse_ref[qrows, :]
                    dl_b = dl_s[qrows, :]
                    for h in range(2):
                        hm = head0 if h == 0 else jnp.logical_not(head0)
                        c0 = h * HEAD_DIM
                        qh = jnp.where(hm, q, 0.0).astype(BF16)
                        doh = jnp.where(hm, dout, 0.0).astype(BF16)
                        st.append(dict(t=t, b=b, qrows=qrows, qh=qh, doh=doh, lse=lse_b[:, c0:c0 + 1],
                                       dl=dl_b[:, c0:c0 + 1], sc=_dot(qh, kt_s[t]), dp=_dot(doh, vt_s[t])))
                for e in st:
                    sc = e["sc"] + bias_s[...]
                    if e["b"] == 0:
                        sc = jnp.concatenate([sc[:, :LANES] + first_pen, sc[:, LANES:]], axis=1)
                    pr = jnp.exp(sc - e["lse"])
                    e["ds"] = (pr * (e["dp"] - e["dl"])).astype(BF16)
                    e["pr"] = pr.astype(BF16)
                for g in range(group):
                    e0, e1 = st[2 * g], st[2 * g + 1]
                    t = e0["t"]
                    dq_s[p, e0["qrows"], :] = jnp.where(head0, _dot(e0["ds"], kn_s[t]), _dot(e1["ds"], kn_s[t]))
                    dkb[t] = _dot_tn(e0["ds"], e0["qh"]) + _dot_tn(e1["ds"], e1["qh"])
                    dvb[t] = _dot_tn(e0["pr"], e0["doh"]) + _dot_tn(e1["pr"], e1["doh"])

            def scatter(t, d=d, nb=nb):
                b, _, crows, prows = _attn_rows(t, d, nb)
                dk_c = dkb[t, pl.ds(ATTN_BLOCK, ATTN_BLOCK), :]
                dv_c = dvb[t, pl.ds(ATTN_BLOCK, ATTN_BLOCK), :]
                if b + 1 < nb:
                    dk_c = dk_c + dkb[t + 1, pl.ds(0, ATTN_BLOCK), :]
                    dv_c = dv_c + dvb[t + 1, pl.ds(0, ATTN_BLOCK), :]
                dkk[crows, :] = dkk[crows, :] + dk_c
                dvv[crows, :] = dvv[crows, :] + dv_c
                if b == 0:
                    dkk[prows, :] = dkk[prows, :] + dkb[t, pl.ds(0, ATTN_BLOCK), :]
                    dvv[prows, :] = dvv[prows, :] + dvb[t, pl.ds(0, ATTN_BLOCK), :]

            n_groups = NBLK // group
            for g in range(group):
                prep(g)
            for tg in range(n_groups):
                if tg + 1 < n_groups:
                    for g in range(group):
                        prep((tg + 1) * group + g)
                main(tg)
                if tg >= 1:
                    for g in range(group):
                        scatter((tg - 1) * group + g)
            for g in range(group):
                scatter((n_groups - 1) * group + g)

        def finish(t, carry):
            rows = pl.ds(t * ATTN_BLOCK, ATTN_BLOCK)
            acc = dq_s[0, rows, :]
            for p in range(1, npat):
                acc = acc + dq_s[p, rows, :]
            dq_ref[rows, :] = acc
            return carry

        for t in range(NBLK):
            finish(t, 0)
        dk_ref[...] = dkk[pl.ds(ATTN_CHUNK, ATTN_CHUNK), :]
        dv_ref[...] = dvv[pl.ds(ATTN_CHUNK, ATTN_CHUNK), :]
        if ex is not None:
            @pl.when(jnp.logical_and(pl.program_id(0) == n_hp - 1, step == nch - 1))
            def _():
                ex.wait(*hx)

    cur = pl.BlockSpec((ATTN_CHUNK, LANES), lambda h, t: (nch - 1 - t, h))
    prev = pl.BlockSpec((ATTN_CHUNK, LANES), lambda h, t: (jnp.maximum(nch - 2 - t, 0), h))
    big = pltpu.VMEM((2 * ATTN_CHUNK, LANES), F32)
    pair_t = pltpu.VMEM((NBLK, LANES, 2 * LANES), BF16)
    hosted = ex is not None
    return pl.pallas_call(
        body, name="attn_bwd", grid=(n_hp, nch),
        in_specs=[cur, prev, cur, prev, cur, cur, cur, cur] + (ex.specs() if hosted else []),
        out_specs=[cur, cur, cur] + (ex.specs() if hosted else []),
        out_shape=[_sds((s, ATTN_W))] * 3 + (ex.out_shape() if hosted else []),
        scratch_shapes=[big, big, big, big, pair_t, pair_t, pltpu.VMEM((NBLK, 2 * ATTN_BLOCK, LANES), BF16),
                        pltpu.VMEM((ATTN_BLOCK, 2 * LANES), F32),
                        pltpu.VMEM((npat, ATTN_CHUNK, LANES), F32), pltpu.VMEM((ATTN_CHUNK, LANES), F32),
                        pltpu.VMEM((NBLK, 2 * ATTN_BLOCK, LANES), F32), pltpu.VMEM((NBLK, 2 * ATTN_BLOCK, LANES), F32)]
        + (ex.scratch() if hosted else []),
        compiler_params=_cp(("arbitrary", "arbitrary")),
    )(qn, kn, kn, v, v, o, lse, do, *(ex.srcs if hosted else []))


def _discretize(lr, li, dt):
    mag = jnp.exp(lr * dt)
    abr = mag * jnp.cos(li * dt)
    abi = mag * jnp.sin(li * dt)
    den = lr * lr + li * li
    nr, ni = abr - 1.0, abi
    cr = (nr * lr + ni * li) / den
    ci = (ni * lr - nr * li) / den
    return abr, abi, den, nr, ni, cr, ci


def _ssm_discretize(a_re, a_im, log_dt, b_re_t, b_im_t):
    def body(ar_ref, ai_ref, ldt_ref, br_ref, bi_ref, abr_ref, abi_ref, bbr_ref, bbi_ref):
        abr, abi, _, _, _, cr, ci = _discretize(ar_ref[...], ai_ref[...], jnp.exp(ldt_ref[...]))
        br, bi = br_ref[...], bi_ref[...]
        abr_ref[...] = abr
        abi_ref[...] = abi
        bbr_ref[...] = cr * br - ci * bi
        bbi_ref[...] = cr * bi + ci * br

    return pl.pallas_call(
        body, name="ssm_discretize",
        out_shape=[_sds(a_re.shape)] * 2 + [_sds(b_re_t.shape)] * 2,
    )(a_re, a_im, log_dt, b_re_t, b_im_t)


def _ssm_discretize_bwd(a_re, a_im, log_dt, b_re_t, b_im_t, dabr, dabi, dbbr, dbbi):
    def body(ar_ref, ai_ref, ldt_ref, br_ref, bi_ref, dabr_ref, dabi_ref, dbbr_ref, dbbi_ref,
             dar_ref, dai_ref, dldt_ref, dbr_ref, dbi_ref):
        lr, li = ar_ref[...], ai_ref[...]
        dt = jnp.exp(ldt_ref[...])
        abr, abi, den, nr, ni, cr, ci = _discretize(lr, li, dt)
        br, bi = br_ref[...], bi_ref[...]
        gbr, gbi = dbbr_ref[...], dbbi_ref[...]
        dcr = jnp.sum(gbr * br + gbi * bi, axis=1, keepdims=True)
        dci = jnp.sum(gbi * br - gbr * bi, axis=1, keepdims=True)
        dbr_ref[...] = cr * gbr + ci * gbi
        dbi_ref[...] = cr * gbi - ci * gbr
        dnr = (dcr * lr - dci * li) / den
        dni = (dcr * li + dci * lr) / den
        dden = -(dcr * cr + dci * ci) / den
        dlr = (dcr * nr + dci * ni) / den + dden * 2.0 * lr
        dli = (dcr * ni - dci * nr) / den + dden * 2.0 * li
        gabr = dabr_ref[...] + dnr
        gabi = dabi_ref[...] + dni
        dphi = gabr * abr + gabi * abi
        dth = gabi * abr - gabr * abi
        dar_ref[...] = dlr + dphi * dt
        dai_ref[...] = dli + dth * dt
        dldt_ref[...] = jnp.sum(dphi * lr + dth * li, axis=2, keepdims=True) * dt

    return pl.pallas_call(
        body, name="ssm_discretize_bwd",
        out_shape=[_sds(a_re.shape)] * 2 + [_sds(log_dt.shape)] + [_sds(b_re_t.shape)] * 2,
    )(a_re, a_im, log_dt, b_re_t, b_im_t, dabr, dabi, dbbr, dbbi)


def _ssm_power(abr, abi, n_sq):
    def body(r_ref, i_ref, or_ref, oi_ref):
        r, i = r_ref[...], i_ref[...]
        for _ in range(n_sq):
            r, i = r * r - i * i, 2.0 * r * i
        or_ref[...] = r
        oi_ref[...] = i

    return pl.pallas_call(body, name="ssm_power", out_shape=[_sds(abr.shape)] * 2)(abr, abi)


N_CB = SSM_W // LANES
CB_STATES = N_STATE // N_CB
ROWS = N_SEG * SSM_LK


class _Neg:
    def __init__(self, ref):
        self.ref = ref

    def __getitem__(self, idx):
        return -self.ref[idx]


def _seg_init(fin_r, fin_i, pw_r, pw_i, x_r, x_i, reverse):
    zero = jnp.zeros((1, N_STATE), F32)
    cr, ci = zero, zero
    order = range(N_SEG - 1, -1, -1) if reverse else range(N_SEG)
    pr = pw_r[...]
    pi = -pw_i[...] if reverse else pw_i[...]
    for j in order:
        x_r[pl.ds(j, 1), :] = cr
        x_i[pl.ds(j, 1), :] = ci
        fr, fi = fin_r[pl.ds(j, 1), :], fin_i[pl.ds(j, 1), :]
        cr, ci = fr + pr * cr - pi * ci, fi + pr * ci + pi * cr


def _permute_in(src_ref, dst):
    for c in range(N_CB):
        dst[c] = src_ref[:, :, pl.ds(c * LANES, LANES)].reshape(ROWS, LANES)


def _permute_out(src, dst_ref):
    for c in range(N_CB):
        dst_ref[:, :, pl.ds(c * LANES, LANES)] = src[c].reshape(SSM_LK, N_SEG, LANES)


def _scan_block(a_r, a_i, c, b_r, b_i, b_off, x_r, x_i, reverse, acc=None):
    cols = pl.ds(c * CB_STATES, CB_STATES)
    ar = jnp.broadcast_to(a_r[:, cols], (N_SEG, CB_STATES))
    ai = jnp.broadcast_to(a_i[:, cols], (N_SEG, CB_STATES))
    xr, xi = x_r[:, cols], x_i[:, cols]
    if acc is not None:
        sr = jnp.zeros((N_SEG, CB_STATES), F32)
        si = jnp.zeros((N_SEG, CB_STATES), F32)
    for t in range(SSM_LK):
        k = (SSM_LK - 1 - t) if reverse else t
        rows = pl.ds(k * N_SEG + b_off, N_SEG)
        xr, xi = ar * xr - ai * xi + b_r[rows, :], ar * xi + ai * xr + b_i[rows, :]
        b_r[rows, :] = xr
        b_i[rows, :] = xi
        if acc is not None:
            pr, pi = acc[0][pl.ds(k * N_SEG, N_SEG), :], acc[1][pl.ds(k * N_SEG, N_SEG), :]
            sr = sr + xr * pr + xi * pi
            si = si + xi * pr - xr * pi
    x_r[:, cols] = xr
    x_i[:, cols] = xi
    if acc is not None:
        acc[2][:, cols] += sr
        acc[3][:, cols] += si


def _ssm_fwd(u3, abr, abi, pw_r, pw_i, fin_r, fin_i, bb_r, bb_i, cc_r, cc_i, dskip, finals_only):
    sl = u3.shape[0]
    nch = sl // SSM_LK

    def body(u_ref, abr_ref, abi_ref, pwr_ref, pwi_ref, finr_ref, fini_ref, bbr_ref, bbi_ref,
             ccr_ref, cci_ref, d_ref, *rest):
        if finals_only:
            xfr_ref, xfi_ref, up, x_r, x_i = rest[:5]
        else:
            y_ref, xsr_ref, xsi_ref, up, yp, x_r, x_i = rest[:7]
        xs_r, xs_i = rest[-2 * N_CB:-N_CB], rest[-N_CB:]
        k = pl.program_id(0)

        @pl.when(k == 0)
        def _():
            _seg_init(finr_ref, fini_ref, pwr_ref, pwi_ref, x_r, x_i, False)

        if not finals_only:
            xsr_ref[0] = x_r[...]
            xsi_ref[0] = x_i[...]
        _permute_in(u_ref, up)

        def drive(c):
            lhs = up[c].astype(BF16)
            xs_r[c][...] = _dot(lhs, bbr_ref[c])
            xs_i[c][...] = _dot(lhs, bbi_ref[c])

        def readout(c):
            yp[c] = (_dot(xs_r[c][...].astype(BF16), ccr_ref[c]) - _dot(xs_i[c][...].astype(BF16), cci_ref[c])
                     + d_ref[:, pl.ds(c * LANES, LANES)] * up[c])

        drive(0)
        for c in range(N_CB):
            if c + 1 < N_CB:
                drive(c + 1)
            if c >= 1 and not finals_only:
                readout(c - 1)
            _scan_block(abr_ref, abi_ref, c, xs_r[c], xs_i[c], 0, x_r, x_i, False)
        if finals_only:
            @pl.when(k == nch - 1)
            def _():
                xfr_ref[...] = x_r[...]
                xfi_ref[...] = x_i[...]
        else:
            readout(N_CB - 1)
            _permute_out(yp, y_ref)

    ublk = pl.BlockSpec((SSM_LK, N_SEG, SSM_W), lambda k: (k, 0, 0))
    st = pl.BlockSpec((1, N_SEG, N_STATE), lambda k: (k, 0, 0))
    vec = _full((1, N_STATE))
    mat = _full((N_SEG, N_STATE))
    chunk = pltpu.VMEM((N_CB, ROWS, LANES), F32)
    blocks = [pltpu.VMEM((ROWS, CB_STATES), F32)] * (2 * N_CB)
    small = pltpu.VMEM((N_SEG, N_STATE), F32)
    if finals_only:
        out_specs, out_shape = [mat, mat], [_sds((N_SEG, N_STATE))] * 2
        scratch, name = [chunk, small, small] + blocks, "ssm_fwd_finals"
    else:
        out_specs = [ublk, st, st]
        out_shape = [_sds(u3.shape)] + [_sds((nch, N_SEG, N_STATE))] * 2
        scratch, name = [chunk, chunk, small, small] + blocks, "ssm_fwd"
    return pl.pallas_call(
        body, name=name, grid=(nch,),
        in_specs=[ublk, vec, vec, vec, vec, mat, mat,
                  _full((N_CB, LANES, CB_STATES)), _full((N_CB, LANES, CB_STATES)),
                  _full((N_CB, CB_STATES, LANES)), _full((N_CB, CB_STATES, LANES)), _full((1, SSM_W))],
        out_specs=out_specs, out_shape=out_shape, scratch_shapes=scratch,
        compiler_params=_cp(("arbitrary",)),
    )(u3, abr, abi, pw_r, pw_i, fin_r, fin_i, bb_r, bb_i, cc_r, cc_i, dskip)


def _ssm_bwd(u3, dy3, xst_r, xst_i, abr, abi, pw_r, pw_i, fin_r, fin_i, bb_r, bb_i, bbt_r, bbt_i,
             cct_r, cct_i, dskip, finals_only):
    sl = u3.shape[0]
    nch = sl // SSM_LK

    def body(u_ref, g_ref, xsr_ref, xsi_ref, abr_ref, abi_ref, pwr_ref, pwi_ref,
             finr_ref, fini_ref, bbr_ref, bbi_ref, btr_ref, bti_ref, ctr_ref, cti_ref, d_ref, *rest):
        if finals_only:
            lfr_ref, lfi_ref, gp, lam_r, lam_i = rest[:5]
            l_r, l_i = rest[-2 * N_CB:-N_CB], rest[-N_CB:]
        else:
            (du_ref, dar_ref, dai_ref, dbr_ref, dbi_ref, dcr_ref, dci_ref, dd_ref,
             gp, up, yp, lam_r, lam_i, x_r, x_i, sar, sai, sdd) = rest[:18]
            l_r, l_i = rest[18:18 + N_CB], rest[18 + N_CB:18 + 2 * N_CB]
            xx_r, xx_i = rest[18 + 2 * N_CB:18 + 3 * N_CB], rest[18 + 3 * N_CB:]
        t = pl.program_id(0)

        @pl.when(t == 0)
        def _():
            _seg_init(finr_ref, fini_ref, pwr_ref, pwi_ref, lam_r, lam_i, True)
            if not finals_only:
                sar[...] = jnp.zeros_like(sar)
                sai[...] = jnp.zeros_like(sai)
                sdd[...] = jnp.zeros_like(sdd)
                dbr_ref[...] = jnp.zeros_like(dbr_ref)
                dbi_ref[...] = jnp.zeros_like(dbi_ref)
                dcr_ref[...] = jnp.zeros_like(dcr_ref)
                dci_ref[...] = jnp.zeros_like(dci_ref)

        _permute_in(g_ref, gp)
        if not finals_only:
            _permute_in(u_ref, up)
            x_r[...] = xsr_ref[0]
            x_i[...] = xsi_ref[0]

        def drive(c):
            lhs = gp[c].astype(BF16)
            l_r[c][...] = _dot(lhs, ctr_ref[c])
            l_i[c][...] = -_dot(lhs, cti_ref[c])
            if not finals_only:
                cols = pl.ds(c * CB_STATES, CB_STATES)
                xx_r[c][pl.ds(0, N_SEG), :] = x_r[:, cols]
                xx_i[c][pl.ds(0, N_SEG), :] = x_i[:, cols]
                ub = up[c].astype(BF16)
                xx_r[c][pl.ds(N_SEG, ROWS), :] = _dot(ub, bbr_ref[c])
                xx_i[c][pl.ds(N_SEG, ROWS), :] = _dot(ub, bbi_ref[c])

        def collect(c):
            lrb = l_r[c][...].astype(BF16)
            lib = l_i[c][...].astype(BF16)
            ub = up[c].astype(BF16)
            gb = gp[c].astype(BF16)
            dbr_ref[c] += _dot_tn(lrb, ub)
            dbi_ref[c] += _dot_tn(lib, ub)
            dcr_ref[c] += _dot_tn(gb, xx_r[c][pl.ds(N_SEG, ROWS), :].astype(BF16))
            dci_ref[c] += -_dot_tn(gb, xx_i[c][pl.ds(N_SEG, ROWS), :].astype(BF16))
            yp[c] = _dot(lrb, btr_ref[c]) + _dot(lib, bti_ref[c]) + d_ref[:, pl.ds(c * LANES, LANES)] * gp[c]
            prod = gp[c] * up[c]
            sdd[:, pl.ds(c * LANES, LANES)] += jnp.sum(prod.reshape(SSM_LK, N_SEG, LANES), axis=0)

        drive(0)
        for c in range(N_CB):
            if c + 1 < N_CB:
                drive(c + 1)
            if finals_only:
                _scan_block(abr_ref, _Neg(abi_ref), c, l_r[c], l_i[c], 0, lam_r, lam_i, True)
            else:
                if c >= 1:
                    collect(c - 1)
                _scan_block(abr_ref, abi_ref, c, xx_r[c], xx_i[c], N_SEG, x_r, x_i, False)
                _scan_block(abr_ref, _Neg(abi_ref), c, l_r[c], l_i[c], 0, lam_r, lam_i, True,
                            acc=(xx_r[c], xx_i[c], sar, sai))
        if finals_only:
            @pl.when(t == nch - 1)
            def _():
                lfr_ref[...] = lam_r[...]
                lfi_ref[...] = lam_i[...]
        else:
            collect(N_CB - 1)
            _permute_out(yp, du_ref)

            @pl.when(t == nch - 1)
            def _():
                dar_ref[...] = jnp.sum(sar[...], axis=0, keepdims=True)
                dai_ref[...] = jnp.sum(sai[...], axis=0, keepdims=True)
                dd_ref[...] = jnp.sum(sdd[...], axis=0, keepdims=True)

    ublk = pl.BlockSpec((SSM_LK, N_SEG, SSM_W), lambda t: (nch - 1 - t, 0, 0))
    st = pl.BlockSpec((1, N_SEG, N_STATE), lambda t: (nch - 1 - t, 0, 0))
    vec = _full((1, N_STATE))
    mat = _full((N_SEG, N_STATE))
    cs = _full((N_CB, LANES, CB_STATES))
    sc = _full((N_CB, CB_STATES, LANES))
    in_specs = [ublk, ublk, st, st, vec, vec, vec, vec, mat, mat, cs, cs, sc, sc, cs, cs, _full((1, SSM_W))]
    chunk = pltpu.VMEM((N_CB, ROWS, LANES), F32)
    blocks = [pltpu.VMEM((ROWS, CB_STATES), F32)] * (2 * N_CB)
    small = pltpu.VMEM((N_SEG, N_STATE), F32)
    if finals_only:
        out_specs, out_shape = [mat, mat], [_sds((N_SEG, N_STATE))] * 2
        scratch, name = [chunk, small, small] + blocks, "ssm_bwd_finals"
    else:
        out_specs = [ublk, vec, vec, sc, sc, cs, cs, _full((1, SSM_W))]
        out_shape = ([_sds(u3.shape), _sds((1, N_STATE)), _sds((1, N_STATE))]
                     + [_sds((N_CB, CB_STATES, LANES))] * 2 + [_sds((N_CB, LANES, CB_STATES))] * 2
                     + [_sds((1, SSM_W))])
        scratch = ([chunk, chunk, chunk, small, small, small, small, small, small, pltpu.VMEM((N_SEG, SSM_W), F32)]
                   + blocks + [pltpu.VMEM((ROWS + N_SEG, CB_STATES), F32)] * (2 * N_CB))
        name = "ssm_bwd"
    return pl.pallas_call(
        body, name=name, grid=(nch,), in_specs=in_specs, out_specs=out_specs, out_shape=out_shape,
        scratch_shapes=scratch, compiler_params=_cp(("arbitrary",)),
    )(u3, dy3, xst_r, xst_i, abr, abi, pw_r, pw_i, fin_r, fin_i, bb_r, bb_i, bbt_r, bbt_i, cct_r, cct_i, dskip)


def _row(tm, w):
    return pl.BlockSpec((tm, w), lambda i: (i, 0))


def _acc_rows(ref, rows, first):
    @pl.when(first)
    def _():
        ref[...] = jnp.zeros_like(ref)

    ref[...] += jnp.sum(rows, axis=0, keepdims=True)


def _fwd_mix(attn, y, x, glu_w, glu_b, ga, gs, w_out, g2, tm=1024):
    s = x.shape[0]

    def body(a_ref, y_ref, x_ref, gw_ref, gb_ref, ga_ref, gs_ref, wo_ref, g2_ref, mix_ref, x2_ref, h_ref):
        a = a_ref[...]
        anb = ((a * _rms(a)) * ga_ref[...]).astype(BF16)
        z, _ = _gelu(y_ref[...])
        so = z * jax.nn.sigmoid(_dot(z.astype(BF16), gw_ref[...]) + gb_ref[...])
        snb = ((so * _rms(so)) * gs_ref[...]).astype(BF16)
        mix_ref[:, pl.ds(0, ATTN_W)] = anb
        mix_ref[:, pl.ds(ATTN_W, SSM_W)] = snb
        x2 = x_ref[...] + (_dot(anb, wo_ref[pl.ds(0, ATTN_W), :]) + _dot(snb, wo_ref[pl.ds(ATTN_W, SSM_W), :]))
        x2_ref[...] = x2
        h_ref[...] = ((x2 * _rms(x2)) * g2_ref[...]).astype(BF16)

    return pl.pallas_call(
        body, name="fwd_mix", grid=(s // tm,),
        in_specs=[_row(tm, ATTN_W), _row(tm, SSM_W), _row(tm, D_MODEL), _full((SSM_W, SSM_W)), _full((1, SSM_W)),
                  _full((1, ATTN_W)), _full((1, SSM_W)), _full((D_MODEL, D_MODEL)), _full((1, D_MODEL))],
        out_specs=[_row(tm, D_MODEL), _row(tm, D_MODEL), _row(tm, D_MODEL)],
        out_shape=[_sds((s, D_MODEL), BF16), _sds((s, D_MODEL)), _sds((s, D_MODEL), BF16)],
        compiler_params=_cp(("parallel",)),
    )(attn, y, x, glu_w, glu_b, ga, gs, w_out, g2)


def _mlp_up(h, w_up, tm=512, bn=1024):
    s = h.shape[0]

    def body(h_ref, w_ref, r_ref, hdn_ref):
        hv = h_ref[...]
        for j in range(D_FF // bn):
            cols = pl.ds(j * bn, bn)
            r = jnp.maximum(_dot(hv, w_ref[:, cols]), 0.0)
            r_ref[:, cols] = r.astype(BF16)
            hdn_ref[:, cols] = (r * r).astype(BF16)

    return pl.pallas_call(
        body, name="mlp_up", grid=(s // tm,),
        in_specs=[_row(tm, D_MODEL), _full((D_MODEL, D_FF))],
        out_specs=[_row(tm, D_FF), _row(tm, D_FF)], out_shape=[_sds((s, D_FF), BF16)] * 2,
        compiler_params=_cp(("parallel",)),
    )(h, w_up)


def _mlp_down_loss(hdn, w_down, x2, tgt, tm=512):
    s = x2.shape[0]

    def body(hdn_ref, w_ref, x2_ref, t_ref, dy_ref, dyb_ref, sse_ref):
        err = (x2_ref[...] + _dot(hdn_ref[...], w_ref[...])) - t_ref[...]
        dy = err * (1.0 / D_MODEL)
        dy_ref[...] = dy
        dyb_ref[...] = dy.astype(BF16)

        @pl.when(pl.program_id(0) == 0)
        def _():
            sse_ref[...] = jnp.zeros_like(sse_ref)

        sse_ref[...] += jnp.sum(jnp.sum(err * err, axis=0, keepdims=True), axis=1, keepdims=True)

    return pl.pallas_call(
        body, name="mlp_down_loss", grid=(s // tm,),
        in_specs=[_row(tm, D_FF), _full((D_FF, D_MODEL)), _row(tm, D_MODEL), _row(tm, D_MODEL)],
        out_specs=[_row(tm, D_MODEL), _row(tm, D_MODEL), _full((1, 1))],
        out_shape=[_sds((s, D_MODEL)), _sds((s, D_MODEL), BF16), _sds((1, 1))],
        compiler_params=_cp(("arbitrary",)),
    )(hdn, w_down, x2, tgt)


def _mlp_down_bwd(dyb, w_down_t, r, tm=512, bn=1024):
    s = dyb.shape[0]

    def body(dy_ref, w_ref, r_ref, dup_ref):
        dyv = dy_ref[...]
        for j in range(D_FF // bn):
            cols = pl.ds(j * bn, bn)
            dup_ref[:, cols] = (_dot(dyv, w_ref[:, cols]) * (2.0 * r_ref[:, cols].astype(F32))).astype(BF16)

    return pl.pallas_call(
        body, name="mlp_down_bwd", grid=(s // tm,),
        in_specs=[_row(tm, D_MODEL), _full((D_MODEL, D_FF)), _row(tm, D_FF)],
        out_specs=_row(tm, D_FF), out_shape=_sds((s, D_FF), BF16),
        compiler_params=_cp(("parallel",)),
    )(dyb, w_down_t, r)


def _mlp_up_bwd(dup, w_up_t, x2, g2, dy, tm=512):
    s = x2.shape[0]

    def body(dup_ref, w_ref, x2_ref, g2_ref, dy_ref, dx2_ref, dx2b_ref, dg_ref):
        dx, dg_rows = _rms_bwd(_dot(dup_ref[...], w_ref[...]), x2_ref[...], g2_ref[...], D_MODEL)
        dx2 = dy_ref[...] + dx
        dx2_ref[...] = dx2
        dx2b_ref[...] = dx2.astype(BF16)
        _acc_rows(dg_ref, dg_rows, pl.program_id(0) == 0)

    return pl.pallas_call(
        body, name="mlp_up_bwd", grid=(s // tm,),
        in_specs=[_row(tm, D_FF), _full((D_FF, D_MODEL)), _row(tm, D_MODEL), _full((1, D_MODEL)), _row(tm, D_MODEL)],
        out_specs=[_row(tm, D_MODEL), _row(tm, D_MODEL), _full((1, D_MODEL))],
        out_shape=[_sds((s, D_MODEL)), _sds((s, D_MODEL), BF16), _sds((1, D_MODEL))],
        compiler_params=_cp(("arbitrary",)),
    )(dup, w_up_t, x2, g2, dy)


def _mix_bwd(dx2b, w_out_t, attn, y, glu_w, glu_b, glu_w_t, ga, gs, tm=1024):
    s = attn.shape[0]

    def body(dx2_ref, wot_ref, a_ref, y_ref, gw_ref, gb_ref, gwt_ref, ga_ref, gs_ref,
             da_ref, dys_ref, z_ref, dpre_ref, dga_ref, dgs_ref, dgb_ref):
        first = pl.program_id(0) == 0
        dmix = _dot(dx2_ref[...], wot_ref[...])
        da, dga_rows = _rms_bwd(dmix[:, :ATTN_W], a_ref[...], ga_ref[...], ATTN_W)
        da_ref[...] = da
        yv = y_ref[...]
        z, t = _gelu(yv)
        gate = jax.nn.sigmoid(_dot(z.astype(BF16), gw_ref[...]) + gb_ref[...])
        dso, dgs_rows = _rms_bwd(dmix[:, ATTN_W:], z * gate, gs_ref[...], SSM_W)
        dpre = dso * z * gate * (1.0 - gate)
        dpre_b = dpre.astype(BF16)
        dz = dso * gate + _dot(dpre_b, gwt_ref[...])
        dgelu = 0.5 * (1.0 + t) + 0.5 * yv * (1.0 - t * t) * (GELU_C * (1.0 + 3.0 * 0.044715 * (yv * yv)))
        dys_ref[...] = dz * dgelu
        z_ref[...] = z.astype(BF16)
        dpre_ref[...] = dpre_b
        _acc_rows(dga_ref, dga_rows, first)
        _acc_rows(dgs_ref, dgs_rows, first)
        _acc_rows(dgb_ref, dpre, first)

    vec = _full((1, SSM_W))
    return pl.pallas_call(
        body, name="mix_bwd", grid=(s // tm,),
        in_specs=[_row(tm, D_MODEL), _full((D_MODEL, D_MODEL)), _row(tm, ATTN_W), _row(tm, SSM_W),
                  _full((SSM_W, SSM_W)), vec, _full((SSM_W, SSM_W)), vec, vec],
        out_specs=[_row(tm, ATTN_W), _row(tm, SSM_W), _row(tm, SSM_W), _row(tm, SSM_W), vec, vec, vec],
        out_shape=[_sds((s, ATTN_W)), _sds((s, SSM_W)), _sds((s, SSM_W), BF16), _sds((s, SSM_W), BF16),
                   _sds((1, ATTN_W)), _sds((1, SSM_W)), _sds((1, SSM_W))],
        compiler_params=_cp(("arbitrary",)),
    )(dx2b, w_out_t, attn, y, glu_w, glu_b, glu_w_t, ga, gs)


def _qk_bwd(dqn, dkn, qk, dv, du, gq, gk, gmat, tm=1024):
    s = qk.shape[0]

    def body(dq_ref, dk_ref, qk_ref, dv_ref, du_ref, gq_ref, gk_ref, gm_ref, dp_ref, dgq_ref, dgk_ref):
        first = pl.program_id(0) == 0
        gm = gm_ref[...]
        for idx, (d_ref, g_ref, dg_ref) in enumerate(((dq_ref, gq_ref, dgq_ref), (dk_ref, gk_ref, dgk_ref))):
            xv = qk_ref[:, pl.ds(idx * ATTN_W, ATTN_W)]
            dyv = d_ref[...]
            r = lax.rsqrt(_group_mean(xv * xv, gm) + EPS)
            gdy = dyv * g_ref[...]
            dx = r * gdy - xv * (r * r * r) * _group_mean(gdy * xv, gm)
            dp_ref[:, pl.ds(idx * ATTN_W, ATTN_W)] = dx.astype(BF16)
            _acc_rows(dg_ref, dyv * (xv * r), first)
        dp_ref[:, pl.ds(2 * ATTN_W, ATTN_W)] = dv_ref[...].astype(BF16)
        dp_ref[:, pl.ds(3 * ATTN_W, SSM_W)] = du_ref[...].astype(BF16)

    vec = _full((1, ATTN_W))
    return pl.pallas_call(
        body, name="qk_bwd", grid=(s // tm,),
        in_specs=[_row(tm, ATTN_W), _row(tm, ATTN_W), _row(tm, 2 * ATTN_W), _row(tm, ATTN_W), _row(tm, SSM_W),
                  vec, vec, _full((MXU_W, MXU_W))],
        out_specs=[_row(tm, 4 * ATTN_W), vec, vec],
        out_shape=[_sds((s, 4 * ATTN_W), BF16), _sds((1, ATTN_W)), _sds((1, ATTN_W))],
        compiler_params=_cp(("arbitrary",)),
    )(dqn, dkn, qk, dv, du, gq, gk, gmat)


def _in_bwd(dproj, w_in_t, x, g1, dx2, ex=None, tm=1024):
    s = x.shape[0]
    steps = s // tm

    def body(*refs):
        (dp_ref, w_ref, x_ref, g1_ref, dx2_ref), (gx_ref, dg_ref), _, hx = _carry(ex, refs, 5, 2)
        if ex is not None:
            @pl.when(pl.program_id(0) == 0)
            def _():
                ex.start(*hx)

        dx, dg_rows = _rms_bwd(_dot(dp_ref[...], w_ref[...]), x_ref[...], g1_ref[...], D_MODEL)
        gx_ref[...] = dx2_ref[...] + dx
        _acc_rows(dg_ref, dg_rows, pl.program_id(0) == 0)
        if ex is not None:
            @pl.when(pl.program_id(0) == steps - 1)
            def _():
                ex.wait(*hx)

    hosted = ex is not None
    return pl.pallas_call(
        body, name="in_bwd", grid=(steps,),
        in_specs=[_row(tm, 4 * ATTN_W), _full((4 * ATTN_W, D_MODEL)), _row(tm, D_MODEL), _full((1, D_MODEL)),
                  _row(tm, D_MODEL)] + (ex.specs() if hosted else []),
        out_specs=[_row(tm, D_MODEL), _full((1, D_MODEL))] + (ex.specs() if hosted else []),
        out_shape=[_sds((s, D_MODEL)), _sds((1, D_MODEL))] + (ex.out_shape() if hosted else []),
        scratch_shapes=ex.scratch() if hosted else [],
        compiler_params=_cp(("arbitrary",)),
    )(dproj, w_in_t, x, g1, dx2, *(ex.srcs if hosted else []))


def _mm_tn(a, b, name, ts=2048):
    s, k = a.shape
    n = b.shape[1]
    bk, bn = min(k, 1024), min(n, 1024)
    steps = s // ts

    def body(a_ref, b_ref, o_ref, acc):
        t = pl.program_id(2)

        @pl.when(t == 0)
        def _():
            acc[...] = jnp.zeros_like(acc)

        acc[...] += _dot_tn(a_ref[...], b_ref[...])

        @pl.when(t == steps - 1)
        def _():
            o_ref[...] = acc[...].astype(BF16)

    return pl.pallas_call(
        body, name=name, grid=(k // bk, n // bn, steps),
        in_specs=[pl.BlockSpec((ts, bk), lambda i, j, t: (t, i)), pl.BlockSpec((ts, bn), lambda i, j, t: (t, j))],
        out_specs=pl.BlockSpec((bk, bn), lambda i, j, t: (i, j)), out_shape=_sds((k, n), BF16),
        scratch_shapes=[pltpu.VMEM((bk, bn), F32)],
        compiler_params=_cp(("parallel", "parallel", "arbitrary")),
    )(a, b)


def _peer(k):
    x, y, c = lax.axis_index("x"), lax.axis_index("y"), lax.axis_index("c")
    px = 1 - x if k & 4 else x
    py = 1 - y if k & 2 else y
    pc = 1 - c if k & 1 else c
    return (px, py, pc), 4 * px + 2 * py + pc


def _gather_rows(x_shard):
    m_per, n = x_shard.shape

    def body(x_ref, out_ref, send_sems, recv_sems, local_sem):
        x, y, c = lax.axis_index("x"), lax.axis_index("y"), lax.axis_index("c")
        me, sibling = (x, y, c), (x, y, 1 - c)
        chips = [(1 - x, y), (x, 1 - y), (1 - x, 1 - y)]

        def rows(px, py, pc):
            return out_ref.at[pl.ds((4 * px + 2 * py + pc) * m_per, m_per), :]

        def copy(k, block, to, src=None):
            return pltpu.make_async_remote_copy(
                src_ref=rows(*block) if src is None else src, dst_ref=rows(*block),
                send_sem=send_sems.at[k], recv_sem=recv_sems.at[k], device_id=to, device_id_type=MESH)

        mine = pltpu.make_async_copy(x_ref, rows(*me), local_sem)
        mine.start()
        first = [copy(0, me, sibling, src=x_ref)]
        first += [copy(1 + j, me, (*chip, c), src=x_ref) for j, chip in enumerate(chips)]
        for cp in first:
            cp.start()
        passed = [copy(4 + j, (*chip, c), sibling) for j, chip in enumerate(chips)]
        for j, chip in enumerate(chips):
            copy(1 + j, (*chip, c), me).wait_recv()
            passed[j].start()
        copy(0, sibling, me).wait_recv()
        for j, chip in enumerate(chips):
            copy(4 + j, (*chip, 1 - c), me).wait_recv()
        for cp in first + passed:
            cp.wait_send()
        mine.wait()

    return pl.pallas_call(
        body, name="gather_weights", out_shape=_sds((N_DEV * m_per, n), x_shard.dtype),
        in_specs=[pl.BlockSpec(memory_space=pltpu.VMEM)], out_specs=pl.BlockSpec(memory_space=pltpu.VMEM),
        scratch_shapes=[pltpu.SemaphoreType.DMA((7,)), pltpu.SemaphoreType.DMA((7,)), pltpu.SemaphoreType.DMA],
        compiler_params=pltpu.CompilerParams(vmem_limit_bytes=VMEM_LIMIT),
    )(x_shard)


class _Exchange:
    def __init__(self, srcs, whole):
        self.srcs, self.whole, self.n = list(srcs), list(whole), len(srcs)
        self.rows = [a.shape[0] // N_DEV if w is False else a.shape[0] for a, w in zip(self.srcs, self.whole)]
        self.cols = [a.shape[1] // N_DEV if w == "col" else a.shape[1] for a, w in zip(self.srcs, self.whole)]

    def specs(self):
        return [pl.BlockSpec(memory_space=pl.ANY)] * self.n

    def out_shape(self):
        return [_sds((N_DEV, r, c), a.dtype) for r, c, a in zip(self.rows, self.cols, self.srcs)]

    def scratch(self):
        return [pltpu.SemaphoreType.DMA((self.n * 7,)), pltpu.SemaphoreType.DMA((self.n * 7,)),
                pltpu.SemaphoreType.DMA((self.n,))]

    def _copies(self, ins, outs, sems):
        send_sems, recv_sems, local_sems = sems
        _, me = _peer(0)
        for w in range(self.n):
            for k in range(N_DEV):
                peer, pidx = _peer(k)
                if self.whole[w] is True:
                    src = ins[w]
                elif self.whole[w] == "col":
                    src = ins[w].at[:, pl.ds(pl.multiple_of(pidx * self.cols[w], self.cols[w]), self.cols[w])]
                else:
                    src = ins[w].at[pl.ds(pidx * self.rows[w], self.rows[w]), :]
                if k == 0:
                    yield k, pltpu.make_async_copy(src, outs[w].at[me], local_sems.at[w]), None
                else:
                    sem = w * 7 + k - 1
                    out = pltpu.make_async_remote_copy(src_ref=src, dst_ref=outs[w].at[me], send_sem=send_sems.at[sem],
                                                       recv_sem=recv_sems.at[sem], device_id=peer, device_id_type=MESH)
                    back = pltpu.make_async_remote_copy(src_ref=src, dst_ref=outs[w].at[pidx], send_sem=send_sems.at[sem],
                                                        recv_sem=recv_sems.at[sem], device_id=peer, device_id_type=MESH)
                    yield k, out, back

    def start(self, ins, outs, sems):
        for _, out, _ in self._copies(ins, outs, sems):
            out.start()

    def wait(self, ins, outs, sems):
        for k, out, back in self._copies(ins, outs, sems):
            if k == 0:
                out.wait()
            else:
                back.wait_recv()
                out.wait_send()


def _carry(ex, refs, n_in, n_out):
    nh = ex.n if ex is not None else 0
    ins, hin = refs[:n_in], refs[n_in:n_in + nh]
    outs = refs[n_in + nh:n_in + nh + n_out]
    hout = refs[n_in + nh + n_out:n_in + 2 * nh + n_out]
    rest = refs[n_in + 2 * nh + n_out:]
    if ex is None:
        return ins, outs, rest, None
    return ins, outs, rest[:-3], (hin, hout, rest[-3:])


def _exchange_now(srcs, whole, name):
    ex = _Exchange(srcs, whole)

    def body(*refs):
        _, _, _, (hin, hout, sems) = _carry(ex, refs, 0, 0)
        ex.start(hin, hout, sems)
        ex.wait(hin, hout, sems)

    return pl.pallas_call(body, name=name, out_shape=ex.out_shape(), in_specs=ex.specs(), out_specs=ex.specs(),
                          scratch_shapes=ex.scratch())(*srcs)


def _split_exchange_start(src):
    rows, width = src.shape[0], src.shape[1] // N_DEV

    def body(src_ref, land_ref, send_sems, recv_sems, src_thru, land_thru, token):
        barrier = pltpu.get_barrier_semaphore()
        for k in range(1, N_DEV):
            pl.semaphore_signal(barrier, inc=1, device_id=_peer(k)[0], device_id_type=MESH)
        pl.semaphore_wait(barrier, N_DEV - 1)
        _, me = _peer(0)
        for k in range(1, N_DEV):
            peer, pidx = _peer(k)
            pltpu.make_async_remote_copy(
                src_ref=src_ref.at[:, pl.ds(pl.multiple_of(pidx * width, width), width)], dst_ref=land_ref.at[me],
                send_sem=send_sems.at[k - 1], recv_sem=recv_sems.at[k - 1], device_id=peer, device_id_type=MESH).start()
        token[...] = jnp.zeros_like(token)

    hbm = pl.BlockSpec(memory_space=pltpu.HBM)
    sem = pl.BlockSpec(memory_space=pltpu.SEMAPHORE)
    land = lax.empty((N_DEV, rows, width), src.dtype)
    return pl.pallas_call(
        body, name="w_in_exchange_start",
        out_shape=(pltpu.SemaphoreType.DMA((N_DEV - 1,)), pltpu.SemaphoreType.DMA((N_DEV - 1,)),
                   pltpu.HBM(src.shape, src.dtype), pltpu.HBM(land.shape, land.dtype), _sds((8, LANES))),
        in_specs=(hbm, hbm), out_specs=(sem, sem, hbm, hbm, pl.BlockSpec(memory_space=pltpu.VMEM)),
        input_output_aliases={0: 2, 1: 3},
        compiler_params=pltpu.CompilerParams(has_side_effects=pltpu.SideEffectType.DATAFLOW_SIDE_EFFECTING, collective_id=0),
    )(pltpu.with_memory_space_constraint(src, pltpu.HBM), pltpu.with_memory_space_constraint(land, pltpu.HBM))


def _split_exchange_wait(send_sems, recv_sems, src_thru, land_thru, after):
    width = land_thru.shape[2]

    def body(src_ref, land_ref, send_sems, recv_sems, after_ref, src_dead, got_ref):
        _, me = _peer(0)
        for k in range(1, N_DEV):
            peer, pidx = _peer(k)
            copy = pltpu.make_async_remote_copy(
                src_ref=src_ref.at[:, pl.ds(pl.multiple_of(pidx * width, width), width)], dst_ref=land_ref.at[pidx],
                send_sem=send_sems.at[k - 1], recv_sem=recv_sems.at[k - 1], device_id=peer, device_id_type=MESH)
            copy.wait_send()
            copy.wait_recv()

    hbm = pl.BlockSpec(memory_space=pltpu.HBM)
    sem = pl.BlockSpec(memory_space=pltpu.SEMAPHORE)
    return pl.pallas_call(
        body, name="w_in_exchange_wait",
        out_shape=(pltpu.HBM(src_thru.shape, src_thru.dtype), pltpu.HBM(land_thru.shape, land_thru.dtype)),
        in_specs=(hbm, hbm, sem, sem, pl.BlockSpec(memory_space=pl.ANY)), out_specs=(hbm, hbm),
        input_output_aliases={0: 0, 1: 1},
        compiler_params=pltpu.CompilerParams(has_side_effects=pltpu.SideEffectType.DATAFLOW_SIDE_EFFECTING),
    )(src_thru, land_thru, send_sems, recv_sems, after)


def _adamw(w, m, v, gparts, name):
    r, c = w.shape
    tr = r if r * c <= 256 * 1024 else 128 * 1024 // c

    def body(w_ref, m_ref, v_ref, g_ref, go_ref, d_ref, mo_ref, vo_ref):
        g = g_ref[0].astype(F32)
        for i in range(1, N_DEV):
            g = g + g_ref[i].astype(F32)
        go_ref[...] = g
        d_ref[...], mo_ref[...], vo_ref[...] = _adamw_step(w_ref[...], m_ref[...], v_ref[...], g)

    blk = pl.BlockSpec((tr, c), lambda i: (i, 0))
    return pl.pallas_call(
        body, name=name, grid=(r // tr,),
        in_specs=[blk, blk, blk, pl.BlockSpec((N_DEV, tr, c), lambda i: (0, i, 0))],
        out_specs=[blk] * 4, out_shape=[_sds((r, c))] * 4,
        compiler_params=_cp(("parallel",)),
    )(w, m, v, gparts)


def _adamw_step(w, m, v, g):
    nm = ADAM_B1 * m + (1.0 - ADAM_B1) * g
    nv = ADAM_B2 * v + (1.0 - ADAM_B2) * (g * g)
    m_hat = nm / (1.0 - ADAM_B1 ** ADAM_STEP)
    v_hat = nv / (1.0 - ADAM_B2 ** ADAM_STEP)
    return -ADAM_LR * (m_hat / (jnp.sqrt(v_hat) + ADAM_EPS) + ADAM_WD * w), nm, nv


def _sum_slots(parts):
    def body(p_ref, o_ref):
        g = p_ref[0]
        for i in range(1, N_DEV):
            g = g + p_ref[i]
        o_ref[...] = g

    return pl.pallas_call(body, name="sum_small_grads", out_shape=_sds(parts.shape[1:]))(parts)


def _adamw_native(ws, ms, vs, gs):
    n = len(ws)

    def body(*refs):
        for i in range(n):
            w_ref, m_ref, v_ref, g_ref = refs[i], refs[n + i], refs[2 * n + i], refs[3 * n + i]
            d, nm, nv = _adamw_step(w_ref[...], m_ref[...], v_ref[...], g_ref[...])
            refs[4 * n + i][...] = d
            refs[5 * n + i][...] = nm
            refs[6 * n + i][...] = nv

    outs = pl.pallas_call(body, name="adamw_small", out_shape=[_sds(w.shape) for w in ws] * 3,
                          compiler_params=pltpu.CompilerParams(vmem_limit_bytes=VMEM_LIMIT))(*ws, *ms, *vs, *gs)
    return outs[:n], outs[n:2 * n], outs[2 * n:]


def _block_diag(a, states_first):
    a4 = a.reshape(N_CB, 8, SSM_GROUP, SSM_STATE)
    eye = jnp.eye(8, dtype=a.dtype)
    if states_first:
        return jnp.einsum("bgcp,gh->bgphc", a4, eye).reshape(N_CB, CB_STATES, LANES)
    return jnp.einsum("bgcp,gh->bgchp", a4, eye).reshape(N_CB, LANES, CB_STATES)


def _block_diag_of(full, states_first):
    if states_first:
        picked = jnp.einsum("bgphc,gh->bgcp", full.reshape(N_CB, 8, SSM_STATE, 8, SSM_GROUP), jnp.eye(8, dtype=full.dtype))
    else:
        picked = jnp.einsum("bgchp,gh->bgcp", full.reshape(N_CB, 8, SSM_GROUP, 8, SSM_STATE), jnp.eye(8, dtype=full.dtype))
    return picked.reshape(SSM_GROUPS, SSM_GROUP, SSM_STATE)


SMALL_EARLY = ("ssm_a_re", "ssm_a_im", "ssm_log_dt", "ssm_b_re", "ssm_b_im", "ssm_c_re", "ssm_c_im", "ssm_d", "glu_b",
               "attn_out_norm_g", "ssm_out_norm_g", "norm2_g")
SMALL_MID = ("q_norm_g", "k_norm_g")
SMALL_LATE = ("norm1_g",)
SMALL = SMALL_EARLY + SMALL_MID + SMALL_LATE


def _pack_small(arrs):
    parts = []
    for a in arrs:
        flat = a.reshape(-1)
        rows = -(-flat.shape[0] // (8 * LANES)) * 8
        parts.append(jnp.pad(flat, (0, rows * LANES - flat.shape[0])).reshape(rows, LANES))
    return jnp.concatenate(parts, axis=0)


def _unpack_small(packed, shapes):
    out, r0 = [], 0
    for shp in shapes:
        size = math.prod(shp)
        rows = -(-size // (8 * LANES)) * 8
        out.append(packed[r0:r0 + rows].reshape(-1)[:size].reshape(shp))
        r0 += rows
    return out


def kernel(x, norm1_g, w_in, q_norm_g, k_norm_g, ssm_a_re, ssm_a_im, ssm_log_dt, ssm_b_re, ssm_b_im, ssm_c_re, ssm_c_im, ssm_d, glu_w, glu_b, attn_out_norm_g, ssm_out_norm_g, w_out, norm2_g, w_mlp_up, w_mlp_down, loss_target, m_norm1_g, m_w_in, m_q_norm_g, m_k_norm_g, m_ssm_a_re, m_ssm_a_im, m_ssm_log_dt, m_ssm_b_re, m_ssm_b_im, m_ssm_c_re, m_ssm_c_im, m_ssm_d, m_glu_w, m_glu_b, m_attn_out_norm_g, m_ssm_out_norm_g, m_w_out, m_norm2_g, m_w_mlp_up, m_w_mlp_down, v_norm1_g, v_w_in, v_q_norm_g, v_k_norm_g, v_ssm_a_re, v_ssm_a_im, v_ssm_log_dt, v_ssm_b_re, v_ssm_b_im, v_ssm_c_re, v_ssm_c_im, v_ssm_d, v_glu_w, v_glu_b, v_attn_out_norm_g, v_ssm_out_norm_g, v_w_out, v_norm2_g, v_w_mlp_up, v_w_mlp_down):
    weights = dict(norm1_g=norm1_g, w_in=w_in, q_norm_g=q_norm_g, k_norm_g=k_norm_g, ssm_a_re=ssm_a_re,
                   ssm_a_im=ssm_a_im, ssm_log_dt=ssm_log_dt, ssm_b_re=ssm_b_re, ssm_b_im=ssm_b_im,
                   ssm_c_re=ssm_c_re, ssm_c_im=ssm_c_im, ssm_d=ssm_d, glu_w=glu_w, glu_b=glu_b,
                   attn_out_norm_g=attn_out_norm_g, ssm_out_norm_g=ssm_out_norm_g, w_out=w_out, norm2_g=norm2_g,
                   w_mlp_up=w_mlp_up, w_mlp_down=w_mlp_down)
    mom_m = dict(norm1_g=m_norm1_g, w_in=m_w_in, q_norm_g=m_q_norm_g, k_norm_g=m_k_norm_g, ssm_a_re=m_ssm_a_re,
                 ssm_a_im=m_ssm_a_im, ssm_log_dt=m_ssm_log_dt, ssm_b_re=m_ssm_b_re, ssm_b_im=m_ssm_b_im,
                 ssm_c_re=m_ssm_c_re, ssm_c_im=m_ssm_c_im, ssm_d=m_ssm_d, glu_w=m_glu_w, glu_b=m_glu_b,
                 attn_out_norm_g=m_attn_out_norm_g, ssm_out_norm_g=m_ssm_out_norm_g, w_out=m_w_out,
                 norm2_g=m_norm2_g, w_mlp_up=m_w_mlp_up, w_mlp_down=m_w_mlp_down)
    mom_v = dict(norm1_g=v_norm1_g, w_in=v_w_in, q_norm_g=v_q_norm_g, k_norm_g=v_k_norm_g, ssm_a_re=v_ssm_a_re,
                 ssm_a_im=v_ssm_a_im, ssm_log_dt=v_ssm_log_dt, ssm_b_re=v_ssm_b_re, ssm_b_im=v_ssm_b_im,
                 ssm_c_re=v_ssm_c_re, ssm_c_im=v_ssm_c_im, ssm_d=v_ssm_d, glu_w=v_glu_w, glu_b=v_glu_b,
                 attn_out_norm_g=v_attn_out_norm_g, ssm_out_norm_g=v_ssm_out_norm_g, w_out=v_w_out,
                 norm2_g=v_norm2_g, w_mlp_up=v_w_mlp_up, w_mlp_down=v_w_mlp_down)
    order = list(weights)

    xs, tgt = x[0], loss_target[0]
    s = xs.shape[0]
    assert s % ATTN_CHUNK == 0 and (s // N_SEG) % SSM_LK == 0
    seg_len = s // N_SEG
    n_sq = seg_len.bit_length() - 1
    assert 1 << n_sq == seg_len

    w_in_t = _gather_rows(w_in[0].T.astype(BF16))
    w_in_full = w_in_t.T
    later = _Exchange([glu_w[0].astype(BF16), w_out[0].astype(BF16), w_mlp_up[0].T.astype(BF16),
                       w_mlp_down[0].astype(BF16)], [True] * 4)

    gq = jnp.tile(q_norm_g[0], ATTN_W // HEAD_DIM)[None]
    gk = jnp.tile(k_norm_g[0], ATTN_W // HEAD_DIM)[None]
    lane = jnp.arange(MXU_W) // HEAD_DIM
    gmat = jnp.where(lane[:, None] == lane[None, :], 1.0 / HEAD_DIM, 0.0).astype(BF16)
    a_re3 = ssm_a_re[0][:, None, :]
    a_im3 = ssm_a_im[0][:, None, :]
    ldt3 = ssm_log_dt[0][:, None, None]
    b_re_t = jnp.swapaxes(ssm_b_re[0], 1, 2)
    b_im_t = jnp.swapaxes(ssm_b_im[0], 1, 2)
    c_re, c_im = ssm_c_re[0], ssm_c_im[0]
    dskip = ssm_d[0].reshape(1, SSM_W)

    qk, qn, kn, v, u, xn = _fwd_proj(xs, norm1_g, w_in_full, gq, gk, gmat)
    attn, lse, glu_g, w_out_g, w_up_g, w_down_g = _attn_fwd(qn, kn, v, later)
    glu_full = glu_g.reshape(SSM_W, SSM_W)
    w_out_full = w_out_g.reshape(D_MODEL, D_MODEL)
    w_up_t = w_up_g.reshape(D_FF, D_MODEL)
    w_down_full = w_down_g.reshape(D_FF, D_MODEL)
    w_up_full, w_out_t, w_down_t, glu_t = w_up_t.T, w_out_full.T, w_down_full.T, glu_full.T

    abr3, abi3, bbr, bbi = _ssm_discretize(a_re3, a_im3, ldt3, b_re_t, b_im_t)
    abr, abi = abr3.reshape(1, N_STATE), abi3.reshape(1, N_STATE)
    pw_r, pw_i = _ssm_power(abr, abi, n_sq)
    bb_r, bb_i = _block_diag(bbr, False).astype(BF16), _block_diag(bbi, False).astype(BF16)
    bbt_r, bbt_i = _block_diag(bbr, True).astype(BF16), _block_diag(bbi, True).astype(BF16)
    cc_r, cc_i = _block_diag(c_re, True).astype(BF16), _block_diag(c_im, True).astype(BF16)
    cct_r, cct_i = _block_diag(c_re, False).astype(BF16), _block_diag(c_im, False).astype(BF16)
    seg_major = lambda a: jnp.swapaxes(a.reshape(N_SEG, seg_len, SSM_W), 0, 1)
    seg_minor = lambda a: jnp.swapaxes(a, 0, 1).reshape(s, SSM_W)
    u3 = seg_major(u)
    zero_fin = jnp.zeros((N_SEG, N_STATE), F32)
    ssm_args = (abr, abi, pw_r, pw_i)
    xf_r, xf_i = _ssm_fwd(u3, *ssm_args, zero_fin, zero_fin, bb_r, bb_i, cc_r, cc_i, dskip, True)
    y3, xst_r, xst_i = _ssm_fwd(u3, *ssm_args, xf_r, xf_i, bb_r, bb_i, cc_r, cc_i, dskip, False)
    y = seg_minor(y3)

    mix, x2, h = _fwd_mix(attn, y, xs, glu_full, glu_b, attn_out_norm_g, ssm_out_norm_g, w_out_full, norm2_g)
    r_act, hdn = _mlp_up(h, w_up_full)
    dy, dyb, sse = _mlp_down_loss(hdn, w_down_full, x2, tgt)

    dup = _mlp_down_bwd(dyb, w_down_t, r_act)
    g_w_down = _mm_tn(hdn, dyb, "grad_w_down")
    dx2, dx2b, g_norm2 = _mlp_up_bwd(dup, w_up_t, x2, norm2_g, dy)
    g_w_up = _mm_tn(h, dup, "grad_w_up")
    dattn, dys, z_b, dpre_b, g_ga, g_gs, g_glu_b = _mix_bwd(dx2b, w_out_t, attn, y, glu_full, glu_b, glu_t,
                                                             attn_out_norm_g, ssm_out_norm_g)
    g_w_out = _mm_tn(mix, dx2b, "grad_w_out")
    g_glu_w = _mm_tn(z_b, dpre_b, "grad_glu_w")
    dy3 = seg_major(dys)
    bwd_args = (u3, dy3, xst_r, xst_i, abr, abi, pw_r, pw_i)
    lf_r, lf_i = _ssm_bwd(*bwd_args, zero_fin, zero_fin, bb_r, bb_i, bbt_r, bbt_i, cct_r, cct_i, dskip, True)
    du3, dab_r, dab_i, dbb_r, dbb_i, dcc_r, dcc_i, g_d = _ssm_bwd(*bwd_args, lf_r, lf_i, bb_r, bb_i, bbt_r, bbt_i,
                                                                 cct_r, cct_i, dskip, False)
    g_a_re3, g_a_im3, g_ldt3, g_b_re_t, g_b_im_t = _ssm_discretize_bwd(
        a_re3, a_im3, ldt3, b_re_t, b_im_t, dab_r.reshape(a_re3.shape), dab_i.reshape(a_re3.shape),
        _block_diag_of(dbb_r, True), _block_diag_of(dbb_i, True))
    g_c_re, g_c_im = _block_diag_of(dcc_r, False), _block_diag_of(dcc_i, False)
    small_grads = dict(
        ssm_a_re=g_a_re3.reshape(ssm_a_re.shape), ssm_a_im=g_a_im3.reshape(ssm_a_im.shape),
        ssm_log_dt=g_ldt3.reshape(ssm_log_dt.shape), ssm_b_re=jnp.swapaxes(g_b_re_t, 1, 2)[None],
        ssm_b_im=jnp.swapaxes(g_b_im_t, 1, 2)[None], ssm_c_re=g_c_re[None], ssm_c_im=g_c_im[None],
        ssm_d=g_d.reshape(ssm_d.shape), glu_b=g_glu_b, attn_out_norm_g=g_ga, ssm_out_norm_g=g_gs, norm2_g=g_norm2)

    early = _Exchange([g_glu_w, g_w_out, g_w_up, g_w_down, _pack_small([small_grads[n] for n in SMALL_EARLY] + [sse])],
                      [False, False, "col", False, True])
    dqn, dkn, dv, p_glu, p_w_out, p_w_up, p_w_down, p_early = _attn_bwd(qn, kn, v, attn, lse, dattn, early)

    dproj, g_gq, g_gk = _qk_bwd(dqn, dkn, qk, dv, seg_minor(du3), gq, gk, gmat)
    g_w_in = _mm_tn(xn, dproj, "grad_w_in")
    small_grads["q_norm_g"] = g_gq.reshape(ATTN_W // HEAD_DIM, HEAD_DIM).sum(0)[None]
    small_grads["k_norm_g"] = g_gk.reshape(ATTN_W // HEAD_DIM, HEAD_DIM).sum(0)[None]
    send_sems, recv_sems, g_thru, land, token = _split_exchange_start(g_w_in)
    grad_x, g_norm1 = _in_bwd(dproj, w_in_t, xs, norm1_g + token[:1, :1], dx2)
    g_w_in, landed = _split_exchange_wait(send_sems, recv_sems, g_thru, land, g_norm1)
    me = 4 * lax.axis_index("x") + 2 * lax.axis_index("y") + lax.axis_index("c")
    shard_w = g_w_in.shape[1] // N_DEV
    own = lax.dynamic_slice(g_w_in, (0, me * shard_w), (g_w_in.shape[0], shard_w))
    p_w_in = lax.dynamic_update_slice(landed, own[None], (me, 0, 0))
    p_mid, p_late = _exchange_now([_pack_small([small_grads[n] for n in SMALL_MID]), _pack_small([g_norm1])],
                                  [True, True], "exchange_tail")

    res = {}
    for name, gp in (("w_in", p_w_in), ("glu_w", p_glu), ("w_out", p_w_out), ("w_mlp_up", p_w_up), ("w_mlp_down", p_w_down)):
        outs = _adamw(weights[name][0], mom_m[name][0], mom_v[name][0], gp, "adamw_" + name)
        res[name] = [o[None] for o in outs]
    shapes = [weights[n].shape for n in SMALL_EARLY] + [sse.shape] + [weights[n].shape for n in SMALL_MID + SMALL_LATE]
    g_small = _unpack_small(_sum_slots(jnp.concatenate([p_early, p_mid, p_late], axis=1)), shapes)
    loss = 0.5 * g_small.pop(len(SMALL_EARLY))[0, 0] / D_MODEL
    d_small, m_small, v_small = _adamw_native([weights[n] for n in SMALL], [mom_m[n] for n in SMALL],
                                              [mom_v[n] for n in SMALL], g_small)
    for i, n in enumerate(SMALL):
        res[n] = [g_small[i], d_small[i], m_small[i], v_small[i]]

    return (loss, grad_x[None], *[res[n][0] for n in order], *[res[n][1] for n in order],
            *[res[n][2] for n in order], *[res[n][3] for n in order])
```

```python
import math

import jax
import jax.numpy as jnp
from jax import lax
from jax.experimental import pallas as pl
from jax.experimental.pallas import tpu as pltpu

F32 = jnp.float32
BF16 = jnp.bfloat16

D_MODEL = 1024
ATTN_W = 512
HEAD_DIM = 64
SSM_W = 512
SSM_GROUP = 16
SSM_GROUPS = 32
SSM_STATE = 64
N_STATE = SSM_GROUPS * SSM_STATE
D_FF = 4096
EPS = 1e-6
NEG_INF = -1e30
ATTN_CHUNK = 2048
ATTN_BLOCK = 128
DILATIONS = (1, 4, 16)
N_SEG = 8
SSM_LK = 64
N_DEV = 8
LANES = 128
MXU_W = 256

ADAM_LR = 0.001
ADAM_B1 = 0.9
ADAM_B2 = 0.999
ADAM_EPS = 1e-08
ADAM_WD = 0.01
ADAM_STEP = 10

VMEM_LIMIT = 56 * 1024 * 1024
GELU_C = math.sqrt(2.0 / math.pi)
MESH = pl.DeviceIdType.MESH


def _cp(sem, vmem=VMEM_LIMIT):
    return pltpu.CompilerParams(dimension_semantics=sem, vmem_limit_bytes=vmem)


def _dot(a, b):
    return jnp.dot(a, b, preferred_element_type=F32)


def _dot_tn(a, b):
    return lax.dot_general(a, b, (((0,), (0,)), ((), ())), preferred_element_type=F32)


def _group_mean(x2, gmat):
    hi = x2.astype(BF16)
    lo = (x2 - hi.astype(F32)).astype(BF16)
    w = gmat.shape[0]
    return jnp.concatenate([_dot(hi[:, c:c + w], gmat) + _dot(lo[:, c:c + w], gmat) for c in range(0, x2.shape[1], w)],
                           axis=1)


def _rms(x):
    return lax.rsqrt(jnp.mean(x * x, axis=-1, keepdims=True) + EPS)


def _rms_bwd(dy, x, g, n):
    r = _rms(x)
    gdy = dy * g
    dx = r * gdy - x * (r * r * r) * (jnp.sum(gdy * x, axis=-1, keepdims=True) / n)
    return dx, dy * (x * r)


def _gelu(y):
    t = jnp.tanh(GELU_C * (y + 0.044715 * (y * y * y)))
    return 0.5 * y * (1.0 + t), t


def _full(shape):
    nd = len(shape)
    return pl.BlockSpec(shape, lambda *_: (0,) * nd)


def _sds(shape, dtype=F32):
    return jax.ShapeDtypeStruct(shape, dtype)


def _fwd_proj(x, g1, w_in, gq, gk, gmat, tm=512):
    s = x.shape[0]

    def body(x_ref, g1_ref, w_ref, gq_ref, gk_ref, gm_ref, qk_ref, qn_ref, kn_ref, v_ref, u_ref, xn_ref):
        xv = x_ref[...]
        xnb = ((xv * _rms(xv)) * g1_ref[...]).astype(BF16)
        xn_ref[...] = xnb
        proj = _dot(xnb, w_ref[...])
        q = proj[:, :ATTN_W]
        k = proj[:, ATTN_W:2 * ATTN_W]
        qk_ref[...] = proj[:, :2 * ATTN_W]
        v_ref[...] = proj[:, 2 * ATTN_W:3 * ATTN_W]
        u_ref[...] = proj[:, 3 * ATTN_W:]
        gm = gm_ref[...]
        qn_ref[...] = (q * lax.rsqrt(_group_mean(q * q, gm) + EPS)) * gq_ref[...]
        kn_ref[...] = (k * lax.rsqrt(_group_mean(k * k, gm) + EPS)) * gk_ref[...]

    row = lambda w: pl.BlockSpec((tm, w), lambda i: (i, 0))
    return pl.pallas_call(
        body, name="fwd_proj", grid=(s // tm,),
        in_specs=[row(D_MODEL), _full((1, D_MODEL)), _full((D_MODEL, 4 * ATTN_W)), _full((1, ATTN_W)),
                  _full((1, ATTN_W)), _full((MXU_W, MXU_W))],
        out_specs=[row(2 * ATTN_W), row(ATTN_W), row(ATTN_W), row(ATTN_W), row(ATTN_W), row(D_MODEL)],
        out_shape=[_sds((s, 2 * ATTN_W)), _sds((s, ATTN_W)), _sds((s, ATTN_W)), _sds((s, ATTN_W)),
                   _sds((s, ATTN_W)), _sds((s, D_MODEL), BF16)],
        compiler_params=_cp(("parallel",)),
    )(x, g1, w_in, gq, gk, gmat)


def _attn_rows(t, d, nb):
    if d == 1:
        q0 = t * ATTN_BLOCK
        return (t, pl.ds(q0, ATTN_BLOCK), pl.ds(ATTN_CHUNK + q0, ATTN_BLOCK),
                pl.ds(ATTN_CHUNK - ATTN_BLOCK + q0, ATTN_BLOCK))
    r = t // nb
    b = t % nb
    return (b, pl.ds(ATTN_BLOCK * b * d + r, ATTN_BLOCK, stride=d),
            pl.ds(ATTN_CHUNK + ATTN_BLOCK * b * d + r, ATTN_BLOCK, stride=d),
            pl.ds(ATTN_CHUNK + ATTN_BLOCK * (b - 1) * d + r, ATTN_BLOCK, stride=d))


def _attn_masks():
    row = lax.broadcasted_iota(jnp.int32, (ATTN_BLOCK, LANES), 0)
    col = lax.broadcasted_iota(jnp.int32, (ATTN_BLOCK, LANES), 1)
    return row, col


NBLK = ATTN_CHUNK // ATTN_BLOCK


def _attn_bias(bias_s):
    row, col = _attn_masks()
    bias_s[:, pl.ds(0, LANES)] = jnp.where(col >= row, 0.0, NEG_INF)
    bias_s[:, pl.ds(LANES, LANES)] = jnp.where(col <= row, 0.0, NEG_INF)
    return col < HEAD_DIM


def _attn_fwd(qn, kn, v, ex=None, group=4):
    s = qn.shape[0]
    nch = s // ATTN_CHUNK
    scale = HEAD_DIM ** -0.5
    npat = len(DILATIONS)
    n_hp = ATTN_W // LANES

    def body(*refs):
        ((q_ref, kp_ref, kc_ref, vp_ref, vc_ref), (o_ref, lse_ref),
         (kk, vv, kt_s, vb_s, bias_s, m_s, a_s), hx) = _carry(ex, refs, 5, 2)
        i = pl.program_id(1)
        if ex is not None:
            @pl.when(jnp.logical_and(pl.program_id(0) == 0, i == 0))
            def _():
                ex.start(*hx)

        kk[pl.ds(0, ATTN_CHUNK), :] = kp_ref[...]
        kk[pl.ds(ATTN_CHUNK, ATTN_CHUNK), :] = kc_ref[...]
        vv[pl.ds(0, ATTN_CHUNK), :] = vp_ref[...]
        vv[pl.ds(ATTN_CHUNK, ATTN_CHUNK), :] = vc_ref[...]
        head0 = _attn_bias(bias_s)
        first_pen = jnp.where(i > 0, 0.0, NEG_INF)

        for p, d in enumerate(DILATIONS):
            nb = ATTN_CHUNK // (ATTN_BLOCK * d)

            def prep(t, d=d, nb=nb):
                b, _, crows, prows = _attn_rows(t, d, nb)
                nxt = t + 1 if b + 1 < nb else None
                ktc = kk[crows, :].T.astype(BF16)
                vc = vv[crows, :]
                vcb = vc.astype(BF16)
                one = jnp.ones((ATTN_BLOCK, LANES), BF16)
                v0 = jnp.where(head0, vcb, one)
                v1 = jnp.where(head0, one, vcb)
                kt_s[t, :, pl.ds(LANES, LANES)] = ktc
                vb_s[2 * t, pl.ds(ATTN_BLOCK, ATTN_BLOCK), :] = v0
                vb_s[2 * t + 1, pl.ds(ATTN_BLOCK, ATTN_BLOCK), :] = v1
                if nxt is not None:
                    kt_s[nxt, :, pl.ds(0, LANES)] = ktc
                    vb_s[2 * nxt, pl.ds(0, ATTN_BLOCK), :] = v0
                    vb_s[2 * nxt + 1, pl.ds(0, ATTN_BLOCK), :] = v1
                if b == 0:
                    kt_s[t, :, pl.ds(0, LANES)] = kk[prows, :].T.astype(BF16)
                    vp = vv[prows, :]
                    vb_s[2 * t, pl.ds(0, ATTN_BLOCK), :] = jnp.where(head0, vp, 1.0).astype(BF16)
                    vb_s[2 * t + 1, pl.ds(0, ATTN_BLOCK), :] = jnp.where(head0, 1.0, vp).astype(BF16)

            def main(tg, p=p, d=d, nb=nb):
                st = []
                for g in range(group):
                    t = tg * group + g
                    b, qrows, _, _ = _attn_rows(t, d, nb)
                    q = (q_ref[qrows, :] * scale).astype(BF16)
                    zero = jnp.zeros((ATTN_BLOCK, LANES), BF16)
                    for h in range(2):
                        qh = jnp.where(head0, q, zero) if h == 0 else jnp.where(head0, zero, q)
                        st.append(dict(t=t, b=b, qrows=qrows, sc=_dot(qh, kt_s[t])))
                for e in st:
                    sc = e["sc"] + bias_s[...]
                    s_p = sc[:, :LANES] + first_pen if e["b"] == 0 else sc[:, :LANES]
                    s_c = sc[:, LANES:]
                    m = jnp.max(jnp.maximum(s_p, s_c), axis=-1, keepdims=True)
                    e["eb"] = jnp.concatenate([jnp.exp(s_p - m), jnp.exp(s_c - m)], axis=1).astype(BF16)
                    e["m"] = jnp.broadcast_to(m, (ATTN_BLOCK, LANES))
                for g in range(group):
                    e0, e1 = st[2 * g], st[2 * g + 1]
                    t = e0["t"]
                    m_s[p, e0["qrows"], :] = jnp.where(head0, e0["m"], e1["m"])
                    a_s[2 * p, e0["qrows"], :] = _dot(e0["eb"], vb_s[2 * t])
                    a_s[2 * p + 1, e0["qrows"], :] = _dot(e1["eb"], vb_s[2 * t + 1])

            for g in range(group):
                prep(g)
            for tg in range(NBLK // group):
                if tg + 1 < NBLK // group:
                    for g in range(group):
                        prep((tg + 1) * group + g)
                main(tg)

        def merge(t, carry):
            rows = pl.ds(t * ATTN_BLOCK, ATTN_BLOCK)
            m_all = m_s[0, rows, :]
            for p in range(1, npat):
                m_all = jnp.maximum(m_all, m_s[p, rows, :])
            num = jnp.zeros((ATTN_BLOCK, LANES), F32)
            den = jnp.zeros((ATTN_BLOCK, LANES), F32)
            for p in range(npat):
                w = jnp.exp(m_s[p, rows, :] - m_all)
                a0, a1 = a_s[2 * p, rows, :], a_s[2 * p + 1, rows, :]
                num = num + jnp.where(head0, a0, a1) * w
                den = den + pltpu.roll(jnp.where(head0, a1, a0), HEAD_DIM, 1) * w
            o_ref[rows, :] = num / den
            lse_ref[rows, :] = m_all + jnp.log(den)
            return carry

        for t in range(NBLK):
            merge(t, 0)
        if ex is not None:
            @pl.when(jnp.logical_and(pl.program_id(0) == n_hp - 1, i == nch - 1))
            def _():
                ex.wait(*hx)

    cur = pl.BlockSpec((ATTN_CHUNK, LANES), lambda h, i: (i, h))
    prev = pl.BlockSpec((ATTN_CHUNK, LANES), lambda h, i: (jnp.maximum(i - 1, 0), h))
    hosted = ex is not None
    return pl.pallas_call(
        body, name="attn_fwd", grid=(n_hp, nch),
        in_specs=[cur, prev, cur, prev, cur] + (ex.specs() if hosted else []),
        out_specs=[cur, cur] + (ex.specs() if hosted else []),
        out_shape=[_sds((s, ATTN_W)), _sds((s, ATTN_W))] + (ex.out_shape() if hosted else []),
        scratch_shapes=[pltpu.VMEM((2 * ATTN_CHUNK, LANES), F32), pltpu.VMEM((2 * ATTN_CHUNK, LANES), F32),
                        pltpu.VMEM((NBLK, LANES, 2 * LANES), BF16), pltpu.VMEM((2 * NBLK, 2 * ATTN_BLOCK, LANES), BF16),
                        pltpu.VMEM((ATTN_BLOCK, 2 * LANES), F32),
                        pltpu.VMEM((npat, ATTN_CHUNK, LANES), F32), pltpu.VMEM((2 * npat, ATTN_CHUNK, LANES), F32)]
        + (ex.scratch() if hosted else []),
        compiler_params=_cp(("arbitrary", "arbitrary")),
    )(qn, kn, kn, v, v, *(ex.srcs if hosted else []))


def _attn_bwd(qn, kn, v, o, lse, do, ex=None, group=4):
    s = qn.shape[0]
    nch = s // ATTN_CHUNK
    scale = HEAD_DIM ** -0.5
    npat = len(DILATIONS)
    n_hp = ATTN_W // LANES

    def body(*refs):
        ((q_ref, kp_ref, kc_ref, vp_ref, vc_ref, o_ref, lse_ref, do_ref), (dq_ref, dk_ref, dv_ref),
         (kk, vv, dkk, dvv, kt_s, vt_s, kn_s, bias_s, dq_s, dl_s, dkb, dvb), hx) = _carry(ex, refs, 8, 3)
        step = pl.program_id(1)
        i = nch - 1 - step
        if ex is not None:
            @pl.when(jnp.logical_and(pl.program_id(0) == 0, step == 0))
            def _():
                ex.start(*hx)

        kk[pl.ds(0, ATTN_CHUNK), :] = kp_ref[...]
        kk[pl.ds(ATTN_CHUNK, ATTN_CHUNK), :] = kc_ref[...]
        vv[pl.ds(0, ATTN_CHUNK), :] = vp_ref[...]
        vv[pl.ds(ATTN_CHUNK, ATTN_CHUNK), :] = vc_ref[...]

        @pl.when(step == 0)
        def _():
            dkk[pl.ds(ATTN_CHUNK, ATTN_CHUNK), :] = jnp.zeros((ATTN_CHUNK, LANES), F32)
            dvv[pl.ds(ATTN_CHUNK, ATTN_CHUNK), :] = jnp.zeros((ATTN_CHUNK, LANES), F32)

        @pl.when(step > 0)
        def _():
            dkk[pl.ds(ATTN_CHUNK, ATTN_CHUNK), :] = dkk[pl.ds(0, ATTN_CHUNK), :]
            dvv[pl.ds(ATTN_CHUNK, ATTN_CHUNK), :] = dvv[pl.ds(0, ATTN_CHUNK), :]

        dkk[pl.ds(0, ATTN_CHUNK), :] = jnp.zeros((ATTN_CHUNK, LANES), F32)
        dvv[pl.ds(0, ATTN_CHUNK), :] = jnp.zeros((ATTN_CHUNK, LANES), F32)
        head0 = _attn_bias(bias_s)

        def delta(t, carry):
            rows = pl.ds(t * ATTN_BLOCK, ATTN_BLOCK)
            prod = do_ref[rows, :] * o_ref[rows, :]
            d0 = jnp.sum(jnp.where(head0, prod, 0.0), axis=-1, keepdims=True)
            d1 = jnp.sum(jnp.where(head0, 0.0, prod), axis=-1, keepdims=True)
            dl_s[rows, :] = jnp.where(head0, d0, d1)
            return carry

        for t in range(NBLK):
            delta(t, 0)

        first_pen = jnp.where(i > 0, 0.0, NEG_INF)

        for p, d in enumerate(DILATIONS):
            nb = ATTN_CHUNK // (ATTN_BLOCK * d)

            def prep(t, d=d, nb=nb):
                b, _, crows, prows = _attn_rows(t, d, nb)
                nxt = t + 1 if b + 1 < nb else None
                kc = kk[crows, :]
                ktc = kc.T.astype(BF16)
                knc = (kc * scale).astype(BF16)
                vtc = vv[crows, :].T.astype(BF16)
                kt_s[t, :, pl.ds(LANES, LANES)] = ktc
                kn_s[t, pl.ds(ATTN_BLOCK, ATTN_BLOCK), :] = knc
                vt_s[t, :, pl.ds(LANES, LANES)] = vtc
                if nxt is not None:
                    kt_s[nxt, :, pl.ds(0, LANES)] = ktc
                    kn_s[nxt, pl.ds(0, ATTN_BLOCK), :] = knc
                    vt_s[nxt, :, pl.ds(0, LANES)] = vtc
                if b == 0:
                    kp = kk[prows, :]
                    kt_s[t, :, pl.ds(0, LANES)] = kp.T.astype(BF16)
                    kn_s[t, pl.ds(0, ATTN_BLOCK), :] = (kp * scale).astype(BF16)
                    vt_s[t, :, pl.ds(0, LANES)] = vv[prows, :].T.astype(BF16)

            def main(tg, p=p, d=d, nb=nb):
                st = []
                for g in range(group):
                    t = tg * group + g
                    b, qrows, _, _ = _attn_rows(t, d, nb)
                    q = q_ref[qrows, :] * scale
                    dout = do_ref[qrows, :]
                    lse_b = lse_ref[qrows, :]
                    dl_b = dl_s[qrows, :]
                    for h in range(2):
                        hm = head0 if h == 0 else jnp.logical_not(head0)
                        c0 = h * HEAD_DIM
                        qh = jnp.where(hm, q, 0.0).astype(BF16)
                        doh = jnp.where(hm, dout, 0.0).astype(BF16)
                        st.append(dict(t=t, b=b, qrows=qrows, qh=qh, doh=doh, lse=lse_b[:, c0:c0 + 1],
                                       dl=dl_b[:, c0:c0 + 1], sc=_dot(qh, kt_s[t]), dp=_dot(doh, vt_s[t])))
                for e in st:
                    sc = e["sc"] + bias_s[...]
                    if e["b"] == 0:
                        sc = jnp.concatenate([sc[:, :LANES] + first_pen, sc[:, LANES:]], axis=1)
                    pr = jnp.exp(sc - e["lse"])
                    e["ds"] = (pr * (e["dp"] - e["dl"])).astype(BF16)
                    e["pr"] = pr.astype(BF16)
                for g in range(group):
                    e0, e1 = st[2 * g], st[2 * g + 1]
                    t = e0["t"]
                    dq_s[p, e0["qrows"], :] = jnp.where(head0, _dot(e0["ds"], kn_s[t]), _dot(e1["ds"], kn_s[t]))
                    dkb[t] = _dot_tn(e0["ds"], e0["qh"]) + _dot_tn(e1["ds"], e1["qh"])
                    dvb[t] = _dot_tn(e0["pr"], e0["doh"]) + _dot_tn(e1["pr"], e1["doh"])

            def scatter(t, d=d, nb=nb):
                b, _, crows, prows = _attn_rows(t, d, nb)
                dk_c = dkb[t, pl.ds(ATTN_BLOCK, ATTN_BLOCK), :]
                dv_c = dvb[t, pl.ds(ATTN_BLOCK, ATTN_BLOCK), :]
                if b + 1 < nb:
                    dk_c = dk_c + dkb[t + 1, pl.ds(0, ATTN_BLOCK), :]
                    dv_c = dv_c + dvb[t + 1, pl.ds(0, ATTN_BLOCK), :]
                dkk[crows, :] = dkk[crows, :] + dk_c
                dvv[crows, :] = dvv[crows, :] + dv_c
                if b == 0:
                    dkk[prows, :] = dkk[prows, :] + dkb[t, pl.ds(0, ATTN_BLOCK), :]
                    dvv[prows, :] = dvv[prows, :] + dvb[t, pl.ds(0, ATTN_BLOCK), :]

            n_groups = NBLK // group
            for g in range(group):
                prep(g)
            for tg in range(n_groups):
                if tg + 1 < n_groups:
                    for g in range(group):
                        prep((tg + 1) * group + g)
                main(tg)
                if tg >= 1:
                    for g in range(group):
                        scatter((tg - 1) * group + g)
            for g in range(group):
                scatter((n_groups - 1) * group + g)

        def finish(t, carry):
            rows = pl.ds(t * ATTN_BLOCK, ATTN_BLOCK)
            acc = dq_s[0, rows, :]
            for p in range(1, npat):
                acc = acc + dq_s[p, rows, :]
            dq_ref[rows, :] = acc
            return carry

        for t in range(NBLK):
            finish(t, 0)
        dk_ref[...] = dkk[pl.ds(ATTN_CHUNK, ATTN_CHUNK), :]
        dv_ref[...] = dvv[pl.ds(ATTN_CHUNK, ATTN_CHUNK), :]
        if ex is not None:
            @pl.when(jnp.logical_and(pl.program_id(0) == n_hp - 1, step == nch - 1))
            def _():
                ex.wait(*hx)

    cur = pl.BlockSpec((ATTN_CHUNK, LANES), lambda h, t: (nch - 1 - t, h))
    prev = pl.BlockSpec((ATTN_CHUNK, LANES), lambda h, t: (jnp.maximum(nch - 2 - t, 0), h))
    big = pltpu.VMEM((2 * ATTN_CHUNK, LANES), F32)
    pair_t = pltpu.VMEM((NBLK, LANES, 2 * LANES), BF16)
    hosted = ex is not None
    return pl.pallas_call(
        body, name="attn_bwd", grid=(n_hp, nch),
        in_specs=[cur, prev, cur, prev, cur, cur, cur, cur] + (ex.specs() if hosted else []),
        out_specs=[cur, cur, cur] + (ex.specs() if hosted else []),
        out_shape=[_sds((s, ATTN_W))] * 3 + (ex.out_shape() if hosted else []),
        scratch_shapes=[big, big, big, big, pair_t, pair_t, pltpu.VMEM((NBLK, 2 * ATTN_BLOCK, LANES), BF16),
                        pltpu.VMEM((ATTN_BLOCK, 2 * LANES), F32),
                        pltpu.VMEM((npat, ATTN_CHUNK, LANES), F32), pltpu.VMEM((ATTN_CHUNK, LANES), F32),
                        pltpu.VMEM((NBLK, 2 * ATTN_BLOCK, LANES), F32), pltpu.VMEM((NBLK, 2 * ATTN_BLOCK, LANES), F32)]
        + (ex.scratch() if hosted else []),
        compiler_params=_cp(("arbitrary", "arbitrary")),
    )(qn, kn, kn, v, v, o, lse, do, *(ex.srcs if hosted else []))


def _discretize(lr, li, dt):
    mag = jnp.exp(lr * dt)
    abr = mag * jnp.cos(li * dt)
    abi = mag * jnp.sin(li * dt)
    den = lr * lr + li * li
    nr, ni = abr - 1.0, abi
    cr = (nr * lr + ni * li) / den
    ci = (ni * lr - nr * li) / den
    return abr, abi, den, nr, ni, cr, ci


def _ssm_discretize(a_re, a_im, log_dt, b_re_t, b_im_t):
    def body(ar_ref, ai_ref, ldt_ref, br_ref, bi_ref, abr_ref, abi_ref, bbr_ref, bbi_ref):
        abr, abi, _, _, _, cr, ci = _discretize(ar_ref[...], ai_ref[...], jnp.exp(ldt_ref[...]))
        br, bi = br_ref[...], bi_ref[...]
        abr_ref[...] = abr
        abi_ref[...] = abi
        bbr_ref[...] = cr * br - ci * bi
        bbi_ref[...] = cr * bi + ci * br

    return pl.pallas_call(
        body, name="ssm_discretize",
        out_shape=[_sds(a_re.shape)] * 2 + [_sds(b_re_t.shape)] * 2,
    )(a_re, a_im, log_dt, b_re_t, b_im_t)


def _ssm_discretize_bwd(a_re, a_im, log_dt, b_re_t, b_im_t, dabr, dabi, dbbr, dbbi):
    def body(ar_ref, ai_ref, ldt_ref, br_ref, bi_ref, dabr_ref, dabi_ref, dbbr_ref, dbbi_ref,
             dar_ref, dai_ref, dldt_ref, dbr_ref, dbi_ref):
        lr, li = ar_ref[...], ai_ref[...]
        dt = jnp.exp(ldt_ref[...])
        abr, abi, den, nr, ni, cr, ci = _discretize(lr, li, dt)
        br, bi = br_ref[...], bi_ref[...]
        gbr, gbi = dbbr_ref[...], dbbi_ref[...]
        dcr = jnp.sum(gbr * br + gbi * bi, axis=1, keepdims=True)
        dci = jnp.sum(gbi * br - gbr * bi, axis=1, keepdims=True)
        dbr_ref[...] = cr * gbr + ci * gbi
        dbi_ref[...] = cr * gbi - ci * gbr
        dnr = (dcr * lr - dci * li) / den
        dni = (dcr * li + dci * lr) / den
        dden = -(dcr * cr + dci * ci) / den
        dlr = (dcr * nr + dci * ni) / den + dden * 2.0 * lr
        dli = (dcr * ni - dci * nr) / den + dden * 2.0 * li
        gabr = dabr_ref[...] + dnr
        gabi = dabi_ref[...] + dni
        dphi = gabr * abr + gabi * abi
        dth = gabi * abr - gabr * abi
        dar_ref[...] = dlr + dphi * dt
        dai_ref[...] = dli + dth * dt
        dldt_ref[...] = jnp.sum(dphi * lr + dth * li, axis=2, keepdims=True) * dt

    return pl.pallas_call(
        body, name="ssm_discretize_bwd",
        out_shape=[_sds(a_re.shape)] * 2 + [_sds(log_dt.shape)] + [_sds(b_re_t.shape)] * 2,
    )(a_re, a_im, log_dt, b_re_t, b_im_t, dabr, dabi, dbbr, dbbi)


def _ssm_power(abr, abi, n_sq):
    def body(r_ref, i_ref, or_ref, oi_ref):
        r, i = r_ref[...], i_ref[...]
        for _ in range(n_sq):
            r, i = r * r - i * i, 2.0 * r * i
        or_ref[...] = r
        oi_ref[...] = i

    return pl.pallas_call(body, name="ssm_power", out_shape=[_sds(abr.shape)] * 2)(abr, abi)


N_CB = SSM_W // LANES
CB_STATES = N_STATE // N_CB
ROWS = N_SEG * SSM_LK


class _Neg:
    def __init__(self, ref):
        self.ref = ref

    def __getitem__(self, idx):
        return -self.ref[idx]


def _seg_init(fin_r, fin_i, pw_r, pw_i, x_r, x_i, reverse):
    zero = jnp.zeros((1, N_STATE), F32)
    cr, ci = zero, zero
    order = range(N_SEG - 1, -1, -1) if reverse else range(N_SEG)
    pr = pw_r[...]
    pi = -pw_i[...] if reverse else pw_i[...]
    for j in order:
        x_r[pl.ds(j, 1), :] = cr
        x_i[pl.ds(j, 1), :] = ci
        fr, fi = fin_r[pl.ds(j, 1), :], fin_i[pl.ds(j, 1), :]
        cr, ci = fr + pr * cr - pi * ci, fi + pr * ci + pi * cr


def _permute_in(src_ref, dst):
    for c in range(N_CB):
        dst[c] = src_ref[:, :, pl.ds(c * LANES, LANES)].reshape(ROWS, LANES)


def _permute_out(src, dst_ref):
    for c in range(N_CB):
        dst_ref[:, :, pl.ds(c * LANES, LANES)] = src[c].reshape(SSM_LK, N_SEG, LANES)


def _scan_block(a_r, a_i, c, b_r, b_i, b_off, x_r, x_i, reverse, acc=None):
    cols = pl.ds(c * CB_STATES, CB_STATES)
    ar = jnp.broadcast_to(a_r[:, cols], (N_SEG, CB_STATES))
    ai = jnp.broadcast_to(a_i[:, cols], (N_SEG, CB_STATES))
    xr, xi = x_r[:, cols], x_i[:, cols]
    if acc is not None:
        sr = jnp.zeros((N_SEG, CB_STATES), F32)
        si = jnp.zeros((N_SEG, CB_STATES), F32)
    for t in range(SSM_LK):
        k = (SSM_LK - 1 - t) if reverse else t
        rows = pl.ds(k * N_SEG + b_off, N_SEG)
        xr, xi = ar * xr - ai * xi + b_r[rows, :], ar * xi + ai * xr + b_i[rows, :]
        b_r[rows, :] = xr
        b_i[rows, :] = xi
        if acc is not None:
            pr, pi = acc[0][pl.ds(k * N_SEG, N_SEG), :], acc[1][pl.ds(k * N_SEG, N_SEG), :]
            sr = sr + xr * pr + xi * pi
            si = si + xi * pr - xr * pi
    x_r[:, cols] = xr
    x_i[:, cols] = xi
    if acc is not None:
        acc[2][:, cols] += sr
        acc[3][:, cols] += si


def _ssm_fwd(u3, abr, abi, pw_r, pw_i, fin_r, fin_i, bb_r, bb_i, cc_r, cc_i, dskip, finals_only):
    sl = u3.shape[0]
    nch = sl // SSM_LK

    def body(u_ref, abr_ref, abi_ref, pwr_ref, pwi_ref, finr_ref, fini_ref, bbr_ref, bbi_ref,
             ccr_ref, cci_ref, d_ref, *rest):
        if finals_only:
            xfr_ref, xfi_ref, up, x_r, x_i = rest[:5]
        else:
            y_ref, xsr_ref, xsi_ref, up, yp, x_r, x_i = rest[:7]
        xs_r, xs_i = rest[-2 * N_CB:-N_CB], rest[-N_CB:]
        k = pl.program_id(0)

        @pl.when(k == 0)
        def _():
            _seg_init(finr_ref, fini_ref, pwr_ref, pwi_ref, x_r, x_i, False)

        if not finals_only:
            xsr_ref[0] = x_r[...]
            xsi_ref[0] = x_i[...]
        _permute_in(u_ref, up)

        def drive(c):
            lhs = up[c].astype(BF16)
            xs_r[c][...] = _dot(lhs, bbr_ref[c])
            xs_i[c][...] = _dot(lhs, bbi_ref[c])

        def readout(c):
            yp[c] = (_dot(xs_r[c][...].astype(BF16), ccr_ref[c]) - _dot(xs_i[c][...].astype(BF16), cci_ref[c])
                     + d_ref[:, pl.ds(c * LANES, LANES)] * up[c])

        drive(0)
        for c in range(N_CB):
            if c + 1 < N_CB:
                drive(c + 1)
            if c >= 1 and not finals_only:
                readout(c - 1)
            _scan_block(abr_ref, abi_ref, c, xs_r[c], xs_i[c], 0, x_r, x_i, False)
        if finals_only:
            @pl.when(k == nch - 1)
            def _():
                xfr_ref[...] = x_r[...]
                xfi_ref[...] = x_i[...]
        else:
            readout(N_CB - 1)
            _permute_out(yp, y_ref)

    ublk = pl.BlockSpec((SSM_LK, N_SEG, SSM_W), lambda k: (k, 0, 0))
    st = pl.BlockSpec((1, N_SEG, N_STATE), lambda k: (k, 0, 0))
    vec = _full((1, N_STATE))
    mat = _full((N_SEG, N_STATE))
    chunk = pltpu.VMEM((N_CB, ROWS, LANES), F32)
    blocks = [pltpu.VMEM((ROWS, CB_STATES), F32)] * (2 * N_CB)
    small = pltpu.VMEM((N_SEG, N_STATE), F32)
    if finals_only:
        out_specs, out_shape = [mat, mat], [_sds((N_SEG, N_STATE))] * 2
        scratch, name = [chunk, small, small] + blocks, "ssm_fwd_finals"
    else:
        out_specs = [ublk, st, st]
        out_shape = [_sds(u3.shape)] + [_sds((nch, N_SEG, N_STATE))] * 2
        scratch, name = [chunk, chunk, small, small] + blocks, "ssm_fwd"
    return pl.pallas_call(
        body, name=name, grid=(nch,),
        in_specs=[ublk, vec, vec, vec, vec, mat, mat,
                  _full((N_CB, LANES, CB_STATES)), _full((N_CB, LANES, CB_STATES)),
                  _full((N_CB, CB_STATES, LANES)), _full((N_CB, CB_STATES, LANES)), _full((1, SSM_W))],
        out_specs=out_specs, out_shape=out_shape, scratch_shapes=scratch,
        compiler_params=_cp(("arbitrary",)),
    )(u3, abr, abi, pw_r, pw_i, fin_r, fin_i, bb_r, bb_i, cc_r, cc_i, dskip)


def _ssm_bwd(u3, dy3, xst_r, xst_i, abr, abi, pw_r, pw_i, fin_r, fin_i, bb_r, bb_i, bbt_r, bbt_i,
             cct_r, cct_i, dskip, finals_only):
    sl = u3.shape[0]
    nch = sl // SSM_LK

    def body(u_ref, g_ref, xsr_ref, xsi_ref, abr_ref, abi_ref, pwr_ref, pwi_ref,
             finr_ref, fini_ref, bbr_ref, bbi_ref, btr_ref, bti_ref, ctr_ref, cti_ref, d_ref, *rest):
        if finals_only:
            lfr_ref, lfi_ref, gp, lam_r, lam_i = rest[:5]
            l_r, l_i = rest[-2 * N_CB:-N_CB], rest[-N_CB:]
        else:
            (du_ref, dar_ref, dai_ref, dbr_ref, dbi_ref, dcr_ref, dci_ref, dd_ref,
             gp, up, yp, lam_r, lam_i, x_r, x_i, sar, sai, sdd) = rest[:18]
            l_r, l_i = rest[18:18 + N_CB], rest[18 + N_CB:18 + 2 * N_CB]
            xx_r, xx_i = rest[18 + 2 * N_CB:18 + 3 * N_CB], rest[18 + 3 * N_CB:]
        t = pl.program_id(0)

        @pl.when(t == 0)
        def _():
            _seg_init(finr_ref, fini_ref, pwr_ref, pwi_ref, lam_r, lam_i, True)
            if not finals_only:
                sar[...] = jnp.zeros_like(sar)
                sai[...] = jnp.zeros_like(sai)
                sdd[...] = jnp.zeros_like(sdd)
                dbr_ref[...] = jnp.zeros_like(dbr_ref)
                dbi_ref[...] = jnp.zeros_like(dbi_ref)
                dcr_ref[...] = jnp.zeros_like(dcr_ref)
                dci_ref[...] = jnp.zeros_like(dci_ref)

        _permute_in(g_ref, gp)
        if not finals_only:
            _permute_in(u_ref, up)
            x_r[...] = xsr_ref[0]
            x_i[...] = xsi_ref[0]

        def drive(c):
            lhs = gp[c].astype(BF16)
            l_r[c][...] = _dot(lhs, ctr_ref[c])
            l_i[c][...] = -_dot(lhs, cti_ref[c])
            if not finals_only:
                cols = pl.ds(c * CB_STATES, CB_STATES)
                xx_r[c][pl.ds(0, N_SEG), :] = x_r[:, cols]
                xx_i[c][pl.ds(0, N_SEG), :] = x_i[:, cols]
                ub = up[c].astype(BF16)
                xx_r[c][pl.ds(N_SEG, ROWS), :] = _dot(ub, bbr_ref[c])
                xx_i[c][pl.ds(N_SEG, ROWS), :] = _dot(ub, bbi_ref[c])

        def collect(c):
            lrb = l_r[c][...].astype(BF16)
            lib = l_i[c][...].astype(BF16)
            ub = up[c].astype(BF16)
            gb = gp[c].astype(BF16)
            dbr_ref[c] += _dot_tn(lrb, ub)
            dbi_ref[c] += _dot_tn(lib, ub)
            dcr_ref[c] += _dot_tn(gb, xx_r[c][pl.ds(N_SEG, ROWS), :].astype(BF16))
            dci_ref[c] += -_dot_tn(gb, xx_i[c][pl.ds(N_SEG, ROWS), :].astype(BF16))
            yp[c] = _dot(lrb, btr_ref[c]) + _dot(lib, bti_ref[c]) + d_ref[:, pl.ds(c * LANES, LANES)] * gp[c]
            prod = gp[c] * up[c]
            sdd[:, pl.ds(c * LANES, LANES)] += jnp.sum(prod.reshape(SSM_LK, N_SEG, LANES), axis=0)

        drive(0)
        for c in range(N_CB):
            if c + 1 < N_CB:
                drive(c + 1)
            if finals_only:
                _scan_block(abr_ref, _Neg(abi_ref), c, l_r[c], l_i[c], 0, lam_r, lam_i, True)
            else:
                if c >= 1:
                    collect(c - 1)
                _scan_block(abr_ref, abi_ref, c, xx_r[c], xx_i[c], N_SEG, x_r, x_i, False)
                _scan_block(abr_ref, _Neg(abi_ref), c, l_r[c], l_i[c], 0, lam_r, lam_i, True,
                            acc=(xx_r[c], xx_i[c], sar, sai))
        if finals_only:
            @pl.when(t == nch - 1)
            def _():
                lfr_ref[...] = lam_r[...]
                lfi_ref[...] = lam_i[...]
        else:
            collect(N_CB - 1)
            _permute_out(yp, du_ref)

            @pl.when(t == nch - 1)
            def _():
                dar_ref[...] = jnp.sum(sar[...], axis=0, keepdims=True)
                dai_ref[...] = jnp.sum(sai[...], axis=0, keepdims=True)
                dd_ref[...] = jnp.sum(sdd[...], axis=0, keepdims=True)

    ublk = pl.BlockSpec((SSM_LK, N_SEG, SSM_W), lambda t: (nch - 1 - t, 0, 0))
    st = pl.BlockSpec((1, N_SEG, N_STATE), lambda t: (nch - 1 - t, 0, 0))
    vec = _full((1, N_STATE))
    mat = _full((N_SEG, N_STATE))
    cs = _full((N_CB, LANES, CB_STATES))
    sc = _full((N_CB, CB_STATES, LANES))
    in_specs = [ublk, ublk, st, st, vec, vec, vec, vec, mat, mat, cs, cs, sc, sc, cs, cs, _full((1, SSM_W))]
    chunk = pltpu.VMEM((N_CB, ROWS, LANES), F32)
    blocks = [pltpu.VMEM((ROWS, CB_STATES), F32)] * (2 * N_CB)
    small = pltpu.VMEM((N_SEG, N_STATE), F32)
    if finals_only:
        out_specs, out_shape = [mat, mat], [_sds((N_SEG, N_STATE))] * 2
        scratch, name = [chunk, small, small] + blocks, "ssm_bwd_finals"
    else:
        out_specs = [ublk, vec, vec, sc, sc, cs, cs, _full((1, SSM_W))]
        out_shape = ([_sds(u3.shape), _sds((1, N_STATE)), _sds((1, N_STATE))]
                     + [_sds((N_CB, CB_STATES, LANES))] * 2 + [_sds((N_CB, LANES, CB_STATES))] * 2
                     + [_sds((1, SSM_W))])
        scratch = ([chunk, chunk, chunk, small, small, small, small, small, small, pltpu.VMEM((N_SEG, SSM_W), F32)]
                   + blocks + [pltpu.VMEM((ROWS + N_SEG, CB_STATES), F32)] * (2 * N_CB))
        name = "ssm_bwd"
    return pl.pallas_call(
        body, name=name, grid=(nch,), in_specs=in_specs, out_specs=out_specs, out_shape=out_shape,
        scratch_shapes=scratch, compiler_params=_cp(("arbitrary",)),
    )(u3, dy3, xst_r, xst_i, abr, abi, pw_r, pw_i, fin_r, fin_i, bb_r, bb_i, bbt_r, bbt_i, cct_r, cct_i, dskip)


def _row(tm, w):
    return pl.BlockSpec((tm, w), lambda i: (i, 0))


def _acc_rows(ref, rows, first):
    @pl.when(first)
    def _():
        ref[...] = jnp.zeros_like(ref)

    ref[...] += jnp.sum(rows, axis=0, keepdims=True)


def _fwd_mix(attn, y, x, glu_w, glu_b, ga, gs, w_out, g2, tm=1024):
    s = x.shape[0]

    def body(a_ref, y_ref, x_ref, gw_ref, gb_ref, ga_ref, gs_ref, wo_ref, g2_ref, mix_ref, x2_ref, h_ref):
        a = a_ref[...]
        anb = ((a * _rms(a)) * ga_ref[...]).astype(BF16)
        z, _ = _gelu(y_ref[...])
        so = z * jax.nn.sigmoid(_dot(z.astype(BF16), gw_ref[...]) + gb_ref[...])
        snb = ((so * _rms(so)) * gs_ref[...]).astype(BF16)
        mix_ref[:, pl.ds(0, ATTN_W)] = anb
        mix_ref[:, pl.ds(ATTN_W, SSM_W)] = snb
        x2 = x_ref[...] + (_dot(anb, wo_ref[pl.ds(0, ATTN_W), :]) + _dot(snb, wo_ref[pl.ds(ATTN_W, SSM_W), :]))
        x2_ref[...] = x2
        h_ref[...] = ((x2 * _rms(x2)) * g2_ref[...]).astype(BF16)

    return pl.pallas_call(
        body, name="fwd_mix", grid=(s // tm,),
        in_specs=[_row(tm, ATTN_W), _row(tm, SSM_W), _row(tm, D_MODEL), _full((SSM_W, SSM_W)), _full((1, SSM_W)),
                  _full((1, ATTN_W)), _full((1, SSM_W)), _full((D_MODEL, D_MODEL)), _full((1, D_MODEL))],
        out_specs=[_row(tm, D_MODEL), _row(tm, D_MODEL), _row(tm, D_MODEL)],
        out_shape=[_sds((s, D_MODEL), BF16), _sds((s, D_MODEL)), _sds((s, D_MODEL), BF16)],
        compiler_params=_cp(("parallel",)),
    )(attn, y, x, glu_w, glu_b, ga, gs, w_out, g2)


def _mlp_up(h, w_up, tm=512, bn=1024):
    s = h.shape[0]

    def body(h_ref, w_ref, r_ref, hdn_ref):
        hv = h_ref[...]
        for j in range(D_FF // bn):
            cols = pl.ds(j * bn, bn)
            r = jnp.maximum(_dot(hv, w_ref[:, cols]), 0.0)
            r_ref[:, cols] = r.astype(BF16)
            hdn_ref[:, cols] = (r * r).astype(BF16)

    return pl.pallas_call(
        body, name="mlp_up", grid=(s // tm,),
        in_specs=[_row(tm, D_MODEL), _full((D_MODEL, D_FF))],
        out_specs=[_row(tm, D_FF), _row(tm, D_FF)], out_shape=[_sds((s, D_FF), BF16)] * 2,
        compiler_params=_cp(("parallel",)),
    )(h, w_up)


def _mlp_down_loss(hdn, w_down, x2, tgt, tm=512):
    s = x2.shape[0]

    def body(hdn_ref, w_ref, x2_ref, t_ref, dy_ref, dyb_ref, sse_ref):
        err = (x2_ref[...] + _dot(hdn_ref[...], w_ref[...])) - t_ref[...]
        dy = err * (1.0 / D_MODEL)
        dy_ref[...] = dy
        dyb_ref[...] = dy.astype(BF16)

        @pl.when(pl.program_id(0) == 0)
        def _():
            sse_ref[...] = jnp.zeros_like(sse_ref)

        sse_ref[...] += jnp.sum(jnp.sum(err * err, axis=0, keepdims=True), axis=1, keepdims=True)

    return pl.pallas_call(
        body, name="mlp_down_loss", grid=(s // tm,),
        in_specs=[_row(tm, D_FF), _full((D_FF, D_MODEL)), _row(tm, D_MODEL), _row(tm, D_MODEL)],
        out_specs=[_row(tm, D_MODEL), _row(tm, D_MODEL), _full((1, 1))],
        out_shape=[_sds((s, D_MODEL)), _sds((s, D_MODEL), BF16), _sds((1, 1))],
        compiler_params=_cp(("arbitrary",)),
    )(hdn, w_down, x2, tgt)


def _mlp_down_bwd(dyb, w_down_t, r, tm=512, bn=1024):
    s = dyb.shape[0]

    def body(dy_ref, w_ref, r_ref, dup_ref):
        dyv = dy_ref[...]
        for j in range(D_FF // bn):
            cols = pl.ds(j * bn, bn)
            dup_ref[:, cols] = (_dot(dyv, w_ref[:, cols]) * (2.0 * r_ref[:, cols].astype(F32))).astype(BF16)

    return pl.pallas_call(
        body, name="mlp_down_bwd", grid=(s // tm,),
        in_specs=[_row(tm, D_MODEL), _full((D_MODEL, D_FF)), _row(tm, D_FF)],
        out_specs=_row(tm, D_FF), out_shape=_sds((s, D_FF), BF16),
        compiler_params=_cp(("parallel",)),
    )(dyb, w_down_t, r)


def _mlp_up_bwd(dup, w_up_t, x2, g2, dy, tm=512):
    s = x2.shape[0]

    def body(dup_ref, w_ref, x2_ref, g2_ref, dy_ref, dx2_ref, dx2b_ref, dg_ref):
        dx, dg_rows = _rms_bwd(_dot(dup_ref[...], w_ref[...]), x2_ref[...], g2_ref[...], D_MODEL)
        dx2 = dy_ref[...] + dx
        dx2_ref[...] = dx2
        dx2b_ref[...] = dx2.astype(BF16)
        _acc_rows(dg_ref, dg_rows, pl.program_id(0) == 0)

    return pl.pallas_call(
        body, name="mlp_up_bwd", grid=(s // tm,),
        in_specs=[_row(tm, D_FF), _full((D_FF, D_MODEL)), _row(tm, D_MODEL), _full((1, D_MODEL)), _row(tm, D_MODEL)],
        out_specs=[_row(tm, D_MODEL), _row(tm, D_MODEL), _full((1, D_MODEL))],
        out_shape=[_sds((s, D_MODEL)), _sds((s, D_MODEL), BF16), _sds((1, D_MODEL))],
        compiler_params=_cp(("arbitrary",)),
    )(dup, w_up_t, x2, g2, dy)


def _mix_bwd(dx2b, w_out_t, attn, y, glu_w, glu_b, glu_w_t, ga, gs, tm=1024, sub=256):
    s = attn.shape[0]

    def body(dx2_ref, wot_ref, a_ref, y_ref, gw_ref, gb_ref, gwt_ref, ga_ref, gs_ref,
             da_ref, dys_ref, z_ref, dpre_ref, dga_ref, dgs_ref, dgb_ref):
        for r0 in range(0, tm, sub):
            rows = pl.ds(r0, sub)
            first = jnp.logical_and(pl.program_id(0) == 0, r0 == 0)
            dmix = _dot(dx2_ref[rows, :], wot_ref[...])
            da, dga_rows = _rms_bwd(dmix[:, :ATTN_W], a_ref[rows, :], ga_ref[...], ATTN_W)
            da_ref[rows, :] = da
            yv = y_ref[rows, :]
            z, t = _gelu(yv)
            gate = jax.nn.sigmoid(_dot(z.astype(BF16), gw_ref[...]) + gb_ref[...])
            dso, dgs_rows = _rms_bwd(dmix[:, ATTN_W:], z * gate, gs_ref[...], SSM_W)
            dpre = dso * z * gate * (1.0 - gate)
            dpre_b = dpre.astype(BF16)
            dz = dso * gate + _dot(dpre_b, gwt_ref[...])
            dgelu = 0.5 * (1.0 + t) + 0.5 * yv * (1.0 - t * t) * (GELU_C * (1.0 + 3.0 * 0.044715 * (yv * yv)))
            dys_ref[rows, :] = dz * dgelu
            z_ref[rows, :] = z.astype(BF16)
            dpre_ref[rows, :] = dpre_b
            _acc_rows(dga_ref, dga_rows, first)
            _acc_rows(dgs_ref, dgs_rows, first)
            _acc_rows(dgb_ref, dpre, first)

    vec = _full((1, SSM_W))
    return pl.pallas_call(
        body, name="mix_bwd", grid=(s // tm,),
        in_specs=[_row(tm, D_MODEL), _full((D_MODEL, D_MODEL)), _row(tm, ATTN_W), _row(tm, SSM_W),
                  _full((SSM_W, SSM_W)), vec, _full((SSM_W, SSM_W)), vec, vec],
        out_specs=[_row(tm, ATTN_W), _row(tm, SSM_W), _row(tm, SSM_W), _row(tm, SSM_W), vec, vec, vec],
        out_shape=[_sds((s, ATTN_W)), _sds((s, SSM_W)), _sds((s, SSM_W), BF16), _sds((s, SSM_W), BF16),
                   _sds((1, ATTN_W)), _sds((1, SSM_W)), _sds((1, SSM_W))],
        compiler_params=_cp(("arbitrary",)),
    )(dx2b, w_out_t, attn, y, glu_w, glu_b, glu_w_t, ga, gs)


def _qk_bwd(dqn, dkn, qk, dv, du, gq, gk, gmat, tm=1024):
    s = qk.shape[0]

    def body(dq_ref, dk_ref, qk_ref, dv_ref, du_ref, gq_ref, gk_ref, gm_ref, dp_ref, dgq_ref, dgk_ref):
        first = pl.program_id(0) == 0
        gm = gm_ref[...]
        for idx, (d_ref, g_ref, dg_ref) in enumerate(((dq_ref, gq_ref, dgq_ref), (dk_ref, gk_ref, dgk_ref))):
            xv = qk_ref[:, pl.ds(idx * ATTN_W, ATTN_W)]
            dyv = d_ref[...]
            r = lax.rsqrt(_group_mean(xv * xv, gm) + EPS)
            gdy = dyv * g_ref[...]
            dx = r * gdy - xv * (r * r * r) * _group_mean(gdy * xv, gm)
            dp_ref[:, pl.ds(idx * ATTN_W, ATTN_W)] = dx.astype(BF16)
            _acc_rows(dg_ref, dyv * (xv * r), first)
        dp_ref[:, pl.ds(2 * ATTN_W, ATTN_W)] = dv_ref[...].astype(BF16)
        dp_ref[:, pl.ds(3 * ATTN_W, SSM_W)] = du_ref[...].astype(BF16)

    vec = _full((1, ATTN_W))
    return pl.pallas_call(
        body, name="qk_bwd", grid=(s // tm,),
        in_specs=[_row(tm, ATTN_W), _row(tm, ATTN_W), _row(tm, 2 * ATTN_W), _row(tm, ATTN_W), _row(tm, SSM_W),
                  vec, vec, _full((MXU_W, MXU_W))],
        out_specs=[_row(tm, 4 * ATTN_W), vec, vec],
        out_shape=[_sds((s, 4 * ATTN_W), BF16), _sds((1, ATTN_W)), _sds((1, ATTN_W))],
        compiler_params=_cp(("arbitrary",)),
    )(dqn, dkn, qk, dv, du, gq, gk, gmat)


def _in_bwd(dproj, w_in_t, x, g1, dx2, ex=None, tm=1024):
    s = x.shape[0]
    steps = s // tm

    def body(*refs):
        (dp_ref, w_ref, x_ref, g1_ref, dx2_ref), (gx_ref, dg_ref), _, hx = _carry(ex, refs, 5, 2)
        if ex is not None:
            @pl.when(pl.program_id(0) == 0)
            def _():
                ex.start(*hx)

        dx, dg_rows = _rms_bwd(_dot(dp_ref[...], w_ref[...]), x_ref[...], g1_ref[...], D_MODEL)
        gx_ref[...] = dx2_ref[...] + dx
        _acc_rows(dg_ref, dg_rows, pl.program_id(0) == 0)
        if ex is not None:
            @pl.when(pl.program_id(0) == steps - 1)
            def _():
                ex.wait(*hx)

    hosted = ex is not None
    return pl.pallas_call(
        body, name="in_bwd", grid=(steps,),
        in_specs=[_row(tm, 4 * ATTN_W), _full((4 * ATTN_W, D_MODEL)), _row(tm, D_MODEL), _full((1, D_MODEL)),
                  _row(tm, D_MODEL)] + (ex.specs() if hosted else []),
        out_specs=[_row(tm, D_MODEL), _full((1, D_MODEL))] + (ex.specs() if hosted else []),
        out_shape=[_sds((s, D_MODEL)), _sds((1, D_MODEL))] + (ex.out_shape() if hosted else []),
        scratch_shapes=ex.scratch() if hosted else [],
        compiler_params=_cp(("arbitrary",)),
    )(dproj, w_in_t, x, g1, dx2, *(ex.srcs if hosted else []))


def _mm_tn(a, b, name, ts=2048):
    s, k = a.shape
    n = b.shape[1]
    bk, bn = min(k, 1024), min(n, 1024)
    steps = s // ts

    def body(a_ref, b_ref, o_ref, acc):
        t = pl.program_id(2)

        @pl.when(t == 0)
        def _():
            acc[...] = jnp.zeros_like(acc)

        acc[...] += _dot_tn(a_ref[...], b_ref[...])

        @pl.when(t == steps - 1)
        def _():
            o_ref[...] = acc[...].astype(BF16)

    return pl.pallas_call(
        body, name=name, grid=(k // bk, n // bn, steps),
        in_specs=[pl.BlockSpec((ts, bk), lambda i, j, t: (t, i)), pl.BlockSpec((ts, bn), lambda i, j, t: (t, j))],
        out_specs=pl.BlockSpec((bk, bn), lambda i, j, t: (i, j)), out_shape=_sds((k, n), BF16),
        scratch_shapes=[pltpu.VMEM((bk, bn), F32)],
        compiler_params=_cp(("parallel", "parallel", "arbitrary")),
    )(a, b)


def _peer(k):
    x, y, c = lax.axis_index("x"), lax.axis_index("y"), lax.axis_index("c")
    px = 1 - x if k & 4 else x
    py = 1 - y if k & 2 else y
    pc = 1 - c if k & 1 else c
    return (px, py, pc), 4 * px + 2 * py + pc


def _gather_rows(x_shard):
    m_per, n = x_shard.shape

    def body(x_ref, out_ref, send_sems, recv_sems, local_sem):
        x, y, c = lax.axis_index("x"), lax.axis_index("y"), lax.axis_index("c")
        me, sibling = (x, y, c), (x, y, 1 - c)
        chips = [(1 - x, y), (x, 1 - y), (1 - x, 1 - y)]

        def rows(px, py, pc):
            return out_ref.at[pl.ds((4 * px + 2 * py + pc) * m_per, m_per), :]

        def copy(k, block, to, src=None):
            return pltpu.make_async_remote_copy(
                src_ref=rows(*block) if src is None else src, dst_ref=rows(*block),
                send_sem=send_sems.at[k], recv_sem=recv_sems.at[k], device_id=to, device_id_type=MESH)

        mine = pltpu.make_async_copy(x_ref, rows(*me), local_sem)
        mine.start()
        first = [copy(0, me, sibling, src=x_ref)]
        first += [copy(1 + j, me, (*chip, c), src=x_ref) for j, chip in enumerate(chips)]
        for cp in first:
            cp.start()
        passed = [copy(4 + j, (*chip, c), sibling) for j, chip in enumerate(chips)]
        for j, chip in enumerate(chips):
            copy(1 + j, (*chip, c), me).wait_recv()
            passed[j].start()
        copy(0, sibling, me).wait_recv()
        for j, chip in enumerate(chips):
            copy(4 + j, (*chip, 1 - c), me).wait_recv()
        for cp in first + passed:
            cp.wait_send()
        mine.wait()

    return pl.pallas_call(
        body, name="gather_weights", out_shape=_sds((N_DEV * m_per, n), x_shard.dtype),
        in_specs=[pl.BlockSpec(memory_space=pltpu.VMEM)], out_specs=pl.BlockSpec(memory_space=pltpu.VMEM),
        scratch_shapes=[pltpu.SemaphoreType.DMA((7,)), pltpu.SemaphoreType.DMA((7,)), pltpu.SemaphoreType.DMA],
        compiler_params=pltpu.CompilerParams(vmem_limit_bytes=VMEM_LIMIT),
    )(x_shard)


class _Exchange:
    def __init__(self, srcs, whole):
        self.srcs, self.whole, self.n = list(srcs), list(whole), len(srcs)
        self.rows = [a.shape[0] // N_DEV if w is False else a.shape[0] for a, w in zip(self.srcs, self.whole)]
        self.cols = [a.shape[1] // N_DEV if w == "col" else a.shape[1] for a, w in zip(self.srcs, self.whole)]

    def specs(self):
        return [pl.BlockSpec(memory_space=pl.ANY)] * self.n

    def out_shape(self):
        return [_sds((N_DEV, r, c), a.dtype) for r, c, a in zip(self.rows, self.cols, self.srcs)]

    def scratch(self):
        return [pltpu.SemaphoreType.DMA((self.n * 7,)), pltpu.SemaphoreType.DMA((self.n * 7,)),
                pltpu.SemaphoreType.DMA((self.n,))]

    def _copies(self, ins, outs, sems):
        send_sems, recv_sems, local_sems = sems
        _, me = _peer(0)
        for w in range(self.n):
            for k in range(N_DEV):
                peer, pidx = _peer(k)
                if self.whole[w] is True:
                    src = ins[w]
                elif self.whole[w] == "col":
                    src = ins[w].at[:, pl.ds(pl.multiple_of(pidx * self.cols[w], self.cols[w]), self.cols[w])]
                else:
                    src = ins[w].at[pl.ds(pidx * self.rows[w], self.rows[w]), :]
                if k == 0:
                    yield k, pltpu.make_async_copy(src, outs[w].at[me], local_sems.at[w]), None
                else:
                    sem = w * 7 + k - 1
                    out = pltpu.make_async_remote_copy(src_ref=src, dst_ref=outs[w].at[me], send_sem=send_sems.at[sem],
                                                       recv_sem=recv_sems.at[sem], device_id=peer, device_id_type=MESH)
                    back = pltpu.make_async_remote_copy(src_ref=src, dst_ref=outs[w].at[pidx], send_sem=send_sems.at[sem],
                                                        recv_sem=recv_sems.at[sem], device_id=peer, device_id_type=MESH)
                    yield k, out, back

    def start(self, ins, outs, sems):
        for _, out, _ in self._copies(ins, outs, sems):
            out.start()

    def wait(self, ins, outs, sems):
        for k, out, back in self._copies(ins, outs, sems):
            if k == 0:
                out.wait()
            else:
                back.wait_recv()
                out.wait_send()


def _carry(ex, refs, n_in, n_out):
    nh = ex.n if ex is not None else 0
    ins, hin = refs[:n_in], refs[n_in:n_in + nh]
    outs = refs[n_in + nh:n_in + nh + n_out]
    hout = refs[n_in + nh + n_out:n_in + 2 * nh + n_out]
    rest = refs[n_in + 2 * nh + n_out:]
    if ex is None:
        return ins, outs, rest, None
    return ins, outs, rest[:-3], (hin, hout, rest[-3:])


def _exchange_now(srcs, whole, name):
    ex = _Exchange(srcs, whole)

    def body(*refs):
        _, _, _, (hin, hout, sems) = _carry(ex, refs, 0, 0)
        ex.start(hin, hout, sems)
        ex.wait(hin, hout, sems)

    return pl.pallas_call(body, name=name, out_shape=ex.out_shape(), in_specs=ex.specs(), out_specs=ex.specs(),
                          scratch_shapes=ex.scratch())(*srcs)


def _split_exchange_start(src):
    rows, width = src.shape[0], src.shape[1] // N_DEV

    def body(src_ref, land_ref, send_sems, recv_sems, src_thru, land_thru, token):
        barrier = pltpu.get_barrier_semaphore()
        for k in range(1, N_DEV):
            pl.semaphore_signal(barrier, inc=1, device_id=_peer(k)[0], device_id_type=MESH)
        pl.semaphore_wait(barrier, N_DEV - 1)
        _, me = _peer(0)
        for k in range(1, N_DEV):
            peer, pidx = _peer(k)
            pltpu.make_async_remote_copy(
                src_ref=src_ref.at[:, pl.ds(pl.multiple_of(pidx * width, width), width)], dst_ref=land_ref.at[me],
                send_sem=send_sems.at[k - 1], recv_sem=recv_sems.at[k - 1], device_id=peer, device_id_type=MESH).start()
        token[...] = jnp.zeros_like(token)

    hbm = pl.BlockSpec(memory_space=pltpu.HBM)
    sem = pl.BlockSpec(memory_space=pltpu.SEMAPHORE)
    land = lax.empty((N_DEV, rows, width), src.dtype)
    return pl.pallas_call(
        body, name="w_in_exchange_start",
        out_shape=(pltpu.SemaphoreType.DMA((N_DEV - 1,)), pltpu.SemaphoreType.DMA((N_DEV - 1,)),
                   pltpu.HBM(src.shape, src.dtype), pltpu.HBM(land.shape, land.dtype), _sds((8, LANES))),
        in_specs=(hbm, hbm), out_specs=(sem, sem, hbm, hbm, pl.BlockSpec(memory_space=pltpu.VMEM)),
        input_output_aliases={0: 2, 1: 3},
        compiler_params=pltpu.CompilerParams(has_side_effects=pltpu.SideEffectType.DATAFLOW_SIDE_EFFECTING, collective_id=0),
    )(pltpu.with_memory_space_constraint(src, pltpu.HBM), pltpu.with_memory_space_constraint(land, pltpu.HBM))


def _split_exchange_wait(send_sems, recv_sems, src_thru, land_thru, after):
    width = land_thru.shape[2]

    def body(src_ref, land_ref, send_sems, recv_sems, after_ref, src_dead, got_ref):
        _, me = _peer(0)
        for k in range(1, N_DEV):
            peer, pidx = _peer(k)
            copy = pltpu.make_async_remote_copy(
                src_ref=src_ref.at[:, pl.ds(pl.multiple_of(pidx * width, width), width)], dst_ref=land_ref.at[pidx],
                send_sem=send_sems.at[k - 1], recv_sem=recv_sems.at[k - 1], device_id=peer, device_id_type=MESH)
            copy.wait_send()
            copy.wait_recv()

    hbm = pl.BlockSpec(memory_space=pltpu.HBM)
    sem = pl.BlockSpec(memory_space=pltpu.SEMAPHORE)
    return pl.pallas_call(
        body, name="w_in_exchange_wait",
        out_shape=(pltpu.HBM(src_thru.shape, src_thru.dtype), pltpu.HBM(land_thru.shape, land_thru.dtype)),
        in_specs=(hbm, hbm, sem, sem, pl.BlockSpec(memory_space=pl.ANY)), out_specs=(hbm, hbm),
        input_output_aliases={0: 0, 1: 1},
        compiler_params=pltpu.CompilerParams(has_side_effects=pltpu.SideEffectType.DATAFLOW_SIDE_EFFECTING),
    )(src_thru, land_thru, send_sems, recv_sems, after)


def _adamw(w, m, v, gparts, name):
    r, c = w.shape
    tr = r if r * c <= 256 * 1024 else 128 * 1024 // c

    def body(w_ref, m_ref, v_ref, g_ref, go_ref, d_ref, mo_ref, vo_ref):
        g = g_ref[0].astype(F32)
        for i in range(1, N_DEV):
            g = g + g_ref[i].astype(F32)
        go_ref[...] = g
        d_ref[...], mo_ref[...], vo_ref[...] = _adamw_step(w_ref[...], m_ref[...], v_ref[...], g)

    blk = pl.BlockSpec((tr, c), lambda i: (i, 0))
    return pl.pallas_call(
        body, name=name, grid=(r // tr,),
        in_specs=[blk, blk, blk, pl.BlockSpec((N_DEV, tr, c), lambda i: (0, i, 0))],
        out_specs=[blk] * 4, out_shape=[_sds((r, c))] * 4,
        compiler_params=_cp(("parallel",)),
    )(w, m, v, gparts)


def _adamw_step(w, m, v, g):
    nm = ADAM_B1 * m + (1.0 - ADAM_B1) * g
    nv = ADAM_B2 * v + (1.0 - ADAM_B2) * (g * g)
    m_hat = nm / (1.0 - ADAM_B1 ** ADAM_STEP)
    v_hat = nv / (1.0 - ADAM_B2 ** ADAM_STEP)
    return -ADAM_LR * (m_hat / (jnp.sqrt(v_hat) + ADAM_EPS) + ADAM_WD * w), nm, nv


def _sum_slots(parts):
    def body(p_ref, o_ref):
        g = p_ref[0]
        for i in range(1, N_DEV):
            g = g + p_ref[i]
        o_ref[...] = g

    return pl.pallas_call(body, name="sum_small_grads", out_shape=_sds(parts.shape[1:]))(parts)


def _adamw_native(ws, ms, vs, gs):
    n = len(ws)

    def body(*refs):
        for i in range(n):
            w_ref, m_ref, v_ref, g_ref = refs[i], refs[n + i], refs[2 * n + i], refs[3 * n + i]
            d, nm, nv = _adamw_step(w_ref[...], m_ref[...], v_ref[...], g_ref[...])
            refs[4 * n + i][...] = d
            refs[5 * n + i][...] = nm
            refs[6 * n + i][...] = nv

    outs = pl.pallas_call(body, name="adamw_small", out_shape=[_sds(w.shape) for w in ws] * 3,
                          compiler_params=pltpu.CompilerParams(vmem_limit_bytes=VMEM_LIMIT))(*ws, *ms, *vs, *gs)
    return outs[:n], outs[n:2 * n], outs[2 * n:]


def _block_diag(a, states_first):
    a4 = a.reshape(N_CB, 8, SSM_GROUP, SSM_STATE)
    eye = jnp.eye(8, dtype=a.dtype)
    if states_first:
        return jnp.einsum("bgcp,gh->bgphc", a4, eye).reshape(N_CB, CB_STATES, LANES)
    return jnp.einsum("bgcp,gh->bgchp", a4, eye).reshape(N_CB, LANES, CB_STATES)


def _block_diag_of(full, states_first):
    if states_first:
        picked = jnp.einsum("bgphc,gh->bgcp", full.reshape(N_CB, 8, SSM_STATE, 8, SSM_GROUP), jnp.eye(8, dtype=full.dtype))
    else:
        picked = jnp.einsum("bgchp,gh->bgcp", full.reshape(N_CB, 8, SSM_GROUP, 8, SSM_STATE), jnp.eye(8, dtype=full.dtype))
    return picked.reshape(SSM_GROUPS, SSM_GROUP, SSM_STATE)


SMALL_EARLY = ("ssm_a_re", "ssm_a_im", "ssm_log_dt", "ssm_b_re", "ssm_b_im", "ssm_c_re", "ssm_c_im", "ssm_d", "glu_b",
               "attn_out_norm_g", "ssm_out_norm_g", "norm2_g")
SMALL_MID = ("q_norm_g", "k_norm_g")
SMALL_LATE = ("norm1_g",)
SMALL = SMALL_EARLY + SMALL_MID + SMALL_LATE


def _pack_small(arrs):
    parts = []
    for a in arrs:
        flat = a.reshape(-1)
        rows = -(-flat.shape[0] // (8 * LANES)) * 8
        parts.append(jnp.pad(flat, (0, rows * LANES - flat.shape[0])).reshape(rows, LANES))
    return jnp.concatenate(parts, axis=0)


def _unpack_small(packed, shapes):
    out, r0 = [], 0
    for shp in shapes:
        size = math.prod(shp)
        rows = -(-size // (8 * LANES)) * 8
        out.append(packed[r0:r0 + rows].reshape(-1)[:size].reshape(shp))
        r0 += rows
    return out


def kernel(x, norm1_g, w_in, q_norm_g, k_norm_g, ssm_a_re, ssm_a_im, ssm_log_dt, ssm_b_re, ssm_b_im, ssm_c_re, ssm_c_im, ssm_d, glu_w, glu_b, attn_out_norm_g, ssm_out_norm_g, w_out, norm2_g, w_mlp_up, w_mlp_down, loss_target, m_norm1_g, m_w_in, m_q_norm_g, m_k_norm_g, m_ssm_a_re, m_ssm_a_im, m_ssm_log_dt, m_ssm_b_re, m_ssm_b_im, m_ssm_c_re, m_ssm_c_im, m_ssm_d, m_glu_w, m_glu_b, m_attn_out_norm_g, m_ssm_out_norm_g, m_w_out, m_norm2_g, m_w_mlp_up, m_w_mlp_down, v_norm1_g, v_w_in, v_q_norm_g, v_k_norm_g, v_ssm_a_re, v_ssm_a_im, v_ssm_log_dt, v_ssm_b_re, v_ssm_b_im, v_ssm_c_re, v_ssm_c_im, v_ssm_d, v_glu_w, v_glu_b, v_attn_out_norm_g, v_ssm_out_norm_g, v_w_out, v_norm2_g, v_w_mlp_up, v_w_mlp_down):
    weights = dict(norm1_g=norm1_g, w_in=w_in, q_norm_g=q_norm_g, k_norm_g=k_norm_g, ssm_a_re=ssm_a_re,
                   ssm_a_im=ssm_a_im, ssm_log_dt=ssm_log_dt, ssm_b_re=ssm_b_re, ssm_b_im=ssm_b_im,
                   ssm_c_re=ssm_c_re, ssm_c_im=ssm_c_im, ssm_d=ssm_d, glu_w=glu_w, glu_b=glu_b,
                   attn_out_norm_g=attn_out_norm_g, ssm_out_norm_g=ssm_out_norm_g, w_out=w_out, norm2_g=norm2_g,
                   w_mlp_up=w_mlp_up, w_mlp_down=w_mlp_down)
    mom_m = dict(norm1_g=m_norm1_g, w_in=m_w_in, q_norm_g=m_q_norm_g, k_norm_g=m_k_norm_g, ssm_a_re=m_ssm_a_re,
                 ssm_a_im=m_ssm_a_im, ssm_log_dt=m_ssm_log_dt, ssm_b_re=m_ssm_b_re, ssm_b_im=m_ssm_b_im,
                 ssm_c_re=m_ssm_c_re, ssm_c_im=m_ssm_c_im, ssm_d=m_ssm_d, glu_w=m_glu_w, glu_b=m_glu_b,
                 attn_out_norm_g=m_attn_out_norm_g, ssm_out_norm_g=m_ssm_out_norm_g, w_out=m_w_out,
                 norm2_g=m_norm2_g, w_mlp_up=m_w_mlp_up, w_mlp_down=m_w_mlp_down)
    mom_v = dict(norm1_g=v_norm1_g, w_in=v_w_in, q_norm_g=v_q_norm_g, k_norm_g=v_k_norm_g, ssm_a_re=v_ssm_a_re,
                 ssm_a_im=v_ssm_a_im, ssm_log_dt=v_ssm_log_dt, ssm_b_re=v_ssm_b_re, ssm_b_im=v_ssm_b_im,
                 ssm_c_re=v_ssm_c_re, ssm_c_im=v_ssm_c_im, ssm_d=v_ssm_d, glu_w=v_glu_w, glu_b=v_glu_b,
                 attn_out_norm_g=v_attn_out_norm_g, ssm_out_norm_g=v_ssm_out_norm_g, w_out=v_w_out,
                 norm2_g=v_norm2_g, w_mlp_up=v_w_mlp_up, w_mlp_down=v_w_mlp_down)
    order = list(weights)

    xs, tgt = x[0], loss_target[0]
    s = xs.shape[0]
    assert s % ATTN_CHUNK == 0 and (s // N_SEG) % SSM_LK == 0
    seg_len = s // N_SEG
    n_sq = seg_len.bit_length() - 1
    assert 1 << n_sq == seg_len

    w_in_t = _gather_rows(w_in[0].T.astype(BF16))
    w_in_full = w_in_t.T
    later = _Exchange([glu_w[0].astype(BF16), w_out[0].astype(BF16), w_mlp_up[0].T.astype(BF16),
                       w_mlp_down[0].astype(BF16)], [True] * 4)

    gq = jnp.tile(q_norm_g[0], ATTN_W // HEAD_DIM)[None]
    gk = jnp.tile(k_norm_g[0], ATTN_W // HEAD_DIM)[None]
    lane = jnp.arange(MXU_W) // HEAD_DIM
    gmat = jnp.where(lane[:, None] == lane[None, :], 1.0 / HEAD_DIM, 0.0).astype(BF16)
    a_re3 = ssm_a_re[0][:, None, :]
    a_im3 = ssm_a_im[0][:, None, :]
    ldt3 = ssm_log_dt[0][:, None, None]
    b_re_t = jnp.swapaxes(ssm_b_re[0], 1, 2)
    b_im_t = jnp.swapaxes(ssm_b_im[0], 1, 2)
    c_re, c_im = ssm_c_re[0], ssm_c_im[0]
    dskip = ssm_d[0].reshape(1, SSM_W)

    qk, qn, kn, v, u, xn = _fwd_proj(xs, norm1_g, w_in_full, gq, gk, gmat)
    attn, lse, glu_g, w_out_g, w_up_g, w_down_g = _attn_fwd(qn, kn, v, later)
    glu_full = glu_g.reshape(SSM_W, SSM_W)
    w_out_full = w_out_g.reshape(D_MODEL, D_MODEL)
    w_up_t = w_up_g.reshape(D_FF, D_MODEL)
    w_down_full = w_down_g.reshape(D_FF, D_MODEL)
    w_up_full, w_out_t, w_down_t, glu_t = w_up_t.T, w_out_full.T, w_down_full.T, glu_full.T

    abr3, abi3, bbr, bbi = _ssm_discretize(a_re3, a_im3, ldt3, b_re_t, b_im_t)
    abr, abi = abr3.reshape(1, N_STATE), abi3.reshape(1, N_STATE)
    pw_r, pw_i = _ssm_power(abr, abi, n_sq)
    bb_r, bb_i = _block_diag(bbr, False).astype(BF16), _block_diag(bbi, False).astype(BF16)
    bbt_r, bbt_i = _block_diag(bbr, True).astype(BF16), _block_diag(bbi, True).astype(BF16)
    cc_r, cc_i = _block_diag(c_re, True).astype(BF16), _block_diag(c_im, True).astype(BF16)
    cct_r, cct_i = _block_diag(c_re, False).astype(BF16), _block_diag(c_im, False).astype(BF16)
    seg_major = lambda a: jnp.swapaxes(a.reshape(N_SEG, seg_len, SSM_W), 0, 1)
    seg_minor = lambda a: jnp.swapaxes(a, 0, 1).reshape(s, SSM_W)
    u3 = seg_major(u)
    zero_fin = jnp.zeros((N_SEG, N_STATE), F32)
    ssm_args = (abr, abi, pw_r, pw_i)
    xf_r, xf_i = _ssm_fwd(u3, *ssm_args, zero_fin, zero_fin, bb_r, bb_i, cc_r, cc_i, dskip, True)
    y3, xst_r, xst_i = _ssm_fwd(u3, *ssm_args, xf_r, xf_i, bb_r, bb_i, cc_r, cc_i, dskip, False)
    y = seg_minor(y3)

    mix, x2, h = _fwd_mix(attn, y, xs, glu_full, glu_b, attn_out_norm_g, ssm_out_norm_g, w_out_full, norm2_g)
    r_act, hdn = _mlp_up(h, w_up_full)
    dy, dyb, sse = _mlp_down_loss(hdn, w_down_full, x2, tgt)

    dup = _mlp_down_bwd(dyb, w_down_t, r_act)
    g_w_down = _mm_tn(hdn, dyb, "grad_w_down")
    dx2, dx2b, g_norm2 = _mlp_up_bwd(dup, w_up_t, x2, norm2_g, dy)
    g_w_up = _mm_tn(h, dup, "grad_w_up")
    dattn, dys, z_b, dpre_b, g_ga, g_gs, g_glu_b = _mix_bwd(dx2b, w_out_t, attn, y, glu_full, glu_b, glu_t,
                                                             attn_out_norm_g, ssm_out_norm_g)
    g_w_out = _mm_tn(mix, dx2b, "grad_w_out")
    g_glu_w = _mm_tn(z_b, dpre_b, "grad_glu_w")
    dy3 = seg_major(dys)
    bwd_args = (u3, dy3, xst_r, xst_i, abr, abi, pw_r, pw_i)
    lf_r, lf_i = _ssm_bwd(*bwd_args, zero_fin, zero_fin, bb_r, bb_i, bbt_r, bbt_i, cct_r, cct_i, dskip, True)
    du3, dab_r, dab_i, dbb_r, dbb_i, dcc_r, dcc_i, g_d = _ssm_bwd(*bwd_args, lf_r, lf_i, bb_r, bb_i, bbt_r, bbt_i,
                                                                 cct_r, cct_i, dskip, False)
    g_a_re3, g_a_im3, g_ldt3, g_b_re_t, g_b_im_t = _ssm_discretize_bwd(
        a_re3, a_im3, ldt3, b_re_t, b_im_t, dab_r.reshape(a_re3.shape), dab_i.reshape(a_re3.shape),
        _block_diag_of(dbb_r, True), _block_diag_of(dbb_i, True))
    g_c_re, g_c_im = _block_diag_of(dcc_r, False), _block_diag_of(dcc_i, False)
    small_grads = dict(
        ssm_a_re=g_a_re3.reshape(ssm_a_re.shape), ssm_a_im=g_a_im3.reshape(ssm_a_im.shape),
        ssm_log_dt=g_ldt3.reshape(ssm_log_dt.shape), ssm_b_re=jnp.swapaxes(g_b_re_t, 1, 2)[None],
        ssm_b_im=jnp.swapaxes(g_b_im_t, 1, 2)[None], ssm_c_re=g_c_re[None], ssm_c_im=g_c_im[None],
        ssm_d=g_d.reshape(ssm_d.shape), glu_b=g_glu_b, attn_out_norm_g=g_ga, ssm_out_norm_g=g_gs, norm2_g=g_norm2)

    early = _Exchange([g_glu_w, g_w_out, g_w_up, g_w_down, _pack_small([small_grads[n] for n in SMALL_EARLY] + [sse])],
                      [False, False, "col", False, True])
    dqn, dkn, dv, p_glu, p_w_out, p_w_up, p_w_down, p_early = _attn_bwd(qn, kn, v, attn, lse, dattn, early)

    dproj, g_gq, g_gk = _qk_bwd(dqn, dkn, qk, dv, seg_minor(du3), gq, gk, gmat)
    g_w_in = _mm_tn(xn, dproj, "grad_w_in")
    small_grads["q_norm_g"] = g_gq.reshape(ATTN_W // HEAD_DIM, HEAD_DIM).sum(0)[None]
    small_grads["k_norm_g"] = g_gk.reshape(ATTN_W // HEAD_DIM, HEAD_DIM).sum(0)[None]
    send_sems, recv_sems, g_thru, land, token = _split_exchange_start(g_w_in)
    grad_x, g_norm1 = _in_bwd(dproj, w_in_t, xs, norm1_g + token[:1, :1], dx2)
    g_w_in, landed = _split_exchange_wait(send_sems, recv_sems, g_thru, land, g_norm1)
    me = 4 * lax.axis_index("x") + 2 * lax.axis_index("y") + lax.axis_index("c")
    shard_w = g_w_in.shape[1] // N_DEV
    own = lax.dynamic_slice(g_w_in, (0, me * shard_w), (g_w_in.shape[0], shard_w))
    p_w_in = lax.dynamic_update_slice(landed, own[None], (me, 0, 0))
    p_mid, p_late = _exchange_now([_pack_small([small_grads[n] for n in SMALL_MID]), _pack_small([g_norm1])],
                                  [True, True], "exchange_tail")

    res = {}
    for name, gp in (("w_in", p_w_in), ("glu_w", p_glu), ("w_out", p_w_out), ("w_mlp_up", p_w_up), ("w_mlp_down", p_w_down)):
        outs = _adamw(weights[name][0], mom_m[name][0], mom_v[name][0], gp, "adamw_" + name)
        res[name] = [o[None] for o in outs]
    shapes = [weights[n].shape for n in SMALL_EARLY] + [sse.shape] + [weights[n].shape for n in SMALL_MID + SMALL_LATE]
    g_small = _unpack_small(_sum_slots(jnp.concatenate([p_early, p_mid, p_late], axis=1)), shapes)
    loss = 0.5 * g_small.pop(len(SMALL_EARLY))[0, 0] / D_MODEL
    d_small, m_small, v_small = _adamw_native([weights[n] for n in SMALL], [mom_m[n] for n in SMALL],
                                              [mom_v[n] for n in SMALL], g_small)
    for i, n in enumerate(SMALL):
        res[n] = [g_small[i], d_small[i], m_small[i], v_small[i]]

    return (loss, grad_x[None], *[res[n][0] for n in order], *[res[n][1] for n in order],
            *[res[n][2] for n in order], *[res[n][3] for n in order])
```

```python
import math

import jax
import jax.numpy as jnp
from jax import lax
from jax.experimental import pallas as pl
from jax.experimental.pallas import tpu as pltpu

F32 = jnp.float32
BF16 = jnp.bfloat16

D_MODEL = 1024
ATTN_W = 512
HEAD_DIM = 64
SSM_W = 512
SSM_GROUP = 16
SSM_GROUPS = 32
SSM_STATE = 64
N_STATE = SSM_GROUPS * SSM_STATE
D_FF = 4096
EPS = 1e-6
NEG_INF = -1e30
ATTN_CHUNK = 2048
ATTN_BLOCK = 128
DILATIONS = (1, 4, 16)
N_SEG = 8
SSM_LK = 64
N_DEV = 8
LANES = 128
MXU_W = 256

ADAM_LR = 0.001
ADAM_B1 = 0.9
ADAM_B2 = 0.999
ADAM_EPS = 1e-08
ADAM_WD = 0.01
ADAM_STEP = 10

VMEM_LIMIT = 56 * 1024 * 1024
GELU_C = math.sqrt(2.0 / math.pi)
MESH = pl.DeviceIdType.MESH


def _cp(sem, vmem=VMEM_LIMIT):
    return pltpu.CompilerParams(dimension_semantics=sem, vmem_limit_bytes=vmem)


def _dot(a, b):
    return jnp.dot(a, b, preferred_element_type=F32)


def _dot_tn(a, b):
    return lax.dot_general(a, b, (((0,), (0,)), ((), ())), preferred_element_type=F32)


def _group_mean(x2, gmat):
    hi = x2.astype(BF16)
    lo = (x2 - hi.astype(F32)).astype(BF16)
    w = gmat.shape[0]
    return jnp.concatenate([_dot(hi[:, c:c + w], gmat) + _dot(lo[:, c:c + w], gmat) for c in range(0, x2.shape[1], w)],
                           axis=1)


def _rms(x):
    return lax.rsqrt(jnp.mean(x * x, axis=-1, keepdims=True) + EPS)


def _rms_bwd(dy, x, g, n):
    r = _rms(x)
    gdy = dy * g
    dx = r * gdy - x * (r * r * r) * (jnp.sum(gdy * x, axis=-1, keepdims=True) / n)
    return dx, dy * (x * r)


def _gelu(y):
    t = jnp.tanh(GELU_C * (y + 0.044715 * (y * y * y)))
    return 0.5 * y * (1.0 + t), t


def _full(shape):
    nd = len(shape)
    return pl.BlockSpec(shape, lambda *_: (0,) * nd)


def _sds(shape, dtype=F32):
    return jax.ShapeDtypeStruct(shape, dtype)


def _fwd_proj(x, g1, w_in, gq, gk, gmat, tm=1024):
    s = x.shape[0]

    def body(x_ref, g1_ref, w_ref, gq_ref, gk_ref, gm_ref, qk_ref, qn_ref, kn_ref, v_ref, u_ref, xn_ref):
        xv = x_ref[...]
        xnb = ((xv * _rms(xv)) * g1_ref[...]).astype(BF16)
        xn_ref[...] = xnb
        proj = _dot(xnb, w_ref[...])
        q = proj[:, :ATTN_W]
        k = proj[:, ATTN_W:2 * ATTN_W]
        qk_ref[...] = proj[:, :2 * ATTN_W]
        v_ref[...] = proj[:, 2 * ATTN_W:3 * ATTN_W]
        u_ref[...] = proj[:, 3 * ATTN_W:]
        gm = gm_ref[...]
        qn_ref[...] = (q * lax.rsqrt(_group_mean(q * q, gm) + EPS)) * gq_ref[...]
        kn_ref[...] = (k * lax.rsqrt(_group_mean(k * k, gm) + EPS)) * gk_ref[...]

    row = lambda w: pl.BlockSpec((tm, w), lambda i: (i, 0))
    return pl.pallas_call(
        body, name="fwd_proj", grid=(s // tm,),
        in_specs=[row(D_MODEL), _full((1, D_MODEL)), _full((D_MODEL, 4 * ATTN_W)), _full((1, ATTN_W)),
                  _full((1, ATTN_W)), _full((MXU_W, MXU_W))],
        out_specs=[row(2 * ATTN_W), row(ATTN_W), row(ATTN_W), row(ATTN_W), row(ATTN_W), row(D_MODEL)],
        out_shape=[_sds((s, 2 * ATTN_W)), _sds((s, ATTN_W)), _sds((s, ATTN_W)), _sds((s, ATTN_W)),
                   _sds((s, ATTN_W)), _sds((s, D_MODEL), BF16)],
        compiler_params=_cp(("parallel",)),
    )(x, g1, w_in, gq, gk, gmat)


def _attn_rows(t, d, nb):
    if d == 1:
        q0 = t * ATTN_BLOCK
        return (t, pl.ds(q0, ATTN_BLOCK), pl.ds(ATTN_CHUNK + q0, ATTN_BLOCK),
                pl.ds(ATTN_CHUNK - ATTN_BLOCK + q0, ATTN_BLOCK))
    r = t // nb
    b = t % nb
    return (b, pl.ds(ATTN_BLOCK * b * d + r, ATTN_BLOCK, stride=d),
            pl.ds(ATTN_CHUNK + ATTN_BLOCK * b * d + r, ATTN_BLOCK, stride=d),
            pl.ds(ATTN_CHUNK + ATTN_BLOCK * (b - 1) * d + r, ATTN_BLOCK, stride=d))


def _attn_masks():
    row = lax.broadcasted_iota(jnp.int32, (ATTN_BLOCK, LANES), 0)
    col = lax.broadcasted_iota(jnp.int32, (ATTN_BLOCK, LANES), 1)
    return row, col


NBLK = ATTN_CHUNK // ATTN_BLOCK


def _attn_bias(bias_s):
    row, col = _attn_masks()
    bias_s[:, pl.ds(0, LANES)] = jnp.where(col >= row, 0.0, NEG_INF)
    bias_s[:, pl.ds(LANES, LANES)] = jnp.where(col <= row, 0.0, NEG_INF)
    return col < HEAD_DIM


def _attn_fwd(qn, kn, v, ex=None, group=4):
    s = qn.shape[0]
    nch = s // ATTN_CHUNK
    scale = HEAD_DIM ** -0.5
    npat = len(DILATIONS)
    n_hp = ATTN_W // LANES

    def body(*refs):
        ((q_ref, kp_ref, kc_ref, vp_ref, vc_ref), (o_ref, lse_ref),
         (kk, vv, kt_s, vb_s, bias_s, m_s, a_s), hx) = _carry(ex, refs, 5, 2)
        i = pl.program_id(1)
        if ex is not None:
            @pl.when(jnp.logical_and(pl.program_id(0) == 0, i == 0))
            def _():
                ex.start(*hx)

        kk[pl.ds(0, ATTN_CHUNK), :] = kp_ref[...]
        kk[pl.ds(ATTN_CHUNK, ATTN_CHUNK), :] = kc_ref[...]
        vv[pl.ds(0, ATTN_CHUNK), :] = vp_ref[...]
        vv[pl.ds(ATTN_CHUNK, ATTN_CHUNK), :] = vc_ref[...]
        head0 = _attn_bias(bias_s)
        first_pen = jnp.where(i > 0, 0.0, NEG_INF)

        for p, d in enumerate(DILATIONS):
            nb = ATTN_CHUNK // (ATTN_BLOCK * d)

            def prep(t, d=d, nb=nb):
                b, _, crows, prows = _attn_rows(t, d, nb)
                nxt = t + 1 if b + 1 < nb else None
                ktc = kk[crows, :].T.astype(BF16)
                vc = vv[crows, :]
                v0 = jnp.where(head0, vc, 1.0).astype(BF16)
                v1 = jnp.where(head0, 1.0, vc).astype(BF16)
                kt_s[t, :, pl.ds(LANES, LANES)] = ktc
                vb_s[2 * t, pl.ds(ATTN_BLOCK, ATTN_BLOCK), :] = v0
                vb_s[2 * t + 1, pl.ds(ATTN_BLOCK, ATTN_BLOCK), :] = v1
                if nxt is not None:
                    kt_s[nxt, :, pl.ds(0, LANES)] = ktc
                    vb_s[2 * nxt, pl.ds(0, ATTN_BLOCK), :] = v0
                    vb_s[2 * nxt + 1, pl.ds(0, ATTN_BLOCK), :] = v1
                if b == 0:
                    kt_s[t, :, pl.ds(0, LANES)] = kk[prows, :].T.astype(BF16)
                    vp = vv[prows, :]
                    vb_s[2 * t, pl.ds(0, ATTN_BLOCK), :] = jnp.where(head0, vp, 1.0).astype(BF16)
                    vb_s[2 * t + 1, pl.ds(0, ATTN_BLOCK), :] = jnp.where(head0, 1.0, vp).astype(BF16)

            def main(tg, p=p, d=d, nb=nb):
                st = []
                for g in range(group):
                    t = tg * group + g
                    b, qrows, _, _ = _attn_rows(t, d, nb)
                    q = q_ref[qrows, :] * scale
                    for h in range(2):
                        hm = head0 if h == 0 else jnp.logical_not(head0)
                        st.append(dict(t=t, b=b, qrows=qrows, sc=_dot(jnp.where(hm, q, 0.0).astype(BF16), kt_s[t])))
                for e in st:
                    sc = e["sc"] + bias_s[...]
                    s_p = sc[:, :LANES] + first_pen if e["b"] == 0 else sc[:, :LANES]
                    s_c = sc[:, LANES:]
                    m = jnp.max(jnp.maximum(s_p, s_c), axis=-1, keepdims=True)
                    e["eb"] = jnp.concatenate([jnp.exp(s_p - m), jnp.exp(s_c - m)], axis=1).astype(BF16)
                    e["m"] = jnp.broadcast_to(m, (ATTN_BLOCK, LANES))
                for g in range(group):
                    e0, e1 = st[2 * g], st[2 * g + 1]
                    t = e0["t"]
                    m_s[p, e0["qrows"], :] = jnp.where(head0, e0["m"], e1["m"])
                    a_s[2 * p, e0["qrows"], :] = _dot(e0["eb"], vb_s[2 * t])
                    a_s[2 * p + 1, e0["qrows"], :] = _dot(e1["eb"], vb_s[2 * t + 1])

            for g in range(group):
                prep(g)
            for tg in range(NBLK // group):
                if tg + 1 < NBLK // group:
                    for g in range(group):
                        prep((tg + 1) * group + g)
                main(tg)

        def merge(t, carry):
            rows = pl.ds(t * ATTN_BLOCK, ATTN_BLOCK)
            m_all = m_s[0, rows, :]
            for p in range(1, npat):
                m_all = jnp.maximum(m_all, m_s[p, rows, :])
            num = jnp.zeros((ATTN_BLOCK, LANES), F32)
            den = jnp.zeros((ATTN_BLOCK, LANES), F32)
            for p in range(npat):
                w = jnp.exp(m_s[p, rows, :] - m_all)
                a0, a1 = a_s[2 * p, rows, :], a_s[2 * p + 1, rows, :]
                num = num + jnp.where(head0, a0, a1) * w
                den = den + pltpu.roll(jnp.where(head0, a1, a0), HEAD_DIM, 1) * w
            o_ref[rows, :] = num / den
            lse_ref[rows, :] = m_all + jnp.log(den)
            return carry

        for t in range(NBLK):
            merge(t, 0)
        if ex is not None:
            @pl.when(jnp.logical_and(pl.program_id(0) == n_hp - 1, i == nch - 1))
            def _():
                ex.wait(*hx)

    cur = pl.BlockSpec((ATTN_CHUNK, LANES), lambda h, i: (i, h))
    prev = pl.BlockSpec((ATTN_CHUNK, LANES), lambda h, i: (jnp.maximum(i - 1, 0), h))
    hosted = ex is not None
    return pl.pallas_call(
        body, name="attn_fwd", grid=(n_hp, nch),
        in_specs=[cur, prev, cur, prev, cur] + (ex.specs() if hosted else []),
        out_specs=[cur, cur] + (ex.specs() if hosted else []),
        out_shape=[_sds((s, ATTN_W)), _sds((s, ATTN_W))] + (ex.out_shape() if hosted else []),
        scratch_shapes=[pltpu.VMEM((2 * ATTN_CHUNK, LANES), F32), pltpu.VMEM((2 * ATTN_CHUNK, LANES), F32),
                        pltpu.VMEM((NBLK, LANES, 2 * LANES), BF16), pltpu.VMEM((2 * NBLK, 2 * ATTN_BLOCK, LANES), BF16),
                        pltpu.VMEM((ATTN_BLOCK, 2 * LANES), F32),
                        pltpu.VMEM((npat, ATTN_CHUNK, LANES), F32), pltpu.VMEM((2 * npat, ATTN_CHUNK, LANES), F32)]
        + (ex.scratch() if hosted else []),
        compiler_params=_cp(("arbitrary", "arbitrary")),
    )(qn, kn, kn, v, v, *(ex.srcs if hosted else []))


def _attn_bwd(qn, kn, v, o, lse, do, ex=None, group=4):
    s = qn.shape[0]
    nch = s // ATTN_CHUNK
    scale = HEAD_DIM ** -0.5
    npat = len(DILATIONS)
    n_hp = ATTN_W // LANES

    def body(*refs):
        ((q_ref, kp_ref, kc_ref, vp_ref, vc_ref, o_ref, lse_ref, do_ref), (dq_ref, dk_ref, dv_ref),
         (kk, vv, dkk, dvv, kt_s, vt_s, kn_s, bias_s, dq_s, dl_s, dkb, dvb), hx) = _carry(ex, refs, 8, 3)
        step = pl.program_id(1)
        i = nch - 1 - step
        if ex is not None:
            @pl.when(jnp.logical_and(pl.program_id(0) == 0, step == 0))
            def _():
                ex.start(*hx)

        kk[pl.ds(0, ATTN_CHUNK), :] = kp_ref[...]
        kk[pl.ds(ATTN_CHUNK, ATTN_CHUNK), :] = kc_ref[...]
        vv[pl.ds(0, ATTN_CHUNK), :] = vp_ref[...]
        vv[pl.ds(ATTN_CHUNK, ATTN_CHUNK), :] = vc_ref[...]

        @pl.when(step == 0)
        def _():
            dkk[pl.ds(ATTN_CHUNK, ATTN_CHUNK), :] = jnp.zeros((ATTN_CHUNK, LANES), F32)
            dvv[pl.ds(ATTN_CHUNK, ATTN_CHUNK), :] = jnp.zeros((ATTN_CHUNK, LANES), F32)

        @pl.when(step > 0)
        def _():
            dkk[pl.ds(ATTN_CHUNK, ATTN_CHUNK), :] = dkk[pl.ds(0, ATTN_CHUNK), :]
            dvv[pl.ds(ATTN_CHUNK, ATTN_CHUNK), :] = dvv[pl.ds(0, ATTN_CHUNK), :]

        dkk[pl.ds(0, ATTN_CHUNK), :] = jnp.zeros((ATTN_CHUNK, LANES), F32)
        dvv[pl.ds(0, ATTN_CHUNK), :] = jnp.zeros((ATTN_CHUNK, LANES), F32)
        head0 = _attn_bias(bias_s)

        def delta(t, carry):
            rows = pl.ds(t * ATTN_BLOCK, ATTN_BLOCK)
            prod = do_ref[rows, :] * o_ref[rows, :]
            d0 = jnp.sum(jnp.where(head0, prod, 0.0), axis=-1, keepdims=True)
            d1 = jnp.sum(jnp.where(head0, 0.0, prod), axis=-1, keepdims=True)
            dl_s[rows, :] = jnp.where(head0, d0, d1)
            return carry

        for t in range(NBLK):
            delta(t, 0)

        first_pen = jnp.where(i > 0, 0.0, NEG_INF)

        for p, d in enumerate(DILATIONS):
            nb = ATTN_CHUNK // (ATTN_BLOCK * d)

            def prep(t, d=d, nb=nb):
                b, _, crows, prows = _attn_rows(t, d, nb)
                nxt = t + 1 if b + 1 < nb else None
                kc = kk[crows, :]
                ktc = kc.T.astype(BF16)
                knc = (kc * scale).astype(BF16)
                vtc = vv[crows, :].T.astype(BF16)
                kt_s[t, :, pl.ds(LANES, LANES)] = ktc
                kn_s[t, pl.ds(ATTN_BLOCK, ATTN_BLOCK), :] = knc
                vt_s[t, :, pl.ds(LANES, LANES)] = vtc
                if nxt is not None:
                    kt_s[nxt, :, pl.ds(0, LANES)] = ktc
                    kn_s[nxt, pl.ds(0, ATTN_BLOCK), :] = knc
                    vt_s[nxt, :, pl.ds(0, LANES)] = vtc
                if b == 0:
                    kp = kk[prows, :]
                    kt_s[t, :, pl.ds(0, LANES)] = kp.T.astype(BF16)
                    kn_s[t, pl.ds(0, ATTN_BLOCK), :] = (kp * scale).astype(BF16)
                    vt_s[t, :, pl.ds(0, LANES)] = vv[prows, :].T.astype(BF16)

            def main(tg, p=p, d=d, nb=nb):
                st = []
                for g in range(group):
                    t = tg * group + g
                    b, qrows, _, _ = _attn_rows(t, d, nb)
                    q = q_ref[qrows, :] * scale
                    dout = do_ref[qrows, :]
                    lse_b = lse_ref[qrows, :]
                    dl_b = dl_s[qrows, :]
                    for h in range(2):
                        hm = head0 if h == 0 else jnp.logical_not(head0)
                        c0 = h * HEAD_DIM
                        qh = jnp.where(hm, q, 0.0).astype(BF16)
                        doh = jnp.where(hm, dout, 0.0).astype(BF16)
                        st.append(dict(t=t, b=b, qrows=qrows, qh=qh, doh=doh, lse=lse_b[:, c0:c0 + 1],
                                       dl=dl_b[:, c0:c0 + 1], sc=_dot(qh, kt_s[t]), dp=_dot(doh, vt_s[t])))
                for e in st:
                    sc = e["sc"] + bias_s[...]
                    if e["b"] == 0:
                        sc = jnp.concatenate([sc[:, :LANES] + first_pen, sc[:, LANES:]], axis=1)
                    pr = jnp.exp(sc - e["lse"])
                    e["ds"] = (pr * (e["dp"] - e["dl"])).astype(BF16)
                    e["pr"] = pr.astype(BF16)
                for g in range(group):
                    e0, e1 = st[2 * g], st[2 * g + 1]
                    t = e0["t"]
                    dq_s[p, e0["qrows"], :] = jnp.where(head0, _dot(e0["ds"], kn_s[t]), _dot(e1["ds"], kn_s[t]))
                    dkb[t] = _dot_tn(e0["ds"], e0["qh"]) + _dot_tn(e1["ds"], e1["qh"])
                    dvb[t] = _dot_tn(e0["pr"], e0["doh"]) + _dot_tn(e1["pr"], e1["doh"])

            def scatter(t, d=d, nb=nb):
                b, _, crows, prows = _attn_rows(t, d, nb)
                dk_c = dkb[t, pl.ds(ATTN_BLOCK, ATTN_BLOCK), :]
                dv_c = dvb[t, pl.ds(ATTN_BLOCK, ATTN_BLOCK), :]
                if b + 1 < nb:
                    dk_c = dk_c + dkb[t + 1, pl.ds(0, ATTN_BLOCK), :]
                    dv_c = dv_c + dvb[t + 1, pl.ds(0, ATTN_BLOCK), :]
                dkk[crows, :] = dkk[crows, :] + dk_c
                dvv[crows, :] = dvv[crows, :] + dv_c
                if b == 0:
                    dkk[prows, :] = dkk[prows, :] + dkb[t, pl.ds(0, ATTN_BLOCK), :]
                    dvv[prows, :] = dvv[prows, :] + dvb[t, pl.ds(0, ATTN_BLOCK), :]

            n_groups = NBLK // group
            for g in range(group):
                prep(g)
            for tg in range(n_groups):
                if tg + 1 < n_groups:
                    for g in range(group):
                        prep((tg + 1) * group + g)
                main(tg)
                if tg >= 1:
                    for g in range(group):
                        scatter((tg - 1) * group + g)
            for g in range(group):
                scatter((n_groups - 1) * group + g)

        def finish(t, carry):
            rows = pl.ds(t * ATTN_BLOCK, ATTN_BLOCK)
            acc = dq_s[0, rows, :]
            for p in range(1, npat):
                acc = acc + dq_s[p, rows, :]
            dq_ref[rows, :] = acc
            return carry

        for t in range(NBLK):
            finish(t, 0)
        dk_ref[...] = dkk[pl.ds(ATTN_CHUNK, ATTN_CHUNK), :]
        dv_ref[...] = dvv[pl.ds(ATTN_CHUNK, ATTN_CHUNK), :]
        if ex is not None:
            @pl.when(jnp.logical_and(pl.program_id(0) == n_hp - 1, step == nch - 1))
            def _():
                ex.wait(*hx)

    cur = pl.BlockSpec((ATTN_CHUNK, LANES), lambda h, t: (nch - 1 - t, h))
    prev = pl.BlockSpec((ATTN_CHUNK, LANES), lambda h, t: (jnp.maximum(nch - 2 - t, 0), h))
    big = pltpu.VMEM((2 * ATTN_CHUNK, LANES), F32)
    pair_t = pltpu.VMEM((NBLK, LANES, 2 * LANES), BF16)
    hosted = ex is not None
    return pl.pallas_call(
        body, name="attn_bwd", grid=(n_hp, nch),
        in_specs=[cur, prev, cur, prev, cur, cur, cur, cur] + (ex.specs() if hosted else []),
        out_specs=[cur, cur, cur] + (ex.specs() if hosted else []),
        out_shape=[_sds((s, ATTN_W))] * 3 + (ex.out_shape() if hosted else []),
        scratch_shapes=[big, big, big, big, pair_t, pair_t, pltpu.VMEM((NBLK, 2 * ATTN_BLOCK, LANES), BF16),
                        pltpu.VMEM((ATTN_BLOCK, 2 * LANES), F32),
                        pltpu.VMEM((npat, ATTN_CHUNK, LANES), F32), pltpu.VMEM((ATTN_CHUNK, LANES), F32),
                        pltpu.VMEM((NBLK, 2 * ATTN_BLOCK, LANES), F32), pltpu.VMEM((NBLK, 2 * ATTN_BLOCK, LANES), F32)]
        + (ex.scratch() if hosted else []),
        compiler_params=_cp(("arbitrary", "arbitrary")),
    )(qn, kn, kn, v, v, o, lse, do, *(ex.srcs if hosted else []))


def _discretize(lr, li, dt):
    mag = jnp.exp(lr * dt)
    abr = mag * jnp.cos(li * dt)
    abi = mag * jnp.sin(li * dt)
    den = lr * lr + li * li
    nr, ni = abr - 1.0, abi
    cr = (nr * lr + ni * li) / den
    ci = (ni * lr - nr * li) / den
    return abr, abi, den, nr, ni, cr, ci


def _ssm_discretize(a_re, a_im, log_dt, b_re_t, b_im_t):
    def body(ar_ref, ai_ref, ldt_ref, br_ref, bi_ref, abr_ref, abi_ref, bbr_ref, bbi_ref):
        abr, abi, _, _, _, cr, ci = _discretize(ar_ref[...], ai_ref[...], jnp.exp(ldt_ref[...]))
        br, bi = br_ref[...], bi_ref[...]
        abr_ref[...] = abr
        abi_ref[...] = abi
        bbr_ref[...] = cr * br - ci * bi
        bbi_ref[...] = cr * bi + ci * br

    return pl.pallas_call(
        body, name="ssm_discretize",
        out_shape=[_sds(a_re.shape)] * 2 + [_sds(b_re_t.shape)] * 2,
    )(a_re, a_im, log_dt, b_re_t, b_im_t)


def _ssm_discretize_bwd(a_re, a_im, log_dt, b_re_t, b_im_t, dabr, dabi, dbbr, dbbi):
    def body(ar_ref, ai_ref, ldt_ref, br_ref, bi_ref, dabr_ref, dabi_ref, dbbr_ref, dbbi_ref,
             dar_ref, dai_ref, dldt_ref, dbr_ref, dbi_ref):
        lr, li = ar_ref[...], ai_ref[...]
        dt = jnp.exp(ldt_ref[...])
        abr, abi, den, nr, ni, cr, ci = _discretize(lr, li, dt)
        br, bi = br_ref[...], bi_ref[...]
        gbr, gbi = dbbr_ref[...], dbbi_ref[...]
        dcr = jnp.sum(gbr * br + gbi * bi, axis=1, keepdims=True)
        dci = jnp.sum(gbi * br - gbr * bi, axis=1, keepdims=True)
        dbr_ref[...] = cr * gbr + ci * gbi
        dbi_ref[...] = cr * gbi - ci * gbr
        dnr = (dcr * lr - dci * li) / den
        dni = (dcr * li + dci * lr) / den
        dden = -(dcr * cr + dci * ci) / den
        dlr = (dcr * nr + dci * ni) / den + dden * 2.0 * lr
        dli = (dcr * ni - dci * nr) / den + dden * 2.0 * li
        gabr = dabr_ref[...] + dnr
        gabi = dabi_ref[...] + dni
        dphi = gabr * abr + gabi * abi
        dth = gabi * abr - gabr * abi
        dar_ref[...] = dlr + dphi * dt
        dai_ref[...] = dli + dth * dt
        dldt_ref[...] = jnp.sum(dphi * lr + dth * li, axis=2, keepdims=True) * dt

    return pl.pallas_call(
        body, name="ssm_discretize_bwd",
        out_shape=[_sds(a_re.shape)] * 2 + [_sds(log_dt.shape)] + [_sds(b_re_t.shape)] * 2,
    )(a_re, a_im, log_dt, b_re_t, b_im_t, dabr, dabi, dbbr, dbbi)


def _ssm_power(abr, abi, n_sq):
    def body(r_ref, i_ref, or_ref, oi_ref):
        r, i = r_ref[...], i_ref[...]
        for _ in range(n_sq):
            r, i = r * r - i * i, 2.0 * r * i
        or_ref[...] = r
        oi_ref[...] = i

    return pl.pallas_call(body, name="ssm_power", out_shape=[_sds(abr.shape)] * 2)(abr, abi)


N_CB = SSM_W // LANES
CB_STATES = N_STATE // N_CB
ROWS = N_SEG * SSM_LK


class _Neg:
    def __init__(self, ref):
        self.ref = ref

    def __getitem__(self, idx):
        return -self.ref[idx]


def _seg_init(fin_r, fin_i, pw_r, pw_i, x_r, x_i, reverse):
    zero = jnp.zeros((1, N_STATE), F32)
    cr, ci = zero, zero
    order = range(N_SEG - 1, -1, -1) if reverse else range(N_SEG)
    pr = pw_r[...]
    pi = -pw_i[...] if reverse else pw_i[...]
    for j in order:
        x_r[pl.ds(j, 1), :] = cr
        x_i[pl.ds(j, 1), :] = ci
        fr, fi = fin_r[pl.ds(j, 1), :], fin_i[pl.ds(j, 1), :]
        cr, ci = fr + pr * cr - pi * ci, fi + pr * ci + pi * cr


def _permute_in(src_ref, dst):
    for c in range(N_CB):
        dst[c] = src_ref[:, :, pl.ds(c * LANES, LANES)].reshape(ROWS, LANES)


def _permute_out(src, dst_ref):
    for c in range(N_CB):
        dst_ref[:, :, pl.ds(c * LANES, LANES)] = src[c].reshape(SSM_LK, N_SEG, LANES)


def _scan_block(a_r, a_i, c, b_r, b_i, b_off, x_r, x_i, reverse, acc=None):
    cols = pl.ds(c * CB_STATES, CB_STATES)
    ar = jnp.broadcast_to(a_r[:, cols], (N_SEG, CB_STATES))
    ai = jnp.broadcast_to(a_i[:, cols], (N_SEG, CB_STATES))
    xr, xi = x_r[:, cols], x_i[:, cols]
    if acc is not None:
        sr = jnp.zeros((N_SEG, CB_STATES), F32)
        si = jnp.zeros((N_SEG, CB_STATES), F32)
    for t in range(SSM_LK):
        k = (SSM_LK - 1 - t) if reverse else t
        rows = pl.ds(k * N_SEG + b_off, N_SEG)
        xr, xi = ar * xr - ai * xi + b_r[rows, :], ar * xi + ai * xr + b_i[rows, :]
        b_r[rows, :] = xr
        b_i[rows, :] = xi
        if acc is not None:
            pr, pi = acc[0][pl.ds(k * N_SEG, N_SEG), :], acc[1][pl.ds(k * N_SEG, N_SEG), :]
            sr = sr + xr * pr + xi * pi
            si = si + xi * pr - xr * pi
    x_r[:, cols] = xr
    x_i[:, cols] = xi
    if acc is not None:
        acc[2][:, cols] += sr
        acc[3][:, cols] += si


def _ssm_fwd(u3, abr, abi, pw_r, pw_i, fin_r, fin_i, bb_r, bb_i, cc_r, cc_i, dskip, finals_only):
    sl = u3.shape[0]
    nch = sl // SSM_LK

    def body(u_ref, abr_ref, abi_ref, pwr_ref, pwi_ref, finr_ref, fini_ref, bbr_ref, bbi_ref,
             ccr_ref, cci_ref, d_ref, *rest):
        if finals_only:
            xfr_ref, xfi_ref, up, x_r, x_i = rest[:5]
        else:
            y_ref, xsr_ref, xsi_ref, up, yp, x_r, x_i = rest[:7]
        xs_r, xs_i = rest[-2 * N_CB:-N_CB], rest[-N_CB:]
        k = pl.program_id(0)

        @pl.when(k == 0)
        def _():
            _seg_init(finr_ref, fini_ref, pwr_ref, pwi_ref, x_r, x_i, False)

        if not finals_only:
            xsr_ref[0] = x_r[...]
            xsi_ref[0] = x_i[...]
        _permute_in(u_ref, up)

        def drive(c):
            lhs = up[c].astype(BF16)
            xs_r[c][...] = _dot(lhs, bbr_ref[c])
            xs_i[c][...] = _dot(lhs, bbi_ref[c])

        def readout(c):
            yp[c] = (_dot(xs_r[c][...].astype(BF16), ccr_ref[c]) - _dot(xs_i[c][...].astype(BF16), cci_ref[c])
                     + d_ref[:, pl.ds(c * LANES, LANES)] * up[c])

        drive(0)
        for c in range(N_CB):
            if c + 1 < N_CB:
                drive(c + 1)
            if c >= 1 and not finals_only:
                readout(c - 1)
            _scan_block(abr_ref, abi_ref, c, xs_r[c], xs_i[c], 0, x_r, x_i, False)
        if finals_only:
            @pl.when(k == nch - 1)
            def _():
                xfr_ref[...] = x_r[...]
                xfi_ref[...] = x_i[...]
        else:
            readout(N_CB - 1)
            _permute_out(yp, y_ref)

    ublk = pl.BlockSpec((SSM_LK, N_SEG, SSM_W), lambda k: (k, 0, 0))
    st = pl.BlockSpec((1, N_SEG, N_STATE), lambda k: (k, 0, 0))
    vec = _full((1, N_STATE))
    mat = _full((N_SEG, N_STATE))
    chunk = pltpu.VMEM((N_CB, ROWS, LANES), F32)
    blocks = [pltpu.VMEM((ROWS, CB_STATES), F32)] * (2 * N_CB)
    small = pltpu.VMEM((N_SEG, N_STATE), F32)
    if finals_only:
        out_specs, out_shape = [mat, mat], [_sds((N_SEG, N_STATE))] * 2
        scratch, name = [chunk, small, small] + blocks, "ssm_fwd_finals"
    else:
        out_specs = [ublk, st, st]
        out_shape = [_sds(u3.shape)] + [_sds((nch, N_SEG, N_STATE))] * 2
        scratch, name = [chunk, chunk, small, small] + blocks, "ssm_fwd"
    return pl.pallas_call(
        body, name=name, grid=(nch,),
        in_specs=[ublk, vec, vec, vec, vec, mat, mat,
                  _full((N_CB, LANES, CB_STATES)), _full((N_CB, LANES, CB_STATES)),
                  _full((N_CB, CB_STATES, LANES)), _full((N_CB, CB_STATES, LANES)), _full((1, SSM_W))],
        out_specs=out_specs, out_shape=out_shape, scratch_shapes=scratch,
        compiler_params=_cp(("arbitrary",)),
    )(u3, abr, abi, pw_r, pw_i, fin_r, fin_i, bb_r, bb_i, cc_r, cc_i, dskip)


def _ssm_bwd(u3, dy3, xst_r, xst_i, abr, abi, pw_r, pw_i, fin_r, fin_i, bb_r, bb_i, bbt_r, bbt_i,
             cct_r, cct_i, dskip, finals_only):
    sl = u3.shape[0]
    nch = sl // SSM_LK

    def body(u_ref, g_ref, xsr_ref, xsi_ref, abr_ref, abi_ref, pwr_ref, pwi_ref,
             finr_ref, fini_ref, bbr_ref, bbi_ref, btr_ref, bti_ref, ctr_ref, cti_ref, d_ref, *rest):
        if finals_only:
            lfr_ref, lfi_ref, gp, lam_r, lam_i = rest[:5]
            l_r, l_i = rest[-2 * N_CB:-N_CB], rest[-N_CB:]
        else:
            (du_ref, dar_ref, dai_ref, dbr_ref, dbi_ref, dcr_ref, dci_ref, dd_ref,
             gp, up, yp, lam_r, lam_i, x_r, x_i, sar, sai, sdd) = rest[:18]
            l_r, l_i = rest[18:18 + N_CB], rest[18 + N_CB:18 + 2 * N_CB]
            xx_r, xx_i = rest[18 + 2 * N_CB:18 + 3 * N_CB], rest[18 + 3 * N_CB:]
        t = pl.program_id(0)

        @pl.when(t == 0)
        def _():
            _seg_init(finr_ref, fini_ref, pwr_ref, pwi_ref, lam_r, lam_i, True)
            if not finals_only:
                sar[...] = jnp.zeros_like(sar)
                sai[...] = jnp.zeros_like(sai)
                sdd[...] = jnp.zeros_like(sdd)
                dbr_ref[...] = jnp.zeros_like(dbr_ref)
                dbi_ref[...] = jnp.zeros_like(dbi_ref)
                dcr_ref[...] = jnp.zeros_like(dcr_ref)
                dci_ref[...] = jnp.zeros_like(dci_ref)

        _permute_in(g_ref, gp)
        if not finals_only:
            _permute_in(u_ref, up)
            x_r[...] = xsr_ref[0]
            x_i[...] = xsi_ref[0]

        def drive(c):
            lhs = gp[c].astype(BF16)
            l_r[c][...] = _dot(lhs, ctr_ref[c])
            l_i[c][...] = -_dot(lhs, cti_ref[c])
            if not finals_only:
                cols = pl.ds(c * CB_STATES, CB_STATES)
                xx_r[c][pl.ds(0, N_SEG), :] = x_r[:, cols]
                xx_i[c][pl.ds(0, N_SEG), :] = x_i[:, cols]
                ub = up[c].astype(BF16)
                xx_r[c][pl.ds(N_SEG, ROWS), :] = _dot(ub, bbr_ref[c])
                xx_i[c][pl.ds(N_SEG, ROWS), :] = _dot(ub, bbi_ref[c])

        def collect(c):
            lrb = l_r[c][...].astype(BF16)
            lib = l_i[c][...].astype(BF16)
            ub = up[c].astype(BF16)
            gb = gp[c].astype(BF16)
            dbr_ref[c] += _dot_tn(lrb, ub)
            dbi_ref[c] += _dot_tn(lib, ub)
            dcr_ref[c] += _dot_tn(gb, xx_r[c][pl.ds(N_SEG, ROWS), :].astype(BF16))
            dci_ref[c] += -_dot_tn(gb, xx_i[c][pl.ds(N_SEG, ROWS), :].astype(BF16))
            yp[c] = _dot(lrb, btr_ref[c]) + _dot(lib, bti_ref[c]) + d_ref[:, pl.ds(c * LANES, LANES)] * gp[c]
            prod = gp[c] * up[c]
            sdd[:, pl.ds(c * LANES, LANES)] += jnp.sum(prod.reshape(SSM_LK, N_SEG, LANES), axis=0)

        drive(0)
        for c in range(N_CB):
            if c + 1 < N_CB:
                drive(c + 1)
            if finals_only:
                _scan_block(abr_ref, _Neg(abi_ref), c, l_r[c], l_i[c], 0, lam_r, lam_i, True)
            else:
                if c >= 1:
                    collect(c - 1)
                _scan_block(abr_ref, abi_ref, c, xx_r[c], xx_i[c], N_SEG, x_r, x_i, False)
                _scan_block(abr_ref, _Neg(abi_ref), c, l_r[c], l_i[c], 0, lam_r, lam_i, True,
                            acc=(xx_r[c], xx_i[c], sar, sai))
        if finals_only:
            @pl.when(t == nch - 1)
            def _():
                lfr_ref[...] = lam_r[...]
                lfi_ref[...] = lam_i[...]
        else:
            collect(N_CB - 1)
            _permute_out(yp, du_ref)

            @pl.when(t == nch - 1)
            def _():
                dar_ref[...] = jnp.sum(sar[...], axis=0, keepdims=True)
                dai_ref[...] = jnp.sum(sai[...], axis=0, keepdims=True)
                dd_ref[...] = jnp.sum(sdd[...], axis=0, keepdims=True)

    ublk = pl.BlockSpec((SSM_LK, N_SEG, SSM_W), lambda t: (nch - 1 - t, 0, 0))
    st = pl.BlockSpec((1, N_SEG, N_STATE), lambda t: (nch - 1 - t, 0, 0))
    vec = _full((1, N_STATE))
    mat = _full((N_SEG, N_STATE))
    cs = _full((N_CB, LANES, CB_STATES))
    sc = _full((N_CB, CB_STATES, LANES))
    in_specs = [ublk, ublk, st, st, vec, vec, vec, vec, mat, mat, cs, cs, sc, sc, cs, cs, _full((1, SSM_W))]
    chunk = pltpu.VMEM((N_CB, ROWS, LANES), F32)
    blocks = [pltpu.VMEM((ROWS, CB_STATES), F32)] * (2 * N_CB)
    small = pltpu.VMEM((N_SEG, N_STATE), F32)
    if finals_only:
        out_specs, out_shape = [mat, mat], [_sds((N_SEG, N_STATE))] * 2
        scratch, name = [chunk, small, small] + blocks, "ssm_bwd_finals"
    else:
        out_specs = [ublk, vec, vec, sc, sc, cs, cs, _full((1, SSM_W))]
        out_shape = ([_sds(u3.shape), _sds((1, N_STATE)), _sds((1, N_STATE))]
                     + [_sds((N_CB, CB_STATES, LANES))] * 2 + [_sds((N_CB, LANES, CB_STATES))] * 2
                     + [_sds((1, SSM_W))])
        scratch = ([chunk, chunk, chunk, small, small, small, small, small, small, pltpu.VMEM((N_SEG, SSM_W), F32)]
                   + blocks + [pltpu.VMEM((ROWS + N_SEG, CB_STATES), F32)] * (2 * N_CB))
        name = "ssm_bwd"
    return pl.pallas_call(
        body, name=name, grid=(nch,), in_specs=in_specs, out_specs=out_specs, out_shape=out_shape,
        scratch_shapes=scratch, compiler_params=_cp(("arbitrary",)),
    )(u3, dy3, xst_r, xst_i, abr, abi, pw_r, pw_i, fin_r, fin_i, bb_r, bb_i, bbt_r, bbt_i, cct_r, cct_i, dskip)


def _row(tm, w):
    return pl.BlockSpec((tm, w), lambda i: (i, 0))


def _acc_rows(ref, rows, first):
    @pl.when(first)
    def _():
        ref[...] = jnp.zeros_like(ref)

    ref[...] += jnp.sum(rows, axis=0, keepdims=True)


def _fwd_mix(attn, y, x, glu_w, glu_b, ga, gs, w_out, g2, tm=1024):
    s = x.shape[0]

    def body(a_ref, y_ref, x_ref, gw_ref, gb_ref, ga_ref, gs_ref, wo_ref, g2_ref, mix_ref, x2_ref, h_ref):
        a = a_ref[...]
        anb = ((a * _rms(a)) * ga_ref[...]).astype(BF16)
        z, _ = _gelu(y_ref[...])
        so = z * jax.nn.sigmoid(_dot(z.astype(BF16), gw_ref[...]) + gb_ref[...])
        snb = ((so * _rms(so)) * gs_ref[...]).astype(BF16)
        mix_ref[:, pl.ds(0, ATTN_W)] = anb
        mix_ref[:, pl.ds(ATTN_W, SSM_W)] = snb
        x2 = x_ref[...] + (_dot(anb, wo_ref[pl.ds(0, ATTN_W), :]) + _dot(snb, wo_ref[pl.ds(ATTN_W, SSM_W), :]))
        x2_ref[...] = x2
        h_ref[...] = ((x2 * _rms(x2)) * g2_ref[...]).astype(BF16)

    return pl.pallas_call(
        body, name="fwd_mix", grid=(s // tm,),
        in_specs=[_row(tm, ATTN_W), _row(tm, SSM_W), _row(tm, D_MODEL), _full((SSM_W, SSM_W)), _full((1, SSM_W)),
                  _full((1, ATTN_W)), _full((1, SSM_W)), _full((D_MODEL, D_MODEL)), _full((1, D_MODEL))],
        out_specs=[_row(tm, D_MODEL), _row(tm, D_MODEL), _row(tm, D_MODEL)],
        out_shape=[_sds((s, D_MODEL), BF16), _sds((s, D_MODEL)), _sds((s, D_MODEL), BF16)],
        compiler_params=_cp(("parallel",)),
    )(attn, y, x, glu_w, glu_b, ga, gs, w_out, g2)


def _mlp_up(h, w_up, tm=512, bn=1024):
    s = h.shape[0]

    def body(h_ref, w_ref, r_ref, hdn_ref):
        hv = h_ref[...]
        for j in range(D_FF // bn):
            cols = pl.ds(j * bn, bn)
            r = jnp.maximum(_dot(hv, w_ref[:, cols]), 0.0)
            r_ref[:, cols] = r.astype(BF16)
            hdn_ref[:, cols] = (r * r).astype(BF16)

    return pl.pallas_call(
        body, name="mlp_up", grid=(s // tm,),
        in_specs=[_row(tm, D_MODEL), _full((D_MODEL, D_FF))],
        out_specs=[_row(tm, D_FF), _row(tm, D_FF)], out_shape=[_sds((s, D_FF), BF16)] * 2,
        compiler_params=_cp(("parallel",)),
    )(h, w_up)


def _mlp_down_loss(hdn, w_down, x2, tgt, tm=512):
    s = x2.shape[0]

    def body(hdn_ref, w_ref, x2_ref, t_ref, dy_ref, dyb_ref, sse_ref):
        err = (x2_ref[...] + _dot(hdn_ref[...], w_ref[...])) - t_ref[...]
        dy = err * (1.0 / D_MODEL)
        dy_ref[...] = dy
        dyb_ref[...] = dy.astype(BF16)

        @pl.when(pl.program_id(0) == 0)
        def _():
            sse_ref[...] = jnp.zeros_like(sse_ref)

        sse_ref[...] += jnp.sum(jnp.sum(err * err, axis=0, keepdims=True), axis=1, keepdims=True)

    return pl.pallas_call(
        body, name="mlp_down_loss", grid=(s // tm,),
        in_specs=[_row(tm, D_FF), _full((D_FF, D_MODEL)), _row(tm, D_MODEL), _row(tm, D_MODEL)],
        out_specs=[_row(tm, D_MODEL), _row(tm, D_MODEL), _full((1, 1))],
        out_shape=[_sds((s, D_MODEL)), _sds((s, D_MODEL), BF16), _sds((1, 1))],
        compiler_params=_cp(("arbitrary",)),
    )(hdn, w_down, x2, tgt)


def _mlp_down_bwd(dyb, w_down_t, r, tm=512, bn=1024):
    s = dyb.shape[0]

    def body(dy_ref, w_ref, r_ref, dup_ref):
        dyv = dy_ref[...]
        for j in range(D_FF // bn):
            cols = pl.ds(j * bn, bn)
            dup_ref[:, cols] = (_dot(dyv, w_ref[:, cols]) * (2.0 * r_ref[:, cols].astype(F32))).astype(BF16)

    return pl.pallas_call(
        body, name="mlp_down_bwd", grid=(s // tm,),
        in_specs=[_row(tm, D_MODEL), _full((D_MODEL, D_FF)), _row(tm, D_FF)],
        out_specs=_row(tm, D_FF), out_shape=_sds((s, D_FF), BF16),
        compiler_params=_cp(("parallel",)),
    )(dyb, w_down_t, r)


def _mlp_up_bwd(dup, w_up_t, x2, g2, dy, tm=512):
    s = x2.shape[0]

    def body(dup_ref, w_ref, x2_ref, g2_ref, dy_ref, dx2_ref, dx2b_ref, dg_ref):
        dx, dg_rows = _rms_bwd(_dot(dup_ref[...], w_ref[...]), x2_ref[...], g2_ref[...], D_MODEL)
        dx2 = dy_ref[...] + dx
        dx2_ref[...] = dx2
        dx2b_ref[...] = dx2.astype(BF16)
        _acc_rows(dg_ref, dg_rows, pl.program_id(0) == 0)

    return pl.pallas_call(
        body, name="mlp_up_bwd", grid=(s // tm,),
        in_specs=[_row(tm, D_FF), _full((D_FF, D_MODEL)), _row(tm, D_MODEL), _full((1, D_MODEL)), _row(tm, D_MODEL)],
        out_specs=[_row(tm, D_MODEL), _row(tm, D_MODEL), _full((1, D_MODEL))],
        out_shape=[_sds((s, D_MODEL)), _sds((s, D_MODEL), BF16), _sds((1, D_MODEL))],
        compiler_params=_cp(("arbitrary",)),
    )(dup, w_up_t, x2, g2, dy)


def _mix_bwd(dx2b, w_out_t, attn, y, glu_w, glu_b, glu_w_t, ga, gs, tm=1024):
    s = attn.shape[0]

    def body(dx2_ref, wot_ref, a_ref, y_ref, gw_ref, gb_ref, gwt_ref, ga_ref, gs_ref,
             da_ref, dys_ref, z_ref, dpre_ref, dga_ref, dgs_ref, dgb_ref):
        first = pl.program_id(0) == 0
        dmix = _dot(dx2_ref[...], wot_ref[...])
        da, dga_rows = _rms_bwd(dmix[:, :ATTN_W], a_ref[...], ga_ref[...], ATTN_W)
        da_ref[...] = da
        yv = y_ref[...]
        z, t = _gelu(yv)
        gate = jax.nn.sigmoid(_dot(z.astype(BF16), gw_ref[...]) + gb_ref[...])
        dso, dgs_rows = _rms_bwd(dmix[:, ATTN_W:], z * gate, gs_ref[...], SSM_W)
        dpre = dso * z * gate * (1.0 - gate)
        dpre_b = dpre.astype(BF16)
        dz = dso * gate + _dot(dpre_b, gwt_ref[...])
        dgelu = 0.5 * (1.0 + t) + 0.5 * yv * (1.0 - t * t) * (GELU_C * (1.0 + 3.0 * 0.044715 * (yv * yv)))
        dys_ref[...] = dz * dgelu
        z_ref[...] = z.astype(BF16)
        dpre_ref[...] = dpre_b
        _acc_rows(dga_ref, dga_rows, first)
        _acc_rows(dgs_ref, dgs_rows, first)
        _acc_rows(dgb_ref, dpre, first)

    vec = _full((1, SSM_W))
    return pl.pallas_call(
        body, name="mix_bwd", grid=(s // tm,),
        in_specs=[_row(tm, D_MODEL), _full((D_MODEL, D_MODEL)), _row(tm, ATTN_W), _row(tm, SSM_W),
                  _full((SSM_W, SSM_W)), vec, _full((SSM_W, SSM_W)), vec, vec],
        out_specs=[_row(tm, ATTN_W), _row(tm, SSM_W), _row(tm, SSM_W), _row(tm, SSM_W), vec, vec, vec],
        out_shape=[_sds((s, ATTN_W)), _sds((s, SSM_W)), _sds((s, SSM_W), BF16), _sds((s, SSM_W), BF16),
                   _sds((1, ATTN_W)), _sds((1, SSM_W)), _sds((1, SSM_W))],
        compiler_params=_cp(("arbitrary",)),
    )(dx2b, w_out_t, attn, y, glu_w, glu_b, glu_w_t, ga, gs)


def _qk_bwd(dqn, dkn, qk, dv, du, gq, gk, gmat, tm=1024):
    s = qk.shape[0]

    def body(dq_ref, dk_ref, qk_ref, dv_ref, du_ref, gq_ref, gk_ref, gm_ref, dp_ref, dgq_ref, dgk_ref):
        first = pl.program_id(0) == 0
        gm = gm_ref[...]
        for idx, (d_ref, g_ref, dg_ref) in enumerate(((dq_ref, gq_ref, dgq_ref), (dk_ref, gk_ref, dgk_ref))):
            xv = qk_ref[:, pl.ds(idx * ATTN_W, ATTN_W)]
            dyv = d_ref[...]
            r = lax.rsqrt(_group_mean(xv * xv, gm) + EPS)
            gdy = dyv * g_ref[...]
            dx = r * gdy - xv * (r * r * r) * _group_mean(gdy * xv, gm)
            dp_ref[:, pl.ds(idx * ATTN_W, ATTN_W)] = dx.astype(BF16)
            _acc_rows(dg_ref, dyv * (xv * r), first)
        dp_ref[:, pl.ds(2 * ATTN_W, ATTN_W)] = dv_ref[...].astype(BF16)
        dp_ref[:, pl.ds(3 * ATTN_W, SSM_W)] = du_ref[...].astype(BF16)

    vec = _full((1, ATTN_W))
    return pl.pallas_call(
        body, name="qk_bwd", grid=(s // tm,),
        in_specs=[_row(tm, ATTN_W), _row(tm, ATTN_W), _row(tm, 2 * ATTN_W), _row(tm, ATTN_W), _row(tm, SSM_W),
                  vec, vec, _full((MXU_W, MXU_W))],
        out_specs=[_row(tm, 4 * ATTN_W), vec, vec],
        out_shape=[_sds((s, 4 * ATTN_W), BF16), _sds((1, ATTN_W)), _sds((1, ATTN_W))],
        compiler_params=_cp(("arbitrary",)),
    )(dqn, dkn, qk, dv, du, gq, gk, gmat)


def _in_bwd(dproj, w_in_t, x, g1, dx2, ex=None, tm=1024):
    s = x.shape[0]
    steps = s // tm

    def body(*refs):
        (dp_ref, w_ref, x_ref, g1_ref, dx2_ref), (gx_ref, dg_ref), _, hx = _carry(ex, refs, 5, 2)
        if ex is not None:
            @pl.when(pl.program_id(0) == 0)
            def _():
                ex.start(*hx)

        dx, dg_rows = _rms_bwd(_dot(dp_ref[...], w_ref[...]), x_ref[...], g1_ref[...], D_MODEL)
        gx_ref[...] = dx2_ref[...] + dx
        _acc_rows(dg_ref, dg_rows, pl.program_id(0) == 0)
        if ex is not None:
            @pl.when(pl.program_id(0) == steps - 1)
            def _():
                ex.wait(*hx)

    hosted = ex is not None
    return pl.pallas_call(
        body, name="in_bwd", grid=(steps,),
        in_specs=[_row(tm, 4 * ATTN_W), _full((4 * ATTN_W, D_MODEL)), _row(tm, D_MODEL), _full((1, D_MODEL)),
                  _row(tm, D_MODEL)] + (ex.specs() if hosted else []),
        out_specs=[_row(tm, D_MODEL), _full((1, D_MODEL))] + (ex.specs() if hosted else []),
        out_shape=[_sds((s, D_MODEL)), _sds((1, D_MODEL))] + (ex.out_shape() if hosted else []),
        scratch_shapes=ex.scratch() if hosted else [],
        compiler_params=_cp(("arbitrary",)),
    )(dproj, w_in_t, x, g1, dx2, *(ex.srcs if hosted else []))


def _mm_tn(a, b, name, ts=2048):
    s, k = a.shape
    n = b.shape[1]
    bk, bn = min(k, 1024), min(n, 1024)
    steps = s // ts

    def body(a_ref, b_ref, o_ref, acc):
        t = pl.program_id(2)

        @pl.when(t == 0)
        def _():
            acc[...] = jnp.zeros_like(acc)

        acc[...] += _dot_tn(a_ref[...], b_ref[...])

        @pl.when(t == steps - 1)
        def _():
            o_ref[...] = acc[...].astype(BF16)

    return pl.pallas_call(
        body, name=name, grid=(k // bk, n // bn, steps),
        in_specs=[pl.BlockSpec((ts, bk), lambda i, j, t: (t, i)), pl.BlockSpec((ts, bn), lambda i, j, t: (t, j))],
        out_specs=pl.BlockSpec((bk, bn), lambda i, j, t: (i, j)), out_shape=_sds((k, n), BF16),
        scratch_shapes=[pltpu.VMEM((bk, bn), F32)],
        compiler_params=_cp(("parallel", "parallel", "arbitrary")),
    )(a, b)


def _peer(k):
    x, y, c = lax.axis_index("x"), lax.axis_index("y"), lax.axis_index("c")
    px = 1 - x if k & 4 else x
    py = 1 - y if k & 2 else y
    pc = 1 - c if k & 1 else c
    return (px, py, pc), 4 * px + 2 * py + pc


def _gather_rows(x_shard):
    m_per, n = x_shard.shape

    def body(x_ref, out_ref, send_sems, recv_sems, local_sem):
        x, y, c = lax.axis_index("x"), lax.axis_index("y"), lax.axis_index("c")
        me, sibling = (x, y, c), (x, y, 1 - c)
        chips = [(1 - x, y), (x, 1 - y), (1 - x, 1 - y)]

        def rows(px, py, pc):
            return out_ref.at[pl.ds((4 * px + 2 * py + pc) * m_per, m_per), :]

        def copy(k, block, to, src=None):
            return pltpu.make_async_remote_copy(
                src_ref=rows(*block) if src is None else src, dst_ref=rows(*block),
                send_sem=send_sems.at[k], recv_sem=recv_sems.at[k], device_id=to, device_id_type=MESH)

        mine = pltpu.make_async_copy(x_ref, rows(*me), local_sem)
        mine.start()
        first = [copy(0, me, sibling, src=x_ref)]
        first += [copy(1 + j, me, (*chip, c), src=x_ref) for j, chip in enumerate(chips)]
        for cp in first:
            cp.start()
        passed = [copy(4 + j, (*chip, c), sibling) for j, chip in enumerate(chips)]
        for j, chip in enumerate(chips):
            copy(1 + j, (*chip, c), me).wait_recv()
            passed[j].start()
        copy(0, sibling, me).wait_recv()
        for j, chip in enumerate(chips):
            copy(4 + j, (*chip, 1 - c), me).wait_recv()
        for cp in first + passed:
            cp.wait_send()
        mine.wait()

    return pl.pallas_call(
        body, name="gather_weights", out_shape=_sds((N_DEV * m_per, n), x_shard.dtype),
        in_specs=[pl.BlockSpec(memory_space=pltpu.VMEM)], out_specs=pl.BlockSpec(memory_space=pltpu.VMEM),
        scratch_shapes=[pltpu.SemaphoreType.DMA((7,)), pltpu.SemaphoreType.DMA((7,)), pltpu.SemaphoreType.DMA],
        compiler_params=pltpu.CompilerParams(vmem_limit_bytes=VMEM_LIMIT),
    )(x_shard)


class _Exchange:
    def __init__(self, srcs, whole):
        self.srcs, self.whole, self.n = list(srcs), list(whole), len(srcs)
        self.rows = [a.shape[0] // N_DEV if w is False else a.shape[0] for a, w in zip(self.srcs, self.whole)]
        self.cols = [a.shape[1] // N_DEV if w == "col" else a.shape[1] for a, w in zip(self.srcs, self.whole)]

    def specs(self):
        return [pl.BlockSpec(memory_space=pl.ANY)] * self.n

    def out_shape(self):
        return [_sds((N_DEV, r, c), a.dtype) for r, c, a in zip(self.rows, self.cols, self.srcs)]

    def scratch(self):
        return [pltpu.SemaphoreType.DMA((self.n * 7,)), pltpu.SemaphoreType.DMA((self.n * 7,)),
                pltpu.SemaphoreType.DMA((self.n,))]

    def _copies(self, ins, outs, sems):
        send_sems, recv_sems, local_sems = sems
        _, me = _peer(0)
        for w in range(self.n):
            for k in range(N_DEV):
                peer, pidx = _peer(k)
                if self.whole[w] is True:
                    src = ins[w]
                elif self.whole[w] == "col":
                    src = ins[w].at[:, pl.ds(pl.multiple_of(pidx * self.cols[w], self.cols[w]), self.cols[w])]
                else:
                    src = ins[w].at[pl.ds(pidx * self.rows[w], self.rows[w]), :]
                if k == 0:
                    yield k, pltpu.make_async_copy(src, outs[w].at[me], local_sems.at[w]), None
                else:
                    sem = w * 7 + k - 1
                    out = pltpu.make_async_remote_copy(src_ref=src, dst_ref=outs[w].at[me], send_sem=send_sems.at[sem],
                                                       recv_sem=recv_sems.at[sem], device_id=peer, device_id_type=MESH)
                    back = pltpu.make_async_remote_copy(src_ref=src, dst_ref=outs[w].at[pidx], send_sem=send_sems.at[sem],
                                                        recv_sem=recv_sems.at[sem], device_id=peer, device_id_type=MESH)
                    yield k, out, back

    def start(self, ins, outs, sems):
        for _, out, _ in self._copies(ins, outs, sems):
            out.start()

    def wait(self, ins, outs, sems):
        for k, out, back in self._copies(ins, outs, sems):
            if k == 0:
                out.wait()
            else:
                back.wait_recv()
                out.wait_send()


def _carry(ex, refs, n_in, n_out):
    nh = ex.n if ex is not None else 0
    ins, hin = refs[:n_in], refs[n_in:n_in + nh]
    outs = refs[n_in + nh:n_in + nh + n_out]
    hout = refs[n_in + nh + n_out:n_in + 2 * nh + n_out]
    rest = refs[n_in + 2 * nh + n_out:]
    if ex is None:
        return ins, outs, rest, None
    return ins, outs, rest[:-3], (hin, hout, rest[-3:])


def _exchange_now(srcs, whole, name):
    ex = _Exchange(srcs, whole)

    def body(*refs):
        _, _, _, (hin, hout, sems) = _carry(ex, refs, 0, 0)
        ex.start(hin, hout, sems)
        ex.wait(hin, hout, sems)

    return pl.pallas_call(body, name=name, out_shape=ex.out_shape(), in_specs=ex.specs(), out_specs=ex.specs(),
                          scratch_shapes=ex.scratch())(*srcs)


def _split_exchange_start(src):
    rows, width = src.shape[0], src.shape[1] // N_DEV

    def body(src_ref, land_ref, send_sems, recv_sems, src_thru, land_thru, token):
        barrier = pltpu.get_barrier_semaphore()
        for k in range(1, N_DEV):
            pl.semaphore_signal(barrier, inc=1, device_id=_peer(k)[0], device_id_type=MESH)
        pl.semaphore_wait(barrier, N_DEV - 1)
        _, me = _peer(0)
        for k in range(1, N_DEV):
            peer, pidx = _peer(k)
            pltpu.make_async_remote_copy(
                src_ref=src_ref.at[:, pl.ds(pl.multiple_of(pidx * width, width), width)], dst_ref=land_ref.at[me],
                send_sem=send_sems.at[k - 1], recv_sem=recv_sems.at[k - 1], device_id=peer, device_id_type=MESH).start()
        token[...] = jnp.zeros_like(token)

    hbm = pl.BlockSpec(memory_space=pltpu.HBM)
    sem = pl.BlockSpec(memory_space=pltpu.SEMAPHORE)
    land = lax.empty((N_DEV, rows, width), src.dtype)
    return pl.pallas_call(
        body, name="w_in_exchange_start",
        out_shape=(pltpu.SemaphoreType.DMA((N_DEV - 1,)), pltpu.SemaphoreType.DMA((N_DEV - 1,)),
                   pltpu.HBM(src.shape, src.dtype), pltpu.HBM(land.shape, land.dtype), _sds((8, LANES))),
        in_specs=(hbm, hbm), out_specs=(sem, sem, hbm, hbm, pl.BlockSpec(memory_space=pltpu.VMEM)),
        input_output_aliases={0: 2, 1: 3},
        compiler_params=pltpu.CompilerParams(has_side_effects=pltpu.SideEffectType.DATAFLOW_SIDE_EFFECTING, collective_id=0),
    )(pltpu.with_memory_space_constraint(src, pltpu.HBM), pltpu.with_memory_space_constraint(land, pltpu.HBM))


def _split_exchange_wait(send_sems, recv_sems, src_thru, land_thru, after):
    width = land_thru.shape[2]

    def body(src_ref, land_ref, send_sems, recv_sems, after_ref, src_dead, got_ref):
        _, me = _peer(0)
        for k in range(1, N_DEV):
            peer, pidx = _peer(k)
            copy = pltpu.make_async_remote_copy(
                src_ref=src_ref.at[:, pl.ds(pl.multiple_of(pidx * width, width), width)], dst_ref=land_ref.at[pidx],
                send_sem=send_sems.at[k - 1], recv_sem=recv_sems.at[k - 1], device_id=peer, device_id_type=MESH)
            copy.wait_send()
            copy.wait_recv()

    hbm = pl.BlockSpec(memory_space=pltpu.HBM)
    sem = pl.BlockSpec(memory_space=pltpu.SEMAPHORE)
    return pl.pallas_call(
        body, name="w_in_exchange_wait",
        out_shape=(pltpu.HBM(src_thru.shape, src_thru.dtype), pltpu.HBM(land_thru.shape, land_thru.dtype)),
        in_specs=(hbm, hbm, sem, sem, pl.BlockSpec(memory_space=pl.ANY)), out_specs=(hbm, hbm),
        input_output_aliases={0: 0, 1: 1},
        compiler_params=pltpu.CompilerParams(has_side_effects=pltpu.SideEffectType.DATAFLOW_SIDE_EFFECTING),
    )(src_thru, land_thru, send_sems, recv_sems, after)


def _adamw(w, m, v, gparts, name):
    r, c = w.shape
    tr = r if r * c <= 256 * 1024 else 128 * 1024 // c

    def body(w_ref, m_ref, v_ref, g_ref, go_ref, d_ref, mo_ref, vo_ref):
        g = g_ref[0].astype(F32)
        for i in range(1, N_DEV):
            g = g + g_ref[i].astype(F32)
        go_ref[...] = g
        d_ref[...], mo_ref[...], vo_ref[...] = _adamw_step(w_ref[...], m_ref[...], v_ref[...], g)

    blk = pl.BlockSpec((tr, c), lambda i: (i, 0))
    return pl.pallas_call(
        body, name=name, grid=(r // tr,),
        in_specs=[blk, blk, blk, pl.BlockSpec((N_DEV, tr, c), lambda i: (0, i, 0))],
        out_specs=[blk] * 4, out_shape=[_sds((r, c))] * 4,
        compiler_params=_cp(("parallel",)),
    )(w, m, v, gparts)


def _adamw_step(w, m, v, g):
    nm = ADAM_B1 * m + (1.0 - ADAM_B1) * g
    nv = ADAM_B2 * v + (1.0 - ADAM_B2) * (g * g)
    m_hat = nm / (1.0 - ADAM_B1 ** ADAM_STEP)
    v_hat = nv / (1.0 - ADAM_B2 ** ADAM_STEP)
    return -ADAM_LR * (m_hat / (jnp.sqrt(v_hat) + ADAM_EPS) + ADAM_WD * w), nm, nv


def _sum_slots(parts):
    def body(p_ref, o_ref):
        g = p_ref[0]
        for i in range(1, N_DEV):
            g = g + p_ref[i]
        o_ref[...] = g

    return pl.pallas_call(body, name="sum_small_grads", out_shape=_sds(parts.shape[1:]))(parts)


def _adamw_native(ws, ms, vs, gs):
    n = len(ws)

    def body(*refs):
        for i in range(n):
            w_ref, m_ref, v_ref, g_ref = refs[i], refs[n + i], refs[2 * n + i], refs[3 * n + i]
            d, nm, nv = _adamw_step(w_ref[...], m_ref[...], v_ref[...], g_ref[...])
            refs[4 * n + i][...] = d
            refs[5 * n + i][...] = nm
            refs[6 * n + i][...] = nv

    outs = pl.pallas_call(body, name="adamw_small", out_shape=[_sds(w.shape) for w in ws] * 3,
                          compiler_params=pltpu.CompilerParams(vmem_limit_bytes=VMEM_LIMIT))(*ws, *ms, *vs, *gs)
    return outs[:n], outs[n:2 * n], outs[2 * n:]


def _block_diag(a, states_first):
    a4 = a.reshape(N_CB, 8, SSM_GROUP, SSM_STATE)
    eye = jnp.eye(8, dtype=a.dtype)
    if states_first:
        return jnp.einsum("bgcp,gh->bgphc", a4, eye).reshape(N_CB, CB_STATES, LANES)
    return jnp.einsum("bgcp,gh->bgchp", a4, eye).reshape(N_CB, LANES, CB_STATES)


def _block_diag_of(full, states_first):
    if states_first:
        picked = jnp.einsum("bgphc,gh->bgcp", full.reshape(N_CB, 8, SSM_STATE, 8, SSM_GROUP), jnp.eye(8, dtype=full.dtype))
    else:
        picked = jnp.einsum("bgchp,gh->bgcp", full.reshape(N_CB, 8, SSM_GROUP, 8, SSM_STATE), jnp.eye(8, dtype=full.dtype))
    return picked.reshape(SSM_GROUPS, SSM_GROUP, SSM_STATE)


SMALL_EARLY = ("ssm_a_re", "ssm_a_im", "ssm_log_dt", "ssm_b_re", "ssm_b_im", "ssm_c_re", "ssm_c_im", "ssm_d", "glu_b",
               "attn_out_norm_g", "ssm_out_norm_g", "norm2_g")
SMALL_MID = ("q_norm_g", "k_norm_g")
SMALL_LATE = ("norm1_g",)
SMALL = SMALL_EARLY + SMALL_MID + SMALL_LATE


def _pack_small(arrs):
    parts = []
    for a in arrs:
        flat = a.reshape(-1)
        rows = -(-flat.shape[0] // (8 * LANES)) * 8
        parts.append(jnp.pad(flat, (0, rows * LANES - flat.shape[0])).reshape(rows, LANES))
    return jnp.concatenate(parts, axis=0)


def _unpack_small(packed, shapes):
    out, r0 = [], 0
    for shp in shapes:
        size = math.prod(shp)
        rows = -(-size // (8 * LANES)) * 8
        out.append(packed[r0:r0 + rows].reshape(-1)[:size].reshape(shp))
        r0 += rows
    return out


def kernel(x, norm1_g, w_in, q_norm_g, k_norm_g, ssm_a_re, ssm_a_im, ssm_log_dt, ssm_b_re, ssm_b_im, ssm_c_re, ssm_c_im, ssm_d, glu_w, glu_b, attn_out_norm_g, ssm_out_norm_g, w_out, norm2_g, w_mlp_up, w_mlp_down, loss_target, m_norm1_g, m_w_in, m_q_norm_g, m_k_norm_g, m_ssm_a_re, m_ssm_a_im, m_ssm_log_dt, m_ssm_b_re, m_ssm_b_im, m_ssm_c_re, m_ssm_c_im, m_ssm_d, m_glu_w, m_glu_b, m_attn_out_norm_g, m_ssm_out_norm_g, m_w_out, m_norm2_g, m_w_mlp_up, m_w_mlp_down, v_norm1_g, v_w_in, v_q_norm_g, v_k_norm_g, v_ssm_a_re, v_ssm_a_im, v_ssm_log_dt, v_ssm_b_re, v_ssm_b_im, v_ssm_c_re, v_ssm_c_im, v_ssm_d, v_glu_w, v_glu_b, v_attn_out_norm_g, v_ssm_out_norm_g, v_w_out, v_norm2_g, v_w_mlp_up, v_w_mlp_down):
    weights = dict(norm1_g=norm1_g, w_in=w_in, q_norm_g=q_norm_g, k_norm_g=k_norm_g, ssm_a_re=ssm_a_re,
                   ssm_a_im=ssm_a_im, ssm_log_dt=ssm_log_dt, ssm_b_re=ssm_b_re, ssm_b_im=ssm_b_im,
                   ssm_c_re=ssm_c_re, ssm_c_im=ssm_c_im, ssm_d=ssm_d, glu_w=glu_w, glu_b=glu_b,
                   attn_out_norm_g=attn_out_norm_g, ssm_out_norm_g=ssm_out_norm_g, w_out=w_out, norm2_g=norm2_g,
                   w_mlp_up=w_mlp_up, w_mlp_down=w_mlp_down)
    mom_m = dict(norm1_g=m_norm1_g, w_in=m_w_in, q_norm_g=m_q_norm_g, k_norm_g=m_k_norm_g, ssm_a_re=m_ssm_a_re,
                 ssm_a_im=m_ssm_a_im, ssm_log_dt=m_ssm_log_dt, ssm_b_re=m_ssm_b_re, ssm_b_im=m_ssm_b_im,
                 ssm_c_re=m_ssm_c_re, ssm_c_im=m_ssm_c_im, ssm_d=m_ssm_d, glu_w=m_glu_w, glu_b=m_glu_b,
                 attn_out_norm_g=m_attn_out_norm_g, ssm_out_norm_g=m_ssm_out_norm_g, w_out=m_w_out,
                 norm2_g=m_norm2_g, w_mlp_up=m_w_mlp_up, w_mlp_down=m_w_mlp_down)
    mom_v = dict(norm1_g=v_norm1_g, w_in=v_w_in, q_norm_g=v_q_norm_g, k_norm_g=v_k_norm_g, ssm_a_re=v_ssm_a_re,
                 ssm_a_im=v_ssm_a_im, ssm_log_dt=v_ssm_log_dt, ssm_b_re=v_ssm_b_re, ssm_b_im=v_ssm_b_im,
                 ssm_c_re=v_ssm_c_re, ssm_c_im=v_ssm_c_im, ssm_d=v_ssm_d, glu_w=v_glu_w, glu_b=v_glu_b,
                 attn_out_norm_g=v_attn_out_norm_g, ssm_out_norm_g=v_ssm_out_norm_g, w_out=v_w_out,
                 norm2_g=v_norm2_g, w_mlp_up=v_w_mlp_up, w_mlp_down=v_w_mlp_down)
    order = list(weights)

    xs, tgt = x[0], loss_target[0]
    s = xs.shape[0]
    assert s % ATTN_CHUNK == 0 and (s // N_SEG) % SSM_LK == 0
    seg_len = s // N_SEG
    n_sq = seg_len.bit_length() - 1
    assert 1 << n_sq == seg_len

    w_in_t = _gather_rows(w_in[0].T.astype(BF16))
    w_in_full = w_in_t.T
    later = _Exchange([glu_w[0].astype(BF16), w_out[0].astype(BF16), w_mlp_up[0].T.astype(BF16),
                       w_mlp_down[0].astype(BF16)], [True] * 4)

    gq = jnp.tile(q_norm_g[0], ATTN_W // HEAD_DIM)[None]
    gk = jnp.tile(k_norm_g[0], ATTN_W // HEAD_DIM)[None]
    lane = jnp.arange(MXU_W) // HEAD_DIM
    gmat = jnp.where(lane[:, None] == lane[None, :], 1.0 / HEAD_DIM, 0.0).astype(BF16)
    a_re3 = ssm_a_re[0][:, None, :]
    a_im3 = ssm_a_im[0][:, None, :]
    ldt3 = ssm_log_dt[0][:, None, None]
    b_re_t = jnp.swapaxes(ssm_b_re[0], 1, 2)
    b_im_t = jnp.swapaxes(ssm_b_im[0], 1, 2)
    c_re, c_im = ssm_c_re[0], ssm_c_im[0]
    dskip = ssm_d[0].reshape(1, SSM_W)

    qk, qn, kn, v, u, xn = _fwd_proj(xs, norm1_g, w_in_full, gq, gk, gmat)
    attn, lse, glu_g, w_out_g, w_up_g, w_down_g = _attn_fwd(qn, kn, v, later)
    glu_full = glu_g.reshape(SSM_W, SSM_W)
    w_out_full = w_out_g.reshape(D_MODEL, D_MODEL)
    w_up_t = w_up_g.reshape(D_FF, D_MODEL)
    w_down_full = w_down_g.reshape(D_FF, D_MODEL)
    w_up_full, w_out_t, w_down_t, glu_t = w_up_t.T, w_out_full.T, w_down_full.T, glu_full.T

    abr3, abi3, bbr, bbi = _ssm_discretize(a_re3, a_im3, ldt3, b_re_t, b_im_t)
    abr, abi = abr3.reshape(1, N_STATE), abi3.reshape(1, N_STATE)
    pw_r, pw_i = _ssm_power(abr, abi, n_sq)
    bb_r, bb_i = _block_diag(bbr, False).astype(BF16), _block_diag(bbi, False).astype(BF16)
    bbt_r, bbt_i = _block_diag(bbr, True).astype(BF16), _block_diag(bbi, True).astype(BF16)
    cc_r, cc_i = _block_diag(c_re, True).astype(BF16), _block_diag(c_im, True).astype(BF16)
    cct_r, cct_i = _block_diag(c_re, False).astype(BF16), _block_diag(c_im, False).astype(BF16)
    seg_major = lambda a: jnp.swapaxes(a.reshape(N_SEG, seg_len, SSM_W), 0, 1)
    seg_minor = lambda a: jnp.swapaxes(a, 0, 1).reshape(s, SSM_W)
    u3 = seg_major(u)
    zero_fin = jnp.zeros((N_SEG, N_STATE), F32)
    ssm_args = (abr, abi, pw_r, pw_i)
    xf_r, xf_i = _ssm_fwd(u3, *ssm_args, zero_fin, zero_fin, bb_r, bb_i, cc_r, cc_i, dskip, True)
    y3, xst_r, xst_i = _ssm_fwd(u3, *ssm_args, xf_r, xf_i, bb_r, bb_i, cc_r, cc_i, dskip, False)
    y = seg_minor(y3)

    mix, x2, h = _fwd_mix(attn, y, xs, glu_full, glu_b, attn_out_norm_g, ssm_out_norm_g, w_out_full, norm2_g)
    r_act, hdn = _mlp_up(h, w_up_full)
    dy, dyb, sse = _mlp_down_loss(hdn, w_down_full, x2, tgt)

    dup = _mlp_down_bwd(dyb, w_down_t, r_act)
    g_w_down = _mm_tn(hdn, dyb, "grad_w_down")
    dx2, dx2b, g_norm2 = _mlp_up_bwd(dup, w_up_t, x2, norm2_g, dy)
    g_w_up = _mm_tn(h, dup, "grad_w_up")
    dattn, dys, z_b, dpre_b, g_ga, g_gs, g_glu_b = _mix_bwd(dx2b, w_out_t, attn, y, glu_full, glu_b, glu_t,
                                                             attn_out_norm_g, ssm_out_norm_g)
    g_w_out = _mm_tn(mix, dx2b, "grad_w_out")
    g_glu_w = _mm_tn(z_b, dpre_b, "grad_glu_w")
    dy3 = seg_major(dys)
    bwd_args = (u3, dy3, xst_r, xst_i, abr, abi, pw_r, pw_i)
    lf_r, lf_i = _ssm_bwd(*bwd_args, zero_fin, zero_fin, bb_r, bb_i, bbt_r, bbt_i, cct_r, cct_i, dskip, True)
    du3, dab_r, dab_i, dbb_r, dbb_i, dcc_r, dcc_i, g_d = _ssm_bwd(*bwd_args, lf_r, lf_i, bb_r, bb_i, bbt_r, bbt_i,
                                                                 cct_r, cct_i, dskip, False)
    g_a_re3, g_a_im3, g_ldt3, g_b_re_t, g_b_im_t = _ssm_discretize_bwd(
        a_re3, a_im3, ldt3, b_re_t, b_im_t, dab_r.reshape(a_re3.shape), dab_i.reshape(a_re3.shape),
        _block_diag_of(dbb_r, True), _block_diag_of(dbb_i, True))
    g_c_re, g_c_im = _block_diag_of(dcc_r, False), _block_diag_of(dcc_i, False)
    small_grads = dict(
        ssm_a_re=g_a_re3.reshape(ssm_a_re.shape), ssm_a_im=g_a_im3.reshape(ssm_a_im.shape),
        ssm_log_dt=g_ldt3.reshape(ssm_log_dt.shape), ssm_b_re=jnp.swapaxes(g_b_re_t, 1, 2)[None],
        ssm_b_im=jnp.swapaxes(g_b_im_t, 1, 2)[None], ssm_c_re=g_c_re[None], ssm_c_im=g_c_im[None],
        ssm_d=g_d.reshape(ssm_d.shape), glu_b=g_glu_b, attn_out_norm_g=g_ga, ssm_out_norm_g=g_gs, norm2_g=g_norm2)

    early = _Exchange([g_glu_w, g_w_out, g_w_up, g_w_down, _pack_small([small_grads[n] for n in SMALL_EARLY] + [sse])],
                      [False, False, "col", False, True])
    dqn, dkn, dv, p_glu, p_w_out, p_w_up, p_w_down, p_early = _attn_bwd(qn, kn, v, attn, lse, dattn, early)

    dproj, g_gq, g_gk = _qk_bwd(dqn, dkn, qk, dv, seg_minor(du3), gq, gk, gmat)
    g_w_in = _mm_tn(xn, dproj, "grad_w_in")
    small_grads["q_norm_g"] = g_gq.reshape(ATTN_W // HEAD_DIM, HEAD_DIM).sum(0)[None]
    small_grads["k_norm_g"] = g_gk.reshape(ATTN_W // HEAD_DIM, HEAD_DIM).sum(0)[None]
    send_sems, recv_sems, g_thru, land, token = _split_exchange_start(g_w_in)
    grad_x, g_norm1 = _in_bwd(dproj, w_in_t, xs, norm1_g + token[:1, :1], dx2)
    g_w_in, landed = _split_exchange_wait(send_sems, recv_sems, g_thru, land, g_norm1)
    me = 4 * lax.axis_index("x") + 2 * lax.axis_index("y") + lax.axis_index("c")
    shard_w = g_w_in.shape[1] // N_DEV
    own = lax.dynamic_slice(g_w_in, (0, me * shard_w), (g_w_in.shape[0], shard_w))
    p_w_in = lax.dynamic_update_slice(landed, own[None], (me, 0, 0))
    p_mid, p_late = _exchange_now([_pack_small([small_grads[n] for n in SMALL_MID]), _pack_small([g_norm1])],
                                  [True, True], "exchange_tail")

    res = {}
    for name, gp in (("w_in", p_w_in), ("glu_w", p_glu), ("w_out", p_w_out), ("w_mlp_up", p_w_up), ("w_mlp_down", p_w_down)):
        outs = _adamw(weights[name][0], mom_m[name][0], mom_v[name][0], gp, "adamw_" + name)
        res[name] = [o[None] for o in outs]
    shapes = [weights[n].shape for n in SMALL_EARLY] + [sse.shape] + [weights[n].shape for n in SMALL_MID + SMALL_LATE]
    g_small = _unpack_small(_sum_slots(jnp.concatenate([p_early, p_mid, p_late], axis=1)), shapes)
    loss = 0.5 * g_small.pop(len(SMALL_EARLY))[0, 0] / D_MODEL
    d_small, m_small, v_small = _adamw_native([weights[n] for n in SMALL], [mom_m[n] for n in SMALL],
                                              [mom_v[n] for n in SMALL], g_small)
    for i, n in enumerate(SMALL):
        res[n] = [g_small[i], d_small[i], m_small[i], v_small[i]]

    return (loss, grad_x[None], *[res[n][0] for n in order], *[res[n][1] for n in order],
            *[res[n][2] for n in order], *[res[n][3] for n in order])
```
